```python
import math
import jax
import jax.numpy as jnp
from jax import lax
import numpy as np

D_MODEL = 1024
BATCH = 16
SEQ = 4096
DEPTH = 1

GRID_W = 64
CTX_LEN = 256
NORM_EPS = 1e-6

POOL_WINDOWS = (2, 4, 8, 16)
N_POOL_GROUPS = 4
POOL_WIDTH = D_MODEL
POOL_GROUP = POOL_WIDTH // N_POOL_GROUPS

SSD_EXPAND = 2
D_INNER = SSD_EXPAND * D_MODEL
HEAD_DIM = 64
N_HEADS = D_INNER // HEAD_DIM
D_STATE = 128
N_BC_GROUPS = 4
CONV_K = 4
CONV_LEFT = CONV_K // 2
CHUNK = 128
N_DIR = 2
SSD_NORM_GROUPS = N_BC_GROUPS
CONV_DIM = D_INNER + 2 * N_BC_GROUPS * D_STATE

N_BRANCH = 2
OFF_POOL_V = 0
OFF_POOL_Z = OFF_POOL_V + POOL_WIDTH
OFF_SSD_Z = OFF_POOL_Z + POOL_WIDTH
OFF_GATE = OFF_SSD_Z + D_INNER
OFF_XBC = OFF_GATE + N_BRANCH * D_MODEL
OFF_DT = OFF_XBC + CONV_DIM
IN_COLS = OFF_DT + N_DIR * N_HEADS

kernel_name = 'hybrid_pool_ssd_diffusion_block'


def rmsnorm(x, w):
    xf = x.astype(jnp.float32)
    y = xf * lax.rsqrt(jnp.mean(xf * xf, axis=-1, keepdims=True) + NORM_EPS)
    return (y * w.astype(jnp.float32)).astype(x.dtype)


def adaln(cond, w_ada, b_ada):
    mod = jax.nn.silu(cond) @ w_ada + b_ada
    return jnp.split(mod, 3, axis=-1)


def centred_dwconv(u, w, b):
    l = u.shape[1]
    up = jnp.pad(u, ((0, 0), (CONV_LEFT, CONV_K - 1 - CONV_LEFT), (0, 0)))
    out = up[:, 0:l] * w[0]
    for k in range(1, CONV_K):
        out = out + up[:, k:k + l] * w[k]
    return out + b


def box_mean(v, k, axis):
    n = v.shape[axis]
    lo, hi = k // 2, k - 1 - k // 2
    cs = jnp.cumsum(v.astype(jnp.float32), axis=axis)
    pad = [(0, 0)] * v.ndim
    pad[axis] = (1, 0)
    cs = jnp.pad(cs, pad)
    t = jnp.arange(n)
    i_hi = jnp.minimum(t + hi + 1, n)
    i_lo = jnp.maximum(t - lo, 0)
    s = jnp.take(cs, i_hi, axis=axis) - jnp.take(cs, i_lo, axis=axis)
    shape = [1] * v.ndim
    shape[axis] = n
    cnt = (i_hi - i_lo).astype(jnp.float32).reshape(shape)
    return (s / cnt).astype(v.dtype)


def pool_mixer(v, pool_w, pool_scale, rows):
    b, l, _ = v.shape
    diffs = []
    for gi, k in enumerate(POOL_WINDOWS):
        vg = v[..., gi * POOL_GROUP:(gi + 1) * POOL_GROUP]
        if rows is None:
            m = box_mean(vg, k, 1)
        else:
            vg2 = vg.reshape(b, rows, GRID_W, POOL_GROUP)
            m = box_mean(box_mean(vg2, k, 1), k, 2).reshape(b, l, POOL_GROUP)
        diffs.append(m - vg)
    d = jnp.stack(diffs, axis=2)
    y = jnp.einsum('blgi,gio->blgo', d, pool_w).reshape(b, l, POOL_WIDTH)
    return y * pool_scale


def ssd_prep(xbc_raw, dt_raw, conv_w, conv_b, dt_bias, a_log):
    b, l, _ = xbc_raw.shape
    xbc = jax.nn.silu(centred_dwconv(xbc_raw, conv_w, conv_b))
    bc = N_BC_GROUPS * D_STATE
    xs = xbc[..., :D_INNER].reshape(b, l, N_HEADS, HEAD_DIM)
    Bm = xbc[..., D_INNER:D_INNER + bc].reshape(b, l, N_BC_GROUPS, D_STATE)
    Cm = xbc[..., D_INNER + bc:].reshape(b, l, N_BC_GROUPS, D_STATE)
    dt = jax.nn.softplus((dt_raw.reshape(b, l, N_DIR, N_HEADS) + dt_bias).astype(jnp.float32))
    A = -jnp.exp(a_log.astype(jnp.float32))
    return xs, Bm, Cm, dt, A


def ssd_scan(xs, dt, A, Bm, Cm, h0, with_output):
    b, l, h, p = xs.shape
    g, n = Bm.shape[2], Bm.shape[3]
    r = h // g
    nc = l // CHUNK
    a_cs = jnp.cumsum((dt * A).reshape(b, nc, CHUNK, g, r), axis=2)
    xdt = (xs * dt[..., None].astype(xs.dtype)).reshape(b, nc, CHUNK, g, r, p)
    Br = Bm.reshape(b, nc, CHUNK, g, n)
    a_last = a_cs[:, :, -1]
    to_end = jnp.exp(a_last[:, :, None] - a_cs).astype(xs.dtype)
    states = jnp.einsum('bcjgn,bcjgrp->bcgrpn', Br, xdt * to_end[..., None])
    chunk_decay = jnp.exp(a_last).astype(xs.dtype)
    if h0 is None:
        h0 = jnp.zeros((b, h, p, n), xs.dtype)

    def step(carry, inp):
        s, d = inp
        nxt = d[..., None, None] * carry + s
        return nxt, (carry if with_output else None)

    h_final, h_start = lax.scan(step, h0.reshape(b, g, r, p, n),
                                (jnp.moveaxis(states, 1, 0), jnp.moveaxis(chunk_decay, 1, 0)))
    h_final = h_final.reshape(b, h, p, n)
    if not with_output:
        return None, h_final
    h_start = jnp.moveaxis(h_start, 0, 1)
    Cr = Cm.reshape(b, nc, CHUNK, g, n)
    seg = a_cs[:, :, :, None] - a_cs[:, :, None]
    lower = jnp.tril(jnp.ones((CHUNK, CHUNK), dtype=bool))[:, :, None, None]
    L = jnp.exp(jnp.where(lower, seg, -jnp.inf)).astype(xs.dtype)
    cb = jnp.einsum('bcign,bcjgn->bcijg', Cr, Br)
    y_diag = jnp.einsum('bcijgr,bcjgrp->bcigrp', cb[..., None] * L, xdt)
    y_off = jnp.einsum('bcign,bcgrpn->bcigrp', Cr, h_start) * jnp.exp(a_cs).astype(xs.dtype)[..., None]
    return (y_diag + y_off).reshape(b, l, h, p), h_final


def bidir_ssd(xs, dt, A, Bm, Cm, h0_f, h0_b, with_output):
    fl = lambda t: jnp.flip(t, axis=1)
    y_f, st_f = ssd_scan(xs, dt[:, :, 0], A[0], Bm, Cm, h0_f, with_output)
    y_b, st_b = ssd_scan(fl(xs), fl(dt[:, :, 1]), A[1], fl(Bm), fl(Cm), h0_b, with_output)
    y = (y_f + fl(y_b)) if with_output else None
    return y, st_f, st_b


def gated_group_rmsnorm(y, z, w):
    b, l, d = y.shape
    u = (y * jax.nn.silu(z)).astype(jnp.float32).reshape(b, l, SSD_NORM_GROUPS, d // SSD_NORM_GROUPS)
    u = u * lax.rsqrt(jnp.mean(u * u, axis=-1, keepdims=True) + NORM_EPS)
    return (u.reshape(b, l, d) * w.astype(jnp.float32)).astype(y.dtype)


def mixer(h, w_in, b_merge, pool_w, pool_scale, conv_w, conv_b, dt_bias, a_log, d_skip, ssd_norm,
          w_proj_pool, w_proj_ssd, w_out, h0_f, h0_b, rows):
    b, l, _ = h.shape
    proj = h @ w_in
    v = proj[..., OFF_POOL_V:OFF_POOL_Z]
    z_pool = proj[..., OFF_POOL_Z:OFF_SSD_Z]
    z_ssd = proj[..., OFF_SSD_Z:OFF_GATE]
    gates = jax.nn.sigmoid(proj[..., OFF_GATE:OFF_XBC] + b_merge)
    xbc_raw = proj[..., OFF_XBC:OFF_DT]
    dt_raw = proj[..., OFF_DT:]
    y_pool = pool_mixer(v, pool_w, pool_scale, rows) * jax.nn.silu(z_pool)
    xs, Bm, Cm, dt, A = ssd_prep(xbc_raw, dt_raw, conv_w, conv_b, dt_bias, a_log)
    y_ssd, st_f, st_b = bidir_ssd(xs, dt, A, Bm, Cm, h0_f, h0_b, True)
    y_ssd = (y_ssd + d_skip[:, None] * xs).reshape(b, l, D_INNER)
    y_ssd = gated_group_rmsnorm(y_ssd, z_ssd, ssd_norm)
    merged = gates[..., :D_MODEL] * (y_pool @ w_proj_pool) + gates[..., D_MODEL:] * (y_ssd @ w_proj_ssd)
    return merged @ w_out, st_f, st_b


def context_states(hc, w_in, conv_w, conv_b, dt_bias, a_log):
    proj = hc @ w_in[:, OFF_XBC:]
    xs, Bm, Cm, dt, A = ssd_prep(proj[..., :CONV_DIM], proj[..., CONV_DIM:], conv_w, conv_b, dt_bias, a_log)
    _, st_f, st_b = bidir_ssd(xs, dt, A, Bm, Cm, None, None, False)
    return st_f, st_b


def _normal(key, shape, scale):
    return jax.random.normal(key, shape, jnp.float32) * scale


def _fwd_setup_inputs(seed: int = 0) -> dict:
    key = jax.random.key(seed)
    ks = jax.random.split(key, 24)
    D = D_MODEL
    dt0 = jnp.exp(jax.random.uniform(ks[12], (DEPTH, N_DIR, N_HEADS), jnp.float32,
                                     minval=math.log(1e-3), maxval=math.log(1e-1)))
    return {
        'x': _normal(ks[0], (BATCH, SEQ, D), 1.0),
        'c': _normal(ks[1], (BATCH, D), 1.0),
        'ctx': _normal(ks[2], (BATCH, CTX_LEN, D), 1.0),
        'c_ctx': _normal(ks[3], (D,), 1.0),
        'w_ada': _normal(ks[4], (DEPTH, D, 3 * D), 0.5 * D ** -0.5),
        'b_ada': _normal(ks[5], (DEPTH, 3 * D), 0.02),
        'norm_pre': 1.0 + _normal(ks[6], (DEPTH, D), 0.05),
        'norm_post': 1.0 + _normal(ks[7], (DEPTH, D), 0.05),
        'w_in': _normal(ks[8], (DEPTH, D, IN_COLS), D ** -0.5),
        'b_merge': _normal(ks[9], (DEPTH, N_BRANCH * D), 0.02),
        'pool_w': _normal(ks[10], (DEPTH, N_POOL_GROUPS, POOL_GROUP, POOL_GROUP), POOL_GROUP ** -0.5),
        'pool_scale': 1.0 + _normal(ks[11], (DEPTH, POOL_WIDTH), 0.05),
        'conv_w': _normal(ks[13], (DEPTH, CONV_K, CONV_DIM), CONV_K ** -0.5),
        'conv_b': _normal(ks[14], (DEPTH, CONV_DIM), 0.02),
        'dt_bias': dt0 + jnp.log(-jnp.expm1(-dt0)),
        'a_log': jnp.log(jax.random.uniform(ks[15], (DEPTH, N_DIR, N_HEADS), jnp.float32, minval=1.0, maxval=16.0)),
        'd_skip': 1.0 + _normal(ks[16], (DEPTH, N_HEADS), 0.05),
        'ssd_norm': 1.0 + _normal(ks[17], (DEPTH, D_INNER), 0.05),
        'w_proj_pool': _normal(ks[18], (DEPTH, POOL_WIDTH, D), POOL_WIDTH ** -0.5),
        'w_proj_ssd': _normal(ks[19], (DEPTH, D_INNER, D), D_INNER ** -0.5),
        'w_out': _normal(ks[20], (DEPTH, D, D), D ** -0.5),
    }


def _fwd_reference(x, c, ctx, c_ctx, w_ada, b_ada, norm_pre, norm_post, w_in, b_merge, pool_w, pool_scale,
              conv_w, conv_b, dt_bias, a_log, d_skip, ssd_norm, w_proj_pool, w_proj_ssd, w_out):
    rows = x.shape[1] // GRID_W
    for layer in range(DEPTH):
        shift, scale, gate = adaln(c[:, None, :], w_ada[layer], b_ada[layer])
        shift_c, scale_c, gate_c = adaln(c_ctx, w_ada[layer], b_ada[layer])
        hc = rmsnorm(ctx, norm_pre[layer]) * (1.0 + scale_c) + shift_c
        if layer + 1 < DEPTH:
            out_c, st_f, st_b = mixer(hc, w_in[layer], b_merge[layer], pool_w[layer], pool_scale[layer],
                                      conv_w[layer], conv_b[layer], dt_bias[layer], a_log[layer],
                                      d_skip[layer], ssd_norm[layer], w_proj_pool[layer],
                                      w_proj_ssd[layer], w_out[layer], None, None, None)
            ctx_next = ctx + gate_c * rmsnorm(out_c, norm_post[layer])
        else:
            st_f, st_b = context_states(hc, w_in[layer], conv_w[layer], conv_b[layer],
                                        dt_bias[layer], a_log[layer])
            ctx_next = ctx
        hx = rmsnorm(x, norm_pre[layer]) * (1.0 + scale) + shift
        out_x, _, _ = mixer(hx, w_in[layer], b_merge[layer], pool_w[layer], pool_scale[layer],
                            conv_w[layer], conv_b[layer], dt_bias[layer], a_log[layer],
                            d_skip[layer], ssd_norm[layer], w_proj_pool[layer],
                            w_proj_ssd[layer], w_out[layer], st_f, st_b, rows)
        x = x + gate * rmsnorm(out_x, norm_post[layer])
        ctx = ctx_next
    return x


import jax as _jax
import jax.numpy as _jnp

TWIN_FORMAT = 'train_step'
FWD_PARAMS = ['x', 'c', 'ctx', 'c_ctx', 'w_ada', 'b_ada', 'norm_pre', 'norm_post', 'w_in', 'b_merge', 'pool_w', 'pool_scale', 'conv_w', 'conv_b', 'dt_bias', 'a_log', 'd_skip', 'ssd_norm', 'w_proj_pool', 'w_proj_ssd', 'w_out']
TWIN_WEIGHTS = ['c_ctx', 'w_ada', 'b_ada', 'norm_pre', 'norm_post', 'w_in', 'b_merge', 'pool_w', 'pool_scale', 'conv_w', 'conv_b', 'dt_bias', 'a_log', 'd_skip', 'ssd_norm', 'w_proj_pool', 'w_proj_ssd', 'w_out']
TWIN_DIFF_INPUT = 'x'
TWIN_INPUTS = ['x', 'c', 'ctx', 'c_ctx', 'w_ada', 'b_ada', 'norm_pre', 'norm_post', 'w_in', 'b_merge', 'pool_w', 'pool_scale', 'conv_w', 'conv_b', 'dt_bias', 'a_log', 'd_skip', 'ssd_norm', 'w_proj_pool', 'w_proj_ssd', 'w_out', 'loss_target', 'm_c_ctx', 'm_w_ada', 'm_b_ada', 'm_norm_pre', 'm_norm_post', 'm_w_in', 'm_b_merge', 'm_pool_w', 'm_pool_scale', 'm_conv_w', 'm_conv_b', 'm_dt_bias', 'm_a_log', 'm_d_skip', 'm_ssd_norm', 'm_w_proj_pool', 'm_w_proj_ssd', 'm_w_out', 'v_c_ctx', 'v_w_ada', 'v_b_ada', 'v_norm_pre', 'v_norm_post', 'v_w_in', 'v_b_merge', 'v_pool_w', 'v_pool_scale', 'v_conv_w', 'v_conv_b', 'v_dt_bias', 'v_a_log', 'v_d_skip', 'v_ssd_norm', 'v_w_proj_pool', 'v_w_proj_ssd', 'v_w_out']
TWIN_OUTPUTS = ['loss', 'grad_x', 'grad_c_ctx', 'grad_w_ada', 'grad_b_ada', 'grad_norm_pre', 'grad_norm_post', 'grad_w_in', 'grad_b_merge', 'grad_pool_w', 'grad_pool_scale', 'grad_conv_w', 'grad_conv_b', 'grad_dt_bias', 'grad_a_log', 'grad_d_skip', 'grad_ssd_norm', 'grad_w_proj_pool', 'grad_w_proj_ssd', 'grad_w_out', 'delta_c_ctx', 'delta_w_ada', 'delta_b_ada', 'delta_norm_pre', 'delta_norm_post', 'delta_w_in', 'delta_b_merge', 'delta_pool_w', 'delta_pool_scale', 'delta_conv_w', 'delta_conv_b', 'delta_dt_bias', 'delta_a_log', 'delta_d_skip', 'delta_ssd_norm', 'delta_w_proj_pool', 'delta_w_proj_ssd', 'delta_w_out', 'new_m_c_ctx', 'new_m_w_ada', 'new_m_b_ada', 'new_m_norm_pre', 'new_m_norm_post', 'new_m_w_in', 'new_m_b_merge', 'new_m_pool_w', 'new_m_pool_scale', 'new_m_conv_w', 'new_m_conv_b', 'new_m_dt_bias', 'new_m_a_log', 'new_m_d_skip', 'new_m_ssd_norm', 'new_m_w_proj_pool', 'new_m_w_proj_ssd', 'new_m_w_out', 'new_v_c_ctx', 'new_v_w_ada', 'new_v_b_ada', 'new_v_norm_pre', 'new_v_norm_post', 'new_v_w_in', 'new_v_b_merge', 'new_v_pool_w', 'new_v_pool_scale', 'new_v_conv_w', 'new_v_conv_b', 'new_v_dt_bias', 'new_v_a_log', 'new_v_d_skip', 'new_v_ssd_norm', 'new_v_w_proj_pool', 'new_v_w_proj_ssd', 'new_v_w_out']
TWIN_LEAF_KINDS = {'loss': 'loss', 'grad_x': 'grad_x', 'grad_c_ctx': 'grad_w', 'grad_w_ada': 'grad_w', 'grad_b_ada': 'grad_w', 'grad_norm_pre': 'grad_w', 'grad_norm_post': 'grad_w', 'grad_w_in': 'grad_w', 'grad_b_merge': 'grad_w', 'grad_pool_w': 'grad_w', 'grad_pool_scale': 'grad_w', 'grad_conv_w': 'grad_w', 'grad_conv_b': 'grad_w', 'grad_dt_bias': 'grad_w', 'grad_a_log': 'grad_w', 'grad_d_skip': 'grad_w', 'grad_ssd_norm': 'grad_w', 'grad_w_proj_pool': 'grad_w', 'grad_w_proj_ssd': 'grad_w', 'grad_w_out': 'grad_w', 'delta_c_ctx': 'delta_w', 'delta_w_ada': 'delta_w', 'delta_b_ada': 'delta_w', 'delta_norm_pre': 'delta_w', 'delta_norm_post': 'delta_w', 'delta_w_in': 'delta_w', 'delta_b_merge': 'delta_w', 'delta_pool_w': 'delta_w', 'delta_pool_scale': 'delta_w', 'delta_conv_w': 'delta_w', 'delta_conv_b': 'delta_w', 'delta_dt_bias': 'delta_w', 'delta_a_log': 'delta_w', 'delta_d_skip': 'delta_w', 'delta_ssd_norm': 'delta_w', 'delta_w_proj_pool': 'delta_w', 'delta_w_proj_ssd': 'delta_w', 'delta_w_out': 'delta_w', 'new_m_c_ctx': 'new_m', 'new_m_w_ada': 'new_m', 'new_m_b_ada': 'new_m', 'new_m_norm_pre': 'new_m', 'new_m_norm_post': 'new_m', 'new_m_w_in': 'new_m', 'new_m_b_merge': 'new_m', 'new_m_pool_w': 'new_m', 'new_m_pool_scale': 'new_m', 'new_m_conv_w': 'new_m', 'new_m_conv_b': 'new_m', 'new_m_dt_bias': 'new_m', 'new_m_a_log': 'new_m', 'new_m_d_skip': 'new_m', 'new_m_ssd_norm': 'new_m', 'new_m_w_proj_pool': 'new_m', 'new_m_w_proj_ssd': 'new_m', 'new_m_w_out': 'new_m', 'new_v_c_ctx': 'new_v', 'new_v_w_ada': 'new_v', 'new_v_b_ada': 'new_v', 'new_v_norm_pre': 'new_v', 'new_v_norm_post': 'new_v', 'new_v_w_in': 'new_v', 'new_v_b_merge': 'new_v', 'new_v_pool_w': 'new_v', 'new_v_pool_scale': 'new_v', 'new_v_conv_w': 'new_v', 'new_v_conv_b': 'new_v', 'new_v_dt_bias': 'new_v', 'new_v_a_log': 'new_v', 'new_v_d_skip': 'new_v', 'new_v_ssd_norm': 'new_v', 'new_v_w_proj_pool': 'new_v', 'new_v_w_proj_ssd': 'new_v', 'new_v_w_out': 'new_v'}


def _forward(args):
    return _fwd_reference(*[args[k] for k in FWD_PARAMS])


def _output_shape():
    out = _jax.eval_shape(lambda: _forward(_fwd_setup_inputs(0)))
    return out.shape, out.dtype

N_MICROBATCH = 1
ADAM_LR = 0.001
ADAM_B1 = 0.9
ADAM_B2 = 0.999
ADAM_EPS = 1e-08
ADAM_WD = 0.01
ADAM_STEP = 10
PER_EXAMPLE_BATCH_AXIS = {'x': 0, 'c': 0, 'ctx': 0, 'loss_target': 0}
SHARED_INPUTS = []
_WEIGHT_DTYPES = {'c_ctx': _jnp.float32, 'w_ada': _jnp.float32, 'b_ada': _jnp.float32, 'norm_pre': _jnp.float32, 'norm_post': _jnp.float32, 'w_in': _jnp.float32, 'b_merge': _jnp.float32, 'pool_w': _jnp.float32, 'pool_scale': _jnp.float32, 'conv_w': _jnp.float32, 'conv_b': _jnp.float32, 'dt_bias': _jnp.float32, 'a_log': _jnp.float32, 'd_skip': _jnp.float32, 'ssd_norm': _jnp.float32, 'w_proj_pool': _jnp.float32, 'w_proj_ssd': _jnp.float32, 'w_out': _jnp.float32}
MOMENT_SCALE = {'c_ctx': 8.959837e-03, 'w_ada': 3.418074e+00, 'b_ada': 6.357325e+00, 'norm_pre': 1.939699e-01, 'norm_post': 7.383513e+00, 'w_in': 7.163358e-02, 'b_merge': 5.647886e-02, 'pool_w': 7.183853e-02, 'pool_scale': 8.281599e-02, 'conv_w': 8.790635e-02, 'conv_b': 1.834491e-01, 'dt_bias': 2.610557e-01, 'a_log': 1.777618e-01, 'd_skip': 2.152001e-01, 'ssd_norm': 1.282908e-01, 'w_proj_pool': 7.443910e-02, 'w_proj_ssd': 1.878646e-01, 'w_out': 2.087652e-01}


def _to_microbatches(a, axis):
    t = _jnp.moveaxis(a, axis, 0)
    t = t.reshape((N_MICROBATCH, t.shape[0] // N_MICROBATCH) + t.shape[1:])
    return _jnp.moveaxis(t, 1, axis + 1)


def setup_inputs(seed: int = 0) -> dict:
    inp = _fwd_setup_inputs(seed)
    key = _jax.random.fold_in(_jax.random.key(seed), 7919)
    shape, _ = _output_shape()
    out = dict(inp)
    out["loss_target"] = _jax.random.normal(_jax.random.fold_in(key, 0), shape, _jnp.float32)
    for i, name in enumerate(TWIN_WEIGHTS):
        w = inp[name].astype(_jnp.float32)
        if MOMENT_SCALE is None:
            s = _jnp.sqrt(_jnp.mean(_jnp.square(w)) + 1e-30)
        else:
            s = MOMENT_SCALE[name]
        km, kv = _jax.random.split(_jax.random.fold_in(key, i + 1))
        out[name] = w
        out["m_" + name] = s * _jax.random.normal(km, w.shape, _jnp.float32)
        out["v_" + name] = (s * s) * _jax.random.uniform(kv, w.shape, _jnp.float32, 0.5, 1.5)
    if N_MICROBATCH > 1:
        for name, axis in PER_EXAMPLE_BATCH_AXIS.items():
            out[name] = _to_microbatches(out[name], axis)
    return {'x': out['x'], 'c': out['c'], 'ctx': out['ctx'], 'c_ctx': out['c_ctx'], 'w_ada': out['w_ada'], 'b_ada': out['b_ada'], 'norm_pre': out['norm_pre'], 'norm_post': out['norm_post'], 'w_in': out['w_in'], 'b_merge': out['b_merge'], 'pool_w': out['pool_w'], 'pool_scale': out['pool_scale'], 'conv_w': out['conv_w'], 'conv_b': out['conv_b'], 'dt_bias': out['dt_bias'], 'a_log': out['a_log'], 'd_skip': out['d_skip'], 'ssd_norm': out['ssd_norm'], 'w_proj_pool': out['w_proj_pool'], 'w_proj_ssd': out['w_proj_ssd'], 'w_out': out['w_out'], 'loss_target': out['loss_target'], 'm_c_ctx': out['m_c_ctx'], 'm_w_ada': out['m_w_ada'], 'm_b_ada': out['m_b_ada'], 'm_norm_pre': out['m_norm_pre'], 'm_norm_post': out['m_norm_post'], 'm_w_in': out['m_w_in'], 'm_b_merge': out['m_b_merge'], 'm_pool_w': out['m_pool_w'], 'm_pool_scale': out['m_pool_scale'], 'm_conv_w': out['m_conv_w'], 'm_conv_b': out['m_conv_b'], 'm_dt_bias': out['m_dt_bias'], 'm_a_log': out['m_a_log'], 'm_d_skip': out['m_d_skip'], 'm_ssd_norm': out['m_ssd_norm'], 'm_w_proj_pool': out['m_w_proj_pool'], 'm_w_proj_ssd': out['m_w_proj_ssd'], 'm_w_out': out['m_w_out'], 'v_c_ctx': out['v_c_ctx'], 'v_w_ada': out['v_w_ada'], 'v_b_ada': out['v_b_ada'], 'v_norm_pre': out['v_norm_pre'], 'v_norm_post': out['v_norm_post'], 'v_w_in': out['v_w_in'], 'v_b_merge': out['v_b_merge'], 'v_pool_w': out['v_pool_w'], 'v_pool_scale': out['v_pool_scale'], 'v_conv_w': out['v_conv_w'], 'v_conv_b': out['v_conv_b'], 'v_dt_bias': out['v_dt_bias'], 'v_a_log': out['v_a_log'], 'v_d_skip': out['v_d_skip'], 'v_ssd_norm': out['v_ssd_norm'], 'v_w_proj_pool': out['v_w_proj_pool'], 'v_w_proj_ssd': out['v_w_proj_ssd'], 'v_w_out': out['v_w_out']}


def _loss(weights, diff, rest, loss_target):
    with _jax.named_scope("forward"):
        args = {**rest, TWIN_DIFF_INPUT: diff, **{k: w.astype(_WEIGHT_DTYPES[k]) for k, w in weights.items()}}
        y = _forward(args)
    with _jax.named_scope("loss_head"):
        err = _jnp.square(y.astype(_jnp.float32) - loss_target)
        return 0.5 * _jnp.sum(_jnp.mean(err, axis=-1)) if err.ndim else 0.5 * err


def _adamw(w, g, m, v):
    m = ADAM_B1 * m + (1.0 - ADAM_B1) * g
    v = ADAM_B2 * v + (1.0 - ADAM_B2) * _jnp.square(g)
    m_hat = m / (1.0 - ADAM_B1 ** ADAM_STEP)
    v_hat = v / (1.0 - ADAM_B2 ** ADAM_STEP)
    delta = -ADAM_LR * (m_hat / (_jnp.sqrt(v_hat) + ADAM_EPS) + ADAM_WD * w)
    return delta, m, v


def reference(x, c, ctx, c_ctx, w_ada, b_ada, norm_pre, norm_post, w_in, b_merge, pool_w, pool_scale, conv_w, conv_b, dt_bias, a_log, d_skip, ssd_norm, w_proj_pool, w_proj_ssd, w_out, loss_target, m_c_ctx, m_w_ada, m_b_ada, m_norm_pre, m_norm_post, m_w_in, m_b_merge, m_pool_w, m_pool_scale, m_conv_w, m_conv_b, m_dt_bias, m_a_log, m_d_skip, m_ssd_norm, m_w_proj_pool, m_w_proj_ssd, m_w_out, v_c_ctx, v_w_ada, v_b_ada, v_norm_pre, v_norm_post, v_w_in, v_b_merge, v_pool_w, v_pool_scale, v_conv_w, v_conv_b, v_dt_bias, v_a_log, v_d_skip, v_ssd_norm, v_w_proj_pool, v_w_proj_ssd, v_w_out):
    given = dict(x=x, c=c, ctx=ctx, c_ctx=c_ctx, w_ada=w_ada, b_ada=b_ada, norm_pre=norm_pre, norm_post=norm_post, w_in=w_in, b_merge=b_merge, pool_w=pool_w, pool_scale=pool_scale, conv_w=conv_w, conv_b=conv_b, dt_bias=dt_bias, a_log=a_log, d_skip=d_skip, ssd_norm=ssd_norm, w_proj_pool=w_proj_pool, w_proj_ssd=w_proj_ssd, w_out=w_out, loss_target=loss_target, m_c_ctx=m_c_ctx, m_w_ada=m_w_ada, m_b_ada=m_b_ada, m_norm_pre=m_norm_pre, m_norm_post=m_norm_post, m_w_in=m_w_in, m_b_merge=m_b_merge, m_pool_w=m_pool_w, m_pool_scale=m_pool_scale, m_conv_w=m_conv_w, m_conv_b=m_conv_b, m_dt_bias=m_dt_bias, m_a_log=m_a_log, m_d_skip=m_d_skip, m_ssd_norm=m_ssd_norm, m_w_proj_pool=m_w_proj_pool, m_w_proj_ssd=m_w_proj_ssd, m_w_out=m_w_out, v_c_ctx=v_c_ctx, v_w_ada=v_w_ada, v_b_ada=v_b_ada, v_norm_pre=v_norm_pre, v_norm_post=v_norm_post, v_w_in=v_w_in, v_b_merge=v_b_merge, v_pool_w=v_pool_w, v_pool_scale=v_pool_scale, v_conv_w=v_conv_w, v_conv_b=v_conv_b, v_dt_bias=v_dt_bias, v_a_log=v_a_log, v_d_skip=v_d_skip, v_ssd_norm=v_ssd_norm, v_w_proj_pool=v_w_proj_pool, v_w_proj_ssd=v_w_proj_ssd, v_w_out=v_w_out)
    weights = {n: given[n] for n in TWIN_WEIGHTS}
    shared = {n: given[n] for n in SHARED_INPUTS}
    per_example = {n: given[n] for n in ['x', 'c', 'ctx']}
    grad_fn = _jax.value_and_grad(_loss, argnums=(0, 1))

    def one_microbatch(ex, loss_target):
        ex = dict(ex)
        diff = ex.pop(TWIN_DIFF_INPUT)
        return grad_fn(weights, diff, {**shared, **ex}, loss_target)

    if N_MICROBATCH == 1:
        loss, (grad_w, grad_x) = one_microbatch(per_example, given["loss_target"])
    else:
        def body(carry, xs):
            loss_sum, grad_sum = carry
            l_k, (gw_k, gx_k) = one_microbatch(xs[0], xs[1])
            with _jax.named_scope("update"):
                return (loss_sum + l_k, _jax.tree.map(_jnp.add, grad_sum, gw_k)), gx_k

        init = (_jnp.zeros((), _jnp.float32), _jax.tree.map(_jnp.zeros_like, weights))
        (loss, grad_w), grad_x = _jax.lax.scan(body, init, (per_example, given["loss_target"]))
    with _jax.named_scope("update"):
        delta_w, new_m, new_v = {}, {}, {}
        for n in TWIN_WEIGHTS:
            delta_w[n], new_m[n], new_v[n] = _adamw(weights[n], grad_w[n], given["m_" + n], given["v_" + n])
    return (loss, grad_x, *[grad_w[n] for n in TWIN_WEIGHTS], *[delta_w[n] for n in TWIN_WEIGHTS],
            *[new_m[n] for n in TWIN_WEIGHTS], *[new_v[n] for n in TWIN_WEIGHTS])
```

```python
import jax
import jax.numpy as jnp
from jax import lax
from jax.experimental import pallas as pl
from jax.experimental.pallas import tpu as pltpu

F32 = jnp.float32
BF16 = jnp.bfloat16
MESH = pl.DeviceIdType.MESH

D = 1024
GRID_W = 64
NORM_EPS = 1e-6
POOL_WINDOWS = (2, 4, 8, 16)
POOL_GROUP = 256
D_INNER = 2048
HEAD_DIM = 64
N_HEADS = 32
D_STATE = 128
N_BC = 4
HPG = N_HEADS // N_BC
GW = HPG * HEAD_DIM
CONV_DIM = 3072
CHUNK = 128
OFF_XBC = 6144
IN_COLS = 9280
N_CHIPS = 4
N_DEV = 8

ADAM_LR = 0.001
ADAM_B1 = 0.9
ADAM_B2 = 0.999
ADAM_EPS = 1e-08
ADAM_WD = 0.01
ADAM_STEP = 10

V7X_VMEM_BYTES = 64 * 1024 * 1024
VMEM_LIMIT = V7X_VMEM_BYTES * 3 // 4
LANES = 128


def _cp(sem=None):
    return pltpu.CompilerParams(dimension_semantics=sem, vmem_limit_bytes=VMEM_LIMIT)


def _dot(a, b):
    return jnp.dot(a, b, preferred_element_type=F32)


def _dot_nt(a, b):
    return lax.dot_general(a, b, (((1,), (1,)), ((), ())), preferred_element_type=F32)


def _dot_tn(a, b):
    return lax.dot_general(a, b, (((0,), (0,)), ((), ())), preferred_element_type=F32)


def _split3(x):
    hi = x.astype(BF16)
    r1 = x - hi.astype(F32)
    mid = r1.astype(BF16)
    lo = (r1 - mid.astype(F32)).astype(BF16)
    return hi, mid, lo


def _sigmoid(x):
    return jax.nn.sigmoid(x)


def _silu(x):
    return x * _sigmoid(x)


def _dsilu(x):
    s = _sigmoid(x)
    return s * (1.0 + x * (1.0 - s))


def _softplus(x):
    return jnp.maximum(x, 0.0) + jnp.log(1.0 + jnp.exp(-jnp.abs(x)))


def mm_nt(name, a, b, out_dtype, tm=512, tn=512):
    M, K = a.shape
    N = b.shape[0]
    tm, tn = min(tm, M), min(tn, N)
    assert M % tm == 0 and N % tn == 0, (M, N, tm, tn)

    def body(a_ref, b_ref, o_ref):
        o_ref[...] = _dot_nt(a_ref[...], b_ref[...]).astype(o_ref.dtype)

    return pl.pallas_call(
        body, name=name, out_shape=jax.ShapeDtypeStruct((M, N), out_dtype), grid=(M // tm, N // tn),
        in_specs=[pl.BlockSpec((tm, K), lambda i, j: (i, 0)), pl.BlockSpec((tn, K), lambda i, j: (j, 0))],
        out_specs=pl.BlockSpec((tm, tn), lambda i, j: (i, j)),
        compiler_params=_cp(("parallel", "arbitrary")))(a, b)


def mm_tn(name, a, b, init=None, tm=512, tn=1024, tk=512):
    T, M = a.shape
    N = b.shape[1]
    tm, tn, tk = min(tm, M), min(tn, N), min(tk, T)
    assert M % tm == 0 and N % tn == 0 and T % tk == 0, (M, N, T)
    has_init = init is not None

    def body(*refs):
        if has_init:
            a_ref, b_ref, i_ref, o_ref = refs
        else:
            a_ref, b_ref, o_ref = refs
        k = pl.program_id(2)

        @pl.when(k == 0)
        def _():
            o_ref[...] = i_ref[...] if has_init else jnp.zeros(o_ref.shape, F32)

        o_ref[...] += _dot_tn(a_ref[...], b_ref[...])

    in_specs = [pl.BlockSpec((tk, tm), lambda i, j, k: (k, i)), pl.BlockSpec((tk, tn), lambda i, j, k: (k, j))]
    args = [a, b]
    if has_init:
        in_specs.append(pl.BlockSpec((tm, tn), lambda i, j, k: (i, j)))
        args.append(init)
    return pl.pallas_call(
        body, name=name, out_shape=jax.ShapeDtypeStruct((M, N), F32), grid=(M // tm, N // tn, T // tk),
        in_specs=in_specs, out_specs=pl.BlockSpec((tm, tn), lambda i, j, k: (i, j)),
        compiler_params=_cp(("parallel", "parallel", "arbitrary")))(*args)


def mm_nn_multi(name, pairs, out_dtype, tm=512, tk=512):
    M = pairs[0][0].shape[0]
    N = pairs[0][1].shape[1]
    tm = min(tm, M)
    assert M % tm == 0
    plan = []
    step = 0
    for a, b in pairs:
        K = a.shape[1]
        t = min(tk, K)
        assert K % t == 0 and b.shape == (K, N)
        plan.append((t, step, K // t))
        step += K // t
    nsteps = step
    npairs = len(pairs)

    def body(*refs):
        o_ref, acc = refs[2 * npairs], refs[2 * npairs + 1]
        k = pl.program_id(1)

        @pl.when(k == 0)
        def _():
            acc[...] = jnp.zeros(acc.shape, F32)

        for p, (_, first, n) in enumerate(plan):
            @pl.when((k >= first) & (k < first + n))
            def _(p=p):
                acc[...] += _dot(refs[2 * p][...], refs[2 * p + 1][...])

        @pl.when(k == nsteps - 1)
        def _():
            o_ref[...] = acc[...].astype(o_ref.dtype)

    in_specs, args = [], []
    for (a, b), (t, first, n) in zip(pairs, plan):
        in_specs.append(pl.BlockSpec((tm, t), lambda i, k, first=first, n=n: (i, jnp.clip(k - first, 0, n - 1))))
        in_specs.append(pl.BlockSpec((t, N), lambda i, k, first=first, n=n: (jnp.clip(k - first, 0, n - 1), 0)))
        args += [a, b]
    return pl.pallas_call(
        body, name=name, out_shape=jax.ShapeDtypeStruct((M, N), out_dtype), grid=(M // tm, nsteps),
        in_specs=in_specs, out_specs=pl.BlockSpec((tm, N), lambda i, k: (i, 0)),
        scratch_shapes=[pltpu.VMEM((tm, N), F32)],
        compiler_params=_cp(("parallel", "arbitrary")))(*args)


def tok_call(name, body, tiled, perb, glob, out_tiled, out_perb, out_glob, tm=256):
    widths = [t[1] if isinstance(t, tuple) else t.shape[2] for t in tiled]
    tiled = [t[0] if isinstance(t, tuple) else t for t in tiled]
    Bn, L = tiled[0].shape[:2]
    tm = min(tm, L)
    assert L % tm == 0
    n_t, n_p, n_g = len(tiled), len(perb), len(glob)
    o_t, o_p, o_g = len(out_tiled), len(out_perb), len(out_glob)
    n_in = n_t + n_p + n_g

    def kern(*refs):
        ins, outs = refs[:n_in], refs[n_in:]
        b, j = pl.program_id(0), pl.program_id(1)
        vals = [r[0] for r in ins[:n_t + n_p]] + [r[...] for r in ins[n_t + n_p:]]
        res = body(*vals)
        if not isinstance(res, (tuple, list)):
            res = (res,)
        assert len(res) == o_t + o_p + o_g, (name, len(res))
        for r, v in zip(outs[:o_t], res[:o_t]):
            r[0] = v.astype(r.dtype)

        def accum(r, v, first, lead):
            @pl.when(first)
            def _():
                r[...] = jnp.zeros(r.shape, F32)
            if lead:
                r[0] += v
            else:
                r[...] += v

        for r, v in zip(outs[o_t:o_t + o_p], res[o_t:o_t + o_p]):
            accum(r, v, j == 0, True)
        for r, v in zip(outs[o_t + o_p:], res[o_t + o_p:]):
            accum(r, v, (j == 0) & (b == 0), False)

    in_specs = ([pl.BlockSpec((1, tm, w), lambda b, j: (b, j, 0)) for w in widths]
                + [pl.BlockSpec((1, 1, a.shape[2]), lambda b, j: (b, 0, 0)) for a in perb]
                + [pl.BlockSpec(a.shape, lambda b, j: (0, 0)) for a in glob])
    out_shape = ([jax.ShapeDtypeStruct((Bn, L, w), dt) for w, dt in out_tiled]
                 + [jax.ShapeDtypeStruct((Bn, 1, w), F32) for w in out_perb]
                 + [jax.ShapeDtypeStruct(s, F32) for s in out_glob])
    out_specs = ([pl.BlockSpec((1, tm, w), lambda b, j: (b, j, 0)) for w, _ in out_tiled]
                 + [pl.BlockSpec((1, 1, w), lambda b, j: (b, 0, 0)) for w in out_perb]
                 + [pl.BlockSpec(s, lambda b, j: (0, 0)) for s in out_glob])
    return pl.pallas_call(
        kern, name=name, out_shape=out_shape, grid=(Bn, L // tm), in_specs=in_specs, out_specs=out_specs,
        compiler_params=_cp(("arbitrary", "arbitrary")))(*tiled, *perb, *glob)


def slab_call(name, body, slabs, colparams, out_slabs, out_colred, wc=LANES):
    Bn, L = slabs[0][0].shape[:2]
    w_out = out_slabs[0][0]
    assert w_out % wc == 0 and all(off % wc == 0 for _, off in slabs + colparams)
    n_col = w_out // wc
    n_s, n_c = len(slabs), len(colparams)
    o_s = len(out_slabs)

    def kern(*refs):
        ins, outs = refs[:n_s + n_c], refs[n_s + n_c:]
        b = pl.program_id(1)
        vals = [r[0] for r in ins[:n_s]] + [r[...] for r in ins[n_s:]]
        res = body(*vals)
        if not isinstance(res, (tuple, list)):
            res = (res,)
        assert len(res) == o_s + len(out_colred), name
        for r, v in zip(outs[:o_s], res[:o_s]):
            r[0] = v.astype(r.dtype)

        def accum(r, v):
            @pl.when(b == 0)
            def _():
                r[...] = jnp.zeros(r.shape, F32)
            r[...] += v

        for r, v in zip(outs[o_s:], res[o_s:]):
            accum(r, v)

    in_specs = ([pl.BlockSpec((1, L, wc), lambda j, b, o=off // wc: (b, 0, o + j)) for _, off in slabs]
                + [pl.BlockSpec((a.shape[0], wc), lambda j, b, o=off // wc: (0, o + j)) for a, off in colparams])
    out_shape = ([jax.ShapeDtypeStruct((Bn, L, w), dt) for w, dt in out_slabs]
                 + [jax.ShapeDtypeStruct((r, w_out), F32) for r in out_colred])
    out_specs = ([pl.BlockSpec((1, L, wc), lambda j, b: (b, 0, j)) for _ in out_slabs]
                 + [pl.BlockSpec((r, wc), lambda j, b: (0, j)) for r in out_colred])
    return pl.pallas_call(
        kern, name=name, out_shape=out_shape, grid=(n_col, Bn), in_specs=in_specs, out_specs=out_specs,
        compiler_params=_cp(("arbitrary", "arbitrary")))(*[a for a, _ in slabs], *[a for a, _ in colparams])


def _rms_r(x):
    return lax.rsqrt(jnp.mean(x * x, axis=-1, keepdims=True) + NORM_EPS)


def _rms_bwd(dxh, x, r):
    return r * (dxh - x * (r * r) * jnp.mean(dxh * x, axis=-1, keepdims=True))


def _colsum(v):
    return jnp.sum(v, axis=0, keepdims=True)


def _stack_rows(rows):
    n, w = len(rows), rows[0].shape[1]
    sub = lax.broadcasted_iota(jnp.int32, (n, w), 0)
    acc = jnp.zeros((n, w), F32)
    for r, row in enumerate(rows):
        acc = acc + jnp.where(sub == r, jnp.broadcast_to(row, (n, w)), 0.0)
    return acc


def prenorm_fwd(name, x, scale, shift, w_pre):
    def body(x, scale, shift, w):
        n = x * _rms_r(x) * w
        return n * (1.0 + scale) + shift

    return tok_call(name, body, [x], [scale, shift], [w_pre], [(D, BF16)], [], [])[0]


def prenorm_bwd(name, x, dhx, scale, w_pre, g_res=None):
    has_res = g_res is not None

    def body(*v):
        if has_res:
            x, dhx, g, scale, w = v
        else:
            x, dhx, scale, w = v
        r = _rms_r(x)
        xr = x * r
        n = xr * w
        dn = dhx * (1.0 + scale)
        dx = _rms_bwd(dn * w, x, r)
        if has_res:
            dx = dx + g
        return dx, _colsum(dhx * n), _colsum(dhx), _colsum(dn * xr)

    tiled = [x, dhx] + ([g_res] if has_res else [])
    return tok_call(name, body, tiled, [scale], [w_pre], [(D, F32)], [D, D], [(1, D)])


def _shift_rows(x, o, tok, L):
    if o == 0:
        return x
    rolled = pltpu.roll(x, (-o) % L, 0)
    return jnp.where((tok + o >= 0) & (tok + o < L), rolled, 0.0)


def conv_fwd(name, xbc_raw, conv_w, conv_b):
    L = xbc_raw.shape[1]

    def body(x, w, b):
        tok = lax.broadcasted_iota(jnp.int32, x.shape, 0)
        pre = b
        for k in range(4):
            pre = pre + _shift_rows(x, k - 2, tok, L) * w[k:k + 1]
        return _silu(pre)

    return slab_call(name, body, [(xbc_raw, 0)], [(conv_w, 0), (conv_b, 0)], [(CONV_DIM, F32)], [])[0]


def conv_bwd(name, xbc_raw, dparts, conv_w, conv_b, col0, width):
    L = xbc_raw.shape[1]
    n_d = len(dparts)

    def body(*v):
        x, ds, w, b = v[0], v[1:1 + n_d], v[1 + n_d], v[2 + n_d]
        tok = lax.broadcasted_iota(jnp.int32, x.shape, 0)
        taps = [_shift_rows(x, k - 2, tok, L) for k in range(4)]
        pre = b
        for k in range(4):
            pre = pre + taps[k] * w[k:k + 1]
        dy = ds[0]
        for extra in ds[1:]:
            dy = dy + extra
        dpre = dy * _dsilu(pre)
        dx = jnp.zeros_like(x)
        for k in range(4):
            dx = dx + _shift_rows(dpre, 2 - k, tok, L) * w[k:k + 1]
        dw = _stack_rows([_colsum(dpre * taps[k]) for k in range(4)])
        return dx, dw, _colsum(dpre)

    return slab_call(name, body, [(xbc_raw, col0)] + [(d, 0) for d in dparts], [(conv_w, col0), (conv_b, col0)],
                     [(width, BF16)], [4, 1])


def _box_mean(x, k, step, pos, n, L, transpose):
    lo, hi = k // 2, k - 1 - k // 2
    cnt = (jnp.minimum(pos + hi + 1, n) - jnp.maximum(pos - lo, 0)).astype(F32)
    if transpose:
        x = x / cnt
        lo, hi = hi, lo
    acc = x
    for o in range(-lo, hi + 1):
        if o == 0:
            continue
        rolled = pltpu.roll(x, (-o * step) % L, 0)
        acc = acc + jnp.where((pos + o >= 0) & (pos + o < n), rolled, 0.0)
    return acc if transpose else acc / cnt


def pool_diff(name, v, col0, gi, transpose):
    L = v.shape[1]
    rows = L // GRID_W
    k = POOL_WINDOWS[gi]

    def body(x):
        tok = lax.broadcasted_iota(jnp.int32, x.shape, 0)
        col = tok & (GRID_W - 1)
        row = tok >> 6
        if not transpose:
            m = _box_mean(x, k, GRID_W, row, rows, L, False)
            m = _box_mean(m, k, 1, col, GRID_W, L, False)
        else:
            m = _box_mean(x, k, 1, col, GRID_W, L, True)
            m = _box_mean(m, k, GRID_W, row, rows, L, True)
        return m - x

    return slab_call(name, body, [(v, col0)], [], [(POOL_GROUP, BF16)], [])[0]


def pool_mix_fwd(name, dgs, z_pool, pool_w, pool_scale):
    def body(d0, d1, d2, d3, z, w, scale):
        q = jnp.concatenate([_dot(d, w[g * POOL_GROUP:(g + 1) * POOL_GROUP]) for g, d in enumerate((d0, d1, d2, d3))], axis=1)
        return q * scale * _silu(z)

    return tok_call(name, body, list(dgs) + [z_pool], [], [pool_w, pool_scale], [(D, BF16)], [], [])[0]


def pool_mix_bwd(name, dgs, z_pool, dyp, pool_w, pool_scale):
    def body(d0, d1, d2, d3, z, dyp, w, scale):
        ds = (d0, d1, d2, d3)
        q = jnp.concatenate([_dot(d, w[g * POOL_GROUP:(g + 1) * POOL_GROUP]) for g, d in enumerate(ds)], axis=1)
        dypm = dyp * _silu(z)
        dz = dyp * (q * scale) * _dsilu(z)
        dq = (dypm * scale).astype(BF16)
        dds, gws = [], []
        for g, d in enumerate(ds):
            dqg = dq[:, g * POOL_GROUP:(g + 1) * POOL_GROUP]
            dds.append(_dot_nt(dqg, w[g * POOL_GROUP:(g + 1) * POOL_GROUP]))
            gws.append(_dot_tn(d, dqg))
        return (*dds, dz, jnp.concatenate(gws, axis=0), _colsum(dypm * q))

    return tok_call(name, body, list(dgs) + [z_pool, dyp], [], [pool_w, pool_scale],
                    [(POOL_GROUP, F32)] * 4 + [(D, BF16)], [], [(D, POOL_GROUP), (1, D)])


def _cumsum_lanes(a, reverse):
    n = a.shape[1]
    lane = lax.broadcasted_iota(jnp.int32, a.shape, 1)
    s = 1
    while s < n:
        if reverse:
            a = a + jnp.where(lane < n - s, pltpu.roll(a, n - s, 1), 0.0)
        else:
            a = a + jnp.where(lane >= s, pltpu.roll(a, s, 1), 0.0)
        s *= 2
    return a


def _rows_to_cols(rows):
    r = rows.shape[0]
    if r < LANES:
        rows = jnp.concatenate([rows, jnp.zeros((LANES - r, rows.shape[1]), F32)], axis=0)
    return rows.T


def _cols_to_rows(cols):
    q = cols[0].shape[0]
    lane = lax.broadcasted_iota(jnp.int32, (q, LANES), 1)
    acc = jnp.zeros((q, LANES), F32)
    for r, c in enumerate(cols):
        acc = acc + jnp.where(lane == r, c, 0.0)
    return acc.T[0:len(cols)]


def _ssd_scalars(dtraw, bias, alog, reverse):
    dt = _softplus(dtraw + bias)
    A = -jnp.exp(alog)
    cs = _cumsum_lanes(dt * A, reverse)
    total = cs[:, 0:1] if reverse else cs[:, CHUNK - 1:CHUNK]
    return dt, A, cs, total


def _decay_matrix(cs_col, cs_row, reverse):
    i = lax.broadcasted_iota(jnp.int32, (CHUNK, CHUNK), 0)
    j = lax.broadcasted_iota(jnp.int32, (CHUNK, CHUNK), 1)
    keep = (i <= j) if reverse else (i >= j)
    return jnp.exp(jnp.where(keep, cs_col - cs_row, -jnp.inf))


def ssd_fwd(name, dtT, bias, alog, xbc, h0, direction, with_y):
    Bn, L = xbc.shape[:2]
    nc = L // CHUNK
    reverse = direction == 1
    rowblk = direction * N_BC

    def chunk_of(s):
        return (nc - 1 - s) if reverse else s

    def kern(dt_ref, bias_ref, alog_ref, x_ref, b_ref, c_ref, h0_ref, *rest):
        if with_y:
            y_ref, hs_ref, hf_ref, h_scr, xt_scr = rest
        else:
            hs_ref, hf_ref, h_scr, xt_scr = rest
        s = pl.program_id(2)

        @pl.when(s == 0)
        def _():
            h_scr[...] = h0_ref[0, 0]

        dt, _, cs, total = _ssd_scalars(dt_ref[0], bias_ref[0], alog_ref[0], reverse)
        e_row = jnp.exp(cs)
        t_row = jnp.exp(total - cs)
        dc = jnp.exp(total)
        cols = _rows_to_cols(jnp.concatenate([dt, e_row, t_row, cs], axis=0))
        x = x_ref[0]
        bm = b_ref[0].astype(BF16)
        cm = c_ref[0].astype(BF16)
        h = h_scr[...]
        hs_ref[0, 0, 0] = h
        if with_y:
            cb = _dot_nt(cm, bm)
            yoff = _dot(cm, h.astype(BF16))
        for r in range(HPG):
            sl = slice(r * HEAD_DIM, (r + 1) * HEAD_DIM)
            xdt = x[:, sl] * cols[:, r:r + 1]
            if with_y:
                lr = _decay_matrix(cols[:, 3 * HPG + r:3 * HPG + r + 1], cs[r:r + 1], reverse)
                ydiag = _dot((cb * lr).astype(BF16), xdt.astype(BF16))
                y_ref[0, :, sl] = ydiag + yoff[:, sl] * cols[:, HPG + r:HPG + r + 1]
            xt_scr[:, sl] = (xdt * cols[:, 2 * HPG + r:2 * HPG + r + 1]).astype(BF16)
        st = _dot_tn(bm, xt_scr[...])
        for r in range(HPG):
            sl = slice(r * HEAD_DIM, (r + 1) * HEAD_DIM)
            h_scr[:, sl] = h[:, sl] * dc[r:r + 1] + st[:, sl]

        @pl.when(s == nc - 1)
        def _():
            hf_ref[0, 0] = h_scr[...]

    in_specs = [
        pl.BlockSpec((1, HPG, CHUNK), lambda b, g, s: (b, rowblk + g, chunk_of(s))),
        pl.BlockSpec((1, HPG, 1), lambda b, g, s: (rowblk + g, 0, 0)),
        pl.BlockSpec((1, HPG, 1), lambda b, g, s: (rowblk + g, 0, 0)),
        pl.BlockSpec((1, CHUNK, GW), lambda b, g, s: (b, chunk_of(s), g)),
        pl.BlockSpec((1, CHUNK, D_STATE), lambda b, g, s: (b, chunk_of(s), D_INNER // D_STATE + g)),
        pl.BlockSpec((1, CHUNK, D_STATE), lambda b, g, s: (b, chunk_of(s), D_INNER // D_STATE + N_BC + g)),
        pl.BlockSpec((1, 1, D_STATE, GW), lambda b, g, s: (b, g, 0, 0)),
    ]
    out_shape, out_specs = [], []
    if with_y:
        out_shape.append(jax.ShapeDtypeStruct((Bn, L, D_INNER), F32))
        out_specs.append(pl.BlockSpec((1, CHUNK, GW), lambda b, g, s: (b, chunk_of(s), g)))
    out_shape += [jax.ShapeDtypeStruct((Bn, N_BC, nc, D_STATE, GW), F32), jax.ShapeDtypeStruct((Bn, N_BC, D_STATE, GW), F32)]
    out_specs += [pl.BlockSpec((1, 1, 1, D_STATE, GW), lambda b, g, s: (b, g, chunk_of(s), 0, 0)),
                  pl.BlockSpec((1, 1, D_STATE, GW), lambda b, g, s: (b, g, 0, 0))]
    return pl.pallas_call(
        kern, name=name, out_shape=out_shape, grid=(Bn, N_BC, nc), in_specs=in_specs, out_specs=out_specs,
        scratch_shapes=[pltpu.VMEM((D_STATE, GW), F32), pltpu.VMEM((CHUNK, GW), BF16)],
        compiler_params=_cp(("arbitrary", "arbitrary", "arbitrary")))(dtT, bias, alog, xbc, xbc, xbc, h0)


def ssd_bwd(name, dtT, bias, alog, xbc, h_start, dy, dh_final, direction):
    Bn, L = xbc.shape[:2]
    nc = L // CHUNK
    reverse = direction == 1
    rowblk = direction * N_BC
    has_y = dy is not None
    last = 0 if reverse else CHUNK - 1

    def chunk_of(s):
        return s if reverse else (nc - 1 - s)

    def kern(*refs):
        if has_y:
            (dt_ref, bias_ref, alog_ref, x_ref, b_ref, c_ref, hs_ref, dhf_ref, dy_ref,
             dx_ref, db_ref, dc_ref, ddt_ref, dbias_ref, dalog_ref, dh0_ref, dh_scr, e_scr, t_scr) = refs
        else:
            (dt_ref, bias_ref, alog_ref, x_ref, b_ref, hs_ref, dhf_ref,
             dx_ref, db_ref, ddt_ref, dbias_ref, dalog_ref, dh0_ref, dh_scr, t_scr) = refs
        s = pl.program_id(2)

        @pl.when(s == 0)
        def _():
            dh_scr[...] = dhf_ref[0, 0]
            dbias_ref[...] = jnp.zeros(dbias_ref.shape, F32)
            dalog_ref[...] = jnp.zeros(dalog_ref.shape, F32)

        dtraw = dt_ref[0]
        dt, A, cs, total = _ssd_scalars(dtraw, bias_ref[0], alog_ref[0], reverse)
        e_row = jnp.exp(cs)
        t_row = jnp.exp(total - cs)
        dcy = jnp.exp(total)
        cols = _rows_to_cols(jnp.concatenate([dt, e_row, t_row, cs], axis=0))
        x = x_ref[0]
        bm = b_ref[0].astype(BF16)
        h = hs_ref[0, 0, 0]
        dh = dh_scr[...]
        dh_bf = dh.astype(BF16)
        bdh = _dot(bm, dh_bf)
        if has_y:
            cm = c_ref[0].astype(BF16)
            dyv = dy_ref[0]
            cb = _dot_nt(cm, bm)
            yoff = _dot(cm, h.astype(BF16))
            dcb = jnp.zeros((CHUNK, CHUNK), F32)
        col_terms, row_terms, ddt_cols, dtot = [], [], [], []
        for r in range(HPG):
            sl = slice(r * HEAD_DIM, (r + 1) * HEAD_DIM)
            dt_c = cols[:, r:r + 1]
            e_c = cols[:, HPG + r:HPG + r + 1]
            t_c = cols[:, 2 * HPG + r:2 * HPG + r + 1]
            xr = x[:, sl]
            xdt = xr * dt_c
            dxdt = t_c * bdh[:, sl]
            d_t = jnp.sum(bdh[:, sl] * xdt, axis=1, keepdims=True)
            col = -(t_c * d_t)
            tot = jnp.sum(t_c * d_t, axis=0, keepdims=True) + dcy[r:r + 1] * jnp.sum(h[:, sl] * dh[:, sl], keepdims=True)
            if has_y:
                dyr = dyv[:, sl]
                lr = _decay_matrix(cols[:, 3 * HPG + r:3 * HPG + r + 1], cs[r:r + 1], reverse)
                w = cb * lr
                gm = _dot_nt(dyr.astype(BF16), xdt.astype(BF16))
                m = gm * w
                dcb = dcb + gm * lr
                dxdt = dxdt + _dot_tn(w.astype(BF16), dyr.astype(BF16))
                col = col + jnp.sum(m, axis=1, keepdims=True) + jnp.sum(yoff[:, sl] * dyr, axis=1, keepdims=True) * e_c
                row_terms.append(-jnp.sum(m, axis=0, keepdims=True))
                e_scr[:, sl] = (e_c * dyr).astype(BF16)
            t_scr[:, sl] = (t_c * xdt).astype(BF16)
            dx_ref[0, :, sl] = dxdt * dt_c
            ddt_cols.append(jnp.sum(dxdt * xr, axis=1, keepdims=True))
            col_terms.append(col)
            dtot.append(tot)
        db = _dot_nt(t_scr[...], dh_bf)
        if has_y:
            dcb_bf = dcb.astype(BF16)
            db = db + _dot_tn(dcb_bf, cm)
            dc_ref[0] = _dot(dcb_bf, bm) + _dot_nt(e_scr[...], h.astype(BF16))
            cte = _dot_tn(cm, e_scr[...])
        db_ref[0] = db
        for r in range(HPG):
            sl = slice(r * HEAD_DIM, (r + 1) * HEAD_DIM)
            new = dh[:, sl] * dcy[r:r + 1]
            if has_y:
                new = new + cte[:, sl]
            dh_scr[:, sl] = new
        dcs = _cols_to_rows(col_terms)
        if has_y:
            dcs = dcs + _stack_rows(row_terms)
        lane = lax.broadcasted_iota(jnp.int32, (HPG, CHUNK), 1)
        dcs = dcs + jnp.where(lane == last, _stack_rows([jnp.broadcast_to(t, (1, CHUNK)) for t in dtot]), 0.0)
        da = _cumsum_lanes(dcs, not reverse)
        ddt = da * A + _cols_to_rows(ddt_cols)
        ddtraw = ddt * _sigmoid(dtraw + bias_ref[0])
        ddt_ref[0] = ddtraw
        dbias_ref[0, 0] += jnp.sum(ddtraw, axis=1, keepdims=True)
        dalog_ref[0, 0] += jnp.sum(da * dt, axis=1, keepdims=True) * A

        @pl.when(s == nc - 1)
        def _():
            dh0_ref[0, 0] = dh_scr[...]

    cidx = lambda b, g, s: (b, chunk_of(s), g)
    in_specs = [
        pl.BlockSpec((1, HPG, CHUNK), lambda b, g, s: (b, rowblk + g, chunk_of(s))),
        pl.BlockSpec((1, HPG, 1), lambda b, g, s: (rowblk + g, 0, 0)),
        pl.BlockSpec((1, HPG, 1), lambda b, g, s: (rowblk + g, 0, 0)),
        pl.BlockSpec((1, CHUNK, GW), cidx),
        pl.BlockSpec((1, CHUNK, D_STATE), lambda b, g, s: (b, chunk_of(s), D_INNER // D_STATE + g)),
    ]
    args = [dtT, bias, alog, xbc, xbc]
    if has_y:
        in_specs.append(pl.BlockSpec((1, CHUNK, D_STATE), lambda b, g, s: (b, chunk_of(s), D_INNER // D_STATE + N_BC + g)))
        args.append(xbc)
    in_specs += [pl.BlockSpec((1, 1, 1, D_STATE, GW), lambda b, g, s: (b, g, chunk_of(s), 0, 0)),
                 pl.BlockSpec((1, 1, D_STATE, GW), lambda b, g, s: (b, g, 0, 0))]
    args += [h_start, dh_final]
    if has_y:
        in_specs.append(pl.BlockSpec((1, CHUNK, GW), cidx))
        args.append(dy)
    out_shape = [jax.ShapeDtypeStruct((Bn, L, D_INNER), F32), jax.ShapeDtypeStruct((Bn, L, N_BC * D_STATE), F32)]
    out_specs = [pl.BlockSpec((1, CHUNK, GW), cidx), pl.BlockSpec((1, CHUNK, D_STATE), cidx)]
    if has_y:
        out_shape.append(jax.ShapeDtypeStruct((Bn, L, N_BC * D_STATE), F32))
        out_specs.append(pl.BlockSpec((1, CHUNK, D_STATE), cidx))
    out_shape += [jax.ShapeDtypeStruct((Bn, N_HEADS, L), F32), jax.ShapeDtypeStruct((Bn, N_BC, HPG, 1), F32),
                  jax.ShapeDtypeStruct((Bn, N_BC, HPG, 1), F32), jax.ShapeDtypeStruct((Bn, N_BC, D_STATE, GW), F32)]
    out_specs += [pl.BlockSpec((1, HPG, CHUNK), lambda b, g, s: (b, g, chunk_of(s))),
                  pl.BlockSpec((1, 1, HPG, 1), lambda b, g, s: (b, g, 0, 0)),
                  pl.BlockSpec((1, 1, HPG, 1), lambda b, g, s: (b, g, 0, 0)),
                  pl.BlockSpec((1, 1, D_STATE, GW), lambda b, g, s: (b, g, 0, 0))]
    scratch = [pltpu.VMEM((D_STATE, GW), F32)] + ([pltpu.VMEM((CHUNK, GW), BF16)] if has_y else []) + [pltpu.VMEM((CHUNK, GW), BF16)]
    res = pl.pallas_call(
        kern, name=name, out_shape=out_shape, grid=(Bn, N_BC, nc), in_specs=in_specs, out_specs=out_specs,
        scratch_shapes=scratch, compiler_params=_cp(("arbitrary", "arbitrary", "arbitrary")))(*args)
    if has_y:
        return res
    dxs, db, ddt, dbias, dalog, dh0 = res
    return dxs, db, None, ddt, dbias, dalog, dh0


def _group_mean(v):
    gw = D_INNER // N_BC
    parts = [jnp.broadcast_to(jnp.mean(v[:, g * gw:(g + 1) * gw], axis=-1, keepdims=True), (v.shape[0], gw)) for g in range(N_BC)]
    return jnp.concatenate(parts, axis=1)


def gated_norm_fwd(name, y_f, y_b, xs_src, z, dskip_lanes, w_norm):
    def body(yf, yb, xs, z, dsk, w):
        u = (yf + yb + dsk * xs) * _silu(z)
        r = lax.rsqrt(_group_mean(u * u) + NORM_EPS)
        return u * r * w

    return tok_call(name, body, [y_f, y_b, xs_src, z], [], [dskip_lanes, w_norm], [(D_INNER, BF16)], [], [])[0]


def gated_norm_bwd(name, y_f, y_b, xs_src, z, d_out, dskip_lanes, w_norm, head_sel):
    def body(yf, yb, xs, z, do, dsk, w, sel):
        y = yf + yb + dsk * xs
        sz = _silu(z)
        u = y * sz
        r = lax.rsqrt(_group_mean(u * u) + NORM_EPS)
        duh = do * w
        du = r * (duh - u * (r * r) * _group_mean(duh * u))
        dy = du * sz
        dz = du * y * _dsilu(z)
        per_col = jnp.broadcast_to(_colsum(dy * xs), (8, D_INNER))
        hi, mid, lo = _split3(per_col)
        dsk_heads = _dot(hi, sel) + _dot(mid, sel) + _dot(lo, sel)
        return dy, dy * dsk, dz, _colsum(do * u * r), dsk_heads

    return tok_call(name, body, [y_f, y_b, xs_src, z, d_out], [], [dskip_lanes, w_norm, head_sel],
                    [(D_INNER, F32), (D_INNER, F32), (D_INNER, BF16)], [], [(1, D_INNER), (8, LANES)])


def merge_fwd(name, y_pool, y_ssd, gatepre, x, target, gate, b_merge, norm_post, w_pp, w_ps, w_out):
    def body(yp, ys, gp, x, tgt, gate, bm, wpost, w_pp, w_ps, w_out):
        p1 = _dot(yp, w_pp)
        p2 = _dot(ys, w_ps)
        gates = _sigmoid(gp + bm)
        merged = gates[:, :D] * p1 + gates[:, D:] * p2
        out = _dot(merged.astype(BF16), w_out)
        r = _rms_r(out)
        outr = out * r
        nq = outr * wpost
        err = x + gate * nq - tgt
        loss = 0.5 * jnp.sum(jnp.mean(err * err, axis=-1, keepdims=True), keepdims=True).reshape(1, 1)
        g = err * (1.0 / D)
        dnq = g * gate
        dout = _rms_bwd(dnq * wpost, out, r)
        return merged, p1, p2, dout, g, _colsum(g * nq), _colsum(dnq * outr), jnp.broadcast_to(loss, (1, LANES))

    return tok_call(name, body, [y_pool, y_ssd, gatepre, x, target], [gate], [b_merge, norm_post, w_pp, w_ps, w_out],
                    [(D, BF16), (D, F32), (D, F32), (D, BF16), (D, F32)], [D], [(1, D), (1, LANES)])


def merge_bwd(name, dout, gatepre, p1, p2, b_merge, w_pp, w_ps, w_out):
    def body(dout, gp, p1, p2, bm, w_pp, w_ps, w_out):
        dmerged = _dot_nt(dout, w_out)
        gates = _sigmoid(gp + bm)
        g1, g2 = gates[:, :D], gates[:, D:]
        dp1 = (dmerged * g1).astype(BF16)
        dp2 = (dmerged * g2).astype(BF16)
        dgp = jnp.concatenate([dmerged * p1 * g1 * (1.0 - g1), dmerged * p2 * g2 * (1.0 - g2)], axis=1)
        return dp1, dp2, dgp, _dot_nt(dp1, w_pp), _dot_nt(dp2, w_ps), _colsum(dgp)

    return tok_call(name, body, [dout, gatepre, p1, p2], [], [b_merge, w_pp, w_ps, w_out],
                    [(D, BF16), (D, BF16), (2 * D, BF16), (D, F32), (D_INNER, F32)], [], [(1, 2 * D)])


def _adamw_math(w, g, m, v):
    m = ADAM_B1 * m + (1.0 - ADAM_B1) * g
    v = ADAM_B2 * v + (1.0 - ADAM_B2) * (g * g)
    m_hat = m / (1.0 - ADAM_B1 ** ADAM_STEP)
    v_hat = v / (1.0 - ADAM_B2 ** ADAM_STEP)
    delta = -ADAM_LR * (m_hat / (jnp.sqrt(v_hat) + ADAM_EPS) + ADAM_WD * w)
    return delta, m, v


def adamw(name, w, g, m, v, tr=256):
    R, C = w.shape
    tr = min(tr, R)
    assert R % tr == 0

    def body(w_ref, g_ref, m_ref, v_ref, d_ref, nm_ref, nv_ref):
        d, nm, nv = _adamw_math(w_ref[...], g_ref[...], m_ref[...], v_ref[...])
        d_ref[...] = d
        nm_ref[...] = nm
        nv_ref[...] = nv

    spec = pl.BlockSpec((tr, C), lambda i: (i, 0))
    return pl.pallas_call(
        body, name=name, out_shape=[jax.ShapeDtypeStruct((R, C), F32)] * 3, grid=(R // tr,),
        in_specs=[spec] * 4, out_specs=[spec] * 3, compiler_params=_cp(("parallel",)))(w, g, m, v)


def _me():
    return lax.axis_index("x"), lax.axis_index("y"), lax.axis_index("c")


def all_gather_small(name, v):
    R, C = v.shape

    def body(v_ref, out_ref, send_sems, recv_sems, local_sem):
        x, y, c = _me()
        me = 4 * x + 2 * y + c
        mine = pltpu.make_async_copy(v_ref, out_ref.at[me], local_sem)
        mine.start()
        copies = []
        for d in range(1, N_DEV):
            dx, dy, dc = d // 4, (d // 2) % 2, d % 2
            px, py, pc = x ^ dx, y ^ dy, c ^ dc
            copies.append(pltpu.make_async_remote_copy(
                src_ref=v_ref, dst_ref=out_ref.at[me], send_sem=send_sems.at[d - 1], recv_sem=recv_sems.at[d - 1],
                device_id=(px, py, pc), device_id_type=MESH))
        for cp in copies:
            cp.start()
        for d in range(1, N_DEV):
            dx, dy, dc = d // 4, (d // 2) % 2, d % 2
            peer = 4 * (x ^ dx) + 2 * (y ^ dy) + (c ^ dc)
            pltpu.make_async_remote_copy(
                src_ref=v_ref, dst_ref=out_ref.at[peer], send_sem=send_sems.at[d - 1], recv_sem=recv_sems.at[d - 1],
                device_id=(x ^ dx, y ^ dy, c ^ dc), device_id_type=MESH).wait_recv()
        for cp in copies:
            cp.wait_send()
        mine.wait()

    return pl.pallas_call(
        body, name=name, out_shape=jax.ShapeDtypeStruct((N_DEV, R, C), F32),
        in_specs=[pl.BlockSpec(memory_space=pltpu.VMEM)], out_specs=pl.BlockSpec(memory_space=pltpu.VMEM),
        scratch_shapes=[pltpu.SemaphoreType.DMA((N_DEV - 1,)), pltpu.SemaphoreType.DMA((N_DEV - 1,)), pltpu.SemaphoreType.DMA],
        compiler_params=pltpu.CompilerParams(vmem_limit_bytes=VMEM_LIMIT))(v)


def all_gather_chips(name, shard):
    R, C = shard.shape
    half = R // 2
    assert R % 32 == 0

    def body(s_ref, out_ref, send_sems, recv_sems, local_sem):
        x, y, c = _me()
        k = 2 * x + y
        chips = [(1 - x, y), (x, 1 - y), (1 - x, 1 - y)]

        def rows(chip, hc):
            return out_ref.at[2 * chip[0] + chip[1], pl.ds(hc * half, half), :]

        mine = pltpu.make_async_copy(s_ref, out_ref.at[k], local_sem)
        mine.start()
        first = [pltpu.make_async_remote_copy(
            src_ref=s_ref.at[pl.ds(c * half, half), :], dst_ref=rows((x, y), c), send_sem=send_sems.at[j],
            recv_sem=recv_sems.at[j], device_id=(*chip, c), device_id_type=MESH) for j, chip in enumerate(chips)]
        for cp in first:
            cp.start()
        passed = [pltpu.make_async_remote_copy(
            src_ref=rows(chip, c), dst_ref=rows(chip, c), send_sem=send_sems.at[3 + j], recv_sem=recv_sems.at[3 + j],
            device_id=(x, y, 1 - c), device_id_type=MESH) for j, chip in enumerate(chips)]
        for j, chip in enumerate(chips):
            pltpu.make_async_remote_copy(
                src_ref=rows(chip, c), dst_ref=rows(chip, c), send_sem=send_sems.at[j], recv_sem=recv_sems.at[j],
                device_id=(*chip, c), device_id_type=MESH).wait_recv()
            passed[j].start()
        for j, chip in enumerate(chips):
            pltpu.make_async_remote_copy(
                src_ref=rows(chip, 1 - c), dst_ref=rows(chip, 1 - c), send_sem=send_sems.at[3 + j], recv_sem=recv_sems.at[3 + j],
                device_id=(x, y, 1 - c), device_id_type=MESH).wait_recv()
        for cp in first + passed:
            cp.wait_send()
        mine.wait()

    return pl.pallas_call(
        body, name=name, out_shape=jax.ShapeDtypeStruct((N_CHIPS, R, C), shard.dtype),
        in_specs=[pl.BlockSpec(memory_space=pl.ANY)], out_specs=pl.BlockSpec(memory_space=pl.ANY),
        scratch_shapes=[pltpu.SemaphoreType.DMA((6,)), pltpu.SemaphoreType.DMA((6,)), pltpu.SemaphoreType.DMA],
        compiler_params=pltpu.CompilerParams(vmem_limit_bytes=VMEM_LIMIT))(shard)


def sibling_swap(name, v):
    def body(v_ref, out_ref, send_sem, recv_sem):
        x, y, c = _me()
        cp = pltpu.make_async_remote_copy(src_ref=v_ref, dst_ref=out_ref, send_sem=send_sem, recv_sem=recv_sem,
                                          device_id=(x, y, 1 - c), device_id_type=MESH)
        cp.start()
        cp.wait()

    return pl.pallas_call(
        body, name=name, out_shape=jax.ShapeDtypeStruct(v.shape, v.dtype),
        in_specs=[pl.BlockSpec(memory_space=pl.ANY)], out_specs=pl.BlockSpec(memory_space=pl.ANY),
        scratch_shapes=[pltpu.SemaphoreType.DMA, pltpu.SemaphoreType.DMA],
        compiler_params=pltpu.CompilerParams(vmem_limit_bytes=VMEM_LIMIT))(v)


def chip_exchange(name, parts):
    def body(p_ref, out_ref, send_sems, recv_sems, local_sem):
        x, y, c = _me()
        k = 2 * x + y
        chips = [(1 - x, y), (x, 1 - y), (1 - x, 1 - y)]
        mine = pltpu.make_async_copy(p_ref.at[k], out_ref.at[k], local_sem)
        mine.start()
        sends = [pltpu.make_async_remote_copy(
            src_ref=p_ref.at[2 * chip[0] + chip[1]], dst_ref=out_ref.at[k], send_sem=send_sems.at[j], recv_sem=recv_sems.at[j],
            device_id=(*chip, c), device_id_type=MESH) for j, chip in enumerate(chips)]
        for cp in sends:
            cp.start()
        for j, chip in enumerate(chips):
            pltpu.make_async_remote_copy(
                src_ref=p_ref.at[k], dst_ref=out_ref.at[2 * chip[0] + chip[1]], send_sem=send_sems.at[j], recv_sem=recv_sems.at[j],
                device_id=(*chip, c), device_id_type=MESH).wait_recv()
        for cp in sends:
            cp.wait_send()
        mine.wait()

    return pl.pallas_call(
        body, name=name, out_shape=jax.ShapeDtypeStruct(parts.shape, parts.dtype),
        in_specs=[pl.BlockSpec(memory_space=pl.ANY)], out_specs=pl.BlockSpec(memory_space=pl.ANY),
        scratch_shapes=[pltpu.SemaphoreType.DMA((3,)), pltpu.SemaphoreType.DMA((3,)), pltpu.SemaphoreType.DMA],
        compiler_params=pltpu.CompilerParams(vmem_limit_bytes=VMEM_LIMIT))(parts)


def _row_tile(rows, cap):
    best = None
    for t in range(8, min(rows, cap) + 1, 8):
        if rows % t == 0:
            best = t
    assert best is not None, rows
    return best


def add_arrays(name, arrs, out_dtype=F32):
    shape = arrs[0].shape
    C = shape[-1]
    flat = [a.reshape(-1, C) for a in arrs]
    R = flat[0].shape[0]
    tr = _row_tile(R, 1024)
    n = len(flat)

    def body(*refs):
        acc = refs[0][...].astype(F32)
        for r in refs[1:n]:
            acc = acc + r[...].astype(F32)
        refs[n][...] = acc.astype(out_dtype)

    spec = pl.BlockSpec((tr, C), lambda i: (i, 0))
    out = pl.pallas_call(
        body, name=name, out_shape=jax.ShapeDtypeStruct((R, C), out_dtype), grid=(R // tr,),
        in_specs=[spec] * n, out_specs=spec, compiler_params=_cp(("parallel",)))(*flat)
    return out.reshape(shape)


def reduce_scatter_chips(slabs):
    _, R, C = slabs.shape
    half = R // 2
    c = lax.axis_index("c")
    k = 2 * lax.axis_index("x") + lax.axis_index("y")
    halves = slabs.reshape(N_CHIPS, 2, half, C)
    own = lax.dynamic_index_in_dim(halves, c, axis=1, keepdims=False)
    other = lax.dynamic_index_in_dim(halves, 1 - c, axis=1, keepdims=False)
    from_sibling = sibling_swap("rs_sibling_halves", other)
    chip_part = add_arrays("rs_add_sibling", [own, from_sibling])
    landed = chip_exchange("rs_chip_exchange", chip_part)
    mine = add_arrays("rs_add_chips", [landed[j] for j in range(N_CHIPS)])
    sib = sibling_swap("rs_sibling_result", mine)
    lo = jnp.where(c == 0, mine, sib)
    hi = jnp.where(c == 0, sib, mine)
    del k
    return jnp.concatenate([lo, hi], axis=0)


def ada_mod_shard(cond_all, w_ada_shard, b_ada_shard):
    def body(c_ref, w_ref, b_ref, o_ref):
        o_ref[...] = _dot(_silu(c_ref[...]).astype(BF16), w_ref[...].astype(BF16)) + b_ref[...]

    return pl.pallas_call(body, name="ada_mod_shard", out_shape=jax.ShapeDtypeStruct((cond_all.shape[0], w_ada_shard.shape[1]), F32),
                          compiler_params=_cp())(cond_all, w_ada_shard, b_ada_shard)


def ada_bwd_shard(cond_all, dmod_all_shard, dmod_all, w_ada_shard, row_is_cctx):
    def body(c_ref, ds_ref, da_ref, w_ref, sel_ref, gw_ref, gb_ref, part_ref):
        sc = _silu(c_ref[...]).astype(BF16)
        gw_ref[...] = _dot_tn(sc, ds_ref[...].astype(BF16))
        gb_ref[...] = _colsum(da_ref[...])
        dc_tot = jnp.broadcast_to(_colsum(ds_ref[...] * sel_ref[...]), (8, ds_ref.shape[1]))
        part_ref[...] = _dot_nt(dc_tot.astype(BF16), w_ref[...].astype(BF16))

    n = cond_all.shape[0]
    return pl.pallas_call(
        body, name="ada_bwd_shard",
        out_shape=[jax.ShapeDtypeStruct(w_ada_shard.shape, F32), jax.ShapeDtypeStruct((1, dmod_all.shape[1]), F32),
                   jax.ShapeDtypeStruct((8, D), F32)],
        compiler_params=_cp())(cond_all, dmod_all_shard, dmod_all, w_ada_shard, row_is_cctx)


def sum_devices(name, gathered):
    def body(g_ref, o_ref):
        acc = g_ref[0]
        for d in range(1, N_DEV):
            acc = acc + g_ref[d]
        o_ref[...] = acc

    return pl.pallas_call(body, name=name, out_shape=jax.ShapeDtypeStruct(gathered.shape[1:], F32), compiler_params=_cp())(gathered)


def cctx_finish(gathered, c_ctx_row):
    def body(g_ref, c_ref, o_ref):
        acc = g_ref[0, 0:1, :]
        for k in range(1, N_CHIPS):
            acc = acc + g_ref[2 * k, 0:1, :]
        o_ref[...] = acc * _dsilu(c_ref[...])

    return pl.pallas_call(body, name="cctx_finish", out_shape=jax.ShapeDtypeStruct((1, D), F32), compiler_params=_cp())(gathered, c_ctx_row)


def _pack(parts, rows):
    flat = []
    for p in parts:
        p = p.reshape(-1)
        pad = (-p.shape[0]) % LANES
        flat.append(jnp.pad(p, (0, pad)) if pad else p)
    v = jnp.concatenate(flat)
    return jnp.pad(v, (0, rows * LANES - v.shape[0])).reshape(rows, LANES)


def _unpack(v, sizes):
    flat = v.reshape(-1)
    out, off = [], 0
    for n in sizes:
        out.append(flat[off:off + n])
        off += n + (-n) % LANES
    return out


W_SHARD_ROWS = 3424
SEG_ROWS = (0, 2320, 2576, 3088, 3344, 3408)


def kernel(x, c, ctx, c_ctx, w_ada, b_ada, norm_pre, norm_post, w_in, b_merge, pool_w, pool_scale, conv_w, conv_b, dt_bias, a_log, d_skip, ssd_norm, w_proj_pool, w_proj_ssd, w_out, loss_target, m_c_ctx, m_w_ada, m_b_ada, m_norm_pre, m_norm_post, m_w_in, m_b_merge, m_pool_w, m_pool_scale, m_conv_w, m_conv_b, m_dt_bias, m_a_log, m_d_skip, m_ssd_norm, m_w_proj_pool, m_w_proj_ssd, m_w_out, v_c_ctx, v_w_ada, v_b_ada, v_norm_pre, v_norm_post, v_w_in, v_b_merge, v_pool_w, v_pool_scale, v_conv_w, v_conv_b, v_dt_bias, v_a_log, v_d_skip, v_ssd_norm, v_w_proj_pool, v_w_proj_ssd, v_w_out):
    Bn, L, _ = x.shape
    Lc = ctx.shape[1]
    T, Tc = Bn * L, Bn * Lc
    assert Bn == 2
    ix, iy, ic = lax.axis_index("x"), lax.axis_index("y"), lax.axis_index("c")
    me = 4 * ix + 2 * iy + ic
    chip = 2 * ix + iy
    ada_cols = w_ada.shape[2]
    cw_cols = conv_w.shape[2]

    cond_own = jnp.pad(c, ((0, 8 - Bn), (0, 0))) + jnp.pad(c_ctx[None, :], ((Bn, 7 - Bn), (0, 0)))
    convw_own = jnp.pad(conv_w[0], ((0, 4), (0, D - cw_cols)))
    g1 = all_gather_small("gather_cond", jnp.concatenate([cond_own, convw_own], axis=0))
    cond_all = g1[:, 0:8].reshape(8 * N_DEV, D)
    conv_w_full = jnp.concatenate([g1[2 * k, 8:12, 0:cw_cols] for k in range(N_CHIPS)], axis=1)
    b_ada_shard = lax.dynamic_slice(b_ada, (0, chip * ada_cols), (1, ada_cols))
    g2 = all_gather_small("gather_mod", ada_mod_shard(cond_all, w_ada[0], b_ada_shard))
    mod_full = jnp.concatenate([g2[2 * k] for k in range(N_CHIPS)], axis=1)
    own = lax.dynamic_slice(mod_full, (8 * me, 0), (8, 3 * D))
    shift, scale, gate = (own[0:Bn, i * D:(i + 1) * D][:, None, :] for i in range(3))
    shift_c, scale_c = (jnp.broadcast_to(own[Bn:Bn + 1, i * D:(i + 1) * D][None], (Bn, 1, D)) for i in range(2))

    shard = jnp.concatenate([w_in[0].T, w_proj_pool[0], w_proj_ssd[0], w_out[0], pool_w[0].reshape(64, D),
                             jnp.zeros((W_SHARD_ROWS - SEG_ROWS[-1], D), F32)], axis=0).astype(BF16)
    gw = all_gather_chips("gather_weights", shard)
    w_inT = gw[:, SEG_ROWS[0]:SEG_ROWS[1]].reshape(IN_COLS, D)
    w_pp = gw[:, SEG_ROWS[1]:SEG_ROWS[2]].reshape(D, D)
    w_ps = gw[:, SEG_ROWS[2]:SEG_ROWS[3]].reshape(D_INNER, D)
    w_o = gw[:, SEG_ROWS[3]:SEG_ROWS[4]].reshape(D, D)
    pool_full = gw[:, SEG_ROWS[4]:SEG_ROWS[5]].reshape(N_CHIPS, 4, 64, POOL_GROUP).transpose(1, 0, 2, 3).reshape(D, POOL_GROUP)
    w_dt = jnp.pad(w_inT[9216:IN_COLS], ((0, LANES - 64), (0, 0)))
    seg_lo = (0, 256, 512, 768, 1024, 2048, 4096, 6144, 8192, 8704)
    seg_hi = (256, 512, 768, 1024, 2048, 4096, 6144, 8192, 8704, 9216)
    w_seg = [w_inT[lo:hi] for lo, hi in zip(seg_lo, seg_hi)] + [w_dt]

    hx = prenorm_fwd("prenorm_x", x, scale, shift, norm_pre)
    hc = prenorm_fwd("prenorm_ctx", ctx, scale_c, shift_c, norm_pre)
    hx2, hc2 = hx.reshape(T, D), hc.reshape(Tc, D)
    v = mm_nt("proj_v", hx2, w_inT[0:1024], F32).reshape(Bn, L, D)
    zp = mm_nt("proj_zpool", hx2, w_inT[1024:2048], F32).reshape(Bn, L, D)
    zs = mm_nt("proj_zssd", hx2, w_inT[2048:4096], F32).reshape(Bn, L, D_INNER)
    gp = mm_nt("proj_gate", hx2, w_inT[4096:6144], F32).reshape(Bn, L, 2 * D)
    xbc_raw = mm_nt("proj_xbc", hx2, w_inT[6144:9216], F32).reshape(Bn, L, CONV_DIM)
    dt_raw = mm_nt("proj_dt", hx2, w_dt, F32)
    xbc_raw_c = mm_nt("proj_xbc_ctx", hc2, w_inT[6144:9216], F32).reshape(Bn, Lc, CONV_DIM)
    dt_raw_c = mm_nt("proj_dt_ctx", hc2, w_dt, F32)
    dtT = dt_raw[:, :64].reshape(Bn, L, 64).transpose(0, 2, 1)
    dtT_c = dt_raw_c[:, :64].reshape(Bn, Lc, 64).transpose(0, 2, 1)
    bias3 = dt_bias.reshape(2 * N_BC, HPG, 1)
    alog3 = a_log.reshape(2 * N_BC, HPG, 1)

    xbc = conv_fwd("conv_x", xbc_raw, conv_w_full, conv_b)
    xbc_c = conv_fwd("conv_ctx", xbc_raw_c, conv_w_full, conv_b)
    zero_state = jnp.zeros((Bn, N_BC, D_STATE, GW), F32)
    ys, hs_x, hs_c = [], [], []
    for d in range(2):
        hsc, hfc = ssd_fwd(f"ssd_fwd_ctx{d}", dtT_c, bias3, alog3, xbc_c, zero_state, d, False)
        y, hsx, _ = ssd_fwd(f"ssd_fwd_x{d}", dtT, bias3, alog3, xbc, hfc, d, True)
        ys.append(y)
        hs_x.append(hsx)
        hs_c.append(hsc)

    dgs = [pool_diff(f"pool_diff{g}", v, g * POOL_GROUP, g, False) for g in range(4)]
    y_pool = pool_mix_fwd("pool_mix", dgs, zp, pool_full, pool_scale)
    dskip_lanes = jnp.repeat(d_skip[0], HEAD_DIM)[None, :]
    y_ssd = gated_norm_fwd("gated_norm", ys[0], ys[1], (xbc, D_INNER), zs, dskip_lanes, ssd_norm)
    merged, p1, p2, dout, g_res, dgate, g_norm_post, loss_part = merge_fwd(
        "merge_fwd", y_pool, y_ssd, gp, x, loss_target, gate, b_merge, norm_post, w_pp, w_ps, w_o)

    dp1, dp2, dgp, dyp, dys, g_b_merge = merge_bwd("merge_bwd", dout, gp, p1, p2, b_merge, w_pp, w_ps, w_o)
    gw_o = mm_tn("gw_out", merged.reshape(T, D), dout.reshape(T, D))
    gw_pp = mm_tn("gw_proj_pool", y_pool.reshape(T, D), dp1.reshape(T, D))
    gw_ps = mm_tn("gw_proj_ssd", y_ssd.reshape(T, D_INNER), dp2.reshape(T, D))

    *dds, dzp, g_pool, g_pool_scale = pool_mix_bwd("pool_mix_bwd", dgs, zp, dyp, pool_full, pool_scale)
    dvs = [pool_diff(f"pool_diff_t{g}", dds[g], 0, g, True) for g in range(4)]

    head_sel = (jnp.arange(D_INNER)[:, None] // HEAD_DIM == jnp.arange(LANES)[None, :]).astype(BF16)
    dy, dxs_skip, dzs, g_ssd_norm, g_dskip = gated_norm_bwd(
        "gated_norm_bwd", ys[0], ys[1], (xbc, D_INNER), zs, dys, dskip_lanes, ssd_norm, head_sel)

    dxs, dbm, dcm, ddt, dxs_c, dbm_c, ddt_c = [], [], [], [], [], [], []
    g_bias = jnp.zeros((2, N_BC, HPG, 1), F32)
    g_alog = jnp.zeros((2, N_BC, HPG, 1), F32)
    for d in range(2):
        a, b_, c_, t_, gb, ga, dh0 = ssd_bwd(f"ssd_bwd_x{d}", dtT, bias3, alog3, xbc, hs_x[d], dy, zero_state, d)
        dxs.append(a), dbm.append(b_), dcm.append(c_), ddt.append(t_)
        ac, bc, _, tc, gbc, gac, _ = ssd_bwd(f"ssd_bwd_ctx{d}", dtT_c, bias3, alog3, xbc_c, hs_c[d], None, dh0, d)
        dxs_c.append(ac), dbm_c.append(bc), ddt_c.append(tc)
        g_bias = g_bias.at[d].set(jnp.sum(gb, axis=0) + jnp.sum(gbc, axis=0))
        g_alog = g_alog.at[d].set(jnp.sum(ga, axis=0) + jnp.sum(gac, axis=0))

    dxr_xs, gcw_xs, gcb_xs = conv_bwd("conv_bwd_xs", xbc_raw, dxs + [dxs_skip], conv_w_full, conv_b, 0, D_INNER)
    dxr_b, gcw_b, gcb_b = conv_bwd("conv_bwd_b", xbc_raw, dbm, conv_w_full, conv_b, D_INNER, N_BC * D_STATE)
    dxr_c, gcw_c, gcb_c = conv_bwd("conv_bwd_c", xbc_raw, dcm, conv_w_full, conv_b, D_INNER + N_BC * D_STATE, N_BC * D_STATE)
    dxr_xs_c, gcw_xs_c, gcb_xs_c = conv_bwd("conv_bwd_xs_ctx", xbc_raw_c, dxs_c, conv_w_full, conv_b, 0, D_INNER)
    dxr_b_c, gcw_b_c, gcb_b_c = conv_bwd("conv_bwd_b_ctx", xbc_raw_c, dbm_c, conv_w_full, conv_b, D_INNER, N_BC * D_STATE)
    g_conv_w = jnp.concatenate([gcw_xs + gcw_xs_c, gcw_b + gcw_b_c, gcw_c], axis=1)
    g_conv_b = jnp.concatenate([gcb_xs + gcb_xs_c, gcb_b + gcb_b_c, gcb_c], axis=1)

    def dt_cols(parts, n_tok):
        t = jnp.concatenate(parts, axis=1).transpose(0, 2, 1).reshape(n_tok, 2 * N_HEADS)
        return jnp.pad(t, ((0, 0), (0, LANES - 2 * N_HEADS))).astype(BF16)

    ddt2, ddt2_c = dt_cols(ddt, T), dt_cols(ddt_c, Tc)
    segs = ([dv.reshape(T, POOL_GROUP) for dv in dvs]
            + [dzp.reshape(T, D), dzs.reshape(T, D_INNER), dgp.reshape(T, 2 * D), dxr_xs.reshape(T, D_INNER),
               dxr_b.reshape(T, N_BC * D_STATE), dxr_c.reshape(T, N_BC * D_STATE), ddt2])
    d_hx = mm_nn_multi("d_hx", list(zip(segs, w_seg)), F32).reshape(Bn, L, D)
    segs_c = {7: dxr_xs_c.reshape(Tc, D_INNER), 8: dxr_b_c.reshape(Tc, N_BC * D_STATE), 10: ddt2_c}
    d_hc = mm_nn_multi("d_hc", [(segs_c[i], w_seg[i]) for i in (7, 8, 10)], F32).reshape(Bn, Lc, D)
    gw_rows = []
    for i, seg in enumerate(segs):
        init = mm_tn(f"gw_in_ctx{i}", segs_c[i], hc2) if i in segs_c else None
        gw_rows.append(mm_tn(f"gw_in{i}", seg, hx2, init=init))
    gw_rows[-1] = gw_rows[-1][0:2 * N_HEADS]
    gw_inT = jnp.concatenate(gw_rows, axis=0)

    grad_x, dscale, dshift, g_npre_x = prenorm_bwd("prenorm_bwd_x", x, d_hx, scale, norm_pre, g_res=g_res)
    _, dscale_c, dshift_c, g_npre_c = prenorm_bwd("prenorm_bwd_ctx", ctx, d_hc, scale_c, norm_pre)

    dmod_x = jnp.concatenate([dshift[:, 0], dscale[:, 0], dgate[:, 0]], axis=1)
    dmod_c = jnp.concatenate([jnp.sum(dshift_c[:, 0], axis=0, keepdims=True), jnp.sum(dscale_c[:, 0], axis=0, keepdims=True),
                              jnp.zeros((1, D), F32)], axis=1)
    dmod_own = jnp.pad(dmod_x, ((0, 8 - Bn), (0, 0))) + jnp.pad(dmod_c, ((Bn, 7 - Bn), (0, 0)))
    dmod_all = all_gather_small("gather_dmod", dmod_own).reshape(8 * N_DEV, 3 * D)
    row_is_cctx = (jnp.arange(8 * N_DEV) % 8 == Bn).astype(F32)[:, None]
    g_w_ada, g_b_ada, cpart = ada_bwd_shard(
        cond_all, lax.dynamic_slice(dmod_all, (0, chip * ada_cols), (8 * N_DEV, ada_cols)), dmod_all, w_ada[0], row_is_cctx)
    g_c_ctx = cctx_finish(all_gather_small("gather_cctx", cpart), c_ctx[None, :])

    small_sizes = (D, D, 2 * D, D, CONV_DIM, 2 * N_HEADS, 2 * N_HEADS, N_HEADS, D_INNER, 4 * CONV_DIM, 1)
    pk = _pack([g_npre_x + g_npre_c, g_norm_post, g_b_merge, g_pool_scale, g_conv_b, g_bias, g_alog, g_dskip[0, 0:N_HEADS],
                g_ssd_norm, g_conv_w, loss_part[0, 0:1]], 184)
    small = sum_devices("sum_small", all_gather_small("gather_small", pk))
    (g_norm_pre, g_norm_post_t, g_b_merge_t, g_pool_scale_t, g_conv_b_t, g_dt_bias, g_a_log, g_d_skip, g_ssd_norm_t,
     g_conv_w_t, loss) = _unpack(small, small_sizes)
    g_conv_w_shard = lax.dynamic_slice(g_conv_w_t.reshape(4, CONV_DIM), (0, chip * cw_cols), (4, cw_cols))

    pool_slab = g_pool.reshape(4, N_CHIPS, 64, POOL_GROUP).transpose(1, 0, 2, 3).reshape(N_CHIPS, 64, D)
    slabs = jnp.concatenate([gw_inT.reshape(N_CHIPS, 2320, D), gw_pp.reshape(N_CHIPS, 256, D), gw_ps.reshape(N_CHIPS, 512, D),
                             gw_o.reshape(N_CHIPS, 256, D), pool_slab, jnp.zeros((N_CHIPS, W_SHARD_ROWS - SEG_ROWS[-1], D), F32)], axis=1)
    gsh = reduce_scatter_chips(slabs)
    g_w_in = gsh[SEG_ROWS[0]:SEG_ROWS[1]].T
    g_w_pp, g_w_ps, g_w_o = (gsh[SEG_ROWS[i]:SEG_ROWS[i + 1]] for i in (1, 2, 3))
    g_pool_w = gsh[SEG_ROWS[4]:SEG_ROWS[5]].reshape(256, POOL_GROUP)

    grads = {
        "c_ctx": g_c_ctx.reshape(c_ctx.shape), "w_ada": g_w_ada[None], "b_ada": g_b_ada, "norm_pre": g_norm_pre[None],
        "norm_post": g_norm_post_t[None], "w_in": g_w_in[None], "b_merge": g_b_merge_t[None],
        "pool_w": g_pool_w.reshape(pool_w.shape), "pool_scale": g_pool_scale_t[None], "conv_w": g_conv_w_shard[None],
        "conv_b": g_conv_b_t[None], "dt_bias": g_dt_bias.reshape(dt_bias.shape), "a_log": g_a_log.reshape(a_log.shape),
        "d_skip": g_d_skip[None], "ssd_norm": g_ssd_norm_t[None], "w_proj_pool": g_w_pp[None], "w_proj_ssd": g_w_ps[None],
        "w_out": g_w_o[None]}
    weights = dict(c_ctx=c_ctx, w_ada=w_ada, b_ada=b_ada, norm_pre=norm_pre, norm_post=norm_post, w_in=w_in, b_merge=b_merge,
                   pool_w=pool_w, pool_scale=pool_scale, conv_w=conv_w, conv_b=conv_b, dt_bias=dt_bias, a_log=a_log,
                   d_skip=d_skip, ssd_norm=ssd_norm, w_proj_pool=w_proj_pool, w_proj_ssd=w_proj_ssd, w_out=w_out)
    m_in = dict(c_ctx=m_c_ctx, w_ada=m_w_ada, b_ada=m_b_ada, norm_pre=m_norm_pre, norm_post=m_norm_post, w_in=m_w_in,
                b_merge=m_b_merge, pool_w=m_pool_w, pool_scale=m_pool_scale, conv_w=m_conv_w, conv_b=m_conv_b,
                dt_bias=m_dt_bias, a_log=m_a_log, d_skip=m_d_skip, ssd_norm=m_ssd_norm, w_proj_pool=m_w_proj_pool,
                w_proj_ssd=m_w_proj_ssd, w_out=m_w_out)
    v_in = dict(c_ctx=v_c_ctx, w_ada=v_w_ada, b_ada=v_b_ada, norm_pre=v_norm_pre, norm_post=v_norm_post, w_in=v_w_in,
                b_merge=v_b_merge, pool_w=v_pool_w, pool_scale=v_pool_scale, conv_w=v_conv_w, conv_b=v_conv_b,
                dt_bias=v_dt_bias, a_log=v_a_log, d_skip=v_d_skip, ssd_norm=v_ssd_norm, w_proj_pool=v_w_proj_pool,
                w_proj_ssd=v_w_proj_ssd, w_out=v_w_out)
    names = list(weights)
    big = ("w_ada", "w_in", "pool_w", "w_proj_pool", "w_proj_ssd", "w_out")
    small_names = [n for n in names if n not in big]
    delta, new_m, new_v = {}, {}, {}
    for n in big:
        shape2 = (-1, weights[n].shape[-1])
        d_, m_, v_ = adamw(f"adamw_{n}", weights[n].reshape(shape2), grads[n].reshape(shape2), m_in[n].reshape(shape2),
                           v_in[n].reshape(shape2), tr=128)
        delta[n], new_m[n], new_v[n] = (t.reshape(weights[n].shape) for t in (d_, m_, v_))
    sizes = [weights[n].size for n in small_names]
    packed = [_pack([src[n] for n in small_names], 144) for src in (weights, grads, m_in, v_in)]
    outs = adamw("adamw_small", *packed, tr=144)
    for res, store in zip(outs, (delta, new_m, new_v)):
        for n, piece in zip(small_names, _unpack(res, sizes)):
            store[n] = piece.reshape(weights[n].shape)

    return (loss.reshape(()), grad_x, *[grads[n] for n in names], *[delta[n] for n in names],
            *[new_m[n] for n in names], *[new_v[n] for n in names])
```

```python
import jax
import jax.numpy as jnp
from jax import lax
from jax.experimental import pallas as pl
from jax.experimental.pallas import tpu as pltpu

F32 = jnp.float32
BF16 = jnp.bfloat16
MESH = pl.DeviceIdType.MESH

D = 1024
GRID_W = 64
NORM_EPS = 1e-6
POOL_WINDOWS = (2, 4, 8, 16)
POOL_GROUP = 256
D_INNER = 2048
HEAD_DIM = 64
N_HEADS = 32
D_STATE = 128
N_BC = 4
HPG = N_HEADS // N_BC
GW = HPG * HEAD_DIM
CONV_DIM = 3072
CHUNK = 128
OFF_XBC = 6144
IN_COLS = 9280
N_CHIPS = 4
N_DEV = 8

ADAM_LR = 0.001
ADAM_B1 = 0.9
ADAM_B2 = 0.999
ADAM_EPS = 1e-08
ADAM_WD = 0.01
ADAM_STEP = 10

V7X_VMEM_BYTES = 64 * 1024 * 1024
VMEM_LIMIT = V7X_VMEM_BYTES * 3 // 4
LANES = 128


def _cp(sem=None):
    return pltpu.CompilerParams(dimension_semantics=sem, vmem_limit_bytes=VMEM_LIMIT)


def _dot(a, b):
    return jnp.dot(a, b, preferred_element_type=F32)


def _dot_nt(a, b):
    return lax.dot_general(a, b, (((1,), (1,)), ((), ())), preferred_element_type=F32)


def _dot_tn(a, b):
    return lax.dot_general(a, b, (((0,), (0,)), ((), ())), preferred_element_type=F32)


def _split3(x):
    hi = x.astype(BF16)
    r1 = x - hi.astype(F32)
    mid = r1.astype(BF16)
    lo = (r1 - mid.astype(F32)).astype(BF16)
    return hi, mid, lo


def _sigmoid(x):
    return jax.nn.sigmoid(x)


def _silu(x):
    return x * _sigmoid(x)


def _dsilu(x):
    s = _sigmoid(x)
    return s * (1.0 + x * (1.0 - s))


def _softplus(x):
    return jnp.maximum(x, 0.0) + jnp.log(1.0 + jnp.exp(-jnp.abs(x)))


def mm_nt(name, a, b, out_dtype, tm=512, tn=512):
    M, K = a.shape
    N = b.shape[0]
    tm, tn = min(tm, M), min(tn, N)
    assert M % tm == 0 and N % tn == 0, (M, N, tm, tn)

    def body(a_ref, b_ref, o_ref):
        o_ref[...] = _dot_nt(a_ref[...], b_ref[...]).astype(o_ref.dtype)

    return pl.pallas_call(
        body, name=name, out_shape=jax.ShapeDtypeStruct((M, N), out_dtype), grid=(M // tm, N // tn),
        in_specs=[pl.BlockSpec((tm, K), lambda i, j: (i, 0)), pl.BlockSpec((tn, K), lambda i, j: (j, 0))],
        out_specs=pl.BlockSpec((tm, tn), lambda i, j: (i, j)),
        compiler_params=_cp(("parallel", "arbitrary")))(a, b)


def mm_tn(name, a, b, init=None, tm=512, tn=1024, tk=512):
    T, M = a.shape
    N = b.shape[1]
    tm, tn, tk = min(tm, M), min(tn, N), min(tk, T)
    assert M % tm == 0 and N % tn == 0 and T % tk == 0, (M, N, T)
    has_init = init is not None

    def body(*refs):
        if has_init:
            a_ref, b_ref, i_ref, o_ref = refs
        else:
            a_ref, b_ref, o_ref = refs
        k = pl.program_id(2)

        @pl.when(k == 0)
        def _():
            o_ref[...] = i_ref[...] if has_init else jnp.zeros(o_ref.shape, F32)

        o_ref[...] += _dot_tn(a_ref[...], b_ref[...])

    in_specs = [pl.BlockSpec((tk, tm), lambda i, j, k: (k, i)), pl.BlockSpec((tk, tn), lambda i, j, k: (k, j))]
    args = [a, b]
    if has_init:
        in_specs.append(pl.BlockSpec((tm, tn), lambda i, j, k: (i, j)))
        args.append(init)
    return pl.pallas_call(
        body, name=name, out_shape=jax.ShapeDtypeStruct((M, N), F32), grid=(M // tm, N // tn, T // tk),
        in_specs=in_specs, out_specs=pl.BlockSpec((tm, tn), lambda i, j, k: (i, j)),
        compiler_params=_cp(("parallel", "parallel", "arbitrary")))(*args)


def mm_nn_multi(name, pairs, out_dtype, tm=512, tk=512):
    M = pairs[0][0].shape[0]
    N = pairs[0][1].shape[1]
    tm = min(tm, M)
    assert M % tm == 0
    plan = []
    step = 0
    for a, b in pairs:
        K = a.shape[1]
        t = min(tk, K)
        assert K % t == 0 and b.shape == (K, N)
        plan.append((t, step, K // t))
        step += K // t
    nsteps = step
    npairs = len(pairs)

    def body(*refs):
        o_ref, acc = refs[2 * npairs], refs[2 * npairs + 1]
        k = pl.program_id(1)

        @pl.when(k == 0)
        def _():
            acc[...] = jnp.zeros(acc.shape, F32)

        for p, (_, first, n) in enumerate(plan):
            @pl.when((k >= first) & (k < first + n))
            def _(p=p):
                acc[...] += _dot(refs[2 * p][...], refs[2 * p + 1][...])

        @pl.when(k == nsteps - 1)
        def _():
            o_ref[...] = acc[...].astype(o_ref.dtype)

    in_specs, args = [], []
    for (a, b), (t, first, n) in zip(pairs, plan):
        in_specs.append(pl.BlockSpec((tm, t), lambda i, k, first=first, n=n: (i, jnp.clip(k - first, 0, n - 1))))
        in_specs.append(pl.BlockSpec((t, N), lambda i, k, first=first, n=n: (jnp.clip(k - first, 0, n - 1), 0)))
        args += [a, b]
    return pl.pallas_call(
        body, name=name, out_shape=jax.ShapeDtypeStruct((M, N), out_dtype), grid=(M // tm, nsteps),
        in_specs=in_specs, out_specs=pl.BlockSpec((tm, N), lambda i, k: (i, 0)),
        scratch_shapes=[pltpu.VMEM((tm, N), F32)],
        compiler_params=_cp(("parallel", "arbitrary")))(*args)


def tok_call(name, body, tiled, perb, glob, out_tiled, out_perb, out_glob, tm=256):
    widths = [t[1] if isinstance(t, tuple) else t.shape[2] for t in tiled]
    tiled = [t[0] if isinstance(t, tuple) else t for t in tiled]
    Bn, L = tiled[0].shape[:2]
    tm = min(tm, L)
    assert L % tm == 0
    n_t, n_p, n_g = len(tiled), len(perb), len(glob)
    o_t, o_p, o_g = len(out_tiled), len(out_perb), len(out_glob)
    n_in = n_t + n_p + n_g

    def kern(*refs):
        ins, outs = refs[:n_in], refs[n_in:]
        b, j = pl.program_id(0), pl.program_id(1)
        vals = [r[0] for r in ins[:n_t + n_p]] + [r[...] for r in ins[n_t + n_p:]]
        res = body(*vals)
        if not isinstance(res, (tuple, list)):
            res = (res,)
        assert len(res) == o_t + o_p + o_g, (name, len(res))
        for r, v in zip(outs[:o_t], res[:o_t]):
            r[0] = v.astype(r.dtype)

        def accum(r, v, first, lead):
            @pl.when(first)
            def _():
                r[...] = jnp.zeros(r.shape, F32)
            if lead:
                r[0] += v
            else:
                r[...] += v

        for r, v in zip(outs[o_t:o_t + o_p], res[o_t:o_t + o_p]):
            accum(r, v, j == 0, True)
        for r, v in zip(outs[o_t + o_p:], res[o_t + o_p:]):
            accum(r, v, (j == 0) & (b == 0), False)

    in_specs = ([pl.BlockSpec((1, tm, w), lambda b, j: (b, j, 0)) for w in widths]
                + [pl.BlockSpec((1, 1, a.shape[2]), lambda b, j: (b, 0, 0)) for a in perb]
                + [pl.BlockSpec(a.shape, lambda b, j: (0, 0)) for a in glob])
    out_shape = ([jax.ShapeDtypeStruct((Bn, L, w), dt) for w, dt in out_tiled]
                 + [jax.ShapeDtypeStruct((Bn, 1, w), F32) for w in out_perb]
                 + [jax.ShapeDtypeStruct(s, F32) for s in out_glob])
    out_specs = ([pl.BlockSpec((1, tm, w), lambda b, j: (b, j, 0)) for w, _ in out_tiled]
                 + [pl.BlockSpec((1, 1, w), lambda b, j: (b, 0, 0)) for w in out_perb]
                 + [pl.BlockSpec(s, lambda b, j: (0, 0)) for s in out_glob])
    return pl.pallas_call(
        kern, name=name, out_shape=out_shape, grid=(Bn, L // tm), in_specs=in_specs, out_specs=out_specs,
        compiler_params=_cp(("arbitrary", "arbitrary")))(*tiled, *perb, *glob)


def slab_call(name, body, slabs, colparams, out_slabs, out_colred, wc=LANES):
    Bn, L = slabs[0][0].shape[:2]
    w_out = out_slabs[0][0]
    assert w_out % wc == 0 and all(off % wc == 0 for _, off in slabs + colparams)
    n_col = w_out // wc
    n_s, n_c = len(slabs), len(colparams)
    o_s = len(out_slabs)

    def kern(*refs):
        ins, outs = refs[:n_s + n_c], refs[n_s + n_c:]
        b = pl.program_id(1)
        vals = [r[0] for r in ins[:n_s]] + [r[...] for r in ins[n_s:]]
        res = body(*vals)
        if not isinstance(res, (tuple, list)):
            res = (res,)
        assert len(res) == o_s + len(out_colred), name
        for r, v in zip(outs[:o_s], res[:o_s]):
            r[0] = v.astype(r.dtype)

        def accum(r, v):
            @pl.when(b == 0)
            def _():
                r[...] = jnp.zeros(r.shape, F32)
            r[...] += v

        for r, v in zip(outs[o_s:], res[o_s:]):
            accum(r, v)

    in_specs = ([pl.BlockSpec((1, L, wc), lambda j, b, o=off // wc: (b, 0, o + j)) for _, off in slabs]
                + [pl.BlockSpec((a.shape[0], wc), lambda j, b, o=off // wc: (0, o + j)) for a, off in colparams])
    out_shape = ([jax.ShapeDtypeStruct((Bn, L, w), dt) for w, dt in out_slabs]
                 + [jax.ShapeDtypeStruct((r, w_out), F32) for r in out_colred])
    out_specs = ([pl.BlockSpec((1, L, wc), lambda j, b: (b, 0, j)) for _ in out_slabs]
                 + [pl.BlockSpec((r, wc), lambda j, b: (0, j)) for r in out_colred])
    return pl.pallas_call(
        kern, name=name, out_shape=out_shape, grid=(n_col, Bn), in_specs=in_specs, out_specs=out_specs,
        compiler_params=_cp(("arbitrary", "arbitrary")))(*[a for a, _ in slabs], *[a for a, _ in colparams])


def _rms_r(x):
    return lax.rsqrt(jnp.mean(x * x, axis=-1, keepdims=True) + NORM_EPS)


def _rms_bwd(dxh, x, r):
    return r * (dxh - x * (r * r) * jnp.mean(dxh * x, axis=-1, keepdims=True))


def _colsum(v):
    return jnp.sum(v, axis=0, keepdims=True)


def _stack_rows(rows):
    n, w = len(rows), rows[0].shape[1]
    sub = lax.broadcasted_iota(jnp.int32, (n, w), 0)
    acc = jnp.zeros((n, w), F32)
    for r, row in enumerate(rows):
        acc = acc + jnp.where(sub == r, jnp.broadcast_to(row, (n, w)), 0.0)
    return acc


def prenorm_fwd(name, x, scale, shift, w_pre):
    def body(x, scale, shift, w):
        n = x * _rms_r(x) * w
        return n * (1.0 + scale) + shift

    return tok_call(name, body, [x], [scale, shift], [w_pre], [(D, BF16)], [], [])[0]


def prenorm_bwd(name, x, dhx, scale, w_pre, g_res=None):
    has_res = g_res is not None

    def body(*v):
        if has_res:
            x, dhx, g, scale, w = v
        else:
            x, dhx, scale, w = v
        r = _rms_r(x)
        xr = x * r
        n = xr * w
        dn = dhx * (1.0 + scale)
        dx = _rms_bwd(dn * w, x, r)
        if has_res:
            dx = dx + g
        return dx, _colsum(dhx * n), _colsum(dhx), _colsum(dn * xr)

    tiled = [x, dhx] + ([g_res] if has_res else [])
    return tok_call(name, body, tiled, [scale], [w_pre], [(D, F32)], [D, D], [(1, D)])


def _shift_rows(x, o, tok, L):
    if o == 0:
        return x
    rolled = pltpu.roll(x, (-o) % L, 0)
    return jnp.where((tok + o >= 0) & (tok + o < L), rolled, 0.0)


def conv_fwd(name, xbc_raw, conv_w, conv_b):
    L = xbc_raw.shape[1]

    def body(x, w, b):
        tok = lax.broadcasted_iota(jnp.int32, x.shape, 0)
        pre = b
        for k in range(4):
            pre = pre + _shift_rows(x, k - 2, tok, L) * w[k:k + 1]
        return _silu(pre)

    return slab_call(name, body, [(xbc_raw, 0)], [(conv_w, 0), (conv_b, 0)], [(CONV_DIM, F32)], [])[0]


def conv_bwd(name, xbc_raw, dparts, conv_w, conv_b, col0, width):
    L = xbc_raw.shape[1]
    n_d = len(dparts)

    def body(*v):
        x, ds, w, b = v[0], v[1:1 + n_d], v[1 + n_d], v[2 + n_d]
        tok = lax.broadcasted_iota(jnp.int32, x.shape, 0)
        taps = [_shift_rows(x, k - 2, tok, L) for k in range(4)]
        pre = b
        for k in range(4):
            pre = pre + taps[k] * w[k:k + 1]
        dy = ds[0]
        for extra in ds[1:]:
            dy = dy + extra
        dpre = dy * _dsilu(pre)
        dx = jnp.zeros_like(x)
        for k in range(4):
            dx = dx + _shift_rows(dpre, 2 - k, tok, L) * w[k:k + 1]
        dw = _stack_rows([_colsum(dpre * taps[k]) for k in range(4)])
        return dx, dw, _colsum(dpre)

    return slab_call(name, body, [(xbc_raw, col0)] + [(d, 0) for d in dparts], [(conv_w, col0), (conv_b, col0)],
                     [(width, BF16)], [4, 1])


def _box_mean(x, k, step, pos, n, L, transpose):
    lo, hi = k // 2, k - 1 - k // 2
    cnt = (jnp.minimum(pos + hi + 1, n) - jnp.maximum(pos - lo, 0)).astype(F32)
    if transpose:
        x = x / cnt
        lo, hi = hi, lo
    acc = x
    for o in range(-lo, hi + 1):
        if o == 0:
            continue
        rolled = pltpu.roll(x, (-o * step) % L, 0)
        acc = acc + jnp.where((pos + o >= 0) & (pos + o < n), rolled, 0.0)
    return acc if transpose else acc / cnt


def pool_diff(name, v, col0, gi, transpose):
    L = v.shape[1]
    rows = L // GRID_W
    k = POOL_WINDOWS[gi]

    def body(x):
        tok = lax.broadcasted_iota(jnp.int32, x.shape, 0)
        col = tok & (GRID_W - 1)
        row = tok >> 6
        if not transpose:
            m = _box_mean(x, k, GRID_W, row, rows, L, False)
            m = _box_mean(m, k, 1, col, GRID_W, L, False)
        else:
            m = _box_mean(x, k, 1, col, GRID_W, L, True)
            m = _box_mean(m, k, GRID_W, row, rows, L, True)
        return m - x

    return slab_call(name, body, [(v, col0)], [], [(POOL_GROUP, BF16)], [])[0]


def pool_mix_fwd(name, dgs, z_pool, pool_w, pool_scale):
    def body(d0, d1, d2, d3, z, w, scale):
        q = jnp.concatenate([_dot(d, w[g * POOL_GROUP:(g + 1) * POOL_GROUP]) for g, d in enumerate((d0, d1, d2, d3))], axis=1)
        return q * scale * _silu(z)

    return tok_call(name, body, list(dgs) + [z_pool], [], [pool_w, pool_scale], [(D, BF16)], [], [])[0]


def pool_mix_bwd(name, dgs, z_pool, dyp, pool_w, pool_scale):
    def body(d0, d1, d2, d3, z, dyp, w, scale):
        ds = (d0, d1, d2, d3)
        q = jnp.concatenate([_dot(d, w[g * POOL_GROUP:(g + 1) * POOL_GROUP]) for g, d in enumerate(ds)], axis=1)
        dypm = dyp * _silu(z)
        dz = dyp * (q * scale) * _dsilu(z)
        dq = (dypm * scale).astype(BF16)
        dds, gws = [], []
        for g, d in enumerate(ds):
            dqg = dq[:, g * POOL_GROUP:(g + 1) * POOL_GROUP]
            dds.append(_dot_nt(dqg, w[g * POOL_GROUP:(g + 1) * POOL_GROUP]))
            gws.append(_dot_tn(d, dqg))
        return (*dds, dz, jnp.concatenate(gws, axis=0), _colsum(dypm * q))

    return tok_call(name, body, list(dgs) + [z_pool, dyp], [], [pool_w, pool_scale],
                    [(POOL_GROUP, F32)] * 4 + [(D, BF16)], [], [(D, POOL_GROUP), (1, D)])


def _cumsum_lanes(a, reverse):
    n = a.shape[1]
    lane = lax.broadcasted_iota(jnp.int32, a.shape, 1)
    s = 1
    while s < n:
        if reverse:
            a = a + jnp.where(lane < n - s, pltpu.roll(a, n - s, 1), 0.0)
        else:
            a = a + jnp.where(lane >= s, pltpu.roll(a, s, 1), 0.0)
        s *= 2
    return a


def _rows_to_cols(rows):
    r = rows.shape[0]
    if r < LANES:
        rows = jnp.concatenate([rows, jnp.zeros((LANES - r, rows.shape[1]), F32)], axis=0)
    return rows.T


def _cols_to_rows(cols):
    q = cols[0].shape[0]
    lane = lax.broadcasted_iota(jnp.int32, (q, LANES), 1)
    acc = jnp.zeros((q, LANES), F32)
    for r, c in enumerate(cols):
        acc = acc + jnp.where(lane == r, c, 0.0)
    return acc.T[0:len(cols)]


def _ssd_scalars(dtraw, bias, alog, reverse):
    dt = _softplus(dtraw + bias)
    A = -jnp.exp(alog)
    cs = _cumsum_lanes(dt * A, reverse)
    total = cs[:, 0:1] if reverse else cs[:, CHUNK - 1:CHUNK]
    return dt, A, cs, total


def _decay_matrix(cs_col, cs_row, reverse):
    i = lax.broadcasted_iota(jnp.int32, (CHUNK, CHUNK), 0)
    j = lax.broadcasted_iota(jnp.int32, (CHUNK, CHUNK), 1)
    keep = (i <= j) if reverse else (i >= j)
    return jnp.exp(jnp.where(keep, cs_col - cs_row, -jnp.inf))


def ssd_fwd_v1(name, dtT, bias, alog, xbc, h0, direction, with_y):
    Bn, L = xbc.shape[:2]
    nc = L // CHUNK
    reverse = direction == 1
    rowblk = direction * N_BC

    def chunk_of(s):
        return (nc - 1 - s) if reverse else s

    def kern(dt_ref, bias_ref, alog_ref, x_ref, b_ref, c_ref, h0_ref, *rest):
        if with_y:
            y_ref, hs_ref, hf_ref, h_scr, xt_scr = rest
        else:
            hs_ref, hf_ref, h_scr, xt_scr = rest
        s = pl.program_id(2)

        @pl.when(s == 0)
        def _():
            h_scr[...] = h0_ref[0, 0]

        dt, _, cs, total = _ssd_scalars(dt_ref[0], bias_ref[0], alog_ref[0], reverse)
        e_row = jnp.exp(cs)
        t_row = jnp.exp(total - cs)
        dc = jnp.exp(total)
        cols = _rows_to_cols(jnp.concatenate([dt, e_row, t_row, cs], axis=0))
        x = x_ref[0]
        bm = b_ref[0].astype(BF16)
        cm = c_ref[0].astype(BF16)
        h = h_scr[...]
        hs_ref[0, 0, 0] = h
        if with_y:
            cb = _dot_nt(cm, bm)
            yoff = _dot(cm, h.astype(BF16))
        for r in range(HPG):
            sl = slice(r * HEAD_DIM, (r + 1) * HEAD_DIM)
            xdt = x[:, sl] * cols[:, r:r + 1]
            if with_y:
                lr = _decay_matrix(cols[:, 3 * HPG + r:3 * HPG + r + 1], cs[r:r + 1], reverse)
                ydiag = _dot((cb * lr).astype(BF16), xdt.astype(BF16))
                y_ref[0, :, sl] = ydiag + yoff[:, sl] * cols[:, HPG + r:HPG + r + 1]
            xt_scr[:, sl] = (xdt * cols[:, 2 * HPG + r:2 * HPG + r + 1]).astype(BF16)
        st = _dot_tn(bm, xt_scr[...])
        for r in range(HPG):
            sl = slice(r * HEAD_DIM, (r + 1) * HEAD_DIM)
            h_scr[:, sl] = h[:, sl] * dc[r:r + 1] + st[:, sl]

        @pl.when(s == nc - 1)
        def _():
            hf_ref[0, 0] = h_scr[...]

    in_specs = [
        pl.BlockSpec((1, HPG, CHUNK), lambda b, g, s: (b, rowblk + g, chunk_of(s))),
        pl.BlockSpec((1, HPG, 1), lambda b, g, s: (rowblk + g, 0, 0)),
        pl.BlockSpec((1, HPG, 1), lambda b, g, s: (rowblk + g, 0, 0)),
        pl.BlockSpec((1, CHUNK, GW), lambda b, g, s: (b, chunk_of(s), g)),
        pl.BlockSpec((1, CHUNK, D_STATE), lambda b, g, s: (b, chunk_of(s), D_INNER // D_STATE + g)),
        pl.BlockSpec((1, CHUNK, D_STATE), lambda b, g, s: (b, chunk_of(s), D_INNER // D_STATE + N_BC + g)),
        pl.BlockSpec((1, 1, D_STATE, GW), lambda b, g, s: (b, g, 0, 0)),
    ]
    out_shape, out_specs = [], []
    if with_y:
        out_shape.append(jax.ShapeDtypeStruct((Bn, L, D_INNER), F32))
        out_specs.append(pl.BlockSpec((1, CHUNK, GW), lambda b, g, s: (b, chunk_of(s), g)))
    out_shape += [jax.ShapeDtypeStruct((Bn, N_BC, nc, D_STATE, GW), F32), jax.ShapeDtypeStruct((Bn, N_BC, D_STATE, GW), F32)]
    out_specs += [pl.BlockSpec((1, 1, 1, D_STATE, GW), lambda b, g, s: (b, g, chunk_of(s), 0, 0)),
                  pl.BlockSpec((1, 1, D_STATE, GW), lambda b, g, s: (b, g, 0, 0))]
    return pl.pallas_call(
        kern, name=name, out_shape=out_shape, grid=(Bn, N_BC, nc), in_specs=in_specs, out_specs=out_specs,
        scratch_shapes=[pltpu.VMEM((D_STATE, GW), F32), pltpu.VMEM((CHUNK, GW), BF16)],
        compiler_params=_cp(("arbitrary", "arbitrary", "arbitrary")))(dtT, bias, alog, xbc, xbc, xbc, h0)


def ssd_bwd_v1(name, dtT, bias, alog, xbc, h_start, dy, dh_final, direction):
    Bn, L = xbc.shape[:2]
    nc = L // CHUNK
    reverse = direction == 1
    rowblk = direction * N_BC
    has_y = dy is not None
    last = 0 if reverse else CHUNK - 1

    def chunk_of(s):
        return s if reverse else (nc - 1 - s)

    def kern(*refs):
        if has_y:
            (dt_ref, bias_ref, alog_ref, x_ref, b_ref, c_ref, hs_ref, dhf_ref, dy_ref,
             dx_ref, db_ref, dc_ref, ddt_ref, dbias_ref, dalog_ref, dh0_ref, dh_scr, e_scr, t_scr) = refs
        else:
            (dt_ref, bias_ref, alog_ref, x_ref, b_ref, hs_ref, dhf_ref,
             dx_ref, db_ref, ddt_ref, dbias_ref, dalog_ref, dh0_ref, dh_scr, t_scr) = refs
        s = pl.program_id(2)

        @pl.when(s == 0)
        def _():
            dh_scr[...] = dhf_ref[0, 0]
            dbias_ref[...] = jnp.zeros(dbias_ref.shape, F32)
            dalog_ref[...] = jnp.zeros(dalog_ref.shape, F32)

        dtraw = dt_ref[0]
        dt, A, cs, total = _ssd_scalars(dtraw, bias_ref[0], alog_ref[0], reverse)
        e_row = jnp.exp(cs)
        t_row = jnp.exp(total - cs)
        dcy = jnp.exp(total)
        cols = _rows_to_cols(jnp.concatenate([dt, e_row, t_row, cs], axis=0))
        x = x_ref[0]
        bm = b_ref[0].astype(BF16)
        h = hs_ref[0, 0, 0]
        dh = dh_scr[...]
        dh_bf = dh.astype(BF16)
        bdh = _dot(bm, dh_bf)
        if has_y:
            cm = c_ref[0].astype(BF16)
            dyv = dy_ref[0]
            cb = _dot_nt(cm, bm)
            yoff = _dot(cm, h.astype(BF16))
            dcb = jnp.zeros((CHUNK, CHUNK), F32)
        col_terms, row_terms, ddt_cols, dtot = [], [], [], []
        for r in range(HPG):
            sl = slice(r * HEAD_DIM, (r + 1) * HEAD_DIM)
            dt_c = cols[:, r:r + 1]
            e_c = cols[:, HPG + r:HPG + r + 1]
            t_c = cols[:, 2 * HPG + r:2 * HPG + r + 1]
            xr = x[:, sl]
            xdt = xr * dt_c
            dxdt = t_c * bdh[:, sl]
            d_t = jnp.sum(bdh[:, sl] * xdt, axis=1, keepdims=True)
            col = -(t_c * d_t)
            tot = jnp.sum(t_c * d_t, axis=0, keepdims=True) + dcy[r:r + 1] * jnp.sum(h[:, sl] * dh[:, sl], keepdims=True)
            if has_y:
                dyr = dyv[:, sl]
                lr = _decay_matrix(cols[:, 3 * HPG + r:3 * HPG + r + 1], cs[r:r + 1], reverse)
                w = cb * lr
                gm = _dot_nt(dyr.astype(BF16), xdt.astype(BF16))
                m = gm * w
                dcb = dcb + gm * lr
                dxdt = dxdt + _dot_tn(w.astype(BF16), dyr.astype(BF16))
                col = col + jnp.sum(m, axis=1, keepdims=True) + jnp.sum(yoff[:, sl] * dyr, axis=1, keepdims=True) * e_c
                row_terms.append(-jnp.sum(m, axis=0, keepdims=True))
                e_scr[:, sl] = (e_c * dyr).astype(BF16)
            t_scr[:, sl] = (t_c * xdt).astype(BF16)
            dx_ref[0, :, sl] = dxdt * dt_c
            ddt_cols.append(jnp.sum(dxdt * xr, axis=1, keepdims=True))
            col_terms.append(col)
            dtot.append(tot)
        db = _dot_nt(t_scr[...], dh_bf)
        if has_y:
            dcb_bf = dcb.astype(BF16)
            db = db + _dot_tn(dcb_bf, cm)
            dc_ref[0] = _dot(dcb_bf, bm) + _dot_nt(e_scr[...], h.astype(BF16))
            cte = _dot_tn(cm, e_scr[...])
        db_ref[0] = db
        for r in range(HPG):
            sl = slice(r * HEAD_DIM, (r + 1) * HEAD_DIM)
            new = dh[:, sl] * dcy[r:r + 1]
            if has_y:
                new = new + cte[:, sl]
            dh_scr[:, sl] = new
        dcs = _cols_to_rows(col_terms)
        if has_y:
            dcs = dcs + _stack_rows(row_terms)
        lane = lax.broadcasted_iota(jnp.int32, (HPG, CHUNK), 1)
        dcs = dcs + jnp.where(lane == last, _stack_rows([jnp.broadcast_to(t, (1, CHUNK)) for t in dtot]), 0.0)
        da = _cumsum_lanes(dcs, not reverse)
        ddt = da * A + _cols_to_rows(ddt_cols)
        ddtraw = ddt * _sigmoid(dtraw + bias_ref[0])
        ddt_ref[0] = ddtraw
        dbias_ref[0, 0] += jnp.sum(ddtraw, axis=1, keepdims=True)
        dalog_ref[0, 0] += jnp.sum(da * dt, axis=1, keepdims=True) * A

        @pl.when(s == nc - 1)
        def _():
            dh0_ref[0, 0] = dh_scr[...]

    cidx = lambda b, g, s: (b, chunk_of(s), g)
    in_specs = [
        pl.BlockSpec((1, HPG, CHUNK), lambda b, g, s: (b, rowblk + g, chunk_of(s))),
        pl.BlockSpec((1, HPG, 1), lambda b, g, s: (rowblk + g, 0, 0)),
        pl.BlockSpec((1, HPG, 1), lambda b, g, s: (rowblk + g, 0, 0)),
        pl.BlockSpec((1, CHUNK, GW), cidx),
        pl.BlockSpec((1, CHUNK, D_STATE), lambda b, g, s: (b, chunk_of(s), D_INNER // D_STATE + g)),
    ]
    args = [dtT, bias, alog, xbc, xbc]
    if has_y:
        in_specs.append(pl.BlockSpec((1, CHUNK, D_STATE), lambda b, g, s: (b, chunk_of(s), D_INNER // D_STATE + N_BC + g)))
        args.append(xbc)
    in_specs += [pl.BlockSpec((1, 1, 1, D_STATE, GW), lambda b, g, s: (b, g, chunk_of(s), 0, 0)),
                 pl.BlockSpec((1, 1, D_STATE, GW), lambda b, g, s: (b, g, 0, 0))]
    args += [h_start, dh_final]
    if has_y:
        in_specs.append(pl.BlockSpec((1, CHUNK, GW), cidx))
        args.append(dy)
    out_shape = [jax.ShapeDtypeStruct((Bn, L, D_INNER), F32), jax.ShapeDtypeStruct((Bn, L, N_BC * D_STATE), F32)]
    out_specs = [pl.BlockSpec((1, CHUNK, GW), cidx), pl.BlockSpec((1, CHUNK, D_STATE), cidx)]
    if has_y:
        out_shape.append(jax.ShapeDtypeStruct((Bn, L, N_BC * D_STATE), F32))
        out_specs.append(pl.BlockSpec((1, CHUNK, D_STATE), cidx))
    out_shape += [jax.ShapeDtypeStruct((Bn, N_HEADS, L), F32), jax.ShapeDtypeStruct((Bn, N_BC, HPG, 1), F32),
                  jax.ShapeDtypeStruct((Bn, N_BC, HPG, 1), F32), jax.ShapeDtypeStruct((Bn, N_BC, D_STATE, GW), F32)]
    out_specs += [pl.BlockSpec((1, HPG, CHUNK), lambda b, g, s: (b, g, chunk_of(s))),
                  pl.BlockSpec((1, 1, HPG, 1), lambda b, g, s: (b, g, 0, 0)),
                  pl.BlockSpec((1, 1, HPG, 1), lambda b, g, s: (b, g, 0, 0)),
                  pl.BlockSpec((1, 1, D_STATE, GW), lambda b, g, s: (b, g, 0, 0))]
    scratch = [pltpu.VMEM((D_STATE, GW), F32)] + ([pltpu.VMEM((CHUNK, GW), BF16)] if has_y else []) + [pltpu.VMEM((CHUNK, GW), BF16)]
    res = pl.pallas_call(
        kern, name=name, out_shape=out_shape, grid=(Bn, N_BC, nc), in_specs=in_specs, out_specs=out_specs,
        scratch_shapes=scratch, compiler_params=_cp(("arbitrary", "arbitrary", "arbitrary")))(*args)
    if has_y:
        return res
    dxs, db, ddt, dbias, dalog, dh0 = res
    return dxs, db, None, ddt, dbias, dalog, dh0


def _tri_mask(transposed, reverse):
    sub = lax.broadcasted_iota(jnp.int32, (CHUNK, CHUNK), 0)
    lane = lax.broadcasted_iota(jnp.int32, (CHUNK, CHUNK), 1)
    i, j = (lane, sub) if transposed else (sub, lane)
    return (i <= j) if reverse else (i >= j)


def ssd_fwd(name, dtT, bias, alog, xbc, h0, direction, with_y):
    Bn, L = xbc.shape[:2]
    nc = L // CHUNK
    reverse = direction == 1
    rowblk = direction * N_BC

    def chunk_of(s):
        return (nc - 1 - s) if reverse else s

    def kern(dt_ref, bias_ref, alog_ref, x_ref, b_ref, c_ref, h0_ref, *rest):
        if with_y:
            y_ref, hs_ref, hf_ref, h_scr = rest
        else:
            hs_ref, hf_ref, h_scr = rest
        s = pl.program_id(2)

        @pl.when(s == 0)
        def _():
            h_scr[...] = h0_ref[0, 0]

        dt, _, cs, total = _ssd_scalars(dt_ref[0], bias_ref[0], alog_ref[0], reverse)
        u = cs - jnp.log(dt)
        dtt = jnp.exp(total - u)
        dc = jnp.exp(total)
        x_bf = x_ref[0].astype(BF16)
        bm = b_ref[0]
        h = h_scr[...]
        h_bf = h.astype(BF16)
        hs_ref[0, 0, 0] = h
        bt = bm.T
        if with_y:
            cm = c_ref[0]
            cb = _dot_nt(cm.astype(BF16), bm.astype(BF16))
            cs_cols = _rows_to_cols(cs)
            keep = _tri_mask(False, reverse)
        for r in range(HPG):
            sl = slice(r * HEAD_DIM, (r + 1) * HEAD_DIM)
            if with_y:
                cs_col = jnp.broadcast_to(cs_cols[:, r:r + 1], (CHUNK, LANES))
                wf = cb * jnp.exp(jnp.where(keep, cs_col - u[r:r + 1], -jnp.inf))
                lhs = jnp.concatenate([wf.astype(BF16), (cm * jnp.exp(cs_col)).astype(BF16)], axis=1)
                rhs = jnp.concatenate([x_bf[:, sl], h_bf[:, sl]], axis=0)
                y_ref[0, :, sl] = _dot(lhs, rhs)
            h_scr[:, sl] = h[:, sl] * dc[r:r + 1] + _dot((bt * dtt[r:r + 1]).astype(BF16), x_bf[:, sl])

        @pl.when(s == nc - 1)
        def _():
            hf_ref[0, 0] = h_scr[...]

    in_specs = [
        pl.BlockSpec((1, HPG, CHUNK), lambda b, g, s: (b, rowblk + g, chunk_of(s))),
        pl.BlockSpec((1, HPG, 1), lambda b, g, s: (rowblk + g, 0, 0)),
        pl.BlockSpec((1, HPG, 1), lambda b, g, s: (rowblk + g, 0, 0)),
        pl.BlockSpec((1, CHUNK, GW), lambda b, g, s: (b, chunk_of(s), g)),
        pl.BlockSpec((1, CHUNK, D_STATE), lambda b, g, s: (b, chunk_of(s), D_INNER // D_STATE + g)),
        pl.BlockSpec((1, CHUNK, D_STATE), lambda b, g, s: (b, chunk_of(s), D_INNER // D_STATE + N_BC + g)),
        pl.BlockSpec((1, 1, D_STATE, GW), lambda b, g, s: (b, g, 0, 0)),
    ]
    out_shape, out_specs = [], []
    if with_y:
        out_shape.append(jax.ShapeDtypeStruct((Bn, L, D_INNER), F32))
        out_specs.append(pl.BlockSpec((1, CHUNK, GW), lambda b, g, s: (b, chunk_of(s), g)))
    out_shape += [jax.ShapeDtypeStruct((Bn, N_BC, nc, D_STATE, GW), F32), jax.ShapeDtypeStruct((Bn, N_BC, D_STATE, GW), F32)]
    out_specs += [pl.BlockSpec((1, 1, 1, D_STATE, GW), lambda b, g, s: (b, g, chunk_of(s), 0, 0)),
                  pl.BlockSpec((1, 1, D_STATE, GW), lambda b, g, s: (b, g, 0, 0))]
    return pl.pallas_call(
        kern, name=name, out_shape=out_shape, grid=(Bn, N_BC, nc), in_specs=in_specs, out_specs=out_specs,
        scratch_shapes=[pltpu.VMEM((D_STATE, GW), F32)],
        compiler_params=_cp(("arbitrary", "arbitrary", "arbitrary")))(dtT, bias, alog, xbc, xbc, xbc, h0)


def ssd_bwd(name, dtT, bias, alog, xbc, h_start, dy, dh_final, direction):
    Bn, L = xbc.shape[:2]
    nc = L // CHUNK
    reverse = direction == 1
    rowblk = direction * N_BC
    has_y = dy is not None
    last = 0 if reverse else CHUNK - 1

    def chunk_of(s):
        return s if reverse else (nc - 1 - s)

    def kern(*refs):
        if has_y:
            (dt_ref, bias_ref, alog_ref, x_ref, b_ref, hs_ref, dhf_ref, c_ref, dy_ref,
             dx_ref, db_ref, ddt_ref, dbias_ref, dalog_ref, dh0_ref, dc_ref, dh_scr) = refs
        else:
            (dt_ref, bias_ref, alog_ref, x_ref, b_ref, hs_ref, dhf_ref,
             dx_ref, db_ref, ddt_ref, dbias_ref, dalog_ref, dh0_ref, dh_scr) = refs
        s = pl.program_id(2)

        @pl.when(s == 0)
        def _():
            dh_scr[...] = dhf_ref[0, 0]
            dbias_ref[...] = jnp.zeros(dbias_ref.shape, F32)
            dalog_ref[...] = jnp.zeros(dalog_ref.shape, F32)

        dtraw = dt_ref[0]
        dt, A, cs, total = _ssd_scalars(dtraw, bias_ref[0], alog_ref[0], reverse)
        u = cs - jnp.log(dt)
        dtt = jnp.exp(total - u)
        dcy = jnp.exp(total)
        u_cols = _rows_to_cols(u)
        x_bf = x_ref[0].astype(BF16)
        bm = b_ref[0]
        bt = bm.T
        h = hs_ref[0, 0, 0]
        dh = dh_scr[...]
        dh_bf = dh.astype(BF16)
        dbt = jnp.zeros((D_STATE, CHUNK), F32)
        if has_y:
            cm = c_ref[0]
            ct = cm.T
            e_row = jnp.exp(cs)
            dy_bf = dy_ref[0].astype(BF16)
            h_bf = h.astype(BF16)
            cbt = _dot_nt(bm.astype(BF16), cm.astype(BF16))
            keep = _tri_mask(True, reverse)
            dcbt = jnp.zeros((CHUNK, CHUNK), F32)
            dct = jnp.zeros((D_STATE, CHUNK), F32)
        tots, out_rows, in_rows, in_cols = [], [], [], []
        for r in range(HPG):
            sl = slice(r * HEAD_DIM, (r + 1) * HEAD_DIM)
            u_col = jnp.broadcast_to(u_cols[:, r:r + 1], (CHUNK, LANES))
            bs = (bm * jnp.exp(total[r:r + 1] - u_col)).astype(BF16)
            if has_y:
                et = jnp.exp(jnp.where(keep, cs[r:r + 1] - u_col, -jnp.inf))
                a1 = _dot_nt(jnp.concatenate([x_bf[:, sl], h_bf[:, sl]], axis=0), dy_bf[:, sl])
                pt = a1[0:CHUNK] * et
                dcbt = dcbt + pt
                wt = cbt * et
                mt = pt * cbt
                ph = a1[CHUNK:] * e_row[r:r + 1]
                dct = dct + ph
                out_rows.append(_colsum(mt) + _colsum(ct * ph))
                in_cols.append(jnp.sum(mt, axis=1, keepdims=True))
                lhs = jnp.concatenate([wt.astype(BF16), bs], axis=1)
                rhs = jnp.concatenate([dy_bf[:, sl], dh_bf[:, sl]], axis=0)
                dx_ref[0, :, sl] = _dot(lhs, rhs)
            else:
                dx_ref[0, :, sl] = _dot(bs, dh_bf[:, sl])
            p2 = _dot_nt(dh_bf[:, sl], x_bf[:, sl]) * dtt[r:r + 1]
            dbt = dbt + p2
            t_term = _colsum(bt * p2)
            in_rows.append(t_term)
            new = dh[:, sl] * dcy[r:r + 1]
            if has_y:
                new = new + _dot((ct * e_row[r:r + 1]).astype(BF16), dy_bf[:, sl])
            dh_scr[:, sl] = new
            tot = jnp.sum(t_term, axis=1, keepdims=True) + dcy[r:r + 1] * jnp.sum(h[:, sl] * dh[:, sl], keepdims=True)
            tots.append(jnp.broadcast_to(tot, (1, CHUNK)))
        db = dbt.T
        if has_y:
            dcbt_bf = dcbt.astype(BF16)
            db = db + _dot(dcbt_bf, cm.astype(BF16))
            dc_ref[0] = dct.T + _dot_tn(dcbt_bf, bm.astype(BF16))
        db_ref[0] = db
        s_row = _stack_rows(in_rows)
        lane = lax.broadcasted_iota(jnp.int32, (HPG, CHUNK), 1)
        dcs = jnp.where(lane == last, _stack_rows(tots), 0.0)
        if has_y:
            s_row = s_row + _cols_to_rows(in_cols)
            dcs = dcs + _stack_rows(out_rows)
        dcs = dcs - s_row
        da = _cumsum_lanes(dcs, not reverse)
        ddt = da * A + jnp.where(dt > 0.0, s_row / dt, 0.0)
        ddtraw = ddt * _sigmoid(dtraw + bias_ref[0])
        ddt_ref[0] = ddtraw
        dbias_ref[0, 0] += jnp.sum(ddtraw, axis=1, keepdims=True)
        dalog_ref[0, 0] += jnp.sum(da * dt, axis=1, keepdims=True) * A

        @pl.when(s == nc - 1)
        def _():
            dh0_ref[0, 0] = dh_scr[...]

    cidx = lambda b, g, s: (b, chunk_of(s), g)
    hidx = lambda b, g, s: (b, g, 0, 0)
    in_specs = [
        pl.BlockSpec((1, HPG, CHUNK), lambda b, g, s: (b, rowblk + g, chunk_of(s))),
        pl.BlockSpec((1, HPG, 1), lambda b, g, s: (rowblk + g, 0, 0)),
        pl.BlockSpec((1, HPG, 1), lambda b, g, s: (rowblk + g, 0, 0)),
        pl.BlockSpec((1, CHUNK, GW), cidx),
        pl.BlockSpec((1, CHUNK, D_STATE), lambda b, g, s: (b, chunk_of(s), D_INNER // D_STATE + g)),
        pl.BlockSpec((1, 1, 1, D_STATE, GW), lambda b, g, s: (b, g, chunk_of(s), 0, 0)),
        pl.BlockSpec((1, 1, D_STATE, GW), hidx),
    ]
    args = [dtT, bias, alog, xbc, xbc, h_start, dh_final]
    if has_y:
        in_specs += [pl.BlockSpec((1, CHUNK, D_STATE), lambda b, g, s: (b, chunk_of(s), D_INNER // D_STATE + N_BC + g)),
                     pl.BlockSpec((1, CHUNK, GW), cidx)]
        args += [xbc, dy]
    out_shape = [jax.ShapeDtypeStruct((Bn, L, D_INNER), F32), jax.ShapeDtypeStruct((Bn, L, N_BC * D_STATE), F32),
                 jax.ShapeDtypeStruct((Bn, N_HEADS, L), F32), jax.ShapeDtypeStruct((Bn, N_BC, HPG, 1), F32),
                 jax.ShapeDtypeStruct((Bn, N_BC, HPG, 1), F32), jax.ShapeDtypeStruct((Bn, N_BC, D_STATE, GW), F32)]
    out_specs = [pl.BlockSpec((1, CHUNK, GW), cidx), pl.BlockSpec((1, CHUNK, D_STATE), cidx),
                 pl.BlockSpec((1, HPG, CHUNK), lambda b, g, s: (b, g, chunk_of(s))),
                 pl.BlockSpec((1, 1, HPG, 1), hidx), pl.BlockSpec((1, 1, HPG, 1), hidx), pl.BlockSpec((1, 1, D_STATE, GW), hidx)]
    if has_y:
        out_shape.append(jax.ShapeDtypeStruct((Bn, L, N_BC * D_STATE), F32))
        out_specs.append(pl.BlockSpec((1, CHUNK, D_STATE), cidx))
    res = pl.pallas_call(
        kern, name=name, out_shape=out_shape, grid=(Bn, N_BC, nc), in_specs=in_specs, out_specs=out_specs,
        scratch_shapes=[pltpu.VMEM((D_STATE, GW), F32)],
        compiler_params=_cp(("arbitrary", "arbitrary", "arbitrary")))(*args)
    dxs, db, ddt, dbias, dalog, dh0 = res[:6]
    return dxs, db, (res[6] if has_y else None), ddt, dbias, dalog, dh0


def _group_mean(v):
    gw = D_INNER // N_BC
    parts = [jnp.broadcast_to(jnp.mean(v[:, g * gw:(g + 1) * gw], axis=-1, keepdims=True), (v.shape[0], gw)) for g in range(N_BC)]
    return jnp.concatenate(parts, axis=1)


def gated_norm_fwd(name, y_f, y_b, xs_src, z, dskip_lanes, w_norm):
    def body(yf, yb, xs, z, dsk, w):
        u = (yf + yb + dsk * xs) * _silu(z)
        r = lax.rsqrt(_group_mean(u * u) + NORM_EPS)
        return u * r * w

    return tok_call(name, body, [y_f, y_b, xs_src, z], [], [dskip_lanes, w_norm], [(D_INNER, BF16)], [], [])[0]


def _dot_exact01(v, sel):
    hi, mid, lo = _split3(v)
    return _dot(hi, sel) + _dot(mid, sel) + _dot(lo, sel)


def gated_norm_bwd(name, y_f, y_b, xs_src, z, d_out, dskip_lanes, w_norm, head_sel):
    def body(yf, yb, xs, z, do, dsk, w, sel):
        y = yf + yb + dsk * xs
        sz = _silu(z)
        u = y * sz
        r = lax.rsqrt(_group_mean(u * u) + NORM_EPS)
        duh = do * w
        du = r * (duh - u * (r * r) * _group_mean(duh * u))
        dy = du * sz
        dz = du * y * _dsilu(z)
        dsk_heads = _dot_exact01(jnp.broadcast_to(_colsum(dy * xs), (8, D_INNER)), sel)
        return dy, dy * dsk, dz, _colsum(do * u * r), dsk_heads

    return tok_call(name, body, [y_f, y_b, xs_src, z, d_out], [], [dskip_lanes, w_norm, head_sel],
                    [(D_INNER, F32), (D_INNER, F32), (D_INNER, BF16)], [], [(1, D_INNER), (8, LANES)], tm=128)


def merge_fwd(name, y_pool, y_ssd, gatepre, x, target, gate, b_merge, norm_post, w_pp, w_ps, w_out):
    def body(yp, ys, gp, x, tgt, gate, bm, wpost, w_pp, w_ps, w_out):
        p1 = _dot(yp, w_pp)
        p2 = _dot(ys, w_ps)
        gates = _sigmoid(gp + bm)
        merged = gates[:, :D] * p1 + gates[:, D:] * p2
        out = _dot(merged.astype(BF16), w_out)
        r = _rms_r(out)
        outr = out * r
        nq = outr * wpost
        err = x + gate * nq - tgt
        loss = 0.5 * jnp.sum(jnp.mean(err * err, axis=-1, keepdims=True), keepdims=True).reshape(1, 1)
        g = err * (1.0 / D)
        dnq = g * gate
        dout = _rms_bwd(dnq * wpost, out, r)
        return merged, p1, p2, dout, g, _colsum(g * nq), _colsum(dnq * outr), jnp.broadcast_to(loss, (1, LANES))

    return tok_call(name, body, [y_pool, y_ssd, gatepre, x, target], [gate], [b_merge, norm_post, w_pp, w_ps, w_out],
                    [(D, BF16), (D, F32), (D, F32), (D, BF16), (D, F32)], [D], [(1, D), (1, LANES)])


def merge_bwd(name, dout, gatepre, p1, p2, b_merge, w_pp, w_ps, w_out):
    def body(dout, gp, p1, p2, bm, w_pp, w_ps, w_out):
        dmerged = _dot_nt(dout, w_out)
        gates = _sigmoid(gp + bm)
        g1, g2 = gates[:, :D], gates[:, D:]
        dp1 = (dmerged * g1).astype(BF16)
        dp2 = (dmerged * g2).astype(BF16)
        dgp = jnp.concatenate([dmerged * p1 * g1 * (1.0 - g1), dmerged * p2 * g2 * (1.0 - g2)], axis=1)
        return dp1, dp2, dgp, _dot_nt(dp1, w_pp), _dot_nt(dp2, w_ps), _colsum(dgp)

    return tok_call(name, body, [dout, gatepre, p1, p2], [], [b_merge, w_pp, w_ps, w_out],
                    [(D, BF16), (D, BF16), (2 * D, BF16), (D, F32), (D_INNER, F32)], [], [(1, 2 * D)])


def _adamw_math(w, g, m, v):
    m = ADAM_B1 * m + (1.0 - ADAM_B1) * g
    v = ADAM_B2 * v + (1.0 - ADAM_B2) * (g * g)
    m_hat = m / (1.0 - ADAM_B1 ** ADAM_STEP)
    v_hat = v / (1.0 - ADAM_B2 ** ADAM_STEP)
    delta = -ADAM_LR * (m_hat / (jnp.sqrt(v_hat) + ADAM_EPS) + ADAM_WD * w)
    return delta, m, v


def adamw(name, w, g, m, v, tr=256):
    R, C = w.shape
    tr = min(tr, R)
    assert R % tr == 0

    def body(w_ref, g_ref, m_ref, v_ref, d_ref, nm_ref, nv_ref):
        d, nm, nv = _adamw_math(w_ref[...], g_ref[...], m_ref[...], v_ref[...])
        d_ref[...] = d
        nm_ref[...] = nm
        nv_ref[...] = nv

    spec = pl.BlockSpec((tr, C), lambda i: (i, 0))
    return pl.pallas_call(
        body, name=name, out_shape=[jax.ShapeDtypeStruct((R, C), F32)] * 3, grid=(R // tr,),
        in_specs=[spec] * 4, out_specs=[spec] * 3, compiler_params=_cp(("parallel",)))(w, g, m, v)


def _me():
    return lax.axis_index("x"), lax.axis_index("y"), lax.axis_index("c")


def all_gather_small(name, v):
    R, C = v.shape

    def body(v_ref, out_ref, send_sems, recv_sems, local_sem):
        x, y, c = _me()
        me = 4 * x + 2 * y + c
        mine = pltpu.make_async_copy(v_ref, out_ref.at[me], local_sem)
        mine.start()
        copies = []
        for d in range(1, N_DEV):
            dx, dy, dc = d // 4, (d // 2) % 2, d % 2
            px, py, pc = x ^ dx, y ^ dy, c ^ dc
            copies.append(pltpu.make_async_remote_copy(
                src_ref=v_ref, dst_ref=out_ref.at[me], send_sem=send_sems.at[d - 1], recv_sem=recv_sems.at[d - 1],
                device_id=(px, py, pc), device_id_type=MESH))
        for cp in copies:
            cp.start()
        for d in range(1, N_DEV):
            dx, dy, dc = d // 4, (d // 2) % 2, d % 2
            peer = 4 * (x ^ dx) + 2 * (y ^ dy) + (c ^ dc)
            pltpu.make_async_remote_copy(
                src_ref=v_ref, dst_ref=out_ref.at[peer], send_sem=send_sems.at[d - 1], recv_sem=recv_sems.at[d - 1],
                device_id=(x ^ dx, y ^ dy, c ^ dc), device_id_type=MESH).wait_recv()
        for cp in copies:
            cp.wait_send()
        mine.wait()

    return pl.pallas_call(
        body, name=name, out_shape=jax.ShapeDtypeStruct((N_DEV, R, C), F32),
        in_specs=[pl.BlockSpec(memory_space=pltpu.VMEM)], out_specs=pl.BlockSpec(memory_space=pltpu.VMEM),
        scratch_shapes=[pltpu.SemaphoreType.DMA((N_DEV - 1,)), pltpu.SemaphoreType.DMA((N_DEV - 1,)), pltpu.SemaphoreType.DMA],
        compiler_params=pltpu.CompilerParams(vmem_limit_bytes=VMEM_LIMIT))(v)


def all_gather_chips(name, shard):
    R, C = shard.shape
    half = R // 2
    assert R % 32 == 0

    def body(s_ref, out_ref, send_sems, recv_sems, local_sem):
        x, y, c = _me()
        k = 2 * x + y
        chips = [(1 - x, y), (x, 1 - y), (1 - x, 1 - y)]

        def rows(chip, hc):
            return out_ref.at[2 * chip[0] + chip[1], pl.ds(hc * half, half), :]

        mine = pltpu.make_async_copy(s_ref, out_ref.at[k], local_sem)
        mine.start()
        first = [pltpu.make_async_remote_copy(
            src_ref=s_ref.at[pl.ds(c * half, half), :], dst_ref=rows((x, y), c), send_sem=send_sems.at[j],
            recv_sem=recv_sems.at[j], device_id=(*chip, c), device_id_type=MESH) for j, chip in enumerate(chips)]
        for cp in first:
            cp.start()
        passed = [pltpu.make_async_remote_copy(
            src_ref=rows(chip, c), dst_ref=rows(chip, c), send_sem=send_sems.at[3 + j], recv_sem=recv_sems.at[3 + j],
            device_id=(x, y, 1 - c), device_id_type=MESH) for j, chip in enumerate(chips)]
        for j, chip in enumerate(chips):
            pltpu.make_async_remote_copy(
                src_ref=rows(chip, c), dst_ref=rows(chip, c), send_sem=send_sems.at[j], recv_sem=recv_sems.at[j],
                device_id=(*chip, c), device_id_type=MESH).wait_recv()
            passed[j].start()
        for j, chip in enumerate(chips):
            pltpu.make_async_remote_copy(
                src_ref=rows(chip, 1 - c), dst_ref=rows(chip, 1 - c), send_sem=send_sems.at[3 + j], recv_sem=recv_sems.at[3 + j],
                device_id=(x, y, 1 - c), device_id_type=MESH).wait_recv()
        for cp in first + passed:
            cp.wait_send()
        mine.wait()

    return pl.pallas_call(
        body, name=name, out_shape=jax.ShapeDtypeStruct((N_CHIPS, R, C), shard.dtype),
        in_specs=[pl.BlockSpec(memory_space=pl.ANY)], out_specs=pl.BlockSpec(memory_space=pl.ANY),
        scratch_shapes=[pltpu.SemaphoreType.DMA((6,)), pltpu.SemaphoreType.DMA((6,)), pltpu.SemaphoreType.DMA],
        compiler_params=pltpu.CompilerParams(vmem_limit_bytes=VMEM_LIMIT))(shard)


def sibling_swap(name, v):
    def body(v_ref, out_ref, send_sem, recv_sem):
        x, y, c = _me()
        cp = pltpu.make_async_remote_copy(src_ref=v_ref, dst_ref=out_ref, send_sem=send_sem, recv_sem=recv_sem,
                                          device_id=(x, y, 1 - c), device_id_type=MESH)
        cp.start()
        cp.wait()

    return pl.pallas_call(
        body, name=name, out_shape=jax.ShapeDtypeStruct(v.shape, v.dtype),
        in_specs=[pl.BlockSpec(memory_space=pl.ANY)], out_specs=pl.BlockSpec(memory_space=pl.ANY),
        scratch_shapes=[pltpu.SemaphoreType.DMA, pltpu.SemaphoreType.DMA],
        compiler_params=pltpu.CompilerParams(vmem_limit_bytes=VMEM_LIMIT))(v)


def chip_exchange(name, parts):
    def body(p_ref, out_ref, send_sems, recv_sems, local_sem):
        x, y, c = _me()
        k = 2 * x + y
        chips = [(1 - x, y), (x, 1 - y), (1 - x, 1 - y)]
        mine = pltpu.make_async_copy(p_ref.at[k], out_ref.at[k], local_sem)
        mine.start()
        sends = [pltpu.make_async_remote_copy(
            src_ref=p_ref.at[2 * chip[0] + chip[1]], dst_ref=out_ref.at[k], send_sem=send_sems.at[j], recv_sem=recv_sems.at[j],
            device_id=(*chip, c), device_id_type=MESH) for j, chip in enumerate(chips)]
        for cp in sends:
            cp.start()
        for j, chip in enumerate(chips):
            pltpu.make_async_remote_copy(
                src_ref=p_ref.at[k], dst_ref=out_ref.at[2 * chip[0] + chip[1]], send_sem=send_sems.at[j], recv_sem=recv_sems.at[j],
                device_id=(*chip, c), device_id_type=MESH).wait_recv()
        for cp in sends:
            cp.wait_send()
        mine.wait()

    return pl.pallas_call(
        body, name=name, out_shape=jax.ShapeDtypeStruct(parts.shape, parts.dtype),
        in_specs=[pl.BlockSpec(memory_space=pl.ANY)], out_specs=pl.BlockSpec(memory_space=pl.ANY),
        scratch_shapes=[pltpu.SemaphoreType.DMA((3,)), pltpu.SemaphoreType.DMA((3,)), pltpu.SemaphoreType.DMA],
        compiler_params=pltpu.CompilerParams(vmem_limit_bytes=VMEM_LIMIT))(parts)


def _row_tile(rows, cap):
    best = None
    for t in range(8, min(rows, cap) + 1, 8):
        if rows % t == 0:
            best = t
    assert best is not None, rows
    return best


def add_arrays(name, arrs, out_dtype=F32):
    shape = arrs[0].shape
    C = shape[-1]
    flat = [a.reshape(-1, C) for a in arrs]
    R = flat[0].shape[0]
    tr = _row_tile(R, 1024)
    n = len(flat)

    def body(*refs):
        acc = refs[0][...].astype(F32)
        for r in refs[1:n]:
            acc = acc + r[...].astype(F32)
        refs[n][...] = acc.astype(out_dtype)

    spec = pl.BlockSpec((tr, C), lambda i: (i, 0))
    out = pl.pallas_call(
        body, name=name, out_shape=jax.ShapeDtypeStruct((R, C), out_dtype), grid=(R // tr,),
        in_specs=[spec] * n, out_specs=spec, compiler_params=_cp(("parallel",)))(*flat)
    return out.reshape(shape)


def reduce_scatter_chips(slabs):
    _, R, C = slabs.shape
    half = R // 2
    c = lax.axis_index("c")
    k = 2 * lax.axis_index("x") + lax.axis_index("y")
    halves = slabs.reshape(N_CHIPS, 2, half, C)
    own = lax.dynamic_index_in_dim(halves, c, axis=1, keepdims=False)
    other = lax.dynamic_index_in_dim(halves, 1 - c, axis=1, keepdims=False)
    from_sibling = sibling_swap("rs_sibling_halves", other)
    chip_part = add_arrays("rs_add_sibling", [own, from_sibling])
    landed = chip_exchange("rs_chip_exchange", chip_part)
    mine = add_arrays("rs_add_chips", [landed[j] for j in range(N_CHIPS)])
    sib = sibling_swap("rs_sibling_result", mine)
    lo = jnp.where(c == 0, mine, sib)
    hi = jnp.where(c == 0, sib, mine)
    del k
    return jnp.concatenate([lo, hi], axis=0)


def ada_mod_shard(cond_all, w_ada_shard, b_ada_shard):
    def body(c_ref, w_ref, b_ref, o_ref):
        o_ref[...] = _dot(_silu(c_ref[...]).astype(BF16), w_ref[...].astype(BF16)) + b_ref[...]

    return pl.pallas_call(body, name="ada_mod_shard", out_shape=jax.ShapeDtypeStruct((cond_all.shape[0], w_ada_shard.shape[1]), F32),
                          compiler_params=_cp())(cond_all, w_ada_shard, b_ada_shard)


def ada_bwd_shard(cond_all, dmod_all_shard, dmod_all, w_ada_shard, row_is_cctx):
    def body(c_ref, ds_ref, da_ref, w_ref, sel_ref, gw_ref, gb_ref, part_ref):
        sc = _silu(c_ref[...]).astype(BF16)
        gw_ref[...] = _dot_tn(sc, ds_ref[...].astype(BF16))
        gb_ref[...] = _colsum(da_ref[...])
        dc_tot = jnp.broadcast_to(_colsum(ds_ref[...] * sel_ref[...]), (8, ds_ref.shape[1]))
        part_ref[...] = _dot_nt(dc_tot.astype(BF16), w_ref[...].astype(BF16))

    n = cond_all.shape[0]
    return pl.pallas_call(
        body, name="ada_bwd_shard",
        out_shape=[jax.ShapeDtypeStruct(w_ada_shard.shape, F32), jax.ShapeDtypeStruct((1, dmod_all.shape[1]), F32),
                   jax.ShapeDtypeStruct((8, D), F32)],
        compiler_params=_cp())(cond_all, dmod_all_shard, dmod_all, w_ada_shard, row_is_cctx)


def sum_devices(name, gathered):
    def body(g_ref, o_ref):
        acc = g_ref[0]
        for d in range(1, N_DEV):
            acc = acc + g_ref[d]
        o_ref[...] = acc

    return pl.pallas_call(body, name=name, out_shape=jax.ShapeDtypeStruct(gathered.shape[1:], F32), compiler_params=_cp())(gathered)


def cctx_finish(gathered, c_ctx_row):
    def body(g_ref, c_ref, o_ref):
        acc = g_ref[0, 0:1, :]
        for k in range(1, N_CHIPS):
            acc = acc + g_ref[2 * k, 0:1, :]
        o_ref[...] = acc * _dsilu(c_ref[...])

    return pl.pallas_call(body, name="cctx_finish", out_shape=jax.ShapeDtypeStruct((1, D), F32), compiler_params=_cp())(gathered, c_ctx_row)


def _pack(parts, rows):
    flat = []
    for p in parts:
        p = p.reshape(-1)
        pad = (-p.shape[0]) % LANES
        flat.append(jnp.pad(p, (0, pad)) if pad else p)
    v = jnp.concatenate(flat)
    return jnp.pad(v, (0, rows * LANES - v.shape[0])).reshape(rows, LANES)


def _unpack(v, sizes):
    flat = v.reshape(-1)
    out, off = [], 0
    for n in sizes:
        out.append(flat[off:off + n])
        off += n + (-n) % LANES
    return out


W_SHARD_ROWS = 3424
SEG_ROWS = (0, 2320, 2576, 3088, 3344, 3408)


def kernel(x, c, ctx, c_ctx, w_ada, b_ada, norm_pre, norm_post, w_in, b_merge, pool_w, pool_scale, conv_w, conv_b, dt_bias, a_log, d_skip, ssd_norm, w_proj_pool, w_proj_ssd, w_out, loss_target, m_c_ctx, m_w_ada, m_b_ada, m_norm_pre, m_norm_post, m_w_in, m_b_merge, m_pool_w, m_pool_scale, m_conv_w, m_conv_b, m_dt_bias, m_a_log, m_d_skip, m_ssd_norm, m_w_proj_pool, m_w_proj_ssd, m_w_out, v_c_ctx, v_w_ada, v_b_ada, v_norm_pre, v_norm_post, v_w_in, v_b_merge, v_pool_w, v_pool_scale, v_conv_w, v_conv_b, v_dt_bias, v_a_log, v_d_skip, v_ssd_norm, v_w_proj_pool, v_w_proj_ssd, v_w_out):
    Bn, L, _ = x.shape
    Lc = ctx.shape[1]
    T, Tc = Bn * L, Bn * Lc
    assert Bn == 2
    ix, iy, ic = lax.axis_index("x"), lax.axis_index("y"), lax.axis_index("c")
    me = 4 * ix + 2 * iy + ic
    chip = 2 * ix + iy
    ada_cols = w_ada.shape[2]
    cw_cols = conv_w.shape[2]

    cond_own = jnp.pad(c, ((0, 8 - Bn), (0, 0))) + jnp.pad(c_ctx[None, :], ((Bn, 7 - Bn), (0, 0)))
    convw_own = jnp.pad(conv_w[0], ((0, 4), (0, D - cw_cols)))
    g1 = all_gather_small("gather_cond", jnp.concatenate([cond_own, convw_own], axis=0))
    cond_all = g1[:, 0:8].reshape(8 * N_DEV, D)
    conv_w_full = jnp.concatenate([g1[2 * k, 8:12, 0:cw_cols] for k in range(N_CHIPS)], axis=1)
    b_ada_shard = lax.dynamic_slice(b_ada, (0, chip * ada_cols), (1, ada_cols))
    g2 = all_gather_small("gather_mod", ada_mod_shard(cond_all, w_ada[0], b_ada_shard))
    mod_full = jnp.concatenate([g2[2 * k] for k in range(N_CHIPS)], axis=1)
    own = lax.dynamic_slice(mod_full, (8 * me, 0), (8, 3 * D))
    shift, scale, gate = (own[0:Bn, i * D:(i + 1) * D][:, None, :] for i in range(3))
    shift_c, scale_c = (jnp.broadcast_to(own[Bn:Bn + 1, i * D:(i + 1) * D][None], (Bn, 1, D)) for i in range(2))

    shard = jnp.concatenate([w_in[0].T, w_proj_pool[0], w_proj_ssd[0], w_out[0], pool_w[0].reshape(64, D),
                             jnp.zeros((W_SHARD_ROWS - SEG_ROWS[-1], D), F32)], axis=0).astype(BF16)
    gw = all_gather_chips("gather_weights", shard)
    w_inT = gw[:, SEG_ROWS[0]:SEG_ROWS[1]].reshape(IN_COLS, D)
    w_pp = gw[:, SEG_ROWS[1]:SEG_ROWS[2]].reshape(D, D)
    w_ps = gw[:, SEG_ROWS[2]:SEG_ROWS[3]].reshape(D_INNER, D)
    w_o = gw[:, SEG_ROWS[3]:SEG_ROWS[4]].reshape(D, D)
    pool_full = gw[:, SEG_ROWS[4]:SEG_ROWS[5]].reshape(N_CHIPS, 4, 64, POOL_GROUP).transpose(1, 0, 2, 3).reshape(D, POOL_GROUP)
    w_dt = jnp.pad(w_inT[9216:IN_COLS], ((0, LANES - 64), (0, 0)))
    seg_lo = (0, 256, 512, 768, 1024, 2048, 4096, 6144, 8192, 8704)
    seg_hi = (256, 512, 768, 1024, 2048, 4096, 6144, 8192, 8704, 9216)
    w_seg = [w_inT[lo:hi] for lo, hi in zip(seg_lo, seg_hi)] + [w_dt]

    hx = prenorm_fwd("prenorm_x", x, scale, shift, norm_pre)
    hc = prenorm_fwd("prenorm_ctx", ctx, scale_c, shift_c, norm_pre)
    hx2, hc2 = hx.reshape(T, D), hc.reshape(Tc, D)
    v = mm_nt("proj_v", hx2, w_inT[0:1024], F32).reshape(Bn, L, D)
    zp = mm_nt("proj_zpool", hx2, w_inT[1024:2048], F32).reshape(Bn, L, D)
    zs = mm_nt("proj_zssd", hx2, w_inT[2048:4096], F32).reshape(Bn, L, D_INNER)
    gp = mm_nt("proj_gate", hx2, w_inT[4096:6144], F32).reshape(Bn, L, 2 * D)
    xbc_raw = mm_nt("proj_xbc", hx2, w_inT[6144:9216], F32).reshape(Bn, L, CONV_DIM)
    dt_raw = mm_nt("proj_dt", hx2, w_dt, F32)
    xbc_raw_c = mm_nt("proj_xbc_ctx", hc2, w_inT[6144:9216], F32).reshape(Bn, Lc, CONV_DIM)
    dt_raw_c = mm_nt("proj_dt_ctx", hc2, w_dt, F32)
    dtT = dt_raw[:, :64].reshape(Bn, L, 64).transpose(0, 2, 1)
    dtT_c = dt_raw_c[:, :64].reshape(Bn, Lc, 64).transpose(0, 2, 1)
    bias3 = dt_bias.reshape(2 * N_BC, HPG, 1)
    alog3 = a_log.reshape(2 * N_BC, HPG, 1)

    xbc = conv_fwd("conv_x", xbc_raw, conv_w_full, conv_b)
    xbc_c = conv_fwd("conv_ctx", xbc_raw_c, conv_w_full, conv_b)
    zero_state = jnp.zeros((Bn, N_BC, D_STATE, GW), F32)
    ys, hs_x, hs_c, hf_x, hf_c = [], [], [], [], []
    for d in range(2):
        hsc, hfc = ssd_fwd(f"ssd_fwd_ctx{d}", dtT_c, bias3, alog3, xbc_c, zero_state, d, False)
        y, hsx, hfx = ssd_fwd(f"ssd_fwd_x{d}", dtT, bias3, alog3, xbc, hfc, d, True)
        ys.append(y)
        hs_x.append(hsx)
        hs_c.append(hsc)
        hf_x.append(hfx)
        hf_c.append(hfc)

    dgs = [pool_diff(f"pool_diff{g}", v, g * POOL_GROUP, g, False) for g in range(4)]
    y_pool = pool_mix_fwd("pool_mix", dgs, zp, pool_full, pool_scale)
    dskip_lanes = jnp.repeat(d_skip[0], HEAD_DIM)[None, :]
    y_ssd = gated_norm_fwd("gated_norm", ys[0], ys[1], (xbc, D_INNER), zs, dskip_lanes, ssd_norm)
    merged, p1, p2, dout, g_res, dgate, g_norm_post, loss_part = merge_fwd(
        "merge_fwd", y_pool, y_ssd, gp, x, loss_target, gate, b_merge, norm_post, w_pp, w_ps, w_o)

    dp1, dp2, dgp, dyp, dys, g_b_merge = merge_bwd("merge_bwd", dout, gp, p1, p2, b_merge, w_pp, w_ps, w_o)
    gw_o = mm_tn("gw_out", merged.reshape(T, D), dout.reshape(T, D))
    gw_pp = mm_tn("gw_proj_pool", y_pool.reshape(T, D), dp1.reshape(T, D))
    gw_ps = mm_tn("gw_proj_ssd", y_ssd.reshape(T, D_INNER), dp2.reshape(T, D))

    *dds, dzp, g_pool, g_pool_scale = pool_mix_bwd("pool_mix_bwd", dgs, zp, dyp, pool_full, pool_scale)
    dvs = [pool_diff(f"pool_diff_t{g}", dds[g], 0, g, True) for g in range(4)]

    head_sel = (jnp.arange(D_INNER)[:, None] // HEAD_DIM == jnp.arange(LANES)[None, :]).astype(BF16)
    dy, dxs_skip, dzs, g_ssd_norm, g_dskip = gated_norm_bwd(
        "gated_norm_bwd", ys[0], ys[1], (xbc, D_INNER), zs, dys, dskip_lanes, ssd_norm, head_sel)

    dxs, dbm, dcm, ddt, dxs_c, dbm_c, ddt_c = [], [], [], [], [], [], []
    g_bias = jnp.zeros((2, N_BC, HPG, 1), F32)
    g_alog = jnp.zeros((2, N_BC, HPG, 1), F32)
    for d in range(2):
        a, b_, c_, t_, gb, ga, dh0 = ssd_bwd(f"ssd_bwd_x{d}", dtT, bias3, alog3, xbc, hs_x[d], dy, zero_state, d)
        dxs.append(a), dbm.append(b_), dcm.append(c_), ddt.append(t_)
        ac, bc, _, tc, gbc, gac, _ = ssd_bwd(f"ssd_bwd_ctx{d}", dtT_c, bias3, alog3, xbc_c, hs_c[d], None, dh0, d)
        dxs_c.append(ac), dbm_c.append(bc), ddt_c.append(tc)
        g_bias = g_bias.at[d].set(jnp.sum(gb, axis=0) + jnp.sum(gbc, axis=0))
        g_alog = g_alog.at[d].set(jnp.sum(ga, axis=0) + jnp.sum(gac, axis=0))

    dxr_xs, gcw_xs, gcb_xs = conv_bwd("conv_bwd_xs", xbc_raw, dxs + [dxs_skip], conv_w_full, conv_b, 0, D_INNER)
    dxr_b, gcw_b, gcb_b = conv_bwd("conv_bwd_b", xbc_raw, dbm, conv_w_full, conv_b, D_INNER, N_BC * D_STATE)
    dxr_c, gcw_c, gcb_c = conv_bwd("conv_bwd_c", xbc_raw, dcm, conv_w_full, conv_b, D_INNER + N_BC * D_STATE, N_BC * D_STATE)
    dxr_xs_c, gcw_xs_c, gcb_xs_c = conv_bwd("conv_bwd_xs_ctx", xbc_raw_c, dxs_c, conv_w_full, conv_b, 0, D_INNER)
    dxr_b_c, gcw_b_c, gcb_b_c = conv_bwd("conv_bwd_b_ctx", xbc_raw_c, dbm_c, conv_w_full, conv_b, D_INNER, N_BC * D_STATE)
    g_conv_w = jnp.concatenate([gcw_xs + gcw_xs_c, gcw_b + gcw_b_c, gcw_c], axis=1)
    g_conv_b = jnp.concatenate([gcb_xs + gcb_xs_c, gcb_b + gcb_b_c, gcb_c], axis=1)

    def dt_cols(parts, n_tok):
        t = jnp.concatenate(parts, axis=1).transpose(0, 2, 1).reshape(n_tok, 2 * N_HEADS)
        return jnp.pad(t, ((0, 0), (0, LANES - 2 * N_HEADS))).astype(BF16)

    ddt2, ddt2_c = dt_cols(ddt, T), dt_cols(ddt_c, Tc)
    segs = ([dv.reshape(T, POOL_GROUP) for dv in dvs]
            + [dzp.reshape(T, D), dzs.reshape(T, D_INNER), dgp.reshape(T, 2 * D), dxr_xs.reshape(T, D_INNER),
               dxr_b.reshape(T, N_BC * D_STATE), dxr_c.reshape(T, N_BC * D_STATE), ddt2])
    d_hx = mm_nn_multi("d_hx", list(zip(segs, w_seg)), F32).reshape(Bn, L, D)
    segs_c = {7: dxr_xs_c.reshape(Tc, D_INNER), 8: dxr_b_c.reshape(Tc, N_BC * D_STATE), 10: ddt2_c}
    d_hc = mm_nn_multi("d_hc", [(segs_c[i], w_seg[i]) for i in (7, 8, 10)], F32).reshape(Bn, Lc, D)
    gw_rows = []
    for i, seg in enumerate(segs):
        init = mm_tn(f"gw_in_ctx{i}", segs_c[i], hc2) if i in segs_c else None
        gw_rows.append(mm_tn(f"gw_in{i}", seg, hx2, init=init))
    gw_rows[-1] = gw_rows[-1][0:2 * N_HEADS]
    gw_inT = jnp.concatenate(gw_rows, axis=0)

    grad_x, dscale, dshift, g_npre_x = prenorm_bwd("prenorm_bwd_x", x, d_hx, scale, norm_pre, g_res=g_res)
    _, dscale_c, dshift_c, g_npre_c = prenorm_bwd("prenorm_bwd_ctx", ctx, d_hc, scale_c, norm_pre)

    dmod_x = jnp.concatenate([dshift[:, 0], dscale[:, 0], dgate[:, 0]], axis=1)
    dmod_c = jnp.concatenate([jnp.sum(dshift_c[:, 0], axis=0, keepdims=True), jnp.sum(dscale_c[:, 0], axis=0, keepdims=True),
                              jnp.zeros((1, D), F32)], axis=1)
    dmod_own = jnp.pad(dmod_x, ((0, 8 - Bn), (0, 0))) + jnp.pad(dmod_c, ((Bn, 7 - Bn), (0, 0)))
    dmod_all = all_gather_small("gather_dmod", dmod_own).reshape(8 * N_DEV, 3 * D)
    row_is_cctx = (jnp.arange(8 * N_DEV) % 8 == Bn).astype(F32)[:, None]
    g_w_ada, g_b_ada, cpart = ada_bwd_shard(
        cond_all, lax.dynamic_slice(dmod_all, (0, chip * ada_cols), (8 * N_DEV, ada_cols)), dmod_all, w_ada[0], row_is_cctx)
    g_c_ctx = cctx_finish(all_gather_small("gather_cctx", cpart), c_ctx[None, :])

    small_sizes = (D, D, 2 * D, D, CONV_DIM, 2 * N_HEADS, 2 * N_HEADS, N_HEADS, D_INNER, 4 * CONV_DIM, 1)
    pk = _pack([g_npre_x + g_npre_c, g_norm_post, g_b_merge, g_pool_scale, g_conv_b, g_bias, g_alog, g_dskip[0, 0:N_HEADS],
                g_ssd_norm, g_conv_w, loss_part[0, 0:1]], 184)
    small = sum_devices("sum_small", all_gather_small("gather_small", pk))
    (g_norm_pre, g_norm_post_t, g_b_merge_t, g_pool_scale_t, g_conv_b_t, g_dt_bias, g_a_log, g_d_skip, g_ssd_norm_t,
     g_conv_w_t, loss) = _unpack(small, small_sizes)
    g_conv_w_shard = lax.dynamic_slice(g_conv_w_t.reshape(4, CONV_DIM), (0, chip * cw_cols), (4, cw_cols))

    pool_slab = g_pool.reshape(4, N_CHIPS, 64, POOL_GROUP).transpose(1, 0, 2, 3).reshape(N_CHIPS, 64, D)
    slabs = jnp.concatenate([gw_inT.reshape(N_CHIPS, 2320, D), gw_pp.reshape(N_CHIPS, 256, D), gw_ps.reshape(N_CHIPS, 512, D),
                             gw_o.reshape(N_CHIPS, 256, D), pool_slab, jnp.zeros((N_CHIPS, W_SHARD_ROWS - SEG_ROWS[-1], D), F32)], axis=1)
    gsh = reduce_scatter_chips(slabs)
    g_w_in = gsh[SEG_ROWS[0]:SEG_ROWS[1]].T
    g_w_pp, g_w_ps, g_w_o = (gsh[SEG_ROWS[i]:SEG_ROWS[i + 1]] for i in (1, 2, 3))
    g_pool_w = gsh[SEG_ROWS[4]:SEG_ROWS[5]].reshape(256, POOL_GROUP)

    grads = {
        "c_ctx": g_c_ctx.reshape(c_ctx.shape), "w_ada": g_w_ada[None], "b_ada": g_b_ada, "norm_pre": g_norm_pre[None],
        "norm_post": g_norm_post_t[None], "w_in": g_w_in[None], "b_merge": g_b_merge_t[None],
        "pool_w": g_pool_w.reshape(pool_w.shape), "pool_scale": g_pool_scale_t[None], "conv_w": g_conv_w_shard[None],
        "conv_b": g_conv_b_t[None], "dt_bias": g_dt_bias.reshape(dt_bias.shape), "a_log": g_a_log.reshape(a_log.shape),
        "d_skip": g_d_skip[None], "ssd_norm": g_ssd_norm_t[None], "w_proj_pool": g_w_pp[None], "w_proj_ssd": g_w_ps[None],
        "w_out": g_w_o[None]}
    weights = dict(c_ctx=c_ctx, w_ada=w_ada, b_ada=b_ada, norm_pre=norm_pre, norm_post=norm_post, w_in=w_in, b_merge=b_merge,
                   pool_w=pool_w, pool_scale=pool_scale, conv_w=conv_w, conv_b=conv_b, dt_bias=dt_bias, a_log=a_log,
                   d_skip=d_skip, ssd_norm=ssd_norm, w_proj_pool=w_proj_pool, w_proj_ssd=w_proj_ssd, w_out=w_out)
    m_in = dict(c_ctx=m_c_ctx, w_ada=m_w_ada, b_ada=m_b_ada, norm_pre=m_norm_pre, norm_post=m_norm_post, w_in=m_w_in,
                b_merge=m_b_merge, pool_w=m_pool_w, pool_scale=m_pool_scale, conv_w=m_conv_w, conv_b=m_conv_b,
                dt_bias=m_dt_bias, a_log=m_a_log, d_skip=m_d_skip, ssd_norm=m_ssd_norm, w_proj_pool=m_w_proj_pool,
                w_proj_ssd=m_w_proj_ssd, w_out=m_w_out)
    v_in = dict(c_ctx=v_c_ctx, w_ada=v_w_ada, b_ada=v_b_ada, norm_pre=v_norm_pre, norm_post=v_norm_post, w_in=v_w_in,
                b_merge=v_b_merge, pool_w=v_pool_w, pool_scale=v_pool_scale, conv_w=v_conv_w, conv_b=v_conv_b,
                dt_bias=v_dt_bias, a_log=v_a_log, d_skip=v_d_skip, ssd_norm=v_ssd_norm, w_proj_pool=v_w_proj_pool,
                w_proj_ssd=v_w_proj_ssd, w_out=v_w_out)
    names = list(weights)
    big = ("w_ada", "w_in", "pool_w", "w_proj_pool", "w_proj_ssd", "w_out")
    small_names = [n for n in names if n not in big]
    delta, new_m, new_v = {}, {}, {}
    for n in big:
        shape2 = (-1, weights[n].shape[-1])
        d_, m_, v_ = adamw(f"adamw_{n}", weights[n].reshape(shape2), grads[n].reshape(shape2), m_in[n].reshape(shape2),
                           v_in[n].reshape(shape2), tr=128)
        delta[n], new_m[n], new_v[n] = (t.reshape(weights[n].shape) for t in (d_, m_, v_))
    sizes = [weights[n].size for n in small_names]
    packed = [_pack([src[n] for n in small_names], 144) for src in (weights, grads, m_in, v_in)]
    outs = adamw("adamw_small", *packed, tr=144)
    for res, store in zip(outs, (delta, new_m, new_v)):
        for n, piece in zip(small_names, _unpack(res, sizes)):
            store[n] = piece.reshape(weights[n].shape)

    return (loss.reshape(()), grad_x, *[grads[n] for n in names], *[delta[n] for n in names],
            *[new_m[n] for n in names], *[new_v[n] for n in names])
```

```python
import jax
import jax.numpy as jnp
from jax import lax
from jax.experimental import pallas as pl
from jax.experimental.pallas import tpu as pltpu

F32 = jnp.float32
BF16 = jnp.bfloat16
MESH = pl.DeviceIdType.MESH

D = 1024
GRID_W = 64
NORM_EPS = 1e-6
POOL_WINDOWS = (2, 4, 8, 16)
POOL_GROUP = 256
D_INNER = 2048
HEAD_DIM = 64
N_HEADS = 32
D_STATE = 128
N_BC = 4
HPG = N_HEADS // N_BC
GW = HPG * HEAD_DIM
CONV_DIM = 3072
CHUNK = 128
OFF_XBC = 6144
IN_COLS = 9280
N_CHIPS = 4
N_DEV = 8

ADAM_LR = 0.001
ADAM_B1 = 0.9
ADAM_B2 = 0.999
ADAM_EPS = 1e-08
ADAM_WD = 0.01
ADAM_STEP = 10

V7X_VMEM_BYTES = 64 * 1024 * 1024
VMEM_LIMIT = V7X_VMEM_BYTES * 3 // 4
LANES = 128


def _cp(sem=None):
    return pltpu.CompilerParams(dimension_semantics=sem, vmem_limit_bytes=VMEM_LIMIT)


def _dot(a, b):
    return jnp.dot(a, b, preferred_element_type=F32)


def _dot_nt(a, b):
    return lax.dot_general(a, b, (((1,), (1,)), ((), ())), preferred_element_type=F32)


def _dot_tn(a, b):
    return lax.dot_general(a, b, (((0,), (0,)), ((), ())), preferred_element_type=F32)


def _split3(x):
    hi = x.astype(BF16)
    r1 = x - hi.astype(F32)
    mid = r1.astype(BF16)
    lo = (r1 - mid.astype(F32)).astype(BF16)
    return hi, mid, lo


def _sigmoid(x):
    return jax.nn.sigmoid(x)


def _silu(x):
    return x * _sigmoid(x)


def _dsilu(x):
    s = _sigmoid(x)
    return s * (1.0 + x * (1.0 - s))


def _softplus(x):
    return jnp.maximum(x, 0.0) + jnp.log(1.0 + jnp.exp(-jnp.abs(x)))


def mm_nt(name, a, b, out_dtype, tm=1024, tn=512):
    M, K = a.shape
    N = b.shape[0]
    tm, tn = min(tm, M), min(tn, N)
    assert M % tm == 0 and N % tn == 0, (M, N, tm, tn)

    def body(a_ref, b_ref, o_ref):
        o_ref[...] = _dot_nt(a_ref[...], b_ref[...]).astype(o_ref.dtype)

    return pl.pallas_call(
        body, name=name, out_shape=jax.ShapeDtypeStruct((M, N), out_dtype), grid=(M // tm, N // tn),
        in_specs=[pl.BlockSpec((tm, K), lambda i, j: (i, 0)), pl.BlockSpec((tn, K), lambda i, j: (j, 0))],
        out_specs=pl.BlockSpec((tm, tn), lambda i, j: (i, j)),
        compiler_params=_cp(("parallel", "arbitrary")))(a, b)


def mm_tn(name, a, b, init=None, tm=512, tn=1024, tk=512):
    T, M = a.shape
    N = b.shape[1]
    tm, tn, tk = min(tm, M), min(tn, N), min(tk, T)
    assert M % tm == 0 and N % tn == 0 and T % tk == 0, (M, N, T)
    has_init = init is not None

    def body(*refs):
        if has_init:
            a_ref, b_ref, i_ref, o_ref = refs
        else:
            a_ref, b_ref, o_ref = refs
        k = pl.program_id(2)

        @pl.when(k == 0)
        def _():
            o_ref[...] = i_ref[...] if has_init else jnp.zeros(o_ref.shape, F32)

        o_ref[...] += _dot_tn(a_ref[...], b_ref[...])

    in_specs = [pl.BlockSpec((tk, tm), lambda i, j, k: (k, i)), pl.BlockSpec((tk, tn), lambda i, j, k: (k, j))]
    args = [a, b]
    if has_init:
        in_specs.append(pl.BlockSpec((tm, tn), lambda i, j, k: (i, j)))
        args.append(init)
    return pl.pallas_call(
        body, name=name, out_shape=jax.ShapeDtypeStruct((M, N), F32), grid=(M // tm, N // tn, T // tk),
        in_specs=in_specs, out_specs=pl.BlockSpec((tm, tn), lambda i, j, k: (i, j)),
        compiler_params=_cp(("parallel", "parallel", "arbitrary")))(*args)


def mm_nn_multi(name, pairs, out_dtype, tm=512, tk=512):
    M = pairs[0][0].shape[0]
    N = pairs[0][1].shape[1]
    tm = min(tm, M)
    assert M % tm == 0
    plan = []
    step = 0
    for a, b in pairs:
        K = a.shape[1]
        t = min(tk, K)
        assert K % t == 0 and b.shape == (K, N)
        plan.append((t, step, K // t))
        step += K // t
    nsteps = step
    npairs = len(pairs)

    def body(*refs):
        o_ref, acc = refs[2 * npairs], refs[2 * npairs + 1]
        k = pl.program_id(1)

        @pl.when(k == 0)
        def _():
            acc[...] = jnp.zeros(acc.shape, F32)

        for p, (_, first, n) in enumerate(plan):
            @pl.when((k >= first) & (k < first + n))
            def _(p=p):
                acc[...] += _dot(refs[2 * p][...], refs[2 * p + 1][...])

        @pl.when(k == nsteps - 1)
        def _():
            o_ref[...] = acc[...].astype(o_ref.dtype)

    in_specs, args = [], []
    for (a, b), (t, first, n) in zip(pairs, plan):
        in_specs.append(pl.BlockSpec((tm, t), lambda i, k, first=first, n=n: (i, jnp.clip(k - first, 0, n - 1))))
        in_specs.append(pl.BlockSpec((t, N), lambda i, k, first=first, n=n: (jnp.clip(k - first, 0, n - 1), 0)))
        args += [a, b]
    return pl.pallas_call(
        body, name=name, out_shape=jax.ShapeDtypeStruct((M, N), out_dtype), grid=(M // tm, nsteps),
        in_specs=in_specs, out_specs=pl.BlockSpec((tm, N), lambda i, k: (i, 0)),
        scratch_shapes=[pltpu.VMEM((tm, N), F32)],
        compiler_params=_cp(("parallel", "arbitrary")))(*args)


def tok_call(name, body, tiled, perb, glob, out_tiled, out_perb, out_glob, tm=256):
    widths = [t[1] if isinstance(t, tuple) else t.shape[2] for t in tiled]
    tiled = [t[0] if isinstance(t, tuple) else t for t in tiled]
    Bn, L = tiled[0].shape[:2]
    tm = min(tm, L)
    assert L % tm == 0
    n_t, n_p, n_g = len(tiled), len(perb), len(glob)
    o_t, o_p, o_g = len(out_tiled), len(out_perb), len(out_glob)
    n_in = n_t + n_p + n_g

    def kern(*refs):
        ins, outs = refs[:n_in], refs[n_in:]
        b, j = pl.program_id(0), pl.program_id(1)
        vals = [r[0] for r in ins[:n_t + n_p]] + [r[...] for r in ins[n_t + n_p:]]
        res = body(*vals)
        if not isinstance(res, (tuple, list)):
            res = (res,)
        assert len(res) == o_t + o_p + o_g, (name, len(res))
        for r, v in zip(outs[:o_t], res[:o_t]):
            r[0] = v.astype(r.dtype)

        def accum(r, v, first, lead):
            @pl.when(first)
            def _():
                r[...] = jnp.zeros(r.shape, F32)
            if lead:
                r[0] += v
            else:
                r[...] += v

        for r, v in zip(outs[o_t:o_t + o_p], res[o_t:o_t + o_p]):
            accum(r, v, j == 0, True)
        for r, v in zip(outs[o_t + o_p:], res[o_t + o_p:]):
            accum(r, v, (j == 0) & (b == 0), False)

    in_specs = ([pl.BlockSpec((1, tm, w), lambda b, j: (b, j, 0)) for w in widths]
                + [pl.BlockSpec((1, 1, a.shape[2]), lambda b, j: (b, 0, 0)) for a in perb]
                + [pl.BlockSpec(a.shape, lambda b, j: (0, 0)) for a in glob])
    out_shape = ([jax.ShapeDtypeStruct((Bn, L, w), dt) for w, dt in out_tiled]
                 + [jax.ShapeDtypeStruct((Bn, 1, w), F32) for w in out_perb]
                 + [jax.ShapeDtypeStruct(s, F32) for s in out_glob])
    out_specs = ([pl.BlockSpec((1, tm, w), lambda b, j: (b, j, 0)) for w, _ in out_tiled]
                 + [pl.BlockSpec((1, 1, w), lambda b, j: (b, 0, 0)) for w in out_perb]
                 + [pl.BlockSpec(s, lambda b, j: (0, 0)) for s in out_glob])
    return pl.pallas_call(
        kern, name=name, out_shape=out_shape, grid=(Bn, L // tm), in_specs=in_specs, out_specs=out_specs,
        compiler_params=_cp(("arbitrary", "arbitrary")))(*tiled, *perb, *glob)


def slab_call(name, body, slabs, colparams, out_slabs, out_colred, wc=LANES):
    Bn, L = slabs[0][0].shape[:2]
    w_out = out_slabs[0][0]
    assert w_out % wc == 0 and all(off % wc == 0 for _, off in slabs + colparams)
    n_col = w_out // wc
    n_s, n_c = len(slabs), len(colparams)
    o_s = len(out_slabs)

    def kern(*refs):
        ins, outs = refs[:n_s + n_c], refs[n_s + n_c:]
        b = pl.program_id(1)
        vals = [r[0] for r in ins[:n_s]] + [r[...] for r in ins[n_s:]]
        res = body(*vals)
        if not isinstance(res, (tuple, list)):
            res = (res,)
        assert len(res) == o_s + len(out_colred), name
        for r, v in zip(outs[:o_s], res[:o_s]):
            r[0] = v.astype(r.dtype)

        def accum(r, v):
            @pl.when(b == 0)
            def _():
                r[...] = jnp.zeros(r.shape, F32)
            r[...] += v

        for r, v in zip(outs[o_s:], res[o_s:]):
            accum(r, v)

    in_specs = ([pl.BlockSpec((1, L, wc), lambda j, b, o=off // wc: (b, 0, o + j)) for _, off in slabs]
                + [pl.BlockSpec((a.shape[0], wc), lambda j, b, o=off // wc: (0, o + j)) for a, off in colparams])
    out_shape = ([jax.ShapeDtypeStruct((Bn, L, w), dt) for w, dt in out_slabs]
                 + [jax.ShapeDtypeStruct((r, w_out), F32) for r in out_colred])
    out_specs = ([pl.BlockSpec((1, L, wc), lambda j, b: (b, 0, j)) for _ in out_slabs]
                 + [pl.BlockSpec((r, wc), lambda j, b: (0, j)) for r in out_colred])
    return pl.pallas_call(
        kern, name=name, out_shape=out_shape, grid=(n_col, Bn), in_specs=in_specs, out_specs=out_specs,
        compiler_params=_cp(("arbitrary", "arbitrary")))(*[a for a, _ in slabs], *[a for a, _ in colparams])


def _rms_r(x):
    return lax.rsqrt(jnp.mean(x * x, axis=-1, keepdims=True) + NORM_EPS)


def _rms_bwd(dxh, x, r):
    return r * (dxh - x * (r * r) * jnp.mean(dxh * x, axis=-1, keepdims=True))


def _colsum(v):
    return jnp.sum(v, axis=0, keepdims=True)


def _stack_rows(rows):
    n, w = len(rows), rows[0].shape[1]
    sub = lax.broadcasted_iota(jnp.int32, (n, w), 0)
    acc = jnp.zeros((n, w), F32)
    for r, row in enumerate(rows):
        acc = acc + jnp.where(sub == r, jnp.broadcast_to(row, (n, w)), 0.0)
    return acc


def prenorm_fwd(name, x, scale, shift, w_pre):
    def body(x, scale, shift, w):
        n = x * _rms_r(x) * w
        return n * (1.0 + scale) + shift

    return tok_call(name, body, [x], [scale, shift], [w_pre], [(D, BF16)], [], [])[0]


def prenorm_bwd(name, x, dhx, scale, w_pre, g_res=None):
    has_res = g_res is not None

    def body(*v):
        if has_res:
            x, dhx, g, scale, w = v
        else:
            x, dhx, scale, w = v
        r = _rms_r(x)
        xr = x * r
        n = xr * w
        dn = dhx * (1.0 + scale)
        dx = _rms_bwd(dn * w, x, r)
        if has_res:
            dx = dx + g
        return dx, _colsum(dhx * n), _colsum(dhx), _colsum(dn * xr)

    tiled = [x, dhx] + ([g_res] if has_res else [])
    return tok_call(name, body, tiled, [scale], [w_pre], [(D, F32)], [D, D], [(1, D)])


def _shift_rows(x, o, tok, L):
    if o == 0:
        return x
    rolled = pltpu.roll(x, (-o) % L, 0)
    return jnp.where((tok + o >= 0) & (tok + o < L), rolled, 0.0)


def conv_fwd(name, xbc_raw, conv_w, conv_b):
    L = xbc_raw.shape[1]

    def body(x, w, b):
        tok = lax.broadcasted_iota(jnp.int32, x.shape, 0)
        pre = b
        for k in range(4):
            pre = pre + _shift_rows(x, k - 2, tok, L) * w[k:k + 1]
        return _silu(pre)

    return slab_call(name, body, [(xbc_raw, 0)], [(conv_w, 0), (conv_b, 0)], [(CONV_DIM, F32)], [])[0]


def conv_bwd(name, xbc_raw, dparts, conv_w, conv_b, col0, width):
    L = xbc_raw.shape[1]
    n_d = len(dparts)

    def body(*v):
        x, ds, w, b = v[0], v[1:1 + n_d], v[1 + n_d], v[2 + n_d]
        tok = lax.broadcasted_iota(jnp.int32, x.shape, 0)
        taps = [_shift_rows(x, k - 2, tok, L) for k in range(4)]
        pre = b
        for k in range(4):
            pre = pre + taps[k] * w[k:k + 1]
        dy = ds[0]
        for extra in ds[1:]:
            dy = dy + extra
        dpre = dy * _dsilu(pre)
        dx = jnp.zeros_like(x)
        for k in range(4):
            dx = dx + _shift_rows(dpre, 2 - k, tok, L) * w[k:k + 1]
        dw = _stack_rows([_colsum(dpre * taps[k]) for k in range(4)])
        return dx, dw, _colsum(dpre)

    return slab_call(name, body, [(xbc_raw, col0)] + [(d, 0) for d in dparts], [(conv_w, col0), (conv_b, col0)],
                     [(width, BF16)], [4, 1])


def _box_mean(x, k, step, pos, n, L, transpose):
    lo, hi = k // 2, k - 1 - k // 2
    cnt = (jnp.minimum(pos + hi + 1, n) - jnp.maximum(pos - lo, 0)).astype(F32)
    if transpose:
        x = x / cnt
        lo, hi = hi, lo
    acc = x
    for o in range(-lo, hi + 1):
        if o == 0:
            continue
        rolled = pltpu.roll(x, (-o * step) % L, 0)
        acc = acc + jnp.where((pos + o >= 0) & (pos + o < n), rolled, 0.0)
    return acc if transpose else acc / cnt


def pool_diff(name, v, col0, gi, transpose):
    L = v.shape[1]
    rows = L // GRID_W
    k = POOL_WINDOWS[gi]

    def body(x):
        tok = lax.broadcasted_iota(jnp.int32, x.shape, 0)
        col = tok & (GRID_W - 1)
        row = tok >> 6
        if not transpose:
            m = _box_mean(x, k, GRID_W, row, rows, L, False)
            m = _box_mean(m, k, 1, col, GRID_W, L, False)
        else:
            m = _box_mean(x, k, 1, col, GRID_W, L, True)
            m = _box_mean(m, k, GRID_W, row, rows, L, True)
        return m - x

    return slab_call(name, body, [(v, col0)], [], [(POOL_GROUP, BF16)], [])[0]


def pool_mix_fwd(name, dgs, z_pool, pool_w, pool_scale):
    def body(d0, d1, d2, d3, z, w, scale):
        q = jnp.concatenate([_dot(d, w[g * POOL_GROUP:(g + 1) * POOL_GROUP]) for g, d in enumerate((d0, d1, d2, d3))], axis=1)
        return q * scale * _silu(z)

    return tok_call(name, body, list(dgs) + [z_pool], [], [pool_w, pool_scale], [(D, BF16)], [], [])[0]


def pool_mix_bwd(name, dgs, z_pool, dyp, pool_w, pool_scale):
    def body(d0, d1, d2, d3, z, dyp, w, scale):
        ds = (d0, d1, d2, d3)
        q = jnp.concatenate([_dot(d, w[g * POOL_GROUP:(g + 1) * POOL_GROUP]) for g, d in enumerate(ds)], axis=1)
        dypm = dyp * _silu(z)
        dz = dyp * (q * scale) * _dsilu(z)
        dq = (dypm * scale).astype(BF16)
        dds, gws = [], []
        for g, d in enumerate(ds):
            dqg = dq[:, g * POOL_GROUP:(g + 1) * POOL_GROUP]
            dds.append(_dot_nt(dqg, w[g * POOL_GROUP:(g + 1) * POOL_GROUP]))
            gws.append(_dot_tn(d, dqg))
        return (*dds, dz, jnp.concatenate(gws, axis=0), _colsum(dypm * q))

    return tok_call(name, body, list(dgs) + [z_pool, dyp], [], [pool_w, pool_scale],
                    [(POOL_GROUP, F32)] * 4 + [(D, BF16)], [], [(D, POOL_GROUP), (1, D)])


def _cumsum_lanes(a, reverse):
    n = a.shape[1]
    lane = lax.broadcasted_iota(jnp.int32, a.shape, 1)
    s = 1
    while s < n:
        if reverse:
            a = a + jnp.where(lane < n - s, pltpu.roll(a, n - s, 1), 0.0)
        else:
            a = a + jnp.where(lane >= s, pltpu.roll(a, s, 1), 0.0)
        s *= 2
    return a


def _rows_to_cols(rows):
    r = rows.shape[0]
    if r < LANES:
        rows = jnp.concatenate([rows, jnp.zeros((LANES - r, rows.shape[1]), F32)], axis=0)
    return rows.T


def _cols_to_rows(cols):
    q = cols[0].shape[0]
    lane = lax.broadcasted_iota(jnp.int32, (q, LANES), 1)
    acc = jnp.zeros((q, LANES), F32)
    for r, c in enumerate(cols):
        acc = acc + jnp.where(lane == r, c, 0.0)
    return acc.T[0:len(cols)]


def _ssd_scalars(dtraw, bias, alog, reverse):
    dt = _softplus(dtraw + bias)
    A = -jnp.exp(alog)
    cs = _cumsum_lanes(dt * A, reverse)
    total = cs[:, 0:1] if reverse else cs[:, CHUNK - 1:CHUNK]
    return dt, A, cs, total


def _decay_matrix(cs_col, cs_row, reverse):
    i = lax.broadcasted_iota(jnp.int32, (CHUNK, CHUNK), 0)
    j = lax.broadcasted_iota(jnp.int32, (CHUNK, CHUNK), 1)
    keep = (i <= j) if reverse else (i >= j)
    return jnp.exp(jnp.where(keep, cs_col - cs_row, -jnp.inf))


def ssd_fwd_v1(name, dtT, bias, alog, xbc, h0, direction, with_y):
    Bn, L = xbc.shape[:2]
    nc = L // CHUNK
    reverse = direction == 1
    rowblk = direction * N_BC

    def chunk_of(s):
        return (nc - 1 - s) if reverse else s

    def kern(dt_ref, bias_ref, alog_ref, x_ref, b_ref, c_ref, h0_ref, *rest):
        if with_y:
            y_ref, hs_ref, hf_ref, h_scr, xt_scr = rest
        else:
            hs_ref, hf_ref, h_scr, xt_scr = rest
        s = pl.program_id(2)

        @pl.when(s == 0)
        def _():
            h_scr[...] = h0_ref[0, 0]

        dt, _, cs, total = _ssd_scalars(dt_ref[0], bias_ref[0], alog_ref[0], reverse)
        e_row = jnp.exp(cs)
        t_row = jnp.exp(total - cs)
        dc = jnp.exp(total)
        cols = _rows_to_cols(jnp.concatenate([dt, e_row, t_row, cs], axis=0))
        x = x_ref[0]
        bm = b_ref[0].astype(BF16)
        cm = c_ref[0].astype(BF16)
        h = h_scr[...]
        hs_ref[0, 0, 0] = h
        if with_y:
            cb = _dot_nt(cm, bm)
            yoff = _dot(cm, h.astype(BF16))
        for r in range(HPG):
            sl = slice(r * HEAD_DIM, (r + 1) * HEAD_DIM)
            xdt = x[:, sl] * cols[:, r:r + 1]
            if with_y:
                lr = _decay_matrix(cols[:, 3 * HPG + r:3 * HPG + r + 1], cs[r:r + 1], reverse)
                ydiag = _dot((cb * lr).astype(BF16), xdt.astype(BF16))
                y_ref[0, :, sl] = ydiag + yoff[:, sl] * cols[:, HPG + r:HPG + r + 1]
            xt_scr[:, sl] = (xdt * cols[:, 2 * HPG + r:2 * HPG + r + 1]).astype(BF16)
        st = _dot_tn(bm, xt_scr[...])
        for r in range(HPG):
            sl = slice(r * HEAD_DIM, (r + 1) * HEAD_DIM)
            h_scr[:, sl] = h[:, sl] * dc[r:r + 1] + st[:, sl]

        @pl.when(s == nc - 1)
        def _():
            hf_ref[0, 0] = h_scr[...]

    in_specs = [
        pl.BlockSpec((1, HPG, CHUNK), lambda b, g, s: (b, rowblk + g, chunk_of(s))),
        pl.BlockSpec((1, HPG, 1), lambda b, g, s: (rowblk + g, 0, 0)),
        pl.BlockSpec((1, HPG, 1), lambda b, g, s: (rowblk + g, 0, 0)),
        pl.BlockSpec((1, CHUNK, GW), lambda b, g, s: (b, chunk_of(s), g)),
        pl.BlockSpec((1, CHUNK, D_STATE), lambda b, g, s: (b, chunk_of(s), D_INNER // D_STATE + g)),
        pl.BlockSpec((1, CHUNK, D_STATE), lambda b, g, s: (b, chunk_of(s), D_INNER // D_STATE + N_BC + g)),
        pl.BlockSpec((1, 1, D_STATE, GW), lambda b, g, s: (b, g, 0, 0)),
    ]
    out_shape, out_specs = [], []
    if with_y:
        out_shape.append(jax.ShapeDtypeStruct((Bn, L, D_INNER), F32))
        out_specs.append(pl.BlockSpec((1, CHUNK, GW), lambda b, g, s: (b, chunk_of(s), g)))
    out_shape += [jax.ShapeDtypeStruct((Bn, N_BC, nc, D_STATE, GW), F32), jax.ShapeDtypeStruct((Bn, N_BC, D_STATE, GW), F32)]
    out_specs += [pl.BlockSpec((1, 1, 1, D_STATE, GW), lambda b, g, s: (b, g, chunk_of(s), 0, 0)),
                  pl.BlockSpec((1, 1, D_STATE, GW), lambda b, g, s: (b, g, 0, 0))]
    return pl.pallas_call(
        kern, name=name, out_shape=out_shape, grid=(Bn, N_BC, nc), in_specs=in_specs, out_specs=out_specs,
        scratch_shapes=[pltpu.VMEM((D_STATE, GW), F32), pltpu.VMEM((CHUNK, GW), BF16)],
        compiler_params=_cp(("arbitrary", "arbitrary", "arbitrary")))(dtT, bias, alog, xbc, xbc, xbc, h0)


def ssd_bwd_v1(name, dtT, bias, alog, xbc, h_start, dy, dh_final, direction):
    Bn, L = xbc.shape[:2]
    nc = L // CHUNK
    reverse = direction == 1
    rowblk = direction * N_BC
    has_y = dy is not None
    last = 0 if reverse else CHUNK - 1

    def chunk_of(s):
        return s if reverse else (nc - 1 - s)

    def kern(*refs):
        if has_y:
            (dt_ref, bias_ref, alog_ref, x_ref, b_ref, c_ref, hs_ref, dhf_ref, dy_ref,
             dx_ref, db_ref, dc_ref, ddt_ref, dbias_ref, dalog_ref, dh0_ref, dh_scr, e_scr, t_scr) = refs
        else:
            (dt_ref, bias_ref, alog_ref, x_ref, b_ref, hs_ref, dhf_ref,
             dx_ref, db_ref, ddt_ref, dbias_ref, dalog_ref, dh0_ref, dh_scr, t_scr) = refs
        s = pl.program_id(2)

        @pl.when(s == 0)
        def _():
            dh_scr[...] = dhf_ref[0, 0]
            dbias_ref[...] = jnp.zeros(dbias_ref.shape, F32)
            dalog_ref[...] = jnp.zeros(dalog_ref.shape, F32)

        dtraw = dt_ref[0]
        dt, A, cs, total = _ssd_scalars(dtraw, bias_ref[0], alog_ref[0], reverse)
        e_row = jnp.exp(cs)
        t_row = jnp.exp(total - cs)
        dcy = jnp.exp(total)
        cols = _rows_to_cols(jnp.concatenate([dt, e_row, t_row, cs], axis=0))
        x = x_ref[0]
        bm = b_ref[0].astype(BF16)
        h = hs_ref[0, 0, 0]
        dh = dh_scr[...]
        dh_bf = dh.astype(BF16)
        bdh = _dot(bm, dh_bf)
        if has_y:
            cm = c_ref[0].astype(BF16)
            dyv = dy_ref[0]
            cb = _dot_nt(cm, bm)
            yoff = _dot(cm, h.astype(BF16))
            dcb = jnp.zeros((CHUNK, CHUNK), F32)
        col_terms, row_terms, ddt_cols, dtot = [], [], [], []
        for r in range(HPG):
            sl = slice(r * HEAD_DIM, (r + 1) * HEAD_DIM)
            dt_c = cols[:, r:r + 1]
            e_c = cols[:, HPG + r:HPG + r + 1]
            t_c = cols[:, 2 * HPG + r:2 * HPG + r + 1]
            xr = x[:, sl]
            xdt = xr * dt_c
            dxdt = t_c * bdh[:, sl]
            d_t = jnp.sum(bdh[:, sl] * xdt, axis=1, keepdims=True)
            col = -(t_c * d_t)
            tot = jnp.sum(t_c * d_t, axis=0, keepdims=True) + dcy[r:r + 1] * jnp.sum(h[:, sl] * dh[:, sl], keepdims=True)
            if has_y:
                dyr = dyv[:, sl]
                lr = _decay_matrix(cols[:, 3 * HPG + r:3 * HPG + r + 1], cs[r:r + 1], reverse)
                w = cb * lr
                gm = _dot_nt(dyr.astype(BF16), xdt.astype(BF16))
                m = gm * w
                dcb = dcb + gm * lr
                dxdt = dxdt + _dot_tn(w.astype(BF16), dyr.astype(BF16))
                col = col + jnp.sum(m, axis=1, keepdims=True) + jnp.sum(yoff[:, sl] * dyr, axis=1, keepdims=True) * e_c
                row_terms.append(-jnp.sum(m, axis=0, keepdims=True))
                e_scr[:, sl] = (e_c * dyr).astype(BF16)
            t_scr[:, sl] = (t_c * xdt).astype(BF16)
            dx_ref[0, :, sl] = dxdt * dt_c
            ddt_cols.append(jnp.sum(dxdt * xr, axis=1, keepdims=True))
            col_terms.append(col)
            dtot.append(tot)
        db = _dot_nt(t_scr[...], dh_bf)
        if has_y:
            dcb_bf = dcb.astype(BF16)
            db = db + _dot_tn(dcb_bf, cm)
            dc_ref[0] = _dot(dcb_bf, bm) + _dot_nt(e_scr[...], h.astype(BF16))
            cte = _dot_tn(cm, e_scr[...])
        db_ref[0] = db
        for r in range(HPG):
            sl = slice(r * HEAD_DIM, (r + 1) * HEAD_DIM)
            new = dh[:, sl] * dcy[r:r + 1]
            if has_y:
                new = new + cte[:, sl]
            dh_scr[:, sl] = new
        dcs = _cols_to_rows(col_terms)
        if has_y:
            dcs = dcs + _stack_rows(row_terms)
        lane = lax.broadcasted_iota(jnp.int32, (HPG, CHUNK), 1)
        dcs = dcs + jnp.where(lane == last, _stack_rows([jnp.broadcast_to(t, (1, CHUNK)) for t in dtot]), 0.0)
        da = _cumsum_lanes(dcs, not reverse)
        ddt = da * A + _cols_to_rows(ddt_cols)
        ddtraw = ddt * _sigmoid(dtraw + bias_ref[0])
        ddt_ref[0] = ddtraw
        dbias_ref[0, 0] += jnp.sum(ddtraw, axis=1, keepdims=True)
        dalog_ref[0, 0] += jnp.sum(da * dt, axis=1, keepdims=True) * A

        @pl.when(s == nc - 1)
        def _():
            dh0_ref[0, 0] = dh_scr[...]

    cidx = lambda b, g, s: (b, chunk_of(s), g)
    in_specs = [
        pl.BlockSpec((1, HPG, CHUNK), lambda b, g, s: (b, rowblk + g, chunk_of(s))),
        pl.BlockSpec((1, HPG, 1), lambda b, g, s: (rowblk + g, 0, 0)),
        pl.BlockSpec((1, HPG, 1), lambda b, g, s: (rowblk + g, 0, 0)),
        pl.BlockSpec((1, CHUNK, GW), cidx),
        pl.BlockSpec((1, CHUNK, D_STATE), lambda b, g, s: (b, chunk_of(s), D_INNER // D_STATE + g)),
    ]
    args = [dtT, bias, alog, xbc, xbc]
    if has_y:
        in_specs.append(pl.BlockSpec((1, CHUNK, D_STATE), lambda b, g, s: (b, chunk_of(s), D_INNER // D_STATE + N_BC + g)))
        args.append(xbc)
    in_specs += [pl.BlockSpec((1, 1, 1, D_STATE, GW), lambda b, g, s: (b, g, chunk_of(s), 0, 0)),
                 pl.BlockSpec((1, 1, D_STATE, GW), lambda b, g, s: (b, g, 0, 0))]
    args += [h_start, dh_final]
    if has_y:
        in_specs.append(pl.BlockSpec((1, CHUNK, GW), cidx))
        args.append(dy)
    out_shape = [jax.ShapeDtypeStruct((Bn, L, D_INNER), F32), jax.ShapeDtypeStruct((Bn, L, N_BC * D_STATE), F32)]
    out_specs = [pl.BlockSpec((1, CHUNK, GW), cidx), pl.BlockSpec((1, CHUNK, D_STATE), cidx)]
    if has_y:
        out_shape.append(jax.ShapeDtypeStruct((Bn, L, N_BC * D_STATE), F32))
        out_specs.append(pl.BlockSpec((1, CHUNK, D_STATE), cidx))
    out_shape += [jax.ShapeDtypeStruct((Bn, N_HEADS, L), F32), jax.ShapeDtypeStruct((Bn, N_BC, HPG, 1), F32),
                  jax.ShapeDtypeStruct((Bn, N_BC, HPG, 1), F32), jax.ShapeDtypeStruct((Bn, N_BC, D_STATE, GW), F32)]
    out_specs += [pl.BlockSpec((1, HPG, CHUNK), lambda b, g, s: (b, g, chunk_of(s))),
                  pl.BlockSpec((1, 1, HPG, 1), lambda b, g, s: (b, g, 0, 0)),
                  pl.BlockSpec((1, 1, HPG, 1), lambda b, g, s: (b, g, 0, 0)),
                  pl.BlockSpec((1, 1, D_STATE, GW), lambda b, g, s: (b, g, 0, 0))]
    scratch = [pltpu.VMEM((D_STATE, GW), F32)] + ([pltpu.VMEM((CHUNK, GW), BF16)] if has_y else []) + [pltpu.VMEM((CHUNK, GW), BF16)]
    res = pl.pallas_call(
        kern, name=name, out_shape=out_shape, grid=(Bn, N_BC, nc), in_specs=in_specs, out_specs=out_specs,
        scratch_shapes=scratch, compiler_params=_cp(("arbitrary", "arbitrary", "arbitrary")))(*args)
    if has_y:
        return res
    dxs, db, ddt, dbias, dalog, dh0 = res
    return dxs, db, None, ddt, dbias, dalog, dh0


def _tri_mask(transposed, reverse):
    sub = lax.broadcasted_iota(jnp.int32, (CHUNK, CHUNK), 0)
    lane = lax.broadcasted_iota(jnp.int32, (CHUNK, CHUNK), 1)
    i, j = (lane, sub) if transposed else (sub, lane)
    return (i <= j) if reverse else (i >= j)


def ssd_fwd(name, dtT, bias, alog, xbc, h0, direction, with_y):
    Bn, L = xbc.shape[:2]
    nc = L // CHUNK
    reverse = direction == 1
    rowblk = direction * N_BC

    def chunk_of(s):
        return (nc - 1 - s) if reverse else s

    def kern(dt_ref, bias_ref, alog_ref, x_ref, b_ref, c_ref, h0_ref, *rest):
        if with_y:
            y_ref, hs_ref, hf_ref, h_scr = rest
        else:
            hs_ref, hf_ref, h_scr = rest
        s = pl.program_id(2)

        @pl.when(s == 0)
        def _():
            h_scr[...] = h0_ref[0, 0]

        dt, _, cs, total = _ssd_scalars(dt_ref[0], bias_ref[0], alog_ref[0], reverse)
        u = cs - jnp.log(dt)
        dtt = jnp.exp(total - u)
        dc = jnp.exp(total)
        x_bf = x_ref[0].astype(BF16)
        bm = b_ref[0]
        h = h_scr[...]
        h_bf = h.astype(BF16)
        hs_ref[0, 0, 0] = h
        bt = bm.T
        if with_y:
            cm = c_ref[0]
            cb = _dot_nt(cm.astype(BF16), bm.astype(BF16))
            cs_cols = _rows_to_cols(cs)
            keep = _tri_mask(False, reverse)
        first = lax.broadcasted_iota(jnp.int32, (1, LANES), 1) < HEAD_DIM
        heads = range(HPG)
        psl = [slice((r // 2) * LANES, (r // 2 + 1) * LANES) for r in heads]
        lhs = []
        if with_y:
            for r in heads:
                cs_col = jnp.broadcast_to(cs_cols[:, r:r + 1], (CHUNK, LANES))
                wf = cb * jnp.exp(jnp.where(keep, cs_col - u[r:r + 1], -jnp.inf))
                lhs.append(jnp.concatenate([wf.astype(BF16), (cm * jnp.exp(cs_col)).astype(BF16)], axis=1))
        bts = [(bt * dtt[r:r + 1]).astype(BF16) for r in heads]
        sts = [_dot(bts[r], x_bf[:, psl[r]]) for r in heads]
        if with_y:
            ys = [_dot(lhs[r], jnp.concatenate([x_bf[:, psl[r]], h_bf[:, psl[r]]], axis=0)) for r in heads]
        for p in range(HPG // 2):
            if with_y:
                y_ref[0, :, psl[2 * p]] = jnp.where(first, ys[2 * p], ys[2 * p + 1])
            dc_p = jnp.where(first, dc[2 * p:2 * p + 1], dc[2 * p + 1:2 * p + 2])
            h_scr[:, psl[2 * p]] = h[:, psl[2 * p]] * dc_p + jnp.where(first, sts[2 * p], sts[2 * p + 1])

        @pl.when(s == nc - 1)
        def _():
            hf_ref[0, 0] = h_scr[...]

    in_specs = [
        pl.BlockSpec((1, HPG, CHUNK), lambda b, g, s: (b, rowblk + g, chunk_of(s))),
        pl.BlockSpec((1, HPG, 1), lambda b, g, s: (rowblk + g, 0, 0)),
        pl.BlockSpec((1, HPG, 1), lambda b, g, s: (rowblk + g, 0, 0)),
        pl.BlockSpec((1, CHUNK, GW), lambda b, g, s: (b, chunk_of(s), g)),
        pl.BlockSpec((1, CHUNK, D_STATE), lambda b, g, s: (b, chunk_of(s), D_INNER // D_STATE + g)),
        pl.BlockSpec((1, CHUNK, D_STATE), lambda b, g, s: (b, chunk_of(s), D_INNER // D_STATE + N_BC + g)),
        pl.BlockSpec((1, 1, D_STATE, GW), lambda b, g, s: (b, g, 0, 0)),
    ]
    out_shape, out_specs = [], []
    if with_y:
        out_shape.append(jax.ShapeDtypeStruct((Bn, L, D_INNER), F32))
        out_specs.append(pl.BlockSpec((1, CHUNK, GW), lambda b, g, s: (b, chunk_of(s), g)))
    out_shape += [jax.ShapeDtypeStruct((Bn, N_BC, nc, D_STATE, GW), F32), jax.ShapeDtypeStruct((Bn, N_BC, D_STATE, GW), F32)]
    out_specs += [pl.BlockSpec((1, 1, 1, D_STATE, GW), lambda b, g, s: (b, g, chunk_of(s), 0, 0)),
                  pl.BlockSpec((1, 1, D_STATE, GW), lambda b, g, s: (b, g, 0, 0))]
    return pl.pallas_call(
        kern, name=name, out_shape=out_shape, grid=(Bn, N_BC, nc), in_specs=in_specs, out_specs=out_specs,
        scratch_shapes=[pltpu.VMEM((D_STATE, GW), F32)],
        compiler_params=_cp(("arbitrary", "arbitrary", "arbitrary")))(dtT, bias, alog, xbc, xbc, xbc, h0)


def ssd_bwd(name, dtT, bias, alog, xbc, h_start, dy, dh_final, direction):
    Bn, L = xbc.shape[:2]
    nc = L // CHUNK
    reverse = direction == 1
    rowblk = direction * N_BC
    has_y = dy is not None
    last = 0 if reverse else CHUNK - 1

    def chunk_of(s):
        return s if reverse else (nc - 1 - s)

    def kern(*refs):
        if has_y:
            (dt_ref, bias_ref, alog_ref, x_ref, b_ref, hs_ref, dhf_ref, c_ref, dy_ref,
             dx_ref, db_ref, ddt_ref, dbias_ref, dalog_ref, dh0_ref, dc_ref, dh_scr) = refs
        else:
            (dt_ref, bias_ref, alog_ref, x_ref, b_ref, hs_ref, dhf_ref,
             dx_ref, db_ref, ddt_ref, dbias_ref, dalog_ref, dh0_ref, dh_scr) = refs
        s = pl.program_id(2)

        @pl.when(s == 0)
        def _():
            dh_scr[...] = dhf_ref[0, 0]
            dbias_ref[...] = jnp.zeros(dbias_ref.shape, F32)
            dalog_ref[...] = jnp.zeros(dalog_ref.shape, F32)

        dtraw = dt_ref[0]
        dt, A, cs, total = _ssd_scalars(dtraw, bias_ref[0], alog_ref[0], reverse)
        u = cs - jnp.log(dt)
        dtt = jnp.exp(total - u)
        dcy = jnp.exp(total)
        u_cols = _rows_to_cols(u)
        x_bf = x_ref[0].astype(BF16)
        bm = b_ref[0]
        bt = bm.T
        h = hs_ref[0, 0, 0]
        dh = dh_scr[...]
        dh_bf = dh.astype(BF16)
        dbt = jnp.zeros((D_STATE, CHUNK), F32)
        if has_y:
            cm = c_ref[0]
            ct = cm.T
            e_row = jnp.exp(cs)
            dy_bf = dy_ref[0].astype(BF16)
            h_bf = h.astype(BF16)
            cbt = _dot_nt(bm.astype(BF16), cm.astype(BF16))
            keep = _tri_mask(True, reverse)
            dcbt = jnp.zeros((CHUNK, CHUNK), F32)
            dct = jnp.zeros((D_STATE, CHUNK), F32)
        tots, out_rows, in_rows, in_cols = [], [], [], []
        first = lax.broadcasted_iota(jnp.int32, (1, LANES), 1) < HEAD_DIM
        heads = range(HPG)
        psl = [slice((r // 2) * LANES, (r // 2 + 1) * LANES) for r in heads]
        mine = [first if r % 2 == 0 else jnp.logical_not(first) for r in heads]
        lhs, ets = [], []
        for r in heads:
            u_col = jnp.broadcast_to(u_cols[:, r:r + 1], (CHUNK, LANES))
            bs = (bm * jnp.exp(total[r:r + 1] - u_col)).astype(BF16)
            if has_y:
                et = jnp.exp(jnp.where(keep, cs[r:r + 1] - u_col, -jnp.inf))
                ets.append(et)
                lhs.append(jnp.concatenate([(cbt * et).astype(BF16), bs], axis=1))
            else:
                lhs.append(bs)
        p2raw = [_dot_nt(dh_bf[:, psl[r]], jnp.where(mine[r], x_bf[:, psl[r]], jnp.zeros((CHUNK, LANES), BF16))) for r in heads]
        if has_y:
            a1 = [_dot_nt(jnp.concatenate([x_bf[:, psl[r]], h_bf[:, psl[r]]], axis=0),
                          jnp.where(mine[r], dy_bf[:, psl[r]], jnp.zeros((CHUNK, LANES), BF16))) for r in heads]
            news = [_dot((ct * e_row[r:r + 1]).astype(BF16), dy_bf[:, psl[r]]) for r in heads]
            dxs = [_dot(lhs[r], jnp.concatenate([dy_bf[:, psl[r]], dh_bf[:, psl[r]]], axis=0)) for r in heads]
        else:
            dxs = [_dot(lhs[r], dh_bf[:, psl[r]]) for r in heads]
        for r in heads:
            if has_y:
                pt = a1[r][0:CHUNK] * ets[r]
                dcbt = dcbt + pt
                mt = pt * cbt
                ph = a1[r][CHUNK:] * e_row[r:r + 1]
                dct = dct + ph
                out_rows.append(_colsum(mt) + _colsum(ct * ph))
                in_cols.append(jnp.sum(mt, axis=1, keepdims=True))
            p2 = p2raw[r] * dtt[r:r + 1]
            dbt = dbt + p2
            t_term = _colsum(bt * p2)
            in_rows.append(t_term)
            hdh = h[:, psl[r]] * dh[:, psl[r]]
            tot = jnp.sum(t_term, axis=1, keepdims=True) + dcy[r:r + 1] * jnp.sum(jnp.where(mine[r], hdh, 0.0), keepdims=True)
            tots.append(jnp.broadcast_to(tot, (1, CHUNK)))
        for p in range(HPG // 2):
            dx_ref[0, :, psl[2 * p]] = jnp.where(first, dxs[2 * p], dxs[2 * p + 1])
            new = dh[:, psl[2 * p]] * jnp.where(first, dcy[2 * p:2 * p + 1], dcy[2 * p + 1:2 * p + 2])
            if has_y:
                new = new + jnp.where(first, news[2 * p], news[2 * p + 1])
            dh_scr[:, psl[2 * p]] = new
        db = dbt.T
        if has_y:
            dcbt_bf = dcbt.astype(BF16)
            db = db + _dot(dcbt_bf, cm.astype(BF16))
            dc_ref[0] = dct.T + _dot_tn(dcbt_bf, bm.astype(BF16))
        db_ref[0] = db
        s_row = _stack_rows(in_rows)
        lane = lax.broadcasted_iota(jnp.int32, (HPG, CHUNK), 1)
        dcs = jnp.where(lane == last, _stack_rows(tots), 0.0)
        if has_y:
            s_row = s_row + _cols_to_rows(in_cols)
            dcs = dcs + _stack_rows(out_rows)
        dcs = dcs - s_row
        da = _cumsum_lanes(dcs, not reverse)
        ddt = da * A + jnp.where(dt > 0.0, s_row / dt, 0.0)
        ddtraw = ddt * _sigmoid(dtraw + bias_ref[0])
        ddt_ref[0] = ddtraw
        dbias_ref[0, 0] += jnp.sum(ddtraw, axis=1, keepdims=True)
        dalog_ref[0, 0] += jnp.sum(da * dt, axis=1, keepdims=True) * A

        @pl.when(s == nc - 1)
        def _():
            dh0_ref[0, 0] = dh_scr[...]

    cidx = lambda b, g, s: (b, chunk_of(s), g)
    hidx = lambda b, g, s: (b, g, 0, 0)
    in_specs = [
        pl.BlockSpec((1, HPG, CHUNK), lambda b, g, s: (b, rowblk + g, chunk_of(s))),
        pl.BlockSpec((1, HPG, 1), lambda b, g, s: (rowblk + g, 0, 0)),
        pl.BlockSpec((1, HPG, 1), lambda b, g, s: (rowblk + g, 0, 0)),
        pl.BlockSpec((1, CHUNK, GW), cidx),
        pl.BlockSpec((1, CHUNK, D_STATE), lambda b, g, s: (b, chunk_of(s), D_INNER // D_STATE + g)),
        pl.BlockSpec((1, 1, 1, D_STATE, GW), lambda b, g, s: (b, g, chunk_of(s), 0, 0)),
        pl.BlockSpec((1, 1, D_STATE, GW), hidx),
    ]
    args = [dtT, bias, alog, xbc, xbc, h_start, dh_final]
    if has_y:
        in_specs += [pl.BlockSpec((1, CHUNK, D_STATE), lambda b, g, s: (b, chunk_of(s), D_INNER // D_STATE + N_BC + g)),
                     pl.BlockSpec((1, CHUNK, GW), cidx)]
        args += [xbc, dy]
    out_shape = [jax.ShapeDtypeStruct((Bn, L, D_INNER), F32), jax.ShapeDtypeStruct((Bn, L, N_BC * D_STATE), F32),
                 jax.ShapeDtypeStruct((Bn, N_HEADS, L), F32), jax.ShapeDtypeStruct((Bn, N_BC, HPG, 1), F32),
                 jax.ShapeDtypeStruct((Bn, N_BC, HPG, 1), F32), jax.ShapeDtypeStruct((Bn, N_BC, D_STATE, GW), F32)]
    out_specs = [pl.BlockSpec((1, CHUNK, GW), cidx), pl.BlockSpec((1, CHUNK, D_STATE), cidx),
                 pl.BlockSpec((1, HPG, CHUNK), lambda b, g, s: (b, g, chunk_of(s))),
                 pl.BlockSpec((1, 1, HPG, 1), hidx), pl.BlockSpec((1, 1, HPG, 1), hidx), pl.BlockSpec((1, 1, D_STATE, GW), hidx)]
    if has_y:
        out_shape.append(jax.ShapeDtypeStruct((Bn, L, N_BC * D_STATE), F32))
        out_specs.append(pl.BlockSpec((1, CHUNK, D_STATE), cidx))
    res = pl.pallas_call(
        kern, name=name, out_shape=out_shape, grid=(Bn, N_BC, nc), in_specs=in_specs, out_specs=out_specs,
        scratch_shapes=[pltpu.VMEM((D_STATE, GW), F32)],
        compiler_params=_cp(("arbitrary", "arbitrary", "arbitrary")))(*args)
    dxs, db, ddt, dbias, dalog, dh0 = res[:6]
    return dxs, db, (res[6] if has_y else None), ddt, dbias, dalog, dh0


def _group_mean(v):
    gw = D_INNER // N_BC
    parts = [jnp.broadcast_to(jnp.mean(v[:, g * gw:(g + 1) * gw], axis=-1, keepdims=True), (v.shape[0], gw)) for g in range(N_BC)]
    return jnp.concatenate(parts, axis=1)


def gated_norm_fwd(name, y_f, y_b, xs_src, z, dskip_lanes, w_norm):
    def body(yf, yb, xs, z, dsk, w):
        u = (yf + yb + dsk * xs) * _silu(z)
        r = lax.rsqrt(_group_mean(u * u) + NORM_EPS)
        return u * r * w

    return tok_call(name, body, [y_f, y_b, xs_src, z], [], [dskip_lanes, w_norm], [(D_INNER, BF16)], [], [])[0]


def _dot_exact01(v, sel):
    hi, mid, lo = _split3(v)
    return _dot(hi, sel) + _dot(mid, sel) + _dot(lo, sel)


def gated_norm_bwd(name, y_f, y_b, xs_src, z, d_out, dskip_lanes, w_norm, head_sel):
    def body(yf, yb, xs, z, do, dsk, w, sel):
        y = yf + yb + dsk * xs
        sz = _silu(z)
        u = y * sz
        r = lax.rsqrt(_group_mean(u * u) + NORM_EPS)
        duh = do * w
        du = r * (duh - u * (r * r) * _group_mean(duh * u))
        dy = du * sz
        dz = du * y * _dsilu(z)
        dsk_heads = _dot_exact01(jnp.broadcast_to(_colsum(dy * xs), (8, D_INNER)), sel)
        return dy, dy * dsk, dz, _colsum(do * u * r), dsk_heads

    return tok_call(name, body, [y_f, y_b, xs_src, z, d_out], [], [dskip_lanes, w_norm, head_sel],
                    [(D_INNER, F32), (D_INNER, F32), (D_INNER, BF16)], [], [(1, D_INNER), (8, LANES)], tm=128)


def merge_fwd(name, y_pool, y_ssd, gatepre, x, target, gate, b_merge, norm_post, w_pp, w_ps, w_out):
    def body(yp, ys, gp, x, tgt, gate, bm, wpost, w_pp, w_ps, w_out):
        p1 = _dot(yp, w_pp)
        p2 = _dot(ys, w_ps)
        gates = _sigmoid(gp + bm)
        merged = gates[:, :D] * p1 + gates[:, D:] * p2
        out = _dot(merged.astype(BF16), w_out)
        r = _rms_r(out)
        outr = out * r
        nq = outr * wpost
        err = x + gate * nq - tgt
        loss = 0.5 * jnp.sum(jnp.mean(err * err, axis=-1, keepdims=True), keepdims=True).reshape(1, 1)
        g = err * (1.0 / D)
        dnq = g * gate
        dout = _rms_bwd(dnq * wpost, out, r)
        return merged, p1, p2, dout, g, _colsum(g * nq), _colsum(dnq * outr), jnp.broadcast_to(loss, (1, LANES))

    return tok_call(name, body, [y_pool, y_ssd, gatepre, x, target], [gate], [b_merge, norm_post, w_pp, w_ps, w_out],
                    [(D, BF16), (D, F32), (D, F32), (D, BF16), (D, F32)], [D], [(1, D), (1, LANES)])


def merge_bwd(name, dout, gatepre, p1, p2, b_merge, w_pp, w_ps, w_out):
    def body(dout, gp, p1, p2, bm, w_pp, w_ps, w_out):
        dmerged = _dot_nt(dout, w_out)
        gates = _sigmoid(gp + bm)
        g1, g2 = gates[:, :D], gates[:, D:]
        dp1 = (dmerged * g1).astype(BF16)
        dp2 = (dmerged * g2).astype(BF16)
        dgp = jnp.concatenate([dmerged * p1 * g1 * (1.0 - g1), dmerged * p2 * g2 * (1.0 - g2)], axis=1)
        return dp1, dp2, dgp, _dot_nt(dp1, w_pp), _dot_nt(dp2, w_ps), _colsum(dgp)

    return tok_call(name, body, [dout, gatepre, p1, p2], [], [b_merge, w_pp, w_ps, w_out],
                    [(D, BF16), (D, BF16), (2 * D, BF16), (D, F32), (D_INNER, F32)], [], [(1, 2 * D)])


def _adamw_math(w, g, m, v):
    m = ADAM_B1 * m + (1.0 - ADAM_B1) * g
    v = ADAM_B2 * v + (1.0 - ADAM_B2) * (g * g)
    m_hat = m / (1.0 - ADAM_B1 ** ADAM_STEP)
    v_hat = v / (1.0 - ADAM_B2 ** ADAM_STEP)
    delta = -ADAM_LR * (m_hat / (jnp.sqrt(v_hat) + ADAM_EPS) + ADAM_WD * w)
    return delta, m, v


def adamw(name, w, g, m, v, tr=256):
    R, C = w.shape
    tr = min(tr, R)
    assert R % tr == 0

    def body(w_ref, g_ref, m_ref, v_ref, d_ref, nm_ref, nv_ref):
        d, nm, nv = _adamw_math(w_ref[...], g_ref[...], m_ref[...], v_ref[...])
        d_ref[...] = d
        nm_ref[...] = nm
        nv_ref[...] = nv

    spec = pl.BlockSpec((tr, C), lambda i: (i, 0))
    return pl.pallas_call(
        body, name=name, out_shape=[jax.ShapeDtypeStruct((R, C), F32)] * 3, grid=(R // tr,),
        in_specs=[spec] * 4, out_specs=[spec] * 3, compiler_params=_cp(("parallel",)))(w, g, m, v)


def _me():
    return lax.axis_index("x"), lax.axis_index("y"), lax.axis_index("c")


def all_gather_small(name, v):
    R, C = v.shape

    def body(v_ref, out_ref, send_sems, recv_sems, local_sem):
        x, y, c = _me()
        me = 4 * x + 2 * y + c
        mine = pltpu.make_async_copy(v_ref, out_ref.at[me], local_sem)
        mine.start()
        copies = []
        for d in range(1, N_DEV):
            dx, dy, dc = d // 4, (d // 2) % 2, d % 2
            px, py, pc = x ^ dx, y ^ dy, c ^ dc
            copies.append(pltpu.make_async_remote_copy(
                src_ref=v_ref, dst_ref=out_ref.at[me], send_sem=send_sems.at[d - 1], recv_sem=recv_sems.at[d - 1],
                device_id=(px, py, pc), device_id_type=MESH))
        for cp in copies:
            cp.start()
        for d in range(1, N_DEV):
            dx, dy, dc = d // 4, (d // 2) % 2, d % 2
            peer = 4 * (x ^ dx) + 2 * (y ^ dy) + (c ^ dc)
            pltpu.make_async_remote_copy(
                src_ref=v_ref, dst_ref=out_ref.at[peer], send_sem=send_sems.at[d - 1], recv_sem=recv_sems.at[d - 1],
                device_id=(x ^ dx, y ^ dy, c ^ dc), device_id_type=MESH).wait_recv()
        for cp in copies:
            cp.wait_send()
        mine.wait()

    return pl.pallas_call(
        body, name=name, out_shape=jax.ShapeDtypeStruct((N_DEV, R, C), F32),
        in_specs=[pl.BlockSpec(memory_space=pltpu.VMEM)], out_specs=pl.BlockSpec(memory_space=pltpu.VMEM),
        scratch_shapes=[pltpu.SemaphoreType.DMA((N_DEV - 1,)), pltpu.SemaphoreType.DMA((N_DEV - 1,)), pltpu.SemaphoreType.DMA],
        compiler_params=pltpu.CompilerParams(vmem_limit_bytes=VMEM_LIMIT))(v)


def all_gather_chips(name, shard):
    R, C = shard.shape
    half = R // 2
    assert R % 32 == 0

    def body(s_ref, out_ref, send_sems, recv_sems, local_sem):
        x, y, c = _me()
        k = 2 * x + y
        chips = [(1 - x, y), (x, 1 - y), (1 - x, 1 - y)]

        def rows(chip, hc):
            return out_ref.at[2 * chip[0] + chip[1], pl.ds(hc * half, half), :]

        mine = pltpu.make_async_copy(s_ref, out_ref.at[k], local_sem)
        mine.start()
        first = [pltpu.make_async_remote_copy(
            src_ref=s_ref.at[pl.ds(c * half, half), :], dst_ref=rows((x, y), c), send_sem=send_sems.at[j],
            recv_sem=recv_sems.at[j], device_id=(*chip, c), device_id_type=MESH) for j, chip in enumerate(chips)]
        for cp in first:
            cp.start()
        passed = [pltpu.make_async_remote_copy(
            src_ref=rows(chip, c), dst_ref=rows(chip, c), send_sem=send_sems.at[3 + j], recv_sem=recv_sems.at[3 + j],
            device_id=(x, y, 1 - c), device_id_type=MESH) for j, chip in enumerate(chips)]
        for j, chip in enumerate(chips):
            pltpu.make_async_remote_copy(
                src_ref=rows(chip, c), dst_ref=rows(chip, c), send_sem=send_sems.at[j], recv_sem=recv_sems.at[j],
                device_id=(*chip, c), device_id_type=MESH).wait_recv()
            passed[j].start()
        for j, chip in enumerate(chips):
            pltpu.make_async_remote_copy(
                src_ref=rows(chip, 1 - c), dst_ref=rows(chip, 1 - c), send_sem=send_sems.at[3 + j], recv_sem=recv_sems.at[3 + j],
                device_id=(x, y, 1 - c), device_id_type=MESH).wait_recv()
        for cp in first + passed:
            cp.wait_send()
        mine.wait()

    return pl.pallas_call(
        body, name=name, out_shape=jax.ShapeDtypeStruct((N_CHIPS, R, C), shard.dtype),
        in_specs=[pl.BlockSpec(memory_space=pl.ANY)], out_specs=pl.BlockSpec(memory_space=pl.ANY),
        scratch_shapes=[pltpu.SemaphoreType.DMA((6,)), pltpu.SemaphoreType.DMA((6,)), pltpu.SemaphoreType.DMA],
        compiler_params=pltpu.CompilerParams(vmem_limit_bytes=VMEM_LIMIT))(shard)


def sibling_swap(name, v):
    def body(v_ref, out_ref, send_sem, recv_sem):
        x, y, c = _me()
        cp = pltpu.make_async_remote_copy(src_ref=v_ref, dst_ref=out_ref, send_sem=send_sem, recv_sem=recv_sem,
                                          device_id=(x, y, 1 - c), device_id_type=MESH)
        cp.start()
        cp.wait()

    return pl.pallas_call(
        body, name=name, out_shape=jax.ShapeDtypeStruct(v.shape, v.dtype),
        in_specs=[pl.BlockSpec(memory_space=pl.ANY)], out_specs=pl.BlockSpec(memory_space=pl.ANY),
        scratch_shapes=[pltpu.SemaphoreType.DMA, pltpu.SemaphoreType.DMA],
        compiler_params=pltpu.CompilerParams(vmem_limit_bytes=VMEM_LIMIT))(v)


def chip_exchange(name, parts):
    def body(p_ref, out_ref, send_sems, recv_sems, local_sem):
        x, y, c = _me()
        k = 2 * x + y
        chips = [(1 - x, y), (x, 1 - y), (1 - x, 1 - y)]
        mine = pltpu.make_async_copy(p_ref.at[k], out_ref.at[k], local_sem)
        mine.start()
        sends = [pltpu.make_async_remote_copy(
            src_ref=p_ref.at[2 * chip[0] + chip[1]], dst_ref=out_ref.at[k], send_sem=send_sems.at[j], recv_sem=recv_sems.at[j],
            device_id=(*chip, c), device_id_type=MESH) for j, chip in enumerate(chips)]
        for cp in sends:
            cp.start()
        for j, chip in enumerate(chips):
            pltpu.make_async_remote_copy(
                src_ref=p_ref.at[k], dst_ref=out_ref.at[2 * chip[0] + chip[1]], send_sem=send_sems.at[j], recv_sem=recv_sems.at[j],
                device_id=(*chip, c), device_id_type=MESH).wait_recv()
        for cp in sends:
            cp.wait_send()
        mine.wait()

    return pl.pallas_call(
        body, name=name, out_shape=jax.ShapeDtypeStruct(parts.shape, parts.dtype),
        in_specs=[pl.BlockSpec(memory_space=pl.ANY)], out_specs=pl.BlockSpec(memory_space=pl.ANY),
        scratch_shapes=[pltpu.SemaphoreType.DMA((3,)), pltpu.SemaphoreType.DMA((3,)), pltpu.SemaphoreType.DMA],
        compiler_params=pltpu.CompilerParams(vmem_limit_bytes=VMEM_LIMIT))(parts)


def _row_tile(rows, cap, mult=8):
    best = None
    for t in range(mult, min(rows, cap) + 1, mult):
        if rows % t == 0:
            best = t
    assert best is not None, rows
    return best


def add_arrays(name, arrs, out_dtype=F32):
    shape = arrs[0].shape
    C = shape[-1]
    flat = [a.reshape(-1, C) for a in arrs]
    R = flat[0].shape[0]
    narrow = out_dtype == BF16 or any(a.dtype == BF16 for a in arrs)
    tr = _row_tile(R, 2048 if len(arrs) <= 2 else 1024, 16 if narrow else 8)
    n = len(flat)

    def body(*refs):
        acc = refs[0][...].astype(F32)
        for r in refs[1:n]:
            acc = acc + r[...].astype(F32)
        refs[n][...] = acc.astype(out_dtype)

    spec = pl.BlockSpec((tr, C), lambda i: (i, 0))
    out = pl.pallas_call(
        body, name=name, out_shape=jax.ShapeDtypeStruct((R, C), out_dtype), grid=(R // tr,),
        in_specs=[spec] * n, out_specs=spec, compiler_params=_cp(("parallel",)))(*flat)
    return out.reshape(shape)


def reduce_scatter_chips(slabs):
    _, R, C = slabs.shape
    half = R // 2
    c = lax.axis_index("c")
    k = 2 * lax.axis_index("x") + lax.axis_index("y")
    halves = slabs.reshape(N_CHIPS, 2, half, C)
    own = lax.dynamic_index_in_dim(halves, c, axis=1, keepdims=False)
    other = lax.dynamic_index_in_dim(halves, 1 - c, axis=1, keepdims=False)
    from_sibling = sibling_swap("rs_sibling_halves", other)
    chip_part = add_arrays("rs_add_sibling", [own, from_sibling], out_dtype=BF16)
    landed = chip_exchange("rs_chip_exchange", chip_part)
    mine = add_arrays("rs_add_chips", [landed[j] for j in range(N_CHIPS)])
    sib = sibling_swap("rs_sibling_result", mine)
    lo = jnp.where(c == 0, mine, sib)
    hi = jnp.where(c == 0, sib, mine)
    del k
    return jnp.concatenate([lo, hi], axis=0)


def ada_mod_shard(cond_all, w_ada_shard, b_ada_shard):
    def body(c_ref, w_ref, b_ref, o_ref):
        o_ref[...] = _dot(_silu(c_ref[...]).astype(BF16), w_ref[...].astype(BF16)) + b_ref[...]

    return pl.pallas_call(body, name="ada_mod_shard", out_shape=jax.ShapeDtypeStruct((cond_all.shape[0], w_ada_shard.shape[1]), F32),
                          compiler_params=_cp())(cond_all, w_ada_shard, b_ada_shard)


def ada_bwd_shard(cond_all, dmod_all_shard, dmod_all, w_ada_shard, row_is_cctx):
    def body(c_ref, ds_ref, da_ref, w_ref, sel_ref, gw_ref, gb_ref, part_ref):
        sc = _silu(c_ref[...]).astype(BF16)
        gw_ref[...] = _dot_tn(sc, ds_ref[...].astype(BF16))
        gb_ref[...] = _colsum(da_ref[...])
        dc_tot = jnp.broadcast_to(_colsum(ds_ref[...] * sel_ref[...]), (8, ds_ref.shape[1]))
        part_ref[...] = _dot_nt(dc_tot.astype(BF16), w_ref[...].astype(BF16))

    n = cond_all.shape[0]
    return pl.pallas_call(
        body, name="ada_bwd_shard",
        out_shape=[jax.ShapeDtypeStruct(w_ada_shard.shape, F32), jax.ShapeDtypeStruct((1, dmod_all.shape[1]), F32),
                   jax.ShapeDtypeStruct((8, D), F32)],
        compiler_params=_cp())(cond_all, dmod_all_shard, dmod_all, w_ada_shard, row_is_cctx)


def sum_devices(name, gathered):
    def body(g_ref, o_ref):
        acc = g_ref[0]
        for d in range(1, N_DEV):
            acc = acc + g_ref[d]
        o_ref[...] = acc

    return pl.pallas_call(body, name=name, out_shape=jax.ShapeDtypeStruct(gathered.shape[1:], F32), compiler_params=_cp())(gathered)


def cctx_finish(gathered, c_ctx_row):
    def body(g_ref, c_ref, o_ref):
        acc = g_ref[0, 0:1, :]
        for k in range(1, N_CHIPS):
            acc = acc + g_ref[2 * k, 0:1, :]
        o_ref[...] = acc * _dsilu(c_ref[...])

    return pl.pallas_call(body, name="cctx_finish", out_shape=jax.ShapeDtypeStruct((1, D), F32), compiler_params=_cp())(gathered, c_ctx_row)


def _pack(parts, rows):
    flat = []
    for p in parts:
        p = p.reshape(-1)
        pad = (-p.shape[0]) % LANES
        flat.append(jnp.pad(p, (0, pad)) if pad else p)
    v = jnp.concatenate(flat)
    return jnp.pad(v, (0, rows * LANES - v.shape[0])).reshape(rows, LANES)


def _unpack(v, sizes):
    flat = v.reshape(-1)
    out, off = [], 0
    for n in sizes:
        out.append(flat[off:off + n])
        off += n + (-n) % LANES
    return out


W_SHARD_ROWS = 3456
SEG_ROWS = (0, 2320, 2576, 3088, 3344, 3408)


def kernel(x, c, ctx, c_ctx, w_ada, b_ada, norm_pre, norm_post, w_in, b_merge, pool_w, pool_scale, conv_w, conv_b, dt_bias, a_log, d_skip, ssd_norm, w_proj_pool, w_proj_ssd, w_out, loss_target, m_c_ctx, m_w_ada, m_b_ada, m_norm_pre, m_norm_post, m_w_in, m_b_merge, m_pool_w, m_pool_scale, m_conv_w, m_conv_b, m_dt_bias, m_a_log, m_d_skip, m_ssd_norm, m_w_proj_pool, m_w_proj_ssd, m_w_out, v_c_ctx, v_w_ada, v_b_ada, v_norm_pre, v_norm_post, v_w_in, v_b_merge, v_pool_w, v_pool_scale, v_conv_w, v_conv_b, v_dt_bias, v_a_log, v_d_skip, v_ssd_norm, v_w_proj_pool, v_w_proj_ssd, v_w_out):
    Bn, L, _ = x.shape
    Lc = ctx.shape[1]
    T, Tc = Bn * L, Bn * Lc
    assert Bn == 2
    ix, iy, ic = lax.axis_index("x"), lax.axis_index("y"), lax.axis_index("c")
    me = 4 * ix + 2 * iy + ic
    chip = 2 * ix + iy
    ada_cols = w_ada.shape[2]
    cw_cols = conv_w.shape[2]

    cond_own = jnp.pad(c, ((0, 8 - Bn), (0, 0))) + jnp.pad(c_ctx[None, :], ((Bn, 7 - Bn), (0, 0)))
    convw_own = jnp.pad(conv_w[0], ((0, 4), (0, D - cw_cols)))
    g1 = all_gather_small("gather_cond", jnp.concatenate([cond_own, convw_own], axis=0))
    cond_all = g1[:, 0:8].reshape(8 * N_DEV, D)
    conv_w_full = jnp.concatenate([g1[2 * k, 8:12, 0:cw_cols] for k in range(N_CHIPS)], axis=1)
    b_ada_shard = lax.dynamic_slice(b_ada, (0, chip * ada_cols), (1, ada_cols))
    g2 = all_gather_small("gather_mod", ada_mod_shard(cond_all, w_ada[0], b_ada_shard))
    mod_full = jnp.concatenate([g2[2 * k] for k in range(N_CHIPS)], axis=1)
    own = lax.dynamic_slice(mod_full, (8 * me, 0), (8, 3 * D))
    shift, scale, gate = (own[0:Bn, i * D:(i + 1) * D][:, None, :] for i in range(3))
    shift_c, scale_c = (jnp.broadcast_to(own[Bn:Bn + 1, i * D:(i + 1) * D][None], (Bn, 1, D)) for i in range(2))

    shard = jnp.concatenate([w_in[0].T, w_proj_pool[0], w_proj_ssd[0], w_out[0], pool_w[0].reshape(64, D),
                             jnp.zeros((W_SHARD_ROWS - SEG_ROWS[-1], D), F32)], axis=0).astype(BF16)
    gw = all_gather_chips("gather_weights", shard)
    w_inT = gw[:, SEG_ROWS[0]:SEG_ROWS[1]].reshape(IN_COLS, D)
    w_pp = gw[:, SEG_ROWS[1]:SEG_ROWS[2]].reshape(D, D)
    w_ps = gw[:, SEG_ROWS[2]:SEG_ROWS[3]].reshape(D_INNER, D)
    w_o = gw[:, SEG_ROWS[3]:SEG_ROWS[4]].reshape(D, D)
    pool_full = gw[:, SEG_ROWS[4]:SEG_ROWS[5]].reshape(N_CHIPS, 4, 64, POOL_GROUP).transpose(1, 0, 2, 3).reshape(D, POOL_GROUP)
    w_dt = jnp.pad(w_inT[9216:IN_COLS], ((0, LANES - 64), (0, 0)))
    seg_lo = (0, 256, 512, 768, 1024, 2048, 4096, 6144, 8192, 8704)
    seg_hi = (256, 512, 768, 1024, 2048, 4096, 6144, 8192, 8704, 9216)
    w_seg = [w_inT[lo:hi] for lo, hi in zip(seg_lo, seg_hi)] + [w_dt]

    hx = prenorm_fwd("prenorm_x", x, scale, shift, norm_pre)
    hc = prenorm_fwd("prenorm_ctx", ctx, scale_c, shift_c, norm_pre)
    hx2, hc2 = hx.reshape(T, D), hc.reshape(Tc, D)
    v = mm_nt("proj_v", hx2, w_inT[0:1024], F32).reshape(Bn, L, D)
    zp = mm_nt("proj_zpool", hx2, w_inT[1024:2048], F32).reshape(Bn, L, D)
    zs = mm_nt("proj_zssd", hx2, w_inT[2048:4096], F32).reshape(Bn, L, D_INNER)
    gp = mm_nt("proj_gate", hx2, w_inT[4096:6144], F32).reshape(Bn, L, 2 * D)
    xbc_raw = mm_nt("proj_xbc", hx2, w_inT[6144:9216], F32).reshape(Bn, L, CONV_DIM)
    dt_raw = mm_nt("proj_dt", hx2, w_dt, F32)
    xbc_raw_c = mm_nt("proj_xbc_ctx", hc2, w_inT[6144:9216], F32).reshape(Bn, Lc, CONV_DIM)
    dt_raw_c = mm_nt("proj_dt_ctx", hc2, w_dt, F32)
    dtT = dt_raw[:, :64].reshape(Bn, L, 64).transpose(0, 2, 1)
    dtT_c = dt_raw_c[:, :64].reshape(Bn, Lc, 64).transpose(0, 2, 1)
    bias3 = dt_bias.reshape(2 * N_BC, HPG, 1)
    alog3 = a_log.reshape(2 * N_BC, HPG, 1)

    xbc = conv_fwd("conv_x", xbc_raw, conv_w_full, conv_b)
    xbc_c = conv_fwd("conv_ctx", xbc_raw_c, conv_w_full, conv_b)
    zero_state = jnp.zeros((Bn, N_BC, D_STATE, GW), F32)
    ys, hs_x, hs_c, hf_x, hf_c = [], [], [], [], []
    for d in range(2):
        hsc, hfc = ssd_fwd(f"ssd_fwd_ctx{d}", dtT_c, bias3, alog3, xbc_c, zero_state, d, False)
        y, hsx, hfx = ssd_fwd(f"ssd_fwd_x{d}", dtT, bias3, alog3, xbc, hfc, d, True)
        ys.append(y)
        hs_x.append(hsx)
        hs_c.append(hsc)
        hf_x.append(hfx)
        hf_c.append(hfc)

    dgs = [pool_diff(f"pool_diff{g}", v, g * POOL_GROUP, g, False) for g in range(4)]
    y_pool = pool_mix_fwd("pool_mix", dgs, zp, pool_full, pool_scale)
    dskip_lanes = jnp.repeat(d_skip[0], HEAD_DIM)[None, :]
    y_ssd = gated_norm_fwd("gated_norm", ys[0], ys[1], (xbc, D_INNER), zs, dskip_lanes, ssd_norm)
    merged, p1, p2, dout, g_res, dgate, g_norm_post, loss_part = merge_fwd(
        "merge_fwd", y_pool, y_ssd, gp, x, loss_target, gate, b_merge, norm_post, w_pp, w_ps, w_o)

    dp1, dp2, dgp, dyp, dys, g_b_merge = merge_bwd("merge_bwd", dout, gp, p1, p2, b_merge, w_pp, w_ps, w_o)
    gw_o = mm_tn("gw_out", merged.reshape(T, D), dout.reshape(T, D))
    gw_pp = mm_tn("gw_proj_pool", y_pool.reshape(T, D), dp1.reshape(T, D))
    gw_ps = mm_tn("gw_proj_ssd", y_ssd.reshape(T, D_INNER), dp2.reshape(T, D))

    *dds, dzp, g_pool, g_pool_scale = pool_mix_bwd("pool_mix_bwd", dgs, zp, dyp, pool_full, pool_scale)
    dvs = [pool_diff(f"pool_diff_t{g}", dds[g], 0, g, True) for g in range(4)]

    head_sel = (jnp.arange(D_INNER)[:, None] // HEAD_DIM == jnp.arange(LANES)[None, :]).astype(BF16)
    dy, dxs_skip, dzs, g_ssd_norm, g_dskip = gated_norm_bwd(
        "gated_norm_bwd", ys[0], ys[1], (xbc, D_INNER), zs, dys, dskip_lanes, ssd_norm, head_sel)

    dxs, dbm, dcm, ddt, dxs_c, dbm_c, ddt_c = [], [], [], [], [], [], []
    g_bias = jnp.zeros((2, N_BC, HPG, 1), F32)
    g_alog = jnp.zeros((2, N_BC, HPG, 1), F32)
    for d in range(2):
        a, b_, c_, t_, gb, ga, dh0 = ssd_bwd(f"ssd_bwd_x{d}", dtT, bias3, alog3, xbc, hs_x[d], dy, zero_state, d)
        dxs.append(a), dbm.append(b_), dcm.append(c_), ddt.append(t_)
        ac, bc, _, tc, gbc, gac, _ = ssd_bwd(f"ssd_bwd_ctx{d}", dtT_c, bias3, alog3, xbc_c, hs_c[d], None, dh0, d)
        dxs_c.append(ac), dbm_c.append(bc), ddt_c.append(tc)
        g_bias = g_bias.at[d].set(jnp.sum(gb, axis=0) + jnp.sum(gbc, axis=0))
        g_alog = g_alog.at[d].set(jnp.sum(ga, axis=0) + jnp.sum(gac, axis=0))

    dxr_xs, gcw_xs, gcb_xs = conv_bwd("conv_bwd_xs", xbc_raw, dxs + [dxs_skip], conv_w_full, conv_b, 0, D_INNER)
    dxr_b, gcw_b, gcb_b = conv_bwd("conv_bwd_b", xbc_raw, dbm, conv_w_full, conv_b, D_INNER, N_BC * D_STATE)
    dxr_c, gcw_c, gcb_c = conv_bwd("conv_bwd_c", xbc_raw, dcm, conv_w_full, conv_b, D_INNER + N_BC * D_STATE, N_BC * D_STATE)
    dxr_xs_c, gcw_xs_c, gcb_xs_c = conv_bwd("conv_bwd_xs_ctx", xbc_raw_c, dxs_c, conv_w_full, conv_b, 0, D_INNER)
    dxr_b_c, gcw_b_c, gcb_b_c = conv_bwd("conv_bwd_b_ctx", xbc_raw_c, dbm_c, conv_w_full, conv_b, D_INNER, N_BC * D_STATE)
    g_conv_w = jnp.concatenate([gcw_xs + gcw_xs_c, gcw_b + gcw_b_c, gcw_c], axis=1)
    g_conv_b = jnp.concatenate([gcb_xs + gcb_xs_c, gcb_b + gcb_b_c, gcb_c], axis=1)

    def dt_cols(parts, n_tok):
        t = jnp.concatenate(parts, axis=1).transpose(0, 2, 1).reshape(n_tok, 2 * N_HEADS)
        return jnp.pad(t, ((0, 0), (0, LANES - 2 * N_HEADS))).astype(BF16)

    ddt2, ddt2_c = dt_cols(ddt, T), dt_cols(ddt_c, Tc)
    segs = ([dv.reshape(T, POOL_GROUP) for dv in dvs]
            + [dzp.reshape(T, D), dzs.reshape(T, D_INNER), dgp.reshape(T, 2 * D), dxr_xs.reshape(T, D_INNER),
               dxr_b.reshape(T, N_BC * D_STATE), dxr_c.reshape(T, N_BC * D_STATE), ddt2])
    d_hx = mm_nn_multi("d_hx", list(zip(segs, w_seg)), F32).reshape(Bn, L, D)
    segs_c = {7: dxr_xs_c.reshape(Tc, D_INNER), 8: dxr_b_c.reshape(Tc, N_BC * D_STATE), 10: ddt2_c}
    d_hc = mm_nn_multi("d_hc", [(segs_c[i], w_seg[i]) for i in (7, 8, 10)], F32).reshape(Bn, Lc, D)
    gw_rows = []
    for i, seg in enumerate(segs):
        init = mm_tn(f"gw_in_ctx{i}", segs_c[i], hc2) if i in segs_c else None
        gw_rows.append(mm_tn(f"gw_in{i}", seg, hx2, init=init))
    gw_rows[-1] = gw_rows[-1][0:2 * N_HEADS]
    gw_inT = jnp.concatenate(gw_rows, axis=0)

    grad_x, dscale, dshift, g_npre_x = prenorm_bwd("prenorm_bwd_x", x, d_hx, scale, norm_pre, g_res=g_res)
    _, dscale_c, dshift_c, g_npre_c = prenorm_bwd("prenorm_bwd_ctx", ctx, d_hc, scale_c, norm_pre)

    dmod_x = jnp.concatenate([dshift[:, 0], dscale[:, 0], dgate[:, 0]], axis=1)
    dmod_c = jnp.concatenate([jnp.sum(dshift_c[:, 0], axis=0, keepdims=True), jnp.sum(dscale_c[:, 0], axis=0, keepdims=True),
                              jnp.zeros((1, D), F32)], axis=1)
    dmod_own = jnp.pad(dmod_x, ((0, 8 - Bn), (0, 0))) + jnp.pad(dmod_c, ((Bn, 7 - Bn), (0, 0)))
    dmod_all = all_gather_small("gather_dmod", dmod_own).reshape(8 * N_DEV, 3 * D)
    row_is_cctx = (jnp.arange(8 * N_DEV) % 8 == Bn).astype(F32)[:, None]
    g_w_ada, g_b_ada, cpart = ada_bwd_shard(
        cond_all, lax.dynamic_slice(dmod_all, (0, chip * ada_cols), (8 * N_DEV, ada_cols)), dmod_all, w_ada[0], row_is_cctx)
    g_c_ctx = cctx_finish(all_gather_small("gather_cctx", cpart), c_ctx[None, :])

    small_sizes = (D, D, 2 * D, D, CONV_DIM, 2 * N_HEADS, 2 * N_HEADS, N_HEADS, D_INNER, 4 * CONV_DIM, 1)
    pk = _pack([g_npre_x + g_npre_c, g_norm_post, g_b_merge, g_pool_scale, g_conv_b, g_bias, g_alog, g_dskip[0, 0:N_HEADS],
                g_ssd_norm, g_conv_w, loss_part[0, 0:1]], 184)
    small = sum_devices("sum_small", all_gather_small("gather_small", pk))
    (g_norm_pre, g_norm_post_t, g_b_merge_t, g_pool_scale_t, g_conv_b_t, g_dt_bias, g_a_log, g_d_skip, g_ssd_norm_t,
     g_conv_w_t, loss) = _unpack(small, small_sizes)
    g_conv_w_shard = lax.dynamic_slice(g_conv_w_t.reshape(4, CONV_DIM), (0, chip * cw_cols), (4, cw_cols))

    pool_slab = g_pool.reshape(4, N_CHIPS, 64, POOL_GROUP).transpose(1, 0, 2, 3).reshape(N_CHIPS, 64, D)
    slabs = jnp.concatenate([gw_inT.reshape(N_CHIPS, 2320, D), gw_pp.reshape(N_CHIPS, 256, D), gw_ps.reshape(N_CHIPS, 512, D),
                             gw_o.reshape(N_CHIPS, 256, D), pool_slab, jnp.zeros((N_CHIPS, W_SHARD_ROWS - SEG_ROWS[-1], D), F32)], axis=1)
    gsh = reduce_scatter_chips(slabs)
    g_w_in = gsh[SEG_ROWS[0]:SEG_ROWS[1]].T
    g_w_pp, g_w_ps, g_w_o = (gsh[SEG_ROWS[i]:SEG_ROWS[i + 1]] for i in (1, 2, 3))
    g_pool_w = gsh[SEG_ROWS[4]:SEG_ROWS[5]].reshape(256, POOL_GROUP)

    grads = {
        "c_ctx": g_c_ctx.reshape(c_ctx.shape), "w_ada": g_w_ada[None], "b_ada": g_b_ada, "norm_pre": g_norm_pre[None],
        "norm_post": g_norm_post_t[None], "w_in": g_w_in[None], "b_merge": g_b_merge_t[None],
        "pool_w": g_pool_w.reshape(pool_w.shape), "pool_scale": g_pool_scale_t[None], "conv_w": g_conv_w_shard[None],
        "conv_b": g_conv_b_t[None], "dt_bias": g_dt_bias.reshape(dt_bias.shape), "a_log": g_a_log.reshape(a_log.shape),
        "d_skip": g_d_skip[None], "ssd_norm": g_ssd_norm_t[None], "w_proj_pool": g_w_pp[None], "w_proj_ssd": g_w_ps[None],
        "w_out": g_w_o[None]}
    weights = dict(c_ctx=c_ctx, w_ada=w_ada, b_ada=b_ada, norm_pre=norm_pre, norm_post=norm_post, w_in=w_in, b_merge=b_merge,
                   pool_w=pool_w, pool_scale=pool_scale, conv_w=conv_w, conv_b=conv_b, dt_bias=dt_bias, a_log=a_log,
                   d_skip=d_skip, ssd_norm=ssd_norm, w_proj_pool=w_proj_pool, w_proj_ssd=w_proj_ssd, w_out=w_out)
    m_in = dict(c_ctx=m_c_ctx, w_ada=m_w_ada, b_ada=m_b_ada, norm_pre=m_norm_pre, norm_post=m_norm_post, w_in=m_w_in,
                b_merge=m_b_merge, pool_w=m_pool_w, pool_scale=m_pool_scale, conv_w=m_conv_w, conv_b=m_conv_b,
                dt_bias=m_dt_bias, a_log=m_a_log, d_skip=m_d_skip, ssd_norm=m_ssd_norm, w_proj_pool=m_w_proj_pool,
                w_proj_ssd=m_w_proj_ssd, w_out=m_w_out)
    v_in = dict(c_ctx=v_c_ctx, w_ada=v_w_ada, b_ada=v_b_ada, norm_pre=v_norm_pre, norm_post=v_norm_post, w_in=v_w_in,
                b_merge=v_b_merge, pool_w=v_pool_w, pool_scale=v_pool_scale, conv_w=v_conv_w, conv_b=v_conv_b,
                dt_bias=v_dt_bias, a_log=v_a_log, d_skip=v_d_skip, ssd_norm=v_ssd_norm, w_proj_pool=v_w_proj_pool,
                w_proj_ssd=v_w_proj_ssd, w_out=v_w_out)
    names = list(weights)
    big = ("w_ada", "w_in", "pool_w", "w_proj_pool", "w_proj_ssd", "w_out")
    small_names = [n for n in names if n not in big]
    delta, new_m, new_v = {}, {}, {}
    for n in big:
        shape2 = (-1, weights[n].shape[-1])
        d_, m_, v_ = adamw(f"adamw_{n}", weights[n].reshape(shape2), grads[n].reshape(shape2), m_in[n].reshape(shape2),
                           v_in[n].reshape(shape2), tr=128)
        delta[n], new_m[n], new_v[n] = (t.reshape(weights[n].shape) for t in (d_, m_, v_))
    sizes = [weights[n].size for n in small_names]
    packed = [_pack([src[n] for n in small_names], 144) for src in (weights, grads, m_in, v_in)]
    outs = adamw("adamw_small", *packed, tr=144)
    for res, store in zip(outs, (delta, new_m, new_v)):
        for n, piece in zip(small_names, _unpack(res, sizes)):
            store[n] = piece.reshape(weights[n].shape)

    return (loss.reshape(()), grad_x, *[grads[n] for n in names], *[delta[n] for n in names],
            *[new_m[n] for n in names], *[new_v[n] for n in names])
```

```python
import jax
import jax.numpy as jnp
from jax import lax
from jax.experimental import pallas as pl
from jax.experimental.pallas import tpu as pltpu

F32 = jnp.float32
BF16 = jnp.bfloat16
MESH = pl.DeviceIdType.MESH

D = 1024
GRID_W = 64
NORM_EPS = 1e-6
POOL_WINDOWS = (2, 4, 8, 16)
POOL_GROUP = 256
D_INNER = 2048
HEAD_DIM = 64
N_HEADS = 32
D_STATE = 128
N_BC = 4
HPG = N_HEADS // N_BC
GW = HPG * HEAD_DIM
CONV_DIM = 3072
CHUNK = 128
OFF_XBC = 6144
IN_COLS = 9280
N_CHIPS = 4
N_DEV = 8

ADAM_LR = 0.001
ADAM_B1 = 0.9
ADAM_B2 = 0.999
ADAM_EPS = 1e-08
ADAM_WD = 0.01
ADAM_STEP = 10

V7X_VMEM_BYTES = 64 * 1024 * 1024
VMEM_LIMIT = V7X_VMEM_BYTES * 3 // 4
LANES = 128


def _cp(sem=None):
    return pltpu.CompilerParams(dimension_semantics=sem, vmem_limit_bytes=VMEM_LIMIT)


def _dot(a, b):
    return jnp.dot(a, b, preferred_element_type=F32)


def _dot_nt(a, b):
    return lax.dot_general(a, b, (((1,), (1,)), ((), ())), preferred_element_type=F32)


def _dot_tn(a, b):
    return lax.dot_general(a, b, (((0,), (0,)), ((), ())), preferred_element_type=F32)


def _split3(x):
    hi = x.astype(BF16)
    r1 = x - hi.astype(F32)
    mid = r1.astype(BF16)
    lo = (r1 - mid.astype(F32)).astype(BF16)
    return hi, mid, lo


def _sigmoid(x):
    return jax.nn.sigmoid(x)


def _silu(x):
    return x * _sigmoid(x)


def _dsilu(x):
    s = _sigmoid(x)
    return s * (1.0 + x * (1.0 - s))


def _softplus(x):
    return jnp.maximum(x, 0.0) + jnp.log(1.0 + jnp.exp(-jnp.abs(x)))


def mm_nt(name, a, b, out_dtype, tm=1024, tn=512):
    M, K = a.shape
    N = b.shape[0]
    tm, tn = min(tm, M), min(tn, N)
    assert M % tm == 0 and N % tn == 0, (M, N, tm, tn)

    def body(a_ref, b_ref, o_ref):
        o_ref[...] = _dot_nt(a_ref[...], b_ref[...]).astype(o_ref.dtype)

    return pl.pallas_call(
        body, name=name, out_shape=jax.ShapeDtypeStruct((M, N), out_dtype), grid=(M // tm, N // tn),
        in_specs=[pl.BlockSpec((tm, K), lambda i, j: (i, 0)), pl.BlockSpec((tn, K), lambda i, j: (j, 0))],
        out_specs=pl.BlockSpec((tm, tn), lambda i, j: (i, j)),
        compiler_params=_cp(("parallel", "arbitrary")))(a, b)


def mm_tn(name, a, b, init=None, tm=1024, tn=1024, tk=512):
    T, M = a.shape
    N = b.shape[1]
    tm, tn, tk = min(tm, M), min(tn, N), min(tk, T)
    assert M % tm == 0 and N % tn == 0 and T % tk == 0, (M, N, T)
    has_init = init is not None

    def body(*refs):
        if has_init:
            a_ref, b_ref, i_ref, o_ref = refs
        else:
            a_ref, b_ref, o_ref = refs
        k = pl.program_id(2)

        @pl.when(k == 0)
        def _():
            o_ref[...] = i_ref[...] if has_init else jnp.zeros(o_ref.shape, F32)

        o_ref[...] += _dot_tn(a_ref[...], b_ref[...])

    in_specs = [pl.BlockSpec((tk, tm), lambda i, j, k: (k, i)), pl.BlockSpec((tk, tn), lambda i, j, k: (k, j))]
    args = [a, b]
    if has_init:
        in_specs.append(pl.BlockSpec((tm, tn), lambda i, j, k: (i, j)))
        args.append(init)
    return pl.pallas_call(
        body, name=name, out_shape=jax.ShapeDtypeStruct((M, N), F32), grid=(M // tm, N // tn, T // tk),
        in_specs=in_specs, out_specs=pl.BlockSpec((tm, tn), lambda i, j, k: (i, j)),
        compiler_params=_cp(("parallel", "parallel", "arbitrary")))(*args)


def mm_nn_multi(name, pairs, out_dtype, tm=512, tk=512):
    M = pairs[0][0].shape[0]
    N = pairs[0][1].shape[1]
    tm = min(tm, M)
    assert M % tm == 0
    plan = []
    step = 0
    for a, b in pairs:
        K = a.shape[1]
        t = min(tk, K)
        assert K % t == 0 and b.shape == (K, N)
        plan.append((t, step, K // t))
        step += K // t
    nsteps = step
    npairs = len(pairs)

    def body(*refs):
        o_ref, acc = refs[2 * npairs], refs[2 * npairs + 1]
        k = pl.program_id(1)

        @pl.when(k == 0)
        def _():
            acc[...] = jnp.zeros(acc.shape, F32)

        for p, (_, first, n) in enumerate(plan):
            @pl.when((k >= first) & (k < first + n))
            def _(p=p):
                acc[...] += _dot(refs[2 * p][...], refs[2 * p + 1][...])

        @pl.when(k == nsteps - 1)
        def _():
            o_ref[...] = acc[...].astype(o_ref.dtype)

    in_specs, args = [], []
    for (a, b), (t, first, n) in zip(pairs, plan):
        in_specs.append(pl.BlockSpec((tm, t), lambda i, k, first=first, n=n: (i, jnp.clip(k - first, 0, n - 1))))
        in_specs.append(pl.BlockSpec((t, N), lambda i, k, first=first, n=n: (jnp.clip(k - first, 0, n - 1), 0)))
        args += [a, b]
    return pl.pallas_call(
        body, name=name, out_shape=jax.ShapeDtypeStruct((M, N), out_dtype), grid=(M // tm, nsteps),
        in_specs=in_specs, out_specs=pl.BlockSpec((tm, N), lambda i, k: (i, 0)),
        scratch_shapes=[pltpu.VMEM((tm, N), F32)],
        compiler_params=_cp(("parallel", "arbitrary")))(*args)


def tok_call(name, body, tiled, perb, glob, out_tiled, out_perb, out_glob, tm=256):
    widths = [t[1] if isinstance(t, tuple) else t.shape[2] for t in tiled]
    tiled = [t[0] if isinstance(t, tuple) else t for t in tiled]
    Bn, L = tiled[0].shape[:2]
    tm = min(tm, L)
    assert L % tm == 0
    n_t, n_p, n_g = len(tiled), len(perb), len(glob)
    o_t, o_p, o_g = len(out_tiled), len(out_perb), len(out_glob)
    n_in = n_t + n_p + n_g

    def kern(*refs):
        ins, outs = refs[:n_in], refs[n_in:]
        b, j = pl.program_id(0), pl.program_id(1)
        vals = [r[0] for r in ins[:n_t + n_p]] + [r[...] for r in ins[n_t + n_p:]]
        res = body(*vals)
        if not isinstance(res, (tuple, list)):
            res = (res,)
        assert len(res) == o_t + o_p + o_g, (name, len(res))
        for r, v in zip(outs[:o_t], res[:o_t]):
            r[0] = v.astype(r.dtype)

        def accum(r, v, first, lead):
            @pl.when(first)
            def _():
                r[...] = jnp.zeros(r.shape, F32)
            if lead:
                r[0] += v
            else:
                r[...] += v

        for r, v in zip(outs[o_t:o_t + o_p], res[o_t:o_t + o_p]):
            accum(r, v, j == 0, True)
        for r, v in zip(outs[o_t + o_p:], res[o_t + o_p:]):
            accum(r, v, (j == 0) & (b == 0), False)

    in_specs = ([pl.BlockSpec((1, tm, w), lambda b, j: (b, j, 0)) for w in widths]
                + [pl.BlockSpec((1, 1, a.shape[2]), lambda b, j: (b, 0, 0)) for a in perb]
                + [pl.BlockSpec(a.shape, lambda b, j: (0, 0)) for a in glob])
    out_shape = ([jax.ShapeDtypeStruct((Bn, L, w), dt) for w, dt in out_tiled]
                 + [jax.ShapeDtypeStruct((Bn, 1, w), F32) for w in out_perb]
                 + [jax.ShapeDtypeStruct(s, F32) for s in out_glob])
    out_specs = ([pl.BlockSpec((1, tm, w), lambda b, j: (b, j, 0)) for w, _ in out_tiled]
                 + [pl.BlockSpec((1, 1, w), lambda b, j: (b, 0, 0)) for w in out_perb]
                 + [pl.BlockSpec(s, lambda b, j: (0, 0)) for s in out_glob])
    return pl.pallas_call(
        kern, name=name, out_shape=out_shape, grid=(Bn, L // tm), in_specs=in_specs, out_specs=out_specs,
        compiler_params=_cp(("arbitrary", "arbitrary")))(*tiled, *perb, *glob)


def slab_call(name, body, slabs, colparams, out_slabs, out_colred, wc=LANES):
    Bn, L = slabs[0][0].shape[:2]
    w_out = out_slabs[0][0]
    assert w_out % wc == 0 and all(off % wc == 0 for _, off in slabs + colparams)
    n_col = w_out // wc
    n_s, n_c = len(slabs), len(colparams)
    o_s = len(out_slabs)

    def kern(*refs):
        ins, outs = refs[:n_s + n_c], refs[n_s + n_c:]
        b = pl.program_id(1)
        vals = [r[0] for r in ins[:n_s]] + [r[...] for r in ins[n_s:]]
        res = body(*vals)
        if not isinstance(res, (tuple, list)):
            res = (res,)
        assert len(res) == o_s + len(out_colred), name
        for r, v in zip(outs[:o_s], res[:o_s]):
            r[0] = v.astype(r.dtype)

        def accum(r, v):
            @pl.when(b == 0)
            def _():
                r[...] = jnp.zeros(r.shape, F32)
            r[...] += v

        for r, v in zip(outs[o_s:], res[o_s:]):
            accum(r, v)

    in_specs = ([pl.BlockSpec((1, L, wc), lambda j, b, o=off // wc: (b, 0, o + j)) for _, off in slabs]
                + [pl.BlockSpec((a.shape[0], wc), lambda j, b, o=off // wc: (0, o + j)) for a, off in colparams])
    out_shape = ([jax.ShapeDtypeStruct((Bn, L, w), dt) for w, dt in out_slabs]
                 + [jax.ShapeDtypeStruct((r, w_out), F32) for r in out_colred])
    out_specs = ([pl.BlockSpec((1, L, wc), lambda j, b: (b, 0, j)) for _ in out_slabs]
                 + [pl.BlockSpec((r, wc), lambda j, b: (0, j)) for r in out_colred])
    return pl.pallas_call(
        kern, name=name, out_shape=out_shape, grid=(n_col, Bn), in_specs=in_specs, out_specs=out_specs,
        compiler_params=_cp(("arbitrary", "arbitrary")))(*[a for a, _ in slabs], *[a for a, _ in colparams])


def _rms_r(x):
    return lax.rsqrt(jnp.mean(x * x, axis=-1, keepdims=True) + NORM_EPS)


def _rms_bwd(dxh, x, r):
    return r * (dxh - x * (r * r) * jnp.mean(dxh * x, axis=-1, keepdims=True))


def _colsum(v):
    return jnp.sum(v, axis=0, keepdims=True)


def _stack_rows(rows):
    n, w = len(rows), rows[0].shape[1]
    sub = lax.broadcasted_iota(jnp.int32, (n, w), 0)
    acc = jnp.zeros((n, w), F32)
    for r, row in enumerate(rows):
        acc = acc + jnp.where(sub == r, jnp.broadcast_to(row, (n, w)), 0.0)
    return acc


def prenorm_fwd(name, x, scale, shift, w_pre):
    def body(x, scale, shift, w):
        n = x * _rms_r(x) * w
        return n * (1.0 + scale) + shift

    return tok_call(name, body, [x], [scale, shift], [w_pre], [(D, BF16)], [], [])[0]


def prenorm_bwd(name, x, dhx, scale, w_pre, g_res=None):
    has_res = g_res is not None

    def body(*v):
        if has_res:
            x, dhx, g, scale, w = v
        else:
            x, dhx, scale, w = v
        r = _rms_r(x)
        xr = x * r
        n = xr * w
        dn = dhx * (1.0 + scale)
        dx = _rms_bwd(dn * w, x, r)
        if has_res:
            dx = dx + g
        return dx, _colsum(dhx * n), _colsum(dhx), _colsum(dn * xr)

    tiled = [x, dhx] + ([g_res] if has_res else [])
    return tok_call(name, body, tiled, [scale], [w_pre], [(D, F32)], [D, D], [(1, D)])


def _shift_rows(x, o, tok, L):
    if o == 0:
        return x
    rolled = pltpu.roll(x, (-o) % L, 0)
    return jnp.where((tok + o >= 0) & (tok + o < L), rolled, 0.0)


def conv_fwd(name, xbc_raw, conv_w, conv_b):
    L = xbc_raw.shape[1]

    def body(x, w, b):
        tok = lax.broadcasted_iota(jnp.int32, x.shape, 0)
        pre = b
        for k in range(4):
            pre = pre + _shift_rows(x, k - 2, tok, L) * w[k:k + 1]
        return _silu(pre)

    return slab_call(name, body, [(xbc_raw, 0)], [(conv_w, 0), (conv_b, 0)], [(CONV_DIM, F32)], [])[0]


def conv_bwd(name, xbc_raw, dparts, conv_w, conv_b, col0, width, scaled=None):
    L = xbc_raw.shape[1]
    n_d = len(dparts) + (1 if scaled is not None else 0)

    def body(*v):
        x, ds, w, b = v[0], v[1:1 + n_d], v[1 + n_d], v[2 + n_d]
        tok = lax.broadcasted_iota(jnp.int32, x.shape, 0)
        taps = [_shift_rows(x, k - 2, tok, L) for k in range(4)]
        pre = b
        for k in range(4):
            pre = pre + taps[k] * w[k:k + 1]
        dy = ds[0] * v[3 + n_d] if scaled is not None else ds[0]
        for extra in ds[1:]:
            dy = dy + extra
        dpre = dy * _dsilu(pre)
        dx = jnp.zeros_like(x)
        for k in range(4):
            dx = dx + _shift_rows(dpre, 2 - k, tok, L) * w[k:k + 1]
        dw = _stack_rows([_colsum(dpre * taps[k]) for k in range(4)])
        return dx, dw, _colsum(dpre)

    slabs = [(xbc_raw, col0)] + ([(scaled[0], 0)] if scaled is not None else []) + [(d, 0) for d in dparts]
    colparams = [(conv_w, col0), (conv_b, col0)] + ([(scaled[1], 0)] if scaled is not None else [])
    return slab_call(name, body, slabs, colparams, [(width, BF16)], [4, 1])


def _box_mean(x, k, step, pos, n, L, transpose):
    lo, hi = k // 2, k - 1 - k // 2
    cnt = (jnp.minimum(pos + hi + 1, n) - jnp.maximum(pos - lo, 0)).astype(F32)
    if transpose:
        x = x / cnt
        lo, hi = hi, lo
    acc = x
    for o in range(-lo, hi + 1):
        if o == 0:
            continue
        rolled = pltpu.roll(x, (-o * step) % L, 0)
        acc = acc + jnp.where((pos + o >= 0) & (pos + o < n), rolled, 0.0)
    return acc if transpose else acc / cnt


def pool_diff(name, v, col0, gi, transpose):
    L = v.shape[1]
    rows = L // GRID_W
    k = POOL_WINDOWS[gi]

    def body(x):
        tok = lax.broadcasted_iota(jnp.int32, x.shape, 0)
        col = tok & (GRID_W - 1)
        row = tok >> 6
        if not transpose:
            m = _box_mean(x, k, GRID_W, row, rows, L, False)
            m = _box_mean(m, k, 1, col, GRID_W, L, False)
        else:
            m = _box_mean(x, k, 1, col, GRID_W, L, True)
            m = _box_mean(m, k, GRID_W, row, rows, L, True)
        return m - x

    return slab_call(name, body, [(v, col0)], [], [(POOL_GROUP, BF16)], [])[0]


def pool_mix_fwd(name, dgs, z_pool, pool_w, pool_scale):
    def body(d0, d1, d2, d3, z, w, scale):
        q = jnp.concatenate([_dot(d, w[g * POOL_GROUP:(g + 1) * POOL_GROUP]) for g, d in enumerate((d0, d1, d2, d3))], axis=1)
        return q * scale * _silu(z)

    return tok_call(name, body, list(dgs) + [z_pool], [], [pool_w, pool_scale], [(D, BF16)], [], [])[0]


def pool_mix_bwd(name, dgs, z_pool, dyp, pool_w, pool_scale):
    def body(d0, d1, d2, d3, z, dyp, w, scale):
        ds = (d0, d1, d2, d3)
        q = jnp.concatenate([_dot(d, w[g * POOL_GROUP:(g + 1) * POOL_GROUP]) for g, d in enumerate(ds)], axis=1)
        dypm = dyp * _silu(z)
        dz = dyp * (q * scale) * _dsilu(z)
        dq = (dypm * scale).astype(BF16)
        dds, gws = [], []
        for g, d in enumerate(ds):
            dqg = dq[:, g * POOL_GROUP:(g + 1) * POOL_GROUP]
            dds.append(_dot_nt(dqg, w[g * POOL_GROUP:(g + 1) * POOL_GROUP]))
            gws.append(_dot_tn(d, dqg))
        return (*dds, dz, jnp.concatenate(gws, axis=0), _colsum(dypm * q))

    return tok_call(name, body, list(dgs) + [z_pool, dyp], [], [pool_w, pool_scale],
                    [(POOL_GROUP, F32)] * 4 + [(D, BF16)], [], [(D, POOL_GROUP), (1, D)])


def _cumsum_lanes(a, reverse):
    n = a.shape[1]
    k = lax.broadcasted_iota(jnp.int32, (n, n), 0)
    i = lax.broadcasted_iota(jnp.int32, (n, n), 1)
    tri = jnp.where((k >= i) if reverse else (k <= i), 1.0, 0.0).astype(BF16)
    return _dot_exact01(a, tri)


def _rows_to_cols(rows):
    r = rows.shape[0]
    if r < LANES:
        rows = jnp.concatenate([rows, jnp.zeros((LANES - r, rows.shape[1]), F32)], axis=0)
    return rows.T


def _cols_to_rows(cols):
    q = cols[0].shape[0]
    lane = lax.broadcasted_iota(jnp.int32, (q, LANES), 1)
    acc = jnp.zeros((q, LANES), F32)
    for r, c in enumerate(cols):
        acc = acc + jnp.where(lane == r, c, 0.0)
    return acc.T[0:len(cols)]


def _ssd_scalars(dtraw, bias, alog, reverse):
    dt = _softplus(dtraw + bias)
    A = -jnp.exp(alog)
    cs = _cumsum_lanes(dt * A, reverse)
    total = cs[:, 0:1] if reverse else cs[:, CHUNK - 1:CHUNK]
    return dt, A, cs, total


def _decay_matrix(cs_col, cs_row, reverse):
    i = lax.broadcasted_iota(jnp.int32, (CHUNK, CHUNK), 0)
    j = lax.broadcasted_iota(jnp.int32, (CHUNK, CHUNK), 1)
    keep = (i <= j) if reverse else (i >= j)
    return jnp.exp(jnp.where(keep, cs_col - cs_row, -jnp.inf))


def ssd_fwd_v1(name, dtT, bias, alog, xbc, h0, direction, with_y):
    Bn, L = xbc.shape[:2]
    nc = L // CHUNK
    reverse = direction == 1
    rowblk = direction * N_BC

    def chunk_of(s):
        return (nc - 1 - s) if reverse else s

    def kern(dt_ref, bias_ref, alog_ref, x_ref, b_ref, c_ref, h0_ref, *rest):
        if with_y:
            y_ref, hs_ref, hf_ref, h_scr, xt_scr = rest
        else:
            hs_ref, hf_ref, h_scr, xt_scr = rest
        s = pl.program_id(2)

        @pl.when(s == 0)
        def _():
            h_scr[...] = h0_ref[0, 0]

        dt, _, cs, total = _ssd_scalars(dt_ref[0], bias_ref[0], alog_ref[0], reverse)
        e_row = jnp.exp(cs)
        t_row = jnp.exp(total - cs)
        dc = jnp.exp(total)
        cols = _rows_to_cols(jnp.concatenate([dt, e_row, t_row, cs], axis=0))
        x = x_ref[0]
        bm = b_ref[0].astype(BF16)
        cm = c_ref[0].astype(BF16)
        h = h_scr[...]
        hs_ref[0, 0, 0] = h
        if with_y:
            cb = _dot_nt(cm, bm)
            yoff = _dot(cm, h.astype(BF16))
        for r in range(HPG):
            sl = slice(r * HEAD_DIM, (r + 1) * HEAD_DIM)
            xdt = x[:, sl] * cols[:, r:r + 1]
            if with_y:
                lr = _decay_matrix(cols[:, 3 * HPG + r:3 * HPG + r + 1], cs[r:r + 1], reverse)
                ydiag = _dot((cb * lr).astype(BF16), xdt.astype(BF16))
                y_ref[0, :, sl] = ydiag + yoff[:, sl] * cols[:, HPG + r:HPG + r + 1]
            xt_scr[:, sl] = (xdt * cols[:, 2 * HPG + r:2 * HPG + r + 1]).astype(BF16)
        st = _dot_tn(bm, xt_scr[...])
        for r in range(HPG):
            sl = slice(r * HEAD_DIM, (r + 1) * HEAD_DIM)
            h_scr[:, sl] = h[:, sl] * dc[r:r + 1] + st[:, sl]

        @pl.when(s == nc - 1)
        def _():
            hf_ref[0, 0] = h_scr[...]

    in_specs = [
        pl.BlockSpec((1, HPG, CHUNK), lambda b, g, s: (b, rowblk + g, chunk_of(s))),
        pl.BlockSpec((1, HPG, 1), lambda b, g, s: (rowblk + g, 0, 0)),
        pl.BlockSpec((1, HPG, 1), lambda b, g, s: (rowblk + g, 0, 0)),
        pl.BlockSpec((1, CHUNK, GW), lambda b, g, s: (b, chunk_of(s), g)),
        pl.BlockSpec((1, CHUNK, D_STATE), lambda b, g, s: (b, chunk_of(s), D_INNER // D_STATE + g)),
        pl.BlockSpec((1, CHUNK, D_STATE), lambda b, g, s: (b, chunk_of(s), D_INNER // D_STATE + N_BC + g)),
        pl.BlockSpec((1, 1, D_STATE, GW), lambda b, g, s: (b, g, 0, 0)),
    ]
    out_shape, out_specs = [], []
    if with_y:
        out_shape.append(jax.ShapeDtypeStruct((Bn, L, D_INNER), F32))
        out_specs.append(pl.BlockSpec((1, CHUNK, GW), lambda b, g, s: (b, chunk_of(s), g)))
    out_shape += [jax.ShapeDtypeStruct((Bn, N_BC, nc, D_STATE, GW), F32), jax.ShapeDtypeStruct((Bn, N_BC, D_STATE, GW), F32)]
    out_specs += [pl.BlockSpec((1, 1, 1, D_STATE, GW), lambda b, g, s: (b, g, chunk_of(s), 0, 0)),
                  pl.BlockSpec((1, 1, D_STATE, GW), lambda b, g, s: (b, g, 0, 0))]
    return pl.pallas_call(
        kern, name=name, out_shape=out_shape, grid=(Bn, N_BC, nc), in_specs=in_specs, out_specs=out_specs,
        scratch_shapes=[pltpu.VMEM((D_STATE, GW), F32), pltpu.VMEM((CHUNK, GW), BF16)],
        compiler_params=_cp(("arbitrary", "arbitrary", "arbitrary")))(dtT, bias, alog, xbc, xbc, xbc, h0)


def ssd_bwd_v1(name, dtT, bias, alog, xbc, h_start, dy, dh_final, direction):
    Bn, L = xbc.shape[:2]
    nc = L // CHUNK
    reverse = direction == 1
    rowblk = direction * N_BC
    has_y = dy is not None
    last = 0 if reverse else CHUNK - 1

    def chunk_of(s):
        return s if reverse else (nc - 1 - s)

    def kern(*refs):
        if has_y:
            (dt_ref, bias_ref, alog_ref, x_ref, b_ref, c_ref, hs_ref, dhf_ref, dy_ref,
             dx_ref, db_ref, dc_ref, ddt_ref, dbias_ref, dalog_ref, dh0_ref, dh_scr, e_scr, t_scr) = refs
        else:
            (dt_ref, bias_ref, alog_ref, x_ref, b_ref, hs_ref, dhf_ref,
             dx_ref, db_ref, ddt_ref, dbias_ref, dalog_ref, dh0_ref, dh_scr, t_scr) = refs
        s = pl.program_id(2)

        @pl.when(s == 0)
        def _():
            dh_scr[...] = dhf_ref[0, 0]
            dbias_ref[...] = jnp.zeros(dbias_ref.shape, F32)
            dalog_ref[...] = jnp.zeros(dalog_ref.shape, F32)

        dtraw = dt_ref[0]
        dt, A, cs, total = _ssd_scalars(dtraw, bias_ref[0], alog_ref[0], reverse)
        e_row = jnp.exp(cs)
        t_row = jnp.exp(total - cs)
        dcy = jnp.exp(total)
        cols = _rows_to_cols(jnp.concatenate([dt, e_row, t_row, cs], axis=0))
        x = x_ref[0]
        bm = b_ref[0].astype(BF16)
        h = hs_ref[0, 0, 0]
        dh = dh_scr[...]
        dh_bf = dh.astype(BF16)
        bdh = _dot(bm, dh_bf)
        if has_y:
            cm = c_ref[0].astype(BF16)
            dyv = dy_ref[0]
            cb = _dot_nt(cm, bm)
            yoff = _dot(cm, h.astype(BF16))
            dcb = jnp.zeros((CHUNK, CHUNK), F32)
        col_terms, row_terms, ddt_cols, dtot = [], [], [], []
        for r in range(HPG):
            sl = slice(r * HEAD_DIM, (r + 1) * HEAD_DIM)
            dt_c = cols[:, r:r + 1]
            e_c = cols[:, HPG + r:HPG + r + 1]
            t_c = cols[:, 2 * HPG + r:2 * HPG + r + 1]
            xr = x[:, sl]
            xdt = xr * dt_c
            dxdt = t_c * bdh[:, sl]
            d_t = jnp.sum(bdh[:, sl] * xdt, axis=1, keepdims=True)
            col = -(t_c * d_t)
            tot = jnp.sum(t_c * d_t, axis=0, keepdims=True) + dcy[r:r + 1] * jnp.sum(h[:, sl] * dh[:, sl], keepdims=True)
            if has_y:
                dyr = dyv[:, sl]
                lr = _decay_matrix(cols[:, 3 * HPG + r:3 * HPG + r + 1], cs[r:r + 1], reverse)
                w = cb * lr
                gm = _dot_nt(dyr.astype(BF16), xdt.astype(BF16))
                m = gm * w
                dcb = dcb + gm * lr
                dxdt = dxdt + _dot_tn(w.astype(BF16), dyr.astype(BF16))
                col = col + jnp.sum(m, axis=1, keepdims=True) + jnp.sum(yoff[:, sl] * dyr, axis=1, keepdims=True) * e_c
                row_terms.append(-jnp.sum(m, axis=0, keepdims=True))
                e_scr[:, sl] = (e_c * dyr).astype(BF16)
            t_scr[:, sl] = (t_c * xdt).astype(BF16)
            dx_ref[0, :, sl] = dxdt * dt_c
            ddt_cols.append(jnp.sum(dxdt * xr, axis=1, keepdims=True))
            col_terms.append(col)
            dtot.append(tot)
        db = _dot_nt(t_scr[...], dh_bf)
        if has_y:
            dcb_bf = dcb.astype(BF16)
            db = db + _dot_tn(dcb_bf, cm)
            dc_ref[0] = _dot(dcb_bf, bm) + _dot_nt(e_scr[...], h.astype(BF16))
            cte = _dot_tn(cm, e_scr[...])
        db_ref[0] = db
        for r in range(HPG):
            sl = slice(r * HEAD_DIM, (r + 1) * HEAD_DIM)
            new = dh[:, sl] * dcy[r:r + 1]
            if has_y:
                new = new + cte[:, sl]
            dh_scr[:, sl] = new
        dcs = _cols_to_rows(col_terms)
        if has_y:
            dcs = dcs + _stack_rows(row_terms)
        lane = lax.broadcasted_iota(jnp.int32, (HPG, CHUNK), 1)
        dcs = dcs + jnp.where(lane == last, _stack_rows([jnp.broadcast_to(t, (1, CHUNK)) for t in dtot]), 0.0)
        da = _cumsum_lanes(dcs, not reverse)
        ddt = da * A + _cols_to_rows(ddt_cols)
        ddtraw = ddt * _sigmoid(dtraw + bias_ref[0])
        ddt_ref[0] = ddtraw
        dbias_ref[0, 0] += jnp.sum(ddtraw, axis=1, keepdims=True)
        dalog_ref[0, 0] += jnp.sum(da * dt, axis=1, keepdims=True) * A

        @pl.when(s == nc - 1)
        def _():
            dh0_ref[0, 0] = dh_scr[...]

    cidx = lambda b, g, s: (b, chunk_of(s), g)
    in_specs = [
        pl.BlockSpec((1, HPG, CHUNK), lambda b, g, s: (b, rowblk + g, chunk_of(s))),
        pl.BlockSpec((1, HPG, 1), lambda b, g, s: (rowblk + g, 0, 0)),
        pl.BlockSpec((1, HPG, 1), lambda b, g, s: (rowblk + g, 0, 0)),
        pl.BlockSpec((1, CHUNK, GW), cidx),
        pl.BlockSpec((1, CHUNK, D_STATE), lambda b, g, s: (b, chunk_of(s), D_INNER // D_STATE + g)),
    ]
    args = [dtT, bias, alog, xbc, xbc]
    if has_y:
        in_specs.append(pl.BlockSpec((1, CHUNK, D_STATE), lambda b, g, s: (b, chunk_of(s), D_INNER // D_STATE + N_BC + g)))
        args.append(xbc)
    in_specs += [pl.BlockSpec((1, 1, 1, D_STATE, GW), lambda b, g, s: (b, g, chunk_of(s), 0, 0)),
                 pl.BlockSpec((1, 1, D_STATE, GW), lambda b, g, s: (b, g, 0, 0))]
    args += [h_start, dh_final]
    if has_y:
        in_specs.append(pl.BlockSpec((1, CHUNK, GW), cidx))
        args.append(dy)
    out_shape = [jax.ShapeDtypeStruct((Bn, L, D_INNER), F32), jax.ShapeDtypeStruct((Bn, L, N_BC * D_STATE), F32)]
    out_specs = [pl.BlockSpec((1, CHUNK, GW), cidx), pl.BlockSpec((1, CHUNK, D_STATE), cidx)]
    if has_y:
        out_shape.append(jax.ShapeDtypeStruct((Bn, L, N_BC * D_STATE), F32))
        out_specs.append(pl.BlockSpec((1, CHUNK, D_STATE), cidx))
    out_shape += [jax.ShapeDtypeStruct((Bn, N_HEADS, L), F32), jax.ShapeDtypeStruct((Bn, N_BC, HPG, 1), F32),
                  jax.ShapeDtypeStruct((Bn, N_BC, HPG, 1), F32), jax.ShapeDtypeStruct((Bn, N_BC, D_STATE, GW), F32)]
    out_specs += [pl.BlockSpec((1, HPG, CHUNK), lambda b, g, s: (b, g, chunk_of(s))),
                  pl.BlockSpec((1, 1, HPG, 1), lambda b, g, s: (b, g, 0, 0)),
                  pl.BlockSpec((1, 1, HPG, 1), lambda b, g, s: (b, g, 0, 0)),
                  pl.BlockSpec((1, 1, D_STATE, GW), lambda b, g, s: (b, g, 0, 0))]
    scratch = [pltpu.VMEM((D_STATE, GW), F32)] + ([pltpu.VMEM((CHUNK, GW), BF16)] if has_y else []) + [pltpu.VMEM((CHUNK, GW), BF16)]
    res = pl.pallas_call(
        kern, name=name, out_shape=out_shape, grid=(Bn, N_BC, nc), in_specs=in_specs, out_specs=out_specs,
        scratch_shapes=scratch, compiler_params=_cp(("arbitrary", "arbitrary", "arbitrary")))(*args)
    if has_y:
        return res
    dxs, db, ddt, dbias, dalog, dh0 = res
    return dxs, db, None, ddt, dbias, dalog, dh0


def _tri_mask(transposed, reverse):
    sub = lax.broadcasted_iota(jnp.int32, (CHUNK, CHUNK), 0)
    lane = lax.broadcasted_iota(jnp.int32, (CHUNK, CHUNK), 1)
    i, j = (lane, sub) if transposed else (sub, lane)
    return (i <= j) if reverse else (i >= j)


def ssd_fwd(name, dtT, bias, alog, xbc, h0, direction, with_y):
    Bn, L = xbc.shape[:2]
    nc = L // CHUNK
    reverse = direction == 1
    rowblk = direction * N_BC

    def chunk_of(s):
        return (nc - 1 - s) if reverse else s

    def kern(dt_ref, bias_ref, alog_ref, x_ref, b_ref, c_ref, h0_ref, *rest):
        if with_y:
            y_ref, hs_ref, hf_ref, h_scr = rest
        else:
            hs_ref, hf_ref, h_scr = rest
        s = pl.program_id(2)

        @pl.when(s == 0)
        def _():
            h_scr[...] = h0_ref[0, 0]

        dt, _, cs, total = _ssd_scalars(dt_ref[0], bias_ref[0], alog_ref[0], reverse)
        u = cs - jnp.log(dt)
        dtt = jnp.exp(total - u)
        dc = jnp.exp(total)
        x_bf = x_ref[0].astype(BF16)
        bm = b_ref[0]
        h = h_scr[...]
        h_bf = h.astype(BF16)
        hs_ref[0, 0, 0] = h
        bt = bm.T
        if with_y:
            cm = c_ref[0]
            cb = _dot_nt(cm.astype(BF16), bm.astype(BF16))
            cs_cols = _rows_to_cols(cs)
            keep = _tri_mask(False, reverse)
        first = lax.broadcasted_iota(jnp.int32, (1, LANES), 1) < HEAD_DIM
        heads = range(HPG)
        psl = [slice((r // 2) * LANES, (r // 2 + 1) * LANES) for r in heads]
        lhs = []
        if with_y:
            for r in heads:
                cs_col = jnp.broadcast_to(cs_cols[:, r:r + 1], (CHUNK, LANES))
                wf = cb * jnp.exp(jnp.where(keep, cs_col - u[r:r + 1], -jnp.inf))
                lhs.append(jnp.concatenate([wf.astype(BF16), (cm * jnp.exp(cs_col)).astype(BF16)], axis=1))
        bts = [(bt * dtt[r:r + 1]).astype(BF16) for r in heads]
        sts = [_dot(bts[r], x_bf[:, psl[r]]) for r in heads]
        if with_y:
            ys = [_dot(lhs[r], jnp.concatenate([x_bf[:, psl[r]], h_bf[:, psl[r]]], axis=0)) for r in heads]
        for p in range(HPG // 2):
            if with_y:
                y_ref[0, :, psl[2 * p]] = jnp.where(first, ys[2 * p], ys[2 * p + 1])
            dc_p = jnp.where(first, dc[2 * p:2 * p + 1], dc[2 * p + 1:2 * p + 2])
            h_scr[:, psl[2 * p]] = h[:, psl[2 * p]] * dc_p + jnp.where(first, sts[2 * p], sts[2 * p + 1])

        @pl.when(s == nc - 1)
        def _():
            hf_ref[0, 0] = h_scr[...]

    in_specs = [
        pl.BlockSpec((1, HPG, CHUNK), lambda b, g, s: (b, rowblk + g, chunk_of(s))),
        pl.BlockSpec((1, HPG, 1), lambda b, g, s: (rowblk + g, 0, 0)),
        pl.BlockSpec((1, HPG, 1), lambda b, g, s: (rowblk + g, 0, 0)),
        pl.BlockSpec((1, CHUNK, GW), lambda b, g, s: (b, chunk_of(s), g)),
        pl.BlockSpec((1, CHUNK, D_STATE), lambda b, g, s: (b, chunk_of(s), D_INNER // D_STATE + g)),
        pl.BlockSpec((1, CHUNK, D_STATE), lambda b, g, s: (b, chunk_of(s), D_INNER // D_STATE + N_BC + g)),
        pl.BlockSpec((1, 1, D_STATE, GW), lambda b, g, s: (b, g, 0, 0)),
    ]
    out_shape, out_specs = [], []
    if with_y:
        out_shape.append(jax.ShapeDtypeStruct((Bn, L, D_INNER), F32))
        out_specs.append(pl.BlockSpec((1, CHUNK, GW), lambda b, g, s: (b, chunk_of(s), g)))
    out_shape += [jax.ShapeDtypeStruct((Bn, N_BC, nc, D_STATE, GW), F32), jax.ShapeDtypeStruct((Bn, N_BC, D_STATE, GW), F32)]
    out_specs += [pl.BlockSpec((1, 1, 1, D_STATE, GW), lambda b, g, s: (b, g, chunk_of(s), 0, 0)),
                  pl.BlockSpec((1, 1, D_STATE, GW), lambda b, g, s: (b, g, 0, 0))]
    return pl.pallas_call(
        kern, name=name, out_shape=out_shape, grid=(Bn, N_BC, nc), in_specs=in_specs, out_specs=out_specs,
        scratch_shapes=[pltpu.VMEM((D_STATE, GW), F32)],
        compiler_params=_cp(("arbitrary", "arbitrary", "arbitrary")))(dtT, bias, alog, xbc, xbc, xbc, h0)


def ssd_bwd(name, dtT, bias, alog, xbc, h_start, dy, dh_final, direction):
    Bn, L = xbc.shape[:2]
    nc = L // CHUNK
    reverse = direction == 1
    rowblk = direction * N_BC
    has_y = dy is not None
    last = 0 if reverse else CHUNK - 1

    def chunk_of(s):
        return s if reverse else (nc - 1 - s)

    def kern(*refs):
        if has_y:
            (dt_ref, bias_ref, alog_ref, x_ref, b_ref, hs_ref, dhf_ref, c_ref, dy_ref,
             dx_ref, db_ref, ddt_ref, dbias_ref, dalog_ref, dh0_ref, dc_ref, dh_scr) = refs
        else:
            (dt_ref, bias_ref, alog_ref, x_ref, b_ref, hs_ref, dhf_ref,
             dx_ref, db_ref, ddt_ref, dbias_ref, dalog_ref, dh0_ref, dh_scr) = refs
        s = pl.program_id(2)

        @pl.when(s == 0)
        def _():
            dh_scr[...] = dhf_ref[0, 0]
            dbias_ref[...] = jnp.zeros(dbias_ref.shape, F32)
            dalog_ref[...] = jnp.zeros(dalog_ref.shape, F32)

        dtraw = dt_ref[0]
        dt, A, cs, total = _ssd_scalars(dtraw, bias_ref[0], alog_ref[0], reverse)
        u = cs - jnp.log(dt)
        dtt = jnp.exp(total - u)
        dcy = jnp.exp(total)
        u_cols = _rows_to_cols(u)
        x_bf = x_ref[0].astype(BF16)
        bm = b_ref[0]
        bt = bm.T
        h = hs_ref[0, 0, 0]
        dh = dh_scr[...]
        dh_bf = dh.astype(BF16)
        dbt = jnp.zeros((D_STATE, CHUNK), F32)
        if has_y:
            cm = c_ref[0]
            ct = cm.T
            e_row = jnp.exp(cs)
            dy_bf = dy_ref[0].astype(BF16)
            h_bf = h.astype(BF16)
            cbt = _dot_nt(bm.astype(BF16), cm.astype(BF16))
            keep = _tri_mask(True, reverse)
            dcbt = jnp.zeros((CHUNK, CHUNK), F32)
            dct = jnp.zeros((D_STATE, CHUNK), F32)
        tots, out_rows, in_rows, in_cols = [], [], [], []
        first = lax.broadcasted_iota(jnp.int32, (1, LANES), 1) < HEAD_DIM
        heads = range(HPG)
        psl = [slice((r // 2) * LANES, (r // 2 + 1) * LANES) for r in heads]
        mine = [first if r % 2 == 0 else jnp.logical_not(first) for r in heads]
        zeros_bf = jnp.zeros((CHUNK, LANES), BF16)

        def prep(r):
            u_col = jnp.broadcast_to(u_cols[:, r:r + 1], (CHUNK, LANES))
            bs = (bm * jnp.exp(total[r:r + 1] - u_col)).astype(BF16)
            if not has_y:
                return bs, None
            et = jnp.exp(jnp.where(keep, cs[r:r + 1] - u_col, -jnp.inf))
            return jnp.concatenate([(cbt * et).astype(BF16), bs], axis=1), et

        def matmuls(r, lhs):
            p2raw = _dot_nt(dh_bf[:, psl[r]], jnp.where(mine[r], x_bf[:, psl[r]], zeros_bf))
            if not has_y:
                return p2raw, None, None, _dot(lhs, dh_bf[:, psl[r]])
            a1 = _dot_nt(jnp.concatenate([x_bf[:, psl[r]], h_bf[:, psl[r]]], axis=0),
                         jnp.where(mine[r], dy_bf[:, psl[r]], zeros_bf))
            new = _dot((ct * e_row[r:r + 1]).astype(BF16), dy_bf[:, psl[r]])
            dx = _dot(lhs, jnp.concatenate([dy_bf[:, psl[r]], dh_bf[:, psl[r]]], axis=0))
            return p2raw, a1, new, dx

        def post(r, p2raw, a1, et, dbt, dcbt, dct):
            if has_y:
                pt = a1[0:CHUNK] * et
                dcbt = dcbt + pt
                mt = pt * cbt
                ph = a1[CHUNK:] * e_row[r:r + 1]
                dct = dct + ph
                out_rows.append(_colsum(mt + ct * ph))
                in_cols.append(jnp.sum(mt, axis=1, keepdims=True))
            p2 = p2raw * dtt[r:r + 1]
            dbt = dbt + p2
            t_term = _colsum(bt * p2)
            in_rows.append(t_term)
            hdh = h[:, psl[r]] * dh[:, psl[r]]
            tot = jnp.sum(t_term, axis=1, keepdims=True) + dcy[r:r + 1] * jnp.sum(jnp.where(mine[r], hdh, 0.0), keepdims=True)
            tots.append(jnp.broadcast_to(tot, (1, CHUNK)))
            return dbt, dcbt, dct

        if not has_y:
            dcbt = dct = None
        dxs, news, pending = [], [], []
        batch = HPG
        for r0 in range(0, HPG, batch):
            preps = [prep(r) for r in range(r0, r0 + batch)]
            mms = [matmuls(r, preps[r - r0][0]) for r in range(r0, r0 + batch)]
            for args in pending:
                dbt, dcbt, dct = post(*args, dbt, dcbt, dct)
            pending = [(r, mms[r - r0][0], mms[r - r0][1], preps[r - r0][1]) for r in range(r0, r0 + batch)]
            dxs += [m[3] for m in mms]
            news += [m[2] for m in mms]
        for args in pending:
            dbt, dcbt, dct = post(*args, dbt, dcbt, dct)
        for p in range(HPG // 2):
            dx_ref[0, :, psl[2 * p]] = jnp.where(first, dxs[2 * p], dxs[2 * p + 1])
            new = dh[:, psl[2 * p]] * jnp.where(first, dcy[2 * p:2 * p + 1], dcy[2 * p + 1:2 * p + 2])
            if has_y:
                new = new + jnp.where(first, news[2 * p], news[2 * p + 1])
            dh_scr[:, psl[2 * p]] = new
        db = dbt.T
        if has_y:
            dcbt_bf = dcbt.astype(BF16)
            db = db + _dot(dcbt_bf, cm.astype(BF16))
            dc_ref[0] = dct.T + _dot_tn(dcbt_bf, bm.astype(BF16))
        db_ref[0] = db
        s_row = _stack_rows(in_rows)
        lane = lax.broadcasted_iota(jnp.int32, (HPG, CHUNK), 1)
        dcs = jnp.where(lane == last, _stack_rows(tots), 0.0)
        if has_y:
            s_row = s_row + _cols_to_rows(in_cols)
            dcs = dcs + _stack_rows(out_rows)
        dcs = dcs - s_row
        da = _cumsum_lanes(dcs, not reverse)
        ddt = da * A + jnp.where(dt > 0.0, s_row / dt, 0.0)
        ddtraw = ddt * _sigmoid(dtraw + bias_ref[0])
        ddt_ref[0] = ddtraw
        dbias_ref[0, 0] += jnp.sum(ddtraw, axis=1, keepdims=True)
        dalog_ref[0, 0] += jnp.sum(da * dt, axis=1, keepdims=True) * A

        @pl.when(s == nc - 1)
        def _():
            dh0_ref[0, 0] = dh_scr[...]

    cidx = lambda b, g, s: (b, chunk_of(s), g)
    hidx = lambda b, g, s: (b, g, 0, 0)
    in_specs = [
        pl.BlockSpec((1, HPG, CHUNK), lambda b, g, s: (b, rowblk + g, chunk_of(s))),
        pl.BlockSpec((1, HPG, 1), lambda b, g, s: (rowblk + g, 0, 0)),
        pl.BlockSpec((1, HPG, 1), lambda b, g, s: (rowblk + g, 0, 0)),
        pl.BlockSpec((1, CHUNK, GW), cidx),
        pl.BlockSpec((1, CHUNK, D_STATE), lambda b, g, s: (b, chunk_of(s), D_INNER // D_STATE + g)),
        pl.BlockSpec((1, 1, 1, D_STATE, GW), lambda b, g, s: (b, g, chunk_of(s), 0, 0)),
        pl.BlockSpec((1, 1, D_STATE, GW), hidx),
    ]
    args = [dtT, bias, alog, xbc, xbc, h_start, dh_final]
    if has_y:
        in_specs += [pl.BlockSpec((1, CHUNK, D_STATE), lambda b, g, s: (b, chunk_of(s), D_INNER // D_STATE + N_BC + g)),
                     pl.BlockSpec((1, CHUNK, GW), cidx)]
        args += [xbc, dy]
    out_shape = [jax.ShapeDtypeStruct((Bn, L, D_INNER), F32), jax.ShapeDtypeStruct((Bn, L, N_BC * D_STATE), F32),
                 jax.ShapeDtypeStruct((Bn, N_HEADS, L), F32), jax.ShapeDtypeStruct((Bn, N_BC, HPG, 1), F32),
                 jax.ShapeDtypeStruct((Bn, N_BC, HPG, 1), F32), jax.ShapeDtypeStruct((Bn, N_BC, D_STATE, GW), F32)]
    out_specs = [pl.BlockSpec((1, CHUNK, GW), cidx), pl.BlockSpec((1, CHUNK, D_STATE), cidx),
                 pl.BlockSpec((1, HPG, CHUNK), lambda b, g, s: (b, g, chunk_of(s))),
                 pl.BlockSpec((1, 1, HPG, 1), hidx), pl.BlockSpec((1, 1, HPG, 1), hidx), pl.BlockSpec((1, 1, D_STATE, GW), hidx)]
    if has_y:
        out_shape.append(jax.ShapeDtypeStruct((Bn, L, N_BC * D_STATE), F32))
        out_specs.append(pl.BlockSpec((1, CHUNK, D_STATE), cidx))
    res = pl.pallas_call(
        kern, name=name, out_shape=out_shape, grid=(Bn, N_BC, nc), in_specs=in_specs, out_specs=out_specs,
        scratch_shapes=[pltpu.VMEM((D_STATE, GW), F32)],
        compiler_params=_cp(("arbitrary", "arbitrary", "arbitrary")))(*args)
    dxs, db, ddt, dbias, dalog, dh0 = res[:6]
    return dxs, db, (res[6] if has_y else None), ddt, dbias, dalog, dh0


def _dot_split2(v, sel):
    hi = v.astype(BF16)
    mid = (v - hi.astype(F32)).astype(BF16)
    return _dot(hi, sel) + _dot(mid, sel)


def ssd_tables():
    lane = jnp.arange(LANES)[:, None]
    col = jnp.arange(2 * GW)[None, :]
    expand = (lane == jnp.where(col < GW, HPG + col // HEAD_DIM, 2 * HPG + (col - GW) // HEAD_DIM)).astype(BF16)
    ch = jnp.arange(GW)[:, None] // HEAD_DIM
    out = jnp.arange(2 * LANES)[None, :]
    seg = ((out == ch) | (out == LANES + HPG + ch)).astype(BF16)
    return expand, seg


def _dc_lanes(dc, first):
    return jnp.concatenate([jnp.where(first, dc[2 * p:2 * p + 1], dc[2 * p + 1:2 * p + 2]) for p in range(HPG // 2)], axis=1)


def ssd_fwd2(name, dtT, bias, alog, xbc, h0, tables, direction, with_y):
    Bn, L = xbc.shape[:2]
    nc = L // CHUNK
    reverse = direction == 1
    rowblk = direction * N_BC
    expand = tables[0]

    def chunk_of(s):
        return (nc - 1 - s) if reverse else s

    def kern(dt_ref, bias_ref, alog_ref, x_ref, b_ref, c_ref, h0_ref, xp_ref, *rest):
        if with_y:
            y_ref, hs_ref, hf_ref, h_scr = rest
        else:
            hs_ref, hf_ref, h_scr = rest
        s = pl.program_id(2)

        @pl.when(s == 0)
        def _():
            h_scr[...] = h0_ref[0, 0]

        dt, _, cs, total = _ssd_scalars(dt_ref[0], bias_ref[0], alog_ref[0], reverse)
        u = cs - jnp.log(dt)
        dtt = jnp.exp(total - u)
        cols = _rows_to_cols(jnp.concatenate([cs, dtt, jnp.exp(cs)], axis=0))
        wide = _dot_split2(cols, xp_ref[...])
        dtt_x, e_x = wide[:, 0:GW], wide[:, GW:]
        first = lax.broadcasted_iota(jnp.int32, (1, LANES), 1) < HEAD_DIM
        x = x_ref[0]
        x_bf = x.astype(BF16)
        bm = b_ref[0]
        h = h_scr[...]
        hs_ref[0, 0, 0] = h
        st = _dot(bm.T.astype(BF16), (x * dtt_x).astype(BF16))
        h_scr[...] = h * _dc_lanes(jnp.exp(total), first) + st
        if with_y:
            cm = c_ref[0].astype(BF16)
            cb = _dot_nt(cm, bm.astype(BF16))
            yoff = _dot(cm, h.astype(BF16)) * e_x
            keep = _tri_mask(False, reverse)
            wfs = []
            for r in range(HPG):
                cs_col = jnp.broadcast_to(cols[:, r:r + 1], (CHUNK, LANES))
                wfs.append((cb * jnp.exp(jnp.where(keep, cs_col - u[r:r + 1], -jnp.inf))).astype(BF16))
            yd = [_dot(wfs[r], x_bf[:, (r // 2) * LANES:(r // 2 + 1) * LANES]) for r in range(HPG)]
            for p in range(HPG // 2):
                psl = slice(p * LANES, (p + 1) * LANES)
                y_ref[0, :, psl] = jnp.where(first, yd[2 * p], yd[2 * p + 1]) + yoff[:, psl]

        @pl.when(s == nc - 1)
        def _():
            hf_ref[0, 0] = h_scr[...]

    in_specs = [
        pl.BlockSpec((1, HPG, CHUNK), lambda b, g, s: (b, rowblk + g, chunk_of(s))),
        pl.BlockSpec((1, HPG, 1), lambda b, g, s: (rowblk + g, 0, 0)),
        pl.BlockSpec((1, HPG, 1), lambda b, g, s: (rowblk + g, 0, 0)),
        pl.BlockSpec((1, CHUNK, GW), lambda b, g, s: (b, chunk_of(s), g)),
        pl.BlockSpec((1, CHUNK, D_STATE), lambda b, g, s: (b, chunk_of(s), D_INNER // D_STATE + g)),
        pl.BlockSpec((1, CHUNK, D_STATE), lambda b, g, s: (b, chunk_of(s), D_INNER // D_STATE + N_BC + g)),
        pl.BlockSpec((1, 1, D_STATE, GW), lambda b, g, s: (b, g, 0, 0)),
        pl.BlockSpec(expand.shape, lambda b, g, s: (0, 0)),
    ]
    out_shape, out_specs = [], []
    if with_y:
        out_shape.append(jax.ShapeDtypeStruct((Bn, L, D_INNER), F32))
        out_specs.append(pl.BlockSpec((1, CHUNK, GW), lambda b, g, s: (b, chunk_of(s), g)))
    out_shape += [jax.ShapeDtypeStruct((Bn, N_BC, nc, D_STATE, GW), F32), jax.ShapeDtypeStruct((Bn, N_BC, D_STATE, GW), F32)]
    out_specs += [pl.BlockSpec((1, 1, 1, D_STATE, GW), lambda b, g, s: (b, g, chunk_of(s), 0, 0)),
                  pl.BlockSpec((1, 1, D_STATE, GW), lambda b, g, s: (b, g, 0, 0))]
    return pl.pallas_call(
        kern, name=name, out_shape=out_shape, grid=(Bn, N_BC, nc), in_specs=in_specs, out_specs=out_specs,
        scratch_shapes=[pltpu.VMEM((D_STATE, GW), F32)],
        compiler_params=_cp(("arbitrary", "arbitrary", "arbitrary")))(dtT, bias, alog, xbc, xbc, xbc, h0, expand)


def ssd_bwd2(name, dtT, bias, alog, xbc, h_start, dy, dh_final, tables, direction):
    Bn, L = xbc.shape[:2]
    nc = L // CHUNK
    reverse = direction == 1
    rowblk = direction * N_BC
    has_y = dy is not None
    last = 0 if reverse else CHUNK - 1
    expand, seg = tables

    def chunk_of(s):
        return s if reverse else (nc - 1 - s)

    def kern(*refs):
        if has_y:
            (dt_ref, bias_ref, alog_ref, x_ref, b_ref, hs_ref, dhf_ref, xp_ref, seg_ref, c_ref, dy_ref,
             dx_ref, db_ref, ddt_ref, dbias_ref, dalog_ref, dh0_ref, dc_ref, dh_scr) = refs
        else:
            (dt_ref, bias_ref, alog_ref, x_ref, b_ref, hs_ref, dhf_ref, xp_ref, seg_ref,
             dx_ref, db_ref, ddt_ref, dbias_ref, dalog_ref, dh0_ref, dh_scr) = refs
        s = pl.program_id(2)

        @pl.when(s == 0)
        def _():
            dh_scr[...] = dhf_ref[0, 0]
            dbias_ref[...] = jnp.zeros(dbias_ref.shape, F32)
            dalog_ref[...] = jnp.zeros(dalog_ref.shape, F32)

        first = lax.broadcasted_iota(jnp.int32, (1, LANES), 1) < HEAD_DIM
        heads = range(HPG)
        psl = [slice((r // 2) * LANES, (r // 2 + 1) * LANES) for r in heads]
        x = x_ref[0]
        bm = b_ref[0].astype(BF16)
        h = hs_ref[0, 0, 0]
        dh = dh_scr[...]
        dh_bf = dh.astype(BF16)
        bdh = _dot(bm, dh_bf)
        if has_y:
            cm = c_ref[0].astype(BF16)
            dyv = dy_ref[0]
            dy_bf = dyv.astype(BF16)
            x_bf = x.astype(BF16)
            h_bf = h.astype(BF16)
            cbt = _dot_nt(bm, cm)
            ch = _dot(cm, h_bf)
            zeros_bf = jnp.zeros((CHUNK, LANES), BF16)
            gts = [_dot_nt(x_bf[:, psl[r]], jnp.where(first if r % 2 == 0 else jnp.logical_not(first), dy_bf[:, psl[r]], zeros_bf))
                   for r in heads]
            ct_bf = c_ref[0].T.astype(BF16)
        dtraw = dt_ref[0]
        dt, A, cs, total = _ssd_scalars(dtraw, bias_ref[0], alog_ref[0], reverse)
        u = cs - jnp.log(dt)
        dtt = jnp.exp(total - u)
        dcy = jnp.exp(total)
        cols = _rows_to_cols(jnp.concatenate([u, dtt, jnp.exp(cs)], axis=0))
        wide = _dot_split2(cols, xp_ref[...])
        dtt_x, e_x = wide[:, 0:GW], wide[:, GW:]
        term2 = bdh * dtt_x
        dbt = _dot_nt(dh_bf, (x * dtt_x).astype(BF16))
        sums = _dot_split2(term2 * x, seg_ref[:, LANES:])
        new_dh = dh * _dc_lanes(dcy, first)
        if has_y:
            dye = dyv * e_x
            dye_bf = dye.astype(BF16)
            dct = _dot_nt(h_bf, dye_bf)
            new_dh = new_dh + _dot(ct_bf, dye_bf)
            sums = sums + _dot_split2(ch * dye, seg_ref[:, 0:LANES])
            keep = _tri_mask(True, reverse)
            ets = []
            for r in heads:
                u_col = jnp.broadcast_to(cols[:, r:r + 1], (CHUNK, LANES))
                ets.append(jnp.exp(jnp.where(keep, cs[r:r + 1] - u_col, -jnp.inf)))
            wts = [(cbt * ets[r]).astype(BF16) for r in heads]
            dxd = [_dot(wts[r], dy_bf[:, psl[r]]) for r in heads]
            dcbt = jnp.zeros((CHUNK, CHUNK), F32)
            out_rows, in_cols = [], []
            for r in heads:
                pt = gts[r] * ets[r]
                dcbt = dcbt + pt
                mt = pt * cbt
                out_rows.append(_colsum(mt))
                in_cols.append(jnp.sum(mt, axis=1, keepdims=True))
            for p in range(HPG // 2):
                dx_ref[0, :, psl[2 * p]] = jnp.where(first, dxd[2 * p], dxd[2 * p + 1]) + term2[:, psl[2 * p]]
            dcbt_bf = dcbt.astype(BF16)
            db_ref[0] = dbt.T + _dot(dcbt_bf, cm)
            dc_ref[0] = dct.T + _dot_tn(dcbt_bf, bm)
        else:
            dx_ref[0] = term2
            db_ref[0] = dbt.T
        dh_scr[...] = new_dh
        sums_t = sums.T
        s_row = sums_t[HPG:2 * HPG]
        hdh = _colsum(h * dh)
        lanes_w = lax.broadcasted_iota(jnp.int32, (1, GW), 1)
        hd = _stack_rows([jnp.sum(jnp.where(lanes_w // HEAD_DIM == r, hdh, 0.0), axis=1, keepdims=True) for r in range(HPG)])
        tot = jnp.sum(s_row, axis=1, keepdims=True) + dcy * hd
        lane = lax.broadcasted_iota(jnp.int32, (HPG, CHUNK), 1)
        dcs = jnp.where(lane == last, tot, 0.0)
        if has_y:
            s_row = s_row + _cols_to_rows(in_cols)
            dcs = dcs + _stack_rows(out_rows) + sums_t[0:HPG]
        dcs = dcs - s_row
        da = _cumsum_lanes(dcs, not reverse)
        ddt = da * A + jnp.where(dt > 0.0, s_row / dt, 0.0)
        ddtraw = ddt * _sigmoid(dtraw + bias_ref[0])
        ddt_ref[0] = ddtraw
        dbias_ref[0, 0] += jnp.sum(ddtraw, axis=1, keepdims=True)
        dalog_ref[0, 0] += jnp.sum(da * dt, axis=1, keepdims=True) * A

        @pl.when(s == nc - 1)
        def _():
            dh0_ref[0, 0] = dh_scr[...]

    cidx = lambda b, g, s: (b, chunk_of(s), g)
    hidx = lambda b, g, s: (b, g, 0, 0)
    in_specs = [
        pl.BlockSpec((1, HPG, CHUNK), lambda b, g, s: (b, rowblk + g, chunk_of(s))),
        pl.BlockSpec((1, HPG, 1), lambda b, g, s: (rowblk + g, 0, 0)),
        pl.BlockSpec((1, HPG, 1), lambda b, g, s: (rowblk + g, 0, 0)),
        pl.BlockSpec((1, CHUNK, GW), cidx),
        pl.BlockSpec((1, CHUNK, D_STATE), lambda b, g, s: (b, chunk_of(s), D_INNER // D_STATE + g)),
        pl.BlockSpec((1, 1, 1, D_STATE, GW), lambda b, g, s: (b, g, chunk_of(s), 0, 0)),
        pl.BlockSpec((1, 1, D_STATE, GW), hidx),
        pl.BlockSpec(expand.shape, lambda b, g, s: (0, 0)),
        pl.BlockSpec(seg.shape, lambda b, g, s: (0, 0)),
    ]
    args = [dtT, bias, alog, xbc, xbc, h_start, dh_final, expand, seg]
    if has_y:
        in_specs += [pl.BlockSpec((1, CHUNK, D_STATE), lambda b, g, s: (b, chunk_of(s), D_INNER // D_STATE + N_BC + g)),
                     pl.BlockSpec((1, CHUNK, GW), cidx)]
        args += [xbc, dy]
    out_shape = [jax.ShapeDtypeStruct((Bn, L, D_INNER), F32), jax.ShapeDtypeStruct((Bn, L, N_BC * D_STATE), F32),
                 jax.ShapeDtypeStruct((Bn, N_HEADS, L), F32), jax.ShapeDtypeStruct((Bn, N_BC, HPG, 1), F32),
                 jax.ShapeDtypeStruct((Bn, N_BC, HPG, 1), F32), jax.ShapeDtypeStruct((Bn, N_BC, D_STATE, GW), F32)]
    out_specs = [pl.BlockSpec((1, CHUNK, GW), cidx), pl.BlockSpec((1, CHUNK, D_STATE), cidx),
                 pl.BlockSpec((1, HPG, CHUNK), lambda b, g, s: (b, g, chunk_of(s))),
                 pl.BlockSpec((1, 1, HPG, 1), hidx), pl.BlockSpec((1, 1, HPG, 1), hidx), pl.BlockSpec((1, 1, D_STATE, GW), hidx)]
    if has_y:
        out_shape.append(jax.ShapeDtypeStruct((Bn, L, N_BC * D_STATE), F32))
        out_specs.append(pl.BlockSpec((1, CHUNK, D_STATE), cidx))
    res = pl.pallas_call(
        kern, name=name, out_shape=out_shape, grid=(Bn, N_BC, nc), in_specs=in_specs, out_specs=out_specs,
        scratch_shapes=[pltpu.VMEM((D_STATE, GW), F32)],
        compiler_params=_cp(("arbitrary", "arbitrary", "arbitrary")))(*args)
    dxs, db, ddt, dbias, dalog, dh0 = res[:6]
    return dxs, db, (res[6] if has_y else None), ddt, dbias, dalog, dh0


def _group_mean(v):
    gw = D_INNER // N_BC
    parts = [jnp.broadcast_to(jnp.mean(v[:, g * gw:(g + 1) * gw], axis=-1, keepdims=True), (v.shape[0], gw)) for g in range(N_BC)]
    return jnp.concatenate(parts, axis=1)


def gated_norm_fwd(name, y_f, y_b, xs_src, z, dskip_lanes, w_norm):
    def body(yf, yb, xs, z, dsk, w):
        u = (yf + yb + dsk * xs) * _silu(z)
        r = lax.rsqrt(_group_mean(u * u) + NORM_EPS)
        return u * r * w

    return tok_call(name, body, [y_f, y_b, xs_src, z], [], [dskip_lanes, w_norm], [(D_INNER, BF16)], [], [])[0]


def _dot_exact01(v, sel):
    hi, mid, lo = _split3(v)
    return _dot(hi, sel) + _dot(mid, sel) + _dot(lo, sel)


def gated_norm_bwd(name, y_f, y_b, xs_src, z, d_out, dskip_lanes, w_norm, head_sel):
    def body(yf, yb, xs, z, do, dsk, w, sel):
        y = yf + yb + dsk * xs
        sz = _silu(z)
        u = y * sz
        r = lax.rsqrt(_group_mean(u * u) + NORM_EPS)
        duh = do * w
        du = r * (duh - u * (r * r) * _group_mean(duh * u))
        dy = du * sz
        dz = du * y * _dsilu(z)
        dsk_heads = _dot_exact01(jnp.broadcast_to(_colsum(dy * xs), (8, D_INNER)), sel)
        return dy, dz, _colsum(do * u * r), dsk_heads

    return tok_call(name, body, [y_f, y_b, xs_src, z, d_out], [], [dskip_lanes, w_norm, head_sel],
                    [(D_INNER, F32), (D_INNER, BF16)], [], [(1, D_INNER), (8, LANES)], tm=128)


def merge_fwd(name, y_pool, y_ssd, gatepre, x, target, gate, b_merge, norm_post, w_pp, w_ps, w_out):
    def body(yp, ys, gp, x, tgt, gate, bm, wpost, w_pp, w_ps, w_out):
        p1 = _dot(yp, w_pp)
        p2 = _dot(ys, w_ps)
        gates = _sigmoid(gp + bm)
        merged = gates[:, :D] * p1 + gates[:, D:] * p2
        out = _dot(merged.astype(BF16), w_out)
        r = _rms_r(out)
        outr = out * r
        nq = outr * wpost
        err = x + gate * nq - tgt
        loss = 0.5 * jnp.sum(jnp.mean(err * err, axis=-1, keepdims=True), keepdims=True).reshape(1, 1)
        g = err * (1.0 / D)
        dnq = g * gate
        dout = _rms_bwd(dnq * wpost, out, r)
        return merged, p1, p2, dout, g, _colsum(g * nq), _colsum(dnq * outr), jnp.broadcast_to(loss, (1, LANES))

    return tok_call(name, body, [y_pool, y_ssd, gatepre, x, target], [gate], [b_merge, norm_post, w_pp, w_ps, w_out],
                    [(D, BF16), (D, F32), (D, F32), (D, BF16), (D, F32)], [D], [(1, D), (1, LANES)])


def merge_bwd(name, dout, gatepre, p1, p2, b_merge, w_pp, w_ps, w_out):
    def body(dout, gp, p1, p2, bm, w_pp, w_ps, w_out):
        dmerged = _dot_nt(dout, w_out)
        gates = _sigmoid(gp + bm)
        g1, g2 = gates[:, :D], gates[:, D:]
        dp1 = (dmerged * g1).astype(BF16)
        dp2 = (dmerged * g2).astype(BF16)
        dgp = jnp.concatenate([dmerged * p1 * g1 * (1.0 - g1), dmerged * p2 * g2 * (1.0 - g2)], axis=1)
        return dp1, dp2, dgp, _dot_nt(dp1, w_pp), _dot_nt(dp2, w_ps), _colsum(dgp)

    return tok_call(name, body, [dout, gatepre, p1, p2], [], [b_merge, w_pp, w_ps, w_out],
                    [(D, BF16), (D, BF16), (2 * D, BF16), (D, F32), (D_INNER, F32)], [], [(1, 2 * D)])


def _adamw_math(w, g, m, v):
    m = ADAM_B1 * m + (1.0 - ADAM_B1) * g
    v = ADAM_B2 * v + (1.0 - ADAM_B2) * (g * g)
    m_hat = m / (1.0 - ADAM_B1 ** ADAM_STEP)
    v_hat = v / (1.0 - ADAM_B2 ** ADAM_STEP)
    delta = -ADAM_LR * (m_hat / (jnp.sqrt(v_hat) + ADAM_EPS) + ADAM_WD * w)
    return delta, m, v


def adamw(name, w, g, m, v, tr=256):
    R, C = w.shape
    tr = min(tr, R)
    assert R % tr == 0

    def body(w_ref, g_ref, m_ref, v_ref, d_ref, nm_ref, nv_ref):
        d, nm, nv = _adamw_math(w_ref[...], g_ref[...], m_ref[...], v_ref[...])
        d_ref[...] = d
        nm_ref[...] = nm
        nv_ref[...] = nv

    spec = pl.BlockSpec((tr, C), lambda i: (i, 0))
    return pl.pallas_call(
        body, name=name, out_shape=[jax.ShapeDtypeStruct((R, C), F32)] * 3, grid=(R // tr,),
        in_specs=[spec] * 4, out_specs=[spec] * 3, compiler_params=_cp(("parallel",)))(w, g, m, v)


def _me():
    return lax.axis_index("x"), lax.axis_index("y"), lax.axis_index("c")


def all_gather_small(name, v):
    R, C = v.shape

    def body(v_ref, out_ref, send_sems, recv_sems, local_sem):
        x, y, c = _me()
        me = 4 * x + 2 * y + c
        mine = pltpu.make_async_copy(v_ref, out_ref.at[me], local_sem)
        mine.start()
        copies = []
        for d in range(1, N_DEV):
            dx, dy, dc = d // 4, (d // 2) % 2, d % 2
            px, py, pc = x ^ dx, y ^ dy, c ^ dc
            copies.append(pltpu.make_async_remote_copy(
                src_ref=v_ref, dst_ref=out_ref.at[me], send_sem=send_sems.at[d - 1], recv_sem=recv_sems.at[d - 1],
                device_id=(px, py, pc), device_id_type=MESH))
        for cp in copies:
            cp.start()
        for d in range(1, N_DEV):
            dx, dy, dc = d // 4, (d // 2) % 2, d % 2
            peer = 4 * (x ^ dx) + 2 * (y ^ dy) + (c ^ dc)
            pltpu.make_async_remote_copy(
                src_ref=v_ref, dst_ref=out_ref.at[peer], send_sem=send_sems.at[d - 1], recv_sem=recv_sems.at[d - 1],
                device_id=(x ^ dx, y ^ dy, c ^ dc), device_id_type=MESH).wait_recv()
        for cp in copies:
            cp.wait_send()
        mine.wait()

    return pl.pallas_call(
        body, name=name, out_shape=jax.ShapeDtypeStruct((N_DEV, R, C), F32),
        in_specs=[pl.BlockSpec(memory_space=pltpu.VMEM)], out_specs=pl.BlockSpec(memory_space=pltpu.VMEM),
        scratch_shapes=[pltpu.SemaphoreType.DMA((N_DEV - 1,)), pltpu.SemaphoreType.DMA((N_DEV - 1,)), pltpu.SemaphoreType.DMA],
        compiler_params=pltpu.CompilerParams(vmem_limit_bytes=VMEM_LIMIT))(v)


def all_gather_chips(name, shard):
    R, C = shard.shape
    half = R // 2
    assert R % 32 == 0

    def body(s_ref, out_ref, send_sems, recv_sems, local_sem):
        x, y, c = _me()
        k = 2 * x + y
        chips = [(1 - x, y), (x, 1 - y), (1 - x, 1 - y)]

        def rows(chip, hc):
            return out_ref.at[2 * chip[0] + chip[1], pl.ds(hc * half, half), :]

        mine = pltpu.make_async_copy(s_ref, out_ref.at[k], local_sem)
        mine.start()
        first = [pltpu.make_async_remote_copy(
            src_ref=s_ref.at[pl.ds(c * half, half), :], dst_ref=rows((x, y), c), send_sem=send_sems.at[j],
            recv_sem=recv_sems.at[j], device_id=(*chip, c), device_id_type=MESH) for j, chip in enumerate(chips)]
        for cp in first:
            cp.start()
        passed = [pltpu.make_async_remote_copy(
            src_ref=rows(chip, c), dst_ref=rows(chip, c), send_sem=send_sems.at[3 + j], recv_sem=recv_sems.at[3 + j],
            device_id=(x, y, 1 - c), device_id_type=MESH) for j, chip in enumerate(chips)]
        for j, chip in enumerate(chips):
            pltpu.make_async_remote_copy(
                src_ref=rows(chip, c), dst_ref=rows(chip, c), send_sem=send_sems.at[j], recv_sem=recv_sems.at[j],
                device_id=(*chip, c), device_id_type=MESH).wait_recv()
            passed[j].start()
        for j, chip in enumerate(chips):
            pltpu.make_async_remote_copy(
                src_ref=rows(chip, 1 - c), dst_ref=rows(chip, 1 - c), send_sem=send_sems.at[3 + j], recv_sem=recv_sems.at[3 + j],
                device_id=(x, y, 1 - c), device_id_type=MESH).wait_recv()
        for cp in first + passed:
            cp.wait_send()
        mine.wait()

    return pl.pallas_call(
        body, name=name, out_shape=jax.ShapeDtypeStruct((N_CHIPS, R, C), shard.dtype),
        in_specs=[pl.BlockSpec(memory_space=pl.ANY)], out_specs=pl.BlockSpec(memory_space=pl.ANY),
        scratch_shapes=[pltpu.SemaphoreType.DMA((6,)), pltpu.SemaphoreType.DMA((6,)), pltpu.SemaphoreType.DMA],
        compiler_params=pltpu.CompilerParams(vmem_limit_bytes=VMEM_LIMIT))(shard)


def sibling_swap(name, v):
    def body(v_ref, out_ref, send_sem, recv_sem):
        x, y, c = _me()
        cp = pltpu.make_async_remote_copy(src_ref=v_ref, dst_ref=out_ref, send_sem=send_sem, recv_sem=recv_sem,
                                          device_id=(x, y, 1 - c), device_id_type=MESH)
        cp.start()
        cp.wait()

    return pl.pallas_call(
        body, name=name, out_shape=jax.ShapeDtypeStruct(v.shape, v.dtype),
        in_specs=[pl.BlockSpec(memory_space=pl.ANY)], out_specs=pl.BlockSpec(memory_space=pl.ANY),
        scratch_shapes=[pltpu.SemaphoreType.DMA, pltpu.SemaphoreType.DMA],
        compiler_params=pltpu.CompilerParams(vmem_limit_bytes=VMEM_LIMIT))(v)


def chip_exchange(name, parts):
    def body(p_ref, out_ref, send_sems, recv_sems, local_sem):
        x, y, c = _me()
        k = 2 * x + y
        chips = [(1 - x, y), (x, 1 - y), (1 - x, 1 - y)]
        mine = pltpu.make_async_copy(p_ref.at[k], out_ref.at[k], local_sem)
        mine.start()
        sends = [pltpu.make_async_remote_copy(
            src_ref=p_ref.at[2 * chip[0] + chip[1]], dst_ref=out_ref.at[k], send_sem=send_sems.at[j], recv_sem=recv_sems.at[j],
            device_id=(*chip, c), device_id_type=MESH) for j, chip in enumerate(chips)]
        for cp in sends:
            cp.start()
        for j, chip in enumerate(chips):
            pltpu.make_async_remote_copy(
                src_ref=p_ref.at[k], dst_ref=out_ref.at[2 * chip[0] + chip[1]], send_sem=send_sems.at[j], recv_sem=recv_sems.at[j],
                device_id=(*chip, c), device_id_type=MESH).wait_recv()
        for cp in sends:
            cp.wait_send()
        mine.wait()

    return pl.pallas_call(
        body, name=name, out_shape=jax.ShapeDtypeStruct(parts.shape, parts.dtype),
        in_specs=[pl.BlockSpec(memory_space=pl.ANY)], out_specs=pl.BlockSpec(memory_space=pl.ANY),
        scratch_shapes=[pltpu.SemaphoreType.DMA((3,)), pltpu.SemaphoreType.DMA((3,)), pltpu.SemaphoreType.DMA],
        compiler_params=pltpu.CompilerParams(vmem_limit_bytes=VMEM_LIMIT))(parts)


def _row_tile(rows, cap, mult=8):
    best = None
    for t in range(mult, min(rows, cap) + 1, mult):
        if rows % t == 0:
            best = t
    assert best is not None, rows
    return best


def add_arrays(name, arrs, out_dtype=F32):
    shape = arrs[0].shape
    C = shape[-1]
    flat = [a.reshape(-1, C) for a in arrs]
    R = flat[0].shape[0]
    narrow = out_dtype == BF16 or any(a.dtype == BF16 for a in arrs)
    tr = _row_tile(R, 2048 if len(arrs) <= 2 else 1024, 16 if narrow else 8)
    n = len(flat)

    def body(*refs):
        acc = refs[0][...].astype(F32)
        for r in refs[1:n]:
            acc = acc + r[...].astype(F32)
        refs[n][...] = acc.astype(out_dtype)

    spec = pl.BlockSpec((tr, C), lambda i: (i, 0))
    out = pl.pallas_call(
        body, name=name, out_shape=jax.ShapeDtypeStruct((R, C), out_dtype), grid=(R // tr,),
        in_specs=[spec] * n, out_specs=spec, compiler_params=_cp(("parallel",)))(*flat)
    return out.reshape(shape)


def reduce_scatter_chips(slabs):
    _, R, C = slabs.shape
    half = R // 2
    c = lax.axis_index("c")
    k = 2 * lax.axis_index("x") + lax.axis_index("y")
    halves = slabs.reshape(N_CHIPS, 2, half, C)
    own = lax.dynamic_index_in_dim(halves, c, axis=1, keepdims=False)
    other = lax.dynamic_index_in_dim(halves, 1 - c, axis=1, keepdims=False)
    from_sibling = sibling_swap("rs_sibling_halves", other)
    chip_part = add_arrays("rs_add_sibling", [own, from_sibling], out_dtype=BF16)
    landed = chip_exchange("rs_chip_exchange", chip_part)
    mine = add_arrays("rs_add_chips", [landed[j] for j in range(N_CHIPS)])
    sib = sibling_swap("rs_sibling_result", mine)
    lo = jnp.where(c == 0, mine, sib)
    hi = jnp.where(c == 0, sib, mine)
    del k
    return jnp.concatenate([lo, hi], axis=0)


def ada_mod_shard(cond_all, w_ada_shard, b_ada_shard):
    def body(c_ref, w_ref, b_ref, o_ref):
        o_ref[...] = _dot(_silu(c_ref[...]).astype(BF16), w_ref[...].astype(BF16)) + b_ref[...]

    return pl.pallas_call(body, name="ada_mod_shard", out_shape=jax.ShapeDtypeStruct((cond_all.shape[0], w_ada_shard.shape[1]), F32),
                          compiler_params=_cp())(cond_all, w_ada_shard, b_ada_shard)


def ada_bwd_shard(cond_all, dmod_all_shard, dmod_all, w_ada_shard, row_is_cctx):
    def body(c_ref, ds_ref, da_ref, w_ref, sel_ref, gw_ref, gb_ref, part_ref):
        sc = _silu(c_ref[...]).astype(BF16)
        gw_ref[...] = _dot_tn(sc, ds_ref[...].astype(BF16))
        gb_ref[...] = _colsum(da_ref[...])
        dc_tot = jnp.broadcast_to(_colsum(ds_ref[...] * sel_ref[...]), (8, ds_ref.shape[1]))
        part_ref[...] = _dot_nt(dc_tot.astype(BF16), w_ref[...].astype(BF16))

    n = cond_all.shape[0]
    return pl.pallas_call(
        body, name="ada_bwd_shard",
        out_shape=[jax.ShapeDtypeStruct(w_ada_shard.shape, F32), jax.ShapeDtypeStruct((1, dmod_all.shape[1]), F32),
                   jax.ShapeDtypeStruct((8, D), F32)],
        compiler_params=_cp())(cond_all, dmod_all_shard, dmod_all, w_ada_shard, row_is_cctx)


def sum_devices(name, gathered):
    def body(g_ref, o_ref):
        acc = g_ref[0]
        for d in range(1, N_DEV):
            acc = acc + g_ref[d]
        o_ref[...] = acc

    return pl.pallas_call(body, name=name, out_shape=jax.ShapeDtypeStruct(gathered.shape[1:], F32), compiler_params=_cp())(gathered)


def cctx_finish(gathered, c_ctx_row):
    def body(g_ref, c_ref, o_ref):
        acc = g_ref[0, 0:1, :]
        for k in range(1, N_CHIPS):
            acc = acc + g_ref[2 * k, 0:1, :]
        o_ref[...] = acc * _dsilu(c_ref[...])

    return pl.pallas_call(body, name="cctx_finish", out_shape=jax.ShapeDtypeStruct((1, D), F32), compiler_params=_cp())(gathered, c_ctx_row)


def _pack(parts, rows):
    flat = []
    for p in parts:
        p = p.reshape(-1)
        pad = (-p.shape[0]) % LANES
        flat.append(jnp.pad(p, (0, pad)) if pad else p)
    v = jnp.concatenate(flat)
    return jnp.pad(v, (0, rows * LANES - v.shape[0])).reshape(rows, LANES)


def _unpack(v, sizes):
    flat = v.reshape(-1)
    out, off = [], 0
    for n in sizes:
        out.append(flat[off:off + n])
        off += n + (-n) % LANES
    return out


W_SHARD_ROWS = 3456
SEG_ROWS = (0, 2320, 2576, 3088, 3344, 3408)


def kernel(x, c, ctx, c_ctx, w_ada, b_ada, norm_pre, norm_post, w_in, b_merge, pool_w, pool_scale, conv_w, conv_b, dt_bias, a_log, d_skip, ssd_norm, w_proj_pool, w_proj_ssd, w_out, loss_target, m_c_ctx, m_w_ada, m_b_ada, m_norm_pre, m_norm_post, m_w_in, m_b_merge, m_pool_w, m_pool_scale, m_conv_w, m_conv_b, m_dt_bias, m_a_log, m_d_skip, m_ssd_norm, m_w_proj_pool, m_w_proj_ssd, m_w_out, v_c_ctx, v_w_ada, v_b_ada, v_norm_pre, v_norm_post, v_w_in, v_b_merge, v_pool_w, v_pool_scale, v_conv_w, v_conv_b, v_dt_bias, v_a_log, v_d_skip, v_ssd_norm, v_w_proj_pool, v_w_proj_ssd, v_w_out):
    Bn, L, _ = x.shape
    Lc = ctx.shape[1]
    T, Tc = Bn * L, Bn * Lc
    assert Bn == 2
    ix, iy, ic = lax.axis_index("x"), lax.axis_index("y"), lax.axis_index("c")
    me = 4 * ix + 2 * iy + ic
    chip = 2 * ix + iy
    ada_cols = w_ada.shape[2]
    cw_cols = conv_w.shape[2]

    cond_own = jnp.pad(c, ((0, 8 - Bn), (0, 0))) + jnp.pad(c_ctx[None, :], ((Bn, 7 - Bn), (0, 0)))
    convw_own = jnp.pad(conv_w[0], ((0, 4), (0, D - cw_cols)))
    g1 = all_gather_small("gather_cond", jnp.concatenate([cond_own, convw_own], axis=0))
    cond_all = g1[:, 0:8].reshape(8 * N_DEV, D)
    conv_w_full = jnp.concatenate([g1[2 * k, 8:12, 0:cw_cols] for k in range(N_CHIPS)], axis=1)
    b_ada_shard = lax.dynamic_slice(b_ada, (0, chip * ada_cols), (1, ada_cols))
    g2 = all_gather_small("gather_mod", ada_mod_shard(cond_all, w_ada[0], b_ada_shard))
    mod_full = jnp.concatenate([g2[2 * k] for k in range(N_CHIPS)], axis=1)
    own = lax.dynamic_slice(mod_full, (8 * me, 0), (8, 3 * D))
    shift, scale, gate = (own[0:Bn, i * D:(i + 1) * D][:, None, :] for i in range(3))
    shift_c, scale_c = (jnp.broadcast_to(own[Bn:Bn + 1, i * D:(i + 1) * D][None], (Bn, 1, D)) for i in range(2))

    shard = jnp.concatenate([w_in[0].T, w_proj_pool[0], w_proj_ssd[0], w_out[0], pool_w[0].reshape(64, D),
                             jnp.zeros((W_SHARD_ROWS - SEG_ROWS[-1], D), F32)], axis=0).astype(BF16)
    gw = all_gather_chips("gather_weights", shard)
    w_inT = gw[:, SEG_ROWS[0]:SEG_ROWS[1]].reshape(IN_COLS, D)
    w_pp = gw[:, SEG_ROWS[1]:SEG_ROWS[2]].reshape(D, D)
    w_ps = gw[:, SEG_ROWS[2]:SEG_ROWS[3]].reshape(D_INNER, D)
    w_o = gw[:, SEG_ROWS[3]:SEG_ROWS[4]].reshape(D, D)
    pool_full = gw[:, SEG_ROWS[4]:SEG_ROWS[5]].reshape(N_CHIPS, 4, 64, POOL_GROUP).transpose(1, 0, 2, 3).reshape(D, POOL_GROUP)
    w_dt = jnp.pad(w_inT[9216:IN_COLS], ((0, LANES - 64), (0, 0)))
    seg_lo = (0, 256, 512, 768, 1024, 2048, 4096, 6144, 8192, 8704)
    seg_hi = (256, 512, 768, 1024, 2048, 4096, 6144, 8192, 8704, 9216)
    w_seg = [w_inT[lo:hi] for lo, hi in zip(seg_lo, seg_hi)] + [w_dt]

    hx = prenorm_fwd("prenorm_x", x, scale, shift, norm_pre)
    hc = prenorm_fwd("prenorm_ctx", ctx, scale_c, shift_c, norm_pre)
    hx2, hc2 = hx.reshape(T, D), hc.reshape(Tc, D)
    v = mm_nt("proj_v", hx2, w_inT[0:1024], F32).reshape(Bn, L, D)
    zp = mm_nt("proj_zpool", hx2, w_inT[1024:2048], F32).reshape(Bn, L, D)
    zs = mm_nt("proj_zssd", hx2, w_inT[2048:4096], F32).reshape(Bn, L, D_INNER)
    gp = mm_nt("proj_gate", hx2, w_inT[4096:6144], F32).reshape(Bn, L, 2 * D)
    xbc_raw = mm_nt("proj_xbc", hx2, w_inT[6144:9216], F32).reshape(Bn, L, CONV_DIM)
    dt_raw = mm_nt("proj_dt", hx2, w_dt, F32)
    xbc_raw_c = mm_nt("proj_xbc_ctx", hc2, w_inT[6144:9216], F32).reshape(Bn, Lc, CONV_DIM)
    dt_raw_c = mm_nt("proj_dt_ctx", hc2, w_dt, F32)
    dtT = dt_raw[:, :64].reshape(Bn, L, 64).transpose(0, 2, 1)
    dtT_c = dt_raw_c[:, :64].reshape(Bn, Lc, 64).transpose(0, 2, 1)
    bias3 = dt_bias.reshape(2 * N_BC, HPG, 1)
    alog3 = a_log.reshape(2 * N_BC, HPG, 1)

    xbc = conv_fwd("conv_x", xbc_raw, conv_w_full, conv_b)
    xbc_c = conv_fwd("conv_ctx", xbc_raw_c, conv_w_full, conv_b)
    zero_state = jnp.zeros((Bn, N_BC, D_STATE, GW), F32)
    tables = ssd_tables()
    ys, hs_x, hs_c, hf_x, hf_c = [], [], [], [], []
    for d in range(2):
        hsc, hfc = ssd_fwd(f"ssd_fwd_ctx{d}", dtT_c, bias3, alog3, xbc_c, zero_state, d, False)
        y, hsx, hfx = ssd_fwd(f"ssd_fwd_x{d}", dtT, bias3, alog3, xbc, hfc, d, True)
        ys.append(y)
        hs_x.append(hsx)
        hs_c.append(hsc)
        hf_x.append(hfx)
        hf_c.append(hfc)

    dgs = [pool_diff(f"pool_diff{g}", v, g * POOL_GROUP, g, False) for g in range(4)]
    y_pool = pool_mix_fwd("pool_mix", dgs, zp, pool_full, pool_scale)
    dskip_lanes = jnp.repeat(d_skip[0], HEAD_DIM)[None, :]
    y_ssd = gated_norm_fwd("gated_norm", ys[0], ys[1], (xbc, D_INNER), zs, dskip_lanes, ssd_norm)
    merged, p1, p2, dout, g_res, dgate, g_norm_post, loss_part = merge_fwd(
        "merge_fwd", y_pool, y_ssd, gp, x, loss_target, gate, b_merge, norm_post, w_pp, w_ps, w_o)

    dp1, dp2, dgp, dyp, dys, g_b_merge = merge_bwd("merge_bwd", dout, gp, p1, p2, b_merge, w_pp, w_ps, w_o)
    gw_o = mm_tn("gw_out", merged.reshape(T, D), dout.reshape(T, D))
    gw_pp = mm_tn("gw_proj_pool", y_pool.reshape(T, D), dp1.reshape(T, D))
    gw_ps = mm_tn("gw_proj_ssd", y_ssd.reshape(T, D_INNER), dp2.reshape(T, D))

    *dds, dzp, g_pool, g_pool_scale = pool_mix_bwd("pool_mix_bwd", dgs, zp, dyp, pool_full, pool_scale)
    dvs = [pool_diff(f"pool_diff_t{g}", dds[g], 0, g, True) for g in range(4)]

    head_sel = (jnp.arange(D_INNER)[:, None] // HEAD_DIM == jnp.arange(LANES)[None, :]).astype(BF16)
    dy, dzs, g_ssd_norm, g_dskip = gated_norm_bwd(
        "gated_norm_bwd", ys[0], ys[1], (xbc, D_INNER), zs, dys, dskip_lanes, ssd_norm, head_sel)

    dxs, dbm, dcm, ddt, dxs_c, dbm_c, ddt_c = [], [], [], [], [], [], []
    g_bias = jnp.zeros((2, N_BC, HPG, 1), F32)
    g_alog = jnp.zeros((2, N_BC, HPG, 1), F32)
    for d in range(2):
        a, b_, c_, t_, gb, ga, dh0 = ssd_bwd(f"ssd_bwd_x{d}", dtT, bias3, alog3, xbc, hs_x[d], dy, zero_state, d)
        dxs.append(a), dbm.append(b_), dcm.append(c_), ddt.append(t_)
        ac, bc, _, tc, gbc, gac, _ = ssd_bwd(f"ssd_bwd_ctx{d}", dtT_c, bias3, alog3, xbc_c, hs_c[d], None, dh0, d)
        dxs_c.append(ac), dbm_c.append(bc), ddt_c.append(tc)
        g_bias = g_bias.at[d].set(jnp.sum(gb, axis=0) + jnp.sum(gbc, axis=0))
        g_alog = g_alog.at[d].set(jnp.sum(ga, axis=0) + jnp.sum(gac, axis=0))

    dxr_xs, gcw_xs, gcb_xs = conv_bwd("conv_bwd_xs", xbc_raw, dxs, conv_w_full, conv_b, 0, D_INNER, scaled=(dy, dskip_lanes))
    dxr_b, gcw_b, gcb_b = conv_bwd("conv_bwd_b", xbc_raw, dbm, conv_w_full, conv_b, D_INNER, N_BC * D_STATE)
    dxr_c, gcw_c, gcb_c = conv_bwd("conv_bwd_c", xbc_raw, dcm, conv_w_full, conv_b, D_INNER + N_BC * D_STATE, N_BC * D_STATE)
    dxr_xs_c, gcw_xs_c, gcb_xs_c = conv_bwd("conv_bwd_xs_ctx", xbc_raw_c, dxs_c, conv_w_full, conv_b, 0, D_INNER)
    dxr_b_c, gcw_b_c, gcb_b_c = conv_bwd("conv_bwd_b_ctx", xbc_raw_c, dbm_c, conv_w_full, conv_b, D_INNER, N_BC * D_STATE)
    g_conv_w = jnp.concatenate([gcw_xs + gcw_xs_c, gcw_b + gcw_b_c, gcw_c], axis=1)
    g_conv_b = jnp.concatenate([gcb_xs + gcb_xs_c, gcb_b + gcb_b_c, gcb_c], axis=1)

    def dt_cols(parts, n_tok):
        t = jnp.concatenate(parts, axis=1).transpose(0, 2, 1).reshape(n_tok, 2 * N_HEADS)
        return jnp.pad(t, ((0, 0), (0, LANES - 2 * N_HEADS))).astype(BF16)

    ddt2, ddt2_c = dt_cols(ddt, T), dt_cols(ddt_c, Tc)
    segs = ([dv.reshape(T, POOL_GROUP) for dv in dvs]
            + [dzp.reshape(T, D), dzs.reshape(T, D_INNER), dgp.reshape(T, 2 * D), dxr_xs.reshape(T, D_INNER),
               dxr_b.reshape(T, N_BC * D_STATE), dxr_c.reshape(T, N_BC * D_STATE), ddt2])
    d_hx = mm_nn_multi("d_hx", list(zip(segs, w_seg)), F32).reshape(Bn, L, D)
    segs_c = {7: dxr_xs_c.reshape(Tc, D_INNER), 8: dxr_b_c.reshape(Tc, N_BC * D_STATE), 10: ddt2_c}
    d_hc = mm_nn_multi("d_hc", [(segs_c[i], w_seg[i]) for i in (7, 8, 10)], F32).reshape(Bn, Lc, D)
    gw_rows = []
    for i, seg in enumerate(segs):
        init = mm_tn(f"gw_in_ctx{i}", segs_c[i], hc2) if i in segs_c else None
        gw_rows.append(mm_tn(f"gw_in{i}", seg, hx2, init=init))
    gw_rows[-1] = gw_rows[-1][0:2 * N_HEADS]
    gw_inT = jnp.concatenate(gw_rows, axis=0)

    grad_x, dscale, dshift, g_npre_x = prenorm_bwd("prenorm_bwd_x", x, d_hx, scale, norm_pre, g_res=g_res)
    _, dscale_c, dshift_c, g_npre_c = prenorm_bwd("prenorm_bwd_ctx", ctx, d_hc, scale_c, norm_pre)

    dmod_x = jnp.concatenate([dshift[:, 0], dscale[:, 0], dgate[:, 0]], axis=1)
    dmod_c = jnp.concatenate([jnp.sum(dshift_c[:, 0], axis=0, keepdims=True), jnp.sum(dscale_c[:, 0], axis=0, keepdims=True),
                              jnp.zeros((1, D), F32)], axis=1)
    dmod_own = jnp.pad(dmod_x, ((0, 8 - Bn), (0, 0))) + jnp.pad(dmod_c, ((Bn, 7 - Bn), (0, 0)))
    dmod_all = all_gather_small("gather_dmod", dmod_own).reshape(8 * N_DEV, 3 * D)
    row_is_cctx = (jnp.arange(8 * N_DEV) % 8 == Bn).astype(F32)[:, None]
    g_w_ada, g_b_ada, cpart = ada_bwd_shard(
        cond_all, lax.dynamic_slice(dmod_all, (0, chip * ada_cols), (8 * N_DEV, ada_cols)), dmod_all, w_ada[0], row_is_cctx)
    g_c_ctx = cctx_finish(all_gather_small("gather_cctx", cpart), c_ctx[None, :])

    small_sizes = (D, D, 2 * D, D, CONV_DIM, 2 * N_HEADS, 2 * N_HEADS, N_HEADS, D_INNER, 4 * CONV_DIM, 1)
    pk = _pack([g_npre_x + g_npre_c, g_norm_post, g_b_merge, g_pool_scale, g_conv_b, g_bias, g_alog, g_dskip[0, 0:N_HEADS],
                g_ssd_norm, g_conv_w, loss_part[0, 0:1]], 184)
    small = sum_devices("sum_small", all_gather_small("gather_small", pk))
    (g_norm_pre, g_norm_post_t, g_b_merge_t, g_pool_scale_t, g_conv_b_t, g_dt_bias, g_a_log, g_d_skip, g_ssd_norm_t,
     g_conv_w_t, loss) = _unpack(small, small_sizes)
    g_conv_w_shard = lax.dynamic_slice(g_conv_w_t.reshape(4, CONV_DIM), (0, chip * cw_cols), (4, cw_cols))

    pool_slab = g_pool.reshape(4, N_CHIPS, 64, POOL_GROUP).transpose(1, 0, 2, 3).reshape(N_CHIPS, 64, D)
    slabs = jnp.concatenate([gw_inT.reshape(N_CHIPS, 2320, D), gw_pp.reshape(N_CHIPS, 256, D), gw_ps.reshape(N_CHIPS, 512, D),
                             gw_o.reshape(N_CHIPS, 256, D), pool_slab, jnp.zeros((N_CHIPS, W_SHARD_ROWS - SEG_ROWS[-1], D), F32)], axis=1)
    gsh = reduce_scatter_chips(slabs)
    g_w_in = gsh[SEG_ROWS[0]:SEG_ROWS[1]].T
    g_w_pp, g_w_ps, g_w_o = (gsh[SEG_ROWS[i]:SEG_ROWS[i + 1]] for i in (1, 2, 3))
    g_pool_w = gsh[SEG_ROWS[4]:SEG_ROWS[5]].reshape(256, POOL_GROUP)

    grads = {
        "c_ctx": g_c_ctx.reshape(c_ctx.shape), "w_ada": g_w_ada[None], "b_ada": g_b_ada, "norm_pre": g_norm_pre[None],
        "norm_post": g_norm_post_t[None], "w_in": g_w_in[None], "b_merge": g_b_merge_t[None],
        "pool_w": g_pool_w.reshape(pool_w.shape), "pool_scale": g_pool_scale_t[None], "conv_w": g_conv_w_shard[None],
        "conv_b": g_conv_b_t[None], "dt_bias": g_dt_bias.reshape(dt_bias.shape), "a_log": g_a_log.reshape(a_log.shape),
        "d_skip": g_d_skip[None], "ssd_norm": g_ssd_norm_t[None], "w_proj_pool": g_w_pp[None], "w_proj_ssd": g_w_ps[None],
        "w_out": g_w_o[None]}
    weights = dict(c_ctx=c_ctx, w_ada=w_ada, b_ada=b_ada, norm_pre=norm_pre, norm_post=norm_post, w_in=w_in, b_merge=b_merge,
                   pool_w=pool_w, pool_scale=pool_scale, conv_w=conv_w, conv_b=conv_b, dt_bias=dt_bias, a_log=a_log,
                   d_skip=d_skip, ssd_norm=ssd_norm, w_proj_pool=w_proj_pool, w_proj_ssd=w_proj_ssd, w_out=w_out)
    m_in = dict(c_ctx=m_c_ctx, w_ada=m_w_ada, b_ada=m_b_ada, norm_pre=m_norm_pre, norm_post=m_norm_post, w_in=m_w_in,
                b_merge=m_b_merge, pool_w=m_pool_w, pool_scale=m_pool_scale, conv_w=m_conv_w, conv_b=m_conv_b,
                dt_bias=m_dt_bias, a_log=m_a_log, d_skip=m_d_skip, ssd_norm=m_ssd_norm, w_proj_pool=m_w_proj_pool,
                w_proj_ssd=m_w_proj_ssd, w_out=m_w_out)
    v_in = dict(c_ctx=v_c_ctx, w_ada=v_w_ada, b_ada=v_b_ada, norm_pre=v_norm_pre, norm_post=v_norm_post, w_in=v_w_in,
                b_merge=v_b_merge, pool_w=v_pool_w, pool_scale=v_pool_scale, conv_w=v_conv_w, conv_b=v_conv_b,
                dt_bias=v_dt_bias, a_log=v_a_log, d_skip=v_d_skip, ssd_norm=v_ssd_norm, w_proj_pool=v_w_proj_pool,
                w_proj_ssd=v_w_proj_ssd, w_out=v_w_out)
    names = list(weights)
    big = ("w_ada", "w_in", "pool_w", "w_proj_pool", "w_proj_ssd", "w_out")
    small_names = [n for n in names if n not in big]
    delta, new_m, new_v = {}, {}, {}
    for n in big:
        shape2 = (-1, weights[n].shape[-1])
        d_, m_, v_ = adamw(f"adamw_{n}", weights[n].reshape(shape2), grads[n].reshape(shape2), m_in[n].reshape(shape2),
                           v_in[n].reshape(shape2), tr=128)
        delta[n], new_m[n], new_v[n] = (t.reshape(weights[n].shape) for t in (d_, m_, v_))
    sizes = [weights[n].size for n in small_names]
    packed = [_pack([src[n] for n in small_names], 144) for src in (weights, grads, m_in, v_in)]
    outs = adamw("adamw_small", *packed, tr=144)
    for res, store in zip(outs, (delta, new_m, new_v)):
        for n, piece in zip(small_names, _unpack(res, sizes)):
            store[n] = piece.reshape(weights[n].shape)

    return (loss.reshape(()), grad_x, *[grads[n] for n in names], *[delta[n] for n in names],
            *[new_m[n] for n in names], *[new_v[n] for n in names])
```

```python
import jax
import jax.numpy as jnp
from jax import lax
from jax.experimental import pallas as pl
from jax.experimental.pallas import tpu as pltpu

F32 = jnp.float32
BF16 = jnp.bfloat16
MESH = pl.DeviceIdType.MESH

D = 1024
GRID_W = 64
NORM_EPS = 1e-6
POOL_WINDOWS = (2, 4, 8, 16)
POOL_GROUP = 256
D_INNER = 2048
HEAD_DIM = 64
N_HEADS = 32
D_STATE = 128
N_BC = 4
HPG = N_HEADS // N_BC
GW = HPG * HEAD_DIM
CONV_DIM = 3072
CHUNK = 128
OFF_XBC = 6144
IN_COLS = 9280
N_CHIPS = 4
N_DEV = 8

ADAM_LR = 0.001
ADAM_B1 = 0.9
ADAM_B2 = 0.999
ADAM_EPS = 1e-08
ADAM_WD = 0.01
ADAM_STEP = 10

V7X_VMEM_BYTES = 64 * 1024 * 1024
VMEM_LIMIT = V7X_VMEM_BYTES * 3 // 4
LANES = 128


def _cp(sem=None):
    return pltpu.CompilerParams(dimension_semantics=sem, vmem_limit_bytes=VMEM_LIMIT)


def _dot(a, b):
    return jnp.dot(a, b, preferred_element_type=F32)


def _dot_nt(a, b):
    return lax.dot_general(a, b, (((1,), (1,)), ((), ())), preferred_element_type=F32)


def _dot_tn(a, b):
    return lax.dot_general(a, b, (((0,), (0,)), ((), ())), preferred_element_type=F32)


def _split3(x):
    hi = x.astype(BF16)
    r1 = x - hi.astype(F32)
    mid = r1.astype(BF16)
    lo = (r1 - mid.astype(F32)).astype(BF16)
    return hi, mid, lo


def _sigmoid(x):
    return jax.nn.sigmoid(x)


def _silu(x):
    return x * _sigmoid(x)


def _dsilu(x):
    s = _sigmoid(x)
    return s * (1.0 + x * (1.0 - s))


def _softplus(x):
    return jnp.maximum(x, 0.0) + jnp.log(1.0 + jnp.exp(-jnp.abs(x)))


def mm_nt(name, a, b, out_dtype, tm=1024, tn=512):
    M, K = a.shape
    N = b.shape[0]
    tm, tn = min(tm, M), min(tn, N)
    assert M % tm == 0 and N % tn == 0, (M, N, tm, tn)

    def body(a_ref, b_ref, o_ref):
        o_ref[...] = _dot_nt(a_ref[...], b_ref[...]).astype(o_ref.dtype)

    return pl.pallas_call(
        body, name=name, out_shape=jax.ShapeDtypeStruct((M, N), out_dtype), grid=(M // tm, N // tn),
        in_specs=[pl.BlockSpec((tm, K), lambda i, j: (i, 0)), pl.BlockSpec((tn, K), lambda i, j: (j, 0))],
        out_specs=pl.BlockSpec((tm, tn), lambda i, j: (i, j)),
        compiler_params=_cp(("parallel", "arbitrary")))(a, b)


def mm_tn(name, a, b, init=None, tm=1024, tn=1024, tk=512):
    T, M = a.shape
    N = b.shape[1]
    tm, tn, tk = min(tm, M), min(tn, N), min(tk, T)
    assert M % tm == 0 and N % tn == 0 and T % tk == 0, (M, N, T)
    has_init = init is not None

    def body(*refs):
        if has_init:
            a_ref, b_ref, i_ref, o_ref = refs
        else:
            a_ref, b_ref, o_ref = refs
        k = pl.program_id(2)

        @pl.when(k == 0)
        def _():
            o_ref[...] = i_ref[...] if has_init else jnp.zeros(o_ref.shape, F32)

        o_ref[...] += _dot_tn(a_ref[...], b_ref[...])

    in_specs = [pl.BlockSpec((tk, tm), lambda i, j, k: (k, i)), pl.BlockSpec((tk, tn), lambda i, j, k: (k, j))]
    args = [a, b]
    if has_init:
        in_specs.append(pl.BlockSpec((tm, tn), lambda i, j, k: (i, j)))
        args.append(init)
    return pl.pallas_call(
        body, name=name, out_shape=jax.ShapeDtypeStruct((M, N), F32), grid=(M // tm, N // tn, T // tk),
        in_specs=in_specs, out_specs=pl.BlockSpec((tm, tn), lambda i, j, k: (i, j)),
        compiler_params=_cp(("parallel", "parallel", "arbitrary")))(*args)


def mm_nn_multi(name, pairs, out_dtype, tm=512, tk=512):
    M = pairs[0][0].shape[0]
    N = pairs[0][1].shape[1]
    tm = min(tm, M)
    assert M % tm == 0
    plan = []
    step = 0
    for a, b in pairs:
        K = a.shape[1]
        t = min(tk, K)
        assert K % t == 0 and b.shape == (K, N)
        plan.append((t, step, K // t))
        step += K // t
    nsteps = step
    npairs = len(pairs)

    def body(*refs):
        o_ref, acc = refs[2 * npairs], refs[2 * npairs + 1]
        k = pl.program_id(1)

        @pl.when(k == 0)
        def _():
            acc[...] = jnp.zeros(acc.shape, F32)

        for p, (_, first, n) in enumerate(plan):
            @pl.when((k >= first) & (k < first + n))
            def _(p=p):
                acc[...] += _dot(refs[2 * p][...], refs[2 * p + 1][...])

        @pl.when(k == nsteps - 1)
        def _():
            o_ref[...] = acc[...].astype(o_ref.dtype)

    in_specs, args = [], []
    for (a, b), (t, first, n) in zip(pairs, plan):
        in_specs.append(pl.BlockSpec((tm, t), lambda i, k, first=first, n=n: (i, jnp.clip(k - first, 0, n - 1))))
        in_specs.append(pl.BlockSpec((t, N), lambda i, k, first=first, n=n: (jnp.clip(k - first, 0, n - 1), 0)))
        args += [a, b]
    return pl.pallas_call(
        body, name=name, out_shape=jax.ShapeDtypeStruct((M, N), out_dtype), grid=(M // tm, nsteps),
        in_specs=in_specs, out_specs=pl.BlockSpec((tm, N), lambda i, k: (i, 0)),
        scratch_shapes=[pltpu.VMEM((tm, N), F32)],
        compiler_params=_cp(("parallel", "arbitrary")))(*args)


def tok_call(name, body, tiled, perb, glob, out_tiled, out_perb, out_glob, tm=256):
    widths = [t[1] if isinstance(t, tuple) else t.shape[2] for t in tiled]
    tiled = [t[0] if isinstance(t, tuple) else t for t in tiled]
    Bn, L = tiled[0].shape[:2]
    tm = min(tm, L)
    assert L % tm == 0
    n_t, n_p, n_g = len(tiled), len(perb), len(glob)
    o_t, o_p, o_g = len(out_tiled), len(out_perb), len(out_glob)
    n_in = n_t + n_p + n_g

    def kern(*refs):
        ins, outs = refs[:n_in], refs[n_in:]
        b, j = pl.program_id(0), pl.program_id(1)
        vals = [r[0] for r in ins[:n_t + n_p]] + [r[...] for r in ins[n_t + n_p:]]
        res = body(*vals)
        if not isinstance(res, (tuple, list)):
            res = (res,)
        assert len(res) == o_t + o_p + o_g, (name, len(res))
        for r, v in zip(outs[:o_t], res[:o_t]):
            r[0] = v.astype(r.dtype)

        def accum(r, v, first, lead):
            @pl.when(first)
            def _():
                r[...] = jnp.zeros(r.shape, F32)
            if lead:
                r[0] += v
            else:
                r[...] += v

        for r, v in zip(outs[o_t:o_t + o_p], res[o_t:o_t + o_p]):
            accum(r, v, j == 0, True)
        for r, v in zip(outs[o_t + o_p:], res[o_t + o_p:]):
            accum(r, v, (j == 0) & (b == 0), False)

    in_specs = ([pl.BlockSpec((1, tm, w), lambda b, j: (b, j, 0)) for w in widths]
                + [pl.BlockSpec((1, 1, a.shape[2]), lambda b, j: (b, 0, 0)) for a in perb]
                + [pl.BlockSpec(a.shape, lambda b, j: (0, 0), pipeline_mode=pl.Buffered(1)) for a in glob])
    out_shape = ([jax.ShapeDtypeStruct((Bn, L, w), dt) for w, dt in out_tiled]
                 + [jax.ShapeDtypeStruct((Bn, 1, w), F32) for w in out_perb]
                 + [jax.ShapeDtypeStruct(s, F32) for s in out_glob])
    out_specs = ([pl.BlockSpec((1, tm, w), lambda b, j: (b, j, 0)) for w, _ in out_tiled]
                 + [pl.BlockSpec((1, 1, w), lambda b, j: (b, 0, 0)) for w in out_perb]
                 + [pl.BlockSpec(s, lambda b, j: (0, 0)) for s in out_glob])
    return pl.pallas_call(
        kern, name=name, out_shape=out_shape, grid=(Bn, L // tm), in_specs=in_specs, out_specs=out_specs,
        compiler_params=_cp(("arbitrary", "arbitrary")))(*tiled, *perb, *glob)


def slab_call(name, body, slabs, colparams, out_slabs, out_colred, wc=LANES):
    Bn, L = slabs[0][0].shape[:2]
    w_out = out_slabs[0][0]
    assert w_out % wc == 0 and all(off % wc == 0 for _, off in slabs + colparams)
    n_col = w_out // wc
    n_s, n_c = len(slabs), len(colparams)
    o_s = len(out_slabs)

    def kern(*refs):
        ins, outs = refs[:n_s + n_c], refs[n_s + n_c:]
        b = pl.program_id(1)
        vals = [r[0] for r in ins[:n_s]] + [r[...] for r in ins[n_s:]]
        res = body(*vals)
        if not isinstance(res, (tuple, list)):
            res = (res,)
        assert len(res) == o_s + len(out_colred), name
        for r, v in zip(outs[:o_s], res[:o_s]):
            r[0] = v.astype(r.dtype)

        def accum(r, v):
            @pl.when(b == 0)
            def _():
                r[...] = jnp.zeros(r.shape, F32)
            r[...] += v

        for r, v in zip(outs[o_s:], res[o_s:]):
            accum(r, v)

    in_specs = ([pl.BlockSpec((1, L, wc), lambda j, b, o=off // wc: (b, 0, o + j)) for _, off in slabs]
                + [pl.BlockSpec((a.shape[0], wc), lambda j, b, o=off // wc: (0, o + j)) for a, off in colparams])
    out_shape = ([jax.ShapeDtypeStruct((Bn, L, w), dt) for w, dt in out_slabs]
                 + [jax.ShapeDtypeStruct((r, w_out), F32) for r in out_colred])
    out_specs = ([pl.BlockSpec((1, L, wc), lambda j, b: (b, 0, j)) for _ in out_slabs]
                 + [pl.BlockSpec((r, wc), lambda j, b: (0, j)) for r in out_colred])
    return pl.pallas_call(
        kern, name=name, out_shape=out_shape, grid=(n_col, Bn), in_specs=in_specs, out_specs=out_specs,
        compiler_params=_cp(("arbitrary", "arbitrary")))(*[a for a, _ in slabs], *[a for a, _ in colparams])


def _rms_r(x):
    return lax.rsqrt(jnp.mean(x * x, axis=-1, keepdims=True) + NORM_EPS)


def _rms_bwd(dxh, x, r):
    return r * (dxh - x * (r * r) * jnp.mean(dxh * x, axis=-1, keepdims=True))


def _colsum(v):
    return jnp.sum(v, axis=0, keepdims=True)


def _stack_rows(rows):
    n, w = len(rows), rows[0].shape[1]
    sub = lax.broadcasted_iota(jnp.int32, (n, w), 0)
    acc = jnp.zeros((n, w), F32)
    for r, row in enumerate(rows):
        acc = acc + jnp.where(sub == r, jnp.broadcast_to(row, (n, w)), 0.0)
    return acc


def prenorm_fwd(name, x, scale, shift, w_pre):
    def body(x, scale, shift, w):
        n = x * _rms_r(x) * w
        return n * (1.0 + scale) + shift

    return tok_call(name, body, [x], [scale, shift], [w_pre], [(D, BF16)], [], [])[0]


def prenorm_bwd(name, x, dhx, scale, w_pre, g_res=None):
    has_res = g_res is not None

    def body(*v):
        if has_res:
            x, dhx, g, scale, w = v
        else:
            x, dhx, scale, w = v
        r = _rms_r(x)
        xr = x * r
        n = xr * w
        dn = dhx * (1.0 + scale)
        dx = _rms_bwd(dn * w, x, r)
        if has_res:
            dx = dx + g
        return dx, _colsum(dhx * n), _colsum(dhx), _colsum(dn * xr)

    tiled = [x, dhx] + ([g_res] if has_res else [])
    return tok_call(name, body, tiled, [scale], [w_pre], [(D, F32)], [D, D], [(1, D)])


def _shift_rows(x, o, tok, L):
    if o == 0:
        return x
    rolled = pltpu.roll(x, (-o) % L, 0)
    return jnp.where((tok + o >= 0) & (tok + o < L), rolled, 0.0)


def conv_fwd(name, xbc_raw, conv_w, conv_b):
    L = xbc_raw.shape[1]

    def body(x, w, b):
        tok = lax.broadcasted_iota(jnp.int32, x.shape, 0)
        pre = b
        for k in range(4):
            pre = pre + _shift_rows(x, k - 2, tok, L) * w[k:k + 1]
        return _silu(pre)

    return slab_call(name, body, [(xbc_raw, 0)], [(conv_w, 0), (conv_b, 0)], [(CONV_DIM, F32)], [])[0]


def conv_bwd(name, xbc_raw, dparts, conv_w, conv_b, col0, width, scaled=None):
    L = xbc_raw.shape[1]
    n_d = len(dparts) + (1 if scaled is not None else 0)

    def body(*v):
        x, ds, w, b = v[0], v[1:1 + n_d], v[1 + n_d], v[2 + n_d]
        tok = lax.broadcasted_iota(jnp.int32, x.shape, 0)
        taps = [_shift_rows(x, k - 2, tok, L) for k in range(4)]
        pre = b
        for k in range(4):
            pre = pre + taps[k] * w[k:k + 1]
        dy = ds[0] * v[3 + n_d] if scaled is not None else ds[0]
        for extra in ds[1:]:
            dy = dy + extra
        dpre = dy * _dsilu(pre)
        dx = jnp.zeros_like(x)
        for k in range(4):
            dx = dx + _shift_rows(dpre, 2 - k, tok, L) * w[k:k + 1]
        dw = _stack_rows([_colsum(dpre * taps[k]) for k in range(4)])
        return dx, dw, _colsum(dpre)

    slabs = [(xbc_raw, col0)] + ([(scaled[0], 0)] if scaled is not None else []) + [(d, 0) for d in dparts]
    colparams = [(conv_w, col0), (conv_b, col0)] + ([(scaled[1], 0)] if scaled is not None else [])
    return slab_call(name, body, slabs, colparams, [(width, BF16)], [4, 1])


def _box_mean(x, k, step, pos, n, L, transpose):
    lo, hi = k // 2, k - 1 - k // 2
    cnt = (jnp.minimum(pos + hi + 1, n) - jnp.maximum(pos - lo, 0)).astype(F32)
    if transpose:
        x = x / cnt
        lo, hi = hi, lo
    acc = x
    for o in range(-lo, hi + 1):
        if o == 0:
            continue
        rolled = pltpu.roll(x, (-o * step) % L, 0)
        acc = acc + jnp.where((pos + o >= 0) & (pos + o < n), rolled, 0.0)
    return acc if transpose else acc / cnt


def pool_diff(name, v, col0, gi, transpose):
    L = v.shape[1]
    rows = L // GRID_W
    k = POOL_WINDOWS[gi]

    def body(x):
        tok = lax.broadcasted_iota(jnp.int32, x.shape, 0)
        col = tok & (GRID_W - 1)
        row = tok >> 6
        if not transpose:
            m = _box_mean(x, k, GRID_W, row, rows, L, False)
            m = _box_mean(m, k, 1, col, GRID_W, L, False)
        else:
            m = _box_mean(x, k, 1, col, GRID_W, L, True)
            m = _box_mean(m, k, GRID_W, row, rows, L, True)
        return m - x

    return slab_call(name, body, [(v, col0)], [], [(POOL_GROUP, BF16)], [])[0]


def pool_mix_fwd(name, dgs, z_pool, pool_w, pool_scale):
    def body(d0, d1, d2, d3, z, w, scale):
        q = jnp.concatenate([_dot(d, w[g * POOL_GROUP:(g + 1) * POOL_GROUP]) for g, d in enumerate((d0, d1, d2, d3))], axis=1)
        return q * scale * _silu(z)

    return tok_call(name, body, list(dgs) + [z_pool], [], [pool_w, pool_scale], [(D, BF16)], [], [])[0]


def pool_mix_bwd(name, dgs, z_pool, dyp, pool_w, pool_scale):
    def body(d0, d1, d2, d3, z, dyp, w, scale):
        ds = (d0, d1, d2, d3)
        q = jnp.concatenate([_dot(d, w[g * POOL_GROUP:(g + 1) * POOL_GROUP]) for g, d in enumerate(ds)], axis=1)
        dypm = dyp * _silu(z)
        dz = dyp * (q * scale) * _dsilu(z)
        dq = (dypm * scale).astype(BF16)
        dds, gws = [], []
        for g, d in enumerate(ds):
            dqg = dq[:, g * POOL_GROUP:(g + 1) * POOL_GROUP]
            dds.append(_dot_nt(dqg, w[g * POOL_GROUP:(g + 1) * POOL_GROUP]))
            gws.append(_dot_tn(d, dqg))
        return (*dds, dz, jnp.concatenate(gws, axis=0), _colsum(dypm * q))

    return tok_call(name, body, list(dgs) + [z_pool, dyp], [], [pool_w, pool_scale],
                    [(POOL_GROUP, F32)] * 4 + [(D, BF16)], [], [(D, POOL_GROUP), (1, D)])


def _cumsum_lanes(a, reverse):
    n = a.shape[1]
    k = lax.broadcasted_iota(jnp.int32, (n, n), 0)
    i = lax.broadcasted_iota(jnp.int32, (n, n), 1)
    tri = jnp.where((k >= i) if reverse else (k <= i), 1.0, 0.0).astype(BF16)
    return _dot_exact01(a, tri)


def _rows_to_cols(rows):
    r = rows.shape[0]
    if r < LANES:
        rows = jnp.concatenate([rows, jnp.zeros((LANES - r, rows.shape[1]), F32)], axis=0)
    return rows.T


def _cols_to_rows(cols):
    q = cols[0].shape[0]
    lane = lax.broadcasted_iota(jnp.int32, (q, LANES), 1)
    acc = jnp.zeros((q, LANES), F32)
    for r, c in enumerate(cols):
        acc = acc + jnp.where(lane == r, c, 0.0)
    return acc.T[0:len(cols)]


def _ssd_scalars(dtraw, bias, alog, reverse):
    dt = _softplus(dtraw + bias)
    A = -jnp.exp(alog)
    cs = _cumsum_lanes(dt * A, reverse)
    total = cs[:, 0:1] if reverse else cs[:, CHUNK - 1:CHUNK]
    return dt, A, cs, total


def _decay_matrix(cs_col, cs_row, reverse):
    i = lax.broadcasted_iota(jnp.int32, (CHUNK, CHUNK), 0)
    j = lax.broadcasted_iota(jnp.int32, (CHUNK, CHUNK), 1)
    keep = (i <= j) if reverse else (i >= j)
    return jnp.exp(jnp.where(keep, cs_col - cs_row, -jnp.inf))


def ssd_fwd_v1(name, dtT, bias, alog, xbc, h0, direction, with_y):
    Bn, L = xbc.shape[:2]
    nc = L // CHUNK
    reverse = direction == 1
    rowblk = direction * N_BC

    def chunk_of(s):
        return (nc - 1 - s) if reverse else s

    def kern(dt_ref, bias_ref, alog_ref, x_ref, b_ref, c_ref, h0_ref, *rest):
        if with_y:
            y_ref, hs_ref, hf_ref, h_scr, xt_scr = rest
        else:
            hs_ref, hf_ref, h_scr, xt_scr = rest
        s = pl.program_id(2)

        @pl.when(s == 0)
        def _():
            h_scr[...] = h0_ref[0, 0]

        dt, _, cs, total = _ssd_scalars(dt_ref[0], bias_ref[0], alog_ref[0], reverse)
        e_row = jnp.exp(cs)
        t_row = jnp.exp(total - cs)
        dc = jnp.exp(total)
        cols = _rows_to_cols(jnp.concatenate([dt, e_row, t_row, cs], axis=0))
        x = x_ref[0]
        bm = b_ref[0].astype(BF16)
        cm = c_ref[0].astype(BF16)
        h = h_scr[...]
        hs_ref[0, 0, 0] = h
        if with_y:
            cb = _dot_nt(cm, bm)
            yoff = _dot(cm, h.astype(BF16))
        for r in range(HPG):
            sl = slice(r * HEAD_DIM, (r + 1) * HEAD_DIM)
            xdt = x[:, sl] * cols[:, r:r + 1]
            if with_y:
                lr = _decay_matrix(cols[:, 3 * HPG + r:3 * HPG + r + 1], cs[r:r + 1], reverse)
                ydiag = _dot((cb * lr).astype(BF16), xdt.astype(BF16))
                y_ref[0, :, sl] = ydiag + yoff[:, sl] * cols[:, HPG + r:HPG + r + 1]
            xt_scr[:, sl] = (xdt * cols[:, 2 * HPG + r:2 * HPG + r + 1]).astype(BF16)
        st = _dot_tn(bm, xt_scr[...])
        for r in range(HPG):
            sl = slice(r * HEAD_DIM, (r + 1) * HEAD_DIM)
            h_scr[:, sl] = h[:, sl] * dc[r:r + 1] + st[:, sl]

        @pl.when(s == nc - 1)
        def _():
            hf_ref[0, 0] = h_scr[...]

    in_specs = [
        pl.BlockSpec((1, HPG, CHUNK), lambda b, g, s: (b, rowblk + g, chunk_of(s))),
        pl.BlockSpec((1, HPG, 1), lambda b, g, s: (rowblk + g, 0, 0)),
        pl.BlockSpec((1, HPG, 1), lambda b, g, s: (rowblk + g, 0, 0)),
        pl.BlockSpec((1, CHUNK, GW), lambda b, g, s: (b, chunk_of(s), g)),
        pl.BlockSpec((1, CHUNK, D_STATE), lambda b, g, s: (b, chunk_of(s), D_INNER // D_STATE + g)),
        pl.BlockSpec((1, CHUNK, D_STATE), lambda b, g, s: (b, chunk_of(s), D_INNER // D_STATE + N_BC + g)),
        pl.BlockSpec((1, 1, D_STATE, GW), lambda b, g, s: (b, g, 0, 0)),
    ]
    out_shape, out_specs = [], []
    if with_y:
        out_shape.append(jax.ShapeDtypeStruct((Bn, L, D_INNER), F32))
        out_specs.append(pl.BlockSpec((1, CHUNK, GW), lambda b, g, s: (b, chunk_of(s), g)))
    out_shape += [jax.ShapeDtypeStruct((Bn, N_BC, nc, D_STATE, GW), F32), jax.ShapeDtypeStruct((Bn, N_BC, D_STATE, GW), F32)]
    out_specs += [pl.BlockSpec((1, 1, 1, D_STATE, GW), lambda b, g, s: (b, g, chunk_of(s), 0, 0)),
                  pl.BlockSpec((1, 1, D_STATE, GW), lambda b, g, s: (b, g, 0, 0))]
    return pl.pallas_call(
        kern, name=name, out_shape=out_shape, grid=(Bn, N_BC, nc), in_specs=in_specs, out_specs=out_specs,
        scratch_shapes=[pltpu.VMEM((D_STATE, GW), F32), pltpu.VMEM((CHUNK, GW), BF16)],
        compiler_params=_cp(("arbitrary", "arbitrary", "arbitrary")))(dtT, bias, alog, xbc, xbc, xbc, h0)


def ssd_bwd_v1(name, dtT, bias, alog, xbc, h_start, dy, dh_final, direction):
    Bn, L = xbc.shape[:2]
    nc = L // CHUNK
    reverse = direction == 1
    rowblk = direction * N_BC
    has_y = dy is not None
    last = 0 if reverse else CHUNK - 1

    def chunk_of(s):
        return s if reverse else (nc - 1 - s)

    def kern(*refs):
        if has_y:
            (dt_ref, bias_ref, alog_ref, x_ref, b_ref, c_ref, hs_ref, dhf_ref, dy_ref,
             dx_ref, db_ref, dc_ref, ddt_ref, dbias_ref, dalog_ref, dh0_ref, dh_scr, e_scr, t_scr) = refs
        else:
            (dt_ref, bias_ref, alog_ref, x_ref, b_ref, hs_ref, dhf_ref,
             dx_ref, db_ref, ddt_ref, dbias_ref, dalog_ref, dh0_ref, dh_scr, t_scr) = refs
        s = pl.program_id(2)

        @pl.when(s == 0)
        def _():
            dh_scr[...] = dhf_ref[0, 0]
            dbias_ref[...] = jnp.zeros(dbias_ref.shape, F32)
            dalog_ref[...] = jnp.zeros(dalog_ref.shape, F32)

        dtraw = dt_ref[0]
        dt, A, cs, total = _ssd_scalars(dtraw, bias_ref[0], alog_ref[0], reverse)
        e_row = jnp.exp(cs)
        t_row = jnp.exp(total - cs)
        dcy = jnp.exp(total)
        cols = _rows_to_cols(jnp.concatenate([dt, e_row, t_row, cs], axis=0))
        x = x_ref[0]
        bm = b_ref[0].astype(BF16)
        h = hs_ref[0, 0, 0]
        dh = dh_scr[...]
        dh_bf = dh.astype(BF16)
        bdh = _dot(bm, dh_bf)
        if has_y:
            cm = c_ref[0].astype(BF16)
            dyv = dy_ref[0]
            cb = _dot_nt(cm, bm)
            yoff = _dot(cm, h.astype(BF16))
            dcb = jnp.zeros((CHUNK, CHUNK), F32)
        col_terms, row_terms, ddt_cols, dtot = [], [], [], []
        for r in range(HPG):
            sl = slice(r * HEAD_DIM, (r + 1) * HEAD_DIM)
            dt_c = cols[:, r:r + 1]
            e_c = cols[:, HPG + r:HPG + r + 1]
            t_c = cols[:, 2 * HPG + r:2 * HPG + r + 1]
            xr = x[:, sl]
            xdt = xr * dt_c
            dxdt = t_c * bdh[:, sl]
            d_t = jnp.sum(bdh[:, sl] * xdt, axis=1, keepdims=True)
            col = -(t_c * d_t)
            tot = jnp.sum(t_c * d_t, axis=0, keepdims=True) + dcy[r:r + 1] * jnp.sum(h[:, sl] * dh[:, sl], keepdims=True)
            if has_y:
                dyr = dyv[:, sl]
                lr = _decay_matrix(cols[:, 3 * HPG + r:3 * HPG + r + 1], cs[r:r + 1], reverse)
                w = cb * lr
                gm = _dot_nt(dyr.astype(BF16), xdt.astype(BF16))
                m = gm * w
                dcb = dcb + gm * lr
                dxdt = dxdt + _dot_tn(w.astype(BF16), dyr.astype(BF16))
                col = col + jnp.sum(m, axis=1, keepdims=True) + jnp.sum(yoff[:, sl] * dyr, axis=1, keepdims=True) * e_c
                row_terms.append(-jnp.sum(m, axis=0, keepdims=True))
                e_scr[:, sl] = (e_c * dyr).astype(BF16)
            t_scr[:, sl] = (t_c * xdt).astype(BF16)
            dx_ref[0, :, sl] = dxdt * dt_c
            ddt_cols.append(jnp.sum(dxdt * xr, axis=1, keepdims=True))
            col_terms.append(col)
            dtot.append(tot)
        db = _dot_nt(t_scr[...], dh_bf)
        if has_y:
            dcb_bf = dcb.astype(BF16)
            db = db + _dot_tn(dcb_bf, cm)
            dc_ref[0] = _dot(dcb_bf, bm) + _dot_nt(e_scr[...], h.astype(BF16))
            cte = _dot_tn(cm, e_scr[...])
        db_ref[0] = db
        for r in range(HPG):
            sl = slice(r * HEAD_DIM, (r + 1) * HEAD_DIM)
            new = dh[:, sl] * dcy[r:r + 1]
            if has_y:
                new = new + cte[:, sl]
            dh_scr[:, sl] = new
        dcs = _cols_to_rows(col_terms)
        if has_y:
            dcs = dcs + _stack_rows(row_terms)
        lane = lax.broadcasted_iota(jnp.int32, (HPG, CHUNK), 1)
        dcs = dcs + jnp.where(lane == last, _stack_rows([jnp.broadcast_to(t, (1, CHUNK)) for t in dtot]), 0.0)
        da = _cumsum_lanes(dcs, not reverse)
        ddt = da * A + _cols_to_rows(ddt_cols)
        ddtraw = ddt * _sigmoid(dtraw + bias_ref[0])
        ddt_ref[0] = ddtraw
        dbias_ref[0, 0] += jnp.sum(ddtraw, axis=1, keepdims=True)
        dalog_ref[0, 0] += jnp.sum(da * dt, axis=1, keepdims=True) * A

        @pl.when(s == nc - 1)
        def _():
            dh0_ref[0, 0] = dh_scr[...]

    cidx = lambda b, g, s: (b, chunk_of(s), g)
    in_specs = [
        pl.BlockSpec((1, HPG, CHUNK), lambda b, g, s: (b, rowblk + g, chunk_of(s))),
        pl.BlockSpec((1, HPG, 1), lambda b, g, s: (rowblk + g, 0, 0)),
        pl.BlockSpec((1, HPG, 1), lambda b, g, s: (rowblk + g, 0, 0)),
        pl.BlockSpec((1, CHUNK, GW), cidx),
        pl.BlockSpec((1, CHUNK, D_STATE), lambda b, g, s: (b, chunk_of(s), D_INNER // D_STATE + g)),
    ]
    args = [dtT, bias, alog, xbc, xbc]
    if has_y:
        in_specs.append(pl.BlockSpec((1, CHUNK, D_STATE), lambda b, g, s: (b, chunk_of(s), D_INNER // D_STATE + N_BC + g)))
        args.append(xbc)
    in_specs += [pl.BlockSpec((1, 1, 1, D_STATE, GW), lambda b, g, s: (b, g, chunk_of(s), 0, 0)),
                 pl.BlockSpec((1, 1, D_STATE, GW), lambda b, g, s: (b, g, 0, 0))]
    args += [h_start, dh_final]
    if has_y:
        in_specs.append(pl.BlockSpec((1, CHUNK, GW), cidx))
        args.append(dy)
    out_shape = [jax.ShapeDtypeStruct((Bn, L, D_INNER), F32), jax.ShapeDtypeStruct((Bn, L, N_BC * D_STATE), F32)]
    out_specs = [pl.BlockSpec((1, CHUNK, GW), cidx), pl.BlockSpec((1, CHUNK, D_STATE), cidx)]
    if has_y:
        out_shape.append(jax.ShapeDtypeStruct((Bn, L, N_BC * D_STATE), F32))
        out_specs.append(pl.BlockSpec((1, CHUNK, D_STATE), cidx))
    out_shape += [jax.ShapeDtypeStruct((Bn, N_HEADS, L), F32), jax.ShapeDtypeStruct((Bn, N_BC, HPG, 1), F32),
                  jax.ShapeDtypeStruct((Bn, N_BC, HPG, 1), F32), jax.ShapeDtypeStruct((Bn, N_BC, D_STATE, GW), F32)]
    out_specs += [pl.BlockSpec((1, HPG, CHUNK), lambda b, g, s: (b, g, chunk_of(s))),
                  pl.BlockSpec((1, 1, HPG, 1), lambda b, g, s: (b, g, 0, 0)),
                  pl.BlockSpec((1, 1, HPG, 1), lambda b, g, s: (b, g, 0, 0)),
                  pl.BlockSpec((1, 1, D_STATE, GW), lambda b, g, s: (b, g, 0, 0))]
    scratch = [pltpu.VMEM((D_STATE, GW), F32)] + ([pltpu.VMEM((CHUNK, GW), BF16)] if has_y else []) + [pltpu.VMEM((CHUNK, GW), BF16)]
    res = pl.pallas_call(
        kern, name=name, out_shape=out_shape, grid=(Bn, N_BC, nc), in_specs=in_specs, out_specs=out_specs,
        scratch_shapes=scratch, compiler_params=_cp(("arbitrary", "arbitrary", "arbitrary")))(*args)
    if has_y:
        return res
    dxs, db, ddt, dbias, dalog, dh0 = res
    return dxs, db, None, ddt, dbias, dalog, dh0


def _tri_mask(transposed, reverse):
    sub = lax.broadcasted_iota(jnp.int32, (CHUNK, CHUNK), 0)
    lane = lax.broadcasted_iota(jnp.int32, (CHUNK, CHUNK), 1)
    i, j = (lane, sub) if transposed else (sub, lane)
    return (i <= j) if reverse else (i >= j)


def ssd_fwd(name, dtT, bias, alog, xbc, h0, direction, with_y):
    Bn, L = xbc.shape[:2]
    nc = L // CHUNK
    reverse = direction == 1
    rowblk = direction * N_BC

    def chunk_of(s):
        return (nc - 1 - s) if reverse else s

    def kern(dt_ref, bias_ref, alog_ref, x_ref, b_ref, c_ref, h0_ref, *rest):
        if with_y:
            y_ref, hs_ref, hf_ref, h_scr = rest
        else:
            hs_ref, hf_ref, h_scr = rest
        s = pl.program_id(2)

        @pl.when(s == 0)
        def _():
            h_scr[...] = h0_ref[0, 0]

        dt, _, cs, total = _ssd_scalars(dt_ref[0], bias_ref[0], alog_ref[0], reverse)
        u = cs - jnp.log(dt)
        dtt = jnp.exp(total - u)
        dc = jnp.exp(total)
        x_bf = x_ref[0].astype(BF16)
        bm = b_ref[0]
        h = h_scr[...]
        h_bf = h.astype(BF16)
        hs_ref[0, 0, 0] = h
        bt = bm.T
        if with_y:
            cm = c_ref[0]
            cb = _dot_nt(cm.astype(BF16), bm.astype(BF16))
            cs_cols = _rows_to_cols(cs)
            keep = _tri_mask(False, reverse)
        first = lax.broadcasted_iota(jnp.int32, (1, LANES), 1) < HEAD_DIM
        heads = range(HPG)
        psl = [slice((r // 2) * LANES, (r // 2 + 1) * LANES) for r in heads]
        lhs = []
        if with_y:
            for r in heads:
                cs_col = jnp.broadcast_to(cs_cols[:, r:r + 1], (CHUNK, LANES))
                wf = cb * jnp.exp(jnp.where(keep, cs_col - u[r:r + 1], -jnp.inf))
                lhs.append(jnp.concatenate([wf.astype(BF16), (cm * jnp.exp(cs_col)).astype(BF16)], axis=1))
        bts = [(bt * dtt[r:r + 1]).astype(BF16) for r in heads]
        sts = [_dot(bts[r], x_bf[:, psl[r]]) for r in heads]
        if with_y:
            ys = [_dot(lhs[r], jnp.concatenate([x_bf[:, psl[r]], h_bf[:, psl[r]]], axis=0)) for r in heads]
        for p in range(HPG // 2):
            if with_y:
                y_ref[0, :, psl[2 * p]] = jnp.where(first, ys[2 * p], ys[2 * p + 1])
            dc_p = jnp.where(first, dc[2 * p:2 * p + 1], dc[2 * p + 1:2 * p + 2])
            h_scr[:, psl[2 * p]] = h[:, psl[2 * p]] * dc_p + jnp.where(first, sts[2 * p], sts[2 * p + 1])

        @pl.when(s == nc - 1)
        def _():
            hf_ref[0, 0] = h_scr[...]

    in_specs = [
        pl.BlockSpec((1, HPG, CHUNK), lambda b, g, s: (b, rowblk + g, chunk_of(s))),
        pl.BlockSpec((1, HPG, 1), lambda b, g, s: (rowblk + g, 0, 0)),
        pl.BlockSpec((1, HPG, 1), lambda b, g, s: (rowblk + g, 0, 0)),
        pl.BlockSpec((1, CHUNK, GW), lambda b, g, s: (b, chunk_of(s), g)),
        pl.BlockSpec((1, CHUNK, D_STATE), lambda b, g, s: (b, chunk_of(s), D_INNER // D_STATE + g)),
        pl.BlockSpec((1, CHUNK, D_STATE), lambda b, g, s: (b, chunk_of(s), D_INNER // D_STATE + N_BC + g)),
        pl.BlockSpec((1, 1, D_STATE, GW), lambda b, g, s: (b, g, 0, 0)),
    ]
    out_shape, out_specs = [], []
    if with_y:
        out_shape.append(jax.ShapeDtypeStruct((Bn, L, D_INNER), F32))
        out_specs.append(pl.BlockSpec((1, CHUNK, GW), lambda b, g, s: (b, chunk_of(s), g)))
    out_shape += [jax.ShapeDtypeStruct((Bn, N_BC, nc, D_STATE, GW), F32), jax.ShapeDtypeStruct((Bn, N_BC, D_STATE, GW), F32)]
    out_specs += [pl.BlockSpec((1, 1, 1, D_STATE, GW), lambda b, g, s: (b, g, chunk_of(s), 0, 0)),
                  pl.BlockSpec((1, 1, D_STATE, GW), lambda b, g, s: (b, g, 0, 0))]
    return pl.pallas_call(
        kern, name=name, out_shape=out_shape, grid=(Bn, N_BC, nc), in_specs=in_specs, out_specs=out_specs,
        scratch_shapes=[pltpu.VMEM((D_STATE, GW), F32)],
        compiler_params=_cp(("arbitrary", "arbitrary", "arbitrary")))(dtT, bias, alog, xbc, xbc, xbc, h0)


def ssd_bwd(name, dtT, bias, alog, xbc, h_start, dy, dh_final, direction):
    Bn, L = xbc.shape[:2]
    nc = L // CHUNK
    reverse = direction == 1
    rowblk = direction * N_BC
    has_y = dy is not None
    last = 0 if reverse else CHUNK - 1

    def chunk_of(s):
        return s if reverse else (nc - 1 - s)

    def kern(*refs):
        if has_y:
            (dt_ref, bias_ref, alog_ref, x_ref, b_ref, hs_ref, dhf_ref, c_ref, dy_ref,
             dx_ref, db_ref, ddt_ref, dbias_ref, dalog_ref, dh0_ref, dc_ref, dh_scr) = refs
        else:
            (dt_ref, bias_ref, alog_ref, x_ref, b_ref, hs_ref, dhf_ref,
             dx_ref, db_ref, ddt_ref, dbias_ref, dalog_ref, dh0_ref, dh_scr) = refs
        s = pl.program_id(2)

        @pl.when(s == 0)
        def _():
            dh_scr[...] = dhf_ref[0, 0]
            dbias_ref[...] = jnp.zeros(dbias_ref.shape, F32)
            dalog_ref[...] = jnp.zeros(dalog_ref.shape, F32)

        dtraw = dt_ref[0]
        dt, A, cs, total = _ssd_scalars(dtraw, bias_ref[0], alog_ref[0], reverse)
        u = cs - jnp.log(dt)
        dtt = jnp.exp(total - u)
        dcy = jnp.exp(total)
        u_cols = _rows_to_cols(u)
        x_bf = x_ref[0].astype(BF16)
        bm = b_ref[0]
        bt = bm.T
        h = hs_ref[0, 0, 0]
        dh = dh_scr[...]
        dh_bf = dh.astype(BF16)
        dbt = jnp.zeros((D_STATE, CHUNK), F32)
        if has_y:
            cm = c_ref[0]
            ct = cm.T
            e_row = jnp.exp(cs)
            dy_bf = dy_ref[0].astype(BF16)
            h_bf = h.astype(BF16)
            cbt = _dot_nt(bm.astype(BF16), cm.astype(BF16))
            keep = _tri_mask(True, reverse)
            dcbt = jnp.zeros((CHUNK, CHUNK), F32)
            dct = jnp.zeros((D_STATE, CHUNK), F32)
        tots, out_rows, in_rows, in_cols = [], [], [], []
        first = lax.broadcasted_iota(jnp.int32, (1, LANES), 1) < HEAD_DIM
        heads = range(HPG)
        psl = [slice((r // 2) * LANES, (r // 2 + 1) * LANES) for r in heads]
        mine = [first if r % 2 == 0 else jnp.logical_not(first) for r in heads]
        zeros_bf = jnp.zeros((CHUNK, LANES), BF16)

        def prep(r):
            u_col = jnp.broadcast_to(u_cols[:, r:r + 1], (CHUNK, LANES))
            bs = (bm * jnp.exp(total[r:r + 1] - u_col)).astype(BF16)
            if not has_y:
                return bs, None
            et = jnp.exp(jnp.where(keep, cs[r:r + 1] - u_col, -jnp.inf))
            return jnp.concatenate([(cbt * et).astype(BF16), bs], axis=1), et

        def matmuls(r, lhs):
            p2raw = _dot_nt(dh_bf[:, psl[r]], jnp.where(mine[r], x_bf[:, psl[r]], zeros_bf))
            if not has_y:
                return p2raw, None, None, _dot(lhs, dh_bf[:, psl[r]])
            a1 = _dot_nt(jnp.concatenate([x_bf[:, psl[r]], h_bf[:, psl[r]]], axis=0),
                         jnp.where(mine[r], dy_bf[:, psl[r]], zeros_bf))
            new = _dot((ct * e_row[r:r + 1]).astype(BF16), dy_bf[:, psl[r]])
            dx = _dot(lhs, jnp.concatenate([dy_bf[:, psl[r]], dh_bf[:, psl[r]]], axis=0))
            return p2raw, a1, new, dx

        def post(r, p2raw, a1, et, dbt, dcbt, dct):
            if has_y:
                pt = a1[0:CHUNK] * et
                dcbt = dcbt + pt
                mt = pt * cbt
                ph = a1[CHUNK:] * e_row[r:r + 1]
                dct = dct + ph
                out_rows.append(_colsum(mt + ct * ph))
                in_cols.append(jnp.sum(mt, axis=1, keepdims=True))
            p2 = p2raw * dtt[r:r + 1]
            dbt = dbt + p2
            t_term = _colsum(bt * p2)
            in_rows.append(t_term)
            hdh = h[:, psl[r]] * dh[:, psl[r]]
            tot = jnp.sum(t_term, axis=1, keepdims=True) + dcy[r:r + 1] * jnp.sum(jnp.where(mine[r], hdh, 0.0), keepdims=True)
            tots.append(jnp.broadcast_to(tot, (1, CHUNK)))
            return dbt, dcbt, dct

        if not has_y:
            dcbt = dct = None
        dxs, news, pending = [], [], []
        batch = HPG
        for r0 in range(0, HPG, batch):
            preps = [prep(r) for r in range(r0, r0 + batch)]
            mms = [matmuls(r, preps[r - r0][0]) for r in range(r0, r0 + batch)]
            for args in pending:
                dbt, dcbt, dct = post(*args, dbt, dcbt, dct)
            pending = [(r, mms[r - r0][0], mms[r - r0][1], preps[r - r0][1]) for r in range(r0, r0 + batch)]
            dxs += [m[3] for m in mms]
            news += [m[2] for m in mms]
        for args in pending:
            dbt, dcbt, dct = post(*args, dbt, dcbt, dct)
        for p in range(HPG // 2):
            dx_ref[0, :, psl[2 * p]] = jnp.where(first, dxs[2 * p], dxs[2 * p + 1])
            new = dh[:, psl[2 * p]] * jnp.where(first, dcy[2 * p:2 * p + 1], dcy[2 * p + 1:2 * p + 2])
            if has_y:
                new = new + jnp.where(first, news[2 * p], news[2 * p + 1])
            dh_scr[:, psl[2 * p]] = new
        db = dbt.T
        if has_y:
            dcbt_bf = dcbt.astype(BF16)
            db = db + _dot(dcbt_bf, cm.astype(BF16))
            dc_ref[0] = dct.T + _dot_tn(dcbt_bf, bm.astype(BF16))
        db_ref[0] = db
        s_row = _stack_rows(in_rows)
        lane = lax.broadcasted_iota(jnp.int32, (HPG, CHUNK), 1)
        dcs = jnp.where(lane == last, _stack_rows(tots), 0.0)
        if has_y:
            s_row = s_row + _cols_to_rows(in_cols)
            dcs = dcs + _stack_rows(out_rows)
        dcs = dcs - s_row
        da = _cumsum_lanes(dcs, not reverse)
        ddt = da * A + jnp.where(dt > 0.0, s_row / dt, 0.0)
        ddtraw = ddt * _sigmoid(dtraw + bias_ref[0])
        ddt_ref[0] = ddtraw
        dbias_ref[0, 0] += jnp.sum(ddtraw, axis=1, keepdims=True)
        dalog_ref[0, 0] += jnp.sum(da * dt, axis=1, keepdims=True) * A

        @pl.when(s == nc - 1)
        def _():
            dh0_ref[0, 0] = dh_scr[...]

    cidx = lambda b, g, s: (b, chunk_of(s), g)
    hidx = lambda b, g, s: (b, g, 0, 0)
    in_specs = [
        pl.BlockSpec((1, HPG, CHUNK), lambda b, g, s: (b, rowblk + g, chunk_of(s))),
        pl.BlockSpec((1, HPG, 1), lambda b, g, s: (rowblk + g, 0, 0)),
        pl.BlockSpec((1, HPG, 1), lambda b, g, s: (rowblk + g, 0, 0)),
        pl.BlockSpec((1, CHUNK, GW), cidx),
        pl.BlockSpec((1, CHUNK, D_STATE), lambda b, g, s: (b, chunk_of(s), D_INNER // D_STATE + g)),
        pl.BlockSpec((1, 1, 1, D_STATE, GW), lambda b, g, s: (b, g, chunk_of(s), 0, 0)),
        pl.BlockSpec((1, 1, D_STATE, GW), hidx),
    ]
    args = [dtT, bias, alog, xbc, xbc, h_start, dh_final]
    if has_y:
        in_specs += [pl.BlockSpec((1, CHUNK, D_STATE), lambda b, g, s: (b, chunk_of(s), D_INNER // D_STATE + N_BC + g)),
                     pl.BlockSpec((1, CHUNK, GW), cidx)]
        args += [xbc, dy]
    out_shape = [jax.ShapeDtypeStruct((Bn, L, D_INNER), F32), jax.ShapeDtypeStruct((Bn, L, N_BC * D_STATE), F32),
                 jax.ShapeDtypeStruct((Bn, N_HEADS, L), F32), jax.ShapeDtypeStruct((Bn, N_BC, HPG, 1), F32),
                 jax.ShapeDtypeStruct((Bn, N_BC, HPG, 1), F32), jax.ShapeDtypeStruct((Bn, N_BC, D_STATE, GW), F32)]
    out_specs = [pl.BlockSpec((1, CHUNK, GW), cidx), pl.BlockSpec((1, CHUNK, D_STATE), cidx),
                 pl.BlockSpec((1, HPG, CHUNK), lambda b, g, s: (b, g, chunk_of(s))),
                 pl.BlockSpec((1, 1, HPG, 1), hidx), pl.BlockSpec((1, 1, HPG, 1), hidx), pl.BlockSpec((1, 1, D_STATE, GW), hidx)]
    if has_y:
        out_shape.append(jax.ShapeDtypeStruct((Bn, L, N_BC * D_STATE), F32))
        out_specs.append(pl.BlockSpec((1, CHUNK, D_STATE), cidx))
    res = pl.pallas_call(
        kern, name=name, out_shape=out_shape, grid=(Bn, N_BC, nc), in_specs=in_specs, out_specs=out_specs,
        scratch_shapes=[pltpu.VMEM((D_STATE, GW), F32)],
        compiler_params=_cp(("arbitrary", "arbitrary", "arbitrary")))(*args)
    dxs, db, ddt, dbias, dalog, dh0 = res[:6]
    return dxs, db, (res[6] if has_y else None), ddt, dbias, dalog, dh0


GPS = 4


def ssd_fwd3(name, dtT, bias, alog, xbc, h0, direction, with_y):
    Bn, L = xbc.shape[:2]
    nc = L // CHUNK
    reverse = direction == 1
    blk0 = direction * (N_BC // GPS)
    gs = range(GPS)

    def chunk_of(s):
        return (nc - 1 - s) if reverse else s

    def kern(dt_ref, bias_ref, alog_ref, x_ref, b_ref, c_ref, h0_ref, *rest):
        if with_y:
            y_ref, hs_ref, hf_ref, h_scr = rest
        else:
            hs_ref, hf_ref, h_scr = rest
        s = pl.program_id(2)

        @pl.when(s == 0)
        def _():
            h_scr[...] = h0_ref[0]

        first = lax.broadcasted_iota(jnp.int32, (1, LANES), 1) < HEAD_DIM
        heads = range(HPG)
        psl = [slice((r // 2) * LANES, (r // 2 + 1) * LANES) for r in heads]
        keep = _tri_mask(False, reverse)
        sc, x_bf, bm, h, h_bf, bt, cm, cb, cs_cols = [], [], [], [], [], [], [], [], []
        for g in gs:
            dt, _, cs, total = _ssd_scalars(dt_ref[0, g * HPG:(g + 1) * HPG], bias_ref[g], alog_ref[g], reverse)
            u = cs - jnp.log(dt)
            sc.append((cs, u, jnp.exp(total - u), jnp.exp(total)))
            x_bf.append(x_ref[0, :, g * GW:(g + 1) * GW].astype(BF16))
            bm.append(b_ref[0, :, g * D_STATE:(g + 1) * D_STATE])
            h.append(h_scr[g])
            h_bf.append(h[g].astype(BF16))
            hs_ref[0, g, 0] = h[g]
            bt.append(bm[g].T)
            if with_y:
                cm.append(c_ref[0, :, g * D_STATE:(g + 1) * D_STATE])
                cb.append(_dot_nt(cm[g].astype(BF16), bm[g].astype(BF16)))
                cs_cols.append(_rows_to_cols(cs))
        lhs = [[] for _ in gs]
        if with_y:
            for g in gs:
                cs, u = sc[g][0], sc[g][1]
                for r in heads:
                    cs_col = jnp.broadcast_to(cs_cols[g][:, r:r + 1], (CHUNK, LANES))
                    wf = cb[g] * jnp.exp(jnp.where(keep, cs_col - u[r:r + 1], -jnp.inf))
                    lhs[g].append(jnp.concatenate([wf.astype(BF16), (cm[g] * jnp.exp(cs_col)).astype(BF16)], axis=1))
        bts = [[(bt[g] * sc[g][2][r:r + 1]).astype(BF16) for r in heads] for g in gs]
        sts = [[_dot(bts[g][r], x_bf[g][:, psl[r]]) for r in heads] for g in gs]
        if with_y:
            ys = [[_dot(lhs[g][r], jnp.concatenate([x_bf[g][:, psl[r]], h_bf[g][:, psl[r]]], axis=0)) for r in heads] for g in gs]
        for g in gs:
            dc = sc[g][3]
            for p in range(HPG // 2):
                if with_y:
                    y_ref[0, :, g * GW + p * LANES:g * GW + (p + 1) * LANES] = jnp.where(first, ys[g][2 * p], ys[g][2 * p + 1])
                dc_p = jnp.where(first, dc[2 * p:2 * p + 1], dc[2 * p + 1:2 * p + 2])
                h_scr[g, :, psl[2 * p]] = h[g][:, psl[2 * p]] * dc_p + jnp.where(first, sts[g][2 * p], sts[g][2 * p + 1])

        @pl.when(s == nc - 1)
        def _():
            hf_ref[0] = h_scr[...]

    nb = D_INNER // (GPS * D_STATE)
    in_specs = [
        pl.BlockSpec((1, GPS * HPG, CHUNK), lambda b, g, s: (b, blk0 + g, chunk_of(s))),
        pl.BlockSpec((GPS, HPG, 1), lambda b, g, s: (blk0 + g, 0, 0)),
        pl.BlockSpec((GPS, HPG, 1), lambda b, g, s: (blk0 + g, 0, 0)),
        pl.BlockSpec((1, CHUNK, GPS * GW), lambda b, g, s: (b, chunk_of(s), g)),
        pl.BlockSpec((1, CHUNK, GPS * D_STATE), lambda b, g, s: (b, chunk_of(s), nb + g)),
        pl.BlockSpec((1, CHUNK, GPS * D_STATE), lambda b, g, s: (b, chunk_of(s), nb + N_BC // GPS + g)),
        pl.BlockSpec((1, GPS, D_STATE, GW), lambda b, g, s: (b, g, 0, 0)),
    ]
    out_shape, out_specs = [], []
    if with_y:
        out_shape.append(jax.ShapeDtypeStruct((Bn, L, D_INNER), F32))
        out_specs.append(pl.BlockSpec((1, CHUNK, GPS * GW), lambda b, g, s: (b, chunk_of(s), g)))
    out_shape += [jax.ShapeDtypeStruct((Bn, N_BC, nc, D_STATE, GW), F32), jax.ShapeDtypeStruct((Bn, N_BC, D_STATE, GW), F32)]
    out_specs += [pl.BlockSpec((1, GPS, 1, D_STATE, GW), lambda b, g, s: (b, g, chunk_of(s), 0, 0)),
                  pl.BlockSpec((1, GPS, D_STATE, GW), lambda b, g, s: (b, g, 0, 0))]
    return pl.pallas_call(
        kern, name=name, out_shape=out_shape, grid=(Bn, N_BC // GPS, nc), in_specs=in_specs, out_specs=out_specs,
        scratch_shapes=[pltpu.VMEM((GPS, D_STATE, GW), F32)],
        compiler_params=_cp(("arbitrary", "arbitrary", "arbitrary")))(dtT, bias, alog, xbc, xbc, xbc, h0)


def ssd_bwd3(name, dtT, bias, alog, xbc, h_start, dy, dh_final, direction):
    Bn, L = xbc.shape[:2]
    nc = L // CHUNK
    reverse = direction == 1
    blk0 = direction * (N_BC // GPS)
    has_y = dy is not None
    last = 0 if reverse else CHUNK - 1
    gs = range(GPS)

    def chunk_of(s):
        return s if reverse else (nc - 1 - s)

    def kern(*refs):
        if has_y:
            (dt_ref, bias_ref, alog_ref, x_ref, b_ref, hs_ref, dhf_ref, c_ref, dy_ref,
             dx_ref, db_ref, ddt_ref, dbias_ref, dalog_ref, dh0_ref, dc_ref, dh_scr) = refs
        else:
            (dt_ref, bias_ref, alog_ref, x_ref, b_ref, hs_ref, dhf_ref,
             dx_ref, db_ref, ddt_ref, dbias_ref, dalog_ref, dh0_ref, dh_scr) = refs
        s = pl.program_id(2)

        @pl.when(s == 0)
        def _():
            dh_scr[...] = dhf_ref[0]
            dbias_ref[...] = jnp.zeros(dbias_ref.shape, F32)
            dalog_ref[...] = jnp.zeros(dalog_ref.shape, F32)

        first = lax.broadcasted_iota(jnp.int32, (1, LANES), 1) < HEAD_DIM
        heads = range(HPG)
        psl = [slice((r // 2) * LANES, (r // 2 + 1) * LANES) for r in heads]
        mine = [first if r % 2 == 0 else jnp.logical_not(first) for r in heads]
        zeros_bf = jnp.zeros((CHUNK, LANES), BF16)
        keep = _tri_mask(True, reverse)
        ctx = []
        for g in gs:
            dtraw = dt_ref[0, g * HPG:(g + 1) * HPG]
            dt, A, cs, total = _ssd_scalars(dtraw, bias_ref[g], alog_ref[g], reverse)
            u = cs - jnp.log(dt)
            c = dict(dtraw=dtraw, dt=dt, A=A, cs=cs, total=total, u=u, dtt=jnp.exp(total - u), dcy=jnp.exp(total),
                     u_cols=_rows_to_cols(u), x_bf=x_ref[0, :, g * GW:(g + 1) * GW].astype(BF16),
                     bm=b_ref[0, :, g * D_STATE:(g + 1) * D_STATE], h=hs_ref[0, g, 0], dh=dh_scr[g])
            c["bt"] = c["bm"].T
            c["dh_bf"] = c["dh"].astype(BF16)
            if has_y:
                c["cm"] = c_ref[0, :, g * D_STATE:(g + 1) * D_STATE]
                c["ct"] = c["cm"].T
                c["e_row"] = jnp.exp(cs)
                c["dy_bf"] = dy_ref[0, :, g * GW:(g + 1) * GW].astype(BF16)
                c["h_bf"] = c["h"].astype(BF16)
                c["cbt"] = _dot_nt(c["bm"].astype(BF16), c["cm"].astype(BF16))
            ctx.append(c)
        for c in ctx:
            c["lhs"], c["et"] = [], []
            for r in heads:
                u_col = jnp.broadcast_to(c["u_cols"][:, r:r + 1], (CHUNK, LANES))
                bs = (c["bm"] * jnp.exp(c["total"][r:r + 1] - u_col)).astype(BF16)
                if has_y:
                    et = jnp.exp(jnp.where(keep, c["cs"][r:r + 1] - u_col, -jnp.inf))
                    c["et"].append(et)
                    c["lhs"].append(jnp.concatenate([(c["cbt"] * et).astype(BF16), bs], axis=1))
                else:
                    c["lhs"].append(bs)
        for c in ctx:
            c["p2raw"] = [_dot_nt(c["dh_bf"][:, psl[r]], jnp.where(mine[r], c["x_bf"][:, psl[r]], zeros_bf)) for r in heads]
            if has_y:
                c["a1"] = [_dot_nt(jnp.concatenate([c["x_bf"][:, psl[r]], c["h_bf"][:, psl[r]]], axis=0),
                                   jnp.where(mine[r], c["dy_bf"][:, psl[r]], zeros_bf)) for r in heads]
                c["news"] = [_dot((c["ct"] * c["e_row"][r:r + 1]).astype(BF16), c["dy_bf"][:, psl[r]]) for r in heads]
                c["dxs"] = [_dot(c["lhs"][r], jnp.concatenate([c["dy_bf"][:, psl[r]], c["dh_bf"][:, psl[r]]], axis=0)) for r in heads]
            else:
                c["dxs"] = [_dot(c["lhs"][r], c["dh_bf"][:, psl[r]]) for r in heads]
        for g, c in enumerate(ctx):
            dbt = jnp.zeros((D_STATE, CHUNK), F32)
            dcbt = jnp.zeros((CHUNK, CHUNK), F32)
            dct = jnp.zeros((D_STATE, CHUNK), F32)
            tots, out_rows, in_rows, in_cols = [], [], [], []
            for r in heads:
                if has_y:
                    pt = c["a1"][r][0:CHUNK] * c["et"][r]
                    dcbt = dcbt + pt
                    mt = pt * c["cbt"]
                    ph = c["a1"][r][CHUNK:] * c["e_row"][r:r + 1]
                    dct = dct + ph
                    out_rows.append(_colsum(mt + c["ct"] * ph))
                    in_cols.append(jnp.sum(mt, axis=1, keepdims=True))
                p2 = c["p2raw"][r] * c["dtt"][r:r + 1]
                dbt = dbt + p2
                t_term = _colsum(c["bt"] * p2)
                in_rows.append(t_term)
                hdh = c["h"][:, psl[r]] * c["dh"][:, psl[r]]
                tot = jnp.sum(t_term, axis=1, keepdims=True) + c["dcy"][r:r + 1] * jnp.sum(jnp.where(mine[r], hdh, 0.0), keepdims=True)
                tots.append(jnp.broadcast_to(tot, (1, CHUNK)))
            for p in range(HPG // 2):
                dx_ref[0, :, g * GW + p * LANES:g * GW + (p + 1) * LANES] = jnp.where(first, c["dxs"][2 * p], c["dxs"][2 * p + 1])
                new = c["dh"][:, psl[2 * p]] * jnp.where(first, c["dcy"][2 * p:2 * p + 1], c["dcy"][2 * p + 1:2 * p + 2])
                if has_y:
                    new = new + jnp.where(first, c["news"][2 * p], c["news"][2 * p + 1])
                dh_scr[g, :, psl[2 * p]] = new
            db = dbt.T
            if has_y:
                dcbt_bf = dcbt.astype(BF16)
                db = db + _dot(dcbt_bf, c["cm"].astype(BF16))
                dc_ref[0, :, g * D_STATE:(g + 1) * D_STATE] = dct.T + _dot_tn(dcbt_bf, c["bm"].astype(BF16))
            db_ref[0, :, g * D_STATE:(g + 1) * D_STATE] = db
            s_row = _stack_rows(in_rows)
            lane = lax.broadcasted_iota(jnp.int32, (HPG, CHUNK), 1)
            dcs = jnp.where(lane == last, _stack_rows(tots), 0.0)
            if has_y:
                s_row = s_row + _cols_to_rows(in_cols)
                dcs = dcs + _stack_rows(out_rows)
            dcs = dcs - s_row
            da = _cumsum_lanes(dcs, not reverse)
            ddt = da * c["A"] + jnp.where(c["dt"] > 0.0, s_row / c["dt"], 0.0)
            ddtraw = ddt * _sigmoid(c["dtraw"] + bias_ref[g])
            ddt_ref[0, g * HPG:(g + 1) * HPG] = ddtraw
            dbias_ref[0, g] += jnp.sum(ddtraw, axis=1, keepdims=True)
            dalog_ref[0, g] += jnp.sum(da * c["dt"], axis=1, keepdims=True) * c["A"]

        @pl.when(s == nc - 1)
        def _():
            dh0_ref[0] = dh_scr[...]

    nb = D_INNER // (GPS * D_STATE)
    cidx = lambda b, g, s: (b, chunk_of(s), g)
    hidx = lambda b, g, s: (b, g, 0, 0)
    in_specs = [
        pl.BlockSpec((1, GPS * HPG, CHUNK), lambda b, g, s: (b, blk0 + g, chunk_of(s))),
        pl.BlockSpec((GPS, HPG, 1), lambda b, g, s: (blk0 + g, 0, 0)),
        pl.BlockSpec((GPS, HPG, 1), lambda b, g, s: (blk0 + g, 0, 0)),
        pl.BlockSpec((1, CHUNK, GPS * GW), cidx),
        pl.BlockSpec((1, CHUNK, GPS * D_STATE), lambda b, g, s: (b, chunk_of(s), nb + g)),
        pl.BlockSpec((1, GPS, 1, D_STATE, GW), lambda b, g, s: (b, g, chunk_of(s), 0, 0)),
        pl.BlockSpec((1, GPS, D_STATE, GW), hidx),
    ]
    args = [dtT, bias, alog, xbc, xbc, h_start, dh_final]
    if has_y:
        in_specs += [pl.BlockSpec((1, CHUNK, GPS * D_STATE), lambda b, g, s: (b, chunk_of(s), nb + N_BC // GPS + g)),
                     pl.BlockSpec((1, CHUNK, GPS * GW), cidx)]
        args += [xbc, dy]
    out_shape = [jax.ShapeDtypeStruct((Bn, L, D_INNER), F32), jax.ShapeDtypeStruct((Bn, L, N_BC * D_STATE), F32),
                 jax.ShapeDtypeStruct((Bn, N_HEADS, L), F32), jax.ShapeDtypeStruct((Bn, N_BC, HPG, 1), F32),
                 jax.ShapeDtypeStruct((Bn, N_BC, HPG, 1), F32), jax.ShapeDtypeStruct((Bn, N_BC, D_STATE, GW), F32)]
    out_specs = [pl.BlockSpec((1, CHUNK, GPS * GW), cidx), pl.BlockSpec((1, CHUNK, GPS * D_STATE), cidx),
                 pl.BlockSpec((1, GPS * HPG, CHUNK), lambda b, g, s: (b, g, chunk_of(s))),
                 pl.BlockSpec((1, GPS, HPG, 1), hidx), pl.BlockSpec((1, GPS, HPG, 1), hidx), pl.BlockSpec((1, GPS, D_STATE, GW), hidx)]
    if has_y:
        out_shape.append(jax.ShapeDtypeStruct((Bn, L, N_BC * D_STATE), F32))
        out_specs.append(pl.BlockSpec((1, CHUNK, GPS * D_STATE), cidx))
    res = pl.pallas_call(
        kern, name=name, out_shape=out_shape, grid=(Bn, N_BC // GPS, nc), in_specs=in_specs, out_specs=out_specs,
        scratch_shapes=[pltpu.VMEM((GPS, D_STATE, GW), F32)],
        compiler_params=_cp(("arbitrary", "arbitrary", "arbitrary")))(*args)
    dxs, db, ddt, dbias, dalog, dh0 = res[:6]
    return dxs, db, (res[6] if has_y else None), ddt, dbias, dalog, dh0


def _dot_split2(v, sel):
    hi = v.astype(BF16)
    mid = (v - hi.astype(F32)).astype(BF16)
    return _dot(hi, sel) + _dot(mid, sel)


def ssd_tables():
    lane = jnp.arange(LANES)[:, None]
    col = jnp.arange(2 * GW)[None, :]
    expand = (lane == jnp.where(col < GW, HPG + col // HEAD_DIM, 2 * HPG + (col - GW) // HEAD_DIM)).astype(BF16)
    ch = jnp.arange(GW)[:, None] // HEAD_DIM
    out = jnp.arange(2 * LANES)[None, :]
    seg = ((out == ch) | (out == LANES + HPG + ch)).astype(BF16)
    return expand, seg


def _dc_lanes(dc, first):
    return jnp.concatenate([jnp.where(first, dc[2 * p:2 * p + 1], dc[2 * p + 1:2 * p + 2]) for p in range(HPG // 2)], axis=1)


def ssd_fwd2(name, dtT, bias, alog, xbc, h0, tables, direction, with_y):
    Bn, L = xbc.shape[:2]
    nc = L // CHUNK
    reverse = direction == 1
    rowblk = direction * N_BC
    expand = tables[0]

    def chunk_of(s):
        return (nc - 1 - s) if reverse else s

    def kern(dt_ref, bias_ref, alog_ref, x_ref, b_ref, c_ref, h0_ref, xp_ref, *rest):
        if with_y:
            y_ref, hs_ref, hf_ref, h_scr = rest
        else:
            hs_ref, hf_ref, h_scr = rest
        s = pl.program_id(2)

        @pl.when(s == 0)
        def _():
            h_scr[...] = h0_ref[0, 0]

        dt, _, cs, total = _ssd_scalars(dt_ref[0], bias_ref[0], alog_ref[0], reverse)
        u = cs - jnp.log(dt)
        dtt = jnp.exp(total - u)
        cols = _rows_to_cols(jnp.concatenate([cs, dtt, jnp.exp(cs)], axis=0))
        wide = _dot_split2(cols, xp_ref[...])
        dtt_x, e_x = wide[:, 0:GW], wide[:, GW:]
        first = lax.broadcasted_iota(jnp.int32, (1, LANES), 1) < HEAD_DIM
        x = x_ref[0]
        x_bf = x.astype(BF16)
        bm = b_ref[0]
        h = h_scr[...]
        hs_ref[0, 0, 0] = h
        st = _dot(bm.T.astype(BF16), (x * dtt_x).astype(BF16))
        h_scr[...] = h * _dc_lanes(jnp.exp(total), first) + st
        if with_y:
            cm = c_ref[0].astype(BF16)
            cb = _dot_nt(cm, bm.astype(BF16))
            yoff = _dot(cm, h.astype(BF16)) * e_x
            keep = _tri_mask(False, reverse)
            wfs = []
            for r in range(HPG):
                cs_col = jnp.broadcast_to(cols[:, r:r + 1], (CHUNK, LANES))
                wfs.append((cb * jnp.exp(jnp.where(keep, cs_col - u[r:r + 1], -jnp.inf))).astype(BF16))
            yd = [_dot(wfs[r], x_bf[:, (r // 2) * LANES:(r // 2 + 1) * LANES]) for r in range(HPG)]
            for p in range(HPG // 2):
                psl = slice(p * LANES, (p + 1) * LANES)
                y_ref[0, :, psl] = jnp.where(first, yd[2 * p], yd[2 * p + 1]) + yoff[:, psl]

        @pl.when(s == nc - 1)
        def _():
            hf_ref[0, 0] = h_scr[...]

    in_specs = [
        pl.BlockSpec((1, HPG, CHUNK), lambda b, g, s: (b, rowblk + g, chunk_of(s))),
        pl.BlockSpec((1, HPG, 1), lambda b, g, s: (rowblk + g, 0, 0)),
        pl.BlockSpec((1, HPG, 1), lambda b, g, s: (rowblk + g, 0, 0)),
        pl.BlockSpec((1, CHUNK, GW), lambda b, g, s: (b, chunk_of(s), g)),
        pl.BlockSpec((1, CHUNK, D_STATE), lambda b, g, s: (b, chunk_of(s), D_INNER // D_STATE + g)),
        pl.BlockSpec((1, CHUNK, D_STATE), lambda b, g, s: (b, chunk_of(s), D_INNER // D_STATE + N_BC + g)),
        pl.BlockSpec((1, 1, D_STATE, GW), lambda b, g, s: (b, g, 0, 0)),
        pl.BlockSpec(expand.shape, lambda b, g, s: (0, 0)),
    ]
    out_shape, out_specs = [], []
    if with_y:
        out_shape.append(jax.ShapeDtypeStruct((Bn, L, D_INNER), F32))
        out_specs.append(pl.BlockSpec((1, CHUNK, GW), lambda b, g, s: (b, chunk_of(s), g)))
    out_shape += [jax.ShapeDtypeStruct((Bn, N_BC, nc, D_STATE, GW), F32), jax.ShapeDtypeStruct((Bn, N_BC, D_STATE, GW), F32)]
    out_specs += [pl.BlockSpec((1, 1, 1, D_STATE, GW), lambda b, g, s: (b, g, chunk_of(s), 0, 0)),
                  pl.BlockSpec((1, 1, D_STATE, GW), lambda b, g, s: (b, g, 0, 0))]
    return pl.pallas_call(
        kern, name=name, out_shape=out_shape, grid=(Bn, N_BC, nc), in_specs=in_specs, out_specs=out_specs,
        scratch_shapes=[pltpu.VMEM((D_STATE, GW), F32)],
        compiler_params=_cp(("arbitrary", "arbitrary", "arbitrary")))(dtT, bias, alog, xbc, xbc, xbc, h0, expand)


def ssd_bwd2(name, dtT, bias, alog, xbc, h_start, dy, dh_final, tables, direction):
    Bn, L = xbc.shape[:2]
    nc = L // CHUNK
    reverse = direction == 1
    rowblk = direction * N_BC
    has_y = dy is not None
    last = 0 if reverse else CHUNK - 1
    expand, seg = tables

    def chunk_of(s):
        return s if reverse else (nc - 1 - s)

    def kern(*refs):
        if has_y:
            (dt_ref, bias_ref, alog_ref, x_ref, b_ref, hs_ref, dhf_ref, xp_ref, seg_ref, c_ref, dy_ref,
             dx_ref, db_ref, ddt_ref, dbias_ref, dalog_ref, dh0_ref, dc_ref, dh_scr) = refs
        else:
            (dt_ref, bias_ref, alog_ref, x_ref, b_ref, hs_ref, dhf_ref, xp_ref, seg_ref,
             dx_ref, db_ref, ddt_ref, dbias_ref, dalog_ref, dh0_ref, dh_scr) = refs
        s = pl.program_id(2)

        @pl.when(s == 0)
        def _():
            dh_scr[...] = dhf_ref[0, 0]
            dbias_ref[...] = jnp.zeros(dbias_ref.shape, F32)
            dalog_ref[...] = jnp.zeros(dalog_ref.shape, F32)

        first = lax.broadcasted_iota(jnp.int32, (1, LANES), 1) < HEAD_DIM
        heads = range(HPG)
        psl = [slice((r // 2) * LANES, (r // 2 + 1) * LANES) for r in heads]
        x = x_ref[0]
        bm = b_ref[0].astype(BF16)
        h = hs_ref[0, 0, 0]
        dh = dh_scr[...]
        dh_bf = dh.astype(BF16)
        bdh = _dot(bm, dh_bf)
        if has_y:
            cm = c_ref[0].astype(BF16)
            dyv = dy_ref[0]
            dy_bf = dyv.astype(BF16)
            x_bf = x.astype(BF16)
            h_bf = h.astype(BF16)
            cbt = _dot_nt(bm, cm)
            ch = _dot(cm, h_bf)
            zeros_bf = jnp.zeros((CHUNK, LANES), BF16)
            gts = [_dot_nt(x_bf[:, psl[r]], jnp.where(first if r % 2 == 0 else jnp.logical_not(first), dy_bf[:, psl[r]], zeros_bf))
                   for r in heads]
            ct_bf = c_ref[0].T.astype(BF16)
        dtraw = dt_ref[0]
        dt, A, cs, total = _ssd_scalars(dtraw, bias_ref[0], alog_ref[0], reverse)
        u = cs - jnp.log(dt)
        dtt = jnp.exp(total - u)
        dcy = jnp.exp(total)
        cols = _rows_to_cols(jnp.concatenate([u, dtt, jnp.exp(cs)], axis=0))
        wide = _dot_split2(cols, xp_ref[...])
        dtt_x, e_x = wide[:, 0:GW], wide[:, GW:]
        term2 = bdh * dtt_x
        dbt = _dot_nt(dh_bf, (x * dtt_x).astype(BF16))
        sums = _dot_split2(term2 * x, seg_ref[:, LANES:])
        new_dh = dh * _dc_lanes(dcy, first)
        if has_y:
            dye = dyv * e_x
            dye_bf = dye.astype(BF16)
            dct = _dot_nt(h_bf, dye_bf)
            new_dh = new_dh + _dot(ct_bf, dye_bf)
            sums = sums + _dot_split2(ch * dye, seg_ref[:, 0:LANES])
            keep = _tri_mask(True, reverse)
            ets = []
            for r in heads:
                u_col = jnp.broadcast_to(cols[:, r:r + 1], (CHUNK, LANES))
                ets.append(jnp.exp(jnp.where(keep, cs[r:r + 1] - u_col, -jnp.inf)))
            wts = [(cbt * ets[r]).astype(BF16) for r in heads]
            dxd = [_dot(wts[r], dy_bf[:, psl[r]]) for r in heads]
            dcbt = jnp.zeros((CHUNK, CHUNK), F32)
            out_rows, in_cols = [], []
            for r in heads:
                pt = gts[r] * ets[r]
                dcbt = dcbt + pt
                mt = pt * cbt
                out_rows.append(_colsum(mt))
                in_cols.append(jnp.sum(mt, axis=1, keepdims=True))
            for p in range(HPG // 2):
                dx_ref[0, :, psl[2 * p]] = jnp.where(first, dxd[2 * p], dxd[2 * p + 1]) + term2[:, psl[2 * p]]
            dcbt_bf = dcbt.astype(BF16)
            db_ref[0] = dbt.T + _dot(dcbt_bf, cm)
            dc_ref[0] = dct.T + _dot_tn(dcbt_bf, bm)
        else:
            dx_ref[0] = term2
            db_ref[0] = dbt.T
        dh_scr[...] = new_dh
        sums_t = sums.T
        s_row = sums_t[HPG:2 * HPG]
        hdh = _colsum(h * dh)
        lanes_w = lax.broadcasted_iota(jnp.int32, (1, GW), 1)
        hd = _stack_rows([jnp.sum(jnp.where(lanes_w // HEAD_DIM == r, hdh, 0.0), axis=1, keepdims=True) for r in range(HPG)])
        tot = jnp.sum(s_row, axis=1, keepdims=True) + dcy * hd
        lane = lax.broadcasted_iota(jnp.int32, (HPG, CHUNK), 1)
        dcs = jnp.where(lane == last, tot, 0.0)
        if has_y:
            s_row = s_row + _cols_to_rows(in_cols)
            dcs = dcs + _stack_rows(out_rows) + sums_t[0:HPG]
        dcs = dcs - s_row
        da = _cumsum_lanes(dcs, not reverse)
        ddt = da * A + jnp.where(dt > 0.0, s_row / dt, 0.0)
        ddtraw = ddt * _sigmoid(dtraw + bias_ref[0])
        ddt_ref[0] = ddtraw
        dbias_ref[0, 0] += jnp.sum(ddtraw, axis=1, keepdims=True)
        dalog_ref[0, 0] += jnp.sum(da * dt, axis=1, keepdims=True) * A

        @pl.when(s == nc - 1)
        def _():
            dh0_ref[0, 0] = dh_scr[...]

    cidx = lambda b, g, s: (b, chunk_of(s), g)
    hidx = lambda b, g, s: (b, g, 0, 0)
    in_specs = [
        pl.BlockSpec((1, HPG, CHUNK), lambda b, g, s: (b, rowblk + g, chunk_of(s))),
        pl.BlockSpec((1, HPG, 1), lambda b, g, s: (rowblk + g, 0, 0)),
        pl.BlockSpec((1, HPG, 1), lambda b, g, s: (rowblk + g, 0, 0)),
        pl.BlockSpec((1, CHUNK, GW), cidx),
        pl.BlockSpec((1, CHUNK, D_STATE), lambda b, g, s: (b, chunk_of(s), D_INNER // D_STATE + g)),
        pl.BlockSpec((1, 1, 1, D_STATE, GW), lambda b, g, s: (b, g, chunk_of(s), 0, 0)),
        pl.BlockSpec((1, 1, D_STATE, GW), hidx),
        pl.BlockSpec(expand.shape, lambda b, g, s: (0, 0)),
        pl.BlockSpec(seg.shape, lambda b, g, s: (0, 0)),
    ]
    args = [dtT, bias, alog, xbc, xbc, h_start, dh_final, expand, seg]
    if has_y:
        in_specs += [pl.BlockSpec((1, CHUNK, D_STATE), lambda b, g, s: (b, chunk_of(s), D_INNER // D_STATE + N_BC + g)),
                     pl.BlockSpec((1, CHUNK, GW), cidx)]
        args += [xbc, dy]
    out_shape = [jax.ShapeDtypeStruct((Bn, L, D_INNER), F32), jax.ShapeDtypeStruct((Bn, L, N_BC * D_STATE), F32),
                 jax.ShapeDtypeStruct((Bn, N_HEADS, L), F32), jax.ShapeDtypeStruct((Bn, N_BC, HPG, 1), F32),
                 jax.ShapeDtypeStruct((Bn, N_BC, HPG, 1), F32), jax.ShapeDtypeStruct((Bn, N_BC, D_STATE, GW), F32)]
    out_specs = [pl.BlockSpec((1, CHUNK, GW), cidx), pl.BlockSpec((1, CHUNK, D_STATE), cidx),
                 pl.BlockSpec((1, HPG, CHUNK), lambda b, g, s: (b, g, chunk_of(s))),
                 pl.BlockSpec((1, 1, HPG, 1), hidx), pl.BlockSpec((1, 1, HPG, 1), hidx), pl.BlockSpec((1, 1, D_STATE, GW), hidx)]
    if has_y:
        out_shape.append(jax.ShapeDtypeStruct((Bn, L, N_BC * D_STATE), F32))
        out_specs.append(pl.BlockSpec((1, CHUNK, D_STATE), cidx))
    res = pl.pallas_call(
        kern, name=name, out_shape=out_shape, grid=(Bn, N_BC, nc), in_specs=in_specs, out_specs=out_specs,
        scratch_shapes=[pltpu.VMEM((D_STATE, GW), F32)],
        compiler_params=_cp(("arbitrary", "arbitrary", "arbitrary")))(*args)
    dxs, db, ddt, dbias, dalog, dh0 = res[:6]
    return dxs, db, (res[6] if has_y else None), ddt, dbias, dalog, dh0


def _group_mean(v):
    gw = D_INNER // N_BC
    parts = [jnp.broadcast_to(jnp.mean(v[:, g * gw:(g + 1) * gw], axis=-1, keepdims=True), (v.shape[0], gw)) for g in range(N_BC)]
    return jnp.concatenate(parts, axis=1)


def gated_norm_fwd(name, y_f, y_b, xs_src, z, dskip_lanes, w_norm):
    def body(yf, yb, xs, z, dsk, w):
        u = (yf + yb + dsk * xs) * _silu(z)
        r = lax.rsqrt(_group_mean(u * u) + NORM_EPS)
        return u * r * w

    return tok_call(name, body, [y_f, y_b, xs_src, z], [], [dskip_lanes, w_norm], [(D_INNER, BF16)], [], [])[0]


def _dot_exact01(v, sel):
    hi, mid, lo = _split3(v)
    return _dot(hi, sel) + _dot(mid, sel) + _dot(lo, sel)


def gated_norm_bwd(name, y_f, y_b, xs_src, z, d_out, dskip_lanes, w_norm, head_sel):
    def body(yf, yb, xs, z, do, dsk, w, sel):
        y = yf + yb + dsk * xs
        sz = _silu(z)
        u = y * sz
        r = lax.rsqrt(_group_mean(u * u) + NORM_EPS)
        duh = do * w
        du = r * (duh - u * (r * r) * _group_mean(duh * u))
        dy = du * sz
        dz = du * y * _dsilu(z)
        dsk_heads = _dot_exact01(jnp.broadcast_to(_colsum(dy * xs), (8, D_INNER)), sel)
        return dy, dz, _colsum(do * u * r), dsk_heads

    return tok_call(name, body, [y_f, y_b, xs_src, z, d_out], [], [dskip_lanes, w_norm, head_sel],
                    [(D_INNER, F32), (D_INNER, BF16)], [], [(1, D_INNER), (8, LANES)], tm=128)


def merge_fwd(name, y_pool, y_ssd, gatepre, x, target, gate, b_merge, norm_post, w_pp, w_ps, w_out):
    def body(yp, ys, gp, x, tgt, gate, bm, wpost, w_pp, w_ps, w_out):
        p1 = _dot(yp, w_pp)
        p2 = _dot(ys, w_ps)
        gates = _sigmoid(gp + bm)
        merged = gates[:, :D] * p1 + gates[:, D:] * p2
        out = _dot(merged.astype(BF16), w_out)
        r = _rms_r(out)
        outr = out * r
        nq = outr * wpost
        err = x + gate * nq - tgt
        loss = 0.5 * jnp.sum(jnp.mean(err * err, axis=-1, keepdims=True), keepdims=True).reshape(1, 1)
        g = err * (1.0 / D)
        dnq = g * gate
        dout = _rms_bwd(dnq * wpost, out, r)
        return merged, p1, p2, dout, g, _colsum(g * nq), _colsum(dnq * outr), jnp.broadcast_to(loss, (1, LANES))

    return tok_call(name, body, [y_pool, y_ssd, gatepre, x, target], [gate], [b_merge, norm_post, w_pp, w_ps, w_out],
                    [(D, BF16), (D, F32), (D, F32), (D, BF16), (D, F32)], [D], [(1, D), (1, LANES)])


def merge_bwd(name, dout, gatepre, p1, p2, b_merge, w_pp, w_ps, w_out):
    def body(dout, gp, p1, p2, bm, w_pp, w_ps, w_out):
        dmerged = _dot_nt(dout, w_out)
        gates = _sigmoid(gp + bm)
        g1, g2 = gates[:, :D], gates[:, D:]
        dp1 = (dmerged * g1).astype(BF16)
        dp2 = (dmerged * g2).astype(BF16)
        dgp = jnp.concatenate([dmerged * p1 * g1 * (1.0 - g1), dmerged * p2 * g2 * (1.0 - g2)], axis=1)
        return dp1, dp2, dgp, _dot_nt(dp1, w_pp), _dot_nt(dp2, w_ps), _colsum(dgp)

    return tok_call(name, body, [dout, gatepre, p1, p2], [], [b_merge, w_pp, w_ps, w_out],
                    [(D, BF16), (D, BF16), (2 * D, BF16), (D, F32), (D_INNER, F32)], [], [(1, 2 * D)])


def _adamw_math(w, g, m, v):
    m = ADAM_B1 * m + (1.0 - ADAM_B1) * g
    v = ADAM_B2 * v + (1.0 - ADAM_B2) * (g * g)
    m_hat = m / (1.0 - ADAM_B1 ** ADAM_STEP)
    v_hat = v / (1.0 - ADAM_B2 ** ADAM_STEP)
    delta = -ADAM_LR * (m_hat / (jnp.sqrt(v_hat) + ADAM_EPS) + ADAM_WD * w)
    return delta, m, v


def adamw(name, w, g, m, v, tr=256):
    R, C = w.shape
    tr = min(tr, R)
    assert R % tr == 0

    def body(w_ref, g_ref, m_ref, v_ref, d_ref, nm_ref, nv_ref):
        d, nm, nv = _adamw_math(w_ref[...], g_ref[...], m_ref[...], v_ref[...])
        d_ref[...] = d
        nm_ref[...] = nm
        nv_ref[...] = nv

    spec = pl.BlockSpec((tr, C), lambda i: (i, 0))
    return pl.pallas_call(
        body, name=name, out_shape=[jax.ShapeDtypeStruct((R, C), F32)] * 3, grid=(R // tr,),
        in_specs=[spec] * 4, out_specs=[spec] * 3, compiler_params=_cp(("parallel",)))(w, g, m, v)


def _me():
    return lax.axis_index("x"), lax.axis_index("y"), lax.axis_index("c")


def all_gather_small(name, v):
    R, C = v.shape

    def body(v_ref, out_ref, send_sems, recv_sems, local_sem):
        x, y, c = _me()
        me = 4 * x + 2 * y + c
        mine = pltpu.make_async_copy(v_ref, out_ref.at[me], local_sem)
        mine.start()
        copies = []
        for d in range(1, N_DEV):
            dx, dy, dc = d // 4, (d // 2) % 2, d % 2
            px, py, pc = x ^ dx, y ^ dy, c ^ dc
            copies.append(pltpu.make_async_remote_copy(
                src_ref=v_ref, dst_ref=out_ref.at[me], send_sem=send_sems.at[d - 1], recv_sem=recv_sems.at[d - 1],
                device_id=(px, py, pc), device_id_type=MESH))
        for cp in copies:
            cp.start()
        for d in range(1, N_DEV):
            dx, dy, dc = d // 4, (d // 2) % 2, d % 2
            peer = 4 * (x ^ dx) + 2 * (y ^ dy) + (c ^ dc)
            pltpu.make_async_remote_copy(
                src_ref=v_ref, dst_ref=out_ref.at[peer], send_sem=send_sems.at[d - 1], recv_sem=recv_sems.at[d - 1],
                device_id=(x ^ dx, y ^ dy, c ^ dc), device_id_type=MESH).wait_recv()
        for cp in copies:
            cp.wait_send()
        mine.wait()

    return pl.pallas_call(
        body, name=name, out_shape=jax.ShapeDtypeStruct((N_DEV, R, C), F32),
        in_specs=[pl.BlockSpec(memory_space=pltpu.VMEM)], out_specs=pl.BlockSpec(memory_space=pltpu.VMEM),
        scratch_shapes=[pltpu.SemaphoreType.DMA((N_DEV - 1,)), pltpu.SemaphoreType.DMA((N_DEV - 1,)), pltpu.SemaphoreType.DMA],
        compiler_params=pltpu.CompilerParams(vmem_limit_bytes=VMEM_LIMIT))(v)


def all_gather_chips(name, shard):
    R, C = shard.shape
    half = R // 2
    assert R % 32 == 0

    def body(s_ref, out_ref, send_sems, recv_sems, local_sem):
        x, y, c = _me()
        k = 2 * x + y
        chips = [(1 - x, y), (x, 1 - y), (1 - x, 1 - y)]

        def rows(chip, hc):
            return out_ref.at[2 * chip[0] + chip[1], pl.ds(hc * half, half), :]

        mine = pltpu.make_async_copy(s_ref, out_ref.at[k], local_sem)
        mine.start()
        first = [pltpu.make_async_remote_copy(
            src_ref=s_ref.at[pl.ds(c * half, half), :], dst_ref=rows((x, y), c), send_sem=send_sems.at[j],
            recv_sem=recv_sems.at[j], device_id=(*chip, c), device_id_type=MESH) for j, chip in enumerate(chips)]
        for cp in first:
            cp.start()
        passed = [pltpu.make_async_remote_copy(
            src_ref=rows(chip, c), dst_ref=rows(chip, c), send_sem=send_sems.at[3 + j], recv_sem=recv_sems.at[3 + j],
            device_id=(x, y, 1 - c), device_id_type=MESH) for j, chip in enumerate(chips)]
        for j, chip in enumerate(chips):
            pltpu.make_async_remote_copy(
                src_ref=rows(chip, c), dst_ref=rows(chip, c), send_sem=send_sems.at[j], recv_sem=recv_sems.at[j],
                device_id=(*chip, c), device_id_type=MESH).wait_recv()
            passed[j].start()
        for j, chip in enumerate(chips):
            pltpu.make_async_remote_copy(
                src_ref=rows(chip, 1 - c), dst_ref=rows(chip, 1 - c), send_sem=send_sems.at[3 + j], recv_sem=recv_sems.at[3 + j],
                device_id=(x, y, 1 - c), device_id_type=MESH).wait_recv()
        for cp in first + passed:
            cp.wait_send()
        mine.wait()

    return pl.pallas_call(
        body, name=name, out_shape=jax.ShapeDtypeStruct((N_CHIPS, R, C), shard.dtype),
        in_specs=[pl.BlockSpec(memory_space=pl.ANY)], out_specs=pl.BlockSpec(memory_space=pl.ANY),
        scratch_shapes=[pltpu.SemaphoreType.DMA((6,)), pltpu.SemaphoreType.DMA((6,)), pltpu.SemaphoreType.DMA],
        compiler_params=pltpu.CompilerParams(vmem_limit_bytes=VMEM_LIMIT))(shard)


def sibling_swap(name, v):
    def body(v_ref, out_ref, send_sem, recv_sem):
        x, y, c = _me()
        cp = pltpu.make_async_remote_copy(src_ref=v_ref, dst_ref=out_ref, send_sem=send_sem, recv_sem=recv_sem,
                                          device_id=(x, y, 1 - c), device_id_type=MESH)
        cp.start()
        cp.wait()

    return pl.pallas_call(
        body, name=name, out_shape=jax.ShapeDtypeStruct(v.shape, v.dtype),
        in_specs=[pl.BlockSpec(memory_space=pl.ANY)], out_specs=pl.BlockSpec(memory_space=pl.ANY),
        scratch_shapes=[pltpu.SemaphoreType.DMA, pltpu.SemaphoreType.DMA],
        compiler_params=pltpu.CompilerParams(vmem_limit_bytes=VMEM_LIMIT))(v)


def sibling_share(name, v):
    def body(v_ref, out_ref, send_sem, recv_sem, local_sem):
        x, y, c = _me()
        mine = pltpu.make_async_copy(v_ref, out_ref.at[c], local_sem)
        mine.start()
        cp = pltpu.make_async_remote_copy(src_ref=v_ref, dst_ref=out_ref.at[c], send_sem=send_sem, recv_sem=recv_sem,
                                          device_id=(x, y, 1 - c), device_id_type=MESH)
        cp.start()
        pltpu.make_async_remote_copy(src_ref=v_ref, dst_ref=out_ref.at[1 - c], send_sem=send_sem, recv_sem=recv_sem,
                                     device_id=(x, y, 1 - c), device_id_type=MESH).wait_recv()
        cp.wait_send()
        mine.wait()

    return pl.pallas_call(
        body, name=name, out_shape=jax.ShapeDtypeStruct((2, *v.shape), v.dtype),
        in_specs=[pl.BlockSpec(memory_space=pl.ANY)], out_specs=pl.BlockSpec(memory_space=pl.ANY),
        scratch_shapes=[pltpu.SemaphoreType.DMA, pltpu.SemaphoreType.DMA, pltpu.SemaphoreType.DMA],
        compiler_params=pltpu.CompilerParams(vmem_limit_bytes=VMEM_LIMIT))(v)


def chip_exchange(name, parts):
    def body(p_ref, out_ref, send_sems, recv_sems, local_sem):
        x, y, c = _me()
        k = 2 * x + y
        chips = [(1 - x, y), (x, 1 - y), (1 - x, 1 - y)]
        mine = pltpu.make_async_copy(p_ref.at[k], out_ref.at[k], local_sem)
        mine.start()
        sends = [pltpu.make_async_remote_copy(
            src_ref=p_ref.at[2 * chip[0] + chip[1]], dst_ref=out_ref.at[k], send_sem=send_sems.at[j], recv_sem=recv_sems.at[j],
            device_id=(*chip, c), device_id_type=MESH) for j, chip in enumerate(chips)]
        for cp in sends:
            cp.start()
        for j, chip in enumerate(chips):
            pltpu.make_async_remote_copy(
                src_ref=p_ref.at[k], dst_ref=out_ref.at[2 * chip[0] + chip[1]], send_sem=send_sems.at[j], recv_sem=recv_sems.at[j],
                device_id=(*chip, c), device_id_type=MESH).wait_recv()
        for cp in sends:
            cp.wait_send()
        mine.wait()

    return pl.pallas_call(
        body, name=name, out_shape=jax.ShapeDtypeStruct(parts.shape, parts.dtype),
        in_specs=[pl.BlockSpec(memory_space=pl.ANY)], out_specs=pl.BlockSpec(memory_space=pl.ANY),
        scratch_shapes=[pltpu.SemaphoreType.DMA((3,)), pltpu.SemaphoreType.DMA((3,)), pltpu.SemaphoreType.DMA],
        compiler_params=pltpu.CompilerParams(vmem_limit_bytes=VMEM_LIMIT))(parts)


def _row_tile(rows, cap, mult=8):
    best = None
    for t in range(mult, min(rows, cap) + 1, mult):
        if rows % t == 0:
            best = t
    assert best is not None, rows
    return best


def add_arrays(name, arrs, out_dtype=F32):
    shape = arrs[0].shape
    C = shape[-1]
    flat = [a.reshape(-1, C) for a in arrs]
    R = flat[0].shape[0]
    narrow = out_dtype == BF16 or any(a.dtype == BF16 for a in arrs)
    tr = _row_tile(R, 2048 if len(arrs) <= 2 else 1024, 16 if narrow else 8)
    n = len(flat)

    def body(*refs):
        acc = refs[0][...].astype(F32)
        for r in refs[1:n]:
            acc = acc + r[...].astype(F32)
        refs[n][...] = acc.astype(out_dtype)

    spec = pl.BlockSpec((tr, C), lambda i: (i, 0))
    out = pl.pallas_call(
        body, name=name, out_shape=jax.ShapeDtypeStruct((R, C), out_dtype), grid=(R // tr,),
        in_specs=[spec] * n, out_specs=spec, compiler_params=_cp(("parallel",)))(*flat)
    return out.reshape(shape)


def reduce_scatter_chips(slabs):
    _, R, C = slabs.shape
    half = R // 2
    c = lax.axis_index("c")
    k = 2 * lax.axis_index("x") + lax.axis_index("y")
    halves = slabs.reshape(N_CHIPS, 2, half, C)
    own = lax.dynamic_index_in_dim(halves, c, axis=1, keepdims=False)
    other = lax.dynamic_index_in_dim(halves, 1 - c, axis=1, keepdims=False)
    from_sibling = sibling_swap("rs_sibling_halves", other)
    chip_part = add_arrays("rs_add_sibling", [own, from_sibling], out_dtype=BF16)
    landed = chip_exchange("rs_chip_exchange", chip_part)
    mine = add_arrays("rs_add_chips", [landed[j] for j in range(N_CHIPS)])
    del k
    return sibling_share("rs_sibling_result", mine).reshape(R, C)


def ada_mod_shard(cond_all, w_ada_shard, b_ada_shard):
    def body(c_ref, w_ref, b_ref, o_ref):
        o_ref[...] = _dot(_silu(c_ref[...]).astype(BF16), w_ref[...].astype(BF16)) + b_ref[...]

    return pl.pallas_call(body, name="ada_mod_shard", out_shape=jax.ShapeDtypeStruct((cond_all.shape[0], w_ada_shard.shape[1]), F32),
                          compiler_params=_cp())(cond_all, w_ada_shard, b_ada_shard)


def ada_bwd_shard(cond_all, dmod_all_shard, dmod_all, w_ada_shard, row_is_cctx):
    def body(c_ref, ds_ref, da_ref, w_ref, sel_ref, gw_ref, gb_ref, part_ref):
        sc = _silu(c_ref[...]).astype(BF16)
        gw_ref[...] = _dot_tn(sc, ds_ref[...].astype(BF16))
        gb_ref[...] = _colsum(da_ref[...])
        dc_tot = jnp.broadcast_to(_colsum(ds_ref[...] * sel_ref[...]), (8, ds_ref.shape[1]))
        part_ref[...] = _dot_nt(dc_tot.astype(BF16), w_ref[...].astype(BF16))

    n = cond_all.shape[0]
    return pl.pallas_call(
        body, name="ada_bwd_shard",
        out_shape=[jax.ShapeDtypeStruct(w_ada_shard.shape, F32), jax.ShapeDtypeStruct((1, dmod_all.shape[1]), F32),
                   jax.ShapeDtypeStruct((8, D), F32)],
        compiler_params=_cp())(cond_all, dmod_all_shard, dmod_all, w_ada_shard, row_is_cctx)


def sum_devices(name, gathered):
    def body(g_ref, o_ref):
        acc = g_ref[0]
        for d in range(1, N_DEV):
            acc = acc + g_ref[d]
        o_ref[...] = acc

    return pl.pallas_call(body, name=name, out_shape=jax.ShapeDtypeStruct(gathered.shape[1:], F32), compiler_params=_cp())(gathered)


def cctx_finish(gathered, c_ctx_row):
    def body(g_ref, c_ref, o_ref):
        acc = g_ref[0, 0:1, :]
        for k in range(1, N_CHIPS):
            acc = acc + g_ref[2 * k, 0:1, :]
        o_ref[...] = acc * _dsilu(c_ref[...])

    return pl.pallas_call(body, name="cctx_finish", out_shape=jax.ShapeDtypeStruct((1, D), F32), compiler_params=_cp())(gathered, c_ctx_row)


def _pack(parts, rows):
    flat = []
    for p in parts:
        p = p.reshape(-1)
        pad = (-p.shape[0]) % LANES
        flat.append(jnp.pad(p, (0, pad)) if pad else p)
    v = jnp.concatenate(flat)
    return jnp.pad(v, (0, rows * LANES - v.shape[0])).reshape(rows, LANES)


def _unpack(v, sizes):
    flat = v.reshape(-1)
    out, off = [], 0
    for n in sizes:
        out.append(flat[off:off + n])
        off += n + (-n) % LANES
    return out


W_SHARD_ROWS = 3456
SEG_ROWS = (0, 2320, 2576, 3088, 3344, 3408)


def kernel(x, c, ctx, c_ctx, w_ada, b_ada, norm_pre, norm_post, w_in, b_merge, pool_w, pool_scale, conv_w, conv_b, dt_bias, a_log, d_skip, ssd_norm, w_proj_pool, w_proj_ssd, w_out, loss_target, m_c_ctx, m_w_ada, m_b_ada, m_norm_pre, m_norm_post, m_w_in, m_b_merge, m_pool_w, m_pool_scale, m_conv_w, m_conv_b, m_dt_bias, m_a_log, m_d_skip, m_ssd_norm, m_w_proj_pool, m_w_proj_ssd, m_w_out, v_c_ctx, v_w_ada, v_b_ada, v_norm_pre, v_norm_post, v_w_in, v_b_merge, v_pool_w, v_pool_scale, v_conv_w, v_conv_b, v_dt_bias, v_a_log, v_d_skip, v_ssd_norm, v_w_proj_pool, v_w_proj_ssd, v_w_out):
    Bn, L, _ = x.shape
    Lc = ctx.shape[1]
    T, Tc = Bn * L, Bn * Lc
    assert Bn == 2
    ix, iy, ic = lax.axis_index("x"), lax.axis_index("y"), lax.axis_index("c")
    me = 4 * ix + 2 * iy + ic
    chip = 2 * ix + iy
    ada_cols = w_ada.shape[2]
    cw_cols = conv_w.shape[2]

    cond_own = jnp.pad(c, ((0, 8 - Bn), (0, 0))) + jnp.pad(c_ctx[None, :], ((Bn, 7 - Bn), (0, 0)))
    convw_own = jnp.pad(conv_w[0], ((0, 4), (0, D - cw_cols)))
    g1 = all_gather_small("gather_cond", jnp.concatenate([cond_own, convw_own], axis=0))
    cond_all = g1[:, 0:8].reshape(8 * N_DEV, D)
    conv_w_full = jnp.concatenate([g1[2 * k, 8:12, 0:cw_cols] for k in range(N_CHIPS)], axis=1)
    b_ada_shard = lax.dynamic_slice(b_ada, (0, chip * ada_cols), (1, ada_cols))
    g2 = all_gather_small("gather_mod", ada_mod_shard(cond_all, w_ada[0], b_ada_shard))
    mod_full = jnp.concatenate([g2[2 * k] for k in range(N_CHIPS)], axis=1)
    own = lax.dynamic_slice(mod_full, (8 * me, 0), (8, 3 * D))
    shift, scale, gate = (own[0:Bn, i * D:(i + 1) * D][:, None, :] for i in range(3))
    shift_c, scale_c = (jnp.broadcast_to(own[Bn:Bn + 1, i * D:(i + 1) * D][None], (Bn, 1, D)) for i in range(2))

    shard = jnp.concatenate([w_in[0].T, w_proj_pool[0], w_proj_ssd[0], w_out[0], pool_w[0].reshape(64, D),
                             jnp.zeros((W_SHARD_ROWS - SEG_ROWS[-1], D), F32)], axis=0).astype(BF16)
    gw = all_gather_chips("gather_weights", shard)
    w_inT = gw[:, SEG_ROWS[0]:SEG_ROWS[1]].reshape(IN_COLS, D)
    w_pp = gw[:, SEG_ROWS[1]:SEG_ROWS[2]].reshape(D, D)
    w_ps = gw[:, SEG_ROWS[2]:SEG_ROWS[3]].reshape(D_INNER, D)
    w_o = gw[:, SEG_ROWS[3]:SEG_ROWS[4]].reshape(D, D)
    pool_full = gw[:, SEG_ROWS[4]:SEG_ROWS[5]].reshape(N_CHIPS, 4, 64, POOL_GROUP).transpose(1, 0, 2, 3).reshape(D, POOL_GROUP)
    w_dt = jnp.pad(w_inT[9216:IN_COLS], ((0, LANES - 64), (0, 0)))
    seg_lo = (0, 256, 512, 768, 1024, 2048, 4096, 6144, 8192, 8704)
    seg_hi = (256, 512, 768, 1024, 2048, 4096, 6144, 8192, 8704, 9216)
    w_seg = [w_inT[lo:hi] for lo, hi in zip(seg_lo, seg_hi)] + [w_dt]

    hx = prenorm_fwd("prenorm_x", x, scale, shift, norm_pre)
    hc = prenorm_fwd("prenorm_ctx", ctx, scale_c, shift_c, norm_pre)
    hx2, hc2 = hx.reshape(T, D), hc.reshape(Tc, D)
    v = mm_nt("proj_v", hx2, w_inT[0:1024], F32).reshape(Bn, L, D)
    zp = mm_nt("proj_zpool", hx2, w_inT[1024:2048], F32).reshape(Bn, L, D)
    zs = mm_nt("proj_zssd", hx2, w_inT[2048:4096], F32).reshape(Bn, L, D_INNER)
    gp = mm_nt("proj_gate", hx2, w_inT[4096:6144], F32).reshape(Bn, L, 2 * D)
    xbc_raw = mm_nt("proj_xbc", hx2, w_inT[6144:9216], F32).reshape(Bn, L, CONV_DIM)
    dt_raw = mm_nt("proj_dt", hx2, w_dt, F32)
    xbc_raw_c = mm_nt("proj_xbc_ctx", hc2, w_inT[6144:9216], F32).reshape(Bn, Lc, CONV_DIM)
    dt_raw_c = mm_nt("proj_dt_ctx", hc2, w_dt, F32)
    dtT = dt_raw[:, :64].reshape(Bn, L, 64).transpose(0, 2, 1)
    dtT_c = dt_raw_c[:, :64].reshape(Bn, Lc, 64).transpose(0, 2, 1)
    bias3 = dt_bias.reshape(2 * N_BC, HPG, 1)
    alog3 = a_log.reshape(2 * N_BC, HPG, 1)

    xbc = conv_fwd("conv_x", xbc_raw, conv_w_full, conv_b)
    xbc_c = conv_fwd("conv_ctx", xbc_raw_c, conv_w_full, conv_b)
    zero_state = jnp.zeros((Bn, N_BC, D_STATE, GW), F32)
    tables = ssd_tables()
    ys, hs_x, hs_c, hf_x, hf_c = [], [], [], [], []
    for d in range(2):
        hsc, hfc = ssd_fwd3(f"ssd_fwd_ctx{d}", dtT_c, bias3, alog3, xbc_c, zero_state, d, False)
        y, hsx, hfx = ssd_fwd3(f"ssd_fwd_x{d}", dtT, bias3, alog3, xbc, hfc, d, True)
        ys.append(y)
        hs_x.append(hsx)
        hs_c.append(hsc)
        hf_x.append(hfx)
        hf_c.append(hfc)

    dgs = [pool_diff(f"pool_diff{g}", v, g * POOL_GROUP, g, False) for g in range(4)]
    y_pool = pool_mix_fwd("pool_mix", dgs, zp, pool_full, pool_scale)
    dskip_lanes = jnp.repeat(d_skip[0], HEAD_DIM)[None, :]
    y_ssd = gated_norm_fwd("gated_norm", ys[0], ys[1], (xbc, D_INNER), zs, dskip_lanes, ssd_norm)
    merged, p1, p2, dout, g_res, dgate, g_norm_post, loss_part = merge_fwd(
        "merge_fwd", y_pool, y_ssd, gp, x, loss_target, gate, b_merge, norm_post, w_pp, w_ps, w_o)

    dp1, dp2, dgp, dyp, dys, g_b_merge = merge_bwd("merge_bwd", dout, gp, p1, p2, b_merge, w_pp, w_ps, w_o)
    gw_o = mm_tn("gw_out", merged.reshape(T, D), dout.reshape(T, D))
    gw_pp = mm_tn("gw_proj_pool", y_pool.reshape(T, D), dp1.reshape(T, D))
    gw_ps = mm_tn("gw_proj_ssd", y_ssd.reshape(T, D_INNER), dp2.reshape(T, D))

    *dds, dzp, g_pool, g_pool_scale = pool_mix_bwd("pool_mix_bwd", dgs, zp, dyp, pool_full, pool_scale)
    dvs = [pool_diff(f"pool_diff_t{g}", dds[g], 0, g, True) for g in range(4)]

    head_sel = (jnp.arange(D_INNER)[:, None] // HEAD_DIM == jnp.arange(LANES)[None, :]).astype(BF16)
    dy, dzs, g_ssd_norm, g_dskip = gated_norm_bwd(
        "gated_norm_bwd", ys[0], ys[1], (xbc, D_INNER), zs, dys, dskip_lanes, ssd_norm, head_sel)

    dxs, dbm, dcm, ddt, dxs_c, dbm_c, ddt_c = [], [], [], [], [], [], []
    g_bias = jnp.zeros((2, N_BC, HPG, 1), F32)
    g_alog = jnp.zeros((2, N_BC, HPG, 1), F32)
    for d in range(2):
        a, b_, c_, t_, gb, ga, dh0 = ssd_bwd3(f"ssd_bwd_x{d}", dtT, bias3, alog3, xbc, hs_x[d], dy, zero_state, d)
        dxs.append(a), dbm.append(b_), dcm.append(c_), ddt.append(t_)
        ac, bc, _, tc, gbc, gac, _ = ssd_bwd3(f"ssd_bwd_ctx{d}", dtT_c, bias3, alog3, xbc_c, hs_c[d], None, dh0, d)
        dxs_c.append(ac), dbm_c.append(bc), ddt_c.append(tc)
        g_bias = g_bias.at[d].set(jnp.sum(gb, axis=0) + jnp.sum(gbc, axis=0))
        g_alog = g_alog.at[d].set(jnp.sum(ga, axis=0) + jnp.sum(gac, axis=0))

    dxr_xs, gcw_xs, gcb_xs = conv_bwd("conv_bwd_xs", xbc_raw, dxs, conv_w_full, conv_b, 0, D_INNER, scaled=(dy, dskip_lanes))
    dxr_b, gcw_b, gcb_b = conv_bwd("conv_bwd_b", xbc_raw, dbm, conv_w_full, conv_b, D_INNER, N_BC * D_STATE)
    dxr_c, gcw_c, gcb_c = conv_bwd("conv_bwd_c", xbc_raw, dcm, conv_w_full, conv_b, D_INNER + N_BC * D_STATE, N_BC * D_STATE)
    dxr_xs_c, gcw_xs_c, gcb_xs_c = conv_bwd("conv_bwd_xs_ctx", xbc_raw_c, dxs_c, conv_w_full, conv_b, 0, D_INNER)
    dxr_b_c, gcw_b_c, gcb_b_c = conv_bwd("conv_bwd_b_ctx", xbc_raw_c, dbm_c, conv_w_full, conv_b, D_INNER, N_BC * D_STATE)
    g_conv_w = jnp.concatenate([gcw_xs + gcw_xs_c, gcw_b + gcw_b_c, gcw_c], axis=1)
    g_conv_b = jnp.concatenate([gcb_xs + gcb_xs_c, gcb_b + gcb_b_c, gcb_c], axis=1)

    def dt_cols(parts, n_tok):
        t = jnp.concatenate(parts, axis=1).transpose(0, 2, 1).reshape(n_tok, 2 * N_HEADS)
        return jnp.pad(t, ((0, 0), (0, LANES - 2 * N_HEADS))).astype(BF16)

    ddt2, ddt2_c = dt_cols(ddt, T), dt_cols(ddt_c, Tc)
    segs = ([dv.reshape(T, POOL_GROUP) for dv in dvs]
            + [dzp.reshape(T, D), dzs.reshape(T, D_INNER), dgp.reshape(T, 2 * D), dxr_xs.reshape(T, D_INNER),
               dxr_b.reshape(T, N_BC * D_STATE), dxr_c.reshape(T, N_BC * D_STATE), ddt2])
    d_hx = mm_nn_multi("d_hx", list(zip(segs, w_seg)), F32, tm=1024, tk=256).reshape(Bn, L, D)
    segs_c = {7: dxr_xs_c.reshape(Tc, D_INNER), 8: dxr_b_c.reshape(Tc, N_BC * D_STATE), 10: ddt2_c}
    d_hc = mm_nn_multi("d_hc", [(segs_c[i], w_seg[i]) for i in (7, 8, 10)], F32).reshape(Bn, Lc, D)
    gw_rows = []
    for i, seg in enumerate(segs):
        init = mm_tn(f"gw_in_ctx{i}", segs_c[i], hc2) if i in segs_c else None
        gw_rows.append(mm_tn(f"gw_in{i}", seg, hx2, init=init))
    gw_rows[-1] = gw_rows[-1][0:2 * N_HEADS]
    gw_inT = jnp.concatenate(gw_rows, axis=0)

    grad_x, dscale, dshift, g_npre_x = prenorm_bwd("prenorm_bwd_x", x, d_hx, scale, norm_pre, g_res=g_res)
    _, dscale_c, dshift_c, g_npre_c = prenorm_bwd("prenorm_bwd_ctx", ctx, d_hc, scale_c, norm_pre)

    dmod_x = jnp.concatenate([dshift[:, 0], dscale[:, 0], dgate[:, 0]], axis=1)
    dmod_c = jnp.concatenate([jnp.sum(dshift_c[:, 0], axis=0, keepdims=True), jnp.sum(dscale_c[:, 0], axis=0, keepdims=True),
                              jnp.zeros((1, D), F32)], axis=1)
    dmod_own = jnp.pad(dmod_x, ((0, 8 - Bn), (0, 0))) + jnp.pad(dmod_c, ((Bn, 7 - Bn), (0, 0)))
    dmod_all = all_gather_small("gather_dmod", dmod_own).reshape(8 * N_DEV, 3 * D)
    row_is_cctx = (jnp.arange(8 * N_DEV) % 8 == Bn).astype(F32)[:, None]
    g_w_ada, g_b_ada, cpart = ada_bwd_shard(
        cond_all, lax.dynamic_slice(dmod_all, (0, chip * ada_cols), (8 * N_DEV, ada_cols)), dmod_all, w_ada[0], row_is_cctx)
    g_c_ctx = cctx_finish(all_gather_small("gather_cctx", cpart), c_ctx[None, :])

    small_sizes = (D, D, 2 * D, D, CONV_DIM, 2 * N_HEADS, 2 * N_HEADS, N_HEADS, D_INNER, 4 * CONV_DIM, 1)
    pk = _pack([g_npre_x + g_npre_c, g_norm_post, g_b_merge, g_pool_scale, g_conv_b, g_bias, g_alog, g_dskip[0, 0:N_HEADS],
                g_ssd_norm, g_conv_w, loss_part[0, 0:1]], 184)
    small = sum_devices("sum_small", all_gather_small("gather_small", pk))
    (g_norm_pre, g_norm_post_t, g_b_merge_t, g_pool_scale_t, g_conv_b_t, g_dt_bias, g_a_log, g_d_skip, g_ssd_norm_t,
     g_conv_w_t, loss) = _unpack(small, small_sizes)
    g_conv_w_shard = lax.dynamic_slice(g_conv_w_t.reshape(4, CONV_DIM), (0, chip * cw_cols), (4, cw_cols))

    pool_slab = g_pool.reshape(4, N_CHIPS, 64, POOL_GROUP).transpose(1, 0, 2, 3).reshape(N_CHIPS, 64, D)
    slabs = jnp.concatenate([gw_inT.reshape(N_CHIPS, 2320, D), gw_pp.reshape(N_CHIPS, 256, D), gw_ps.reshape(N_CHIPS, 512, D),
                             gw_o.reshape(N_CHIPS, 256, D), pool_slab, jnp.zeros((N_CHIPS, W_SHARD_ROWS - SEG_ROWS[-1], D), F32)], axis=1)
    gsh = reduce_scatter_chips(slabs)
    g_w_in = gsh[SEG_ROWS[0]:SEG_ROWS[1]].T
    g_w_pp, g_w_ps, g_w_o = (gsh[SEG_ROWS[i]:SEG_ROWS[i + 1]] for i in (1, 2, 3))
    g_pool_w = gsh[SEG_ROWS[4]:SEG_ROWS[5]].reshape(256, POOL_GROUP)

    grads = {
        "c_ctx": g_c_ctx.reshape(c_ctx.shape), "w_ada": g_w_ada[None], "b_ada": g_b_ada, "norm_pre": g_norm_pre[None],
        "norm_post": g_norm_post_t[None], "w_in": g_w_in[None], "b_merge": g_b_merge_t[None],
        "pool_w": g_pool_w.reshape(pool_w.shape), "pool_scale": g_pool_scale_t[None], "conv_w": g_conv_w_shard[None],
        "conv_b": g_conv_b_t[None], "dt_bias": g_dt_bias.reshape(dt_bias.shape), "a_log": g_a_log.reshape(a_log.shape),
        "d_skip": g_d_skip[None], "ssd_norm": g_ssd_norm_t[None], "w_proj_pool": g_w_pp[None], "w_proj_ssd": g_w_ps[None],
        "w_out": g_w_o[None]}
    weights = dict(c_ctx=c_ctx, w_ada=w_ada, b_ada=b_ada, norm_pre=norm_pre, norm_post=norm_post, w_in=w_in, b_merge=b_merge,
                   pool_w=pool_w, pool_scale=pool_scale, conv_w=conv_w, conv_b=conv_b, dt_bias=dt_bias, a_log=a_log,
                   d_skip=d_skip, ssd_norm=ssd_norm, w_proj_pool=w_proj_pool, w_proj_ssd=w_proj_ssd, w_out=w_out)
    m_in = dict(c_ctx=m_c_ctx, w_ada=m_w_ada, b_ada=m_b_ada, norm_pre=m_norm_pre, norm_post=m_norm_post, w_in=m_w_in,
                b_merge=m_b_merge, pool_w=m_pool_w, pool_scale=m_pool_scale, conv_w=m_conv_w, conv_b=m_conv_b,
                dt_bias=m_dt_bias, a_log=m_a_log, d_skip=m_d_skip, ssd_norm=m_ssd_norm, w_proj_pool=m_w_proj_pool,
                w_proj_ssd=m_w_proj_ssd, w_out=m_w_out)
    v_in = dict(c_ctx=v_c_ctx, w_ada=v_w_ada, b_ada=v_b_ada, norm_pre=v_norm_pre, norm_post=v_norm_post, w_in=v_w_in,
                b_merge=v_b_merge, pool_w=v_pool_w, pool_scale=v_pool_scale, conv_w=v_conv_w, conv_b=v_conv_b,
                dt_bias=v_dt_bias, a_log=v_a_log, d_skip=v_d_skip, ssd_norm=v_ssd_norm, w_proj_pool=v_w_proj_pool,
                w_proj_ssd=v_w_proj_ssd, w_out=v_w_out)
    names = list(weights)
    big = ("w_ada", "w_in", "pool_w", "w_proj_pool", "w_proj_ssd", "w_out")
    small_names = [n for n in names if n not in big]
    delta, new_m, new_v = {}, {}, {}
    for n in big:
        shape2 = (-1, weights[n].shape[-1])
        d_, m_, v_ = adamw(f"adamw_{n}", weights[n].reshape(shape2), grads[n].reshape(shape2), m_in[n].reshape(shape2),
                           v_in[n].reshape(shape2), tr=128)
        delta[n], new_m[n], new_v[n] = (t.reshape(weights[n].shape) for t in (d_, m_, v_))
    sizes = [weights[n].size for n in small_names]
    packed = [_pack([src[n] for n in small_names], 144) for src in (weights, grads, m_in, v_in)]
    outs = adamw("adamw_small", *packed, tr=144)
    for res, store in zip(outs, (delta, new_m, new_v)):
        for n, piece in zip(small_names, _unpack(res, sizes)):
            store[n] = piece.reshape(weights[n].shape)

    return (loss.reshape(()), grad_x, *[grads[n] for n in names], *[delta[n] for n in names],
            *[new_m[n] for n in names], *[new_v[n] for n in names])
```

```python
import jax
import jax.numpy as jnp
from jax import lax
from jax.experimental import pallas as pl
from jax.experimental.pallas import tpu as pltpu

F32 = jnp.float32
BF16 = jnp.bfloat16
MESH = pl.DeviceIdType.MESH

D = 1024
GRID_W = 64
NORM_EPS = 1e-6
POOL_WINDOWS = (2, 4, 8, 16)
POOL_GROUP = 256
D_INNER = 2048
HEAD_DIM = 64
N_HEADS = 32
D_STATE = 128
N_BC = 4
HPG = N_HEADS // N_BC
GW = HPG * HEAD_DIM
CONV_DIM = 3072
CHUNK = 128
OFF_XBC = 6144
IN_COLS = 9280
N_CHIPS = 4
N_DEV = 8

ADAM_LR = 0.001
ADAM_B1 = 0.9
ADAM_B2 = 0.999
ADAM_EPS = 1e-08
ADAM_WD = 0.01
ADAM_STEP = 10

V7X_VMEM_BYTES = 64 * 1024 * 1024
VMEM_LIMIT = V7X_VMEM_BYTES * 3 // 4
LANES = 128


def _cp(sem=None):
    return pltpu.CompilerParams(dimension_semantics=sem, vmem_limit_bytes=VMEM_LIMIT)


def _dot(a, b):
    return jnp.dot(a, b, preferred_element_type=F32)


def _dot_nt(a, b):
    return lax.dot_general(a, b, (((1,), (1,)), ((), ())), preferred_element_type=F32)


def _dot_tn(a, b):
    return lax.dot_general(a, b, (((0,), (0,)), ((), ())), preferred_element_type=F32)


def _split3(x):
    hi = x.astype(BF16)
    r1 = x - hi.astype(F32)
    mid = r1.astype(BF16)
    lo = (r1 - mid.astype(F32)).astype(BF16)
    return hi, mid, lo


def _sigmoid(x):
    return jax.nn.sigmoid(x)


def _silu(x):
    return x * _sigmoid(x)


def _dsilu(x):
    s = _sigmoid(x)
    return s * (1.0 + x * (1.0 - s))


def _softplus(x):
    return jnp.maximum(x, 0.0) + jnp.log(1.0 + jnp.exp(-jnp.abs(x)))


def mm_nt(name, a, b, out_dtype, tm=1024, tn=512):
    M, K = a.shape
    N = b.shape[0]
    tm, tn = min(tm, M), min(tn, N)
    assert M % tm == 0 and N % tn == 0, (M, N, tm, tn)

    def body(a_ref, b_ref, o_ref):
        o_ref[...] = _dot_nt(a_ref[...], b_ref[...]).astype(o_ref.dtype)

    return pl.pallas_call(
        body, name=name, out_shape=jax.ShapeDtypeStruct((M, N), out_dtype), grid=(M // tm, N // tn),
        in_specs=[pl.BlockSpec((tm, K), lambda i, j: (i, 0)), pl.BlockSpec((tn, K), lambda i, j: (j, 0))],
        out_specs=pl.BlockSpec((tm, tn), lambda i, j: (i, j)),
        compiler_params=_cp(("parallel", "arbitrary")))(a, b)


def mm_tn(name, a, b, init=None, tm=1024, tn=1024, tk=512):
    T, M = a.shape
    N = b.shape[1]
    tm, tn, tk = min(tm, M), min(tn, N), min(tk, T)
    assert M % tm == 0 and N % tn == 0 and T % tk == 0, (M, N, T)
    has_init = init is not None

    def body(*refs):
        if has_init:
            a_ref, b_ref, i_ref, o_ref = refs
        else:
            a_ref, b_ref, o_ref = refs
        k = pl.program_id(2)

        @pl.when(k == 0)
        def _():
            o_ref[...] = i_ref[...] if has_init else jnp.zeros(o_ref.shape, F32)

        o_ref[...] += _dot_tn(a_ref[...], b_ref[...])

    in_specs = [pl.BlockSpec((tk, tm), lambda i, j, k: (k, i)), pl.BlockSpec((tk, tn), lambda i, j, k: (k, j))]
    args = [a, b]
    if has_init:
        in_specs.append(pl.BlockSpec((tm, tn), lambda i, j, k: (i, j)))
        args.append(init)
    return pl.pallas_call(
        body, name=name, out_shape=jax.ShapeDtypeStruct((M, N), F32), grid=(M // tm, N // tn, T // tk),
        in_specs=in_specs, out_specs=pl.BlockSpec((tm, tn), lambda i, j, k: (i, j)),
        compiler_params=_cp(("parallel", "parallel", "arbitrary")))(*args)


def mm_nn_multi(name, pairs, out_dtype, tm=512, tk=512):
    M = pairs[0][0].shape[0]
    N = pairs[0][1].shape[1]
    tm = min(tm, M)
    assert M % tm == 0
    plan = []
    step = 0
    for a, b in pairs:
        K = a.shape[1]
        t = min(tk, K)
        assert K % t == 0 and b.shape == (K, N)
        plan.append((t, step, K // t))
        step += K // t
    nsteps = step
    npairs = len(pairs)

    def body(*refs):
        o_ref, acc = refs[2 * npairs], refs[2 * npairs + 1]
        k = pl.program_id(1)

        @pl.when(k == 0)
        def _():
            acc[...] = jnp.zeros(acc.shape, F32)

        for p, (_, first, n) in enumerate(plan):
            @pl.when((k >= first) & (k < first + n))
            def _(p=p):
                acc[...] += _dot(refs[2 * p][...], refs[2 * p + 1][...])

        @pl.when(k == nsteps - 1)
        def _():
            o_ref[...] = acc[...].astype(o_ref.dtype)

    in_specs, args = [], []
    for (a, b), (t, first, n) in zip(pairs, plan):
        in_specs.append(pl.BlockSpec((tm, t), lambda i, k, first=first, n=n: (i, jnp.clip(k - first, 0, n - 1))))
        in_specs.append(pl.BlockSpec((t, N), lambda i, k, first=first, n=n: (jnp.clip(k - first, 0, n - 1), 0)))
        args += [a, b]
    return pl.pallas_call(
        body, name=name, out_shape=jax.ShapeDtypeStruct((M, N), out_dtype), grid=(M // tm, nsteps),
        in_specs=in_specs, out_specs=pl.BlockSpec((tm, N), lambda i, k: (i, 0)),
        scratch_shapes=[pltpu.VMEM((tm, N), F32)],
        compiler_params=_cp(("parallel", "arbitrary")))(*args)


def tok_call(name, body, tiled, perb, glob, out_tiled, out_perb, out_glob, tm=256):
    widths = [t[1] if isinstance(t, tuple) else t.shape[2] for t in tiled]
    tiled = [t[0] if isinstance(t, tuple) else t for t in tiled]
    Bn, L = tiled[0].shape[:2]
    tm = min(tm, L)
    assert L % tm == 0
    n_t, n_p, n_g = len(tiled), len(perb), len(glob)
    o_t, o_p, o_g = len(out_tiled), len(out_perb), len(out_glob)
    n_in = n_t + n_p + n_g

    def kern(*refs):
        ins, outs = refs[:n_in], refs[n_in:]
        b, j = pl.program_id(0), pl.program_id(1)
        vals = [r[0] for r in ins[:n_t + n_p]] + [r[...] for r in ins[n_t + n_p:]]
        res = body(*vals)
        if not isinstance(res, (tuple, list)):
            res = (res,)
        assert len(res) == o_t + o_p + o_g, (name, len(res))
        for r, v in zip(outs[:o_t], res[:o_t]):
            r[0] = v.astype(r.dtype)

        def accum(r, v, first, lead):
            @pl.when(first)
            def _():
                r[...] = jnp.zeros(r.shape, F32)
            if lead:
                r[0] += v
            else:
                r[...] += v

        for r, v in zip(outs[o_t:o_t + o_p], res[o_t:o_t + o_p]):
            accum(r, v, j == 0, True)
        for r, v in zip(outs[o_t + o_p:], res[o_t + o_p:]):
            accum(r, v, (j == 0) & (b == 0), False)

    in_specs = ([pl.BlockSpec((1, tm, w), lambda b, j: (b, j, 0)) for w in widths]
                + [pl.BlockSpec((1, 1, a.shape[2]), lambda b, j: (b, 0, 0)) for a in perb]
                + [pl.BlockSpec(a.shape, lambda b, j: (0, 0), pipeline_mode=pl.Buffered(1)) for a in glob])
    out_shape = ([jax.ShapeDtypeStruct((Bn, L, w), dt) for w, dt in out_tiled]
                 + [jax.ShapeDtypeStruct((Bn, 1, w), F32) for w in out_perb]
                 + [jax.ShapeDtypeStruct(s, F32) for s in out_glob])
    out_specs = ([pl.BlockSpec((1, tm, w), lambda b, j: (b, j, 0)) for w, _ in out_tiled]
                 + [pl.BlockSpec((1, 1, w), lambda b, j: (b, 0, 0)) for w in out_perb]
                 + [pl.BlockSpec(s, lambda b, j: (0, 0)) for s in out_glob])
    return pl.pallas_call(
        kern, name=name, out_shape=out_shape, grid=(Bn, L // tm), in_specs=in_specs, out_specs=out_specs,
        compiler_params=_cp(("arbitrary", "arbitrary")))(*tiled, *perb, *glob)


def slab_call(name, body, slabs, colparams, out_slabs, out_colred, wc=LANES):
    Bn, L = slabs[0][0].shape[:2]
    w_out = out_slabs[0][0]
    assert w_out % wc == 0 and all(off % wc == 0 for _, off in slabs + colparams)
    n_col = w_out // wc
    n_s, n_c = len(slabs), len(colparams)
    o_s = len(out_slabs)

    def kern(*refs):
        ins, outs = refs[:n_s + n_c], refs[n_s + n_c:]
        b = pl.program_id(1)
        vals = [r[0] for r in ins[:n_s]] + [r[...] for r in ins[n_s:]]
        res = body(*vals)
        if not isinstance(res, (tuple, list)):
            res = (res,)
        assert len(res) == o_s + len(out_colred), name
        for r, v in zip(outs[:o_s], res[:o_s]):
            r[0] = v.astype(r.dtype)

        def accum(r, v):
            @pl.when(b == 0)
            def _():
                r[...] = jnp.zeros(r.shape, F32)
            r[...] += v

        for r, v in zip(outs[o_s:], res[o_s:]):
            accum(r, v)

    in_specs = ([pl.BlockSpec((1, L, wc), lambda j, b, o=off // wc: (b, 0, o + j)) for _, off in slabs]
                + [pl.BlockSpec((a.shape[0], wc), lambda j, b, o=off // wc: (0, o + j)) for a, off in colparams])
    out_shape = ([jax.ShapeDtypeStruct((Bn, L, w), dt) for w, dt in out_slabs]
                 + [jax.ShapeDtypeStruct((r, w_out), F32) for r in out_colred])
    out_specs = ([pl.BlockSpec((1, L, wc), lambda j, b: (b, 0, j)) for _ in out_slabs]
                 + [pl.BlockSpec((r, wc), lambda j, b: (0, j)) for r in out_colred])
    return pl.pallas_call(
        kern, name=name, out_shape=out_shape, grid=(n_col, Bn), in_specs=in_specs, out_specs=out_specs,
        compiler_params=_cp(("arbitrary", "arbitrary")))(*[a for a, _ in slabs], *[a for a, _ in colparams])


def _rms_r(x):
    return lax.rsqrt(jnp.mean(x * x, axis=-1, keepdims=True) + NORM_EPS)


def _rms_bwd(dxh, x, r):
    return r * (dxh - x * (r * r) * jnp.mean(dxh * x, axis=-1, keepdims=True))


def _colsum(v):
    return jnp.sum(v, axis=0, keepdims=True)


def _stack_rows(rows):
    n, w = len(rows), rows[0].shape[1]
    sub = lax.broadcasted_iota(jnp.int32, (n, w), 0)
    acc = jnp.zeros((n, w), F32)
    for r, row in enumerate(rows):
        acc = acc + jnp.where(sub == r, jnp.broadcast_to(row, (n, w)), 0.0)
    return acc


def prenorm_fwd(name, x, scale, shift, w_pre):
    def body(x, scale, shift, w):
        n = x * _rms_r(x) * w
        return n * (1.0 + scale) + shift

    return tok_call(name, body, [x], [scale, shift], [w_pre], [(D, BF16)], [], [])[0]


def prenorm_bwd(name, x, dhx, scale, w_pre, g_res=None):
    has_res = g_res is not None

    def body(*v):
        if has_res:
            x, dhx, g, scale, w = v
        else:
            x, dhx, scale, w = v
        r = _rms_r(x)
        xr = x * r
        n = xr * w
        dn = dhx * (1.0 + scale)
        dx = _rms_bwd(dn * w, x, r)
        if has_res:
            dx = dx + g
        return dx, _colsum(dhx * n), _colsum(dhx), _colsum(dn * xr)

    tiled = [x, dhx] + ([g_res] if has_res else [])
    return tok_call(name, body, tiled, [scale], [w_pre], [(D, F32)], [D, D], [(1, D)])


def _shift_rows(x, o, tok, L):
    if o == 0:
        return x
    rolled = pltpu.roll(x, (-o) % L, 0)
    return jnp.where((tok + o >= 0) & (tok + o < L), rolled, 0.0)


def conv_fwd(name, xbc_raw, conv_w, conv_b):
    L = xbc_raw.shape[1]

    def body(x, w, b):
        tok = lax.broadcasted_iota(jnp.int32, x.shape, 0)
        pre = b
        for k in range(4):
            pre = pre + _shift_rows(x, k - 2, tok, L) * w[k:k + 1]
        return _silu(pre)

    return slab_call(name, body, [(xbc_raw, 0)], [(conv_w, 0), (conv_b, 0)], [(CONV_DIM, F32)], [])[0]


def conv_bwd(name, xbc_raw, dparts, conv_w, conv_b, col0, width, scaled=None):
    L = xbc_raw.shape[1]
    n_d = len(dparts) + (1 if scaled is not None else 0)

    def body(*v):
        x, ds, w, b = v[0], v[1:1 + n_d], v[1 + n_d], v[2 + n_d]
        tok = lax.broadcasted_iota(jnp.int32, x.shape, 0)
        taps = [_shift_rows(x, k - 2, tok, L) for k in range(4)]
        pre = b
        for k in range(4):
            pre = pre + taps[k] * w[k:k + 1]
        dy = ds[0] * v[3 + n_d] if scaled is not None else ds[0]
        for extra in ds[1:]:
            dy = dy + extra
        dpre = dy * _dsilu(pre)
        dx = jnp.zeros_like(x)
        for k in range(4):
            dx = dx + _shift_rows(dpre, 2 - k, tok, L) * w[k:k + 1]
        dw = _stack_rows([_colsum(dpre * taps[k]) for k in range(4)])
        return dx, dw, _colsum(dpre)

    slabs = [(xbc_raw, col0)] + ([(scaled[0], 0)] if scaled is not None else []) + [(d, 0) for d in dparts]
    colparams = [(conv_w, col0), (conv_b, col0)] + ([(scaled[1], 0)] if scaled is not None else [])
    return slab_call(name, body, slabs, colparams, [(width, BF16)], [4, 1])


def _box_mean(x, k, step, pos, n, L, transpose):
    lo, hi = k // 2, k - 1 - k // 2
    cnt = (jnp.minimum(pos + hi + 1, n) - jnp.maximum(pos - lo, 0)).astype(F32)
    if transpose:
        x = x / cnt
        lo, hi = hi, lo
    acc = x
    for o in range(-lo, hi + 1):
        if o == 0:
            continue
        rolled = pltpu.roll(x, (-o * step) % L, 0)
        acc = acc + jnp.where((pos + o >= 0) & (pos + o < n), rolled, 0.0)
    return acc if transpose else acc / cnt


def pool_diff(name, v, col0, gi, transpose):
    L = v.shape[1]
    rows = L // GRID_W
    k = POOL_WINDOWS[gi]

    def body(x):
        tok = lax.broadcasted_iota(jnp.int32, x.shape, 0)
        col = tok & (GRID_W - 1)
        row = tok >> 6
        if not transpose:
            m = _box_mean(x, k, GRID_W, row, rows, L, False)
            m = _box_mean(m, k, 1, col, GRID_W, L, False)
        else:
            m = _box_mean(x, k, 1, col, GRID_W, L, True)
            m = _box_mean(m, k, GRID_W, row, rows, L, True)
        return m - x

    return slab_call(name, body, [(v, col0)], [], [(POOL_GROUP, BF16)], [])[0]


def pool_mix_fwd(name, dgs, z_pool, pool_w, pool_scale):
    def body(d0, d1, d2, d3, z, w, scale):
        q = jnp.concatenate([_dot(d, w[g * POOL_GROUP:(g + 1) * POOL_GROUP]) for g, d in enumerate((d0, d1, d2, d3))], axis=1)
        return q * scale * _silu(z)

    return tok_call(name, body, list(dgs) + [z_pool], [], [pool_w, pool_scale], [(D, BF16)], [], [])[0]


def pool_mix_bwd(name, dgs, z_pool, dyp, pool_w, pool_scale):
    def body(d0, d1, d2, d3, z, dyp, w, scale):
        ds = (d0, d1, d2, d3)
        q = jnp.concatenate([_dot(d, w[g * POOL_GROUP:(g + 1) * POOL_GROUP]) for g, d in enumerate(ds)], axis=1)
        dypm = dyp * _silu(z)
        dz = dyp * (q * scale) * _dsilu(z)
        dq = (dypm * scale).astype(BF16)
        dds, gws = [], []
        for g, d in enumerate(ds):
            dqg = dq[:, g * POOL_GROUP:(g + 1) * POOL_GROUP]
            dds.append(_dot_nt(dqg, w[g * POOL_GROUP:(g + 1) * POOL_GROUP]))
            gws.append(_dot_tn(d, dqg))
        return (*dds, dz, jnp.concatenate(gws, axis=0), _colsum(dypm * q))

    return tok_call(name, body, list(dgs) + [z_pool, dyp], [], [pool_w, pool_scale],
                    [(POOL_GROUP, F32)] * 4 + [(D, BF16)], [], [(D, POOL_GROUP), (1, D)])


def _cumsum_lanes(a, reverse):
    n = a.shape[1]
    k = lax.broadcasted_iota(jnp.int32, (n, n), 0)
    i = lax.broadcasted_iota(jnp.int32, (n, n), 1)
    tri = jnp.where((k >= i) if reverse else (k <= i), 1.0, 0.0).astype(BF16)
    return _dot_exact01(a, tri)


def _rows_to_cols(rows):
    r = rows.shape[0]
    if r < LANES:
        rows = jnp.concatenate([rows, jnp.zeros((LANES - r, rows.shape[1]), F32)], axis=0)
    return rows.T


def _cols_to_rows(cols):
    q = cols[0].shape[0]
    lane = lax.broadcasted_iota(jnp.int32, (q, LANES), 1)
    acc = jnp.zeros((q, LANES), F32)
    for r, c in enumerate(cols):
        acc = acc + jnp.where(lane == r, c, 0.0)
    return acc.T[0:len(cols)]


def _ssd_scalars(dtraw, bias, alog, reverse):
    dt = _softplus(dtraw + bias)
    A = -jnp.exp(alog)
    cs = _cumsum_lanes(dt * A, reverse)
    total = cs[:, 0:1] if reverse else cs[:, CHUNK - 1:CHUNK]
    return dt, A, cs, total


def _decay_matrix(cs_col, cs_row, reverse):
    i = lax.broadcasted_iota(jnp.int32, (CHUNK, CHUNK), 0)
    j = lax.broadcasted_iota(jnp.int32, (CHUNK, CHUNK), 1)
    keep = (i <= j) if reverse else (i >= j)
    return jnp.exp(jnp.where(keep, cs_col - cs_row, -jnp.inf))


def ssd_fwd_v1(name, dtT, bias, alog, xbc, h0, direction, with_y):
    Bn, L = xbc.shape[:2]
    nc = L // CHUNK
    reverse = direction == 1
    rowblk = direction * N_BC

    def chunk_of(s):
        return (nc - 1 - s) if reverse else s

    def kern(dt_ref, bias_ref, alog_ref, x_ref, b_ref, c_ref, h0_ref, *rest):
        if with_y:
            y_ref, hs_ref, hf_ref, h_scr, xt_scr = rest
        else:
            hs_ref, hf_ref, h_scr, xt_scr = rest
        s = pl.program_id(2)

        @pl.when(s == 0)
        def _():
            h_scr[...] = h0_ref[0, 0]

        dt, _, cs, total = _ssd_scalars(dt_ref[0], bias_ref[0], alog_ref[0], reverse)
        e_row = jnp.exp(cs)
        t_row = jnp.exp(total - cs)
        dc = jnp.exp(total)
        cols = _rows_to_cols(jnp.concatenate([dt, e_row, t_row, cs], axis=0))
        x = x_ref[0]
        bm = b_ref[0].astype(BF16)
        cm = c_ref[0].astype(BF16)
        h = h_scr[...]
        hs_ref[0, 0, 0] = h
        if with_y:
            cb = _dot_nt(cm, bm)
            yoff = _dot(cm, h.astype(BF16))
        for r in range(HPG):
            sl = slice(r * HEAD_DIM, (r + 1) * HEAD_DIM)
            xdt = x[:, sl] * cols[:, r:r + 1]
            if with_y:
                lr = _decay_matrix(cols[:, 3 * HPG + r:3 * HPG + r + 1], cs[r:r + 1], reverse)
                ydiag = _dot((cb * lr).astype(BF16), xdt.astype(BF16))
                y_ref[0, :, sl] = ydiag + yoff[:, sl] * cols[:, HPG + r:HPG + r + 1]
            xt_scr[:, sl] = (xdt * cols[:, 2 * HPG + r:2 * HPG + r + 1]).astype(BF16)
        st = _dot_tn(bm, xt_scr[...])
        for r in range(HPG):
            sl = slice(r * HEAD_DIM, (r + 1) * HEAD_DIM)
            h_scr[:, sl] = h[:, sl] * dc[r:r + 1] + st[:, sl]

        @pl.when(s == nc - 1)
        def _():
            hf_ref[0, 0] = h_scr[...]

    in_specs = [
        pl.BlockSpec((1, HPG, CHUNK), lambda b, g, s: (b, rowblk + g, chunk_of(s))),
        pl.BlockSpec((1, HPG, 1), lambda b, g, s: (rowblk + g, 0, 0)),
        pl.BlockSpec((1, HPG, 1), lambda b, g, s: (rowblk + g, 0, 0)),
        pl.BlockSpec((1, CHUNK, GW), lambda b, g, s: (b, chunk_of(s), g)),
        pl.BlockSpec((1, CHUNK, D_STATE), lambda b, g, s: (b, chunk_of(s), D_INNER // D_STATE + g)),
        pl.BlockSpec((1, CHUNK, D_STATE), lambda b, g, s: (b, chunk_of(s), D_INNER // D_STATE + N_BC + g)),
        pl.BlockSpec((1, 1, D_STATE, GW), lambda b, g, s: (b, g, 0, 0)),
    ]
    out_shape, out_specs = [], []
    if with_y:
        out_shape.append(jax.ShapeDtypeStruct((Bn, L, D_INNER), F32))
        out_specs.append(pl.BlockSpec((1, CHUNK, GW), lambda b, g, s: (b, chunk_of(s), g)))
    out_shape += [jax.ShapeDtypeStruct((Bn, N_BC, nc, D_STATE, GW), F32), jax.ShapeDtypeStruct((Bn, N_BC, D_STATE, GW), F32)]
    out_specs += [pl.BlockSpec((1, 1, 1, D_STATE, GW), lambda b, g, s: (b, g, chunk_of(s), 0, 0)),
                  pl.BlockSpec((1, 1, D_STATE, GW), lambda b, g, s: (b, g, 0, 0))]
    return pl.pallas_call(
        kern, name=name, out_shape=out_shape, grid=(Bn, N_BC, nc), in_specs=in_specs, out_specs=out_specs,
        scratch_shapes=[pltpu.VMEM((D_STATE, GW), F32), pltpu.VMEM((CHUNK, GW), BF16)],
        compiler_params=_cp(("arbitrary", "arbitrary", "arbitrary")))(dtT, bias, alog, xbc, xbc, xbc, h0)


def ssd_bwd_v1(name, dtT, bias, alog, xbc, h_start, dy, dh_final, direction):
    Bn, L = xbc.shape[:2]
    nc = L // CHUNK
    reverse = direction == 1
    rowblk = direction * N_BC
    has_y = dy is not None
    last = 0 if reverse else CHUNK - 1

    def chunk_of(s):
        return s if reverse else (nc - 1 - s)

    def kern(*refs):
        if has_y:
            (dt_ref, bias_ref, alog_ref, x_ref, b_ref, c_ref, hs_ref, dhf_ref, dy_ref,
             dx_ref, db_ref, dc_ref, ddt_ref, dbias_ref, dalog_ref, dh0_ref, dh_scr, e_scr, t_scr) = refs
        else:
            (dt_ref, bias_ref, alog_ref, x_ref, b_ref, hs_ref, dhf_ref,
             dx_ref, db_ref, ddt_ref, dbias_ref, dalog_ref, dh0_ref, dh_scr, t_scr) = refs
        s = pl.program_id(2)

        @pl.when(s == 0)
        def _():
            dh_scr[...] = dhf_ref[0, 0]
            dbias_ref[...] = jnp.zeros(dbias_ref.shape, F32)
            dalog_ref[...] = jnp.zeros(dalog_ref.shape, F32)

        dtraw = dt_ref[0]
        dt, A, cs, total = _ssd_scalars(dtraw, bias_ref[0], alog_ref[0], reverse)
        e_row = jnp.exp(cs)
        t_row = jnp.exp(total - cs)
        dcy = jnp.exp(total)
        cols = _rows_to_cols(jnp.concatenate([dt, e_row, t_row, cs], axis=0))
        x = x_ref[0]
        bm = b_ref[0].astype(BF16)
        h = hs_ref[0, 0, 0]
        dh = dh_scr[...]
        dh_bf = dh.astype(BF16)
        bdh = _dot(bm, dh_bf)
        if has_y:
            cm = c_ref[0].astype(BF16)
            dyv = dy_ref[0]
            cb = _dot_nt(cm, bm)
            yoff = _dot(cm, h.astype(BF16))
            dcb = jnp.zeros((CHUNK, CHUNK), F32)
        col_terms, row_terms, ddt_cols, dtot = [], [], [], []
        for r in range(HPG):
            sl = slice(r * HEAD_DIM, (r + 1) * HEAD_DIM)
            dt_c = cols[:, r:r + 1]
            e_c = cols[:, HPG + r:HPG + r + 1]
            t_c = cols[:, 2 * HPG + r:2 * HPG + r + 1]
            xr = x[:, sl]
            xdt = xr * dt_c
            dxdt = t_c * bdh[:, sl]
            d_t = jnp.sum(bdh[:, sl] * xdt, axis=1, keepdims=True)
            col = -(t_c * d_t)
            tot = jnp.sum(t_c * d_t, axis=0, keepdims=True) + dcy[r:r + 1] * jnp.sum(h[:, sl] * dh[:, sl], keepdims=True)
            if has_y:
                dyr = dyv[:, sl]
                lr = _decay_matrix(cols[:, 3 * HPG + r:3 * HPG + r + 1], cs[r:r + 1], reverse)
                w = cb * lr
                gm = _dot_nt(dyr.astype(BF16), xdt.astype(BF16))
                m = gm * w
                dcb = dcb + gm * lr
                dxdt = dxdt + _dot_tn(w.astype(BF16), dyr.astype(BF16))
                col = col + jnp.sum(m, axis=1, keepdims=True) + jnp.sum(yoff[:, sl] * dyr, axis=1, keepdims=True) * e_c
                row_terms.append(-jnp.sum(m, axis=0, keepdims=True))
                e_scr[:, sl] = (e_c * dyr).astype(BF16)
            t_scr[:, sl] = (t_c * xdt).astype(BF16)
            dx_ref[0, :, sl] = dxdt * dt_c
            ddt_cols.append(jnp.sum(dxdt * xr, axis=1, keepdims=True))
            col_terms.append(col)
            dtot.append(tot)
        db = _dot_nt(t_scr[...], dh_bf)
        if has_y:
            dcb_bf = dcb.astype(BF16)
            db = db + _dot_tn(dcb_bf, cm)
            dc_ref[0] = _dot(dcb_bf, bm) + _dot_nt(e_scr[...], h.astype(BF16))
            cte = _dot_tn(cm, e_scr[...])
        db_ref[0] = db
        for r in range(HPG):
            sl = slice(r * HEAD_DIM, (r + 1) * HEAD_DIM)
            new = dh[:, sl] * dcy[r:r + 1]
            if has_y:
                new = new + cte[:, sl]
            dh_scr[:, sl] = new
        dcs = _cols_to_rows(col_terms)
        if has_y:
            dcs = dcs + _stack_rows(row_terms)
        lane = lax.broadcasted_iota(jnp.int32, (HPG, CHUNK), 1)
        dcs = dcs + jnp.where(lane == last, _stack_rows([jnp.broadcast_to(t, (1, CHUNK)) for t in dtot]), 0.0)
        da = _cumsum_lanes(dcs, not reverse)
        ddt = da * A + _cols_to_rows(ddt_cols)
        ddtraw = ddt * _sigmoid(dtraw + bias_ref[0])
        ddt_ref[0] = ddtraw
        dbias_ref[0, 0] += jnp.sum(ddtraw, axis=1, keepdims=True)
        dalog_ref[0, 0] += jnp.sum(da * dt, axis=1, keepdims=True) * A

        @pl.when(s == nc - 1)
        def _():
            dh0_ref[0, 0] = dh_scr[...]

    cidx = lambda b, g, s: (b, chunk_of(s), g)
    in_specs = [
        pl.BlockSpec((1, HPG, CHUNK), lambda b, g, s: (b, rowblk + g, chunk_of(s))),
        pl.BlockSpec((1, HPG, 1), lambda b, g, s: (rowblk + g, 0, 0)),
        pl.BlockSpec((1, HPG, 1), lambda b, g, s: (rowblk + g, 0, 0)),
        pl.BlockSpec((1, CHUNK, GW), cidx),
        pl.BlockSpec((1, CHUNK, D_STATE), lambda b, g, s: (b, chunk_of(s), D_INNER // D_STATE + g)),
    ]
    args = [dtT, bias, alog, xbc, xbc]
    if has_y:
        in_specs.append(pl.BlockSpec((1, CHUNK, D_STATE), lambda b, g, s: (b, chunk_of(s), D_INNER // D_STATE + N_BC + g)))
        args.append(xbc)
    in_specs += [pl.BlockSpec((1, 1, 1, D_STATE, GW), lambda b, g, s: (b, g, chunk_of(s), 0, 0)),
                 pl.BlockSpec((1, 1, D_STATE, GW), lambda b, g, s: (b, g, 0, 0))]
    args += [h_start, dh_final]
    if has_y:
        in_specs.append(pl.BlockSpec((1, CHUNK, GW), cidx))
        args.append(dy)
    out_shape = [jax.ShapeDtypeStruct((Bn, L, D_INNER), F32), jax.ShapeDtypeStruct((Bn, L, N_BC * D_STATE), F32)]
    out_specs = [pl.BlockSpec((1, CHUNK, GW), cidx), pl.BlockSpec((1, CHUNK, D_STATE), cidx)]
    if has_y:
        out_shape.append(jax.ShapeDtypeStruct((Bn, L, N_BC * D_STATE), F32))
        out_specs.append(pl.BlockSpec((1, CHUNK, D_STATE), cidx))
    out_shape += [jax.ShapeDtypeStruct((Bn, N_HEADS, L), F32), jax.ShapeDtypeStruct((Bn, N_BC, HPG, 1), F32),
                  jax.ShapeDtypeStruct((Bn, N_BC, HPG, 1), F32), jax.ShapeDtypeStruct((Bn, N_BC, D_STATE, GW), F32)]
    out_specs += [pl.BlockSpec((1, HPG, CHUNK), lambda b, g, s: (b, g, chunk_of(s))),
                  pl.BlockSpec((1, 1, HPG, 1), lambda b, g, s: (b, g, 0, 0)),
                  pl.BlockSpec((1, 1, HPG, 1), lambda b, g, s: (b, g, 0, 0)),
                  pl.BlockSpec((1, 1, D_STATE, GW), lambda b, g, s: (b, g, 0, 0))]
    scratch = [pltpu.VMEM((D_STATE, GW), F32)] + ([pltpu.VMEM((CHUNK, GW), BF16)] if has_y else []) + [pltpu.VMEM((CHUNK, GW), BF16)]
    res = pl.pallas_call(
        kern, name=name, out_shape=out_shape, grid=(Bn, N_BC, nc), in_specs=in_specs, out_specs=out_specs,
        scratch_shapes=scratch, compiler_params=_cp(("arbitrary", "arbitrary", "arbitrary")))(*args)
    if has_y:
        return res
    dxs, db, ddt, dbias, dalog, dh0 = res
    return dxs, db, None, ddt, dbias, dalog, dh0


def _tri_mask(transposed, reverse):
    sub = lax.broadcasted_iota(jnp.int32, (CHUNK, CHUNK), 0)
    lane = lax.broadcasted_iota(jnp.int32, (CHUNK, CHUNK), 1)
    i, j = (lane, sub) if transposed else (sub, lane)
    return (i <= j) if reverse else (i >= j)


def ssd_fwd(name, dtT, bias, alog, xbc, h0, direction, with_y):
    Bn, L = xbc.shape[:2]
    nc = L // CHUNK
    reverse = direction == 1
    rowblk = direction * N_BC

    def chunk_of(s):
        return (nc - 1 - s) if reverse else s

    def kern(dt_ref, bias_ref, alog_ref, x_ref, b_ref, c_ref, h0_ref, *rest):
        if with_y:
            y_ref, hs_ref, hf_ref, h_scr = rest
        else:
            hs_ref, hf_ref, h_scr = rest
        s = pl.program_id(2)

        @pl.when(s == 0)
        def _():
            h_scr[...] = h0_ref[0, 0]

        dt, _, cs, total = _ssd_scalars(dt_ref[0], bias_ref[0], alog_ref[0], reverse)
        u = cs - jnp.log(dt)
        dtt = jnp.exp(total - u)
        dc = jnp.exp(total)
        x_bf = x_ref[0].astype(BF16)
        bm = b_ref[0]
        h = h_scr[...]
        h_bf = h.astype(BF16)
        hs_ref[0, 0, 0] = h
        bt = bm.T
        if with_y:
            cm = c_ref[0]
            cb = _dot_nt(cm.astype(BF16), bm.astype(BF16))
            cs_cols = _rows_to_cols(cs)
            keep = _tri_mask(False, reverse)
        first = lax.broadcasted_iota(jnp.int32, (1, LANES), 1) < HEAD_DIM
        heads = range(HPG)
        psl = [slice((r // 2) * LANES, (r // 2 + 1) * LANES) for r in heads]
        lhs = []
        if with_y:
            for r in heads:
                cs_col = jnp.broadcast_to(cs_cols[:, r:r + 1], (CHUNK, LANES))
                wf = cb * jnp.exp(jnp.where(keep, cs_col - u[r:r + 1], -jnp.inf))
                lhs.append(jnp.concatenate([wf.astype(BF16), (cm * jnp.exp(cs_col)).astype(BF16)], axis=1))
        bts = [(bt * dtt[r:r + 1]).astype(BF16) for r in heads]
        sts = [_dot(bts[r], x_bf[:, psl[r]]) for r in heads]
        if with_y:
            ys = [_dot(lhs[r], jnp.concatenate([x_bf[:, psl[r]], h_bf[:, psl[r]]], axis=0)) for r in heads]
        for p in range(HPG // 2):
            if with_y:
                y_ref[0, :, psl[2 * p]] = jnp.where(first, ys[2 * p], ys[2 * p + 1])
            dc_p = jnp.where(first, dc[2 * p:2 * p + 1], dc[2 * p + 1:2 * p + 2])
            h_scr[:, psl[2 * p]] = h[:, psl[2 * p]] * dc_p + jnp.where(first, sts[2 * p], sts[2 * p + 1])

        @pl.when(s == nc - 1)
        def _():
            hf_ref[0, 0] = h_scr[...]

    in_specs = [
        pl.BlockSpec((1, HPG, CHUNK), lambda b, g, s: (b, rowblk + g, chunk_of(s))),
        pl.BlockSpec((1, HPG, 1), lambda b, g, s: (rowblk + g, 0, 0)),
        pl.BlockSpec((1, HPG, 1), lambda b, g, s: (rowblk + g, 0, 0)),
        pl.BlockSpec((1, CHUNK, GW), lambda b, g, s: (b, chunk_of(s), g)),
        pl.BlockSpec((1, CHUNK, D_STATE), lambda b, g, s: (b, chunk_of(s), D_INNER // D_STATE + g)),
        pl.BlockSpec((1, CHUNK, D_STATE), lambda b, g, s: (b, chunk_of(s), D_INNER // D_STATE + N_BC + g)),
        pl.BlockSpec((1, 1, D_STATE, GW), lambda b, g, s: (b, g, 0, 0)),
    ]
    out_shape, out_specs = [], []
    if with_y:
        out_shape.append(jax.ShapeDtypeStruct((Bn, L, D_INNER), F32))
        out_specs.append(pl.BlockSpec((1, CHUNK, GW), lambda b, g, s: (b, chunk_of(s), g)))
    out_shape += [jax.ShapeDtypeStruct((Bn, N_BC, nc, D_STATE, GW), F32), jax.ShapeDtypeStruct((Bn, N_BC, D_STATE, GW), F32)]
    out_specs += [pl.BlockSpec((1, 1, 1, D_STATE, GW), lambda b, g, s: (b, g, chunk_of(s), 0, 0)),
                  pl.BlockSpec((1, 1, D_STATE, GW), lambda b, g, s: (b, g, 0, 0))]
    return pl.pallas_call(
        kern, name=name, out_shape=out_shape, grid=(Bn, N_BC, nc), in_specs=in_specs, out_specs=out_specs,
        scratch_shapes=[pltpu.VMEM((D_STATE, GW), F32)],
        compiler_params=_cp(("arbitrary", "arbitrary", "arbitrary")))(dtT, bias, alog, xbc, xbc, xbc, h0)


def ssd_bwd(name, dtT, bias, alog, xbc, h_start, dy, dh_final, direction):
    Bn, L = xbc.shape[:2]
    nc = L // CHUNK
    reverse = direction == 1
    rowblk = direction * N_BC
    has_y = dy is not None
    last = 0 if reverse else CHUNK - 1

    def chunk_of(s):
        return s if reverse else (nc - 1 - s)

    def kern(*refs):
        if has_y:
            (dt_ref, bias_ref, alog_ref, x_ref, b_ref, hs_ref, dhf_ref, c_ref, dy_ref,
             dx_ref, db_ref, ddt_ref, dbias_ref, dalog_ref, dh0_ref, dc_ref, dh_scr) = refs
        else:
            (dt_ref, bias_ref, alog_ref, x_ref, b_ref, hs_ref, dhf_ref,
             dx_ref, db_ref, ddt_ref, dbias_ref, dalog_ref, dh0_ref, dh_scr) = refs
        s = pl.program_id(2)

        @pl.when(s == 0)
        def _():
            dh_scr[...] = dhf_ref[0, 0]
            dbias_ref[...] = jnp.zeros(dbias_ref.shape, F32)
            dalog_ref[...] = jnp.zeros(dalog_ref.shape, F32)

        dtraw = dt_ref[0]
        dt, A, cs, total = _ssd_scalars(dtraw, bias_ref[0], alog_ref[0], reverse)
        u = cs - jnp.log(dt)
        dtt = jnp.exp(total - u)
        dcy = jnp.exp(total)
        u_cols = _rows_to_cols(u)
        x_bf = x_ref[0].astype(BF16)
        bm = b_ref[0]
        bt = bm.T
        h = hs_ref[0, 0, 0]
        dh = dh_scr[...]
        dh_bf = dh.astype(BF16)
        dbt = jnp.zeros((D_STATE, CHUNK), F32)
        if has_y:
            cm = c_ref[0]
            ct = cm.T
            e_row = jnp.exp(cs)
            dy_bf = dy_ref[0].astype(BF16)
            h_bf = h.astype(BF16)
            cbt = _dot_nt(bm.astype(BF16), cm.astype(BF16))
            keep = _tri_mask(True, reverse)
            dcbt = jnp.zeros((CHUNK, CHUNK), F32)
            dct = jnp.zeros((D_STATE, CHUNK), F32)
        tots, out_rows, in_rows, in_cols = [], [], [], []
        first = lax.broadcasted_iota(jnp.int32, (1, LANES), 1) < HEAD_DIM
        heads = range(HPG)
        psl = [slice((r // 2) * LANES, (r // 2 + 1) * LANES) for r in heads]
        mine = [first if r % 2 == 0 else jnp.logical_not(first) for r in heads]
        zeros_bf = jnp.zeros((CHUNK, LANES), BF16)

        def prep(r):
            u_col = jnp.broadcast_to(u_cols[:, r:r + 1], (CHUNK, LANES))
            bs = (bm * jnp.exp(total[r:r + 1] - u_col)).astype(BF16)
            if not has_y:
                return bs, None
            et = jnp.exp(jnp.where(keep, cs[r:r + 1] - u_col, -jnp.inf))
            return jnp.concatenate([(cbt * et).astype(BF16), bs], axis=1), et

        def matmuls(r, lhs):
            p2raw = _dot_nt(dh_bf[:, psl[r]], jnp.where(mine[r], x_bf[:, psl[r]], zeros_bf))
            if not has_y:
                return p2raw, None, None, _dot(lhs, dh_bf[:, psl[r]])
            a1 = _dot_nt(jnp.concatenate([x_bf[:, psl[r]], h_bf[:, psl[r]]], axis=0),
                         jnp.where(mine[r], dy_bf[:, psl[r]], zeros_bf))
            new = _dot((ct * e_row[r:r + 1]).astype(BF16), dy_bf[:, psl[r]])
            dx = _dot(lhs, jnp.concatenate([dy_bf[:, psl[r]], dh_bf[:, psl[r]]], axis=0))
            return p2raw, a1, new, dx

        def post(r, p2raw, a1, et, dbt, dcbt, dct):
            if has_y:
                pt = a1[0:CHUNK] * et
                dcbt = dcbt + pt
                mt = pt * cbt
                ph = a1[CHUNK:] * e_row[r:r + 1]
                dct = dct + ph
                out_rows.append(_colsum(mt + ct * ph))
                in_cols.append(jnp.sum(mt, axis=1, keepdims=True))
            p2 = p2raw * dtt[r:r + 1]
            dbt = dbt + p2
            t_term = _colsum(bt * p2)
            in_rows.append(t_term)
            hdh = h[:, psl[r]] * dh[:, psl[r]]
            tot = jnp.sum(t_term, axis=1, keepdims=True) + dcy[r:r + 1] * jnp.sum(jnp.where(mine[r], hdh, 0.0), keepdims=True)
            tots.append(jnp.broadcast_to(tot, (1, CHUNK)))
            return dbt, dcbt, dct

        if not has_y:
            dcbt = dct = None
        dxs, news, pending = [], [], []
        batch = HPG
        for r0 in range(0, HPG, batch):
            preps = [prep(r) for r in range(r0, r0 + batch)]
            mms = [matmuls(r, preps[r - r0][0]) for r in range(r0, r0 + batch)]
            for args in pending:
                dbt, dcbt, dct = post(*args, dbt, dcbt, dct)
            pending = [(r, mms[r - r0][0], mms[r - r0][1], preps[r - r0][1]) for r in range(r0, r0 + batch)]
            dxs += [m[3] for m in mms]
            news += [m[2] for m in mms]
        for args in pending:
            dbt, dcbt, dct = post(*args, dbt, dcbt, dct)
        for p in range(HPG // 2):
            dx_ref[0, :, psl[2 * p]] = jnp.where(first, dxs[2 * p], dxs[2 * p + 1])
            new = dh[:, psl[2 * p]] * jnp.where(first, dcy[2 * p:2 * p + 1], dcy[2 * p + 1:2 * p + 2])
            if has_y:
                new = new + jnp.where(first, news[2 * p], news[2 * p + 1])
            dh_scr[:, psl[2 * p]] = new
        db = dbt.T
        if has_y:
            dcbt_bf = dcbt.astype(BF16)
            db = db + _dot(dcbt_bf, cm.astype(BF16))
            dc_ref[0] = dct.T + _dot_tn(dcbt_bf, bm.astype(BF16))
        db_ref[0] = db
        s_row = _stack_rows(in_rows)
        lane = lax.broadcasted_iota(jnp.int32, (HPG, CHUNK), 1)
        dcs = jnp.where(lane == last, _stack_rows(tots), 0.0)
        if has_y:
            s_row = s_row + _cols_to_rows(in_cols)
            dcs = dcs + _stack_rows(out_rows)
        dcs = dcs - s_row
        da = _cumsum_lanes(dcs, not reverse)
        ddt = da * A + jnp.where(dt > 0.0, s_row / dt, 0.0)
        ddtraw = ddt * _sigmoid(dtraw + bias_ref[0])
        ddt_ref[0] = ddtraw
        dbias_ref[0, 0] += jnp.sum(ddtraw, axis=1, keepdims=True)
        dalog_ref[0, 0] += jnp.sum(da * dt, axis=1, keepdims=True) * A

        @pl.when(s == nc - 1)
        def _():
            dh0_ref[0, 0] = dh_scr[...]

    cidx = lambda b, g, s: (b, chunk_of(s), g)
    hidx = lambda b, g, s: (b, g, 0, 0)
    in_specs = [
        pl.BlockSpec((1, HPG, CHUNK), lambda b, g, s: (b, rowblk + g, chunk_of(s))),
        pl.BlockSpec((1, HPG, 1), lambda b, g, s: (rowblk + g, 0, 0)),
        pl.BlockSpec((1, HPG, 1), lambda b, g, s: (rowblk + g, 0, 0)),
        pl.BlockSpec((1, CHUNK, GW), cidx),
        pl.BlockSpec((1, CHUNK, D_STATE), lambda b, g, s: (b, chunk_of(s), D_INNER // D_STATE + g)),
        pl.BlockSpec((1, 1, 1, D_STATE, GW), lambda b, g, s: (b, g, chunk_of(s), 0, 0)),
        pl.BlockSpec((1, 1, D_STATE, GW), hidx),
    ]
    args = [dtT, bias, alog, xbc, xbc, h_start, dh_final]
    if has_y:
        in_specs += [pl.BlockSpec((1, CHUNK, D_STATE), lambda b, g, s: (b, chunk_of(s), D_INNER // D_STATE + N_BC + g)),
                     pl.BlockSpec((1, CHUNK, GW), cidx)]
        args += [xbc, dy]
    out_shape = [jax.ShapeDtypeStruct((Bn, L, D_INNER), F32), jax.ShapeDtypeStruct((Bn, L, N_BC * D_STATE), F32),
                 jax.ShapeDtypeStruct((Bn, N_HEADS, L), F32), jax.ShapeDtypeStruct((Bn, N_BC, HPG, 1), F32),
                 jax.ShapeDtypeStruct((Bn, N_BC, HPG, 1), F32), jax.ShapeDtypeStruct((Bn, N_BC, D_STATE, GW), F32)]
    out_specs = [pl.BlockSpec((1, CHUNK, GW), cidx), pl.BlockSpec((1, CHUNK, D_STATE), cidx),
                 pl.BlockSpec((1, HPG, CHUNK), lambda b, g, s: (b, g, chunk_of(s))),
                 pl.BlockSpec((1, 1, HPG, 1), hidx), pl.BlockSpec((1, 1, HPG, 1), hidx), pl.BlockSpec((1, 1, D_STATE, GW), hidx)]
    if has_y:
        out_shape.append(jax.ShapeDtypeStruct((Bn, L, N_BC * D_STATE), F32))
        out_specs.append(pl.BlockSpec((1, CHUNK, D_STATE), cidx))
    res = pl.pallas_call(
        kern, name=name, out_shape=out_shape, grid=(Bn, N_BC, nc), in_specs=in_specs, out_specs=out_specs,
        scratch_shapes=[pltpu.VMEM((D_STATE, GW), F32)],
        compiler_params=_cp(("arbitrary", "arbitrary", "arbitrary")))(*args)
    dxs, db, ddt, dbias, dalog, dh0 = res[:6]
    return dxs, db, (res[6] if has_y else None), ddt, dbias, dalog, dh0


GPS = 4


def ssd_fwd3(name, dtT, bias, alog, xbc, h0, direction, with_y):
    Bn, L = xbc.shape[:2]
    nc = L // CHUNK
    reverse = direction == 1
    blk0 = direction * (N_BC // GPS)
    gs = range(GPS)

    def chunk_of(s):
        return (nc - 1 - s) if reverse else s

    def kern(dt_ref, bias_ref, alog_ref, x_ref, b_ref, c_ref, h0_ref, *rest):
        if with_y:
            y_ref, hs_ref, hf_ref, h_scr = rest
        else:
            hs_ref, hf_ref, h_scr = rest
        s = pl.program_id(2)

        @pl.when(s == 0)
        def _():
            h_scr[...] = h0_ref[0]

        first = lax.broadcasted_iota(jnp.int32, (1, LANES), 1) < HEAD_DIM
        heads = range(HPG)
        psl = [slice((r // 2) * LANES, (r // 2 + 1) * LANES) for r in heads]
        keep = _tri_mask(False, reverse)
        sc, x_bf, bm, h, h_bf, bt, cm, cb, cs_cols = [], [], [], [], [], [], [], [], []
        for g in gs:
            dt, _, cs, total = _ssd_scalars(dt_ref[0, g * HPG:(g + 1) * HPG], bias_ref[g], alog_ref[g], reverse)
            u = cs - jnp.log(dt)
            sc.append((cs, u, jnp.exp(total - u), jnp.exp(total)))
            x_bf.append(x_ref[0, :, g * GW:(g + 1) * GW].astype(BF16))
            bm.append(b_ref[0, :, g * D_STATE:(g + 1) * D_STATE])
            h.append(h_scr[g])
            h_bf.append(h[g].astype(BF16))
            hs_ref[0, g, 0] = h[g]
            bt.append(bm[g].T)
            if with_y:
                cm.append(c_ref[0, :, g * D_STATE:(g + 1) * D_STATE])
                cb.append(_dot_nt(cm[g].astype(BF16), bm[g].astype(BF16)))
                cs_cols.append(_rows_to_cols(cs))
        lhs = [[] for _ in gs]
        if with_y:
            for g in gs:
                cs, u = sc[g][0], sc[g][1]
                for r in heads:
                    cs_col = jnp.broadcast_to(cs_cols[g][:, r:r + 1], (CHUNK, LANES))
                    wf = cb[g] * jnp.exp(jnp.where(keep, cs_col - u[r:r + 1], -jnp.inf))
                    lhs[g].append(jnp.concatenate([wf.astype(BF16), (cm[g] * jnp.exp(cs_col)).astype(BF16)], axis=1))
        bts = [[(bt[g] * sc[g][2][r:r + 1]).astype(BF16) for r in heads] for g in gs]
        sts = [[_dot(bts[g][r], x_bf[g][:, psl[r]]) for r in heads] for g in gs]
        if with_y:
            ys = [[_dot(lhs[g][r], jnp.concatenate([x_bf[g][:, psl[r]], h_bf[g][:, psl[r]]], axis=0)) for r in heads] for g in gs]
        for g in gs:
            dc = sc[g][3]
            for p in range(HPG // 2):
                if with_y:
                    y_ref[0, :, g * GW + p * LANES:g * GW + (p + 1) * LANES] = jnp.where(first, ys[g][2 * p], ys[g][2 * p + 1])
                dc_p = jnp.where(first, dc[2 * p:2 * p + 1], dc[2 * p + 1:2 * p + 2])
                h_scr[g, :, psl[2 * p]] = h[g][:, psl[2 * p]] * dc_p + jnp.where(first, sts[g][2 * p], sts[g][2 * p + 1])

        @pl.when(s == nc - 1)
        def _():
            hf_ref[0] = h_scr[...]

    nb = D_INNER // (GPS * D_STATE)
    in_specs = [
        pl.BlockSpec((1, GPS * HPG, CHUNK), lambda b, g, s: (b, blk0 + g, chunk_of(s))),
        pl.BlockSpec((GPS, HPG, 1), lambda b, g, s: (blk0 + g, 0, 0)),
        pl.BlockSpec((GPS, HPG, 1), lambda b, g, s: (blk0 + g, 0, 0)),
        pl.BlockSpec((1, CHUNK, GPS * GW), lambda b, g, s: (b, chunk_of(s), g)),
        pl.BlockSpec((1, CHUNK, GPS * D_STATE), lambda b, g, s: (b, chunk_of(s), nb + g)),
        pl.BlockSpec((1, CHUNK, GPS * D_STATE), lambda b, g, s: (b, chunk_of(s), nb + N_BC // GPS + g)),
        pl.BlockSpec((1, GPS, D_STATE, GW), lambda b, g, s: (b, g, 0, 0)),
    ]
    out_shape, out_specs = [], []
    if with_y:
        out_shape.append(jax.ShapeDtypeStruct((Bn, L, D_INNER), F32))
        out_specs.append(pl.BlockSpec((1, CHUNK, GPS * GW), lambda b, g, s: (b, chunk_of(s), g)))
    out_shape += [jax.ShapeDtypeStruct((Bn, N_BC, nc, D_STATE, GW), F32), jax.ShapeDtypeStruct((Bn, N_BC, D_STATE, GW), F32)]
    out_specs += [pl.BlockSpec((1, GPS, 1, D_STATE, GW), lambda b, g, s: (b, g, chunk_of(s), 0, 0)),
                  pl.BlockSpec((1, GPS, D_STATE, GW), lambda b, g, s: (b, g, 0, 0))]
    return pl.pallas_call(
        kern, name=name, out_shape=out_shape, grid=(Bn, N_BC // GPS, nc), in_specs=in_specs, out_specs=out_specs,
        scratch_shapes=[pltpu.VMEM((GPS, D_STATE, GW), F32)],
        compiler_params=_cp(("arbitrary", "arbitrary", "arbitrary")))(dtT, bias, alog, xbc, xbc, xbc, h0)


def ssd_bwd3(name, dtT, bias, alog, xbc, h_start, dy, dh_final, direction):
    Bn, L = xbc.shape[:2]
    nc = L // CHUNK
    reverse = direction == 1
    blk0 = direction * (N_BC // GPS)
    has_y = dy is not None
    last = 0 if reverse else CHUNK - 1
    gs = range(GPS)

    def chunk_of(s):
        return s if reverse else (nc - 1 - s)

    def kern(*refs):
        if has_y:
            (dt_ref, bias_ref, alog_ref, x_ref, b_ref, hs_ref, dhf_ref, c_ref, dy_ref,
             dx_ref, db_ref, ddt_ref, dbias_ref, dalog_ref, dh0_ref, dc_ref, dh_scr) = refs
        else:
            (dt_ref, bias_ref, alog_ref, x_ref, b_ref, hs_ref, dhf_ref,
             dx_ref, db_ref, ddt_ref, dbias_ref, dalog_ref, dh0_ref, dh_scr) = refs
        s = pl.program_id(2)

        @pl.when(s == 0)
        def _():
            dh_scr[...] = dhf_ref[0]
            dbias_ref[...] = jnp.zeros(dbias_ref.shape, F32)
            dalog_ref[...] = jnp.zeros(dalog_ref.shape, F32)

        first = lax.broadcasted_iota(jnp.int32, (1, LANES), 1) < HEAD_DIM
        heads = range(HPG)
        psl = [slice((r // 2) * LANES, (r // 2 + 1) * LANES) for r in heads]
        mine = [first if r % 2 == 0 else jnp.logical_not(first) for r in heads]
        zeros_bf = jnp.zeros((CHUNK, LANES), BF16)
        keep = _tri_mask(True, reverse)
        ctx = []
        for g in gs:
            dtraw = dt_ref[0, g * HPG:(g + 1) * HPG]
            dt, A, cs, total = _ssd_scalars(dtraw, bias_ref[g], alog_ref[g], reverse)
            u = cs - jnp.log(dt)
            c = dict(dtraw=dtraw, dt=dt, A=A, cs=cs, total=total, u=u, dtt=jnp.exp(total - u), dcy=jnp.exp(total),
                     u_cols=_rows_to_cols(u), x_bf=x_ref[0, :, g * GW:(g + 1) * GW].astype(BF16),
                     bm=b_ref[0, :, g * D_STATE:(g + 1) * D_STATE], h=hs_ref[0, g, 0], dh=dh_scr[g])
            c["bt"] = c["bm"].T
            c["dh_bf"] = c["dh"].astype(BF16)
            if has_y:
                c["cm"] = c_ref[0, :, g * D_STATE:(g + 1) * D_STATE]
                c["ct"] = c["cm"].T
                c["e_row"] = jnp.exp(cs)
                c["dy_bf"] = dy_ref[0, :, g * GW:(g + 1) * GW].astype(BF16)
                c["h_bf"] = c["h"].astype(BF16)
                c["cbt"] = _dot_nt(c["bm"].astype(BF16), c["cm"].astype(BF16))
            ctx.append(c)
        for c in ctx:
            c["lhs"], c["et"] = [], []
            for r in heads:
                u_col = jnp.broadcast_to(c["u_cols"][:, r:r + 1], (CHUNK, LANES))
                bs = (c["bm"] * jnp.exp(c["total"][r:r + 1] - u_col)).astype(BF16)
                if has_y:
                    et = jnp.exp(jnp.where(keep, c["cs"][r:r + 1] - u_col, -jnp.inf))
                    c["et"].append(et)
                    c["lhs"].append(jnp.concatenate([(c["cbt"] * et).astype(BF16), bs], axis=1))
                else:
                    c["lhs"].append(bs)
        for c in ctx:
            c["p2raw"] = [_dot_nt(c["dh_bf"][:, psl[r]], jnp.where(mine[r], c["x_bf"][:, psl[r]], zeros_bf)) for r in heads]
            if has_y:
                c["a1"] = [_dot_nt(jnp.concatenate([c["x_bf"][:, psl[r]], c["h_bf"][:, psl[r]]], axis=0),
                                   jnp.where(mine[r], c["dy_bf"][:, psl[r]], zeros_bf)) for r in heads]
                c["news"] = [_dot((c["ct"] * c["e_row"][r:r + 1]).astype(BF16), c["dy_bf"][:, psl[r]]) for r in heads]
                c["dxs"] = [_dot(c["lhs"][r], jnp.concatenate([c["dy_bf"][:, psl[r]], c["dh_bf"][:, psl[r]]], axis=0)) for r in heads]
            else:
                c["dxs"] = [_dot(c["lhs"][r], c["dh_bf"][:, psl[r]]) for r in heads]
        for g, c in enumerate(ctx):
            dbt = jnp.zeros((D_STATE, CHUNK), F32)
            dcbt = jnp.zeros((CHUNK, CHUNK), F32)
            dct = jnp.zeros((D_STATE, CHUNK), F32)
            tots, out_rows, in_rows, in_cols = [], [], [], []
            for r in heads:
                if has_y:
                    pt = c["a1"][r][0:CHUNK] * c["et"][r]
                    dcbt = dcbt + pt
                    mt = pt * c["cbt"]
                    ph = c["a1"][r][CHUNK:] * c["e_row"][r:r + 1]
                    dct = dct + ph
                    out_rows.append(_colsum(mt + c["ct"] * ph))
                    in_cols.append(jnp.sum(mt, axis=1, keepdims=True))
                p2 = c["p2raw"][r] * c["dtt"][r:r + 1]
                dbt = dbt + p2
                t_term = _colsum(c["bt"] * p2)
                in_rows.append(t_term)
                hdh = c["h"][:, psl[r]] * c["dh"][:, psl[r]]
                tot = jnp.sum(t_term, axis=1, keepdims=True) + c["dcy"][r:r + 1] * jnp.sum(jnp.where(mine[r], hdh, 0.0), keepdims=True)
                tots.append(jnp.broadcast_to(tot, (1, CHUNK)))
            for p in range(HPG // 2):
                dx_ref[0, :, g * GW + p * LANES:g * GW + (p + 1) * LANES] = jnp.where(first, c["dxs"][2 * p], c["dxs"][2 * p + 1])
                new = c["dh"][:, psl[2 * p]] * jnp.where(first, c["dcy"][2 * p:2 * p + 1], c["dcy"][2 * p + 1:2 * p + 2])
                if has_y:
                    new = new + jnp.where(first, c["news"][2 * p], c["news"][2 * p + 1])
                dh_scr[g, :, psl[2 * p]] = new
            db = dbt.T
            if has_y:
                dcbt_bf = dcbt.astype(BF16)
                db = db + _dot(dcbt_bf, c["cm"].astype(BF16))
                dc_ref[0, :, g * D_STATE:(g + 1) * D_STATE] = dct.T + _dot_tn(dcbt_bf, c["bm"].astype(BF16))
            db_ref[0, :, g * D_STATE:(g + 1) * D_STATE] = db
            s_row = _stack_rows(in_rows)
            lane = lax.broadcasted_iota(jnp.int32, (HPG, CHUNK), 1)
            dcs = jnp.where(lane == last, _stack_rows(tots), 0.0)
            if has_y:
                s_row = s_row + _cols_to_rows(in_cols)
                dcs = dcs + _stack_rows(out_rows)
            dcs = dcs - s_row
            da = _cumsum_lanes(dcs, not reverse)
            ddt = da * c["A"] + jnp.where(c["dt"] > 0.0, s_row / c["dt"], 0.0)
            ddtraw = ddt * _sigmoid(c["dtraw"] + bias_ref[g])
            ddt_ref[0, g * HPG:(g + 1) * HPG] = ddtraw
            dbias_ref[0, g] += jnp.sum(ddtraw, axis=1, keepdims=True)
            dalog_ref[0, g] += jnp.sum(da * c["dt"], axis=1, keepdims=True) * c["A"]

        @pl.when(s == nc - 1)
        def _():
            dh0_ref[0] = dh_scr[...]

    nb = D_INNER // (GPS * D_STATE)
    cidx = lambda b, g, s: (b, chunk_of(s), g)
    hidx = lambda b, g, s: (b, g, 0, 0)
    in_specs = [
        pl.BlockSpec((1, GPS * HPG, CHUNK), lambda b, g, s: (b, blk0 + g, chunk_of(s))),
        pl.BlockSpec((GPS, HPG, 1), lambda b, g, s: (blk0 + g, 0, 0)),
        pl.BlockSpec((GPS, HPG, 1), lambda b, g, s: (blk0 + g, 0, 0)),
        pl.BlockSpec((1, CHUNK, GPS * GW), cidx),
        pl.BlockSpec((1, CHUNK, GPS * D_STATE), lambda b, g, s: (b, chunk_of(s), nb + g)),
        pl.BlockSpec((1, GPS, 1, D_STATE, GW), lambda b, g, s: (b, g, chunk_of(s), 0, 0)),
        pl.BlockSpec((1, GPS, D_STATE, GW), hidx),
    ]
    args = [dtT, bias, alog, xbc, xbc, h_start, dh_final]
    if has_y:
        in_specs += [pl.BlockSpec((1, CHUNK, GPS * D_STATE), lambda b, g, s: (b, chunk_of(s), nb + N_BC // GPS + g)),
                     pl.BlockSpec((1, CHUNK, GPS * GW), cidx)]
        args += [xbc, dy]
    out_shape = [jax.ShapeDtypeStruct((Bn, L, D_INNER), F32), jax.ShapeDtypeStruct((Bn, L, N_BC * D_STATE), F32),
                 jax.ShapeDtypeStruct((Bn, N_HEADS, L), F32), jax.ShapeDtypeStruct((Bn, N_BC, HPG, 1), F32),
                 jax.ShapeDtypeStruct((Bn, N_BC, HPG, 1), F32), jax.ShapeDtypeStruct((Bn, N_BC, D_STATE, GW), F32)]
    out_specs = [pl.BlockSpec((1, CHUNK, GPS * GW), cidx), pl.BlockSpec((1, CHUNK, GPS * D_STATE), cidx),
                 pl.BlockSpec((1, GPS * HPG, CHUNK), lambda b, g, s: (b, g, chunk_of(s))),
                 pl.BlockSpec((1, GPS, HPG, 1), hidx), pl.BlockSpec((1, GPS, HPG, 1), hidx), pl.BlockSpec((1, GPS, D_STATE, GW), hidx)]
    if has_y:
        out_shape.append(jax.ShapeDtypeStruct((Bn, L, N_BC * D_STATE), F32))
        out_specs.append(pl.BlockSpec((1, CHUNK, GPS * D_STATE), cidx))
    res = pl.pallas_call(
        kern, name=name, out_shape=out_shape, grid=(Bn, N_BC // GPS, nc), in_specs=in_specs, out_specs=out_specs,
        scratch_shapes=[pltpu.VMEM((GPS, D_STATE, GW), F32)],
        compiler_params=_cp(("arbitrary", "arbitrary", "arbitrary")))(*args)
    dxs, db, ddt, dbias, dalog, dh0 = res[:6]
    return dxs, db, (res[6] if has_y else None), ddt, dbias, dalog, dh0


def _dot_split2(v, sel):
    hi = v.astype(BF16)
    mid = (v - hi.astype(F32)).astype(BF16)
    return _dot(hi, sel) + _dot(mid, sel)


def ssd_tables():
    lane = jnp.arange(LANES)[:, None]
    col = jnp.arange(2 * GW)[None, :]
    expand = (lane == jnp.where(col < GW, HPG + col // HEAD_DIM, 2 * HPG + (col - GW) // HEAD_DIM)).astype(BF16)
    ch = jnp.arange(GW)[:, None] // HEAD_DIM
    out = jnp.arange(2 * LANES)[None, :]
    seg = ((out == ch) | (out == LANES + HPG + ch)).astype(BF16)
    return expand, seg


def _dc_lanes(dc, first):
    return jnp.concatenate([jnp.where(first, dc[2 * p:2 * p + 1], dc[2 * p + 1:2 * p + 2]) for p in range(HPG // 2)], axis=1)


def ssd_fwd2(name, dtT, bias, alog, xbc, h0, tables, direction, with_y):
    Bn, L = xbc.shape[:2]
    nc = L // CHUNK
    reverse = direction == 1
    rowblk = direction * N_BC
    expand = tables[0]

    def chunk_of(s):
        return (nc - 1 - s) if reverse else s

    def kern(dt_ref, bias_ref, alog_ref, x_ref, b_ref, c_ref, h0_ref, xp_ref, *rest):
        if with_y:
            y_ref, hs_ref, hf_ref, h_scr = rest
        else:
            hs_ref, hf_ref, h_scr = rest
        s = pl.program_id(2)

        @pl.when(s == 0)
        def _():
            h_scr[...] = h0_ref[0, 0]

        dt, _, cs, total = _ssd_scalars(dt_ref[0], bias_ref[0], alog_ref[0], reverse)
        u = cs - jnp.log(dt)
        dtt = jnp.exp(total - u)
        cols = _rows_to_cols(jnp.concatenate([cs, dtt, jnp.exp(cs)], axis=0))
        wide = _dot_split2(cols, xp_ref[...])
        dtt_x, e_x = wide[:, 0:GW], wide[:, GW:]
        first = lax.broadcasted_iota(jnp.int32, (1, LANES), 1) < HEAD_DIM
        x = x_ref[0]
        x_bf = x.astype(BF16)
        bm = b_ref[0]
        h = h_scr[...]
        hs_ref[0, 0, 0] = h
        st = _dot(bm.T.astype(BF16), (x * dtt_x).astype(BF16))
        h_scr[...] = h * _dc_lanes(jnp.exp(total), first) + st
        if with_y:
            cm = c_ref[0].astype(BF16)
            cb = _dot_nt(cm, bm.astype(BF16))
            yoff = _dot(cm, h.astype(BF16)) * e_x
            keep = _tri_mask(False, reverse)
            wfs = []
            for r in range(HPG):
                cs_col = jnp.broadcast_to(cols[:, r:r + 1], (CHUNK, LANES))
                wfs.append((cb * jnp.exp(jnp.where(keep, cs_col - u[r:r + 1], -jnp.inf))).astype(BF16))
            yd = [_dot(wfs[r], x_bf[:, (r // 2) * LANES:(r // 2 + 1) * LANES]) for r in range(HPG)]
            for p in range(HPG // 2):
                psl = slice(p * LANES, (p + 1) * LANES)
                y_ref[0, :, psl] = jnp.where(first, yd[2 * p], yd[2 * p + 1]) + yoff[:, psl]

        @pl.when(s == nc - 1)
        def _():
            hf_ref[0, 0] = h_scr[...]

    in_specs = [
        pl.BlockSpec((1, HPG, CHUNK), lambda b, g, s: (b, rowblk + g, chunk_of(s))),
        pl.BlockSpec((1, HPG, 1), lambda b, g, s: (rowblk + g, 0, 0)),
        pl.BlockSpec((1, HPG, 1), lambda b, g, s: (rowblk + g, 0, 0)),
        pl.BlockSpec((1, CHUNK, GW), lambda b, g, s: (b, chunk_of(s), g)),
        pl.BlockSpec((1, CHUNK, D_STATE), lambda b, g, s: (b, chunk_of(s), D_INNER // D_STATE + g)),
        pl.BlockSpec((1, CHUNK, D_STATE), lambda b, g, s: (b, chunk_of(s), D_INNER // D_STATE + N_BC + g)),
        pl.BlockSpec((1, 1, D_STATE, GW), lambda b, g, s: (b, g, 0, 0)),
        pl.BlockSpec(expand.shape, lambda b, g, s: (0, 0)),
    ]
    out_shape, out_specs = [], []
    if with_y:
        out_shape.append(jax.ShapeDtypeStruct((Bn, L, D_INNER), F32))
        out_specs.append(pl.BlockSpec((1, CHUNK, GW), lambda b, g, s: (b, chunk_of(s), g)))
    out_shape += [jax.ShapeDtypeStruct((Bn, N_BC, nc, D_STATE, GW), F32), jax.ShapeDtypeStruct((Bn, N_BC, D_STATE, GW), F32)]
    out_specs += [pl.BlockSpec((1, 1, 1, D_STATE, GW), lambda b, g, s: (b, g, chunk_of(s), 0, 0)),
                  pl.BlockSpec((1, 1, D_STATE, GW), lambda b, g, s: (b, g, 0, 0))]
    return pl.pallas_call(
        kern, name=name, out_shape=out_shape, grid=(Bn, N_BC, nc), in_specs=in_specs, out_specs=out_specs,
        scratch_shapes=[pltpu.VMEM((D_STATE, GW), F32)],
        compiler_params=_cp(("arbitrary", "arbitrary", "arbitrary")))(dtT, bias, alog, xbc, xbc, xbc, h0, expand)


def ssd_bwd2(name, dtT, bias, alog, xbc, h_start, dy, dh_final, tables, direction):
    Bn, L = xbc.shape[:2]
    nc = L // CHUNK
    reverse = direction == 1
    rowblk = direction * N_BC
    has_y = dy is not None
    last = 0 if reverse else CHUNK - 1
    expand, seg = tables

    def chunk_of(s):
        return s if reverse else (nc - 1 - s)

    def kern(*refs):
        if has_y:
            (dt_ref, bias_ref, alog_ref, x_ref, b_ref, hs_ref, dhf_ref, xp_ref, seg_ref, c_ref, dy_ref,
             dx_ref, db_ref, ddt_ref, dbias_ref, dalog_ref, dh0_ref, dc_ref, dh_scr) = refs
        else:
            (dt_ref, bias_ref, alog_ref, x_ref, b_ref, hs_ref, dhf_ref, xp_ref, seg_ref,
             dx_ref, db_ref, ddt_ref, dbias_ref, dalog_ref, dh0_ref, dh_scr) = refs
        s = pl.program_id(2)

        @pl.when(s == 0)
        def _():
            dh_scr[...] = dhf_ref[0, 0]
            dbias_ref[...] = jnp.zeros(dbias_ref.shape, F32)
            dalog_ref[...] = jnp.zeros(dalog_ref.shape, F32)

        first = lax.broadcasted_iota(jnp.int32, (1, LANES), 1) < HEAD_DIM
        heads = range(HPG)
        psl = [slice((r // 2) * LANES, (r // 2 + 1) * LANES) for r in heads]
        x = x_ref[0]
        bm = b_ref[0].astype(BF16)
        h = hs_ref[0, 0, 0]
        dh = dh_scr[...]
        dh_bf = dh.astype(BF16)
        bdh = _dot(bm, dh_bf)
        if has_y:
            cm = c_ref[0].astype(BF16)
            dyv = dy_ref[0]
            dy_bf = dyv.astype(BF16)
            x_bf = x.astype(BF16)
            h_bf = h.astype(BF16)
            cbt = _dot_nt(bm, cm)
            ch = _dot(cm, h_bf)
            zeros_bf = jnp.zeros((CHUNK, LANES), BF16)
            gts = [_dot_nt(x_bf[:, psl[r]], jnp.where(first if r % 2 == 0 else jnp.logical_not(first), dy_bf[:, psl[r]], zeros_bf))
                   for r in heads]
            ct_bf = c_ref[0].T.astype(BF16)
        dtraw = dt_ref[0]
        dt, A, cs, total = _ssd_scalars(dtraw, bias_ref[0], alog_ref[0], reverse)
        u = cs - jnp.log(dt)
        dtt = jnp.exp(total - u)
        dcy = jnp.exp(total)
        cols = _rows_to_cols(jnp.concatenate([u, dtt, jnp.exp(cs)], axis=0))
        wide = _dot_split2(cols, xp_ref[...])
        dtt_x, e_x = wide[:, 0:GW], wide[:, GW:]
        term2 = bdh * dtt_x
        dbt = _dot_nt(dh_bf, (x * dtt_x).astype(BF16))
        sums = _dot_split2(term2 * x, seg_ref[:, LANES:])
        new_dh = dh * _dc_lanes(dcy, first)
        if has_y:
            dye = dyv * e_x
            dye_bf = dye.astype(BF16)
            dct = _dot_nt(h_bf, dye_bf)
            new_dh = new_dh + _dot(ct_bf, dye_bf)
            sums = sums + _dot_split2(ch * dye, seg_ref[:, 0:LANES])
            keep = _tri_mask(True, reverse)
            ets = []
            for r in heads:
                u_col = jnp.broadcast_to(cols[:, r:r + 1], (CHUNK, LANES))
                ets.append(jnp.exp(jnp.where(keep, cs[r:r + 1] - u_col, -jnp.inf)))
            wts = [(cbt * ets[r]).astype(BF16) for r in heads]
            dxd = [_dot(wts[r], dy_bf[:, psl[r]]) for r in heads]
            dcbt = jnp.zeros((CHUNK, CHUNK), F32)
            out_rows, in_cols = [], []
            for r in heads:
                pt = gts[r] * ets[r]
                dcbt = dcbt + pt
                mt = pt * cbt
                out_rows.append(_colsum(mt))
                in_cols.append(jnp.sum(mt, axis=1, keepdims=True))
            for p in range(HPG // 2):
                dx_ref[0, :, psl[2 * p]] = jnp.where(first, dxd[2 * p], dxd[2 * p + 1]) + term2[:, psl[2 * p]]
            dcbt_bf = dcbt.astype(BF16)
            db_ref[0] = dbt.T + _dot(dcbt_bf, cm)
            dc_ref[0] = dct.T + _dot_tn(dcbt_bf, bm)
        else:
            dx_ref[0] = term2
            db_ref[0] = dbt.T
        dh_scr[...] = new_dh
        sums_t = sums.T
        s_row = sums_t[HPG:2 * HPG]
        hdh = _colsum(h * dh)
        lanes_w = lax.broadcasted_iota(jnp.int32, (1, GW), 1)
        hd = _stack_rows([jnp.sum(jnp.where(lanes_w // HEAD_DIM == r, hdh, 0.0), axis=1, keepdims=True) for r in range(HPG)])
        tot = jnp.sum(s_row, axis=1, keepdims=True) + dcy * hd
        lane = lax.broadcasted_iota(jnp.int32, (HPG, CHUNK), 1)
        dcs = jnp.where(lane == last, tot, 0.0)
        if has_y:
            s_row = s_row + _cols_to_rows(in_cols)
            dcs = dcs + _stack_rows(out_rows) + sums_t[0:HPG]
        dcs = dcs - s_row
        da = _cumsum_lanes(dcs, not reverse)
        ddt = da * A + jnp.where(dt > 0.0, s_row / dt, 0.0)
        ddtraw = ddt * _sigmoid(dtraw + bias_ref[0])
        ddt_ref[0] = ddtraw
        dbias_ref[0, 0] += jnp.sum(ddtraw, axis=1, keepdims=True)
        dalog_ref[0, 0] += jnp.sum(da * dt, axis=1, keepdims=True) * A

        @pl.when(s == nc - 1)
        def _():
            dh0_ref[0, 0] = dh_scr[...]

    cidx = lambda b, g, s: (b, chunk_of(s), g)
    hidx = lambda b, g, s: (b, g, 0, 0)
    in_specs = [
        pl.BlockSpec((1, HPG, CHUNK), lambda b, g, s: (b, rowblk + g, chunk_of(s))),
        pl.BlockSpec((1, HPG, 1), lambda b, g, s: (rowblk + g, 0, 0)),
        pl.BlockSpec((1, HPG, 1), lambda b, g, s: (rowblk + g, 0, 0)),
        pl.BlockSpec((1, CHUNK, GW), cidx),
        pl.BlockSpec((1, CHUNK, D_STATE), lambda b, g, s: (b, chunk_of(s), D_INNER // D_STATE + g)),
        pl.BlockSpec((1, 1, 1, D_STATE, GW), lambda b, g, s: (b, g, chunk_of(s), 0, 0)),
        pl.BlockSpec((1, 1, D_STATE, GW), hidx),
        pl.BlockSpec(expand.shape, lambda b, g, s: (0, 0)),
        pl.BlockSpec(seg.shape, lambda b, g, s: (0, 0)),
    ]
    args = [dtT, bias, alog, xbc, xbc, h_start, dh_final, expand, seg]
    if has_y:
        in_specs += [pl.BlockSpec((1, CHUNK, D_STATE), lambda b, g, s: (b, chunk_of(s), D_INNER // D_STATE + N_BC + g)),
                     pl.BlockSpec((1, CHUNK, GW), cidx)]
        args += [xbc, dy]
    out_shape = [jax.ShapeDtypeStruct((Bn, L, D_INNER), F32), jax.ShapeDtypeStruct((Bn, L, N_BC * D_STATE), F32),
                 jax.ShapeDtypeStruct((Bn, N_HEADS, L), F32), jax.ShapeDtypeStruct((Bn, N_BC, HPG, 1), F32),
                 jax.ShapeDtypeStruct((Bn, N_BC, HPG, 1), F32), jax.ShapeDtypeStruct((Bn, N_BC, D_STATE, GW), F32)]
    out_specs = [pl.BlockSpec((1, CHUNK, GW), cidx), pl.BlockSpec((1, CHUNK, D_STATE), cidx),
                 pl.BlockSpec((1, HPG, CHUNK), lambda b, g, s: (b, g, chunk_of(s))),
                 pl.BlockSpec((1, 1, HPG, 1), hidx), pl.BlockSpec((1, 1, HPG, 1), hidx), pl.BlockSpec((1, 1, D_STATE, GW), hidx)]
    if has_y:
        out_shape.append(jax.ShapeDtypeStruct((Bn, L, N_BC * D_STATE), F32))
        out_specs.append(pl.BlockSpec((1, CHUNK, D_STATE), cidx))
    res = pl.pallas_call(
        kern, name=name, out_shape=out_shape, grid=(Bn, N_BC, nc), in_specs=in_specs, out_specs=out_specs,
        scratch_shapes=[pltpu.VMEM((D_STATE, GW), F32)],
        compiler_params=_cp(("arbitrary", "arbitrary", "arbitrary")))(*args)
    dxs, db, ddt, dbias, dalog, dh0 = res[:6]
    return dxs, db, (res[6] if has_y else None), ddt, dbias, dalog, dh0


def _group_mean(v):
    gw = D_INNER // N_BC
    parts = [jnp.broadcast_to(jnp.mean(v[:, g * gw:(g + 1) * gw], axis=-1, keepdims=True), (v.shape[0], gw)) for g in range(N_BC)]
    return jnp.concatenate(parts, axis=1)


def gated_norm_fwd(name, y_f, y_b, xs_src, z, dskip_lanes, w_norm):
    def body(yf, yb, xs, z, dsk, w):
        u = (yf + yb + dsk * xs) * _silu(z)
        r = lax.rsqrt(_group_mean(u * u) + NORM_EPS)
        return u * r * w

    return tok_call(name, body, [y_f, y_b, xs_src, z], [], [dskip_lanes, w_norm], [(D_INNER, BF16)], [], [])[0]


def _dot_exact01(v, sel):
    hi, mid, lo = _split3(v)
    return _dot(hi, sel) + _dot(mid, sel) + _dot(lo, sel)


def gated_norm_bwd(name, y_f, y_b, xs_src, z, d_out, dskip_lanes, w_norm, head_sel):
    def body(yf, yb, xs, z, do, dsk, w, sel):
        y = yf + yb + dsk * xs
        sz = _silu(z)
        u = y * sz
        r = lax.rsqrt(_group_mean(u * u) + NORM_EPS)
        duh = do * w
        du = r * (duh - u * (r * r) * _group_mean(duh * u))
        dy = du * sz
        dz = du * y * _dsilu(z)
        dsk_heads = _dot_exact01(jnp.broadcast_to(_colsum(dy * xs), (8, D_INNER)), sel)
        return dy, dz, _colsum(do * u * r), dsk_heads

    return tok_call(name, body, [y_f, y_b, xs_src, z, d_out], [], [dskip_lanes, w_norm, head_sel],
                    [(D_INNER, F32), (D_INNER, BF16)], [], [(1, D_INNER), (8, LANES)], tm=128)


def merge_fwd(name, y_pool, y_ssd, gatepre, x, target, gate, b_merge, norm_post, w_pp, w_ps, w_out):
    def body(yp, ys, gp, x, tgt, gate, bm, wpost, w_pp, w_ps, w_out):
        p1 = _dot(yp, w_pp)
        p2 = _dot(ys, w_ps)
        gates = _sigmoid(gp + bm)
        merged = gates[:, :D] * p1 + gates[:, D:] * p2
        out = _dot(merged.astype(BF16), w_out)
        r = _rms_r(out)
        outr = out * r
        nq = outr * wpost
        err = x + gate * nq - tgt
        loss = 0.5 * jnp.sum(jnp.mean(err * err, axis=-1, keepdims=True), keepdims=True).reshape(1, 1)
        g = err * (1.0 / D)
        dnq = g * gate
        dout = _rms_bwd(dnq * wpost, out, r)
        return merged, p1, p2, dout, g, _colsum(g * nq), _colsum(dnq * outr), jnp.broadcast_to(loss, (1, LANES))

    return tok_call(name, body, [y_pool, y_ssd, gatepre, x, target], [gate], [b_merge, norm_post, w_pp, w_ps, w_out],
                    [(D, BF16), (D, F32), (D, F32), (D, BF16), (D, F32)], [D], [(1, D), (1, LANES)])


def merge_bwd(name, dout, gatepre, p1, p2, b_merge, w_pp, w_ps, w_out):
    def body(dout, gp, p1, p2, bm, w_pp, w_ps, w_out):
        dmerged = _dot_nt(dout, w_out)
        gates = _sigmoid(gp + bm)
        g1, g2 = gates[:, :D], gates[:, D:]
        dp1 = (dmerged * g1).astype(BF16)
        dp2 = (dmerged * g2).astype(BF16)
        dgp = jnp.concatenate([dmerged * p1 * g1 * (1.0 - g1), dmerged * p2 * g2 * (1.0 - g2)], axis=1)
        return dp1, dp2, dgp, _dot_nt(dp1, w_pp), _dot_nt(dp2, w_ps), _colsum(dgp)

    return tok_call(name, body, [dout, gatepre, p1, p2], [], [b_merge, w_pp, w_ps, w_out],
                    [(D, BF16), (D, BF16), (2 * D, BF16), (D, F32), (D_INNER, F32)], [], [(1, 2 * D)])


def _adamw_math(w, g, m, v):
    m = ADAM_B1 * m + (1.0 - ADAM_B1) * g
    v = ADAM_B2 * v + (1.0 - ADAM_B2) * (g * g)
    m_hat = m / (1.0 - ADAM_B1 ** ADAM_STEP)
    v_hat = v / (1.0 - ADAM_B2 ** ADAM_STEP)
    delta = -ADAM_LR * (m_hat / (jnp.sqrt(v_hat) + ADAM_EPS) + ADAM_WD * w)
    return delta, m, v


def adamw(name, w, g, m, v, tr=256):
    R, C = w.shape
    tr = min(tr, R)
    assert R % tr == 0

    def body(w_ref, g_ref, m_ref, v_ref, d_ref, nm_ref, nv_ref):
        d, nm, nv = _adamw_math(w_ref[...], g_ref[...], m_ref[...], v_ref[...])
        d_ref[...] = d
        nm_ref[...] = nm
        nv_ref[...] = nv

    spec = pl.BlockSpec((tr, C), lambda i: (i, 0))
    return pl.pallas_call(
        body, name=name, out_shape=[jax.ShapeDtypeStruct((R, C), F32)] * 3, grid=(R // tr,),
        in_specs=[spec] * 4, out_specs=[spec] * 3, compiler_params=_cp(("parallel",)))(w, g, m, v)


def _me():
    return lax.axis_index("x"), lax.axis_index("y"), lax.axis_index("c")


def all_gather_small(name, v):
    R, C = v.shape

    def body(v_ref, out_ref, send_sems, recv_sems, local_sem):
        x, y, c = _me()
        me = 4 * x + 2 * y + c
        mine = pltpu.make_async_copy(v_ref, out_ref.at[me], local_sem)
        mine.start()
        copies = []
        for d in range(1, N_DEV):
            dx, dy, dc = d // 4, (d // 2) % 2, d % 2
            px, py, pc = x ^ dx, y ^ dy, c ^ dc
            copies.append(pltpu.make_async_remote_copy(
                src_ref=v_ref, dst_ref=out_ref.at[me], send_sem=send_sems.at[d - 1], recv_sem=recv_sems.at[d - 1],
                device_id=(px, py, pc), device_id_type=MESH))
        for cp in copies:
            cp.start()
        for d in range(1, N_DEV):
            dx, dy, dc = d // 4, (d // 2) % 2, d % 2
            peer = 4 * (x ^ dx) + 2 * (y ^ dy) + (c ^ dc)
            pltpu.make_async_remote_copy(
                src_ref=v_ref, dst_ref=out_ref.at[peer], send_sem=send_sems.at[d - 1], recv_sem=recv_sems.at[d - 1],
                device_id=(x ^ dx, y ^ dy, c ^ dc), device_id_type=MESH).wait_recv()
        for cp in copies:
            cp.wait_send()
        mine.wait()

    return pl.pallas_call(
        body, name=name, out_shape=jax.ShapeDtypeStruct((N_DEV, R, C), F32),
        in_specs=[pl.BlockSpec(memory_space=pltpu.VMEM)], out_specs=pl.BlockSpec(memory_space=pltpu.VMEM),
        scratch_shapes=[pltpu.SemaphoreType.DMA((N_DEV - 1,)), pltpu.SemaphoreType.DMA((N_DEV - 1,)), pltpu.SemaphoreType.DMA],
        compiler_params=pltpu.CompilerParams(vmem_limit_bytes=VMEM_LIMIT))(v)


def all_gather_chips(name, shard):
    R, C = shard.shape
    half = R // 2
    assert R % 32 == 0

    def body(s_ref, out_ref, send_sems, recv_sems):
        x, y, c = _me()
        chips = [(1 - x, y), (x, 1 - y), (1 - x, 1 - y)]

        def rows(chip, hc):
            return out_ref.at[2 * chip[0] + chip[1], pl.ds(hc * half, half), :]

        first = [pltpu.make_async_remote_copy(
            src_ref=s_ref.at[pl.ds(c * half, half), :], dst_ref=rows((x, y), c), send_sem=send_sems.at[j],
            recv_sem=recv_sems.at[j], device_id=(*chip, c), device_id_type=MESH) for j, chip in enumerate(chips)]
        for cp in first:
            cp.start()
        passed = [pltpu.make_async_remote_copy(
            src_ref=rows(chip, c), dst_ref=rows(chip, c), send_sem=send_sems.at[3 + j], recv_sem=recv_sems.at[3 + j],
            device_id=(x, y, 1 - c), device_id_type=MESH) for j, chip in enumerate(chips)]
        for j, chip in enumerate(chips):
            pltpu.make_async_remote_copy(
                src_ref=rows(chip, c), dst_ref=rows(chip, c), send_sem=send_sems.at[j], recv_sem=recv_sems.at[j],
                device_id=(*chip, c), device_id_type=MESH).wait_recv()
            passed[j].start()
        for j, chip in enumerate(chips):
            pltpu.make_async_remote_copy(
                src_ref=rows(chip, 1 - c), dst_ref=rows(chip, 1 - c), send_sem=send_sems.at[3 + j], recv_sem=recv_sems.at[3 + j],
                device_id=(x, y, 1 - c), device_id_type=MESH).wait_recv()
        for cp in first + passed:
            cp.wait_send()

    out = pl.pallas_call(
        body, name=name, out_shape=jax.ShapeDtypeStruct((N_CHIPS, R, C), shard.dtype),
        in_specs=[pl.BlockSpec(memory_space=pl.ANY)], out_specs=pl.BlockSpec(memory_space=pl.ANY),
        scratch_shapes=[pltpu.SemaphoreType.DMA((6,)), pltpu.SemaphoreType.DMA((6,))],
        compiler_params=pltpu.CompilerParams(vmem_limit_bytes=VMEM_LIMIT))(shard)
    chip = 2 * lax.axis_index("x") + lax.axis_index("y")
    return lax.dynamic_update_index_in_dim(out, shard, chip, 0)


def sibling_swap(name, v):
    def body(v_ref, out_ref, send_sem, recv_sem):
        x, y, c = _me()
        cp = pltpu.make_async_remote_copy(src_ref=v_ref, dst_ref=out_ref, send_sem=send_sem, recv_sem=recv_sem,
                                          device_id=(x, y, 1 - c), device_id_type=MESH)
        cp.start()
        cp.wait()

    return pl.pallas_call(
        body, name=name, out_shape=jax.ShapeDtypeStruct(v.shape, v.dtype),
        in_specs=[pl.BlockSpec(memory_space=pl.ANY)], out_specs=pl.BlockSpec(memory_space=pl.ANY),
        scratch_shapes=[pltpu.SemaphoreType.DMA, pltpu.SemaphoreType.DMA],
        compiler_params=pltpu.CompilerParams(vmem_limit_bytes=VMEM_LIMIT))(v)


def sibling_share(name, v):
    def body(v_ref, out_ref, send_sem, recv_sem, local_sem):
        x, y, c = _me()
        mine = pltpu.make_async_copy(v_ref, out_ref.at[c], local_sem)
        mine.start()
        cp = pltpu.make_async_remote_copy(src_ref=v_ref, dst_ref=out_ref.at[c], send_sem=send_sem, recv_sem=recv_sem,
                                          device_id=(x, y, 1 - c), device_id_type=MESH)
        cp.start()
        pltpu.make_async_remote_copy(src_ref=v_ref, dst_ref=out_ref.at[1 - c], send_sem=send_sem, recv_sem=recv_sem,
                                     device_id=(x, y, 1 - c), device_id_type=MESH).wait_recv()
        cp.wait_send()
        mine.wait()

    return pl.pallas_call(
        body, name=name, out_shape=jax.ShapeDtypeStruct((2, *v.shape), v.dtype),
        in_specs=[pl.BlockSpec(memory_space=pl.ANY)], out_specs=pl.BlockSpec(memory_space=pl.ANY),
        scratch_shapes=[pltpu.SemaphoreType.DMA, pltpu.SemaphoreType.DMA, pltpu.SemaphoreType.DMA],
        compiler_params=pltpu.CompilerParams(vmem_limit_bytes=VMEM_LIMIT))(v)


def chip_exchange(name, parts):
    def body(p_ref, out_ref, send_sems, recv_sems):
        x, y, c = _me()
        k = 2 * x + y
        chips = [(1 - x, y), (x, 1 - y), (1 - x, 1 - y)]
        sends = [pltpu.make_async_remote_copy(
            src_ref=p_ref.at[2 * chip[0] + chip[1]], dst_ref=out_ref.at[k], send_sem=send_sems.at[j], recv_sem=recv_sems.at[j],
            device_id=(*chip, c), device_id_type=MESH) for j, chip in enumerate(chips)]
        for cp in sends:
            cp.start()
        for j, chip in enumerate(chips):
            pltpu.make_async_remote_copy(
                src_ref=p_ref.at[k], dst_ref=out_ref.at[2 * chip[0] + chip[1]], send_sem=send_sems.at[j], recv_sem=recv_sems.at[j],
                device_id=(*chip, c), device_id_type=MESH).wait_recv()
        for cp in sends:
            cp.wait_send()

    out = pl.pallas_call(
        body, name=name, out_shape=jax.ShapeDtypeStruct(parts.shape, parts.dtype),
        in_specs=[pl.BlockSpec(memory_space=pl.ANY)], out_specs=pl.BlockSpec(memory_space=pl.ANY),
        scratch_shapes=[pltpu.SemaphoreType.DMA((3,)), pltpu.SemaphoreType.DMA((3,))],
        compiler_params=pltpu.CompilerParams(vmem_limit_bytes=VMEM_LIMIT))(parts)
    chip = 2 * lax.axis_index("x") + lax.axis_index("y")
    own = lax.dynamic_index_in_dim(parts, chip, 0, keepdims=True)
    return lax.dynamic_update_slice_in_dim(out, own, chip, 0)


def _row_tile(rows, cap, mult=8):
    best = None
    for t in range(mult, min(rows, cap) + 1, mult):
        if rows % t == 0:
            best = t
    assert best is not None, rows
    return best


def add_arrays(name, arrs, out_dtype=F32):
    shape = arrs[0].shape
    C = shape[-1]
    flat = [a.reshape(-1, C) for a in arrs]
    R = flat[0].shape[0]
    narrow = out_dtype == BF16 or any(a.dtype == BF16 for a in arrs)
    tr = _row_tile(R, 2048 if len(arrs) <= 2 else 1024, 16 if narrow else 8)
    n = len(flat)

    def body(*refs):
        acc = refs[0][...].astype(F32)
        for r in refs[1:n]:
            acc = acc + r[...].astype(F32)
        refs[n][...] = acc.astype(out_dtype)

    spec = pl.BlockSpec((tr, C), lambda i: (i, 0))
    out = pl.pallas_call(
        body, name=name, out_shape=jax.ShapeDtypeStruct((R, C), out_dtype), grid=(R // tr,),
        in_specs=[spec] * n, out_specs=spec, compiler_params=_cp(("parallel",)))(*flat)
    return out.reshape(shape)


def reduce_scatter_chips(slabs):
    _, R, C = slabs.shape
    half = R // 2
    c = lax.axis_index("c")
    k = 2 * lax.axis_index("x") + lax.axis_index("y")
    halves = slabs.reshape(N_CHIPS, 2, half, C)
    own = lax.dynamic_index_in_dim(halves, c, axis=1, keepdims=False)
    other = lax.dynamic_index_in_dim(halves, 1 - c, axis=1, keepdims=False)
    from_sibling = sibling_swap("rs_sibling_halves", other)
    chip_part = add_arrays("rs_add_sibling", [own, from_sibling], out_dtype=BF16)
    landed = chip_exchange("rs_chip_exchange", chip_part)
    mine = add_arrays("rs_add_chips", [landed[j] for j in range(N_CHIPS)])
    del k
    sib = sibling_swap("rs_sibling_result", mine)
    return jnp.concatenate([jnp.where(c == 0, mine, sib), jnp.where(c == 0, sib, mine)], axis=0)


def ada_mod_shard(cond_all, w_ada_shard, b_ada_shard):
    def body(c_ref, w_ref, b_ref, o_ref):
        o_ref[...] = _dot(_silu(c_ref[...]).astype(BF16), w_ref[...].astype(BF16)) + b_ref[...]

    return pl.pallas_call(body, name="ada_mod_shard", out_shape=jax.ShapeDtypeStruct((cond_all.shape[0], w_ada_shard.shape[1]), F32),
                          compiler_params=_cp())(cond_all, w_ada_shard, b_ada_shard)


def ada_bwd_shard(cond_all, dmod_all_shard, dmod_all, w_ada_shard, row_is_cctx):
    def body(c_ref, ds_ref, da_ref, w_ref, sel_ref, gw_ref, gb_ref, part_ref):
        sc = _silu(c_ref[...]).astype(BF16)
        gw_ref[...] = _dot_tn(sc, ds_ref[...].astype(BF16))
        gb_ref[...] = _colsum(da_ref[...])
        dc_tot = jnp.broadcast_to(_colsum(ds_ref[...] * sel_ref[...]), (8, ds_ref.shape[1]))
        part_ref[...] = _dot_nt(dc_tot.astype(BF16), w_ref[...].astype(BF16))

    n = cond_all.shape[0]
    return pl.pallas_call(
        body, name="ada_bwd_shard",
        out_shape=[jax.ShapeDtypeStruct(w_ada_shard.shape, F32), jax.ShapeDtypeStruct((1, dmod_all.shape[1]), F32),
                   jax.ShapeDtypeStruct((8, D), F32)],
        compiler_params=_cp())(cond_all, dmod_all_shard, dmod_all, w_ada_shard, row_is_cctx)


def sum_devices(name, gathered):
    def body(g_ref, o_ref):
        acc = g_ref[0]
        for d in range(1, N_DEV):
            acc = acc + g_ref[d]
        o_ref[...] = acc

    return pl.pallas_call(body, name=name, out_shape=jax.ShapeDtypeStruct(gathered.shape[1:], F32), compiler_params=_cp())(gathered)


def cctx_finish(gathered, c_ctx_row):
    def body(g_ref, c_ref, o_ref):
        acc = g_ref[0, 0:1, :]
        for k in range(1, N_CHIPS):
            acc = acc + g_ref[2 * k, 0:1, :]
        o_ref[...] = acc * _dsilu(c_ref[...])

    return pl.pallas_call(body, name="cctx_finish", out_shape=jax.ShapeDtypeStruct((1, D), F32), compiler_params=_cp())(gathered, c_ctx_row)


def _pack(parts, rows):
    flat = []
    for p in parts:
        p = p.reshape(-1)
        pad = (-p.shape[0]) % LANES
        flat.append(jnp.pad(p, (0, pad)) if pad else p)
    v = jnp.concatenate(flat)
    return jnp.pad(v, (0, rows * LANES - v.shape[0])).reshape(rows, LANES)


def _unpack(v, sizes):
    flat = v.reshape(-1)
    out, off = [], 0
    for n in sizes:
        out.append(flat[off:off + n])
        off += n + (-n) % LANES
    return out


W_SHARD_ROWS = 3456
SEG_ROWS = (0, 2320, 2576, 3088, 3344, 3408)


def kernel(x, c, ctx, c_ctx, w_ada, b_ada, norm_pre, norm_post, w_in, b_merge, pool_w, pool_scale, conv_w, conv_b, dt_bias, a_log, d_skip, ssd_norm, w_proj_pool, w_proj_ssd, w_out, loss_target, m_c_ctx, m_w_ada, m_b_ada, m_norm_pre, m_norm_post, m_w_in, m_b_merge, m_pool_w, m_pool_scale, m_conv_w, m_conv_b, m_dt_bias, m_a_log, m_d_skip, m_ssd_norm, m_w_proj_pool, m_w_proj_ssd, m_w_out, v_c_ctx, v_w_ada, v_b_ada, v_norm_pre, v_norm_post, v_w_in, v_b_merge, v_pool_w, v_pool_scale, v_conv_w, v_conv_b, v_dt_bias, v_a_log, v_d_skip, v_ssd_norm, v_w_proj_pool, v_w_proj_ssd, v_w_out):
    Bn, L, _ = x.shape
    Lc = ctx.shape[1]
    T, Tc = Bn * L, Bn * Lc
    assert Bn == 2
    ix, iy, ic = lax.axis_index("x"), lax.axis_index("y"), lax.axis_index("c")
    me = 4 * ix + 2 * iy + ic
    chip = 2 * ix + iy
    ada_cols = w_ada.shape[2]
    cw_cols = conv_w.shape[2]

    cond_own = jnp.pad(c, ((0, 8 - Bn), (0, 0))) + jnp.pad(c_ctx[None, :], ((Bn, 7 - Bn), (0, 0)))
    convw_own = jnp.pad(conv_w[0], ((0, 4), (0, D - cw_cols)))
    g1 = all_gather_small("gather_cond", jnp.concatenate([cond_own, convw_own], axis=0))
    cond_all = g1[:, 0:8].reshape(8 * N_DEV, D)
    conv_w_full = jnp.concatenate([g1[2 * k, 8:12, 0:cw_cols] for k in range(N_CHIPS)], axis=1)
    b_ada_shard = lax.dynamic_slice(b_ada, (0, chip * ada_cols), (1, ada_cols))
    g2 = all_gather_small("gather_mod", ada_mod_shard(cond_all, w_ada[0], b_ada_shard))
    mod_full = jnp.concatenate([g2[2 * k] for k in range(N_CHIPS)], axis=1)
    own = lax.dynamic_slice(mod_full, (8 * me, 0), (8, 3 * D))
    shift, scale, gate = (own[0:Bn, i * D:(i + 1) * D][:, None, :] for i in range(3))
    shift_c, scale_c = (jnp.broadcast_to(own[Bn:Bn + 1, i * D:(i + 1) * D][None], (Bn, 1, D)) for i in range(2))

    shard = jnp.concatenate([w_in[0].T, w_proj_pool[0], w_proj_ssd[0], w_out[0], pool_w[0].reshape(64, D),
                             jnp.zeros((W_SHARD_ROWS - SEG_ROWS[-1], D), F32)], axis=0).astype(BF16)
    gw = all_gather_chips("gather_weights", shard)
    w_inT = gw[:, SEG_ROWS[0]:SEG_ROWS[1]].reshape(IN_COLS, D)
    w_pp = gw[:, SEG_ROWS[1]:SEG_ROWS[2]].reshape(D, D)
    w_ps = gw[:, SEG_ROWS[2]:SEG_ROWS[3]].reshape(D_INNER, D)
    w_o = gw[:, SEG_ROWS[3]:SEG_ROWS[4]].reshape(D, D)
    pool_full = gw[:, SEG_ROWS[4]:SEG_ROWS[5]].reshape(N_CHIPS, 4, 64, POOL_GROUP).transpose(1, 0, 2, 3).reshape(D, POOL_GROUP)
    w_dt = jnp.pad(w_inT[9216:IN_COLS], ((0, LANES - 64), (0, 0)))
    seg_lo = (0, 256, 512, 768, 1024, 2048, 4096, 6144, 8192, 8704)
    seg_hi = (256, 512, 768, 1024, 2048, 4096, 6144, 8192, 8704, 9216)
    w_seg = [w_inT[lo:hi] for lo, hi in zip(seg_lo, seg_hi)] + [w_dt]

    hx = prenorm_fwd("prenorm_x", x, scale, shift, norm_pre)
    hc = prenorm_fwd("prenorm_ctx", ctx, scale_c, shift_c, norm_pre)
    hx2, hc2 = hx.reshape(T, D), hc.reshape(Tc, D)
    v = mm_nt("proj_v", hx2, w_inT[0:1024], F32).reshape(Bn, L, D)
    zp = mm_nt("proj_zpool", hx2, w_inT[1024:2048], F32).reshape(Bn, L, D)
    zs = mm_nt("proj_zssd", hx2, w_inT[2048:4096], F32).reshape(Bn, L, D_INNER)
    gp = mm_nt("proj_gate", hx2, w_inT[4096:6144], F32).reshape(Bn, L, 2 * D)
    xbc_raw = mm_nt("proj_xbc", hx2, w_inT[6144:9216], F32).reshape(Bn, L, CONV_DIM)
    dt_raw = mm_nt("proj_dt", hx2, w_dt, F32)
    xbc_raw_c = mm_nt("proj_xbc_ctx", hc2, w_inT[6144:9216], F32).reshape(Bn, Lc, CONV_DIM)
    dt_raw_c = mm_nt("proj_dt_ctx", hc2, w_dt, F32)
    dtT = dt_raw[:, :64].reshape(Bn, L, 64).transpose(0, 2, 1)
    dtT_c = dt_raw_c[:, :64].reshape(Bn, Lc, 64).transpose(0, 2, 1)
    bias3 = dt_bias.reshape(2 * N_BC, HPG, 1)
    alog3 = a_log.reshape(2 * N_BC, HPG, 1)

    xbc = conv_fwd("conv_x", xbc_raw, conv_w_full, conv_b)
    xbc_c = conv_fwd("conv_ctx", xbc_raw_c, conv_w_full, conv_b)
    zero_state = jnp.zeros((Bn, N_BC, D_STATE, GW), F32)
    tables = ssd_tables()
    ys, hs_x, hs_c, hf_x, hf_c = [], [], [], [], []
    for d in range(2):
        hsc, hfc = ssd_fwd3(f"ssd_fwd_ctx{d}", dtT_c, bias3, alog3, xbc_c, zero_state, d, False)
        y, hsx, hfx = ssd_fwd3(f"ssd_fwd_x{d}", dtT, bias3, alog3, xbc, hfc, d, True)
        ys.append(y)
        hs_x.append(hsx)
        hs_c.append(hsc)
        hf_x.append(hfx)
        hf_c.append(hfc)

    dgs = [pool_diff(f"pool_diff{g}", v, g * POOL_GROUP, g, False) for g in range(4)]
    y_pool = pool_mix_fwd("pool_mix", dgs, zp, pool_full, pool_scale)
    dskip_lanes = jnp.repeat(d_skip[0], HEAD_DIM)[None, :]
    y_ssd = gated_norm_fwd("gated_norm", ys[0], ys[1], (xbc, D_INNER), zs, dskip_lanes, ssd_norm)
    merged, p1, p2, dout, g_res, dgate, g_norm_post, loss_part = merge_fwd(
        "merge_fwd", y_pool, y_ssd, gp, x, loss_target, gate, b_merge, norm_post, w_pp, w_ps, w_o)

    dp1, dp2, dgp, dyp, dys, g_b_merge = merge_bwd("merge_bwd", dout, gp, p1, p2, b_merge, w_pp, w_ps, w_o)
    gw_o = mm_tn("gw_out", merged.reshape(T, D), dout.reshape(T, D))
    gw_pp = mm_tn("gw_proj_pool", y_pool.reshape(T, D), dp1.reshape(T, D))
    gw_ps = mm_tn("gw_proj_ssd", y_ssd.reshape(T, D_INNER), dp2.reshape(T, D))

    *dds, dzp, g_pool, g_pool_scale = pool_mix_bwd("pool_mix_bwd", dgs, zp, dyp, pool_full, pool_scale)
    dvs = [pool_diff(f"pool_diff_t{g}", dds[g], 0, g, True) for g in range(4)]

    head_sel = (jnp.arange(D_INNER)[:, None] // HEAD_DIM == jnp.arange(LANES)[None, :]).astype(BF16)
    dy, dzs, g_ssd_norm, g_dskip = gated_norm_bwd(
        "gated_norm_bwd", ys[0], ys[1], (xbc, D_INNER), zs, dys, dskip_lanes, ssd_norm, head_sel)

    dxs, dbm, dcm, ddt, dxs_c, dbm_c, ddt_c = [], [], [], [], [], [], []
    g_bias = jnp.zeros((2, N_BC, HPG, 1), F32)
    g_alog = jnp.zeros((2, N_BC, HPG, 1), F32)
    for d in range(2):
        a, b_, c_, t_, gb, ga, dh0 = ssd_bwd3(f"ssd_bwd_x{d}", dtT, bias3, alog3, xbc, hs_x[d], dy, zero_state, d)
        dxs.append(a), dbm.append(b_), dcm.append(c_), ddt.append(t_)
        ac, bc, _, tc, gbc, gac, _ = ssd_bwd3(f"ssd_bwd_ctx{d}", dtT_c, bias3, alog3, xbc_c, hs_c[d], None, dh0, d)
        dxs_c.append(ac), dbm_c.append(bc), ddt_c.append(tc)
        g_bias = g_bias.at[d].set(jnp.sum(gb, axis=0) + jnp.sum(gbc, axis=0))
        g_alog = g_alog.at[d].set(jnp.sum(ga, axis=0) + jnp.sum(gac, axis=0))

    dxr_xs, gcw_xs, gcb_xs = conv_bwd("conv_bwd_xs", xbc_raw, dxs, conv_w_full, conv_b, 0, D_INNER, scaled=(dy, dskip_lanes))
    dxr_b, gcw_b, gcb_b = conv_bwd("conv_bwd_b", xbc_raw, dbm, conv_w_full, conv_b, D_INNER, N_BC * D_STATE)
    dxr_c, gcw_c, gcb_c = conv_bwd("conv_bwd_c", xbc_raw, dcm, conv_w_full, conv_b, D_INNER + N_BC * D_STATE, N_BC * D_STATE)
    dxr_xs_c, gcw_xs_c, gcb_xs_c = conv_bwd("conv_bwd_xs_ctx", xbc_raw_c, dxs_c, conv_w_full, conv_b, 0, D_INNER)
    dxr_b_c, gcw_b_c, gcb_b_c = conv_bwd("conv_bwd_b_ctx", xbc_raw_c, dbm_c, conv_w_full, conv_b, D_INNER, N_BC * D_STATE)
    g_conv_w = jnp.concatenate([gcw_xs + gcw_xs_c, gcw_b + gcw_b_c, gcw_c], axis=1)
    g_conv_b = jnp.concatenate([gcb_xs + gcb_xs_c, gcb_b + gcb_b_c, gcb_c], axis=1)

    def dt_cols(parts, n_tok):
        t = jnp.concatenate(parts, axis=1).transpose(0, 2, 1).reshape(n_tok, 2 * N_HEADS)
        return jnp.pad(t, ((0, 0), (0, LANES - 2 * N_HEADS))).astype(BF16)

    ddt2, ddt2_c = dt_cols(ddt, T), dt_cols(ddt_c, Tc)
    segs = ([dv.reshape(T, POOL_GROUP) for dv in dvs]
            + [dzp.reshape(T, D), dzs.reshape(T, D_INNER), dgp.reshape(T, 2 * D), dxr_xs.reshape(T, D_INNER),
               dxr_b.reshape(T, N_BC * D_STATE), dxr_c.reshape(T, N_BC * D_STATE), ddt2])
    d_hx = mm_nn_multi("d_hx", list(zip(segs, w_seg)), F32, tm=1024, tk=256).reshape(Bn, L, D)
    segs_c = {7: dxr_xs_c.reshape(Tc, D_INNER), 8: dxr_b_c.reshape(Tc, N_BC * D_STATE), 10: ddt2_c}
    d_hc = mm_nn_multi("d_hc", [(segs_c[i], w_seg[i]) for i in (7, 8, 10)], F32).reshape(Bn, Lc, D)
    gw_rows = []
    for i, seg in enumerate(segs):
        init = mm_tn(f"gw_in_ctx{i}", segs_c[i], hc2) if i in segs_c else None
        gw_rows.append(mm_tn(f"gw_in{i}", seg, hx2, init=init))
    gw_rows[-1] = gw_rows[-1][0:2 * N_HEADS]
    gw_inT = jnp.concatenate(gw_rows, axis=0)

    grad_x, dscale, dshift, g_npre_x = prenorm_bwd("prenorm_bwd_x", x, d_hx, scale, norm_pre, g_res=g_res)
    _, dscale_c, dshift_c, g_npre_c = prenorm_bwd("prenorm_bwd_ctx", ctx, d_hc, scale_c, norm_pre)

    dmod_x = jnp.concatenate([dshift[:, 0], dscale[:, 0], dgate[:, 0]], axis=1)
    dmod_c = jnp.concatenate([jnp.sum(dshift_c[:, 0], axis=0, keepdims=True), jnp.sum(dscale_c[:, 0], axis=0, keepdims=True),
                              jnp.zeros((1, D), F32)], axis=1)
    dmod_own = jnp.pad(dmod_x, ((0, 8 - Bn), (0, 0))) + jnp.pad(dmod_c, ((Bn, 7 - Bn), (0, 0)))
    dmod_all = all_gather_small("gather_dmod", dmod_own).reshape(8 * N_DEV, 3 * D)
    row_is_cctx = (jnp.arange(8 * N_DEV) % 8 == Bn).astype(F32)[:, None]
    g_w_ada, g_b_ada, cpart = ada_bwd_shard(
        cond_all, lax.dynamic_slice(dmod_all, (0, chip * ada_cols), (8 * N_DEV, ada_cols)), dmod_all, w_ada[0], row_is_cctx)
    g_c_ctx = cctx_finish(all_gather_small("gather_cctx", cpart), c_ctx[None, :])

    small_sizes = (D, D, 2 * D, D, CONV_DIM, 2 * N_HEADS, 2 * N_HEADS, N_HEADS, D_INNER, 4 * CONV_DIM, 1)
    pk = _pack([g_npre_x + g_npre_c, g_norm_post, g_b_merge, g_pool_scale, g_conv_b, g_bias, g_alog, g_dskip[0, 0:N_HEADS],
                g_ssd_norm, g_conv_w, loss_part[0, 0:1]], 184)
    small = sum_devices("sum_small", all_gather_small("gather_small", pk))
    (g_norm_pre, g_norm_post_t, g_b_merge_t, g_pool_scale_t, g_conv_b_t, g_dt_bias, g_a_log, g_d_skip, g_ssd_norm_t,
     g_conv_w_t, loss) = _unpack(small, small_sizes)
    g_conv_w_shard = lax.dynamic_slice(g_conv_w_t.reshape(4, CONV_DIM), (0, chip * cw_cols), (4, cw_cols))

    pool_slab = g_pool.reshape(4, N_CHIPS, 64, POOL_GROUP).transpose(1, 0, 2, 3).reshape(N_CHIPS, 64, D)
    slabs = jnp.concatenate([gw_inT.reshape(N_CHIPS, 2320, D), gw_pp.reshape(N_CHIPS, 256, D), gw_ps.reshape(N_CHIPS, 512, D),
                             gw_o.reshape(N_CHIPS, 256, D), pool_slab, jnp.zeros((N_CHIPS, W_SHARD_ROWS - SEG_ROWS[-1], D), F32)], axis=1)
    gsh = reduce_scatter_chips(slabs)
    g_w_in = gsh[SEG_ROWS[0]:SEG_ROWS[1]].T
    g_w_pp, g_w_ps, g_w_o = (gsh[SEG_ROWS[i]:SEG_ROWS[i + 1]] for i in (1, 2, 3))
    g_pool_w = gsh[SEG_ROWS[4]:SEG_ROWS[5]].reshape(256, POOL_GROUP)

    grads = {
        "c_ctx": g_c_ctx.reshape(c_ctx.shape), "w_ada": g_w_ada[None], "b_ada": g_b_ada, "norm_pre": g_norm_pre[None],
        "norm_post": g_norm_post_t[None], "w_in": g_w_in[None], "b_merge": g_b_merge_t[None],
        "pool_w": g_pool_w.reshape(pool_w.shape), "pool_scale": g_pool_scale_t[None], "conv_w": g_conv_w_shard[None],
        "conv_b": g_conv_b_t[None], "dt_bias": g_dt_bias.reshape(dt_bias.shape), "a_log": g_a_log.reshape(a_log.shape),
        "d_skip": g_d_skip[None], "ssd_norm": g_ssd_norm_t[None], "w_proj_pool": g_w_pp[None], "w_proj_ssd": g_w_ps[None],
        "w_out": g_w_o[None]}
    weights = dict(c_ctx=c_ctx, w_ada=w_ada, b_ada=b_ada, norm_pre=norm_pre, norm_post=norm_post, w_in=w_in, b_merge=b_merge,
                   pool_w=pool_w, pool_scale=pool_scale, conv_w=conv_w, conv_b=conv_b, dt_bias=dt_bias, a_log=a_log,
                   d_skip=d_skip, ssd_norm=ssd_norm, w_proj_pool=w_proj_pool, w_proj_ssd=w_proj_ssd, w_out=w_out)
    m_in = dict(c_ctx=m_c_ctx, w_ada=m_w_ada, b_ada=m_b_ada, norm_pre=m_norm_pre, norm_post=m_norm_post, w_in=m_w_in,
                b_merge=m_b_merge, pool_w=m_pool_w, pool_scale=m_pool_scale, conv_w=m_conv_w, conv_b=m_conv_b,
                dt_bias=m_dt_bias, a_log=m_a_log, d_skip=m_d_skip, ssd_norm=m_ssd_norm, w_proj_pool=m_w_proj_pool,
                w_proj_ssd=m_w_proj_ssd, w_out=m_w_out)
    v_in = dict(c_ctx=v_c_ctx, w_ada=v_w_ada, b_ada=v_b_ada, norm_pre=v_norm_pre, norm_post=v_norm_post, w_in=v_w_in,
                b_merge=v_b_merge, pool_w=v_pool_w, pool_scale=v_pool_scale, conv_w=v_conv_w, conv_b=v_conv_b,
                dt_bias=v_dt_bias, a_log=v_a_log, d_skip=v_d_skip, ssd_norm=v_ssd_norm, w_proj_pool=v_w_proj_pool,
                w_proj_ssd=v_w_proj_ssd, w_out=v_w_out)
    names = list(weights)
    big = ("w_ada", "w_in", "pool_w", "w_proj_pool", "w_proj_ssd", "w_out")
    small_names = [n for n in names if n not in big]
    delta, new_m, new_v = {}, {}, {}
    for n in big:
        shape2 = (-1, weights[n].shape[-1])
        d_, m_, v_ = adamw(f"adamw_{n}", weights[n].reshape(shape2), grads[n].reshape(shape2), m_in[n].reshape(shape2),
                           v_in[n].reshape(shape2), tr=128)
        delta[n], new_m[n], new_v[n] = (t.reshape(weights[n].shape) for t in (d_, m_, v_))
    sizes = [weights[n].size for n in small_names]
    packed = [_pack([src[n] for n in small_names], 144) for src in (weights, grads, m_in, v_in)]
    outs = adamw("adamw_small", *packed, tr=144)
    for res, store in zip(outs, (delta, new_m, new_v)):
        for n, piece in zip(small_names, _unpack(res, sizes)):
            store[n] = piece.reshape(weights[n].shape)

    return (loss.reshape(()), grad_x, *[grads[n] for n in names], *[delta[n] for n in names],
            *[new_m[n] for n in names], *[new_v[n] for n in names])
```

```python
import jax
import jax.numpy as jnp
from jax import lax
from jax.experimental import pallas as pl
from jax.experimental.pallas import tpu as pltpu

F32 = jnp.float32
BF16 = jnp.bfloat16
MESH = pl.DeviceIdType.MESH

D = 1024
GRID_W = 64
NORM_EPS = 1e-6
POOL_WINDOWS = (2, 4, 8, 16)
POOL_GROUP = 256
D_INNER = 2048
HEAD_DIM = 64
N_HEADS = 32
D_STATE = 128
N_BC = 4
HPG = N_HEADS // N_BC
GW = HPG * HEAD_DIM
CONV_DIM = 3072
CHUNK = 128
OFF_XBC = 6144
IN_COLS = 9280
N_CHIPS = 4
N_DEV = 8

ADAM_LR = 0.001
ADAM_B1 = 0.9
ADAM_B2 = 0.999
ADAM_EPS = 1e-08
ADAM_WD = 0.01
ADAM_STEP = 10

V7X_VMEM_BYTES = 64 * 1024 * 1024
VMEM_LIMIT = V7X_VMEM_BYTES * 3 // 4
LANES = 128


def _cp(sem=None):
    return pltpu.CompilerParams(dimension_semantics=sem, vmem_limit_bytes=VMEM_LIMIT)


def _dot(a, b):
    return jnp.dot(a, b, preferred_element_type=F32)


def _dot_nt(a, b):
    return lax.dot_general(a, b, (((1,), (1,)), ((), ())), preferred_element_type=F32)


def _dot_tn(a, b):
    return lax.dot_general(a, b, (((0,), (0,)), ((), ())), preferred_element_type=F32)


def _split3(x):
    hi = x.astype(BF16)
    r1 = x - hi.astype(F32)
    mid = r1.astype(BF16)
    lo = (r1 - mid.astype(F32)).astype(BF16)
    return hi, mid, lo


def _sigmoid(x):
    return jax.nn.sigmoid(x)


def _silu(x):
    return x * _sigmoid(x)


def _dsilu(x):
    s = _sigmoid(x)
    return s * (1.0 + x * (1.0 - s))


def _softplus(x):
    return jnp.maximum(x, 0.0) + jnp.log(1.0 + jnp.exp(-jnp.abs(x)))


def mm_nt(name, a, b, out_dtype, tm=1024, tn=512):
    M, K = a.shape
    N = b.shape[0]
    tm, tn = min(tm, M), min(tn, N)
    assert M % tm == 0 and N % tn == 0, (M, N, tm, tn)

    def body(a_ref, b_ref, o_ref):
        o_ref[...] = _dot_nt(a_ref[...], b_ref[...]).astype(o_ref.dtype)

    return pl.pallas_call(
        body, name=name, out_shape=jax.ShapeDtypeStruct((M, N), out_dtype), grid=(M // tm, N // tn),
        in_specs=[pl.BlockSpec((tm, K), lambda i, j: (i, 0)), pl.BlockSpec((tn, K), lambda i, j: (j, 0))],
        out_specs=pl.BlockSpec((tm, tn), lambda i, j: (i, j)),
        compiler_params=_cp(("parallel", "arbitrary")))(a, b)


def mm_tn(name, a, b, init=None, tm=1024, tn=1024, tk=512):
    T, M = a.shape
    N = b.shape[1]
    tm, tn, tk = min(tm, M), min(tn, N), min(tk, T)
    assert M % tm == 0 and N % tn == 0 and T % tk == 0, (M, N, T)
    has_init = init is not None

    def body(*refs):
        if has_init:
            a_ref, b_ref, i_ref, o_ref = refs
        else:
            a_ref, b_ref, o_ref = refs
        k = pl.program_id(2)

        @pl.when(k == 0)
        def _():
            o_ref[...] = i_ref[...] if has_init else jnp.zeros(o_ref.shape, F32)

        o_ref[...] += _dot_tn(a_ref[...], b_ref[...])

    in_specs = [pl.BlockSpec((tk, tm), lambda i, j, k: (k, i)), pl.BlockSpec((tk, tn), lambda i, j, k: (k, j))]
    args = [a, b]
    if has_init:
        in_specs.append(pl.BlockSpec((tm, tn), lambda i, j, k: (i, j)))
        args.append(init)
    return pl.pallas_call(
        body, name=name, out_shape=jax.ShapeDtypeStruct((M, N), F32), grid=(M // tm, N // tn, T // tk),
        in_specs=in_specs, out_specs=pl.BlockSpec((tm, tn), lambda i, j, k: (i, j)),
        compiler_params=_cp(("parallel", "parallel", "arbitrary")))(*args)


def mm_nn_multi(name, pairs, out_dtype, tm=512, tk=512, exchange=None):
    M = pairs[0][0].shape[0]
    N = pairs[0][1].shape[1]
    tm = min(tm, M)
    assert M % tm == 0
    plan = []
    step = 0
    for a, b in pairs:
        K = a.shape[1]
        t = min(tk, K)
        assert K % t == 0 and b.shape == (K, N)
        plan.append((t, step, K // t))
        step += K // t
    nsteps = step
    npairs = len(pairs)

    n_i = M // tm
    has_x = exchange is not None

    def body(*refs):
        if has_x:
            p_ref, o_ref, land_ref, acc, send_sems, recv_sems = refs[2 * npairs:]
        else:
            o_ref, acc = refs[2 * npairs:]
        i, k = pl.program_id(0), pl.program_id(1)

        if has_x:
            x, y, c = _me()
            me_chip = 2 * x + y
            chips = [(1 - x, y), (x, 1 - y), (1 - x, 1 - y)]

            def copy(j, src_chip, dst_chip, to):
                return pltpu.make_async_remote_copy(
                    src_ref=p_ref.at[src_chip], dst_ref=land_ref.at[dst_chip], send_sem=send_sems.at[j], recv_sem=recv_sems.at[j],
                    device_id=(*to, c), device_id_type=MESH)

            @pl.when((i == 0) & (k == 0))
            def _():
                for j, chip in enumerate(chips):
                    copy(j, 2 * chip[0] + chip[1], me_chip, chip).start()

        @pl.when(k == 0)
        def _():
            acc[...] = jnp.zeros(acc.shape, F32)

        for p, (_, first, n) in enumerate(plan):
            @pl.when((k >= first) & (k < first + n))
            def _(p=p):
                acc[...] += _dot(refs[2 * p][...], refs[2 * p + 1][...])

        @pl.when(k == nsteps - 1)
        def _():
            o_ref[...] = acc[...].astype(o_ref.dtype)

        if has_x:
            @pl.when((i == n_i - 1) & (k == nsteps - 1))
            def _():
                for j, chip in enumerate(chips):
                    copy(j, me_chip, 2 * chip[0] + chip[1], chip).wait_recv()
                for j, chip in enumerate(chips):
                    copy(j, 2 * chip[0] + chip[1], me_chip, chip).wait_send()

    in_specs, args = [], []
    for (a, b), (t, first, n) in zip(pairs, plan):
        in_specs.append(pl.BlockSpec((tm, t), lambda i, k, first=first, n=n: (i, jnp.clip(k - first, 0, n - 1))))
        in_specs.append(pl.BlockSpec((t, N), lambda i, k, first=first, n=n: (jnp.clip(k - first, 0, n - 1), 0)))
        args += [a, b]
    out_shape = jax.ShapeDtypeStruct((M, N), out_dtype)
    out_specs = pl.BlockSpec((tm, N), lambda i, k: (i, 0))
    scratch = [pltpu.VMEM((tm, N), F32)]
    if has_x:
        in_specs.append(pl.BlockSpec(memory_space=pl.ANY))
        args.append(exchange)
        out_shape = [out_shape, jax.ShapeDtypeStruct(exchange.shape, exchange.dtype)]
        out_specs = [out_specs, pl.BlockSpec(memory_space=pl.ANY)]
        scratch += [pltpu.SemaphoreType.DMA((3,)), pltpu.SemaphoreType.DMA((3,))]
    res = pl.pallas_call(
        body, name=name, out_shape=out_shape, grid=(n_i, nsteps), in_specs=in_specs, out_specs=out_specs,
        scratch_shapes=scratch, compiler_params=_cp(("arbitrary", "arbitrary")))(*args)
    if not has_x:
        return res
    out, landed = res
    chip = 2 * lax.axis_index("x") + lax.axis_index("y")
    own = lax.dynamic_index_in_dim(exchange, chip, 0, keepdims=True)
    return out, lax.dynamic_update_slice_in_dim(landed, own, chip, 0)


def tok_call(name, body, tiled, perb, glob, out_tiled, out_perb, out_glob, tm=256):
    widths = [t[1] if isinstance(t, tuple) else t.shape[2] for t in tiled]
    tiled = [t[0] if isinstance(t, tuple) else t for t in tiled]
    Bn, L = tiled[0].shape[:2]
    tm = min(tm, L)
    assert L % tm == 0
    n_t, n_p, n_g = len(tiled), len(perb), len(glob)
    o_t, o_p, o_g = len(out_tiled), len(out_perb), len(out_glob)
    n_in = n_t + n_p + n_g

    def kern(*refs):
        ins, outs = refs[:n_in], refs[n_in:]
        b, j = pl.program_id(0), pl.program_id(1)
        vals = [r[0] for r in ins[:n_t + n_p]] + [r[...] for r in ins[n_t + n_p:]]
        res = body(*vals)
        if not isinstance(res, (tuple, list)):
            res = (res,)
        assert len(res) == o_t + o_p + o_g, (name, len(res))
        for r, v in zip(outs[:o_t], res[:o_t]):
            r[0] = v.astype(r.dtype)

        def accum(r, v, first, lead):
            @pl.when(first)
            def _():
                r[...] = jnp.zeros(r.shape, F32)
            if lead:
                r[0] += v
            else:
                r[...] += v

        for r, v in zip(outs[o_t:o_t + o_p], res[o_t:o_t + o_p]):
            accum(r, v, j == 0, True)
        for r, v in zip(outs[o_t + o_p:], res[o_t + o_p:]):
            accum(r, v, (j == 0) & (b == 0), False)

    in_specs = ([pl.BlockSpec((1, tm, w), lambda b, j: (b, j, 0)) for w in widths]
                + [pl.BlockSpec((1, 1, a.shape[2]), lambda b, j: (b, 0, 0)) for a in perb]
                + [pl.BlockSpec(a.shape, lambda b, j: (0, 0), pipeline_mode=pl.Buffered(1)) for a in glob])
    out_shape = ([jax.ShapeDtypeStruct((Bn, L, w), dt) for w, dt in out_tiled]
                 + [jax.ShapeDtypeStruct((Bn, 1, w), F32) for w in out_perb]
                 + [jax.ShapeDtypeStruct(s, F32) for s in out_glob])
    out_specs = ([pl.BlockSpec((1, tm, w), lambda b, j: (b, j, 0)) for w, _ in out_tiled]
                 + [pl.BlockSpec((1, 1, w), lambda b, j: (b, 0, 0)) for w in out_perb]
                 + [pl.BlockSpec(s, lambda b, j: (0, 0)) for s in out_glob])
    return pl.pallas_call(
        kern, name=name, out_shape=out_shape, grid=(Bn, L // tm), in_specs=in_specs, out_specs=out_specs,
        compiler_params=_cp(("arbitrary", "arbitrary")))(*tiled, *perb, *glob)


def slab_call(name, body, slabs, colparams, out_slabs, out_colred, wc=LANES):
    Bn, L = slabs[0][0].shape[:2]
    w_out = out_slabs[0][0]
    assert w_out % wc == 0 and all(off % wc == 0 for _, off in slabs + colparams)
    n_col = w_out // wc
    n_s, n_c = len(slabs), len(colparams)
    o_s = len(out_slabs)

    def kern(*refs):
        ins, outs = refs[:n_s + n_c], refs[n_s + n_c:]
        b = pl.program_id(1)
        vals = [r[0] for r in ins[:n_s]] + [r[...] for r in ins[n_s:]]
        res = body(*vals)
        if not isinstance(res, (tuple, list)):
            res = (res,)
        assert len(res) == o_s + len(out_colred), name
        for r, v in zip(outs[:o_s], res[:o_s]):
            r[0] = v.astype(r.dtype)

        def accum(r, v):
            @pl.when(b == 0)
            def _():
                r[...] = jnp.zeros(r.shape, F32)
            r[...] += v

        for r, v in zip(outs[o_s:], res[o_s:]):
            accum(r, v)

    in_specs = ([pl.BlockSpec((1, L, wc), lambda j, b, o=off // wc: (b, 0, o + j)) for _, off in slabs]
                + [pl.BlockSpec((a.shape[0], wc), lambda j, b, o=off // wc: (0, o + j)) for a, off in colparams])
    out_shape = ([jax.ShapeDtypeStruct((Bn, L, w), dt) for w, dt in out_slabs]
                 + [jax.ShapeDtypeStruct((r, w_out), F32) for r in out_colred])
    out_specs = ([pl.BlockSpec((1, L, wc), lambda j, b: (b, 0, j)) for _ in out_slabs]
                 + [pl.BlockSpec((r, wc), lambda j, b: (0, j)) for r in out_colred])
    return pl.pallas_call(
        kern, name=name, out_shape=out_shape, grid=(n_col, Bn), in_specs=in_specs, out_specs=out_specs,
        compiler_params=_cp(("arbitrary", "arbitrary")))(*[a for a, _ in slabs], *[a for a, _ in colparams])


def _rms_r(x):
    return lax.rsqrt(jnp.mean(x * x, axis=-1, keepdims=True) + NORM_EPS)


def _rms_bwd(dxh, x, r):
    return r * (dxh - x * (r * r) * jnp.mean(dxh * x, axis=-1, keepdims=True))


def _colsum(v):
    return jnp.sum(v, axis=0, keepdims=True)


def _stack_rows(rows):
    n, w = len(rows), rows[0].shape[1]
    sub = lax.broadcasted_iota(jnp.int32, (n, w), 0)
    acc = jnp.zeros((n, w), F32)
    for r, row in enumerate(rows):
        acc = acc + jnp.where(sub == r, jnp.broadcast_to(row, (n, w)), 0.0)
    return acc


def prenorm_fwd(name, x, scale, shift, w_pre):
    def body(x, scale, shift, w):
        n = x * _rms_r(x) * w
        return n * (1.0 + scale) + shift

    return tok_call(name, body, [x], [scale, shift], [w_pre], [(D, BF16)], [], [])[0]


def prenorm_bwd(name, x, dhx, scale, w_pre, g_res=None):
    has_res = g_res is not None

    def body(*v):
        if has_res:
            x, dhx, g, scale, w = v
        else:
            x, dhx, scale, w = v
        r = _rms_r(x)
        xr = x * r
        n = xr * w
        dn = dhx * (1.0 + scale)
        dx = _rms_bwd(dn * w, x, r)
        if has_res:
            dx = dx + g
        return dx, _colsum(dhx * n), _colsum(dhx), _colsum(dn * xr)

    tiled = [x, dhx] + ([g_res] if has_res else [])
    return tok_call(name, body, tiled, [scale], [w_pre], [(D, F32)], [D, D], [(1, D)])


def _shift_rows(x, o, tok, L):
    if o == 0:
        return x
    rolled = pltpu.roll(x, (-o) % L, 0)
    return jnp.where((tok + o >= 0) & (tok + o < L), rolled, 0.0)


def conv_fwd(name, xbc_raw, conv_w, conv_b):
    L = xbc_raw.shape[1]

    def body(x, w, b):
        tok = lax.broadcasted_iota(jnp.int32, x.shape, 0)
        pre = b
        for k in range(4):
            pre = pre + _shift_rows(x, k - 2, tok, L) * w[k:k + 1]
        return _silu(pre)

    return slab_call(name, body, [(xbc_raw, 0)], [(conv_w, 0), (conv_b, 0)], [(CONV_DIM, F32)], [])[0]


def conv_bwd(name, xbc_raw, dparts, conv_w, conv_b, col0, width, scaled=None):
    L = xbc_raw.shape[1]
    n_d = len(dparts) + (1 if scaled is not None else 0)

    def body(*v):
        x, ds, w, b = v[0], v[1:1 + n_d], v[1 + n_d], v[2 + n_d]
        tok = lax.broadcasted_iota(jnp.int32, x.shape, 0)
        taps = [_shift_rows(x, k - 2, tok, L) for k in range(4)]
        pre = b
        for k in range(4):
            pre = pre + taps[k] * w[k:k + 1]
        dy = ds[0] * v[3 + n_d] if scaled is not None else ds[0]
        for extra in ds[1:]:
            dy = dy + extra
        dpre = dy * _dsilu(pre)
        dx = jnp.zeros_like(x)
        for k in range(4):
            dx = dx + _shift_rows(dpre, 2 - k, tok, L) * w[k:k + 1]
        dw = _stack_rows([_colsum(dpre * taps[k]) for k in range(4)])
        return dx, dw, _colsum(dpre)

    slabs = [(xbc_raw, col0)] + ([(scaled[0], 0)] if scaled is not None else []) + [(d, 0) for d in dparts]
    colparams = [(conv_w, col0), (conv_b, col0)] + ([(scaled[1], 0)] if scaled is not None else [])
    return slab_call(name, body, slabs, colparams, [(width, BF16)], [4, 1])


def _box_mean(x, k, step, pos, n, L, transpose):
    lo, hi = k // 2, k - 1 - k // 2
    cnt = (jnp.minimum(pos + hi + 1, n) - jnp.maximum(pos - lo, 0)).astype(F32)
    if transpose:
        x = x / cnt
        lo, hi = hi, lo
    acc = x
    for o in range(-lo, hi + 1):
        if o == 0:
            continue
        rolled = pltpu.roll(x, (-o * step) % L, 0)
        acc = acc + jnp.where((pos + o >= 0) & (pos + o < n), rolled, 0.0)
    return acc if transpose else acc / cnt


def pool_diff(name, v, col0, gi, transpose):
    L = v.shape[1]
    rows = L // GRID_W
    k = POOL_WINDOWS[gi]

    def body(x):
        tok = lax.broadcasted_iota(jnp.int32, x.shape, 0)
        col = tok & (GRID_W - 1)
        row = tok >> 6
        if not transpose:
            m = _box_mean(x, k, GRID_W, row, rows, L, False)
            m = _box_mean(m, k, 1, col, GRID_W, L, False)
        else:
            m = _box_mean(x, k, 1, col, GRID_W, L, True)
            m = _box_mean(m, k, GRID_W, row, rows, L, True)
        return m - x

    return slab_call(name, body, [(v, col0)], [], [(POOL_GROUP, BF16)], [])[0]


def pool_mix_fwd(name, dgs, z_pool, pool_w, pool_scale):
    def body(d0, d1, d2, d3, z, w, scale):
        q = jnp.concatenate([_dot(d, w[g * POOL_GROUP:(g + 1) * POOL_GROUP]) for g, d in enumerate((d0, d1, d2, d3))], axis=1)
        return q * scale * _silu(z)

    return tok_call(name, body, list(dgs) + [z_pool], [], [pool_w, pool_scale], [(D, BF16)], [], [])[0]


def pool_mix_bwd(name, dgs, z_pool, dyp, pool_w, pool_scale):
    def body(d0, d1, d2, d3, z, dyp, w, scale):
        ds = (d0, d1, d2, d3)
        q = jnp.concatenate([_dot(d, w[g * POOL_GROUP:(g + 1) * POOL_GROUP]) for g, d in enumerate(ds)], axis=1)
        dypm = dyp * _silu(z)
        dz = dyp * (q * scale) * _dsilu(z)
        dq = (dypm * scale).astype(BF16)
        dds, gws = [], []
        for g, d in enumerate(ds):
            dqg = dq[:, g * POOL_GROUP:(g + 1) * POOL_GROUP]
            dds.append(_dot_nt(dqg, w[g * POOL_GROUP:(g + 1) * POOL_GROUP]))
            gws.append(_dot_tn(d, dqg))
        return (*dds, dz, jnp.concatenate(gws, axis=0), _colsum(dypm * q))

    return tok_call(name, body, list(dgs) + [z_pool, dyp], [], [pool_w, pool_scale],
                    [(POOL_GROUP, F32)] * 4 + [(D, BF16)], [], [(D, POOL_GROUP), (1, D)])


def _cumsum_lanes(a, reverse):
    n = a.shape[1]
    k = lax.broadcasted_iota(jnp.int32, (n, n), 0)
    i = lax.broadcasted_iota(jnp.int32, (n, n), 1)
    tri = jnp.where((k >= i) if reverse else (k <= i), 1.0, 0.0).astype(BF16)
    return _dot_exact01(a, tri)


def _rows_to_cols(rows):
    r = rows.shape[0]
    if r < LANES:
        rows = jnp.concatenate([rows, jnp.zeros((LANES - r, rows.shape[1]), F32)], axis=0)
    return rows.T


def _cols_to_rows(cols):
    q = cols[0].shape[0]
    lane = lax.broadcasted_iota(jnp.int32, (q, LANES), 1)
    acc = jnp.zeros((q, LANES), F32)
    for r, c in enumerate(cols):
        acc = acc + jnp.where(lane == r, c, 0.0)
    return acc.T[0:len(cols)]


def _ssd_scalars(dtraw, bias, alog, reverse):
    dt = _softplus(dtraw + bias)
    A = -jnp.exp(alog)
    cs = _cumsum_lanes(dt * A, reverse)
    total = cs[:, 0:1] if reverse else cs[:, CHUNK - 1:CHUNK]
    return dt, A, cs, total


def _decay_matrix(cs_col, cs_row, reverse):
    i = lax.broadcasted_iota(jnp.int32, (CHUNK, CHUNK), 0)
    j = lax.broadcasted_iota(jnp.int32, (CHUNK, CHUNK), 1)
    keep = (i <= j) if reverse else (i >= j)
    return jnp.exp(jnp.where(keep, cs_col - cs_row, -jnp.inf))


def ssd_fwd_v1(name, dtT, bias, alog, xbc, h0, direction, with_y):
    Bn, L = xbc.shape[:2]
    nc = L // CHUNK
    reverse = direction == 1
    rowblk = direction * N_BC

    def chunk_of(s):
        return (nc - 1 - s) if reverse else s

    def kern(dt_ref, bias_ref, alog_ref, x_ref, b_ref, c_ref, h0_ref, *rest):
        if with_y:
            y_ref, hs_ref, hf_ref, h_scr, xt_scr = rest
        else:
            hs_ref, hf_ref, h_scr, xt_scr = rest
        s = pl.program_id(2)

        @pl.when(s == 0)
        def _():
            h_scr[...] = h0_ref[0, 0]

        dt, _, cs, total = _ssd_scalars(dt_ref[0], bias_ref[0], alog_ref[0], reverse)
        e_row = jnp.exp(cs)
        t_row = jnp.exp(total - cs)
        dc = jnp.exp(total)
        cols = _rows_to_cols(jnp.concatenate([dt, e_row, t_row, cs], axis=0))
        x = x_ref[0]
        bm = b_ref[0].astype(BF16)
        cm = c_ref[0].astype(BF16)
        h = h_scr[...]
        hs_ref[0, 0, 0] = h
        if with_y:
            cb = _dot_nt(cm, bm)
            yoff = _dot(cm, h.astype(BF16))
        for r in range(HPG):
            sl = slice(r * HEAD_DIM, (r + 1) * HEAD_DIM)
            xdt = x[:, sl] * cols[:, r:r + 1]
            if with_y:
                lr = _decay_matrix(cols[:, 3 * HPG + r:3 * HPG + r + 1], cs[r:r + 1], reverse)
                ydiag = _dot((cb * lr).astype(BF16), xdt.astype(BF16))
                y_ref[0, :, sl] = ydiag + yoff[:, sl] * cols[:, HPG + r:HPG + r + 1]
            xt_scr[:, sl] = (xdt * cols[:, 2 * HPG + r:2 * HPG + r + 1]).astype(BF16)
        st = _dot_tn(bm, xt_scr[...])
        for r in range(HPG):
            sl = slice(r * HEAD_DIM, (r + 1) * HEAD_DIM)
            h_scr[:, sl] = h[:, sl] * dc[r:r + 1] + st[:, sl]

        @pl.when(s == nc - 1)
        def _():
            hf_ref[0, 0] = h_scr[...]

    in_specs = [
        pl.BlockSpec((1, HPG, CHUNK), lambda b, g, s: (b, rowblk + g, chunk_of(s))),
        pl.BlockSpec((1, HPG, 1), lambda b, g, s: (rowblk + g, 0, 0)),
        pl.BlockSpec((1, HPG, 1), lambda b, g, s: (rowblk + g, 0, 0)),
        pl.BlockSpec((1, CHUNK, GW), lambda b, g, s: (b, chunk_of(s), g)),
        pl.BlockSpec((1, CHUNK, D_STATE), lambda b, g, s: (b, chunk_of(s), D_INNER // D_STATE + g)),
        pl.BlockSpec((1, CHUNK, D_STATE), lambda b, g, s: (b, chunk_of(s), D_INNER // D_STATE + N_BC + g)),
        pl.BlockSpec((1, 1, D_STATE, GW), lambda b, g, s: (b, g, 0, 0)),
    ]
    out_shape, out_specs = [], []
    if with_y:
        out_shape.append(jax.ShapeDtypeStruct((Bn, L, D_INNER), F32))
        out_specs.append(pl.BlockSpec((1, CHUNK, GW), lambda b, g, s: (b, chunk_of(s), g)))
    out_shape += [jax.ShapeDtypeStruct((Bn, N_BC, nc, D_STATE, GW), F32), jax.ShapeDtypeStruct((Bn, N_BC, D_STATE, GW), F32)]
    out_specs += [pl.BlockSpec((1, 1, 1, D_STATE, GW), lambda b, g, s: (b, g, chunk_of(s), 0, 0)),
                  pl.BlockSpec((1, 1, D_STATE, GW), lambda b, g, s: (b, g, 0, 0))]
    return pl.pallas_call(
        kern, name=name, out_shape=out_shape, grid=(Bn, N_BC, nc), in_specs=in_specs, out_specs=out_specs,
        scratch_shapes=[pltpu.VMEM((D_STATE, GW), F32), pltpu.VMEM((CHUNK, GW), BF16)],
        compiler_params=_cp(("arbitrary", "arbitrary", "arbitrary")))(dtT, bias, alog, xbc, xbc, xbc, h0)


def ssd_bwd_v1(name, dtT, bias, alog, xbc, h_start, dy, dh_final, direction):
    Bn, L = xbc.shape[:2]
    nc = L // CHUNK
    reverse = direction == 1
    rowblk = direction * N_BC
    has_y = dy is not None
    last = 0 if reverse else CHUNK - 1

    def chunk_of(s):
        return s if reverse else (nc - 1 - s)

    def kern(*refs):
        if has_y:
            (dt_ref, bias_ref, alog_ref, x_ref, b_ref, c_ref, hs_ref, dhf_ref, dy_ref,
             dx_ref, db_ref, dc_ref, ddt_ref, dbias_ref, dalog_ref, dh0_ref, dh_scr, e_scr, t_scr) = refs
        else:
            (dt_ref, bias_ref, alog_ref, x_ref, b_ref, hs_ref, dhf_ref,
             dx_ref, db_ref, ddt_ref, dbias_ref, dalog_ref, dh0_ref, dh_scr, t_scr) = refs
        s = pl.program_id(2)

        @pl.when(s == 0)
        def _():
            dh_scr[...] = dhf_ref[0, 0]
            dbias_ref[...] = jnp.zeros(dbias_ref.shape, F32)
            dalog_ref[...] = jnp.zeros(dalog_ref.shape, F32)

        dtraw = dt_ref[0]
        dt, A, cs, total = _ssd_scalars(dtraw, bias_ref[0], alog_ref[0], reverse)
        e_row = jnp.exp(cs)
        t_row = jnp.exp(total - cs)
        dcy = jnp.exp(total)
        cols = _rows_to_cols(jnp.concatenate([dt, e_row, t_row, cs], axis=0))
        x = x_ref[0]
        bm = b_ref[0].astype(BF16)
        h = hs_ref[0, 0, 0]
        dh = dh_scr[...]
        dh_bf = dh.astype(BF16)
        bdh = _dot(bm, dh_bf)
        if has_y:
            cm = c_ref[0].astype(BF16)
            dyv = dy_ref[0]
            cb = _dot_nt(cm, bm)
            yoff = _dot(cm, h.astype(BF16))
            dcb = jnp.zeros((CHUNK, CHUNK), F32)
        col_terms, row_terms, ddt_cols, dtot = [], [], [], []
        for r in range(HPG):
            sl = slice(r * HEAD_DIM, (r + 1) * HEAD_DIM)
            dt_c = cols[:, r:r + 1]
            e_c = cols[:, HPG + r:HPG + r + 1]
            t_c = cols[:, 2 * HPG + r:2 * HPG + r + 1]
            xr = x[:, sl]
            xdt = xr * dt_c
            dxdt = t_c * bdh[:, sl]
            d_t = jnp.sum(bdh[:, sl] * xdt, axis=1, keepdims=True)
            col = -(t_c * d_t)
            tot = jnp.sum(t_c * d_t, axis=0, keepdims=True) + dcy[r:r + 1] * jnp.sum(h[:, sl] * dh[:, sl], keepdims=True)
            if has_y:
                dyr = dyv[:, sl]
                lr = _decay_matrix(cols[:, 3 * HPG + r:3 * HPG + r + 1], cs[r:r + 1], reverse)
                w = cb * lr
                gm = _dot_nt(dyr.astype(BF16), xdt.astype(BF16))
                m = gm * w
                dcb = dcb + gm * lr
                dxdt = dxdt + _dot_tn(w.astype(BF16), dyr.astype(BF16))
                col = col + jnp.sum(m, axis=1, keepdims=True) + jnp.sum(yoff[:, sl] * dyr, axis=1, keepdims=True) * e_c
                row_terms.append(-jnp.sum(m, axis=0, keepdims=True))
                e_scr[:, sl] = (e_c * dyr).astype(BF16)
            t_scr[:, sl] = (t_c * xdt).astype(BF16)
            dx_ref[0, :, sl] = dxdt * dt_c
            ddt_cols.append(jnp.sum(dxdt * xr, axis=1, keepdims=True))
            col_terms.append(col)
            dtot.append(tot)
        db = _dot_nt(t_scr[...], dh_bf)
        if has_y:
            dcb_bf = dcb.astype(BF16)
            db = db + _dot_tn(dcb_bf, cm)
            dc_ref[0] = _dot(dcb_bf, bm) + _dot_nt(e_scr[...], h.astype(BF16))
            cte = _dot_tn(cm, e_scr[...])
        db_ref[0] = db
        for r in range(HPG):
            sl = slice(r * HEAD_DIM, (r + 1) * HEAD_DIM)
            new = dh[:, sl] * dcy[r:r + 1]
            if has_y:
                new = new + cte[:, sl]
            dh_scr[:, sl] = new
        dcs = _cols_to_rows(col_terms)
        if has_y:
            dcs = dcs + _stack_rows(row_terms)
        lane = lax.broadcasted_iota(jnp.int32, (HPG, CHUNK), 1)
        dcs = dcs + jnp.where(lane == last, _stack_rows([jnp.broadcast_to(t, (1, CHUNK)) for t in dtot]), 0.0)
        da = _cumsum_lanes(dcs, not reverse)
        ddt = da * A + _cols_to_rows(ddt_cols)
        ddtraw = ddt * _sigmoid(dtraw + bias_ref[0])
        ddt_ref[0] = ddtraw
        dbias_ref[0, 0] += jnp.sum(ddtraw, axis=1, keepdims=True)
        dalog_ref[0, 0] += jnp.sum(da * dt, axis=1, keepdims=True) * A

        @pl.when(s == nc - 1)
        def _():
            dh0_ref[0, 0] = dh_scr[...]

    cidx = lambda b, g, s: (b, chunk_of(s), g)
    in_specs = [
        pl.BlockSpec((1, HPG, CHUNK), lambda b, g, s: (b, rowblk + g, chunk_of(s))),
        pl.BlockSpec((1, HPG, 1), lambda b, g, s: (rowblk + g, 0, 0)),
        pl.BlockSpec((1, HPG, 1), lambda b, g, s: (rowblk + g, 0, 0)),
        pl.BlockSpec((1, CHUNK, GW), cidx),
        pl.BlockSpec((1, CHUNK, D_STATE), lambda b, g, s: (b, chunk_of(s), D_INNER // D_STATE + g)),
    ]
    args = [dtT, bias, alog, xbc, xbc]
    if has_y:
        in_specs.append(pl.BlockSpec((1, CHUNK, D_STATE), lambda b, g, s: (b, chunk_of(s), D_INNER // D_STATE + N_BC + g)))
        args.append(xbc)
    in_specs += [pl.BlockSpec((1, 1, 1, D_STATE, GW), lambda b, g, s: (b, g, chunk_of(s), 0, 0)),
                 pl.BlockSpec((1, 1, D_STATE, GW), lambda b, g, s: (b, g, 0, 0))]
    args += [h_start, dh_final]
    if has_y:
        in_specs.append(pl.BlockSpec((1, CHUNK, GW), cidx))
        args.append(dy)
    out_shape = [jax.ShapeDtypeStruct((Bn, L, D_INNER), F32), jax.ShapeDtypeStruct((Bn, L, N_BC * D_STATE), F32)]
    out_specs = [pl.BlockSpec((1, CHUNK, GW), cidx), pl.BlockSpec((1, CHUNK, D_STATE), cidx)]
    if has_y:
        out_shape.append(jax.ShapeDtypeStruct((Bn, L, N_BC * D_STATE), F32))
        out_specs.append(pl.BlockSpec((1, CHUNK, D_STATE), cidx))
    out_shape += [jax.ShapeDtypeStruct((Bn, N_HEADS, L), F32), jax.ShapeDtypeStruct((Bn, N_BC, HPG, 1), F32),
                  jax.ShapeDtypeStruct((Bn, N_BC, HPG, 1), F32), jax.ShapeDtypeStruct((Bn, N_BC, D_STATE, GW), F32)]
    out_specs += [pl.BlockSpec((1, HPG, CHUNK), lambda b, g, s: (b, g, chunk_of(s))),
                  pl.BlockSpec((1, 1, HPG, 1), lambda b, g, s: (b, g, 0, 0)),
                  pl.BlockSpec((1, 1, HPG, 1), lambda b, g, s: (b, g, 0, 0)),
                  pl.BlockSpec((1, 1, D_STATE, GW), lambda b, g, s: (b, g, 0, 0))]
    scratch = [pltpu.VMEM((D_STATE, GW), F32)] + ([pltpu.VMEM((CHUNK, GW), BF16)] if has_y else []) + [pltpu.VMEM((CHUNK, GW), BF16)]
    res = pl.pallas_call(
        kern, name=name, out_shape=out_shape, grid=(Bn, N_BC, nc), in_specs=in_specs, out_specs=out_specs,
        scratch_shapes=scratch, compiler_params=_cp(("arbitrary", "arbitrary", "arbitrary")))(*args)
    if has_y:
        return res
    dxs, db, ddt, dbias, dalog, dh0 = res
    return dxs, db, None, ddt, dbias, dalog, dh0


def _tri_mask(transposed, reverse):
    sub = lax.broadcasted_iota(jnp.int32, (CHUNK, CHUNK), 0)
    lane = lax.broadcasted_iota(jnp.int32, (CHUNK, CHUNK), 1)
    i, j = (lane, sub) if transposed else (sub, lane)
    return (i <= j) if reverse else (i >= j)


def ssd_fwd(name, dtT, bias, alog, xbc, h0, direction, with_y):
    Bn, L = xbc.shape[:2]
    nc = L // CHUNK
    reverse = direction == 1
    rowblk = direction * N_BC

    def chunk_of(s):
        return (nc - 1 - s) if reverse else s

    def kern(dt_ref, bias_ref, alog_ref, x_ref, b_ref, c_ref, h0_ref, *rest):
        if with_y:
            y_ref, hs_ref, hf_ref, h_scr = rest
        else:
            hs_ref, hf_ref, h_scr = rest
        s = pl.program_id(2)

        @pl.when(s == 0)
        def _():
            h_scr[...] = h0_ref[0, 0]

        dt, _, cs, total = _ssd_scalars(dt_ref[0], bias_ref[0], alog_ref[0], reverse)
        u = cs - jnp.log(dt)
        dtt = jnp.exp(total - u)
        dc = jnp.exp(total)
        x_bf = x_ref[0].astype(BF16)
        bm = b_ref[0]
        h = h_scr[...]
        h_bf = h.astype(BF16)
        hs_ref[0, 0, 0] = h
        bt = bm.T
        if with_y:
            cm = c_ref[0]
            cb = _dot_nt(cm.astype(BF16), bm.astype(BF16))
            cs_cols = _rows_to_cols(cs)
            keep = _tri_mask(False, reverse)
        first = lax.broadcasted_iota(jnp.int32, (1, LANES), 1) < HEAD_DIM
        heads = range(HPG)
        psl = [slice((r // 2) * LANES, (r // 2 + 1) * LANES) for r in heads]
        lhs = []
        if with_y:
            for r in heads:
                cs_col = jnp.broadcast_to(cs_cols[:, r:r + 1], (CHUNK, LANES))
                wf = cb * jnp.exp(jnp.where(keep, cs_col - u[r:r + 1], -jnp.inf))
                lhs.append(jnp.concatenate([wf.astype(BF16), (cm * jnp.exp(cs_col)).astype(BF16)], axis=1))
        bts = [(bt * dtt[r:r + 1]).astype(BF16) for r in heads]
        sts = [_dot(bts[r], x_bf[:, psl[r]]) for r in heads]
        if with_y:
            ys = [_dot(lhs[r], jnp.concatenate([x_bf[:, psl[r]], h_bf[:, psl[r]]], axis=0)) for r in heads]
        for p in range(HPG // 2):
            if with_y:
                y_ref[0, :, psl[2 * p]] = jnp.where(first, ys[2 * p], ys[2 * p + 1])
            dc_p = jnp.where(first, dc[2 * p:2 * p + 1], dc[2 * p + 1:2 * p + 2])
            h_scr[:, psl[2 * p]] = h[:, psl[2 * p]] * dc_p + jnp.where(first, sts[2 * p], sts[2 * p + 1])

        @pl.when(s == nc - 1)
        def _():
            hf_ref[0, 0] = h_scr[...]

    in_specs = [
        pl.BlockSpec((1, HPG, CHUNK), lambda b, g, s: (b, rowblk + g, chunk_of(s))),
        pl.BlockSpec((1, HPG, 1), lambda b, g, s: (rowblk + g, 0, 0)),
        pl.BlockSpec((1, HPG, 1), lambda b, g, s: (rowblk + g, 0, 0)),
        pl.BlockSpec((1, CHUNK, GW), lambda b, g, s: (b, chunk_of(s), g)),
        pl.BlockSpec((1, CHUNK, D_STATE), lambda b, g, s: (b, chunk_of(s), D_INNER // D_STATE + g)),
        pl.BlockSpec((1, CHUNK, D_STATE), lambda b, g, s: (b, chunk_of(s), D_INNER // D_STATE + N_BC + g)),
        pl.BlockSpec((1, 1, D_STATE, GW), lambda b, g, s: (b, g, 0, 0)),
    ]
    out_shape, out_specs = [], []
    if with_y:
        out_shape.append(jax.ShapeDtypeStruct((Bn, L, D_INNER), F32))
        out_specs.append(pl.BlockSpec((1, CHUNK, GW), lambda b, g, s: (b, chunk_of(s), g)))
    out_shape += [jax.ShapeDtypeStruct((Bn, N_BC, nc, D_STATE, GW), F32), jax.ShapeDtypeStruct((Bn, N_BC, D_STATE, GW), F32)]
    out_specs += [pl.BlockSpec((1, 1, 1, D_STATE, GW), lambda b, g, s: (b, g, chunk_of(s), 0, 0)),
                  pl.BlockSpec((1, 1, D_STATE, GW), lambda b, g, s: (b, g, 0, 0))]
    return pl.pallas_call(
        kern, name=name, out_shape=out_shape, grid=(Bn, N_BC, nc), in_specs=in_specs, out_specs=out_specs,
        scratch_shapes=[pltpu.VMEM((D_STATE, GW), F32)],
        compiler_params=_cp(("arbitrary", "arbitrary", "arbitrary")))(dtT, bias, alog, xbc, xbc, xbc, h0)


def ssd_bwd(name, dtT, bias, alog, xbc, h_start, dy, dh_final, direction):
    Bn, L = xbc.shape[:2]
    nc = L // CHUNK
    reverse = direction == 1
    rowblk = direction * N_BC
    has_y = dy is not None
    last = 0 if reverse else CHUNK - 1

    def chunk_of(s):
        return s if reverse else (nc - 1 - s)

    def kern(*refs):
        if has_y:
            (dt_ref, bias_ref, alog_ref, x_ref, b_ref, hs_ref, dhf_ref, c_ref, dy_ref,
             dx_ref, db_ref, ddt_ref, dbias_ref, dalog_ref, dh0_ref, dc_ref, dh_scr) = refs
        else:
            (dt_ref, bias_ref, alog_ref, x_ref, b_ref, hs_ref, dhf_ref,
             dx_ref, db_ref, ddt_ref, dbias_ref, dalog_ref, dh0_ref, dh_scr) = refs
        s = pl.program_id(2)

        @pl.when(s == 0)
        def _():
            dh_scr[...] = dhf_ref[0, 0]
            dbias_ref[...] = jnp.zeros(dbias_ref.shape, F32)
            dalog_ref[...] = jnp.zeros(dalog_ref.shape, F32)

        dtraw = dt_ref[0]
        dt, A, cs, total = _ssd_scalars(dtraw, bias_ref[0], alog_ref[0], reverse)
        u = cs - jnp.log(dt)
        dtt = jnp.exp(total - u)
        dcy = jnp.exp(total)
        u_cols = _rows_to_cols(u)
        x_bf = x_ref[0].astype(BF16)
        bm = b_ref[0]
        bt = bm.T
        h = hs_ref[0, 0, 0]
        dh = dh_scr[...]
        dh_bf = dh.astype(BF16)
        dbt = jnp.zeros((D_STATE, CHUNK), F32)
        if has_y:
            cm = c_ref[0]
            ct = cm.T
            e_row = jnp.exp(cs)
            dy_bf = dy_ref[0].astype(BF16)
            h_bf = h.astype(BF16)
            cbt = _dot_nt(bm.astype(BF16), cm.astype(BF16))
            keep = _tri_mask(True, reverse)
            dcbt = jnp.zeros((CHUNK, CHUNK), F32)
            dct = jnp.zeros((D_STATE, CHUNK), F32)
        tots, out_rows, in_rows, in_cols = [], [], [], []
        first = lax.broadcasted_iota(jnp.int32, (1, LANES), 1) < HEAD_DIM
        heads = range(HPG)
        psl = [slice((r // 2) * LANES, (r // 2 + 1) * LANES) for r in heads]
        mine = [first if r % 2 == 0 else jnp.logical_not(first) for r in heads]
        zeros_bf = jnp.zeros((CHUNK, LANES), BF16)

        def prep(r):
            u_col = jnp.broadcast_to(u_cols[:, r:r + 1], (CHUNK, LANES))
            bs = (bm * jnp.exp(total[r:r + 1] - u_col)).astype(BF16)
            if not has_y:
                return bs, None
            et = jnp.exp(jnp.where(keep, cs[r:r + 1] - u_col, -jnp.inf))
            return jnp.concatenate([(cbt * et).astype(BF16), bs], axis=1), et

        def matmuls(r, lhs):
            p2raw = _dot_nt(dh_bf[:, psl[r]], jnp.where(mine[r], x_bf[:, psl[r]], zeros_bf))
            if not has_y:
                return p2raw, None, None, _dot(lhs, dh_bf[:, psl[r]])
            a1 = _dot_nt(jnp.concatenate([x_bf[:, psl[r]], h_bf[:, psl[r]]], axis=0),
                         jnp.where(mine[r], dy_bf[:, psl[r]], zeros_bf))
            new = _dot((ct * e_row[r:r + 1]).astype(BF16), dy_bf[:, psl[r]])
            dx = _dot(lhs, jnp.concatenate([dy_bf[:, psl[r]], dh_bf[:, psl[r]]], axis=0))
            return p2raw, a1, new, dx

        def post(r, p2raw, a1, et, dbt, dcbt, dct):
            if has_y:
                pt = a1[0:CHUNK] * et
                dcbt = dcbt + pt
                mt = pt * cbt
                ph = a1[CHUNK:] * e_row[r:r + 1]
                dct = dct + ph
                out_rows.append(_colsum(mt + ct * ph))
                in_cols.append(jnp.sum(mt, axis=1, keepdims=True))
            p2 = p2raw * dtt[r:r + 1]
            dbt = dbt + p2
            t_term = _colsum(bt * p2)
            in_rows.append(t_term)
            hdh = h[:, psl[r]] * dh[:, psl[r]]
            tot = jnp.sum(t_term, axis=1, keepdims=True) + dcy[r:r + 1] * jnp.sum(jnp.where(mine[r], hdh, 0.0), keepdims=True)
            tots.append(jnp.broadcast_to(tot, (1, CHUNK)))
            return dbt, dcbt, dct

        if not has_y:
            dcbt = dct = None
        dxs, news, pending = [], [], []
        batch = HPG
        for r0 in range(0, HPG, batch):
            preps = [prep(r) for r in range(r0, r0 + batch)]
            mms = [matmuls(r, preps[r - r0][0]) for r in range(r0, r0 + batch)]
            for args in pending:
                dbt, dcbt, dct = post(*args, dbt, dcbt, dct)
            pending = [(r, mms[r - r0][0], mms[r - r0][1], preps[r - r0][1]) for r in range(r0, r0 + batch)]
            dxs += [m[3] for m in mms]
            news += [m[2] for m in mms]
        for args in pending:
            dbt, dcbt, dct = post(*args, dbt, dcbt, dct)
        for p in range(HPG // 2):
            dx_ref[0, :, psl[2 * p]] = jnp.where(first, dxs[2 * p], dxs[2 * p + 1])
            new = dh[:, psl[2 * p]] * jnp.where(first, dcy[2 * p:2 * p + 1], dcy[2 * p + 1:2 * p + 2])
            if has_y:
                new = new + jnp.where(first, news[2 * p], news[2 * p + 1])
            dh_scr[:, psl[2 * p]] = new
        db = dbt.T
        if has_y:
            dcbt_bf = dcbt.astype(BF16)
            db = db + _dot(dcbt_bf, cm.astype(BF16))
            dc_ref[0] = dct.T + _dot_tn(dcbt_bf, bm.astype(BF16))
        db_ref[0] = db
        s_row = _stack_rows(in_rows)
        lane = lax.broadcasted_iota(jnp.int32, (HPG, CHUNK), 1)
        dcs = jnp.where(lane == last, _stack_rows(tots), 0.0)
        if has_y:
            s_row = s_row + _cols_to_rows(in_cols)
            dcs = dcs + _stack_rows(out_rows)
        dcs = dcs - s_row
        da = _cumsum_lanes(dcs, not reverse)
        ddt = da * A + jnp.where(dt > 0.0, s_row / dt, 0.0)
        ddtraw = ddt * _sigmoid(dtraw + bias_ref[0])
        ddt_ref[0] = ddtraw
        dbias_ref[0, 0] += jnp.sum(ddtraw, axis=1, keepdims=True)
        dalog_ref[0, 0] += jnp.sum(da * dt, axis=1, keepdims=True) * A

        @pl.when(s == nc - 1)
        def _():
            dh0_ref[0, 0] = dh_scr[...]

    cidx = lambda b, g, s: (b, chunk_of(s), g)
    hidx = lambda b, g, s: (b, g, 0, 0)
    in_specs = [
        pl.BlockSpec((1, HPG, CHUNK), lambda b, g, s: (b, rowblk + g, chunk_of(s))),
        pl.BlockSpec((1, HPG, 1), lambda b, g, s: (rowblk + g, 0, 0)),
        pl.BlockSpec((1, HPG, 1), lambda b, g, s: (rowblk + g, 0, 0)),
        pl.BlockSpec((1, CHUNK, GW), cidx),
        pl.BlockSpec((1, CHUNK, D_STATE), lambda b, g, s: (b, chunk_of(s), D_INNER // D_STATE + g)),
        pl.BlockSpec((1, 1, 1, D_STATE, GW), lambda b, g, s: (b, g, chunk_of(s), 0, 0)),
        pl.BlockSpec((1, 1, D_STATE, GW), hidx),
    ]
    args = [dtT, bias, alog, xbc, xbc, h_start, dh_final]
    if has_y:
        in_specs += [pl.BlockSpec((1, CHUNK, D_STATE), lambda b, g, s: (b, chunk_of(s), D_INNER // D_STATE + N_BC + g)),
                     pl.BlockSpec((1, CHUNK, GW), cidx)]
        args += [xbc, dy]
    out_shape = [jax.ShapeDtypeStruct((Bn, L, D_INNER), F32), jax.ShapeDtypeStruct((Bn, L, N_BC * D_STATE), F32),
                 jax.ShapeDtypeStruct((Bn, N_HEADS, L), F32), jax.ShapeDtypeStruct((Bn, N_BC, HPG, 1), F32),
                 jax.ShapeDtypeStruct((Bn, N_BC, HPG, 1), F32), jax.ShapeDtypeStruct((Bn, N_BC, D_STATE, GW), F32)]
    out_specs = [pl.BlockSpec((1, CHUNK, GW), cidx), pl.BlockSpec((1, CHUNK, D_STATE), cidx),
                 pl.BlockSpec((1, HPG, CHUNK), lambda b, g, s: (b, g, chunk_of(s))),
                 pl.BlockSpec((1, 1, HPG, 1), hidx), pl.BlockSpec((1, 1, HPG, 1), hidx), pl.BlockSpec((1, 1, D_STATE, GW), hidx)]
    if has_y:
        out_shape.append(jax.ShapeDtypeStruct((Bn, L, N_BC * D_STATE), F32))
        out_specs.append(pl.BlockSpec((1, CHUNK, D_STATE), cidx))
    res = pl.pallas_call(
        kern, name=name, out_shape=out_shape, grid=(Bn, N_BC, nc), in_specs=in_specs, out_specs=out_specs,
        scratch_shapes=[pltpu.VMEM((D_STATE, GW), F32)],
        compiler_params=_cp(("arbitrary", "arbitrary", "arbitrary")))(*args)
    dxs, db, ddt, dbias, dalog, dh0 = res[:6]
    return dxs, db, (res[6] if has_y else None), ddt, dbias, dalog, dh0


GPS = 4


def ssd_fwd3(name, dtT, bias, alog, xbc, h0, direction, with_y):
    Bn, L = xbc.shape[:2]
    nc = L // CHUNK
    reverse = direction == 1
    blk0 = direction * (N_BC // GPS)
    gs = range(GPS)

    def chunk_of(s):
        return (nc - 1 - s) if reverse else s

    def kern(dt_ref, bias_ref, alog_ref, x_ref, b_ref, c_ref, h0_ref, *rest):
        if with_y:
            y_ref, hs_ref, hf_ref, h_scr = rest
        else:
            hs_ref, hf_ref, h_scr = rest
        s = pl.program_id(2)

        @pl.when(s == 0)
        def _():
            h_scr[...] = h0_ref[0]

        first = lax.broadcasted_iota(jnp.int32, (1, LANES), 1) < HEAD_DIM
        heads = range(HPG)
        psl = [slice((r // 2) * LANES, (r // 2 + 1) * LANES) for r in heads]
        keep = _tri_mask(False, reverse)
        sc, x_bf, bm, h, h_bf, bt, cm, cb, cs_cols = [], [], [], [], [], [], [], [], []
        for g in gs:
            dt, _, cs, total = _ssd_scalars(dt_ref[0, g * HPG:(g + 1) * HPG], bias_ref[g], alog_ref[g], reverse)
            u = cs - jnp.log(dt)
            sc.append((cs, u, jnp.exp(total - u), jnp.exp(total)))
            x_bf.append(x_ref[0, :, g * GW:(g + 1) * GW].astype(BF16))
            bm.append(b_ref[0, :, g * D_STATE:(g + 1) * D_STATE])
            h.append(h_scr[g])
            h_bf.append(h[g].astype(BF16))
            hs_ref[0, g, 0] = h[g]
            bt.append(bm[g].T)
            if with_y:
                cm.append(c_ref[0, :, g * D_STATE:(g + 1) * D_STATE])
                cb.append(_dot_nt(cm[g].astype(BF16), bm[g].astype(BF16)))
                cs_cols.append(_rows_to_cols(cs))
        lhs = [[] for _ in gs]
        if with_y:
            for g in gs:
                cs, u = sc[g][0], sc[g][1]
                for r in heads:
                    cs_col = jnp.broadcast_to(cs_cols[g][:, r:r + 1], (CHUNK, LANES))
                    wf = cb[g] * jnp.exp(jnp.where(keep, cs_col - u[r:r + 1], -jnp.inf))
                    lhs[g].append(jnp.concatenate([wf.astype(BF16), (cm[g] * jnp.exp(cs_col)).astype(BF16)], axis=1))
        bts = [[(bt[g] * sc[g][2][r:r + 1]).astype(BF16) for r in heads] for g in gs]
        sts = [[_dot(bts[g][r], x_bf[g][:, psl[r]]) for r in heads] for g in gs]
        if with_y:
            ys = [[_dot(lhs[g][r], jnp.concatenate([x_bf[g][:, psl[r]], h_bf[g][:, psl[r]]], axis=0)) for r in heads] for g in gs]
        for g in gs:
            dc = sc[g][3]
            for p in range(HPG // 2):
                if with_y:
                    y_ref[0, :, g * GW + p * LANES:g * GW + (p + 1) * LANES] = jnp.where(first, ys[g][2 * p], ys[g][2 * p + 1])
                dc_p = jnp.where(first, dc[2 * p:2 * p + 1], dc[2 * p + 1:2 * p + 2])
                h_scr[g, :, psl[2 * p]] = h[g][:, psl[2 * p]] * dc_p + jnp.where(first, sts[g][2 * p], sts[g][2 * p + 1])

        @pl.when(s == nc - 1)
        def _():
            hf_ref[0] = h_scr[...]

    nb = D_INNER // (GPS * D_STATE)
    in_specs = [
        pl.BlockSpec((1, GPS * HPG, CHUNK), lambda b, g, s: (b, blk0 + g, chunk_of(s))),
        pl.BlockSpec((GPS, HPG, 1), lambda b, g, s: (blk0 + g, 0, 0)),
        pl.BlockSpec((GPS, HPG, 1), lambda b, g, s: (blk0 + g, 0, 0)),
        pl.BlockSpec((1, CHUNK, GPS * GW), lambda b, g, s: (b, chunk_of(s), g)),
        pl.BlockSpec((1, CHUNK, GPS * D_STATE), lambda b, g, s: (b, chunk_of(s), nb + g)),
        pl.BlockSpec((1, CHUNK, GPS * D_STATE), lambda b, g, s: (b, chunk_of(s), nb + N_BC // GPS + g)),
        pl.BlockSpec((1, GPS, D_STATE, GW), lambda b, g, s: (b, g, 0, 0)),
    ]
    out_shape, out_specs = [], []
    if with_y:
        out_shape.append(jax.ShapeDtypeStruct((Bn, L, D_INNER), F32))
        out_specs.append(pl.BlockSpec((1, CHUNK, GPS * GW), lambda b, g, s: (b, chunk_of(s), g)))
    out_shape += [jax.ShapeDtypeStruct((Bn, N_BC, nc, D_STATE, GW), F32), jax.ShapeDtypeStruct((Bn, N_BC, D_STATE, GW), F32)]
    out_specs += [pl.BlockSpec((1, GPS, 1, D_STATE, GW), lambda b, g, s: (b, g, chunk_of(s), 0, 0)),
                  pl.BlockSpec((1, GPS, D_STATE, GW), lambda b, g, s: (b, g, 0, 0))]
    return pl.pallas_call(
        kern, name=name, out_shape=out_shape, grid=(Bn, N_BC // GPS, nc), in_specs=in_specs, out_specs=out_specs,
        scratch_shapes=[pltpu.VMEM((GPS, D_STATE, GW), F32)],
        compiler_params=_cp(("arbitrary", "arbitrary", "arbitrary")))(dtT, bias, alog, xbc, xbc, xbc, h0)


def ssd_bwd3(name, dtT, bias, alog, xbc, h_start, dy, dh_final, direction):
    Bn, L = xbc.shape[:2]
    nc = L // CHUNK
    reverse = direction == 1
    blk0 = direction * (N_BC // GPS)
    has_y = dy is not None
    last = 0 if reverse else CHUNK - 1
    gs = range(GPS)

    def chunk_of(s):
        return s if reverse else (nc - 1 - s)

    def kern(*refs):
        if has_y:
            (dt_ref, bias_ref, alog_ref, x_ref, b_ref, hs_ref, dhf_ref, c_ref, dy_ref,
             dx_ref, db_ref, ddt_ref, dbias_ref, dalog_ref, dh0_ref, dc_ref, dh_scr) = refs
        else:
            (dt_ref, bias_ref, alog_ref, x_ref, b_ref, hs_ref, dhf_ref,
             dx_ref, db_ref, ddt_ref, dbias_ref, dalog_ref, dh0_ref, dh_scr) = refs
        s = pl.program_id(2)

        @pl.when(s == 0)
        def _():
            dh_scr[...] = dhf_ref[0]
            dbias_ref[...] = jnp.zeros(dbias_ref.shape, F32)
            dalog_ref[...] = jnp.zeros(dalog_ref.shape, F32)

        first = lax.broadcasted_iota(jnp.int32, (1, LANES), 1) < HEAD_DIM
        heads = range(HPG)
        psl = [slice((r // 2) * LANES, (r // 2 + 1) * LANES) for r in heads]
        mine = [first if r % 2 == 0 else jnp.logical_not(first) for r in heads]
        zeros_bf = jnp.zeros((CHUNK, LANES), BF16)
        keep = _tri_mask(True, reverse)
        ctx = []
        for g in gs:
            dtraw = dt_ref[0, g * HPG:(g + 1) * HPG]
            dt, A, cs, total = _ssd_scalars(dtraw, bias_ref[g], alog_ref[g], reverse)
            u = cs - jnp.log(dt)
            c = dict(dtraw=dtraw, dt=dt, A=A, cs=cs, total=total, u=u, dtt=jnp.exp(total - u), dcy=jnp.exp(total),
                     u_cols=_rows_to_cols(u), x_bf=x_ref[0, :, g * GW:(g + 1) * GW].astype(BF16),
                     bm=b_ref[0, :, g * D_STATE:(g + 1) * D_STATE], h=hs_ref[0, g, 0], dh=dh_scr[g])
            c["bt"] = c["bm"].T
            c["dh_bf"] = c["dh"].astype(BF16)
            if has_y:
                c["cm"] = c_ref[0, :, g * D_STATE:(g + 1) * D_STATE]
                c["ct"] = c["cm"].T
                c["e_row"] = jnp.exp(cs)
                c["dy_bf"] = dy_ref[0, :, g * GW:(g + 1) * GW].astype(BF16)
                c["h_bf"] = c["h"].astype(BF16)
                c["cbt"] = _dot_nt(c["bm"].astype(BF16), c["cm"].astype(BF16))
            ctx.append(c)
        for c in ctx:
            c["lhs"], c["et"] = [], []
            for r in heads:
                u_col = jnp.broadcast_to(c["u_cols"][:, r:r + 1], (CHUNK, LANES))
                bs = (c["bm"] * jnp.exp(c["total"][r:r + 1] - u_col)).astype(BF16)
                if has_y:
                    et = jnp.exp(jnp.where(keep, c["cs"][r:r + 1] - u_col, -jnp.inf))
                    c["et"].append(et)
                    c["lhs"].append(jnp.concatenate([(c["cbt"] * et).astype(BF16), bs], axis=1))
                else:
                    c["lhs"].append(bs)
        for c in ctx:
            c["p2raw"] = [_dot_nt(c["dh_bf"][:, psl[r]], jnp.where(mine[r], c["x_bf"][:, psl[r]], zeros_bf)) for r in heads]
            if has_y:
                c["a1"] = [_dot_nt(jnp.concatenate([c["x_bf"][:, psl[r]], c["h_bf"][:, psl[r]]], axis=0),
                                   jnp.where(mine[r], c["dy_bf"][:, psl[r]], zeros_bf)) for r in heads]
                c["news"] = [_dot((c["ct"] * c["e_row"][r:r + 1]).astype(BF16), c["dy_bf"][:, psl[r]]) for r in heads]
                c["dxs"] = [_dot(c["lhs"][r], jnp.concatenate([c["dy_bf"][:, psl[r]], c["dh_bf"][:, psl[r]]], axis=0)) for r in heads]
            else:
                c["dxs"] = [_dot(c["lhs"][r], c["dh_bf"][:, psl[r]]) for r in heads]
        for g, c in enumerate(ctx):
            dbt = jnp.zeros((D_STATE, CHUNK), F32)
            dcbt = jnp.zeros((CHUNK, CHUNK), F32)
            dct = jnp.zeros((D_STATE, CHUNK), F32)
            tots, out_rows, in_rows, in_cols = [], [], [], []
            for r in heads:
                if has_y:
                    pt = c["a1"][r][0:CHUNK] * c["et"][r]
                    dcbt = dcbt + pt
                    mt = pt * c["cbt"]
                    ph = c["a1"][r][CHUNK:] * c["e_row"][r:r + 1]
                    dct = dct + ph
                    out_rows.append(_colsum(mt + c["ct"] * ph))
                    in_cols.append(jnp.sum(mt, axis=1, keepdims=True))
                p2 = c["p2raw"][r] * c["dtt"][r:r + 1]
                dbt = dbt + p2
                t_term = _colsum(c["bt"] * p2)
                in_rows.append(t_term)
                hdh = c["h"][:, psl[r]] * c["dh"][:, psl[r]]
                tot = jnp.sum(t_term, axis=1, keepdims=True) + c["dcy"][r:r + 1] * jnp.sum(jnp.where(mine[r], hdh, 0.0), keepdims=True)
                tots.append(jnp.broadcast_to(tot, (1, CHUNK)))
            for p in range(HPG // 2):
                dx_ref[0, :, g * GW + p * LANES:g * GW + (p + 1) * LANES] = jnp.where(first, c["dxs"][2 * p], c["dxs"][2 * p + 1])
                new = c["dh"][:, psl[2 * p]] * jnp.where(first, c["dcy"][2 * p:2 * p + 1], c["dcy"][2 * p + 1:2 * p + 2])
                if has_y:
                    new = new + jnp.where(first, c["news"][2 * p], c["news"][2 * p + 1])
                dh_scr[g, :, psl[2 * p]] = new
            db = dbt.T
            if has_y:
                dcbt_bf = dcbt.astype(BF16)
                db = db + _dot(dcbt_bf, c["cm"].astype(BF16))
                dc_ref[0, :, g * D_STATE:(g + 1) * D_STATE] = dct.T + _dot_tn(dcbt_bf, c["bm"].astype(BF16))
            db_ref[0, :, g * D_STATE:(g + 1) * D_STATE] = db
            s_row = _stack_rows(in_rows)
            lane = lax.broadcasted_iota(jnp.int32, (HPG, CHUNK), 1)
            dcs = jnp.where(lane == last, _stack_rows(tots), 0.0)
            if has_y:
                s_row = s_row + _cols_to_rows(in_cols)
                dcs = dcs + _stack_rows(out_rows)
            dcs = dcs - s_row
            da = _cumsum_lanes(dcs, not reverse)
            ddt = da * c["A"] + jnp.where(c["dt"] > 0.0, s_row / c["dt"], 0.0)
            ddtraw = ddt * _sigmoid(c["dtraw"] + bias_ref[g])
            ddt_ref[0, g * HPG:(g + 1) * HPG] = ddtraw
            dbias_ref[0, g] += jnp.sum(ddtraw, axis=1, keepdims=True)
            dalog_ref[0, g] += jnp.sum(da * c["dt"], axis=1, keepdims=True) * c["A"]

        @pl.when(s == nc - 1)
        def _():
            dh0_ref[0] = dh_scr[...]

    nb = D_INNER // (GPS * D_STATE)
    cidx = lambda b, g, s: (b, chunk_of(s), g)
    hidx = lambda b, g, s: (b, g, 0, 0)
    in_specs = [
        pl.BlockSpec((1, GPS * HPG, CHUNK), lambda b, g, s: (b, blk0 + g, chunk_of(s))),
        pl.BlockSpec((GPS, HPG, 1), lambda b, g, s: (blk0 + g, 0, 0)),
        pl.BlockSpec((GPS, HPG, 1), lambda b, g, s: (blk0 + g, 0, 0)),
        pl.BlockSpec((1, CHUNK, GPS * GW), cidx),
        pl.BlockSpec((1, CHUNK, GPS * D_STATE), lambda b, g, s: (b, chunk_of(s), nb + g)),
        pl.BlockSpec((1, GPS, 1, D_STATE, GW), lambda b, g, s: (b, g, chunk_of(s), 0, 0)),
        pl.BlockSpec((1, GPS, D_STATE, GW), hidx),
    ]
    args = [dtT, bias, alog, xbc, xbc, h_start, dh_final]
    if has_y:
        in_specs += [pl.BlockSpec((1, CHUNK, GPS * D_STATE), lambda b, g, s: (b, chunk_of(s), nb + N_BC // GPS + g)),
                     pl.BlockSpec((1, CHUNK, GPS * GW), cidx)]
        args += [xbc, dy]
    out_shape = [jax.ShapeDtypeStruct((Bn, L, D_INNER), F32), jax.ShapeDtypeStruct((Bn, L, N_BC * D_STATE), F32),
                 jax.ShapeDtypeStruct((Bn, N_HEADS, L), F32), jax.ShapeDtypeStruct((Bn, N_BC, HPG, 1), F32),
                 jax.ShapeDtypeStruct((Bn, N_BC, HPG, 1), F32), jax.ShapeDtypeStruct((Bn, N_BC, D_STATE, GW), F32)]
    out_specs = [pl.BlockSpec((1, CHUNK, GPS * GW), cidx), pl.BlockSpec((1, CHUNK, GPS * D_STATE), cidx),
                 pl.BlockSpec((1, GPS * HPG, CHUNK), lambda b, g, s: (b, g, chunk_of(s))),
                 pl.BlockSpec((1, GPS, HPG, 1), hidx), pl.BlockSpec((1, GPS, HPG, 1), hidx), pl.BlockSpec((1, GPS, D_STATE, GW), hidx)]
    if has_y:
        out_shape.append(jax.ShapeDtypeStruct((Bn, L, N_BC * D_STATE), F32))
        out_specs.append(pl.BlockSpec((1, CHUNK, GPS * D_STATE), cidx))
    res = pl.pallas_call(
        kern, name=name, out_shape=out_shape, grid=(Bn, N_BC // GPS, nc), in_specs=in_specs, out_specs=out_specs,
        scratch_shapes=[pltpu.VMEM((GPS, D_STATE, GW), F32)],
        compiler_params=_cp(("arbitrary", "arbitrary", "arbitrary")))(*args)
    dxs, db, ddt, dbias, dalog, dh0 = res[:6]
    return dxs, db, (res[6] if has_y else None), ddt, dbias, dalog, dh0


def _dot_split2(v, sel):
    hi = v.astype(BF16)
    mid = (v - hi.astype(F32)).astype(BF16)
    return _dot(hi, sel) + _dot(mid, sel)


def ssd_tables():
    lane = jnp.arange(LANES)[:, None]
    col = jnp.arange(2 * GW)[None, :]
    expand = (lane == jnp.where(col < GW, HPG + col // HEAD_DIM, 2 * HPG + (col - GW) // HEAD_DIM)).astype(BF16)
    ch = jnp.arange(GW)[:, None] // HEAD_DIM
    out = jnp.arange(2 * LANES)[None, :]
    seg = ((out == ch) | (out == LANES + HPG + ch)).astype(BF16)
    return expand, seg


def _dc_lanes(dc, first):
    return jnp.concatenate([jnp.where(first, dc[2 * p:2 * p + 1], dc[2 * p + 1:2 * p + 2]) for p in range(HPG // 2)], axis=1)


def ssd_fwd2(name, dtT, bias, alog, xbc, h0, tables, direction, with_y):
    Bn, L = xbc.shape[:2]
    nc = L // CHUNK
    reverse = direction == 1
    rowblk = direction * N_BC
    expand = tables[0]

    def chunk_of(s):
        return (nc - 1 - s) if reverse else s

    def kern(dt_ref, bias_ref, alog_ref, x_ref, b_ref, c_ref, h0_ref, xp_ref, *rest):
        if with_y:
            y_ref, hs_ref, hf_ref, h_scr = rest
        else:
            hs_ref, hf_ref, h_scr = rest
        s = pl.program_id(2)

        @pl.when(s == 0)
        def _():
            h_scr[...] = h0_ref[0, 0]

        dt, _, cs, total = _ssd_scalars(dt_ref[0], bias_ref[0], alog_ref[0], reverse)
        u = cs - jnp.log(dt)
        dtt = jnp.exp(total - u)
        cols = _rows_to_cols(jnp.concatenate([cs, dtt, jnp.exp(cs)], axis=0))
        wide = _dot_split2(cols, xp_ref[...])
        dtt_x, e_x = wide[:, 0:GW], wide[:, GW:]
        first = lax.broadcasted_iota(jnp.int32, (1, LANES), 1) < HEAD_DIM
        x = x_ref[0]
        x_bf = x.astype(BF16)
        bm = b_ref[0]
        h = h_scr[...]
        hs_ref[0, 0, 0] = h
        st = _dot(bm.T.astype(BF16), (x * dtt_x).astype(BF16))
        h_scr[...] = h * _dc_lanes(jnp.exp(total), first) + st
        if with_y:
            cm = c_ref[0].astype(BF16)
            cb = _dot_nt(cm, bm.astype(BF16))
            yoff = _dot(cm, h.astype(BF16)) * e_x
            keep = _tri_mask(False, reverse)
            wfs = []
            for r in range(HPG):
                cs_col = jnp.broadcast_to(cols[:, r:r + 1], (CHUNK, LANES))
                wfs.append((cb * jnp.exp(jnp.where(keep, cs_col - u[r:r + 1], -jnp.inf))).astype(BF16))
            yd = [_dot(wfs[r], x_bf[:, (r // 2) * LANES:(r // 2 + 1) * LANES]) for r in range(HPG)]
            for p in range(HPG // 2):
                psl = slice(p * LANES, (p + 1) * LANES)
                y_ref[0, :, psl] = jnp.where(first, yd[2 * p], yd[2 * p + 1]) + yoff[:, psl]

        @pl.when(s == nc - 1)
        def _():
            hf_ref[0, 0] = h_scr[...]

    in_specs = [
        pl.BlockSpec((1, HPG, CHUNK), lambda b, g, s: (b, rowblk + g, chunk_of(s))),
        pl.BlockSpec((1, HPG, 1), lambda b, g, s: (rowblk + g, 0, 0)),
        pl.BlockSpec((1, HPG, 1), lambda b, g, s: (rowblk + g, 0, 0)),
        pl.BlockSpec((1, CHUNK, GW), lambda b, g, s: (b, chunk_of(s), g)),
        pl.BlockSpec((1, CHUNK, D_STATE), lambda b, g, s: (b, chunk_of(s), D_INNER // D_STATE + g)),
        pl.BlockSpec((1, CHUNK, D_STATE), lambda b, g, s: (b, chunk_of(s), D_INNER // D_STATE + N_BC + g)),
        pl.BlockSpec((1, 1, D_STATE, GW), lambda b, g, s: (b, g, 0, 0)),
        pl.BlockSpec(expand.shape, lambda b, g, s: (0, 0)),
    ]
    out_shape, out_specs = [], []
    if with_y:
        out_shape.append(jax.ShapeDtypeStruct((Bn, L, D_INNER), F32))
        out_specs.append(pl.BlockSpec((1, CHUNK, GW), lambda b, g, s: (b, chunk_of(s), g)))
    out_shape += [jax.ShapeDtypeStruct((Bn, N_BC, nc, D_STATE, GW), F32), jax.ShapeDtypeStruct((Bn, N_BC, D_STATE, GW), F32)]
    out_specs += [pl.BlockSpec((1, 1, 1, D_STATE, GW), lambda b, g, s: (b, g, chunk_of(s), 0, 0)),
                  pl.BlockSpec((1, 1, D_STATE, GW), lambda b, g, s: (b, g, 0, 0))]
    return pl.pallas_call(
        kern, name=name, out_shape=out_shape, grid=(Bn, N_BC, nc), in_specs=in_specs, out_specs=out_specs,
        scratch_shapes=[pltpu.VMEM((D_STATE, GW), F32)],
        compiler_params=_cp(("arbitrary", "arbitrary", "arbitrary")))(dtT, bias, alog, xbc, xbc, xbc, h0, expand)


def ssd_bwd2(name, dtT, bias, alog, xbc, h_start, dy, dh_final, tables, direction):
    Bn, L = xbc.shape[:2]
    nc = L // CHUNK
    reverse = direction == 1
    rowblk = direction * N_BC
    has_y = dy is not None
    last = 0 if reverse else CHUNK - 1
    expand, seg = tables

    def chunk_of(s):
        return s if reverse else (nc - 1 - s)

    def kern(*refs):
        if has_y:
            (dt_ref, bias_ref, alog_ref, x_ref, b_ref, hs_ref, dhf_ref, xp_ref, seg_ref, c_ref, dy_ref,
             dx_ref, db_ref, ddt_ref, dbias_ref, dalog_ref, dh0_ref, dc_ref, dh_scr) = refs
        else:
            (dt_ref, bias_ref, alog_ref, x_ref, b_ref, hs_ref, dhf_ref, xp_ref, seg_ref,
             dx_ref, db_ref, ddt_ref, dbias_ref, dalog_ref, dh0_ref, dh_scr) = refs
        s = pl.program_id(2)

        @pl.when(s == 0)
        def _():
            dh_scr[...] = dhf_ref[0, 0]
            dbias_ref[...] = jnp.zeros(dbias_ref.shape, F32)
            dalog_ref[...] = jnp.zeros(dalog_ref.shape, F32)

        first = lax.broadcasted_iota(jnp.int32, (1, LANES), 1) < HEAD_DIM
        heads = range(HPG)
        psl = [slice((r // 2) * LANES, (r // 2 + 1) * LANES) for r in heads]
        x = x_ref[0]
        bm = b_ref[0].astype(BF16)
        h = hs_ref[0, 0, 0]
        dh = dh_scr[...]
        dh_bf = dh.astype(BF16)
        bdh = _dot(bm, dh_bf)
        if has_y:
            cm = c_ref[0].astype(BF16)
            dyv = dy_ref[0]
            dy_bf = dyv.astype(BF16)
            x_bf = x.astype(BF16)
            h_bf = h.astype(BF16)
            cbt = _dot_nt(bm, cm)
            ch = _dot(cm, h_bf)
            zeros_bf = jnp.zeros((CHUNK, LANES), BF16)
            gts = [_dot_nt(x_bf[:, psl[r]], jnp.where(first if r % 2 == 0 else jnp.logical_not(first), dy_bf[:, psl[r]], zeros_bf))
                   for r in heads]
            ct_bf = c_ref[0].T.astype(BF16)
        dtraw = dt_ref[0]
        dt, A, cs, total = _ssd_scalars(dtraw, bias_ref[0], alog_ref[0], reverse)
        u = cs - jnp.log(dt)
        dtt = jnp.exp(total - u)
        dcy = jnp.exp(total)
        cols = _rows_to_cols(jnp.concatenate([u, dtt, jnp.exp(cs)], axis=0))
        wide = _dot_split2(cols, xp_ref[...])
        dtt_x, e_x = wide[:, 0:GW], wide[:, GW:]
        term2 = bdh * dtt_x
        dbt = _dot_nt(dh_bf, (x * dtt_x).astype(BF16))
        sums = _dot_split2(term2 * x, seg_ref[:, LANES:])
        new_dh = dh * _dc_lanes(dcy, first)
        if has_y:
            dye = dyv * e_x
            dye_bf = dye.astype(BF16)
            dct = _dot_nt(h_bf, dye_bf)
            new_dh = new_dh + _dot(ct_bf, dye_bf)
            sums = sums + _dot_split2(ch * dye, seg_ref[:, 0:LANES])
            keep = _tri_mask(True, reverse)
            ets = []
            for r in heads:
                u_col = jnp.broadcast_to(cols[:, r:r + 1], (CHUNK, LANES))
                ets.append(jnp.exp(jnp.where(keep, cs[r:r + 1] - u_col, -jnp.inf)))
            wts = [(cbt * ets[r]).astype(BF16) for r in heads]
            dxd = [_dot(wts[r], dy_bf[:, psl[r]]) for r in heads]
            dcbt = jnp.zeros((CHUNK, CHUNK), F32)
            out_rows, in_cols = [], []
            for r in heads:
                pt = gts[r] * ets[r]
                dcbt = dcbt + pt
                mt = pt * cbt
                out_rows.append(_colsum(mt))
                in_cols.append(jnp.sum(mt, axis=1, keepdims=True))
            for p in range(HPG // 2):
                dx_ref[0, :, psl[2 * p]] = jnp.where(first, dxd[2 * p], dxd[2 * p + 1]) + term2[:, psl[2 * p]]
            dcbt_bf = dcbt.astype(BF16)
            db_ref[0] = dbt.T + _dot(dcbt_bf, cm)
            dc_ref[0] = dct.T + _dot_tn(dcbt_bf, bm)
        else:
            dx_ref[0] = term2
            db_ref[0] = dbt.T
        dh_scr[...] = new_dh
        sums_t = sums.T
        s_row = sums_t[HPG:2 * HPG]
        hdh = _colsum(h * dh)
        lanes_w = lax.broadcasted_iota(jnp.int32, (1, GW), 1)
        hd = _stack_rows([jnp.sum(jnp.where(lanes_w // HEAD_DIM == r, hdh, 0.0), axis=1, keepdims=True) for r in range(HPG)])
        tot = jnp.sum(s_row, axis=1, keepdims=True) + dcy * hd
        lane = lax.broadcasted_iota(jnp.int32, (HPG, CHUNK), 1)
        dcs = jnp.where(lane == last, tot, 0.0)
        if has_y:
            s_row = s_row + _cols_to_rows(in_cols)
            dcs = dcs + _stack_rows(out_rows) + sums_t[0:HPG]
        dcs = dcs - s_row
        da = _cumsum_lanes(dcs, not reverse)
        ddt = da * A + jnp.where(dt > 0.0, s_row / dt, 0.0)
        ddtraw = ddt * _sigmoid(dtraw + bias_ref[0])
        ddt_ref[0] = ddtraw
        dbias_ref[0, 0] += jnp.sum(ddtraw, axis=1, keepdims=True)
        dalog_ref[0, 0] += jnp.sum(da * dt, axis=1, keepdims=True) * A

        @pl.when(s == nc - 1)
        def _():
            dh0_ref[0, 0] = dh_scr[...]

    cidx = lambda b, g, s: (b, chunk_of(s), g)
    hidx = lambda b, g, s: (b, g, 0, 0)
    in_specs = [
        pl.BlockSpec((1, HPG, CHUNK), lambda b, g, s: (b, rowblk + g, chunk_of(s))),
        pl.BlockSpec((1, HPG, 1), lambda b, g, s: (rowblk + g, 0, 0)),
        pl.BlockSpec((1, HPG, 1), lambda b, g, s: (rowblk + g, 0, 0)),
        pl.BlockSpec((1, CHUNK, GW), cidx),
        pl.BlockSpec((1, CHUNK, D_STATE), lambda b, g, s: (b, chunk_of(s), D_INNER // D_STATE + g)),
        pl.BlockSpec((1, 1, 1, D_STATE, GW), lambda b, g, s: (b, g, chunk_of(s), 0, 0)),
        pl.BlockSpec((1, 1, D_STATE, GW), hidx),
        pl.BlockSpec(expand.shape, lambda b, g, s: (0, 0)),
        pl.BlockSpec(seg.shape, lambda b, g, s: (0, 0)),
    ]
    args = [dtT, bias, alog, xbc, xbc, h_start, dh_final, expand, seg]
    if has_y:
        in_specs += [pl.BlockSpec((1, CHUNK, D_STATE), lambda b, g, s: (b, chunk_of(s), D_INNER // D_STATE + N_BC + g)),
                     pl.BlockSpec((1, CHUNK, GW), cidx)]
        args += [xbc, dy]
    out_shape = [jax.ShapeDtypeStruct((Bn, L, D_INNER), F32), jax.ShapeDtypeStruct((Bn, L, N_BC * D_STATE), F32),
                 jax.ShapeDtypeStruct((Bn, N_HEADS, L), F32), jax.ShapeDtypeStruct((Bn, N_BC, HPG, 1), F32),
                 jax.ShapeDtypeStruct((Bn, N_BC, HPG, 1), F32), jax.ShapeDtypeStruct((Bn, N_BC, D_STATE, GW), F32)]
    out_specs = [pl.BlockSpec((1, CHUNK, GW), cidx), pl.BlockSpec((1, CHUNK, D_STATE), cidx),
                 pl.BlockSpec((1, HPG, CHUNK), lambda b, g, s: (b, g, chunk_of(s))),
                 pl.BlockSpec((1, 1, HPG, 1), hidx), pl.BlockSpec((1, 1, HPG, 1), hidx), pl.BlockSpec((1, 1, D_STATE, GW), hidx)]
    if has_y:
        out_shape.append(jax.ShapeDtypeStruct((Bn, L, N_BC * D_STATE), F32))
        out_specs.append(pl.BlockSpec((1, CHUNK, D_STATE), cidx))
    res = pl.pallas_call(
        kern, name=name, out_shape=out_shape, grid=(Bn, N_BC, nc), in_specs=in_specs, out_specs=out_specs,
        scratch_shapes=[pltpu.VMEM((D_STATE, GW), F32)],
        compiler_params=_cp(("arbitrary", "arbitrary", "arbitrary")))(*args)
    dxs, db, ddt, dbias, dalog, dh0 = res[:6]
    return dxs, db, (res[6] if has_y else None), ddt, dbias, dalog, dh0


def _group_mean(v):
    gw = D_INNER // N_BC
    parts = [jnp.broadcast_to(jnp.mean(v[:, g * gw:(g + 1) * gw], axis=-1, keepdims=True), (v.shape[0], gw)) for g in range(N_BC)]
    return jnp.concatenate(parts, axis=1)


def gated_norm_fwd(name, y_f, y_b, xs_src, z, dskip_lanes, w_norm):
    def body(yf, yb, xs, z, dsk, w):
        u = (yf + yb + dsk * xs) * _silu(z)
        r = lax.rsqrt(_group_mean(u * u) + NORM_EPS)
        return u * r * w

    return tok_call(name, body, [y_f, y_b, xs_src, z], [], [dskip_lanes, w_norm], [(D_INNER, BF16)], [], [])[0]


def _dot_exact01(v, sel):
    hi, mid, lo = _split3(v)
    return _dot(hi, sel) + _dot(mid, sel) + _dot(lo, sel)


def gated_norm_bwd(name, y_f, y_b, xs_src, z, d_out, dskip_lanes, w_norm, head_sel):
    def body(yf, yb, xs, z, do, dsk, w, sel):
        y = yf + yb + dsk * xs
        sz = _silu(z)
        u = y * sz
        r = lax.rsqrt(_group_mean(u * u) + NORM_EPS)
        duh = do * w
        du = r * (duh - u * (r * r) * _group_mean(duh * u))
        dy = du * sz
        dz = du * y * _dsilu(z)
        dsk_heads = _dot_exact01(jnp.broadcast_to(_colsum(dy * xs), (8, D_INNER)), sel)
        return dy, dz, _colsum(do * u * r), dsk_heads

    return tok_call(name, body, [y_f, y_b, xs_src, z, d_out], [], [dskip_lanes, w_norm, head_sel],
                    [(D_INNER, F32), (D_INNER, BF16)], [], [(1, D_INNER), (8, LANES)], tm=128)


def merge_fwd(name, y_pool, y_ssd, gatepre, x, target, gate, b_merge, norm_post, w_pp, w_ps, w_out):
    def body(yp, ys, gp, x, tgt, gate, bm, wpost, w_pp, w_ps, w_out):
        p1 = _dot(yp, w_pp)
        p2 = _dot(ys, w_ps)
        gates = _sigmoid(gp + bm)
        merged = gates[:, :D] * p1 + gates[:, D:] * p2
        out = _dot(merged.astype(BF16), w_out)
        r = _rms_r(out)
        outr = out * r
        nq = outr * wpost
        err = x + gate * nq - tgt
        loss = 0.5 * jnp.sum(jnp.mean(err * err, axis=-1, keepdims=True), keepdims=True).reshape(1, 1)
        g = err * (1.0 / D)
        dnq = g * gate
        dout = _rms_bwd(dnq * wpost, out, r)
        return merged, p1, p2, dout, g, _colsum(g * nq), _colsum(dnq * outr), jnp.broadcast_to(loss, (1, LANES))

    return tok_call(name, body, [y_pool, y_ssd, gatepre, x, target], [gate], [b_merge, norm_post, w_pp, w_ps, w_out],
                    [(D, BF16), (D, F32), (D, F32), (D, BF16), (D, F32)], [D], [(1, D), (1, LANES)])


def merge_bwd(name, dout, gatepre, p1, p2, b_merge, w_pp, w_ps, w_out):
    def body(dout, gp, p1, p2, bm, w_pp, w_ps, w_out):
        dmerged = _dot_nt(dout, w_out)
        gates = _sigmoid(gp + bm)
        g1, g2 = gates[:, :D], gates[:, D:]
        dp1 = (dmerged * g1).astype(BF16)
        dp2 = (dmerged * g2).astype(BF16)
        dgp = jnp.concatenate([dmerged * p1 * g1 * (1.0 - g1), dmerged * p2 * g2 * (1.0 - g2)], axis=1)
        return dp1, dp2, dgp, _dot_nt(dp1, w_pp), _dot_nt(dp2, w_ps), _colsum(dgp)

    return tok_call(name, body, [dout, gatepre, p1, p2], [], [b_merge, w_pp, w_ps, w_out],
                    [(D, BF16), (D, BF16), (2 * D, BF16), (D, F32), (D_INNER, F32)], [], [(1, 2 * D)])


def _adamw_math(w, g, m, v):
    m = ADAM_B1 * m + (1.0 - ADAM_B1) * g
    v = ADAM_B2 * v + (1.0 - ADAM_B2) * (g * g)
    m_hat = m / (1.0 - ADAM_B1 ** ADAM_STEP)
    v_hat = v / (1.0 - ADAM_B2 ** ADAM_STEP)
    delta = -ADAM_LR * (m_hat / (jnp.sqrt(v_hat) + ADAM_EPS) + ADAM_WD * w)
    return delta, m, v


def adamw(name, w, g, m, v, tr=256):
    R, C = w.shape
    tr = min(tr, R)
    assert R % tr == 0

    def body(w_ref, g_ref, m_ref, v_ref, d_ref, nm_ref, nv_ref):
        d, nm, nv = _adamw_math(w_ref[...], g_ref[...], m_ref[...], v_ref[...])
        d_ref[...] = d
        nm_ref[...] = nm
        nv_ref[...] = nv

    spec = pl.BlockSpec((tr, C), lambda i: (i, 0))
    return pl.pallas_call(
        body, name=name, out_shape=[jax.ShapeDtypeStruct((R, C), F32)] * 3, grid=(R // tr,),
        in_specs=[spec] * 4, out_specs=[spec] * 3, compiler_params=_cp(("parallel",)))(w, g, m, v)


def _me():
    return lax.axis_index("x"), lax.axis_index("y"), lax.axis_index("c")


def all_gather_small(name, v):
    R, C = v.shape

    def body(v_ref, out_ref, send_sems, recv_sems, local_sem):
        x, y, c = _me()
        me = 4 * x + 2 * y + c
        mine = pltpu.make_async_copy(v_ref, out_ref.at[me], local_sem)
        mine.start()
        copies = []
        for d in range(1, N_DEV):
            dx, dy, dc = d // 4, (d // 2) % 2, d % 2
            px, py, pc = x ^ dx, y ^ dy, c ^ dc
            copies.append(pltpu.make_async_remote_copy(
                src_ref=v_ref, dst_ref=out_ref.at[me], send_sem=send_sems.at[d - 1], recv_sem=recv_sems.at[d - 1],
                device_id=(px, py, pc), device_id_type=MESH))
        for cp in copies:
            cp.start()
        for d in range(1, N_DEV):
            dx, dy, dc = d // 4, (d // 2) % 2, d % 2
            peer = 4 * (x ^ dx) + 2 * (y ^ dy) + (c ^ dc)
            pltpu.make_async_remote_copy(
                src_ref=v_ref, dst_ref=out_ref.at[peer], send_sem=send_sems.at[d - 1], recv_sem=recv_sems.at[d - 1],
                device_id=(x ^ dx, y ^ dy, c ^ dc), device_id_type=MESH).wait_recv()
        for cp in copies:
            cp.wait_send()
        mine.wait()

    return pl.pallas_call(
        body, name=name, out_shape=jax.ShapeDtypeStruct((N_DEV, R, C), F32),
        in_specs=[pl.BlockSpec(memory_space=pltpu.VMEM)], out_specs=pl.BlockSpec(memory_space=pltpu.VMEM),
        scratch_shapes=[pltpu.SemaphoreType.DMA((N_DEV - 1,)), pltpu.SemaphoreType.DMA((N_DEV - 1,)), pltpu.SemaphoreType.DMA],
        compiler_params=pltpu.CompilerParams(vmem_limit_bytes=VMEM_LIMIT))(v)


def all_gather_chips(name, shard):
    R, C = shard.shape
    half = R // 2
    assert R % 32 == 0

    def body(s_ref, out_ref, send_sems, recv_sems):
        x, y, c = _me()
        chips = [(1 - x, y), (x, 1 - y), (1 - x, 1 - y)]

        def rows(chip, hc):
            return out_ref.at[2 * chip[0] + chip[1], pl.ds(hc * half, half), :]

        first = [pltpu.make_async_remote_copy(
            src_ref=s_ref.at[pl.ds(c * half, half), :], dst_ref=rows((x, y), c), send_sem=send_sems.at[j],
            recv_sem=recv_sems.at[j], device_id=(*chip, c), device_id_type=MESH) for j, chip in enumerate(chips)]
        for cp in first:
            cp.start()
        passed = [pltpu.make_async_remote_copy(
            src_ref=rows(chip, c), dst_ref=rows(chip, c), send_sem=send_sems.at[3 + j], recv_sem=recv_sems.at[3 + j],
            device_id=(x, y, 1 - c), device_id_type=MESH) for j, chip in enumerate(chips)]
        for j, chip in enumerate(chips):
            pltpu.make_async_remote_copy(
                src_ref=rows(chip, c), dst_ref=rows(chip, c), send_sem=send_sems.at[j], recv_sem=recv_sems.at[j],
                device_id=(*chip, c), device_id_type=MESH).wait_recv()
            passed[j].start()
        for j, chip in enumerate(chips):
            pltpu.make_async_remote_copy(
                src_ref=rows(chip, 1 - c), dst_ref=rows(chip, 1 - c), send_sem=send_sems.at[3 + j], recv_sem=recv_sems.at[3 + j],
                device_id=(x, y, 1 - c), device_id_type=MESH).wait_recv()
        for cp in first + passed:
            cp.wait_send()

    out = pl.pallas_call(
        body, name=name, out_shape=jax.ShapeDtypeStruct((N_CHIPS, R, C), shard.dtype),
        in_specs=[pl.BlockSpec(memory_space=pl.ANY)], out_specs=pl.BlockSpec(memory_space=pl.ANY),
        scratch_shapes=[pltpu.SemaphoreType.DMA((6,)), pltpu.SemaphoreType.DMA((6,))],
        compiler_params=pltpu.CompilerParams(vmem_limit_bytes=VMEM_LIMIT))(shard)
    chip = 2 * lax.axis_index("x") + lax.axis_index("y")
    return lax.dynamic_update_index_in_dim(out, shard, chip, 0)


def sibling_swap(name, v):
    def body(v_ref, out_ref, send_sem, recv_sem):
        x, y, c = _me()
        cp = pltpu.make_async_remote_copy(src_ref=v_ref, dst_ref=out_ref, send_sem=send_sem, recv_sem=recv_sem,
                                          device_id=(x, y, 1 - c), device_id_type=MESH)
        cp.start()
        cp.wait()

    return pl.pallas_call(
        body, name=name, out_shape=jax.ShapeDtypeStruct(v.shape, v.dtype),
        in_specs=[pl.BlockSpec(memory_space=pl.ANY)], out_specs=pl.BlockSpec(memory_space=pl.ANY),
        scratch_shapes=[pltpu.SemaphoreType.DMA, pltpu.SemaphoreType.DMA],
        compiler_params=pltpu.CompilerParams(vmem_limit_bytes=VMEM_LIMIT))(v)


def sibling_share(name, v):
    def body(v_ref, out_ref, send_sem, recv_sem, local_sem):
        x, y, c = _me()
        mine = pltpu.make_async_copy(v_ref, out_ref.at[c], local_sem)
        mine.start()
        cp = pltpu.make_async_remote_copy(src_ref=v_ref, dst_ref=out_ref.at[c], send_sem=send_sem, recv_sem=recv_sem,
                                          device_id=(x, y, 1 - c), device_id_type=MESH)
        cp.start()
        pltpu.make_async_remote_copy(src_ref=v_ref, dst_ref=out_ref.at[1 - c], send_sem=send_sem, recv_sem=recv_sem,
                                     device_id=(x, y, 1 - c), device_id_type=MESH).wait_recv()
        cp.wait_send()
        mine.wait()

    return pl.pallas_call(
        body, name=name, out_shape=jax.ShapeDtypeStruct((2, *v.shape), v.dtype),
        in_specs=[pl.BlockSpec(memory_space=pl.ANY)], out_specs=pl.BlockSpec(memory_space=pl.ANY),
        scratch_shapes=[pltpu.SemaphoreType.DMA, pltpu.SemaphoreType.DMA, pltpu.SemaphoreType.DMA],
        compiler_params=pltpu.CompilerParams(vmem_limit_bytes=VMEM_LIMIT))(v)


def chip_exchange(name, parts):
    def body(p_ref, out_ref, send_sems, recv_sems):
        x, y, c = _me()
        k = 2 * x + y
        chips = [(1 - x, y), (x, 1 - y), (1 - x, 1 - y)]
        sends = [pltpu.make_async_remote_copy(
            src_ref=p_ref.at[2 * chip[0] + chip[1]], dst_ref=out_ref.at[k], send_sem=send_sems.at[j], recv_sem=recv_sems.at[j],
            device_id=(*chip, c), device_id_type=MESH) for j, chip in enumerate(chips)]
        for cp in sends:
            cp.start()
        for j, chip in enumerate(chips):
            pltpu.make_async_remote_copy(
                src_ref=p_ref.at[k], dst_ref=out_ref.at[2 * chip[0] + chip[1]], send_sem=send_sems.at[j], recv_sem=recv_sems.at[j],
                device_id=(*chip, c), device_id_type=MESH).wait_recv()
        for cp in sends:
            cp.wait_send()

    out = pl.pallas_call(
        body, name=name, out_shape=jax.ShapeDtypeStruct(parts.shape, parts.dtype),
        in_specs=[pl.BlockSpec(memory_space=pl.ANY)], out_specs=pl.BlockSpec(memory_space=pl.ANY),
        scratch_shapes=[pltpu.SemaphoreType.DMA((3,)), pltpu.SemaphoreType.DMA((3,))],
        compiler_params=pltpu.CompilerParams(vmem_limit_bytes=VMEM_LIMIT))(parts)
    chip = 2 * lax.axis_index("x") + lax.axis_index("y")
    own = lax.dynamic_index_in_dim(parts, chip, 0, keepdims=True)
    return lax.dynamic_update_slice_in_dim(out, own, chip, 0)


def _row_tile(rows, cap, mult=8):
    best = None
    for t in range(mult, min(rows, cap) + 1, mult):
        if rows % t == 0:
            best = t
    assert best is not None, rows
    return best


def add_arrays(name, arrs, out_dtype=F32):
    shape = arrs[0].shape
    C = shape[-1]
    flat = [a.reshape(-1, C) for a in arrs]
    R = flat[0].shape[0]
    narrow = out_dtype == BF16 or any(a.dtype == BF16 for a in arrs)
    tr = _row_tile(R, 2048 if len(arrs) <= 2 else 1024, 16 if narrow else 8)
    n = len(flat)

    def body(*refs):
        acc = refs[0][...].astype(F32)
        for r in refs[1:n]:
            acc = acc + r[...].astype(F32)
        refs[n][...] = acc.astype(out_dtype)

    spec = pl.BlockSpec((tr, C), lambda i: (i, 0))
    out = pl.pallas_call(
        body, name=name, out_shape=jax.ShapeDtypeStruct((R, C), out_dtype), grid=(R // tr,),
        in_specs=[spec] * n, out_specs=spec, compiler_params=_cp(("parallel",)))(*flat)
    return out.reshape(shape)


def reduce_scatter_chips(slabs):
    _, R, C = slabs.shape
    half = R // 2
    c = lax.axis_index("c")
    k = 2 * lax.axis_index("x") + lax.axis_index("y")
    halves = slabs.reshape(N_CHIPS, 2, half, C)
    own = lax.dynamic_index_in_dim(halves, c, axis=1, keepdims=False)
    other = lax.dynamic_index_in_dim(halves, 1 - c, axis=1, keepdims=False)
    from_sibling = sibling_swap("rs_sibling_halves", other)
    del k
    return add_arrays("rs_add_sibling", [own, from_sibling], out_dtype=BF16)


def reduce_scatter_finish(landed):
    c = lax.axis_index("c")
    mine = add_arrays("rs_add_chips", [landed[j] for j in range(N_CHIPS)])
    sib = sibling_swap("rs_sibling_result", mine)
    return jnp.concatenate([jnp.where(c == 0, mine, sib), jnp.where(c == 0, sib, mine)], axis=0)


def ada_mod_shard(cond_all, w_ada_shard, b_ada_shard):
    def body(c_ref, w_ref, b_ref, o_ref):
        o_ref[...] = _dot(_silu(c_ref[...]).astype(BF16), w_ref[...].astype(BF16)) + b_ref[...]

    return pl.pallas_call(body, name="ada_mod_shard", out_shape=jax.ShapeDtypeStruct((cond_all.shape[0], w_ada_shard.shape[1]), F32),
                          compiler_params=_cp())(cond_all, w_ada_shard, b_ada_shard)


def ada_bwd_shard(cond_all, dmod_all_shard, dmod_all, w_ada_shard, row_is_cctx):
    def body(c_ref, ds_ref, da_ref, w_ref, sel_ref, gw_ref, gb_ref, part_ref):
        sc = _silu(c_ref[...]).astype(BF16)
        gw_ref[...] = _dot_tn(sc, ds_ref[...].astype(BF16))
        gb_ref[...] = _colsum(da_ref[...])
        dc_tot = jnp.broadcast_to(_colsum(ds_ref[...] * sel_ref[...]), (8, ds_ref.shape[1]))
        part_ref[...] = _dot_nt(dc_tot.astype(BF16), w_ref[...].astype(BF16))

    n = cond_all.shape[0]
    return pl.pallas_call(
        body, name="ada_bwd_shard",
        out_shape=[jax.ShapeDtypeStruct(w_ada_shard.shape, F32), jax.ShapeDtypeStruct((1, dmod_all.shape[1]), F32),
                   jax.ShapeDtypeStruct((8, D), F32)],
        compiler_params=_cp())(cond_all, dmod_all_shard, dmod_all, w_ada_shard, row_is_cctx)


def sum_devices(name, gathered):
    def body(g_ref, o_ref):
        acc = g_ref[0]
        for d in range(1, N_DEV):
            acc = acc + g_ref[d]
        o_ref[...] = acc

    return pl.pallas_call(body, name=name, out_shape=jax.ShapeDtypeStruct(gathered.shape[1:], F32), compiler_params=_cp())(gathered)


def cctx_finish(gathered, c_ctx_row):
    def body(g_ref, c_ref, o_ref):
        acc = g_ref[0, 0:1, :]
        for k in range(1, N_CHIPS):
            acc = acc + g_ref[2 * k, 0:1, :]
        o_ref[...] = acc * _dsilu(c_ref[...])

    return pl.pallas_call(body, name="cctx_finish", out_shape=jax.ShapeDtypeStruct((1, D), F32), compiler_params=_cp())(gathered, c_ctx_row)


def _pack(parts, rows):
    flat = []
    for p in parts:
        p = p.reshape(-1)
        pad = (-p.shape[0]) % LANES
        flat.append(jnp.pad(p, (0, pad)) if pad else p)
    v = jnp.concatenate(flat)
    return jnp.pad(v, (0, rows * LANES - v.shape[0])).reshape(rows, LANES)


def _unpack(v, sizes):
    flat = v.reshape(-1)
    out, off = [], 0
    for n in sizes:
        out.append(flat[off:off + n])
        off += n + (-n) % LANES
    return out


W_SHARD_ROWS = 3456
SEG_ROWS = (0, 2320, 2576, 3088, 3344, 3408)


def kernel(x, c, ctx, c_ctx, w_ada, b_ada, norm_pre, norm_post, w_in, b_merge, pool_w, pool_scale, conv_w, conv_b, dt_bias, a_log, d_skip, ssd_norm, w_proj_pool, w_proj_ssd, w_out, loss_target, m_c_ctx, m_w_ada, m_b_ada, m_norm_pre, m_norm_post, m_w_in, m_b_merge, m_pool_w, m_pool_scale, m_conv_w, m_conv_b, m_dt_bias, m_a_log, m_d_skip, m_ssd_norm, m_w_proj_pool, m_w_proj_ssd, m_w_out, v_c_ctx, v_w_ada, v_b_ada, v_norm_pre, v_norm_post, v_w_in, v_b_merge, v_pool_w, v_pool_scale, v_conv_w, v_conv_b, v_dt_bias, v_a_log, v_d_skip, v_ssd_norm, v_w_proj_pool, v_w_proj_ssd, v_w_out):
    Bn, L, _ = x.shape
    Lc = ctx.shape[1]
    T, Tc = Bn * L, Bn * Lc
    assert Bn == 2
    ix, iy, ic = lax.axis_index("x"), lax.axis_index("y"), lax.axis_index("c")
    me = 4 * ix + 2 * iy + ic
    chip = 2 * ix + iy
    ada_cols = w_ada.shape[2]
    cw_cols = conv_w.shape[2]

    cond_own = jnp.pad(c, ((0, 8 - Bn), (0, 0))) + jnp.pad(c_ctx[None, :], ((Bn, 7 - Bn), (0, 0)))
    convw_own = jnp.pad(conv_w[0], ((0, 4), (0, D - cw_cols)))
    g1 = all_gather_small("gather_cond", jnp.concatenate([cond_own, convw_own], axis=0))
    cond_all = g1[:, 0:8].reshape(8 * N_DEV, D)
    conv_w_full = jnp.concatenate([g1[2 * k, 8:12, 0:cw_cols] for k in range(N_CHIPS)], axis=1)
    b_ada_shard = lax.dynamic_slice(b_ada, (0, chip * ada_cols), (1, ada_cols))
    g2 = all_gather_small("gather_mod", ada_mod_shard(cond_all, w_ada[0], b_ada_shard))
    mod_full = jnp.concatenate([g2[2 * k] for k in range(N_CHIPS)], axis=1)
    own = lax.dynamic_slice(mod_full, (8 * me, 0), (8, 3 * D))
    shift, scale, gate = (own[0:Bn, i * D:(i + 1) * D][:, None, :] for i in range(3))
    shift_c, scale_c = (jnp.broadcast_to(own[Bn:Bn + 1, i * D:(i + 1) * D][None], (Bn, 1, D)) for i in range(2))

    shard = jnp.concatenate([w_in[0].T, w_proj_pool[0], w_proj_ssd[0], w_out[0], pool_w[0].reshape(64, D),
                             jnp.zeros((W_SHARD_ROWS - SEG_ROWS[-1], D), F32)], axis=0).astype(BF16)
    gw = all_gather_chips("gather_weights", shard)
    w_inT = gw[:, SEG_ROWS[0]:SEG_ROWS[1]].reshape(IN_COLS, D)
    w_pp = gw[:, SEG_ROWS[1]:SEG_ROWS[2]].reshape(D, D)
    w_ps = gw[:, SEG_ROWS[2]:SEG_ROWS[3]].reshape(D_INNER, D)
    w_o = gw[:, SEG_ROWS[3]:SEG_ROWS[4]].reshape(D, D)
    pool_full = gw[:, SEG_ROWS[4]:SEG_ROWS[5]].reshape(N_CHIPS, 4, 64, POOL_GROUP).transpose(1, 0, 2, 3).reshape(D, POOL_GROUP)
    w_dt = jnp.pad(w_inT[9216:IN_COLS], ((0, LANES - 64), (0, 0)))
    seg_lo = (0, 256, 512, 768, 1024, 2048, 4096, 6144, 8192, 8704)
    seg_hi = (256, 512, 768, 1024, 2048, 4096, 6144, 8192, 8704, 9216)
    w_seg = [w_inT[lo:hi] for lo, hi in zip(seg_lo, seg_hi)] + [w_dt]

    hx = prenorm_fwd("prenorm_x", x, scale, shift, norm_pre)
    hc = prenorm_fwd("prenorm_ctx", ctx, scale_c, shift_c, norm_pre)
    hx2, hc2 = hx.reshape(T, D), hc.reshape(Tc, D)
    v = mm_nt("proj_v", hx2, w_inT[0:1024], F32).reshape(Bn, L, D)
    zp = mm_nt("proj_zpool", hx2, w_inT[1024:2048], F32).reshape(Bn, L, D)
    zs = mm_nt("proj_zssd", hx2, w_inT[2048:4096], F32).reshape(Bn, L, D_INNER)
    gp = mm_nt("proj_gate", hx2, w_inT[4096:6144], F32).reshape(Bn, L, 2 * D)
    xbc_raw = mm_nt("proj_xbc", hx2, w_inT[6144:9216], F32).reshape(Bn, L, CONV_DIM)
    dt_raw = mm_nt("proj_dt", hx2, w_dt, F32)
    xbc_raw_c = mm_nt("proj_xbc_ctx", hc2, w_inT[6144:9216], F32).reshape(Bn, Lc, CONV_DIM)
    dt_raw_c = mm_nt("proj_dt_ctx", hc2, w_dt, F32)
    dtT = dt_raw[:, :64].reshape(Bn, L, 64).transpose(0, 2, 1)
    dtT_c = dt_raw_c[:, :64].reshape(Bn, Lc, 64).transpose(0, 2, 1)
    bias3 = dt_bias.reshape(2 * N_BC, HPG, 1)
    alog3 = a_log.reshape(2 * N_BC, HPG, 1)

    xbc = conv_fwd("conv_x", xbc_raw, conv_w_full, conv_b)
    xbc_c = conv_fwd("conv_ctx", xbc_raw_c, conv_w_full, conv_b)
    zero_state = jnp.zeros((Bn, N_BC, D_STATE, GW), F32)
    tables = ssd_tables()
    ys, hs_x, hs_c, hf_x, hf_c = [], [], [], [], []
    for d in range(2):
        hsc, hfc = ssd_fwd3(f"ssd_fwd_ctx{d}", dtT_c, bias3, alog3, xbc_c, zero_state, d, False)
        y, hsx, hfx = ssd_fwd3(f"ssd_fwd_x{d}", dtT, bias3, alog3, xbc, hfc, d, True)
        ys.append(y)
        hs_x.append(hsx)
        hs_c.append(hsc)
        hf_x.append(hfx)
        hf_c.append(hfc)

    dgs = [pool_diff(f"pool_diff{g}", v, g * POOL_GROUP, g, False) for g in range(4)]
    y_pool = pool_mix_fwd("pool_mix", dgs, zp, pool_full, pool_scale)
    dskip_lanes = jnp.repeat(d_skip[0], HEAD_DIM)[None, :]
    y_ssd = gated_norm_fwd("gated_norm", ys[0], ys[1], (xbc, D_INNER), zs, dskip_lanes, ssd_norm)
    merged, p1, p2, dout, g_res, dgate, g_norm_post, loss_part = merge_fwd(
        "merge_fwd", y_pool, y_ssd, gp, x, loss_target, gate, b_merge, norm_post, w_pp, w_ps, w_o)

    dp1, dp2, dgp, dyp, dys, g_b_merge = merge_bwd("merge_bwd", dout, gp, p1, p2, b_merge, w_pp, w_ps, w_o)
    gw_o = mm_tn("gw_out", merged.reshape(T, D), dout.reshape(T, D))
    gw_pp = mm_tn("gw_proj_pool", y_pool.reshape(T, D), dp1.reshape(T, D))
    gw_ps = mm_tn("gw_proj_ssd", y_ssd.reshape(T, D_INNER), dp2.reshape(T, D))

    *dds, dzp, g_pool, g_pool_scale = pool_mix_bwd("pool_mix_bwd", dgs, zp, dyp, pool_full, pool_scale)
    dvs = [pool_diff(f"pool_diff_t{g}", dds[g], 0, g, True) for g in range(4)]

    head_sel = (jnp.arange(D_INNER)[:, None] // HEAD_DIM == jnp.arange(LANES)[None, :]).astype(BF16)
    dy, dzs, g_ssd_norm, g_dskip = gated_norm_bwd(
        "gated_norm_bwd", ys[0], ys[1], (xbc, D_INNER), zs, dys, dskip_lanes, ssd_norm, head_sel)

    dxs, dbm, dcm, ddt, dxs_c, dbm_c, ddt_c = [], [], [], [], [], [], []
    g_bias = jnp.zeros((2, N_BC, HPG, 1), F32)
    g_alog = jnp.zeros((2, N_BC, HPG, 1), F32)
    for d in range(2):
        a, b_, c_, t_, gb, ga, dh0 = ssd_bwd3(f"ssd_bwd_x{d}", dtT, bias3, alog3, xbc, hs_x[d], dy, zero_state, d)
        dxs.append(a), dbm.append(b_), dcm.append(c_), ddt.append(t_)
        ac, bc, _, tc, gbc, gac, _ = ssd_bwd3(f"ssd_bwd_ctx{d}", dtT_c, bias3, alog3, xbc_c, hs_c[d], None, dh0, d)
        dxs_c.append(ac), dbm_c.append(bc), ddt_c.append(tc)
        g_bias = g_bias.at[d].set(jnp.sum(gb, axis=0) + jnp.sum(gbc, axis=0))
        g_alog = g_alog.at[d].set(jnp.sum(ga, axis=0) + jnp.sum(gac, axis=0))

    dxr_xs, gcw_xs, gcb_xs = conv_bwd("conv_bwd_xs", xbc_raw, dxs, conv_w_full, conv_b, 0, D_INNER, scaled=(dy, dskip_lanes))
    dxr_b, gcw_b, gcb_b = conv_bwd("conv_bwd_b", xbc_raw, dbm, conv_w_full, conv_b, D_INNER, N_BC * D_STATE)
    dxr_c, gcw_c, gcb_c = conv_bwd("conv_bwd_c", xbc_raw, dcm, conv_w_full, conv_b, D_INNER + N_BC * D_STATE, N_BC * D_STATE)
    dxr_xs_c, gcw_xs_c, gcb_xs_c = conv_bwd("conv_bwd_xs_ctx", xbc_raw_c, dxs_c, conv_w_full, conv_b, 0, D_INNER)
    dxr_b_c, gcw_b_c, gcb_b_c = conv_bwd("conv_bwd_b_ctx", xbc_raw_c, dbm_c, conv_w_full, conv_b, D_INNER, N_BC * D_STATE)
    g_conv_w = jnp.concatenate([gcw_xs + gcw_xs_c, gcw_b + gcw_b_c, gcw_c], axis=1)
    g_conv_b = jnp.concatenate([gcb_xs + gcb_xs_c, gcb_b + gcb_b_c, gcb_c], axis=1)

    def dt_cols(parts, n_tok):
        t = jnp.concatenate(parts, axis=1).transpose(0, 2, 1).reshape(n_tok, 2 * N_HEADS)
        return jnp.pad(t, ((0, 0), (0, LANES - 2 * N_HEADS))).astype(BF16)

    ddt2, ddt2_c = dt_cols(ddt, T), dt_cols(ddt_c, Tc)
    segs = ([dv.reshape(T, POOL_GROUP) for dv in dvs]
            + [dzp.reshape(T, D), dzs.reshape(T, D_INNER), dgp.reshape(T, 2 * D), dxr_xs.reshape(T, D_INNER),
               dxr_b.reshape(T, N_BC * D_STATE), dxr_c.reshape(T, N_BC * D_STATE), ddt2])
    segs_c = {7: dxr_xs_c.reshape(Tc, D_INNER), 8: dxr_b_c.reshape(Tc, N_BC * D_STATE), 10: ddt2_c}
    gw_rows = []
    for i, seg in enumerate(segs):
        init = mm_tn(f"gw_in_ctx{i}", segs_c[i], hc2) if i in segs_c else None
        gw_rows.append(mm_tn(f"gw_in{i}", seg, hx2, init=init))
    gw_rows[-1] = gw_rows[-1][0:2 * N_HEADS]
    gw_inT = jnp.concatenate(gw_rows, axis=0)

    pool_slab = g_pool.reshape(4, N_CHIPS, 64, POOL_GROUP).transpose(1, 0, 2, 3).reshape(N_CHIPS, 64, D)
    slabs = jnp.concatenate([gw_inT.reshape(N_CHIPS, 2320, D), gw_pp.reshape(N_CHIPS, 256, D), gw_ps.reshape(N_CHIPS, 512, D),
                             gw_o.reshape(N_CHIPS, 256, D), pool_slab, jnp.zeros((N_CHIPS, W_SHARD_ROWS - SEG_ROWS[-1], D), F32)], axis=1)
    chip_part = reduce_scatter_chips(slabs)
    d_hx, landed = mm_nn_multi("d_hx", list(zip(segs, w_seg)), F32, tm=1024, tk=256, exchange=chip_part)
    d_hx = d_hx.reshape(Bn, L, D)
    gsh = reduce_scatter_finish(landed)
    d_hc = mm_nn_multi("d_hc", [(segs_c[i], w_seg[i]) for i in (7, 8, 10)], F32).reshape(Bn, Lc, D)

    grad_x, dscale, dshift, g_npre_x = prenorm_bwd("prenorm_bwd_x", x, d_hx, scale, norm_pre, g_res=g_res)
    _, dscale_c, dshift_c, g_npre_c = prenorm_bwd("prenorm_bwd_ctx", ctx, d_hc, scale_c, norm_pre)

    dmod_x = jnp.concatenate([dshift[:, 0], dscale[:, 0], dgate[:, 0]], axis=1)
    dmod_c = jnp.concatenate([jnp.sum(dshift_c[:, 0], axis=0, keepdims=True), jnp.sum(dscale_c[:, 0], axis=0, keepdims=True),
                              jnp.zeros((1, D), F32)], axis=1)
    dmod_own = jnp.pad(dmod_x, ((0, 8 - Bn), (0, 0))) + jnp.pad(dmod_c, ((Bn, 7 - Bn), (0, 0)))
    dmod_all = all_gather_small("gather_dmod", dmod_own).reshape(8 * N_DEV, 3 * D)
    row_is_cctx = (jnp.arange(8 * N_DEV) % 8 == Bn).astype(F32)[:, None]
    g_w_ada, g_b_ada, cpart = ada_bwd_shard(
        cond_all, lax.dynamic_slice(dmod_all, (0, chip * ada_cols), (8 * N_DEV, ada_cols)), dmod_all, w_ada[0], row_is_cctx)
    g_c_ctx = cctx_finish(all_gather_small("gather_cctx", cpart), c_ctx[None, :])

    small_sizes = (D, D, 2 * D, D, CONV_DIM, 2 * N_HEADS, 2 * N_HEADS, N_HEADS, D_INNER, 4 * CONV_DIM, 1)
    pk = _pack([g_npre_x + g_npre_c, g_norm_post, g_b_merge, g_pool_scale, g_conv_b, g_bias, g_alog, g_dskip[0, 0:N_HEADS],
                g_ssd_norm, g_conv_w, loss_part[0, 0:1]], 184)
    small = sum_devices("sum_small", all_gather_small("gather_small", pk))
    (g_norm_pre, g_norm_post_t, g_b_merge_t, g_pool_scale_t, g_conv_b_t, g_dt_bias, g_a_log, g_d_skip, g_ssd_norm_t,
     g_conv_w_t, loss) = _unpack(small, small_sizes)
    g_conv_w_shard = lax.dynamic_slice(g_conv_w_t.reshape(4, CONV_DIM), (0, chip * cw_cols), (4, cw_cols))

    g_w_in = gsh[SEG_ROWS[0]:SEG_ROWS[1]].T
    g_w_pp, g_w_ps, g_w_o = (gsh[SEG_ROWS[i]:SEG_ROWS[i + 1]] for i in (1, 2, 3))
    g_pool_w = gsh[SEG_ROWS[4]:SEG_ROWS[5]].reshape(256, POOL_GROUP)

    grads = {
        "c_ctx": g_c_ctx.reshape(c_ctx.shape), "w_ada": g_w_ada[None], "b_ada": g_b_ada, "norm_pre": g_norm_pre[None],
        "norm_post": g_norm_post_t[None], "w_in": g_w_in[None], "b_merge": g_b_merge_t[None],
        "pool_w": g_pool_w.reshape(pool_w.shape), "pool_scale": g_pool_scale_t[None], "conv_w": g_conv_w_shard[None],
        "conv_b": g_conv_b_t[None], "dt_bias": g_dt_bias.reshape(dt_bias.shape), "a_log": g_a_log.reshape(a_log.shape),
        "d_skip": g_d_skip[None], "ssd_norm": g_ssd_norm_t[None], "w_proj_pool": g_w_pp[None], "w_proj_ssd": g_w_ps[None],
        "w_out": g_w_o[None]}
    weights = dict(c_ctx=c_ctx, w_ada=w_ada, b_ada=b_ada, norm_pre=norm_pre, norm_post=norm_post, w_in=w_in, b_merge=b_merge,
                   pool_w=pool_w, pool_scale=pool_scale, conv_w=conv_w, conv_b=conv_b, dt_bias=dt_bias, a_log=a_log,
                   d_skip=d_skip, ssd_norm=ssd_norm, w_proj_pool=w_proj_pool, w_proj_ssd=w_proj_ssd, w_out=w_out)
    m_in = dict(c_ctx=m_c_ctx, w_ada=m_w_ada, b_ada=m_b_ada, norm_pre=m_norm_pre, norm_post=m_norm_post, w_in=m_w_in,
                b_merge=m_b_merge, pool_w=m_pool_w, pool_scale=m_pool_scale, conv_w=m_conv_w, conv_b=m_conv_b,
                dt_bias=m_dt_bias, a_log=m_a_log, d_skip=m_d_skip, ssd_norm=m_ssd_norm, w_proj_pool=m_w_proj_pool,
                w_proj_ssd=m_w_proj_ssd, w_out=m_w_out)
    v_in = dict(c_ctx=v_c_ctx, w_ada=v_w_ada, b_ada=v_b_ada, norm_pre=v_norm_pre, norm_post=v_norm_post, w_in=v_w_in,
                b_merge=v_b_merge, pool_w=v_pool_w, pool_scale=v_pool_scale, conv_w=v_conv_w, conv_b=v_conv_b,
                dt_bias=v_dt_bias, a_log=v_a_log, d_skip=v_d_skip, ssd_norm=v_ssd_norm, w_proj_pool=v_w_proj_pool,
                w_proj_ssd=v_w_proj_ssd, w_out=v_w_out)
    names = list(weights)
    big = ("w_ada", "w_in", "pool_w", "w_proj_pool", "w_proj_ssd", "w_out")
    small_names = [n for n in names if n not in big]
    delta, new_m, new_v = {}, {}, {}
    for n in big:
        shape2 = (-1, weights[n].shape[-1])
        d_, m_, v_ = adamw(f"adamw_{n}", weights[n].reshape(shape2), grads[n].reshape(shape2), m_in[n].reshape(shape2),
                           v_in[n].reshape(shape2), tr=128)
        delta[n], new_m[n], new_v[n] = (t.reshape(weights[n].shape) for t in (d_, m_, v_))
    sizes = [weights[n].size for n in small_names]
    packed = [_pack([src[n] for n in small_names], 144) for src in (weights, grads, m_in, v_in)]
    outs = adamw("adamw_small", *packed, tr=144)
    for res, store in zip(outs, (delta, new_m, new_v)):
        for n, piece in zip(small_names, _unpack(res, sizes)):
            store[n] = piece.reshape(weights[n].shape)

    return (loss.reshape(()), grad_x, *[grads[n] for n in names], *[delta[n] for n in names],
            *[new_m[n] for n in names], *[new_v[n] for n in names])
```

```python
import jax
import jax.numpy as jnp
from jax import lax
from jax.experimental import pallas as pl
from jax.experimental.pallas import tpu as pltpu

F32 = jnp.float32
BF16 = jnp.bfloat16
MESH = pl.DeviceIdType.MESH

D = 1024
GRID_W = 64
NORM_EPS = 1e-6
POOL_WINDOWS = (2, 4, 8, 16)
POOL_GROUP = 256
D_INNER = 2048
HEAD_DIM = 64
N_HEADS = 32
D_STATE = 128
N_BC = 4
HPG = N_HEADS // N_BC
GW = HPG * HEAD_DIM
CONV_DIM = 3072
CHUNK = 128
OFF_XBC = 6144
IN_COLS = 9280
N_CHIPS = 4
N_DEV = 8

ADAM_LR = 0.001
ADAM_B1 = 0.9
ADAM_B2 = 0.999
ADAM_EPS = 1e-08
ADAM_WD = 0.01
ADAM_STEP = 10

V7X_VMEM_BYTES = 64 * 1024 * 1024
VMEM_LIMIT = V7X_VMEM_BYTES * 3 // 4
LANES = 128


def _cp(sem=None):
    return pltpu.CompilerParams(dimension_semantics=sem, vmem_limit_bytes=VMEM_LIMIT)


def _dot(a, b):
    return jnp.dot(a, b, preferred_element_type=F32)


def _dot_nt(a, b):
    return lax.dot_general(a, b, (((1,), (1,)), ((), ())), preferred_element_type=F32)


def _dot_tn(a, b):
    return lax.dot_general(a, b, (((0,), (0,)), ((), ())), preferred_element_type=F32)


def _split3(x):
    hi = x.astype(BF16)
    r1 = x - hi.astype(F32)
    mid = r1.astype(BF16)
    lo = (r1 - mid.astype(F32)).astype(BF16)
    return hi, mid, lo


def _sigmoid(x):
    return jax.nn.sigmoid(x)


def _silu(x):
    return x * _sigmoid(x)


def _dsilu(x):
    s = _sigmoid(x)
    return s * (1.0 + x * (1.0 - s))


def _softplus(x):
    return jnp.maximum(x, 0.0) + jnp.log(1.0 + jnp.exp(-jnp.abs(x)))


def mm_nt(name, a, b, out_dtype, tm=1024, tn=512):
    M, K = a.shape
    N = b.shape[0]
    tm, tn = min(tm, M), min(tn, N)
    assert M % tm == 0 and N % tn == 0, (M, N, tm, tn)

    def body(a_ref, b_ref, o_ref):
        o_ref[...] = _dot_nt(a_ref[...], b_ref[...]).astype(o_ref.dtype)

    return pl.pallas_call(
        body, name=name, out_shape=jax.ShapeDtypeStruct((M, N), out_dtype), grid=(M // tm, N // tn),
        in_specs=[pl.BlockSpec((tm, K), lambda i, j: (i, 0)), pl.BlockSpec((tn, K), lambda i, j: (j, 0))],
        out_specs=pl.BlockSpec((tm, tn), lambda i, j: (i, j)),
        compiler_params=_cp(("parallel", "arbitrary")))(a, b)


def mm_tn(name, a, b, init=None, tm=1024, tn=1024, tk=512):
    T, M = a.shape
    N = b.shape[1]
    tm, tn, tk = min(tm, M), min(tn, N), min(tk, T)
    assert M % tm == 0 and N % tn == 0 and T % tk == 0, (M, N, T)
    has_init = init is not None

    def body(*refs):
        if has_init:
            a_ref, b_ref, i_ref, o_ref = refs
        else:
            a_ref, b_ref, o_ref = refs
        k = pl.program_id(2)

        @pl.when(k == 0)
        def _():
            o_ref[...] = i_ref[...] if has_init else jnp.zeros(o_ref.shape, F32)

        o_ref[...] += _dot_tn(a_ref[...], b_ref[...])

    in_specs = [pl.BlockSpec((tk, tm), lambda i, j, k: (k, i)), pl.BlockSpec((tk, tn), lambda i, j, k: (k, j))]
    args = [a, b]
    if has_init:
        in_specs.append(pl.BlockSpec((tm, tn), lambda i, j, k: (i, j)))
        args.append(init)
    return pl.pallas_call(
        body, name=name, out_shape=jax.ShapeDtypeStruct((M, N), F32), grid=(M // tm, N // tn, T // tk),
        in_specs=in_specs, out_specs=pl.BlockSpec((tm, tn), lambda i, j, k: (i, j)),
        compiler_params=_cp(("parallel", "parallel", "arbitrary")))(*args)


def mm_nn_multi(name, pairs, out_dtype, tm=512, tk=512, exchange=None):
    M = pairs[0][0].shape[0]
    N = pairs[0][1].shape[1]
    tm = min(tm, M)
    assert M % tm == 0
    plan = []
    step = 0
    for a, b in pairs:
        K = a.shape[1]
        t = min(tk, K)
        assert K % t == 0 and b.shape == (K, N)
        plan.append((t, step, K // t))
        step += K // t
    nsteps = step
    npairs = len(pairs)

    n_i = M // tm
    has_x = exchange is not None

    def body(*refs):
        if has_x:
            p_ref, o_ref, land_ref, acc, send_sems, recv_sems = refs[2 * npairs:]
        else:
            o_ref, acc = refs[2 * npairs:]
        i, k = pl.program_id(0), pl.program_id(1)

        if has_x:
            x, y, c = _me()
            me_chip = 2 * x + y
            chips = [(1 - x, y), (x, 1 - y), (1 - x, 1 - y)]

            def copy(j, src_chip, dst_chip, to):
                return pltpu.make_async_remote_copy(
                    src_ref=p_ref.at[src_chip], dst_ref=land_ref.at[dst_chip], send_sem=send_sems.at[j], recv_sem=recv_sems.at[j],
                    device_id=(*to, c), device_id_type=MESH)

            @pl.when((i == 0) & (k == 0))
            def _():
                for j, chip in enumerate(chips):
                    copy(j, 2 * chip[0] + chip[1], me_chip, chip).start()

        @pl.when(k == 0)
        def _():
            acc[...] = jnp.zeros(acc.shape, F32)

        for p, (_, first, n) in enumerate(plan):
            @pl.when((k >= first) & (k < first + n))
            def _(p=p):
                acc[...] += _dot(refs[2 * p][...], refs[2 * p + 1][...])

        @pl.when(k == nsteps - 1)
        def _():
            o_ref[...] = acc[...].astype(o_ref.dtype)

        if has_x:
            @pl.when((i == n_i - 1) & (k == nsteps - 1))
            def _():
                for j, chip in enumerate(chips):
                    copy(j, me_chip, 2 * chip[0] + chip[1], chip).wait_recv()
                for j, chip in enumerate(chips):
                    copy(j, 2 * chip[0] + chip[1], me_chip, chip).wait_send()

    in_specs, args = [], []
    for (a, b), (t, first, n) in zip(pairs, plan):
        in_specs.append(pl.BlockSpec((tm, t), lambda i, k, first=first, n=n: (i, jnp.clip(k - first, 0, n - 1))))
        in_specs.append(pl.BlockSpec((t, N), lambda i, k, first=first, n=n: (jnp.clip(k - first, 0, n - 1), 0)))
        args += [a, b]
    out_shape = jax.ShapeDtypeStruct((M, N), out_dtype)
    out_specs = pl.BlockSpec((tm, N), lambda i, k: (i, 0))
    scratch = [pltpu.VMEM((tm, N), F32)]
    if has_x:
        in_specs.append(pl.BlockSpec(memory_space=pl.ANY))
        args.append(exchange)
        out_shape = [out_shape, jax.ShapeDtypeStruct(exchange.shape, exchange.dtype)]
        out_specs = [out_specs, pl.BlockSpec(memory_space=pl.ANY)]
        scratch += [pltpu.SemaphoreType.DMA((3,)), pltpu.SemaphoreType.DMA((3,))]
    res = pl.pallas_call(
        body, name=name, out_shape=out_shape, grid=(n_i, nsteps), in_specs=in_specs, out_specs=out_specs,
        scratch_shapes=scratch, compiler_params=_cp(("arbitrary", "arbitrary")))(*args)
    if not has_x:
        return res
    out, landed = res
    chip = 2 * lax.axis_index("x") + lax.axis_index("y")
    own = lax.dynamic_index_in_dim(exchange, chip, 0, keepdims=True)
    return out, lax.dynamic_update_slice_in_dim(landed, own, chip, 0)


def tok_call(name, body, tiled, perb, glob, out_tiled, out_perb, out_glob, tm=256):
    widths = [t[1] if isinstance(t, tuple) else t.shape[2] for t in tiled]
    tiled = [t[0] if isinstance(t, tuple) else t for t in tiled]
    Bn, L = tiled[0].shape[:2]
    tm = min(tm, L)
    assert L % tm == 0
    n_t, n_p, n_g = len(tiled), len(perb), len(glob)
    o_t, o_p, o_g = len(out_tiled), len(out_perb), len(out_glob)
    n_in = n_t + n_p + n_g

    def kern(*refs):
        ins, outs = refs[:n_in], refs[n_in:]
        b, j = pl.program_id(0), pl.program_id(1)
        vals = [r[0] for r in ins[:n_t + n_p]] + [r[...] for r in ins[n_t + n_p:]]
        res = body(*vals)
        if not isinstance(res, (tuple, list)):
            res = (res,)
        assert len(res) == o_t + o_p + o_g, (name, len(res))
        for r, v in zip(outs[:o_t], res[:o_t]):
            r[0] = v.astype(r.dtype)

        def accum(r, v, first, lead):
            @pl.when(first)
            def _():
                r[...] = jnp.zeros(r.shape, F32)
            if lead:
                r[0] += v
            else:
                r[...] += v

        for r, v in zip(outs[o_t:o_t + o_p], res[o_t:o_t + o_p]):
            accum(r, v, j == 0, True)
        for r, v in zip(outs[o_t + o_p:], res[o_t + o_p:]):
            accum(r, v, (j == 0) & (b == 0), False)

    in_specs = ([pl.BlockSpec((1, tm, w), lambda b, j: (b, j, 0)) for w in widths]
                + [pl.BlockSpec((1, 1, a.shape[2]), lambda b, j: (b, 0, 0)) for a in perb]
                + [pl.BlockSpec(a.shape, lambda b, j: (0, 0), pipeline_mode=pl.Buffered(1)) for a in glob])
    out_shape = ([jax.ShapeDtypeStruct((Bn, L, w), dt) for w, dt in out_tiled]
                 + [jax.ShapeDtypeStruct((Bn, 1, w), F32) for w in out_perb]
                 + [jax.ShapeDtypeStruct(s, F32) for s in out_glob])
    out_specs = ([pl.BlockSpec((1, tm, w), lambda b, j: (b, j, 0)) for w, _ in out_tiled]
                 + [pl.BlockSpec((1, 1, w), lambda b, j: (b, 0, 0)) for w in out_perb]
                 + [pl.BlockSpec(s, lambda b, j: (0, 0)) for s in out_glob])
    return pl.pallas_call(
        kern, name=name, out_shape=out_shape, grid=(Bn, L // tm), in_specs=in_specs, out_specs=out_specs,
        compiler_params=_cp(("arbitrary", "arbitrary")))(*tiled, *perb, *glob)


def slab_call(name, body, slabs, colparams, out_slabs, out_colred, wc=LANES):
    Bn, L = slabs[0][0].shape[:2]
    w_out = out_slabs[0][0]
    assert w_out % wc == 0 and all(off % wc == 0 for _, off in slabs + colparams)
    n_col = w_out // wc
    n_s, n_c = len(slabs), len(colparams)
    o_s = len(out_slabs)

    def kern(*refs):
        ins, outs = refs[:n_s + n_c], refs[n_s + n_c:]
        b = pl.program_id(1)
        vals = [r[0] for r in ins[:n_s]] + [r[...] for r in ins[n_s:]]
        res = body(*vals)
        if not isinstance(res, (tuple, list)):
            res = (res,)
        assert len(res) == o_s + len(out_colred), name
        for r, v in zip(outs[:o_s], res[:o_s]):
            r[0] = v.astype(r.dtype)

        def accum(r, v):
            @pl.when(b == 0)
            def _():
                r[...] = jnp.zeros(r.shape, F32)
            r[...] += v

        for r, v in zip(outs[o_s:], res[o_s:]):
            accum(r, v)

    in_specs = ([pl.BlockSpec((1, L, wc), lambda j, b, o=off // wc: (b, 0, o + j)) for _, off in slabs]
                + [pl.BlockSpec((a.shape[0], wc), lambda j, b, o=off // wc: (0, o + j)) for a, off in colparams])
    out_shape = ([jax.ShapeDtypeStruct((Bn, L, w), dt) for w, dt in out_slabs]
                 + [jax.ShapeDtypeStruct((r, w_out), F32) for r in out_colred])
    out_specs = ([pl.BlockSpec((1, L, wc), lambda j, b: (b, 0, j)) for _ in out_slabs]
                 + [pl.BlockSpec((r, wc), lambda j, b: (0, j)) for r in out_colred])
    return pl.pallas_call(
        kern, name=name, out_shape=out_shape, grid=(n_col, Bn), in_specs=in_specs, out_specs=out_specs,
        compiler_params=_cp(("arbitrary", "arbitrary")))(*[a for a, _ in slabs], *[a for a, _ in colparams])


def _rms_r(x):
    return lax.rsqrt(jnp.mean(x * x, axis=-1, keepdims=True) + NORM_EPS)


def _rms_bwd(dxh, x, r):
    return r * (dxh - x * (r * r) * jnp.mean(dxh * x, axis=-1, keepdims=True))


def _colsum(v):
    return jnp.sum(v, axis=0, keepdims=True)


def _stack_rows(rows):
    n, w = len(rows), rows[0].shape[1]
    sub = lax.broadcasted_iota(jnp.int32, (n, w), 0)
    acc = jnp.zeros((n, w), F32)
    for r, row in enumerate(rows):
        acc = acc + jnp.where(sub == r, jnp.broadcast_to(row, (n, w)), 0.0)
    return acc


def prenorm_fwd(name, x, scale, shift, w_pre):
    def body(x, scale, shift, w):
        n = x * _rms_r(x) * w
        return n * (1.0 + scale) + shift

    return tok_call(name, body, [x], [scale, shift], [w_pre], [(D, BF16)], [], [])[0]


def prenorm_bwd(name, x, dhx, scale, w_pre, g_res=None):
    has_res = g_res is not None

    def body(*v):
        if has_res:
            x, dhx, g, scale, w = v
        else:
            x, dhx, scale, w = v
        r = _rms_r(x)
        xr = x * r
        n = xr * w
        dn = dhx * (1.0 + scale)
        dx = _rms_bwd(dn * w, x, r)
        if has_res:
            dx = dx + g
        return dx, _colsum(dhx * n), _colsum(dhx), _colsum(dn * xr)

    tiled = [x, dhx] + ([g_res] if has_res else [])
    return tok_call(name, body, tiled, [scale], [w_pre], [(D, F32)], [D, D], [(1, D)])


def _shift_rows(x, o, tok, L):
    if o == 0:
        return x
    rolled = pltpu.roll(x, (-o) % L, 0)
    return jnp.where((tok + o >= 0) & (tok + o < L), rolled, 0.0)


def conv_fwd(name, xbc_raw, conv_w, conv_b):
    L = xbc_raw.shape[1]

    def body(x, w, b):
        tok = lax.broadcasted_iota(jnp.int32, x.shape, 0)
        pre = b
        for k in range(4):
            pre = pre + _shift_rows(x, k - 2, tok, L) * w[k:k + 1]
        return _silu(pre)

    return slab_call(name, body, [(xbc_raw, 0)], [(conv_w, 0), (conv_b, 0)], [(CONV_DIM, F32)], [])[0]


def conv_bwd(name, xbc_raw, dparts, conv_w, conv_b, col0, width, scaled=None):
    L = xbc_raw.shape[1]
    n_d = len(dparts) + (1 if scaled is not None else 0)

    def body(*v):
        x, ds, w, b = v[0], v[1:1 + n_d], v[1 + n_d], v[2 + n_d]
        tok = lax.broadcasted_iota(jnp.int32, x.shape, 0)
        taps = [_shift_rows(x, k - 2, tok, L) for k in range(4)]
        pre = b
        for k in range(4):
            pre = pre + taps[k] * w[k:k + 1]
        dy = ds[0] * v[3 + n_d] if scaled is not None else ds[0]
        for extra in ds[1:]:
            dy = dy + extra
        dpre = dy * _dsilu(pre)
        dx = jnp.zeros_like(x)
        for k in range(4):
            dx = dx + _shift_rows(dpre, 2 - k, tok, L) * w[k:k + 1]
        dw = _stack_rows([_colsum(dpre * taps[k]) for k in range(4)])
        return dx, dw, _colsum(dpre)

    slabs = [(xbc_raw, col0)] + ([(scaled[0], 0)] if scaled is not None else []) + [(d, 0) for d in dparts]
    colparams = [(conv_w, col0), (conv_b, col0)] + ([(scaled[1], 0)] if scaled is not None else [])
    return slab_call(name, body, slabs, colparams, [(width, BF16)], [4, 1])


CONV_ROWS = 128
CONV_HALO = 8


def _halo_chunks(L, load, work):
    ch, hl = CONV_ROWS, CONV_HALO
    n = L // ch
    assert L % ch == 0
    if n == 1:
        z = jnp.zeros_like(load(0, hl))
        work(0, jnp.concatenate([z, load(0, ch), z], axis=0))
        return
    z = jnp.zeros_like(load(0, hl))
    work(0, jnp.concatenate([z, load(0, ch + hl)], axis=0))

    def step(i, carry):
        start = pl.multiple_of(i * ch, ch)
        work(start, load(pl.multiple_of(start - hl, hl), ch + 2 * hl))
        return carry

    lax.fori_loop(1, n - 1, step, 0)
    work(L - ch, jnp.concatenate([load(L - ch - hl, ch + hl), z], axis=0))


def _rows_at(xh, o):
    return xh if o == 0 else pltpu.roll(xh, (-o) % xh.shape[0], 0)


def conv_fwd_stream(name, xbc_raw, conv_w, conv_b, wc=LANES):
    Bn, L, W = xbc_raw.shape
    mid = slice(CONV_HALO, CONV_HALO + CONV_ROWS)

    def kern(x_ref, w_ref, b_ref, o_ref):
        w, b = w_ref[...], b_ref[...]

        def work(start, xh):
            pre = b
            for k in range(4):
                pre = pre + _rows_at(xh, k - 2) * w[k:k + 1]
            o_ref[0, pl.ds(start, CONV_ROWS), :] = _silu(pre)[mid]

        _halo_chunks(L, lambda s, n: x_ref[0, pl.ds(s, n), :], work)

    return pl.pallas_call(
        kern, name=name, out_shape=jax.ShapeDtypeStruct((Bn, L, W), F32), grid=(W // wc, Bn),
        in_specs=[pl.BlockSpec((1, L, wc), lambda j, b: (b, 0, j)), pl.BlockSpec((4, wc), lambda j, b: (0, j)),
                  pl.BlockSpec((1, wc), lambda j, b: (0, j))],
        out_specs=pl.BlockSpec((1, L, wc), lambda j, b: (b, 0, j)),
        compiler_params=_cp(("arbitrary", "arbitrary")))(xbc_raw, conv_w, conv_b)


def conv_bwd_stream(name, xbc_raw, dparts, conv_w, conv_b, col0, width, scaled=None, wc=LANES):
    Bn, L, _ = xbc_raw.shape
    n_d = len(dparts)
    has_s = scaled is not None
    mid = slice(CONV_HALO, CONV_HALO + CONV_ROWS)
    c0 = col0 // wc

    def kern(*refs):
        x_ref, d_refs = refs[0], refs[1:1 + n_d]
        pos = 1 + n_d
        if has_s:
            s_ref, pos = refs[pos], pos + 1
        w_ref, b_ref = refs[pos], refs[pos + 1]
        pos += 2
        if has_s:
            scale = refs[pos][...]
            pos += 1
        dx_ref, dw_ref, db_ref = refs[pos:pos + 3]
        acc = refs[pos + 3]
        w, b = w_ref[...], b_ref[...]
        acc[...] = jnp.zeros(acc.shape, F32)

        def load(s, n):
            dy = d_refs[0][0, pl.ds(s, n), :]
            for r in d_refs[1:]:
                dy = dy + r[0, pl.ds(s, n), :]
            if has_s:
                dy = dy + s_ref[0, pl.ds(s, n), :] * scale
            return jnp.concatenate([x_ref[0, pl.ds(s, n), :], dy], axis=1)

        def work(start, both):
            xh, dyh = both[:, 0:wc], both[:, wc:]
            taps = [_rows_at(xh, k - 2) for k in range(4)]
            pre = b
            for k in range(4):
                pre = pre + taps[k] * w[k:k + 1]
            dpre = dyh * _dsilu(pre)
            dx = dpre * w[2:3]
            for k in (0, 1, 3):
                dx = dx + _rows_at(dpre, 2 - k) * w[k:k + 1]
            dx_ref[0, pl.ds(start, CONV_ROWS), :] = dx[mid].astype(dx_ref.dtype)
            dm = dpre[mid]
            acc[...] += _stack_rows([_colsum(dm * taps[k][mid]) for k in range(4)] + [_colsum(dm)] + [jnp.zeros((1, wc), F32)] * 3)

        _halo_chunks(L, load, work)
        first = pl.program_id(1) == 0

        @pl.when(first)
        def _():
            dw_ref[...] = acc[0:4]
            db_ref[...] = acc[4:5]

        @pl.when(jnp.logical_not(first))
        def _():
            dw_ref[...] += acc[0:4]
            db_ref[...] += acc[4:5]

    slab = lambda off: pl.BlockSpec((1, L, wc), lambda j, b, off=off: (b, 0, off + j))
    in_specs = [slab(c0)] + [slab(0)] * n_d + ([slab(0)] if has_s else [])
    in_specs += [pl.BlockSpec((4, wc), lambda j, b: (0, c0 + j)), pl.BlockSpec((1, wc), lambda j, b: (0, c0 + j))]
    args = [xbc_raw, *dparts] + ([scaled[0]] if has_s else []) + [conv_w, conv_b]
    if has_s:
        in_specs.append(pl.BlockSpec((1, wc), lambda j, b: (0, j)))
        args.append(scaled[1])
    return pl.pallas_call(
        kern, name=name,
        out_shape=[jax.ShapeDtypeStruct((Bn, L, width), BF16), jax.ShapeDtypeStruct((4, width), F32), jax.ShapeDtypeStruct((1, width), F32)],
        grid=(width // wc, Bn), in_specs=in_specs,
        out_specs=[pl.BlockSpec((1, L, wc), lambda j, b: (b, 0, j)), pl.BlockSpec((4, wc), lambda j, b: (0, j)),
                   pl.BlockSpec((1, wc), lambda j, b: (0, j))],
        scratch_shapes=[pltpu.VMEM((8, wc), F32)],
        compiler_params=_cp(("arbitrary", "arbitrary")))(*args)


def _box_mean(x, k, step, pos, n, L, transpose):
    lo, hi = k // 2, k - 1 - k // 2
    cnt = (jnp.minimum(pos + hi + 1, n) - jnp.maximum(pos - lo, 0)).astype(F32)
    if transpose:
        x = x / cnt
        lo, hi = hi, lo
    acc = x
    for o in range(-lo, hi + 1):
        if o == 0:
            continue
        rolled = pltpu.roll(x, (-o * step) % L, 0)
        acc = acc + jnp.where((pos + o >= 0) & (pos + o < n), rolled, 0.0)
    return acc if transpose else acc / cnt


def pool_diff(name, v, col0, gi, transpose):
    L = v.shape[1]
    rows = L // GRID_W
    k = POOL_WINDOWS[gi]

    def body(x):
        tok = lax.broadcasted_iota(jnp.int32, x.shape, 0)
        col = tok & (GRID_W - 1)
        row = tok >> 6
        if not transpose:
            m = _box_mean(x, k, GRID_W, row, rows, L, False)
            m = _box_mean(m, k, 1, col, GRID_W, L, False)
        else:
            m = _box_mean(x, k, 1, col, GRID_W, L, True)
            m = _box_mean(m, k, GRID_W, row, rows, L, True)
        return m - x

    return slab_call(name, body, [(v, col0)], [], [(POOL_GROUP, BF16)], [])[0]


def pool_mix_fwd(name, dgs, z_pool, pool_w, pool_scale):
    def body(d0, d1, d2, d3, z, w, scale):
        q = jnp.concatenate([_dot(d, w[g * POOL_GROUP:(g + 1) * POOL_GROUP]) for g, d in enumerate((d0, d1, d2, d3))], axis=1)
        return q * scale * _silu(z)

    return tok_call(name, body, list(dgs) + [z_pool], [], [pool_w, pool_scale], [(D, BF16)], [], [])[0]


def pool_mix_bwd(name, dgs, z_pool, dyp, pool_w, pool_scale):
    def body(d0, d1, d2, d3, z, dyp, w, scale):
        ds = (d0, d1, d2, d3)
        q = jnp.concatenate([_dot(d, w[g * POOL_GROUP:(g + 1) * POOL_GROUP]) for g, d in enumerate(ds)], axis=1)
        dypm = dyp * _silu(z)
        dz = dyp * (q * scale) * _dsilu(z)
        dq = (dypm * scale).astype(BF16)
        dds, gws = [], []
        for g, d in enumerate(ds):
            dqg = dq[:, g * POOL_GROUP:(g + 1) * POOL_GROUP]
            dds.append(_dot_nt(dqg, w[g * POOL_GROUP:(g + 1) * POOL_GROUP]))
            gws.append(_dot_tn(d, dqg))
        return (*dds, dz, jnp.concatenate(gws, axis=0), _colsum(dypm * q))

    return tok_call(name, body, list(dgs) + [z_pool, dyp], [], [pool_w, pool_scale],
                    [(POOL_GROUP, F32)] * 4 + [(D, BF16)], [], [(D, POOL_GROUP), (1, D)])


def _cumsum_lanes(a, reverse):
    n = a.shape[1]
    k = lax.broadcasted_iota(jnp.int32, (n, n), 0)
    i = lax.broadcasted_iota(jnp.int32, (n, n), 1)
    tri = jnp.where((k >= i) if reverse else (k <= i), 1.0, 0.0).astype(BF16)
    return _dot_exact01(a, tri)


def _rows_to_cols(rows):
    r = rows.shape[0]
    if r < LANES:
        rows = jnp.concatenate([rows, jnp.zeros((LANES - r, rows.shape[1]), F32)], axis=0)
    return rows.T


def _cols_to_rows(cols):
    q = cols[0].shape[0]
    lane = lax.broadcasted_iota(jnp.int32, (q, LANES), 1)
    acc = jnp.zeros((q, LANES), F32)
    for r, c in enumerate(cols):
        acc = acc + jnp.where(lane == r, c, 0.0)
    return acc.T[0:len(cols)]


def _ssd_scalars(dtraw, bias, alog, reverse):
    dt = _softplus(dtraw + bias)
    A = -jnp.exp(alog)
    cs = _cumsum_lanes(dt * A, reverse)
    total = cs[:, 0:1] if reverse else cs[:, CHUNK - 1:CHUNK]
    return dt, A, cs, total


def _decay_matrix(cs_col, cs_row, reverse):
    i = lax.broadcasted_iota(jnp.int32, (CHUNK, CHUNK), 0)
    j = lax.broadcasted_iota(jnp.int32, (CHUNK, CHUNK), 1)
    keep = (i <= j) if reverse else (i >= j)
    return jnp.exp(jnp.where(keep, cs_col - cs_row, -jnp.inf))


def ssd_fwd_v1(name, dtT, bias, alog, xbc, h0, direction, with_y):
    Bn, L = xbc.shape[:2]
    nc = L // CHUNK
    reverse = direction == 1
    rowblk = direction * N_BC

    def chunk_of(s):
        return (nc - 1 - s) if reverse else s

    def kern(dt_ref, bias_ref, alog_ref, x_ref, b_ref, c_ref, h0_ref, *rest):
        if with_y:
            y_ref, hs_ref, hf_ref, h_scr, xt_scr = rest
        else:
            hs_ref, hf_ref, h_scr, xt_scr = rest
        s = pl.program_id(2)

        @pl.when(s == 0)
        def _():
            h_scr[...] = h0_ref[0, 0]

        dt, _, cs, total = _ssd_scalars(dt_ref[0], bias_ref[0], alog_ref[0], reverse)
        e_row = jnp.exp(cs)
        t_row = jnp.exp(total - cs)
        dc = jnp.exp(total)
        cols = _rows_to_cols(jnp.concatenate([dt, e_row, t_row, cs], axis=0))
        x = x_ref[0]
        bm = b_ref[0].astype(BF16)
        cm = c_ref[0].astype(BF16)
        h = h_scr[...]
        hs_ref[0, 0, 0] = h
        if with_y:
            cb = _dot_nt(cm, bm)
            yoff = _dot(cm, h.astype(BF16))
        for r in range(HPG):
            sl = slice(r * HEAD_DIM, (r + 1) * HEAD_DIM)
            xdt = x[:, sl] * cols[:, r:r + 1]
            if with_y:
                lr = _decay_matrix(cols[:, 3 * HPG + r:3 * HPG + r + 1], cs[r:r + 1], reverse)
                ydiag = _dot((cb * lr).astype(BF16), xdt.astype(BF16))
                y_ref[0, :, sl] = ydiag + yoff[:, sl] * cols[:, HPG + r:HPG + r + 1]
            xt_scr[:, sl] = (xdt * cols[:, 2 * HPG + r:2 * HPG + r + 1]).astype(BF16)
        st = _dot_tn(bm, xt_scr[...])
        for r in range(HPG):
            sl = slice(r * HEAD_DIM, (r + 1) * HEAD_DIM)
            h_scr[:, sl] = h[:, sl] * dc[r:r + 1] + st[:, sl]

        @pl.when(s == nc - 1)
        def _():
            hf_ref[0, 0] = h_scr[...]

    in_specs = [
        pl.BlockSpec((1, HPG, CHUNK), lambda b, g, s: (b, rowblk + g, chunk_of(s))),
        pl.BlockSpec((1, HPG, 1), lambda b, g, s: (rowblk + g, 0, 0)),
        pl.BlockSpec((1, HPG, 1), lambda b, g, s: (rowblk + g, 0, 0)),
        pl.BlockSpec((1, CHUNK, GW), lambda b, g, s: (b, chunk_of(s), g)),
        pl.BlockSpec((1, CHUNK, D_STATE), lambda b, g, s: (b, chunk_of(s), D_INNER // D_STATE + g)),
        pl.BlockSpec((1, CHUNK, D_STATE), lambda b, g, s: (b, chunk_of(s), D_INNER // D_STATE + N_BC + g)),
        pl.BlockSpec((1, 1, D_STATE, GW), lambda b, g, s: (b, g, 0, 0)),
    ]
    out_shape, out_specs = [], []
    if with_y:
        out_shape.append(jax.ShapeDtypeStruct((Bn, L, D_INNER), F32))
        out_specs.append(pl.BlockSpec((1, CHUNK, GW), lambda b, g, s: (b, chunk_of(s), g)))
    out_shape += [jax.ShapeDtypeStruct((Bn, N_BC, nc, D_STATE, GW), F32), jax.ShapeDtypeStruct((Bn, N_BC, D_STATE, GW), F32)]
    out_specs += [pl.BlockSpec((1, 1, 1, D_STATE, GW), lambda b, g, s: (b, g, chunk_of(s), 0, 0)),
                  pl.BlockSpec((1, 1, D_STATE, GW), lambda b, g, s: (b, g, 0, 0))]
    return pl.pallas_call(
        kern, name=name, out_shape=out_shape, grid=(Bn, N_BC, nc), in_specs=in_specs, out_specs=out_specs,
        scratch_shapes=[pltpu.VMEM((D_STATE, GW), F32), pltpu.VMEM((CHUNK, GW), BF16)],
        compiler_params=_cp(("arbitrary", "arbitrary", "arbitrary")))(dtT, bias, alog, xbc, xbc, xbc, h0)


def ssd_bwd_v1(name, dtT, bias, alog, xbc, h_start, dy, dh_final, direction):
    Bn, L = xbc.shape[:2]
    nc = L // CHUNK
    reverse = direction == 1
    rowblk = direction * N_BC
    has_y = dy is not None
    last = 0 if reverse else CHUNK - 1

    def chunk_of(s):
        return s if reverse else (nc - 1 - s)

    def kern(*refs):
        if has_y:
            (dt_ref, bias_ref, alog_ref, x_ref, b_ref, c_ref, hs_ref, dhf_ref, dy_ref,
             dx_ref, db_ref, dc_ref, ddt_ref, dbias_ref, dalog_ref, dh0_ref, dh_scr, e_scr, t_scr) = refs
        else:
            (dt_ref, bias_ref, alog_ref, x_ref, b_ref, hs_ref, dhf_ref,
             dx_ref, db_ref, ddt_ref, dbias_ref, dalog_ref, dh0_ref, dh_scr, t_scr) = refs
        s = pl.program_id(2)

        @pl.when(s == 0)
        def _():
            dh_scr[...] = dhf_ref[0, 0]
            dbias_ref[...] = jnp.zeros(dbias_ref.shape, F32)
            dalog_ref[...] = jnp.zeros(dalog_ref.shape, F32)

        dtraw = dt_ref[0]
        dt, A, cs, total = _ssd_scalars(dtraw, bias_ref[0], alog_ref[0], reverse)
        e_row = jnp.exp(cs)
        t_row = jnp.exp(total - cs)
        dcy = jnp.exp(total)
        cols = _rows_to_cols(jnp.concatenate([dt, e_row, t_row, cs], axis=0))
        x = x_ref[0]
        bm = b_ref[0].astype(BF16)
        h = hs_ref[0, 0, 0]
        dh = dh_scr[...]
        dh_bf = dh.astype(BF16)
        bdh = _dot(bm, dh_bf)
        if has_y:
            cm = c_ref[0].astype(BF16)
            dyv = dy_ref[0]
            cb = _dot_nt(cm, bm)
            yoff = _dot(cm, h.astype(BF16))
            dcb = jnp.zeros((CHUNK, CHUNK), F32)
        col_terms, row_terms, ddt_cols, dtot = [], [], [], []
        for r in range(HPG):
            sl = slice(r * HEAD_DIM, (r + 1) * HEAD_DIM)
            dt_c = cols[:, r:r + 1]
            e_c = cols[:, HPG + r:HPG + r + 1]
            t_c = cols[:, 2 * HPG + r:2 * HPG + r + 1]
            xr = x[:, sl]
            xdt = xr * dt_c
            dxdt = t_c * bdh[:, sl]
            d_t = jnp.sum(bdh[:, sl] * xdt, axis=1, keepdims=True)
            col = -(t_c * d_t)
            tot = jnp.sum(t_c * d_t, axis=0, keepdims=True) + dcy[r:r + 1] * jnp.sum(h[:, sl] * dh[:, sl], keepdims=True)
            if has_y:
                dyr = dyv[:, sl]
                lr = _decay_matrix(cols[:, 3 * HPG + r:3 * HPG + r + 1], cs[r:r + 1], reverse)
                w = cb * lr
                gm = _dot_nt(dyr.astype(BF16), xdt.astype(BF16))
                m = gm * w
                dcb = dcb + gm * lr
                dxdt = dxdt + _dot_tn(w.astype(BF16), dyr.astype(BF16))
                col = col + jnp.sum(m, axis=1, keepdims=True) + jnp.sum(yoff[:, sl] * dyr, axis=1, keepdims=True) * e_c
                row_terms.append(-jnp.sum(m, axis=0, keepdims=True))
                e_scr[:, sl] = (e_c * dyr).astype(BF16)
            t_scr[:, sl] = (t_c * xdt).astype(BF16)
            dx_ref[0, :, sl] = dxdt * dt_c
            ddt_cols.append(jnp.sum(dxdt * xr, axis=1, keepdims=True))
            col_terms.append(col)
            dtot.append(tot)
        db = _dot_nt(t_scr[...], dh_bf)
        if has_y:
            dcb_bf = dcb.astype(BF16)
            db = db + _dot_tn(dcb_bf, cm)
            dc_ref[0] = _dot(dcb_bf, bm) + _dot_nt(e_scr[...], h.astype(BF16))
            cte = _dot_tn(cm, e_scr[...])
        db_ref[0] = db
        for r in range(HPG):
            sl = slice(r * HEAD_DIM, (r + 1) * HEAD_DIM)
            new = dh[:, sl] * dcy[r:r + 1]
            if has_y:
                new = new + cte[:, sl]
            dh_scr[:, sl] = new
        dcs = _cols_to_rows(col_terms)
        if has_y:
            dcs = dcs + _stack_rows(row_terms)
        lane = lax.broadcasted_iota(jnp.int32, (HPG, CHUNK), 1)
        dcs = dcs + jnp.where(lane == last, _stack_rows([jnp.broadcast_to(t, (1, CHUNK)) for t in dtot]), 0.0)
        da = _cumsum_lanes(dcs, not reverse)
        ddt = da * A + _cols_to_rows(ddt_cols)
        ddtraw = ddt * _sigmoid(dtraw + bias_ref[0])
        ddt_ref[0] = ddtraw
        dbias_ref[0, 0] += jnp.sum(ddtraw, axis=1, keepdims=True)
        dalog_ref[0, 0] += jnp.sum(da * dt, axis=1, keepdims=True) * A

        @pl.when(s == nc - 1)
        def _():
            dh0_ref[0, 0] = dh_scr[...]

    cidx = lambda b, g, s: (b, chunk_of(s), g)
    in_specs = [
        pl.BlockSpec((1, HPG, CHUNK), lambda b, g, s: (b, rowblk + g, chunk_of(s))),
        pl.BlockSpec((1, HPG, 1), lambda b, g, s: (rowblk + g, 0, 0)),
        pl.BlockSpec((1, HPG, 1), lambda b, g, s: (rowblk + g, 0, 0)),
        pl.BlockSpec((1, CHUNK, GW), cidx),
        pl.BlockSpec((1, CHUNK, D_STATE), lambda b, g, s: (b, chunk_of(s), D_INNER // D_STATE + g)),
    ]
    args = [dtT, bias, alog, xbc, xbc]
    if has_y:
        in_specs.append(pl.BlockSpec((1, CHUNK, D_STATE), lambda b, g, s: (b, chunk_of(s), D_INNER // D_STATE + N_BC + g)))
        args.append(xbc)
    in_specs += [pl.BlockSpec((1, 1, 1, D_STATE, GW), lambda b, g, s: (b, g, chunk_of(s), 0, 0)),
                 pl.BlockSpec((1, 1, D_STATE, GW), lambda b, g, s: (b, g, 0, 0))]
    args += [h_start, dh_final]
    if has_y:
        in_specs.append(pl.BlockSpec((1, CHUNK, GW), cidx))
        args.append(dy)
    out_shape = [jax.ShapeDtypeStruct((Bn, L, D_INNER), F32), jax.ShapeDtypeStruct((Bn, L, N_BC * D_STATE), F32)]
    out_specs = [pl.BlockSpec((1, CHUNK, GW), cidx), pl.BlockSpec((1, CHUNK, D_STATE), cidx)]
    if has_y:
        out_shape.append(jax.ShapeDtypeStruct((Bn, L, N_BC * D_STATE), F32))
        out_specs.append(pl.BlockSpec((1, CHUNK, D_STATE), cidx))
    out_shape += [jax.ShapeDtypeStruct((Bn, N_HEADS, L), F32), jax.ShapeDtypeStruct((Bn, N_BC, HPG, 1), F32),
                  jax.ShapeDtypeStruct((Bn, N_BC, HPG, 1), F32), jax.ShapeDtypeStruct((Bn, N_BC, D_STATE, GW), F32)]
    out_specs += [pl.BlockSpec((1, HPG, CHUNK), lambda b, g, s: (b, g, chunk_of(s))),
                  pl.BlockSpec((1, 1, HPG, 1), lambda b, g, s: (b, g, 0, 0)),
                  pl.BlockSpec((1, 1, HPG, 1), lambda b, g, s: (b, g, 0, 0)),
                  pl.BlockSpec((1, 1, D_STATE, GW), lambda b, g, s: (b, g, 0, 0))]
    scratch = [pltpu.VMEM((D_STATE, GW), F32)] + ([pltpu.VMEM((CHUNK, GW), BF16)] if has_y else []) + [pltpu.VMEM((CHUNK, GW), BF16)]
    res = pl.pallas_call(
        kern, name=name, out_shape=out_shape, grid=(Bn, N_BC, nc), in_specs=in_specs, out_specs=out_specs,
        scratch_shapes=scratch, compiler_params=_cp(("arbitrary", "arbitrary", "arbitrary")))(*args)
    if has_y:
        return res
    dxs, db, ddt, dbias, dalog, dh0 = res
    return dxs, db, None, ddt, dbias, dalog, dh0


def _tri_mask(transposed, reverse):
    sub = lax.broadcasted_iota(jnp.int32, (CHUNK, CHUNK), 0)
    lane = lax.broadcasted_iota(jnp.int32, (CHUNK, CHUNK), 1)
    i, j = (lane, sub) if transposed else (sub, lane)
    return (i <= j) if reverse else (i >= j)


def ssd_fwd(name, dtT, bias, alog, xbc, h0, direction, with_y):
    Bn, L = xbc.shape[:2]
    nc = L // CHUNK
    reverse = direction == 1
    rowblk = direction * N_BC

    def chunk_of(s):
        return (nc - 1 - s) if reverse else s

    def kern(dt_ref, bias_ref, alog_ref, x_ref, b_ref, c_ref, h0_ref, *rest):
        if with_y:
            y_ref, hs_ref, hf_ref, h_scr = rest
        else:
            hs_ref, hf_ref, h_scr = rest
        s = pl.program_id(2)

        @pl.when(s == 0)
        def _():
            h_scr[...] = h0_ref[0, 0]

        dt, _, cs, total = _ssd_scalars(dt_ref[0], bias_ref[0], alog_ref[0], reverse)
        u = cs - jnp.log(dt)
        dtt = jnp.exp(total - u)
        dc = jnp.exp(total)
        x_bf = x_ref[0].astype(BF16)
        bm = b_ref[0]
        h = h_scr[...]
        h_bf = h.astype(BF16)
        hs_ref[0, 0, 0] = h
        bt = bm.T
        if with_y:
            cm = c_ref[0]
            cb = _dot_nt(cm.astype(BF16), bm.astype(BF16))
            cs_cols = _rows_to_cols(cs)
            keep = _tri_mask(False, reverse)
        first = lax.broadcasted_iota(jnp.int32, (1, LANES), 1) < HEAD_DIM
        heads = range(HPG)
        psl = [slice((r // 2) * LANES, (r // 2 + 1) * LANES) for r in heads]
        lhs = []
        if with_y:
            for r in heads:
                cs_col = jnp.broadcast_to(cs_cols[:, r:r + 1], (CHUNK, LANES))
                wf = cb * jnp.exp(jnp.where(keep, cs_col - u[r:r + 1], -jnp.inf))
                lhs.append(jnp.concatenate([wf.astype(BF16), (cm * jnp.exp(cs_col)).astype(BF16)], axis=1))
        bts = [(bt * dtt[r:r + 1]).astype(BF16) for r in heads]
        sts = [_dot(bts[r], x_bf[:, psl[r]]) for r in heads]
        if with_y:
            ys = [_dot(lhs[r], jnp.concatenate([x_bf[:, psl[r]], h_bf[:, psl[r]]], axis=0)) for r in heads]
        for p in range(HPG // 2):
            if with_y:
                y_ref[0, :, psl[2 * p]] = jnp.where(first, ys[2 * p], ys[2 * p + 1])
            dc_p = jnp.where(first, dc[2 * p:2 * p + 1], dc[2 * p + 1:2 * p + 2])
            h_scr[:, psl[2 * p]] = h[:, psl[2 * p]] * dc_p + jnp.where(first, sts[2 * p], sts[2 * p + 1])

        @pl.when(s == nc - 1)
        def _():
            hf_ref[0, 0] = h_scr[...]

    in_specs = [
        pl.BlockSpec((1, HPG, CHUNK), lambda b, g, s: (b, rowblk + g, chunk_of(s))),
        pl.BlockSpec((1, HPG, 1), lambda b, g, s: (rowblk + g, 0, 0)),
        pl.BlockSpec((1, HPG, 1), lambda b, g, s: (rowblk + g, 0, 0)),
        pl.BlockSpec((1, CHUNK, GW), lambda b, g, s: (b, chunk_of(s), g)),
        pl.BlockSpec((1, CHUNK, D_STATE), lambda b, g, s: (b, chunk_of(s), D_INNER // D_STATE + g)),
        pl.BlockSpec((1, CHUNK, D_STATE), lambda b, g, s: (b, chunk_of(s), D_INNER // D_STATE + N_BC + g)),
        pl.BlockSpec((1, 1, D_STATE, GW), lambda b, g, s: (b, g, 0, 0)),
    ]
    out_shape, out_specs = [], []
    if with_y:
        out_shape.append(jax.ShapeDtypeStruct((Bn, L, D_INNER), F32))
        out_specs.append(pl.BlockSpec((1, CHUNK, GW), lambda b, g, s: (b, chunk_of(s), g)))
    out_shape += [jax.ShapeDtypeStruct((Bn, N_BC, nc, D_STATE, GW), F32), jax.ShapeDtypeStruct((Bn, N_BC, D_STATE, GW), F32)]
    out_specs += [pl.BlockSpec((1, 1, 1, D_STATE, GW), lambda b, g, s: (b, g, chunk_of(s), 0, 0)),
                  pl.BlockSpec((1, 1, D_STATE, GW), lambda b, g, s: (b, g, 0, 0))]
    return pl.pallas_call(
        kern, name=name, out_shape=out_shape, grid=(Bn, N_BC, nc), in_specs=in_specs, out_specs=out_specs,
        scratch_shapes=[pltpu.VMEM((D_STATE, GW), F32)],
        compiler_params=_cp(("arbitrary", "arbitrary", "arbitrary")))(dtT, bias, alog, xbc, xbc, xbc, h0)


def ssd_bwd(name, dtT, bias, alog, xbc, h_start, dy, dh_final, direction):
    Bn, L = xbc.shape[:2]
    nc = L // CHUNK
    reverse = direction == 1
    rowblk = direction * N_BC
    has_y = dy is not None
    last = 0 if reverse else CHUNK - 1

    def chunk_of(s):
        return s if reverse else (nc - 1 - s)

    def kern(*refs):
        if has_y:
            (dt_ref, bias_ref, alog_ref, x_ref, b_ref, hs_ref, dhf_ref, c_ref, dy_ref,
             dx_ref, db_ref, ddt_ref, dbias_ref, dalog_ref, dh0_ref, dc_ref, dh_scr) = refs
        else:
            (dt_ref, bias_ref, alog_ref, x_ref, b_ref, hs_ref, dhf_ref,
             dx_ref, db_ref, ddt_ref, dbias_ref, dalog_ref, dh0_ref, dh_scr) = refs
        s = pl.program_id(2)

        @pl.when(s == 0)
        def _():
            dh_scr[...] = dhf_ref[0, 0]
            dbias_ref[...] = jnp.zeros(dbias_ref.shape, F32)
            dalog_ref[...] = jnp.zeros(dalog_ref.shape, F32)

        dtraw = dt_ref[0]
        dt, A, cs, total = _ssd_scalars(dtraw, bias_ref[0], alog_ref[0], reverse)
        u = cs - jnp.log(dt)
        dtt = jnp.exp(total - u)
        dcy = jnp.exp(total)
        u_cols = _rows_to_cols(u)
        x_bf = x_ref[0].astype(BF16)
        bm = b_ref[0]
        bt = bm.T
        h = hs_ref[0, 0, 0]
        dh = dh_scr[...]
        dh_bf = dh.astype(BF16)
        dbt = jnp.zeros((D_STATE, CHUNK), F32)
        if has_y:
            cm = c_ref[0]
            ct = cm.T
            e_row = jnp.exp(cs)
            dy_bf = dy_ref[0].astype(BF16)
            h_bf = h.astype(BF16)
            cbt = _dot_nt(bm.astype(BF16), cm.astype(BF16))
            keep = _tri_mask(True, reverse)
            dcbt = jnp.zeros((CHUNK, CHUNK), F32)
            dct = jnp.zeros((D_STATE, CHUNK), F32)
        tots, out_rows, in_rows, in_cols = [], [], [], []
        first = lax.broadcasted_iota(jnp.int32, (1, LANES), 1) < HEAD_DIM
        heads = range(HPG)
        psl = [slice((r // 2) * LANES, (r // 2 + 1) * LANES) for r in heads]
        mine = [first if r % 2 == 0 else jnp.logical_not(first) for r in heads]
        zeros_bf = jnp.zeros((CHUNK, LANES), BF16)

        def prep(r):
            u_col = jnp.broadcast_to(u_cols[:, r:r + 1], (CHUNK, LANES))
            bs = (bm * jnp.exp(total[r:r + 1] - u_col)).astype(BF16)
            if not has_y:
                return bs, None
            et = jnp.exp(jnp.where(keep, cs[r:r + 1] - u_col, -jnp.inf))
            return jnp.concatenate([(cbt * et).astype(BF16), bs], axis=1), et

        def matmuls(r, lhs):
            p2raw = _dot_nt(dh_bf[:, psl[r]], jnp.where(mine[r], x_bf[:, psl[r]], zeros_bf))
            if not has_y:
                return p2raw, None, None, _dot(lhs, dh_bf[:, psl[r]])
            a1 = _dot_nt(jnp.concatenate([x_bf[:, psl[r]], h_bf[:, psl[r]]], axis=0),
                         jnp.where(mine[r], dy_bf[:, psl[r]], zeros_bf))
            new = _dot((ct * e_row[r:r + 1]).astype(BF16), dy_bf[:, psl[r]])
            dx = _dot(lhs, jnp.concatenate([dy_bf[:, psl[r]], dh_bf[:, psl[r]]], axis=0))
            return p2raw, a1, new, dx

        def post(r, p2raw, a1, et, dbt, dcbt, dct):
            if has_y:
                pt = a1[0:CHUNK] * et
                dcbt = dcbt + pt
                mt = pt * cbt
                ph = a1[CHUNK:] * e_row[r:r + 1]
                dct = dct + ph
                out_rows.append(_colsum(mt + ct * ph))
                in_cols.append(jnp.sum(mt, axis=1, keepdims=True))
            p2 = p2raw * dtt[r:r + 1]
            dbt = dbt + p2
            t_term = _colsum(bt * p2)
            in_rows.append(t_term)
            hdh = h[:, psl[r]] * dh[:, psl[r]]
            tot = jnp.sum(t_term, axis=1, keepdims=True) + dcy[r:r + 1] * jnp.sum(jnp.where(mine[r], hdh, 0.0), keepdims=True)
            tots.append(jnp.broadcast_to(tot, (1, CHUNK)))
            return dbt, dcbt, dct

        if not has_y:
            dcbt = dct = None
        dxs, news, pending = [], [], []
        batch = HPG
        for r0 in range(0, HPG, batch):
            preps = [prep(r) for r in range(r0, r0 + batch)]
            mms = [matmuls(r, preps[r - r0][0]) for r in range(r0, r0 + batch)]
            for args in pending:
                dbt, dcbt, dct = post(*args, dbt, dcbt, dct)
            pending = [(r, mms[r - r0][0], mms[r - r0][1], preps[r - r0][1]) for r in range(r0, r0 + batch)]
            dxs += [m[3] for m in mms]
            news += [m[2] for m in mms]
        for args in pending:
            dbt, dcbt, dct = post(*args, dbt, dcbt, dct)
        for p in range(HPG // 2):
            dx_ref[0, :, psl[2 * p]] = jnp.where(first, dxs[2 * p], dxs[2 * p + 1])
            new = dh[:, psl[2 * p]] * jnp.where(first, dcy[2 * p:2 * p + 1], dcy[2 * p + 1:2 * p + 2])
            if has_y:
                new = new + jnp.where(first, news[2 * p], news[2 * p + 1])
            dh_scr[:, psl[2 * p]] = new
        db = dbt.T
        if has_y:
            dcbt_bf = dcbt.astype(BF16)
            db = db + _dot(dcbt_bf, cm.astype(BF16))
            dc_ref[0] = dct.T + _dot_tn(dcbt_bf, bm.astype(BF16))
        db_ref[0] = db
        s_row = _stack_rows(in_rows)
        lane = lax.broadcasted_iota(jnp.int32, (HPG, CHUNK), 1)
        dcs = jnp.where(lane == last, _stack_rows(tots), 0.0)
        if has_y:
            s_row = s_row + _cols_to_rows(in_cols)
            dcs = dcs + _stack_rows(out_rows)
        dcs = dcs - s_row
        da = _cumsum_lanes(dcs, not reverse)
        ddt = da * A + jnp.where(dt > 0.0, s_row / dt, 0.0)
        ddtraw = ddt * _sigmoid(dtraw + bias_ref[0])
        ddt_ref[0] = ddtraw
        dbias_ref[0, 0] += jnp.sum(ddtraw, axis=1, keepdims=True)
        dalog_ref[0, 0] += jnp.sum(da * dt, axis=1, keepdims=True) * A

        @pl.when(s == nc - 1)
        def _():
            dh0_ref[0, 0] = dh_scr[...]

    cidx = lambda b, g, s: (b, chunk_of(s), g)
    hidx = lambda b, g, s: (b, g, 0, 0)
    in_specs = [
        pl.BlockSpec((1, HPG, CHUNK), lambda b, g, s: (b, rowblk + g, chunk_of(s))),
        pl.BlockSpec((1, HPG, 1), lambda b, g, s: (rowblk + g, 0, 0)),
        pl.BlockSpec((1, HPG, 1), lambda b, g, s: (rowblk + g, 0, 0)),
        pl.BlockSpec((1, CHUNK, GW), cidx),
        pl.BlockSpec((1, CHUNK, D_STATE), lambda b, g, s: (b, chunk_of(s), D_INNER // D_STATE + g)),
        pl.BlockSpec((1, 1, 1, D_STATE, GW), lambda b, g, s: (b, g, chunk_of(s), 0, 0)),
        pl.BlockSpec((1, 1, D_STATE, GW), hidx),
    ]
    args = [dtT, bias, alog, xbc, xbc, h_start, dh_final]
    if has_y:
        in_specs += [pl.BlockSpec((1, CHUNK, D_STATE), lambda b, g, s: (b, chunk_of(s), D_INNER // D_STATE + N_BC + g)),
                     pl.BlockSpec((1, CHUNK, GW), cidx)]
        args += [xbc, dy]
    out_shape = [jax.ShapeDtypeStruct((Bn, L, D_INNER), F32), jax.ShapeDtypeStruct((Bn, L, N_BC * D_STATE), F32),
                 jax.ShapeDtypeStruct((Bn, N_HEADS, L), F32), jax.ShapeDtypeStruct((Bn, N_BC, HPG, 1), F32),
                 jax.ShapeDtypeStruct((Bn, N_BC, HPG, 1), F32), jax.ShapeDtypeStruct((Bn, N_BC, D_STATE, GW), F32)]
    out_specs = [pl.BlockSpec((1, CHUNK, GW), cidx), pl.BlockSpec((1, CHUNK, D_STATE), cidx),
                 pl.BlockSpec((1, HPG, CHUNK), lambda b, g, s: (b, g, chunk_of(s))),
                 pl.BlockSpec((1, 1, HPG, 1), hidx), pl.BlockSpec((1, 1, HPG, 1), hidx), pl.BlockSpec((1, 1, D_STATE, GW), hidx)]
    if has_y:
        out_shape.append(jax.ShapeDtypeStruct((Bn, L, N_BC * D_STATE), F32))
        out_specs.append(pl.BlockSpec((1, CHUNK, D_STATE), cidx))
    res = pl.pallas_call(
        kern, name=name, out_shape=out_shape, grid=(Bn, N_BC, nc), in_specs=in_specs, out_specs=out_specs,
        scratch_shapes=[pltpu.VMEM((D_STATE, GW), F32)],
        compiler_params=_cp(("arbitrary", "arbitrary", "arbitrary")))(*args)
    dxs, db, ddt, dbias, dalog, dh0 = res[:6]
    return dxs, db, (res[6] if has_y else None), ddt, dbias, dalog, dh0


GPS = 4


def ssd_fwd3(name, dtT, bias, alog, xbc, h0, direction, with_y):
    Bn, L = xbc.shape[:2]
    nc = L // CHUNK
    reverse = direction == 1
    blk0 = direction * (N_BC // GPS)
    gs = range(GPS)

    def chunk_of(s):
        return (nc - 1 - s) if reverse else s

    def kern(dt_ref, bias_ref, alog_ref, x_ref, b_ref, c_ref, h0_ref, *rest):
        if with_y:
            y_ref, hs_ref, hf_ref, h_scr = rest
        else:
            hs_ref, hf_ref, h_scr = rest
        s = pl.program_id(2)

        @pl.when(s == 0)
        def _():
            h_scr[...] = h0_ref[0]

        first = lax.broadcasted_iota(jnp.int32, (1, LANES), 1) < HEAD_DIM
        heads = range(HPG)
        psl = [slice((r // 2) * LANES, (r // 2 + 1) * LANES) for r in heads]
        keep = _tri_mask(False, reverse)
        sc, x_bf, bm, h, h_bf, bt, cm, cb, cs_cols = [], [], [], [], [], [], [], [], []
        for g in gs:
            dt, _, cs, total = _ssd_scalars(dt_ref[0, g * HPG:(g + 1) * HPG], bias_ref[g], alog_ref[g], reverse)
            u = cs - jnp.log(dt)
            sc.append((cs, u, jnp.exp(total - u), jnp.exp(total)))
            x_bf.append(x_ref[0, :, g * GW:(g + 1) * GW].astype(BF16))
            bm.append(b_ref[0, :, g * D_STATE:(g + 1) * D_STATE])
            h.append(h_scr[g])
            h_bf.append(h[g].astype(BF16))
            hs_ref[0, g, 0] = h[g]
            bt.append(bm[g].T)
            if with_y:
                cm.append(c_ref[0, :, g * D_STATE:(g + 1) * D_STATE])
                cb.append(_dot_nt(cm[g].astype(BF16), bm[g].astype(BF16)))
                cs_cols.append(_rows_to_cols(cs))
        lhs = [[] for _ in gs]
        if with_y:
            for g in gs:
                cs, u = sc[g][0], sc[g][1]
                for r in heads:
                    cs_col = jnp.broadcast_to(cs_cols[g][:, r:r + 1], (CHUNK, LANES))
                    wf = cb[g] * jnp.exp(jnp.where(keep, cs_col - u[r:r + 1], -jnp.inf))
                    lhs[g].append(jnp.concatenate([wf.astype(BF16), (cm[g] * jnp.exp(cs_col)).astype(BF16)], axis=1))
        bts = [[(bt[g] * sc[g][2][r:r + 1]).astype(BF16) for r in heads] for g in gs]
        sts = [[_dot(bts[g][r], x_bf[g][:, psl[r]]) for r in heads] for g in gs]
        if with_y:
            ys = [[_dot(lhs[g][r], jnp.concatenate([x_bf[g][:, psl[r]], h_bf[g][:, psl[r]]], axis=0)) for r in heads] for g in gs]
        for g in gs:
            dc = sc[g][3]
            for p in range(HPG // 2):
                if with_y:
                    y_ref[0, :, g * GW + p * LANES:g * GW + (p + 1) * LANES] = jnp.where(first, ys[g][2 * p], ys[g][2 * p + 1])
                dc_p = jnp.where(first, dc[2 * p:2 * p + 1], dc[2 * p + 1:2 * p + 2])
                h_scr[g, :, psl[2 * p]] = h[g][:, psl[2 * p]] * dc_p + jnp.where(first, sts[g][2 * p], sts[g][2 * p + 1])

        @pl.when(s == nc - 1)
        def _():
            hf_ref[0] = h_scr[...]

    nb = D_INNER // (GPS * D_STATE)
    in_specs = [
        pl.BlockSpec((1, GPS * HPG, CHUNK), lambda b, g, s: (b, blk0 + g, chunk_of(s))),
        pl.BlockSpec((GPS, HPG, 1), lambda b, g, s: (blk0 + g, 0, 0)),
        pl.BlockSpec((GPS, HPG, 1), lambda b, g, s: (blk0 + g, 0, 0)),
        pl.BlockSpec((1, CHUNK, GPS * GW), lambda b, g, s: (b, chunk_of(s), g)),
        pl.BlockSpec((1, CHUNK, GPS * D_STATE), lambda b, g, s: (b, chunk_of(s), nb + g)),
        pl.BlockSpec((1, CHUNK, GPS * D_STATE), lambda b, g, s: (b, chunk_of(s), nb + N_BC // GPS + g)),
        pl.BlockSpec((1, GPS, D_STATE, GW), lambda b, g, s: (b, g, 0, 0)),
    ]
    out_shape, out_specs = [], []
    if with_y:
        out_shape.append(jax.ShapeDtypeStruct((Bn, L, D_INNER), F32))
        out_specs.append(pl.BlockSpec((1, CHUNK, GPS * GW), lambda b, g, s: (b, chunk_of(s), g)))
    out_shape += [jax.ShapeDtypeStruct((Bn, N_BC, nc, D_STATE, GW), F32), jax.ShapeDtypeStruct((Bn, N_BC, D_STATE, GW), F32)]
    out_specs += [pl.BlockSpec((1, GPS, 1, D_STATE, GW), lambda b, g, s: (b, g, chunk_of(s), 0, 0)),
                  pl.BlockSpec((1, GPS, D_STATE, GW), lambda b, g, s: (b, g, 0, 0))]
    return pl.pallas_call(
        kern, name=name, out_shape=out_shape, grid=(Bn, N_BC // GPS, nc), in_specs=in_specs, out_specs=out_specs,
        scratch_shapes=[pltpu.VMEM((GPS, D_STATE, GW), F32)],
        compiler_params=_cp(("arbitrary", "arbitrary", "arbitrary")))(dtT, bias, alog, xbc, xbc, xbc, h0)


def ssd_bwd3(name, dtT, bias, alog, xbc, h_start, dy, dh_final, direction):
    Bn, L = xbc.shape[:2]
    nc = L // CHUNK
    reverse = direction == 1
    blk0 = direction * (N_BC // GPS)
    has_y = dy is not None
    last = 0 if reverse else CHUNK - 1
    gs = range(GPS)

    def chunk_of(s):
        return s if reverse else (nc - 1 - s)

    def kern(*refs):
        if has_y:
            (dt_ref, bias_ref, alog_ref, x_ref, b_ref, hs_ref, dhf_ref, c_ref, dy_ref,
             dx_ref, db_ref, ddt_ref, dbias_ref, dalog_ref, dh0_ref, dc_ref, dh_scr) = refs
        else:
            (dt_ref, bias_ref, alog_ref, x_ref, b_ref, hs_ref, dhf_ref,
             dx_ref, db_ref, ddt_ref, dbias_ref, dalog_ref, dh0_ref, dh_scr) = refs
        s = pl.program_id(2)

        @pl.when(s == 0)
        def _():
            dh_scr[...] = dhf_ref[0]
            dbias_ref[...] = jnp.zeros(dbias_ref.shape, F32)
            dalog_ref[...] = jnp.zeros(dalog_ref.shape, F32)

        first = lax.broadcasted_iota(jnp.int32, (1, LANES), 1) < HEAD_DIM
        heads = range(HPG)
        psl = [slice((r // 2) * LANES, (r // 2 + 1) * LANES) for r in heads]
        mine = [first if r % 2 == 0 else jnp.logical_not(first) for r in heads]
        zeros_bf = jnp.zeros((CHUNK, LANES), BF16)
        keep = _tri_mask(True, reverse)
        ctx = []
        for g in gs:
            dtraw = dt_ref[0, g * HPG:(g + 1) * HPG]
            dt, A, cs, total = _ssd_scalars(dtraw, bias_ref[g], alog_ref[g], reverse)
            u = cs - jnp.log(dt)
            c = dict(dtraw=dtraw, dt=dt, A=A, cs=cs, total=total, u=u, dtt=jnp.exp(total - u), dcy=jnp.exp(total),
                     u_cols=_rows_to_cols(u), x_bf=x_ref[0, :, g * GW:(g + 1) * GW].astype(BF16),
                     bm=b_ref[0, :, g * D_STATE:(g + 1) * D_STATE], h=hs_ref[0, g, 0], dh=dh_scr[g])
            c["bt"] = c["bm"].T
            c["dh_bf"] = c["dh"].astype(BF16)
            if has_y:
                c["cm"] = c_ref[0, :, g * D_STATE:(g + 1) * D_STATE]
                c["ct"] = c["cm"].T
                c["e_row"] = jnp.exp(cs)
                c["dy_bf"] = dy_ref[0, :, g * GW:(g + 1) * GW].astype(BF16)
                c["h_bf"] = c["h"].astype(BF16)
                c["cbt"] = _dot_nt(c["bm"].astype(BF16), c["cm"].astype(BF16))
            ctx.append(c)
        for c in ctx:
            c["lhs"], c["et"] = [], []
            for r in heads:
                u_col = jnp.broadcast_to(c["u_cols"][:, r:r + 1], (CHUNK, LANES))
                bs = (c["bm"] * jnp.exp(c["total"][r:r + 1] - u_col)).astype(BF16)
                if has_y:
                    et = jnp.exp(jnp.where(keep, c["cs"][r:r + 1] - u_col, -jnp.inf))
                    c["et"].append(et)
                    c["lhs"].append(jnp.concatenate([(c["cbt"] * et).astype(BF16), bs], axis=1))
                else:
                    c["lhs"].append(bs)
        for c in ctx:
            c["p2raw"] = [_dot_nt(c["dh_bf"][:, psl[r]], jnp.where(mine[r], c["x_bf"][:, psl[r]], zeros_bf)) for r in heads]
            if has_y:
                c["a1"] = [_dot_nt(jnp.concatenate([c["x_bf"][:, psl[r]], c["h_bf"][:, psl[r]]], axis=0),
                                   jnp.where(mine[r], c["dy_bf"][:, psl[r]], zeros_bf)) for r in heads]
                c["news"] = [_dot((c["ct"] * c["e_row"][r:r + 1]).astype(BF16), c["dy_bf"][:, psl[r]]) for r in heads]
                c["dxs"] = [_dot(c["lhs"][r], jnp.concatenate([c["dy_bf"][:, psl[r]], c["dh_bf"][:, psl[r]]], axis=0)) for r in heads]
            else:
                c["dxs"] = [_dot(c["lhs"][r], c["dh_bf"][:, psl[r]]) for r in heads]
        for g, c in enumerate(ctx):
            dbt = jnp.zeros((D_STATE, CHUNK), F32)
            dcbt = jnp.zeros((CHUNK, CHUNK), F32)
            dct = jnp.zeros((D_STATE, CHUNK), F32)
            tots, out_rows, in_rows, in_cols = [], [], [], []
            for r in heads:
                if has_y:
                    pt = c["a1"][r][0:CHUNK] * c["et"][r]
                    dcbt = dcbt + pt
                    mt = pt * c["cbt"]
                    ph = c["a1"][r][CHUNK:] * c["e_row"][r:r + 1]
                    dct = dct + ph
                    out_rows.append(_colsum(mt + c["ct"] * ph))
                    in_cols.append(jnp.sum(mt, axis=1, keepdims=True))
                p2 = c["p2raw"][r] * c["dtt"][r:r + 1]
                dbt = dbt + p2
                t_term = _colsum(c["bt"] * p2)
                in_rows.append(t_term)
                hdh = c["h"][:, psl[r]] * c["dh"][:, psl[r]]
                tot = jnp.sum(t_term, axis=1, keepdims=True) + c["dcy"][r:r + 1] * jnp.sum(jnp.where(mine[r], hdh, 0.0), keepdims=True)
                tots.append(jnp.broadcast_to(tot, (1, CHUNK)))
            for p in range(HPG // 2):
                dx_ref[0, :, g * GW + p * LANES:g * GW + (p + 1) * LANES] = jnp.where(first, c["dxs"][2 * p], c["dxs"][2 * p + 1])
                new = c["dh"][:, psl[2 * p]] * jnp.where(first, c["dcy"][2 * p:2 * p + 1], c["dcy"][2 * p + 1:2 * p + 2])
                if has_y:
                    new = new + jnp.where(first, c["news"][2 * p], c["news"][2 * p + 1])
                dh_scr[g, :, psl[2 * p]] = new
            db = dbt.T
            if has_y:
                dcbt_bf = dcbt.astype(BF16)
                db = db + _dot(dcbt_bf, c["cm"].astype(BF16))
                dc_ref[0, :, g * D_STATE:(g + 1) * D_STATE] = dct.T + _dot_tn(dcbt_bf, c["bm"].astype(BF16))
            db_ref[0, :, g * D_STATE:(g + 1) * D_STATE] = db
            s_row = _stack_rows(in_rows)
            lane = lax.broadcasted_iota(jnp.int32, (HPG, CHUNK), 1)
            dcs = jnp.where(lane == last, _stack_rows(tots), 0.0)
            if has_y:
                s_row = s_row + _cols_to_rows(in_cols)
                dcs = dcs + _stack_rows(out_rows)
            dcs = dcs - s_row
            da = _cumsum_lanes(dcs, not reverse)
            ddt = da * c["A"] + jnp.where(c["dt"] > 0.0, s_row / c["dt"], 0.0)
            ddtraw = ddt * _sigmoid(c["dtraw"] + bias_ref[g])
            ddt_ref[0, g * HPG:(g + 1) * HPG] = ddtraw
            dbias_ref[0, g] += jnp.sum(ddtraw, axis=1, keepdims=True)
            dalog_ref[0, g] += jnp.sum(da * c["dt"], axis=1, keepdims=True) * c["A"]

        @pl.when(s == nc - 1)
        def _():
            dh0_ref[0] = dh_scr[...]

    nb = D_INNER // (GPS * D_STATE)
    cidx = lambda b, g, s: (b, chunk_of(s), g)
    hidx = lambda b, g, s: (b, g, 0, 0)
    in_specs = [
        pl.BlockSpec((1, GPS * HPG, CHUNK), lambda b, g, s: (b, blk0 + g, chunk_of(s))),
        pl.BlockSpec((GPS, HPG, 1), lambda b, g, s: (blk0 + g, 0, 0)),
        pl.BlockSpec((GPS, HPG, 1), lambda b, g, s: (blk0 + g, 0, 0)),
        pl.BlockSpec((1, CHUNK, GPS * GW), cidx),
        pl.BlockSpec((1, CHUNK, GPS * D_STATE), lambda b, g, s: (b, chunk_of(s), nb + g)),
        pl.BlockSpec((1, GPS, 1, D_STATE, GW), lambda b, g, s: (b, g, chunk_of(s), 0, 0)),
        pl.BlockSpec((1, GPS, D_STATE, GW), hidx),
    ]
    args = [dtT, bias, alog, xbc, xbc, h_start, dh_final]
    if has_y:
        in_specs += [pl.BlockSpec((1, CHUNK, GPS * D_STATE), lambda b, g, s: (b, chunk_of(s), nb + N_BC // GPS + g)),
                     pl.BlockSpec((1, CHUNK, GPS * GW), cidx)]
        args += [xbc, dy]
    out_shape = [jax.ShapeDtypeStruct((Bn, L, D_INNER), F32), jax.ShapeDtypeStruct((Bn, L, N_BC * D_STATE), F32),
                 jax.ShapeDtypeStruct((Bn, N_HEADS, L), F32), jax.ShapeDtypeStruct((Bn, N_BC, HPG, 1), F32),
                 jax.ShapeDtypeStruct((Bn, N_BC, HPG, 1), F32), jax.ShapeDtypeStruct((Bn, N_BC, D_STATE, GW), F32)]
    out_specs = [pl.BlockSpec((1, CHUNK, GPS * GW), cidx), pl.BlockSpec((1, CHUNK, GPS * D_STATE), cidx),
                 pl.BlockSpec((1, GPS * HPG, CHUNK), lambda b, g, s: (b, g, chunk_of(s))),
                 pl.BlockSpec((1, GPS, HPG, 1), hidx), pl.BlockSpec((1, GPS, HPG, 1), hidx), pl.BlockSpec((1, GPS, D_STATE, GW), hidx)]
    if has_y:
        out_shape.append(jax.ShapeDtypeStruct((Bn, L, N_BC * D_STATE), F32))
        out_specs.append(pl.BlockSpec((1, CHUNK, GPS * D_STATE), cidx))
    res = pl.pallas_call(
        kern, name=name, out_shape=out_shape, grid=(Bn, N_BC // GPS, nc), in_specs=in_specs, out_specs=out_specs,
        scratch_shapes=[pltpu.VMEM((GPS, D_STATE, GW), F32)],
        compiler_params=_cp(("arbitrary", "arbitrary", "arbitrary")))(*args)
    dxs, db, ddt, dbias, dalog, dh0 = res[:6]
    return dxs, db, (res[6] if has_y else None), ddt, dbias, dalog, dh0


def _dot_split2(v, sel):
    hi = v.astype(BF16)
    mid = (v - hi.astype(F32)).astype(BF16)
    return _dot(hi, sel) + _dot(mid, sel)


def ssd_tables():
    lane = jnp.arange(LANES)[:, None]
    col = jnp.arange(2 * GW)[None, :]
    expand = (lane == jnp.where(col < GW, HPG + col // HEAD_DIM, 2 * HPG + (col - GW) // HEAD_DIM)).astype(BF16)
    ch = jnp.arange(GW)[:, None] // HEAD_DIM
    out = jnp.arange(2 * LANES)[None, :]
    seg = ((out == ch) | (out == LANES + HPG + ch)).astype(BF16)
    return expand, seg


def _dc_lanes(dc, first):
    return jnp.concatenate([jnp.where(first, dc[2 * p:2 * p + 1], dc[2 * p + 1:2 * p + 2]) for p in range(HPG // 2)], axis=1)


def ssd_fwd2(name, dtT, bias, alog, xbc, h0, tables, direction, with_y):
    Bn, L = xbc.shape[:2]
    nc = L // CHUNK
    reverse = direction == 1
    rowblk = direction * N_BC
    expand = tables[0]

    def chunk_of(s):
        return (nc - 1 - s) if reverse else s

    def kern(dt_ref, bias_ref, alog_ref, x_ref, b_ref, c_ref, h0_ref, xp_ref, *rest):
        if with_y:
            y_ref, hs_ref, hf_ref, h_scr = rest
        else:
            hs_ref, hf_ref, h_scr = rest
        s = pl.program_id(2)

        @pl.when(s == 0)
        def _():
            h_scr[...] = h0_ref[0, 0]

        dt, _, cs, total = _ssd_scalars(dt_ref[0], bias_ref[0], alog_ref[0], reverse)
        u = cs - jnp.log(dt)
        dtt = jnp.exp(total - u)
        cols = _rows_to_cols(jnp.concatenate([cs, dtt, jnp.exp(cs)], axis=0))
        wide = _dot_split2(cols, xp_ref[...])
        dtt_x, e_x = wide[:, 0:GW], wide[:, GW:]
        first = lax.broadcasted_iota(jnp.int32, (1, LANES), 1) < HEAD_DIM
        x = x_ref[0]
        x_bf = x.astype(BF16)
        bm = b_ref[0]
        h = h_scr[...]
        hs_ref[0, 0, 0] = h
        st = _dot(bm.T.astype(BF16), (x * dtt_x).astype(BF16))
        h_scr[...] = h * _dc_lanes(jnp.exp(total), first) + st
        if with_y:
            cm = c_ref[0].astype(BF16)
            cb = _dot_nt(cm, bm.astype(BF16))
            yoff = _dot(cm, h.astype(BF16)) * e_x
            keep = _tri_mask(False, reverse)
            wfs = []
            for r in range(HPG):
                cs_col = jnp.broadcast_to(cols[:, r:r + 1], (CHUNK, LANES))
                wfs.append((cb * jnp.exp(jnp.where(keep, cs_col - u[r:r + 1], -jnp.inf))).astype(BF16))
            yd = [_dot(wfs[r], x_bf[:, (r // 2) * LANES:(r // 2 + 1) * LANES]) for r in range(HPG)]
            for p in range(HPG // 2):
                psl = slice(p * LANES, (p + 1) * LANES)
                y_ref[0, :, psl] = jnp.where(first, yd[2 * p], yd[2 * p + 1]) + yoff[:, psl]

        @pl.when(s == nc - 1)
        def _():
            hf_ref[0, 0] = h_scr[...]

    in_specs = [
        pl.BlockSpec((1, HPG, CHUNK), lambda b, g, s: (b, rowblk + g, chunk_of(s))),
        pl.BlockSpec((1, HPG, 1), lambda b, g, s: (rowblk + g, 0, 0)),
        pl.BlockSpec((1, HPG, 1), lambda b, g, s: (rowblk + g, 0, 0)),
        pl.BlockSpec((1, CHUNK, GW), lambda b, g, s: (b, chunk_of(s), g)),
        pl.BlockSpec((1, CHUNK, D_STATE), lambda b, g, s: (b, chunk_of(s), D_INNER // D_STATE + g)),
        pl.BlockSpec((1, CHUNK, D_STATE), lambda b, g, s: (b, chunk_of(s), D_INNER // D_STATE + N_BC + g)),
        pl.BlockSpec((1, 1, D_STATE, GW), lambda b, g, s: (b, g, 0, 0)),
        pl.BlockSpec(expand.shape, lambda b, g, s: (0, 0)),
    ]
    out_shape, out_specs = [], []
    if with_y:
        out_shape.append(jax.ShapeDtypeStruct((Bn, L, D_INNER), F32))
        out_specs.append(pl.BlockSpec((1, CHUNK, GW), lambda b, g, s: (b, chunk_of(s), g)))
    out_shape += [jax.ShapeDtypeStruct((Bn, N_BC, nc, D_STATE, GW), F32), jax.ShapeDtypeStruct((Bn, N_BC, D_STATE, GW), F32)]
    out_specs += [pl.BlockSpec((1, 1, 1, D_STATE, GW), lambda b, g, s: (b, g, chunk_of(s), 0, 0)),
                  pl.BlockSpec((1, 1, D_STATE, GW), lambda b, g, s: (b, g, 0, 0))]
    return pl.pallas_call(
        kern, name=name, out_shape=out_shape, grid=(Bn, N_BC, nc), in_specs=in_specs, out_specs=out_specs,
        scratch_shapes=[pltpu.VMEM((D_STATE, GW), F32)],
        compiler_params=_cp(("arbitrary", "arbitrary", "arbitrary")))(dtT, bias, alog, xbc, xbc, xbc, h0, expand)


def ssd_bwd2(name, dtT, bias, alog, xbc, h_start, dy, dh_final, tables, direction):
    Bn, L = xbc.shape[:2]
    nc = L // CHUNK
    reverse = direction == 1
    rowblk = direction * N_BC
    has_y = dy is not None
    last = 0 if reverse else CHUNK - 1
    expand, seg = tables

    def chunk_of(s):
        return s if reverse else (nc - 1 - s)

    def kern(*refs):
        if has_y:
            (dt_ref, bias_ref, alog_ref, x_ref, b_ref, hs_ref, dhf_ref, xp_ref, seg_ref, c_ref, dy_ref,
             dx_ref, db_ref, ddt_ref, dbias_ref, dalog_ref, dh0_ref, dc_ref, dh_scr) = refs
        else:
            (dt_ref, bias_ref, alog_ref, x_ref, b_ref, hs_ref, dhf_ref, xp_ref, seg_ref,
             dx_ref, db_ref, ddt_ref, dbias_ref, dalog_ref, dh0_ref, dh_scr) = refs
        s = pl.program_id(2)

        @pl.when(s == 0)
        def _():
            dh_scr[...] = dhf_ref[0, 0]
            dbias_ref[...] = jnp.zeros(dbias_ref.shape, F32)
            dalog_ref[...] = jnp.zeros(dalog_ref.shape, F32)

        first = lax.broadcasted_iota(jnp.int32, (1, LANES), 1) < HEAD_DIM
        heads = range(HPG)
        psl = [slice((r // 2) * LANES, (r // 2 + 1) * LANES) for r in heads]
        x = x_ref[0]
        bm = b_ref[0].astype(BF16)
        h = hs_ref[0, 0, 0]
        dh = dh_scr[...]
        dh_bf = dh.astype(BF16)
        bdh = _dot(bm, dh_bf)
        if has_y:
            cm = c_ref[0].astype(BF16)
            dyv = dy_ref[0]
            dy_bf = dyv.astype(BF16)
            x_bf = x.astype(BF16)
            h_bf = h.astype(BF16)
            cbt = _dot_nt(bm, cm)
            ch = _dot(cm, h_bf)
            zeros_bf = jnp.zeros((CHUNK, LANES), BF16)
            gts = [_dot_nt(x_bf[:, psl[r]], jnp.where(first if r % 2 == 0 else jnp.logical_not(first), dy_bf[:, psl[r]], zeros_bf))
                   for r in heads]
            ct_bf = c_ref[0].T.astype(BF16)
        dtraw = dt_ref[0]
        dt, A, cs, total = _ssd_scalars(dtraw, bias_ref[0], alog_ref[0], reverse)
        u = cs - jnp.log(dt)
        dtt = jnp.exp(total - u)
        dcy = jnp.exp(total)
        cols = _rows_to_cols(jnp.concatenate([u, dtt, jnp.exp(cs)], axis=0))
        wide = _dot_split2(cols, xp_ref[...])
        dtt_x, e_x = wide[:, 0:GW], wide[:, GW:]
        term2 = bdh * dtt_x
        dbt = _dot_nt(dh_bf, (x * dtt_x).astype(BF16))
        sums = _dot_split2(term2 * x, seg_ref[:, LANES:])
        new_dh = dh * _dc_lanes(dcy, first)
        if has_y:
            dye = dyv * e_x
            dye_bf = dye.astype(BF16)
            dct = _dot_nt(h_bf, dye_bf)
            new_dh = new_dh + _dot(ct_bf, dye_bf)
            sums = sums + _dot_split2(ch * dye, seg_ref[:, 0:LANES])
            keep = _tri_mask(True, reverse)
            ets = []
            for r in heads:
                u_col = jnp.broadcast_to(cols[:, r:r + 1], (CHUNK, LANES))
                ets.append(jnp.exp(jnp.where(keep, cs[r:r + 1] - u_col, -jnp.inf)))
            wts = [(cbt * ets[r]).astype(BF16) for r in heads]
            dxd = [_dot(wts[r], dy_bf[:, psl[r]]) for r in heads]
            dcbt = jnp.zeros((CHUNK, CHUNK), F32)
            out_rows, in_cols = [], []
            for r in heads:
                pt = gts[r] * ets[r]
                dcbt = dcbt + pt
                mt = pt * cbt
                out_rows.append(_colsum(mt))
                in_cols.append(jnp.sum(mt, axis=1, keepdims=True))
            for p in range(HPG // 2):
                dx_ref[0, :, psl[2 * p]] = jnp.where(first, dxd[2 * p], dxd[2 * p + 1]) + term2[:, psl[2 * p]]
            dcbt_bf = dcbt.astype(BF16)
            db_ref[0] = dbt.T + _dot(dcbt_bf, cm)
            dc_ref[0] = dct.T + _dot_tn(dcbt_bf, bm)
        else:
            dx_ref[0] = term2
            db_ref[0] = dbt.T
        dh_scr[...] = new_dh
        sums_t = sums.T
        s_row = sums_t[HPG:2 * HPG]
        hdh = _colsum(h * dh)
        lanes_w = lax.broadcasted_iota(jnp.int32, (1, GW), 1)
        hd = _stack_rows([jnp.sum(jnp.where(lanes_w // HEAD_DIM == r, hdh, 0.0), axis=1, keepdims=True) for r in range(HPG)])
        tot = jnp.sum(s_row, axis=1, keepdims=True) + dcy * hd
        lane = lax.broadcasted_iota(jnp.int32, (HPG, CHUNK), 1)
        dcs = jnp.where(lane == last, tot, 0.0)
        if has_y:
            s_row = s_row + _cols_to_rows(in_cols)
            dcs = dcs + _stack_rows(out_rows) + sums_t[0:HPG]
        dcs = dcs - s_row
        da = _cumsum_lanes(dcs, not reverse)
        ddt = da * A + jnp.where(dt > 0.0, s_row / dt, 0.0)
        ddtraw = ddt * _sigmoid(dtraw + bias_ref[0])
        ddt_ref[0] = ddtraw
        dbias_ref[0, 0] += jnp.sum(ddtraw, axis=1, keepdims=True)
        dalog_ref[0, 0] += jnp.sum(da * dt, axis=1, keepdims=True) * A

        @pl.when(s == nc - 1)
        def _():
            dh0_ref[0, 0] = dh_scr[...]

    cidx = lambda b, g, s: (b, chunk_of(s), g)
    hidx = lambda b, g, s: (b, g, 0, 0)
    in_specs = [
        pl.BlockSpec((1, HPG, CHUNK), lambda b, g, s: (b, rowblk + g, chunk_of(s))),
        pl.BlockSpec((1, HPG, 1), lambda b, g, s: (rowblk + g, 0, 0)),
        pl.BlockSpec((1, HPG, 1), lambda b, g, s: (rowblk + g, 0, 0)),
        pl.BlockSpec((1, CHUNK, GW), cidx),
        pl.BlockSpec((1, CHUNK, D_STATE), lambda b, g, s: (b, chunk_of(s), D_INNER // D_STATE + g)),
        pl.BlockSpec((1, 1, 1, D_STATE, GW), lambda b, g, s: (b, g, chunk_of(s), 0, 0)),
        pl.BlockSpec((1, 1, D_STATE, GW), hidx),
        pl.BlockSpec(expand.shape, lambda b, g, s: (0, 0)),
        pl.BlockSpec(seg.shape, lambda b, g, s: (0, 0)),
    ]
    args = [dtT, bias, alog, xbc, xbc, h_start, dh_final, expand, seg]
    if has_y:
        in_specs += [pl.BlockSpec((1, CHUNK, D_STATE), lambda b, g, s: (b, chunk_of(s), D_INNER // D_STATE + N_BC + g)),
                     pl.BlockSpec((1, CHUNK, GW), cidx)]
        args += [xbc, dy]
    out_shape = [jax.ShapeDtypeStruct((Bn, L, D_INNER), F32), jax.ShapeDtypeStruct((Bn, L, N_BC * D_STATE), F32),
                 jax.ShapeDtypeStruct((Bn, N_HEADS, L), F32), jax.ShapeDtypeStruct((Bn, N_BC, HPG, 1), F32),
                 jax.ShapeDtypeStruct((Bn, N_BC, HPG, 1), F32), jax.ShapeDtypeStruct((Bn, N_BC, D_STATE, GW), F32)]
    out_specs = [pl.BlockSpec((1, CHUNK, GW), cidx), pl.BlockSpec((1, CHUNK, D_STATE), cidx),
                 pl.BlockSpec((1, HPG, CHUNK), lambda b, g, s: (b, g, chunk_of(s))),
                 pl.BlockSpec((1, 1, HPG, 1), hidx), pl.BlockSpec((1, 1, HPG, 1), hidx), pl.BlockSpec((1, 1, D_STATE, GW), hidx)]
    if has_y:
        out_shape.append(jax.ShapeDtypeStruct((Bn, L, N_BC * D_STATE), F32))
        out_specs.append(pl.BlockSpec((1, CHUNK, D_STATE), cidx))
    res = pl.pallas_call(
        kern, name=name, out_shape=out_shape, grid=(Bn, N_BC, nc), in_specs=in_specs, out_specs=out_specs,
        scratch_shapes=[pltpu.VMEM((D_STATE, GW), F32)],
        compiler_params=_cp(("arbitrary", "arbitrary", "arbitrary")))(*args)
    dxs, db, ddt, dbias, dalog, dh0 = res[:6]
    return dxs, db, (res[6] if has_y else None), ddt, dbias, dalog, dh0


def _group_mean(v):
    gw = D_INNER // N_BC
    parts = [jnp.broadcast_to(jnp.mean(v[:, g * gw:(g + 1) * gw], axis=-1, keepdims=True), (v.shape[0], gw)) for g in range(N_BC)]
    return jnp.concatenate(parts, axis=1)


def gated_norm_fwd(name, y_f, y_b, xs_src, z, dskip_lanes, w_norm):
    def body(yf, yb, xs, z, dsk, w):
        u = (yf + yb + dsk * xs) * _silu(z)
        r = lax.rsqrt(_group_mean(u * u) + NORM_EPS)
        return u * r * w

    return tok_call(name, body, [y_f, y_b, xs_src, z], [], [dskip_lanes, w_norm], [(D_INNER, BF16)], [], [])[0]


def _dot_exact01(v, sel):
    hi, mid, lo = _split3(v)
    return _dot(hi, sel) + _dot(mid, sel) + _dot(lo, sel)


def gated_norm_bwd(name, y_f, y_b, xs_src, z, d_out, dskip_lanes, w_norm, head_sel):
    def body(yf, yb, xs, z, do, dsk, w, sel):
        y = yf + yb + dsk * xs
        sz = _silu(z)
        u = y * sz
        r = lax.rsqrt(_group_mean(u * u) + NORM_EPS)
        duh = do * w
        du = r * (duh - u * (r * r) * _group_mean(duh * u))
        dy = du * sz
        dz = du * y * _dsilu(z)
        dsk_heads = _dot_exact01(jnp.broadcast_to(_colsum(dy * xs), (8, D_INNER)), sel)
        return dy, dz, _colsum(do * u * r), dsk_heads

    return tok_call(name, body, [y_f, y_b, xs_src, z, d_out], [], [dskip_lanes, w_norm, head_sel],
                    [(D_INNER, F32), (D_INNER, BF16)], [], [(1, D_INNER), (8, LANES)], tm=128)


def merge_fwd(name, y_pool, y_ssd, gatepre, x, target, gate, b_merge, norm_post, w_pp, w_ps, w_out):
    def body(yp, ys, gp, x, tgt, gate, bm, wpost, w_pp, w_ps, w_out):
        p1 = _dot(yp, w_pp)
        p2 = _dot(ys, w_ps)
        gates = _sigmoid(gp + bm)
        merged = gates[:, :D] * p1 + gates[:, D:] * p2
        out = _dot(merged.astype(BF16), w_out)
        r = _rms_r(out)
        outr = out * r
        nq = outr * wpost
        err = x + gate * nq - tgt
        loss = 0.5 * jnp.sum(jnp.mean(err * err, axis=-1, keepdims=True), keepdims=True).reshape(1, 1)
        g = err * (1.0 / D)
        dnq = g * gate
        dout = _rms_bwd(dnq * wpost, out, r)
        return merged, p1, p2, dout, g, _colsum(g * nq), _colsum(dnq * outr), jnp.broadcast_to(loss, (1, LANES))

    return tok_call(name, body, [y_pool, y_ssd, gatepre, x, target], [gate], [b_merge, norm_post, w_pp, w_ps, w_out],
                    [(D, BF16), (D, F32), (D, F32), (D, BF16), (D, F32)], [D], [(1, D), (1, LANES)])


def merge_bwd(name, dout, gatepre, p1, p2, b_merge, w_pp, w_ps, w_out):
    def body(dout, gp, p1, p2, bm, w_pp, w_ps, w_out):
        dmerged = _dot_nt(dout, w_out)
        gates = _sigmoid(gp + bm)
        g1, g2 = gates[:, :D], gates[:, D:]
        dp1 = (dmerged * g1).astype(BF16)
        dp2 = (dmerged * g2).astype(BF16)
        dgp = jnp.concatenate([dmerged * p1 * g1 * (1.0 - g1), dmerged * p2 * g2 * (1.0 - g2)], axis=1)
        return dp1, dp2, dgp, _dot_nt(dp1, w_pp), _dot_nt(dp2, w_ps), _colsum(dgp)

    return tok_call(name, body, [dout, gatepre, p1, p2], [], [b_merge, w_pp, w_ps, w_out],
                    [(D, BF16), (D, BF16), (2 * D, BF16), (D, F32), (D_INNER, F32)], [], [(1, 2 * D)])


def _adamw_math(w, g, m, v):
    m = ADAM_B1 * m + (1.0 - ADAM_B1) * g
    v = ADAM_B2 * v + (1.0 - ADAM_B2) * (g * g)
    m_hat = m / (1.0 - ADAM_B1 ** ADAM_STEP)
    v_hat = v / (1.0 - ADAM_B2 ** ADAM_STEP)
    delta = -ADAM_LR * (m_hat / (jnp.sqrt(v_hat) + ADAM_EPS) + ADAM_WD * w)
    return delta, m, v


def adamw(name, w, g, m, v, tr=256):
    R, C = w.shape
    tr = min(tr, R)
    assert R % tr == 0

    def body(w_ref, g_ref, m_ref, v_ref, d_ref, nm_ref, nv_ref):
        d, nm, nv = _adamw_math(w_ref[...], g_ref[...], m_ref[...], v_ref[...])
        d_ref[...] = d
        nm_ref[...] = nm
        nv_ref[...] = nv

    spec = pl.BlockSpec((tr, C), lambda i: (i, 0))
    return pl.pallas_call(
        body, name=name, out_shape=[jax.ShapeDtypeStruct((R, C), F32)] * 3, grid=(R // tr,),
        in_specs=[spec] * 4, out_specs=[spec] * 3, compiler_params=_cp(("parallel",)))(w, g, m, v)


def _me():
    return lax.axis_index("x"), lax.axis_index("y"), lax.axis_index("c")


def all_gather_small(name, v):
    R, C = v.shape

    def body(v_ref, out_ref, send_sems, recv_sems, local_sem):
        x, y, c = _me()
        me = 4 * x + 2 * y + c
        mine = pltpu.make_async_copy(v_ref, out_ref.at[me], local_sem)
        mine.start()
        copies = []
        for d in range(1, N_DEV):
            dx, dy, dc = d // 4, (d // 2) % 2, d % 2
            px, py, pc = x ^ dx, y ^ dy, c ^ dc
            copies.append(pltpu.make_async_remote_copy(
                src_ref=v_ref, dst_ref=out_ref.at[me], send_sem=send_sems.at[d - 1], recv_sem=recv_sems.at[d - 1],
                device_id=(px, py, pc), device_id_type=MESH))
        for cp in copies:
            cp.start()
        for d in range(1, N_DEV):
            dx, dy, dc = d // 4, (d // 2) % 2, d % 2
            peer = 4 * (x ^ dx) + 2 * (y ^ dy) + (c ^ dc)
            pltpu.make_async_remote_copy(
                src_ref=v_ref, dst_ref=out_ref.at[peer], send_sem=send_sems.at[d - 1], recv_sem=recv_sems.at[d - 1],
                device_id=(x ^ dx, y ^ dy, c ^ dc), device_id_type=MESH).wait_recv()
        for cp in copies:
            cp.wait_send()
        mine.wait()

    return pl.pallas_call(
        body, name=name, out_shape=jax.ShapeDtypeStruct((N_DEV, R, C), F32),
        in_specs=[pl.BlockSpec(memory_space=pltpu.VMEM)], out_specs=pl.BlockSpec(memory_space=pltpu.VMEM),
        scratch_shapes=[pltpu.SemaphoreType.DMA((N_DEV - 1,)), pltpu.SemaphoreType.DMA((N_DEV - 1,)), pltpu.SemaphoreType.DMA],
        compiler_params=pltpu.CompilerParams(vmem_limit_bytes=VMEM_LIMIT))(v)


def all_gather_chips(name, shard):
    R, C = shard.shape
    half = R // 2
    assert R % 32 == 0

    def body(s_ref, out_ref, send_sems, recv_sems):
        x, y, c = _me()
        chips = [(1 - x, y), (x, 1 - y), (1 - x, 1 - y)]

        def rows(chip, hc):
            return out_ref.at[2 * chip[0] + chip[1], pl.ds(hc * half, half), :]

        first = [pltpu.make_async_remote_copy(
            src_ref=s_ref.at[pl.ds(c * half, half), :], dst_ref=rows((x, y), c), send_sem=send_sems.at[j],
            recv_sem=recv_sems.at[j], device_id=(*chip, c), device_id_type=MESH) for j, chip in enumerate(chips)]
        for cp in first:
            cp.start()
        passed = [pltpu.make_async_remote_copy(
            src_ref=rows(chip, c), dst_ref=rows(chip, c), send_sem=send_sems.at[3 + j], recv_sem=recv_sems.at[3 + j],
            device_id=(x, y, 1 - c), device_id_type=MESH) for j, chip in enumerate(chips)]
        for j, chip in enumerate(chips):
            pltpu.make_async_remote_copy(
                src_ref=rows(chip, c), dst_ref=rows(chip, c), send_sem=send_sems.at[j], recv_sem=recv_sems.at[j],
                device_id=(*chip, c), device_id_type=MESH).wait_recv()
            passed[j].start()
        for j, chip in enumerate(chips):
            pltpu.make_async_remote_copy(
                src_ref=rows(chip, 1 - c), dst_ref=rows(chip, 1 - c), send_sem=send_sems.at[3 + j], recv_sem=recv_sems.at[3 + j],
                device_id=(x, y, 1 - c), device_id_type=MESH).wait_recv()
        for cp in first + passed:
            cp.wait_send()

    out = pl.pallas_call(
        body, name=name, out_shape=jax.ShapeDtypeStruct((N_CHIPS, R, C), shard.dtype),
        in_specs=[pl.BlockSpec(memory_space=pl.ANY)], out_specs=pl.BlockSpec(memory_space=pl.ANY),
        scratch_shapes=[pltpu.SemaphoreType.DMA((6,)), pltpu.SemaphoreType.DMA((6,))],
        compiler_params=pltpu.CompilerParams(vmem_limit_bytes=VMEM_LIMIT))(shard)
    chip = 2 * lax.axis_index("x") + lax.axis_index("y")
    return lax.dynamic_update_index_in_dim(out, shard, chip, 0)


def sibling_swap(name, v):
    def body(v_ref, out_ref, send_sem, recv_sem):
        x, y, c = _me()
        cp = pltpu.make_async_remote_copy(src_ref=v_ref, dst_ref=out_ref, send_sem=send_sem, recv_sem=recv_sem,
                                          device_id=(x, y, 1 - c), device_id_type=MESH)
        cp.start()
        cp.wait()

    return pl.pallas_call(
        body, name=name, out_shape=jax.ShapeDtypeStruct(v.shape, v.dtype),
        in_specs=[pl.BlockSpec(memory_space=pl.ANY)], out_specs=pl.BlockSpec(memory_space=pl.ANY),
        scratch_shapes=[pltpu.SemaphoreType.DMA, pltpu.SemaphoreType.DMA],
        compiler_params=pltpu.CompilerParams(vmem_limit_bytes=VMEM_LIMIT))(v)


def sibling_share(name, v):
    def body(v_ref, out_ref, send_sem, recv_sem, local_sem):
        x, y, c = _me()
        mine = pltpu.make_async_copy(v_ref, out_ref.at[c], local_sem)
        mine.start()
        cp = pltpu.make_async_remote_copy(src_ref=v_ref, dst_ref=out_ref.at[c], send_sem=send_sem, recv_sem=recv_sem,
                                          device_id=(x, y, 1 - c), device_id_type=MESH)
        cp.start()
        pltpu.make_async_remote_copy(src_ref=v_ref, dst_ref=out_ref.at[1 - c], send_sem=send_sem, recv_sem=recv_sem,
                                     device_id=(x, y, 1 - c), device_id_type=MESH).wait_recv()
        cp.wait_send()
        mine.wait()

    return pl.pallas_call(
        body, name=name, out_shape=jax.ShapeDtypeStruct((2, *v.shape), v.dtype),
        in_specs=[pl.BlockSpec(memory_space=pl.ANY)], out_specs=pl.BlockSpec(memory_space=pl.ANY),
        scratch_shapes=[pltpu.SemaphoreType.DMA, pltpu.SemaphoreType.DMA, pltpu.SemaphoreType.DMA],
        compiler_params=pltpu.CompilerParams(vmem_limit_bytes=VMEM_LIMIT))(v)


def chip_exchange(name, parts):
    def body(p_ref, out_ref, send_sems, recv_sems):
        x, y, c = _me()
        k = 2 * x + y
        chips = [(1 - x, y), (x, 1 - y), (1 - x, 1 - y)]
        sends = [pltpu.make_async_remote_copy(
            src_ref=p_ref.at[2 * chip[0] + chip[1]], dst_ref=out_ref.at[k], send_sem=send_sems.at[j], recv_sem=recv_sems.at[j],
            device_id=(*chip, c), device_id_type=MESH) for j, chip in enumerate(chips)]
        for cp in sends:
            cp.start()
        for j, chip in enumerate(chips):
            pltpu.make_async_remote_copy(
                src_ref=p_ref.at[k], dst_ref=out_ref.at[2 * chip[0] + chip[1]], send_sem=send_sems.at[j], recv_sem=recv_sems.at[j],
                device_id=(*chip, c), device_id_type=MESH).wait_recv()
        for cp in sends:
            cp.wait_send()

    out = pl.pallas_call(
        body, name=name, out_shape=jax.ShapeDtypeStruct(parts.shape, parts.dtype),
        in_specs=[pl.BlockSpec(memory_space=pl.ANY)], out_specs=pl.BlockSpec(memory_space=pl.ANY),
        scratch_shapes=[pltpu.SemaphoreType.DMA((3,)), pltpu.SemaphoreType.DMA((3,))],
        compiler_params=pltpu.CompilerParams(vmem_limit_bytes=VMEM_LIMIT))(parts)
    chip = 2 * lax.axis_index("x") + lax.axis_index("y")
    own = lax.dynamic_index_in_dim(parts, chip, 0, keepdims=True)
    return lax.dynamic_update_slice_in_dim(out, own, chip, 0)


def _row_tile(rows, cap, mult=8):
    best = None
    for t in range(mult, min(rows, cap) + 1, mult):
        if rows % t == 0:
            best = t
    assert best is not None, rows
    return best


def add_arrays(name, arrs, out_dtype=F32):
    shape = arrs[0].shape
    C = shape[-1]
    flat = [a.reshape(-1, C) for a in arrs]
    R = flat[0].shape[0]
    narrow = out_dtype == BF16 or any(a.dtype == BF16 for a in arrs)
    tr = _row_tile(R, 2048 if len(arrs) <= 2 else 1024, 16 if narrow else 8)
    n = len(flat)

    def body(*refs):
        acc = refs[0][...].astype(F32)
        for r in refs[1:n]:
            acc = acc + r[...].astype(F32)
        refs[n][...] = acc.astype(out_dtype)

    spec = pl.BlockSpec((tr, C), lambda i: (i, 0))
    out = pl.pallas_call(
        body, name=name, out_shape=jax.ShapeDtypeStruct((R, C), out_dtype), grid=(R // tr,),
        in_specs=[spec] * n, out_specs=spec, compiler_params=_cp(("parallel",)))(*flat)
    return out.reshape(shape)


def reduce_scatter_chips(slabs):
    _, R, C = slabs.shape
    half = R // 2
    c = lax.axis_index("c")
    k = 2 * lax.axis_index("x") + lax.axis_index("y")
    halves = slabs.reshape(N_CHIPS, 2, half, C)
    own = lax.dynamic_index_in_dim(halves, c, axis=1, keepdims=False)
    other = lax.dynamic_index_in_dim(halves, 1 - c, axis=1, keepdims=False)
    from_sibling = sibling_swap("rs_sibling_halves", other)
    del k
    return add_arrays("rs_add_sibling", [own, from_sibling], out_dtype=BF16)


def reduce_scatter_finish(landed):
    c = lax.axis_index("c")
    mine = add_arrays("rs_add_chips", [landed[j] for j in range(N_CHIPS)])
    sib = sibling_swap("rs_sibling_result", mine)
    return jnp.concatenate([jnp.where(c == 0, mine, sib), jnp.where(c == 0, sib, mine)], axis=0)


def ada_mod_shard(cond_all, w_ada_shard, b_ada_shard):
    def body(c_ref, w_ref, b_ref, o_ref):
        o_ref[...] = _dot(_silu(c_ref[...]).astype(BF16), w_ref[...].astype(BF16)) + b_ref[...]

    return pl.pallas_call(body, name="ada_mod_shard", out_shape=jax.ShapeDtypeStruct((cond_all.shape[0], w_ada_shard.shape[1]), F32),
                          compiler_params=_cp())(cond_all, w_ada_shard, b_ada_shard)


def ada_bwd_shard(cond_all, dmod_all_shard, dmod_all, w_ada_shard, row_is_cctx):
    def body(c_ref, ds_ref, da_ref, w_ref, sel_ref, gw_ref, gb_ref, part_ref):
        sc = _silu(c_ref[...]).astype(BF16)
        gw_ref[...] = _dot_tn(sc, ds_ref[...].astype(BF16))
        gb_ref[...] = _colsum(da_ref[...])
        dc_tot = jnp.broadcast_to(_colsum(ds_ref[...] * sel_ref[...]), (8, ds_ref.shape[1]))
        part_ref[...] = _dot_nt(dc_tot.astype(BF16), w_ref[...].astype(BF16))

    n = cond_all.shape[0]
    return pl.pallas_call(
        body, name="ada_bwd_shard",
        out_shape=[jax.ShapeDtypeStruct(w_ada_shard.shape, F32), jax.ShapeDtypeStruct((1, dmod_all.shape[1]), F32),
                   jax.ShapeDtypeStruct((8, D), F32)],
        compiler_params=_cp())(cond_all, dmod_all_shard, dmod_all, w_ada_shard, row_is_cctx)


def sum_devices(name, gathered):
    def body(g_ref, o_ref):
        acc = g_ref[0]
        for d in range(1, N_DEV):
            acc = acc + g_ref[d]
        o_ref[...] = acc

    return pl.pallas_call(body, name=name, out_shape=jax.ShapeDtypeStruct(gathered.shape[1:], F32), compiler_params=_cp())(gathered)


def cctx_finish(gathered, c_ctx_row):
    def body(g_ref, c_ref, o_ref):
        acc = g_ref[0, 0:1, :]
        for k in range(1, N_CHIPS):
            acc = acc + g_ref[2 * k, 0:1, :]
        o_ref[...] = acc * _dsilu(c_ref[...])

    return pl.pallas_call(body, name="cctx_finish", out_shape=jax.ShapeDtypeStruct((1, D), F32), compiler_params=_cp())(gathered, c_ctx_row)


def _pack(parts, rows):
    flat = []
    for p in parts:
        p = p.reshape(-1)
        pad = (-p.shape[0]) % LANES
        flat.append(jnp.pad(p, (0, pad)) if pad else p)
    v = jnp.concatenate(flat)
    return jnp.pad(v, (0, rows * LANES - v.shape[0])).reshape(rows, LANES)


def _unpack(v, sizes):
    flat = v.reshape(-1)
    out, off = [], 0
    for n in sizes:
        out.append(flat[off:off + n])
        off += n + (-n) % LANES
    return out


W_SHARD_ROWS = 3456
SEG_ROWS = (0, 2320, 2576, 3088, 3344, 3408)


def kernel(x, c, ctx, c_ctx, w_ada, b_ada, norm_pre, norm_post, w_in, b_merge, pool_w, pool_scale, conv_w, conv_b, dt_bias, a_log, d_skip, ssd_norm, w_proj_pool, w_proj_ssd, w_out, loss_target, m_c_ctx, m_w_ada, m_b_ada, m_norm_pre, m_norm_post, m_w_in, m_b_merge, m_pool_w, m_pool_scale, m_conv_w, m_conv_b, m_dt_bias, m_a_log, m_d_skip, m_ssd_norm, m_w_proj_pool, m_w_proj_ssd, m_w_out, v_c_ctx, v_w_ada, v_b_ada, v_norm_pre, v_norm_post, v_w_in, v_b_merge, v_pool_w, v_pool_scale, v_conv_w, v_conv_b, v_dt_bias, v_a_log, v_d_skip, v_ssd_norm, v_w_proj_pool, v_w_proj_ssd, v_w_out):
    Bn, L, _ = x.shape
    Lc = ctx.shape[1]
    T, Tc = Bn * L, Bn * Lc
    assert Bn == 2
    ix, iy, ic = lax.axis_index("x"), lax.axis_index("y"), lax.axis_index("c")
    me = 4 * ix + 2 * iy + ic
    chip = 2 * ix + iy
    ada_cols = w_ada.shape[2]
    cw_cols = conv_w.shape[2]

    cond_own = jnp.pad(c, ((0, 8 - Bn), (0, 0))) + jnp.pad(c_ctx[None, :], ((Bn, 7 - Bn), (0, 0)))
    convw_own = jnp.pad(conv_w[0], ((0, 4), (0, D - cw_cols)))
    g1 = all_gather_small("gather_cond", jnp.concatenate([cond_own, convw_own], axis=0))
    cond_all = g1[:, 0:8].reshape(8 * N_DEV, D)
    conv_w_full = jnp.concatenate([g1[2 * k, 8:12, 0:cw_cols] for k in range(N_CHIPS)], axis=1)
    b_ada_shard = lax.dynamic_slice(b_ada, (0, chip * ada_cols), (1, ada_cols))
    g2 = all_gather_small("gather_mod", ada_mod_shard(cond_all, w_ada[0], b_ada_shard))
    mod_full = jnp.concatenate([g2[2 * k] for k in range(N_CHIPS)], axis=1)
    own = lax.dynamic_slice(mod_full, (8 * me, 0), (8, 3 * D))
    shift, scale, gate = (own[0:Bn, i * D:(i + 1) * D][:, None, :] for i in range(3))
    shift_c, scale_c = (jnp.broadcast_to(own[Bn:Bn + 1, i * D:(i + 1) * D][None], (Bn, 1, D)) for i in range(2))

    shard = jnp.concatenate([w_in[0].T, w_proj_pool[0], w_proj_ssd[0], w_out[0], pool_w[0].reshape(64, D),
                             jnp.zeros((W_SHARD_ROWS - SEG_ROWS[-1], D), F32)], axis=0).astype(BF16)
    gw = all_gather_chips("gather_weights", shard)
    w_inT = gw[:, SEG_ROWS[0]:SEG_ROWS[1]].reshape(IN_COLS, D)
    w_pp = gw[:, SEG_ROWS[1]:SEG_ROWS[2]].reshape(D, D)
    w_ps = gw[:, SEG_ROWS[2]:SEG_ROWS[3]].reshape(D_INNER, D)
    w_o = gw[:, SEG_ROWS[3]:SEG_ROWS[4]].reshape(D, D)
    pool_full = gw[:, SEG_ROWS[4]:SEG_ROWS[5]].reshape(N_CHIPS, 4, 64, POOL_GROUP).transpose(1, 0, 2, 3).reshape(D, POOL_GROUP)
    w_dt = jnp.pad(w_inT[9216:IN_COLS], ((0, LANES - 64), (0, 0)))
    seg_lo = (0, 256, 512, 768, 1024, 2048, 4096, 6144, 8192, 8704)
    seg_hi = (256, 512, 768, 1024, 2048, 4096, 6144, 8192, 8704, 9216)
    w_seg = [w_inT[lo:hi] for lo, hi in zip(seg_lo, seg_hi)] + [w_dt]

    hx = prenorm_fwd("prenorm_x", x, scale, shift, norm_pre)
    hc = prenorm_fwd("prenorm_ctx", ctx, scale_c, shift_c, norm_pre)
    hx2, hc2 = hx.reshape(T, D), hc.reshape(Tc, D)
    v = mm_nt("proj_v", hx2, w_inT[0:1024], F32).reshape(Bn, L, D)
    zp = mm_nt("proj_zpool", hx2, w_inT[1024:2048], F32).reshape(Bn, L, D)
    zs = mm_nt("proj_zssd", hx2, w_inT[2048:4096], F32).reshape(Bn, L, D_INNER)
    gp = mm_nt("proj_gate", hx2, w_inT[4096:6144], F32).reshape(Bn, L, 2 * D)
    xbc_raw = mm_nt("proj_xbc", hx2, w_inT[6144:9216], F32).reshape(Bn, L, CONV_DIM)
    dt_raw = mm_nt("proj_dt", hx2, w_dt, F32)
    xbc_raw_c = mm_nt("proj_xbc_ctx", hc2, w_inT[6144:9216], F32).reshape(Bn, Lc, CONV_DIM)
    dt_raw_c = mm_nt("proj_dt_ctx", hc2, w_dt, F32)
    dtT = dt_raw[:, :64].reshape(Bn, L, 64).transpose(0, 2, 1)
    dtT_c = dt_raw_c[:, :64].reshape(Bn, Lc, 64).transpose(0, 2, 1)
    bias3 = dt_bias.reshape(2 * N_BC, HPG, 1)
    alog3 = a_log.reshape(2 * N_BC, HPG, 1)

    xbc = conv_fwd_stream("conv_x", xbc_raw, conv_w_full, conv_b)
    xbc_c = conv_fwd_stream("conv_ctx", xbc_raw_c, conv_w_full, conv_b)
    zero_state = jnp.zeros((Bn, N_BC, D_STATE, GW), F32)
    tables = ssd_tables()
    ys, hs_x, hs_c, hf_x, hf_c = [], [], [], [], []
    for d in range(2):
        hsc, hfc = ssd_fwd3(f"ssd_fwd_ctx{d}", dtT_c, bias3, alog3, xbc_c, zero_state, d, False)
        y, hsx, hfx = ssd_fwd3(f"ssd_fwd_x{d}", dtT, bias3, alog3, xbc, hfc, d, True)
        ys.append(y)
        hs_x.append(hsx)
        hs_c.append(hsc)
        hf_x.append(hfx)
        hf_c.append(hfc)

    dgs = [pool_diff(f"pool_diff{g}", v, g * POOL_GROUP, g, False) for g in range(4)]
    y_pool = pool_mix_fwd("pool_mix", dgs, zp, pool_full, pool_scale)
    dskip_lanes = jnp.repeat(d_skip[0], HEAD_DIM)[None, :]
    y_ssd = gated_norm_fwd("gated_norm", ys[0], ys[1], (xbc, D_INNER), zs, dskip_lanes, ssd_norm)
    merged, p1, p2, dout, g_res, dgate, g_norm_post, loss_part = merge_fwd(
        "merge_fwd", y_pool, y_ssd, gp, x, loss_target, gate, b_merge, norm_post, w_pp, w_ps, w_o)

    dp1, dp2, dgp, dyp, dys, g_b_merge = merge_bwd("merge_bwd", dout, gp, p1, p2, b_merge, w_pp, w_ps, w_o)
    gw_o = mm_tn("gw_out", merged.reshape(T, D), dout.reshape(T, D))
    gw_pp = mm_tn("gw_proj_pool", y_pool.reshape(T, D), dp1.reshape(T, D))
    gw_ps = mm_tn("gw_proj_ssd", y_ssd.reshape(T, D_INNER), dp2.reshape(T, D))

    *dds, dzp, g_pool, g_pool_scale = pool_mix_bwd("pool_mix_bwd", dgs, zp, dyp, pool_full, pool_scale)
    dvs = [pool_diff(f"pool_diff_t{g}", dds[g], 0, g, True) for g in range(4)]

    head_sel = (jnp.arange(D_INNER)[:, None] // HEAD_DIM == jnp.arange(LANES)[None, :]).astype(BF16)
    dy, dzs, g_ssd_norm, g_dskip = gated_norm_bwd(
        "gated_norm_bwd", ys[0], ys[1], (xbc, D_INNER), zs, dys, dskip_lanes, ssd_norm, head_sel)

    dxs, dbm, dcm, ddt, dxs_c, dbm_c, ddt_c = [], [], [], [], [], [], []
    g_bias = jnp.zeros((2, N_BC, HPG, 1), F32)
    g_alog = jnp.zeros((2, N_BC, HPG, 1), F32)
    for d in range(2):
        a, b_, c_, t_, gb, ga, dh0 = ssd_bwd3(f"ssd_bwd_x{d}", dtT, bias3, alog3, xbc, hs_x[d], dy, zero_state, d)
        dxs.append(a), dbm.append(b_), dcm.append(c_), ddt.append(t_)
        ac, bc, _, tc, gbc, gac, _ = ssd_bwd3(f"ssd_bwd_ctx{d}", dtT_c, bias3, alog3, xbc_c, hs_c[d], None, dh0, d)
        dxs_c.append(ac), dbm_c.append(bc), ddt_c.append(tc)
        g_bias = g_bias.at[d].set(jnp.sum(gb, axis=0) + jnp.sum(gbc, axis=0))
        g_alog = g_alog.at[d].set(jnp.sum(ga, axis=0) + jnp.sum(gac, axis=0))

    dxr_xs, gcw_xs, gcb_xs = conv_bwd_stream("conv_bwd_xs", xbc_raw, dxs, conv_w_full, conv_b, 0, D_INNER, scaled=(dy, dskip_lanes))
    dxr_b, gcw_b, gcb_b = conv_bwd_stream("conv_bwd_b", xbc_raw, dbm, conv_w_full, conv_b, D_INNER, N_BC * D_STATE)
    dxr_c, gcw_c, gcb_c = conv_bwd_stream("conv_bwd_c", xbc_raw, dcm, conv_w_full, conv_b, D_INNER + N_BC * D_STATE, N_BC * D_STATE)
    dxr_xs_c, gcw_xs_c, gcb_xs_c = conv_bwd_stream("conv_bwd_xs_ctx", xbc_raw_c, dxs_c, conv_w_full, conv_b, 0, D_INNER)
    dxr_b_c, gcw_b_c, gcb_b_c = conv_bwd_stream("conv_bwd_b_ctx", xbc_raw_c, dbm_c, conv_w_full, conv_b, D_INNER, N_BC * D_STATE)
    g_conv_w = jnp.concatenate([gcw_xs + gcw_xs_c, gcw_b + gcw_b_c, gcw_c], axis=1)
    g_conv_b = jnp.concatenate([gcb_xs + gcb_xs_c, gcb_b + gcb_b_c, gcb_c], axis=1)

    def dt_cols(parts, n_tok):
        t = jnp.concatenate(parts, axis=1).transpose(0, 2, 1).reshape(n_tok, 2 * N_HEADS)
        return jnp.pad(t, ((0, 0), (0, LANES - 2 * N_HEADS))).astype(BF16)

    ddt2, ddt2_c = dt_cols(ddt, T), dt_cols(ddt_c, Tc)
    segs = ([dv.reshape(T, POOL_GROUP) for dv in dvs]
            + [dzp.reshape(T, D), dzs.reshape(T, D_INNER), dgp.reshape(T, 2 * D), dxr_xs.reshape(T, D_INNER),
               dxr_b.reshape(T, N_BC * D_STATE), dxr_c.reshape(T, N_BC * D_STATE), ddt2])
    segs_c = {7: dxr_xs_c.reshape(Tc, D_INNER), 8: dxr_b_c.reshape(Tc, N_BC * D_STATE), 10: ddt2_c}
    gw_rows = []
    for i, seg in enumerate(segs):
        init = mm_tn(f"gw_in_ctx{i}", segs_c[i], hc2) if i in segs_c else None
        gw_rows.append(mm_tn(f"gw_in{i}", seg, hx2, init=init))
    gw_rows[-1] = gw_rows[-1][0:2 * N_HEADS]
    gw_inT = jnp.concatenate(gw_rows, axis=0)

    pool_slab = g_pool.reshape(4, N_CHIPS, 64, POOL_GROUP).transpose(1, 0, 2, 3).reshape(N_CHIPS, 64, D)
    slabs = jnp.concatenate([gw_inT.reshape(N_CHIPS, 2320, D), gw_pp.reshape(N_CHIPS, 256, D), gw_ps.reshape(N_CHIPS, 512, D),
                             gw_o.reshape(N_CHIPS, 256, D), pool_slab, jnp.zeros((N_CHIPS, W_SHARD_ROWS - SEG_ROWS[-1], D), F32)], axis=1)
    chip_part = reduce_scatter_chips(slabs)
    d_hx, landed = mm_nn_multi("d_hx", list(zip(segs, w_seg)), F32, tm=1024, tk=256, exchange=chip_part)
    d_hx = d_hx.reshape(Bn, L, D)
    gsh = reduce_scatter_finish(landed)
    d_hc = mm_nn_multi("d_hc", [(segs_c[i], w_seg[i]) for i in (7, 8, 10)], F32).reshape(Bn, Lc, D)

    grad_x, dscale, dshift, g_npre_x = prenorm_bwd("prenorm_bwd_x", x, d_hx, scale, norm_pre, g_res=g_res)
    _, dscale_c, dshift_c, g_npre_c = prenorm_bwd("prenorm_bwd_ctx", ctx, d_hc, scale_c, norm_pre)

    dmod_x = jnp.concatenate([dshift[:, 0], dscale[:, 0], dgate[:, 0]], axis=1)
    dmod_c = jnp.concatenate([jnp.sum(dshift_c[:, 0], axis=0, keepdims=True), jnp.sum(dscale_c[:, 0], axis=0, keepdims=True),
                              jnp.zeros((1, D), F32)], axis=1)
    dmod_own = jnp.pad(dmod_x, ((0, 8 - Bn), (0, 0))) + jnp.pad(dmod_c, ((Bn, 7 - Bn), (0, 0)))
    dmod_all = all_gather_small("gather_dmod", dmod_own).reshape(8 * N_DEV, 3 * D)
    row_is_cctx = (jnp.arange(8 * N_DEV) % 8 == Bn).astype(F32)[:, None]
    g_w_ada, g_b_ada, cpart = ada_bwd_shard(
        cond_all, lax.dynamic_slice(dmod_all, (0, chip * ada_cols), (8 * N_DEV, ada_cols)), dmod_all, w_ada[0], row_is_cctx)
    g_c_ctx = cctx_finish(all_gather_small("gather_cctx", cpart), c_ctx[None, :])

    small_sizes = (D, D, 2 * D, D, CONV_DIM, 2 * N_HEADS, 2 * N_HEADS, N_HEADS, D_INNER, 4 * CONV_DIM, 1)
    pk = _pack([g_npre_x + g_npre_c, g_norm_post, g_b_merge, g_pool_scale, g_conv_b, g_bias, g_alog, g_dskip[0, 0:N_HEADS],
                g_ssd_norm, g_conv_w, loss_part[0, 0:1]], 184)
    small = sum_devices("sum_small", all_gather_small("gather_small", pk))
    (g_norm_pre, g_norm_post_t, g_b_merge_t, g_pool_scale_t, g_conv_b_t, g_dt_bias, g_a_log, g_d_skip, g_ssd_norm_t,
     g_conv_w_t, loss) = _unpack(small, small_sizes)
    g_conv_w_shard = lax.dynamic_slice(g_conv_w_t.reshape(4, CONV_DIM), (0, chip * cw_cols), (4, cw_cols))

    g_w_in = gsh[SEG_ROWS[0]:SEG_ROWS[1]].T
    g_w_pp, g_w_ps, g_w_o = (gsh[SEG_ROWS[i]:SEG_ROWS[i + 1]] for i in (1, 2, 3))
    g_pool_w = gsh[SEG_ROWS[4]:SEG_ROWS[5]].reshape(256, POOL_GROUP)

    grads = {
        "c_ctx": g_c_ctx.reshape(c_ctx.shape), "w_ada": g_w_ada[None], "b_ada": g_b_ada, "norm_pre": g_norm_pre[None],
        "norm_post": g_norm_post_t[None], "w_in": g_w_in[None], "b_merge": g_b_merge_t[None],
        "pool_w": g_pool_w.reshape(pool_w.shape), "pool_scale": g_pool_scale_t[None], "conv_w": g_conv_w_shard[None],
        "conv_b": g_conv_b_t[None], "dt_bias": g_dt_bias.reshape(dt_bias.shape), "a_log": g_a_log.reshape(a_log.shape),
        "d_skip": g_d_skip[None], "ssd_norm": g_ssd_norm_t[None], "w_proj_pool": g_w_pp[None], "w_proj_ssd": g_w_ps[None],
        "w_out": g_w_o[None]}
    weights = dict(c_ctx=c_ctx, w_ada=w_ada, b_ada=b_ada, norm_pre=norm_pre, norm_post=norm_post, w_in=w_in, b_merge=b_merge,
                   pool_w=pool_w, pool_scale=pool_scale, conv_w=conv_w, conv_b=conv_b, dt_bias=dt_bias, a_log=a_log,
                   d_skip=d_skip, ssd_norm=ssd_norm, w_proj_pool=w_proj_pool, w_proj_ssd=w_proj_ssd, w_out=w_out)
    m_in = dict(c_ctx=m_c_ctx, w_ada=m_w_ada, b_ada=m_b_ada, norm_pre=m_norm_pre, norm_post=m_norm_post, w_in=m_w_in,
                b_merge=m_b_merge, pool_w=m_pool_w, pool_scale=m_pool_scale, conv_w=m_conv_w, conv_b=m_conv_b,
                dt_bias=m_dt_bias, a_log=m_a_log, d_skip=m_d_skip, ssd_norm=m_ssd_norm, w_proj_pool=m_w_proj_pool,
                w_proj_ssd=m_w_proj_ssd, w_out=m_w_out)
    v_in = dict(c_ctx=v_c_ctx, w_ada=v_w_ada, b_ada=v_b_ada, norm_pre=v_norm_pre, norm_post=v_norm_post, w_in=v_w_in,
                b_merge=v_b_merge, pool_w=v_pool_w, pool_scale=v_pool_scale, conv_w=v_conv_w, conv_b=v_conv_b,
                dt_bias=v_dt_bias, a_log=v_a_log, d_skip=v_d_skip, ssd_norm=v_ssd_norm, w_proj_pool=v_w_proj_pool,
                w_proj_ssd=v_w_proj_ssd, w_out=v_w_out)
    names = list(weights)
    big = ("w_ada", "w_in", "pool_w", "w_proj_pool", "w_proj_ssd", "w_out")
    small_names = [n for n in names if n not in big]
    delta, new_m, new_v = {}, {}, {}
    for n in big:
        shape2 = (-1, weights[n].shape[-1])
        d_, m_, v_ = adamw(f"adamw_{n}", weights[n].reshape(shape2), grads[n].reshape(shape2), m_in[n].reshape(shape2),
                           v_in[n].reshape(shape2), tr=128)
        delta[n], new_m[n], new_v[n] = (t.reshape(weights[n].shape) for t in (d_, m_, v_))
    sizes = [weights[n].size for n in small_names]
    packed = [_pack([src[n] for n in small_names], 144) for src in (weights, grads, m_in, v_in)]
    outs = adamw("adamw_small", *packed, tr=144)
    for res, store in zip(outs, (delta, new_m, new_v)):
        for n, piece in zip(small_names, _unpack(res, sizes)):
            store[n] = piece.reshape(weights[n].shape)

    return (loss.reshape(()), grad_x, *[grads[n] for n in names], *[delta[n] for n in names],
            *[new_m[n] for n in names], *[new_v[n] for n in names])
```

```python
import jax
import jax.numpy as jnp
from jax import lax
from jax.experimental import pallas as pl
from jax.experimental.pallas import tpu as pltpu

F32 = jnp.float32
BF16 = jnp.bfloat16
MESH = pl.DeviceIdType.MESH

D = 1024
GRID_W = 64
NORM_EPS = 1e-6
POOL_WINDOWS = (2, 4, 8, 16)
POOL_GROUP = 256
D_INNER = 2048
HEAD_DIM = 64
N_HEADS = 32
D_STATE = 128
N_BC = 4
HPG = N_HEADS // N_BC
GW = HPG * HEAD_DIM
CONV_DIM = 3072
CHUNK = 128
OFF_XBC = 6144
IN_COLS = 9280
N_CHIPS = 4
N_DEV = 8

ADAM_LR = 0.001
ADAM_B1 = 0.9
ADAM_B2 = 0.999
ADAM_EPS = 1e-08
ADAM_WD = 0.01
ADAM_STEP = 10

V7X_VMEM_BYTES = 64 * 1024 * 1024
VMEM_LIMIT = V7X_VMEM_BYTES * 3 // 4
LANES = 128


def _cp(sem=None):
    return pltpu.CompilerParams(dimension_semantics=sem, vmem_limit_bytes=VMEM_LIMIT)


def _dot(a, b):
    return jnp.dot(a, b, preferred_element_type=F32)


def _dot_nt(a, b):
    return lax.dot_general(a, b, (((1,), (1,)), ((), ())), preferred_element_type=F32)


def _dot_tn(a, b):
    return lax.dot_general(a, b, (((0,), (0,)), ((), ())), preferred_element_type=F32)


def _split3(x):
    hi = x.astype(BF16)
    r1 = x - hi.astype(F32)
    mid = r1.astype(BF16)
    lo = (r1 - mid.astype(F32)).astype(BF16)
    return hi, mid, lo


def _sigmoid(x):
    return jax.nn.sigmoid(x)


def _silu(x):
    return x * _sigmoid(x)


def _dsilu(x):
    s = _sigmoid(x)
    return s * (1.0 + x * (1.0 - s))


def _softplus(x):
    return jnp.maximum(x, 0.0) + jnp.log(1.0 + jnp.exp(-jnp.abs(x)))


def mm_nt(name, a, b, out_dtype, tm=1024, tn=512, gather=None):
    M, K = a.shape
    N = b.shape[0]
    tm, tn = min(tm, M), min(tn, N)
    assert M % tm == 0 and N % tn == 0, (M, N, tm, tn)
    n_i, n_j = M // tm, N // tn
    has_g = gather is not None
    if has_g:
        half = gather.shape[0] // 2
        assert gather.shape[0] % 32 == 0 and n_i * n_j >= 4

    def body(*refs):
        a_ref, b_ref = refs[0], refs[1]
        if has_g:
            s_ref, o_ref, g_ref, send_sems, recv_sems = refs[2:]
            x, y, c = _me()
            chips = [(1 - x, y), (x, 1 - y), (1 - x, 1 - y)]
            step = pl.program_id(0) * n_j + pl.program_id(1)

            def rows(chip, hc):
                return g_ref.at[2 * chip[0] + chip[1], pl.ds(hc * half, half), :]

            def first(j, chip):
                return pltpu.make_async_remote_copy(
                    src_ref=s_ref.at[pl.ds(c * half, half), :], dst_ref=rows((x, y), c), send_sem=send_sems.at[j],
                    recv_sem=recv_sems.at[j], device_id=(*chip, c), device_id_type=MESH)

            def landed(j, chip, hc):
                return pltpu.make_async_remote_copy(
                    src_ref=rows(chip, hc), dst_ref=rows(chip, hc), send_sem=send_sems.at[j], recv_sem=recv_sems.at[j],
                    device_id=(x, y, 1 - c), device_id_type=MESH)

            @pl.when(step == 0)
            def _():
                for j, chip in enumerate(chips):
                    first(j, chip).start()

            @pl.when(step == (3 * n_i * n_j) // 4)
            def _():
                for j, chip in enumerate(chips):
                    landed(j, chip, c).wait_recv()
                    landed(3 + j, chip, c).start()
        else:
            o_ref = refs[2]

        o_ref[...] = _dot_nt(a_ref[...], b_ref[...]).astype(o_ref.dtype)

        if has_g:
            @pl.when(step == n_i * n_j - 1)
            def _():
                for j, chip in enumerate(chips):
                    landed(3 + j, chip, 1 - c).wait_recv()
                for j, chip in enumerate(chips):
                    first(j, chip).wait_send()
                    landed(3 + j, chip, c).wait_send()

    in_specs = [pl.BlockSpec((tm, K), lambda i, j: (i, 0)), pl.BlockSpec((tn, K), lambda i, j: (j, 0))]
    out_shape = jax.ShapeDtypeStruct((M, N), out_dtype)
    out_specs = pl.BlockSpec((tm, tn), lambda i, j: (i, j))
    if not has_g:
        return pl.pallas_call(body, name=name, out_shape=out_shape, grid=(n_i, n_j), in_specs=in_specs, out_specs=out_specs,
                              compiler_params=_cp(("parallel", "arbitrary")))(a, b)
    out, g = pl.pallas_call(
        body, name=name, out_shape=[out_shape, jax.ShapeDtypeStruct((N_CHIPS, *gather.shape), gather.dtype)], grid=(n_i, n_j),
        in_specs=in_specs + [pl.BlockSpec(memory_space=pl.ANY)], out_specs=[out_specs, pl.BlockSpec(memory_space=pl.ANY)],
        scratch_shapes=[pltpu.SemaphoreType.DMA((6,)), pltpu.SemaphoreType.DMA((6,))],
        compiler_params=_cp(("arbitrary", "arbitrary")))(a, b, gather)
    chip = 2 * lax.axis_index("x") + lax.axis_index("y")
    return out, lax.dynamic_update_index_in_dim(g, gather, chip, 0)


def mm_tn(name, a, b, init=None, tm=1024, tn=1024, tk=512):
    T, M = a.shape
    N = b.shape[1]
    tm, tn, tk = min(tm, M), min(tn, N), min(tk, T)
    assert M % tm == 0 and N % tn == 0 and T % tk == 0, (M, N, T)
    has_init = init is not None

    def body(*refs):
        if has_init:
            a_ref, b_ref, i_ref, o_ref = refs
        else:
            a_ref, b_ref, o_ref = refs
        k = pl.program_id(2)

        @pl.when(k == 0)
        def _():
            o_ref[...] = i_ref[...] if has_init else jnp.zeros(o_ref.shape, F32)

        o_ref[...] += _dot_tn(a_ref[...], b_ref[...])

    in_specs = [pl.BlockSpec((tk, tm), lambda i, j, k: (k, i)), pl.BlockSpec((tk, tn), lambda i, j, k: (k, j))]
    args = [a, b]
    if has_init:
        in_specs.append(pl.BlockSpec((tm, tn), lambda i, j, k: (i, j)))
        args.append(init)
    return pl.pallas_call(
        body, name=name, out_shape=jax.ShapeDtypeStruct((M, N), F32), grid=(M // tm, N // tn, T // tk),
        in_specs=in_specs, out_specs=pl.BlockSpec((tm, tn), lambda i, j, k: (i, j)),
        compiler_params=_cp(("parallel", "parallel", "arbitrary")))(*args)


def mm_nn_multi(name, pairs, out_dtype, tm=512, tk=512, exchange=None):
    M = pairs[0][0].shape[0]
    N = pairs[0][1].shape[1]
    tm = min(tm, M)
    assert M % tm == 0
    plan = []
    step = 0
    for a, b in pairs:
        K = a.shape[1]
        t = min(tk, K)
        assert K % t == 0 and b.shape == (K, N)
        plan.append((t, step, K // t))
        step += K // t
    nsteps = step
    npairs = len(pairs)

    n_i = M // tm
    has_x = exchange is not None

    def body(*refs):
        if has_x:
            p_ref, o_ref, land_ref, acc, send_sems, recv_sems = refs[2 * npairs:]
        else:
            o_ref, acc = refs[2 * npairs:]
        i, k = pl.program_id(0), pl.program_id(1)

        if has_x:
            x, y, c = _me()
            me_chip = 2 * x + y
            chips = [(1 - x, y), (x, 1 - y), (1 - x, 1 - y)]

            def copy(j, src_chip, dst_chip, to):
                return pltpu.make_async_remote_copy(
                    src_ref=p_ref.at[src_chip], dst_ref=land_ref.at[dst_chip], send_sem=send_sems.at[j], recv_sem=recv_sems.at[j],
                    device_id=(*to, c), device_id_type=MESH)

            @pl.when((i == 0) & (k == 0))
            def _():
                for j, chip in enumerate(chips):
                    copy(j, 2 * chip[0] + chip[1], me_chip, chip).start()

        @pl.when(k == 0)
        def _():
            acc[...] = jnp.zeros(acc.shape, F32)

        for p, (_, first, n) in enumerate(plan):
            @pl.when((k >= first) & (k < first + n))
            def _(p=p):
                acc[...] += _dot(refs[2 * p][...], refs[2 * p + 1][...])

        @pl.when(k == nsteps - 1)
        def _():
            o_ref[...] = acc[...].astype(o_ref.dtype)

        if has_x:
            @pl.when((i == n_i - 1) & (k == nsteps - 1))
            def _():
                for j, chip in enumerate(chips):
                    copy(j, me_chip, 2 * chip[0] + chip[1], chip).wait_recv()
                for j, chip in enumerate(chips):
                    copy(j, 2 * chip[0] + chip[1], me_chip, chip).wait_send()

    in_specs, args = [], []
    for (a, b), (t, first, n) in zip(pairs, plan):
        in_specs.append(pl.BlockSpec((tm, t), lambda i, k, first=first, n=n: (i, jnp.clip(k - first, 0, n - 1))))
        in_specs.append(pl.BlockSpec((t, N), lambda i, k, first=first, n=n: (jnp.clip(k - first, 0, n - 1), 0)))
        args += [a, b]
    out_shape = jax.ShapeDtypeStruct((M, N), out_dtype)
    out_specs = pl.BlockSpec((tm, N), lambda i, k: (i, 0))
    scratch = [pltpu.VMEM((tm, N), F32)]
    if has_x:
        in_specs.append(pl.BlockSpec(memory_space=pl.ANY))
        args.append(exchange)
        out_shape = [out_shape, jax.ShapeDtypeStruct(exchange.shape, exchange.dtype)]
        out_specs = [out_specs, pl.BlockSpec(memory_space=pl.ANY)]
        scratch += [pltpu.SemaphoreType.DMA((3,)), pltpu.SemaphoreType.DMA((3,))]
    res = pl.pallas_call(
        body, name=name, out_shape=out_shape, grid=(n_i, nsteps), in_specs=in_specs, out_specs=out_specs,
        scratch_shapes=scratch, compiler_params=_cp(("arbitrary", "arbitrary")))(*args)
    if not has_x:
        return res
    out, landed = res
    chip = 2 * lax.axis_index("x") + lax.axis_index("y")
    own = lax.dynamic_index_in_dim(exchange, chip, 0, keepdims=True)
    return out, lax.dynamic_update_slice_in_dim(landed, own, chip, 0)


def tok_call(name, body, tiled, perb, glob, out_tiled, out_perb, out_glob, tm=256):
    widths = [t[1] if isinstance(t, tuple) else t.shape[2] for t in tiled]
    tiled = [t[0] if isinstance(t, tuple) else t for t in tiled]
    Bn, L = tiled[0].shape[:2]
    tm = min(tm, L)
    assert L % tm == 0
    n_t, n_p, n_g = len(tiled), len(perb), len(glob)
    o_t, o_p, o_g = len(out_tiled), len(out_perb), len(out_glob)
    n_in = n_t + n_p + n_g

    def kern(*refs):
        ins, outs = refs[:n_in], refs[n_in:]
        b, j = pl.program_id(0), pl.program_id(1)
        vals = [r[0] for r in ins[:n_t + n_p]] + [r[...] for r in ins[n_t + n_p:]]
        res = body(*vals)
        if not isinstance(res, (tuple, list)):
            res = (res,)
        assert len(res) == o_t + o_p + o_g, (name, len(res))
        for r, v in zip(outs[:o_t], res[:o_t]):
            r[0] = v.astype(r.dtype)

        def accum(r, v, first, lead):
            @pl.when(first)
            def _():
                r[...] = jnp.zeros(r.shape, F32)
            if lead:
                r[0] += v
            else:
                r[...] += v

        for r, v in zip(outs[o_t:o_t + o_p], res[o_t:o_t + o_p]):
            accum(r, v, j == 0, True)
        for r, v in zip(outs[o_t + o_p:], res[o_t + o_p:]):
            accum(r, v, (j == 0) & (b == 0), False)

    in_specs = ([pl.BlockSpec((1, tm, w), lambda b, j: (b, j, 0)) for w in widths]
                + [pl.BlockSpec((1, 1, a.shape[2]), lambda b, j: (b, 0, 0)) for a in perb]
                + [pl.BlockSpec(a.shape, lambda b, j: (0, 0), pipeline_mode=pl.Buffered(1)) for a in glob])
    out_shape = ([jax.ShapeDtypeStruct((Bn, L, w), dt) for w, dt in out_tiled]
                 + [jax.ShapeDtypeStruct((Bn, 1, w), F32) for w in out_perb]
                 + [jax.ShapeDtypeStruct(s, F32) for s in out_glob])
    out_specs = ([pl.BlockSpec((1, tm, w), lambda b, j: (b, j, 0)) for w, _ in out_tiled]
                 + [pl.BlockSpec((1, 1, w), lambda b, j: (b, 0, 0)) for w in out_perb]
                 + [pl.BlockSpec(s, lambda b, j: (0, 0)) for s in out_glob])
    return pl.pallas_call(
        kern, name=name, out_shape=out_shape, grid=(Bn, L // tm), in_specs=in_specs, out_specs=out_specs,
        compiler_params=_cp(("arbitrary", "arbitrary")))(*tiled, *perb, *glob)


def slab_call(name, body, slabs, colparams, out_slabs, out_colred, wc=LANES):
    Bn, L = slabs[0][0].shape[:2]
    w_out = out_slabs[0][0]
    assert w_out % wc == 0 and all(off % wc == 0 for _, off in slabs + colparams)
    n_col = w_out // wc
    n_s, n_c = len(slabs), len(colparams)
    o_s = len(out_slabs)

    def kern(*refs):
        ins, outs = refs[:n_s + n_c], refs[n_s + n_c:]
        b = pl.program_id(1)
        vals = [r[0] for r in ins[:n_s]] + [r[...] for r in ins[n_s:]]
        res = body(*vals)
        if not isinstance(res, (tuple, list)):
            res = (res,)
        assert len(res) == o_s + len(out_colred), name
        for r, v in zip(outs[:o_s], res[:o_s]):
            r[0] = v.astype(r.dtype)

        def accum(r, v):
            @pl.when(b == 0)
            def _():
                r[...] = jnp.zeros(r.shape, F32)
            r[...] += v

        for r, v in zip(outs[o_s:], res[o_s:]):
            accum(r, v)

    in_specs = ([pl.BlockSpec((1, L, wc), lambda j, b, o=off // wc: (b, 0, o + j)) for _, off in slabs]
                + [pl.BlockSpec((a.shape[0], wc), lambda j, b, o=off // wc: (0, o + j)) for a, off in colparams])
    out_shape = ([jax.ShapeDtypeStruct((Bn, L, w), dt) for w, dt in out_slabs]
                 + [jax.ShapeDtypeStruct((r, w_out), F32) for r in out_colred])
    out_specs = ([pl.BlockSpec((1, L, wc), lambda j, b: (b, 0, j)) for _ in out_slabs]
                 + [pl.BlockSpec((r, wc), lambda j, b: (0, j)) for r in out_colred])
    return pl.pallas_call(
        kern, name=name, out_shape=out_shape, grid=(n_col, Bn), in_specs=in_specs, out_specs=out_specs,
        compiler_params=_cp(("arbitrary", "arbitrary")))(*[a for a, _ in slabs], *[a for a, _ in colparams])


def _rms_r(x):
    return lax.rsqrt(jnp.mean(x * x, axis=-1, keepdims=True) + NORM_EPS)


def _rms_bwd(dxh, x, r):
    return r * (dxh - x * (r * r) * jnp.mean(dxh * x, axis=-1, keepdims=True))


def _colsum(v):
    return jnp.sum(v, axis=0, keepdims=True)


def _stack_rows(rows):
    n, w = len(rows), rows[0].shape[1]
    sub = lax.broadcasted_iota(jnp.int32, (n, w), 0)
    acc = jnp.zeros((n, w), F32)
    for r, row in enumerate(rows):
        acc = acc + jnp.where(sub == r, jnp.broadcast_to(row, (n, w)), 0.0)
    return acc


def prenorm_fwd(name, x, scale, shift, w_pre):
    def body(x, scale, shift, w):
        n = x * _rms_r(x) * w
        return n * (1.0 + scale) + shift

    return tok_call(name, body, [x], [scale, shift], [w_pre], [(D, BF16)], [], [])[0]


def prenorm_bwd(name, x, dhx, scale, w_pre, g_res=None):
    has_res = g_res is not None

    def body(*v):
        if has_res:
            x, dhx, g, scale, w = v
        else:
            x, dhx, scale, w = v
        r = _rms_r(x)
        xr = x * r
        n = xr * w
        dn = dhx * (1.0 + scale)
        dx = _rms_bwd(dn * w, x, r)
        if has_res:
            dx = dx + g
        return dx, _colsum(dhx * n), _colsum(dhx), _colsum(dn * xr)

    tiled = [x, dhx] + ([g_res] if has_res else [])
    return tok_call(name, body, tiled, [scale], [w_pre], [(D, F32)], [D, D], [(1, D)])


def _shift_rows(x, o, tok, L):
    if o == 0:
        return x
    rolled = pltpu.roll(x, (-o) % L, 0)
    return jnp.where((tok + o >= 0) & (tok + o < L), rolled, 0.0)


def conv_fwd(name, xbc_raw, conv_w, conv_b):
    L = xbc_raw.shape[1]

    def body(x, w, b):
        tok = lax.broadcasted_iota(jnp.int32, x.shape, 0)
        pre = b
        for k in range(4):
            pre = pre + _shift_rows(x, k - 2, tok, L) * w[k:k + 1]
        return _silu(pre)

    return slab_call(name, body, [(xbc_raw, 0)], [(conv_w, 0), (conv_b, 0)], [(CONV_DIM, F32)], [])[0]


def conv_bwd(name, xbc_raw, dparts, conv_w, conv_b, col0, width, scaled=None):
    L = xbc_raw.shape[1]
    n_d = len(dparts) + (1 if scaled is not None else 0)

    def body(*v):
        x, ds, w, b = v[0], v[1:1 + n_d], v[1 + n_d], v[2 + n_d]
        tok = lax.broadcasted_iota(jnp.int32, x.shape, 0)
        taps = [_shift_rows(x, k - 2, tok, L) for k in range(4)]
        pre = b
        for k in range(4):
            pre = pre + taps[k] * w[k:k + 1]
        dy = ds[0] * v[3 + n_d] if scaled is not None else ds[0]
        for extra in ds[1:]:
            dy = dy + extra
        dpre = dy * _dsilu(pre)
        dx = jnp.zeros_like(x)
        for k in range(4):
            dx = dx + _shift_rows(dpre, 2 - k, tok, L) * w[k:k + 1]
        dw = _stack_rows([_colsum(dpre * taps[k]) for k in range(4)])
        return dx, dw, _colsum(dpre)

    slabs = [(xbc_raw, col0)] + ([(scaled[0], 0)] if scaled is not None else []) + [(d, 0) for d in dparts]
    colparams = [(conv_w, col0), (conv_b, col0)] + ([(scaled[1], 0)] if scaled is not None else [])
    return slab_call(name, body, slabs, colparams, [(width, BF16)], [4, 1])


CONV_ROWS = 128
CONV_HALO = 8


def _halo_chunks(L, load, work):
    ch, hl = CONV_ROWS, CONV_HALO
    n = L // ch
    assert L % ch == 0
    if n == 1:
        z = jnp.zeros_like(load(0, hl))
        work(0, jnp.concatenate([z, load(0, ch), z], axis=0))
        return
    z = jnp.zeros_like(load(0, hl))
    work(0, jnp.concatenate([z, load(0, ch + hl)], axis=0))

    def step(i, carry):
        start = pl.multiple_of(i * ch, ch)
        work(start, load(pl.multiple_of(start - hl, hl), ch + 2 * hl))
        return carry

    lax.fori_loop(1, n - 1, step, 0)
    work(L - ch, jnp.concatenate([load(L - ch - hl, ch + hl), z], axis=0))


def _rows_at(xh, o):
    return xh if o == 0 else pltpu.roll(xh, (-o) % xh.shape[0], 0)


def conv_fwd_stream(name, xbc_raw, conv_w, conv_b, wc=LANES):
    Bn, L, W = xbc_raw.shape
    mid = slice(CONV_HALO, CONV_HALO + CONV_ROWS)

    def kern(x_ref, w_ref, b_ref, o_ref):
        w, b = w_ref[...], b_ref[...]

        def work(start, xh):
            pre = b
            for k in range(4):
                pre = pre + _rows_at(xh, k - 2) * w[k:k + 1]
            o_ref[0, pl.ds(start, CONV_ROWS), :] = _silu(pre)[mid]

        _halo_chunks(L, lambda s, n: x_ref[0, pl.ds(s, n), :], work)

    return pl.pallas_call(
        kern, name=name, out_shape=jax.ShapeDtypeStruct((Bn, L, W), F32), grid=(W // wc, Bn),
        in_specs=[pl.BlockSpec((1, L, wc), lambda j, b: (b, 0, j)), pl.BlockSpec((4, wc), lambda j, b: (0, j)),
                  pl.BlockSpec((1, wc), lambda j, b: (0, j))],
        out_specs=pl.BlockSpec((1, L, wc), lambda j, b: (b, 0, j)),
        compiler_params=_cp(("arbitrary", "arbitrary")))(xbc_raw, conv_w, conv_b)


def conv_bwd_stream(name, xbc_raw, dparts, conv_w, conv_b, col0, width, scaled=None, wc=LANES):
    Bn, L, _ = xbc_raw.shape
    n_d = len(dparts)
    has_s = scaled is not None
    mid = slice(CONV_HALO, CONV_HALO + CONV_ROWS)
    c0 = col0 // wc

    def kern(*refs):
        x_ref, d_refs = refs[0], refs[1:1 + n_d]
        pos = 1 + n_d
        if has_s:
            s_ref, pos = refs[pos], pos + 1
        w_ref, b_ref = refs[pos], refs[pos + 1]
        pos += 2
        if has_s:
            scale = refs[pos][...]
            pos += 1
        dx_ref, dw_ref, db_ref = refs[pos:pos + 3]
        acc = refs[pos + 3]
        w, b = w_ref[...], b_ref[...]
        acc[...] = jnp.zeros(acc.shape, F32)

        def load(s, n):
            dy = d_refs[0][0, pl.ds(s, n), :]
            for r in d_refs[1:]:
                dy = dy + r[0, pl.ds(s, n), :]
            if has_s:
                dy = dy + s_ref[0, pl.ds(s, n), :] * scale
            return jnp.concatenate([x_ref[0, pl.ds(s, n), :], dy], axis=1)

        def work(start, both):
            xh, dyh = both[:, 0:wc], both[:, wc:]
            taps = [_rows_at(xh, k - 2) for k in range(4)]
            pre = b
            for k in range(4):
                pre = pre + taps[k] * w[k:k + 1]
            dpre = dyh * _dsilu(pre)
            dx = dpre * w[2:3]
            for k in (0, 1, 3):
                dx = dx + _rows_at(dpre, 2 - k) * w[k:k + 1]
            dx_ref[0, pl.ds(start, CONV_ROWS), :] = dx[mid].astype(dx_ref.dtype)
            dm = dpre[mid]
            acc[...] += _stack_rows([_colsum(dm * taps[k][mid]) for k in range(4)] + [_colsum(dm)] + [jnp.zeros((1, wc), F32)] * 3)

        _halo_chunks(L, load, work)
        first = pl.program_id(1) == 0

        @pl.when(first)
        def _():
            dw_ref[...] = acc[0:4]
            db_ref[...] = acc[4:5]

        @pl.when(jnp.logical_not(first))
        def _():
            dw_ref[...] += acc[0:4]
            db_ref[...] += acc[4:5]

    slab = lambda off: pl.BlockSpec((1, L, wc), lambda j, b, off=off: (b, 0, off + j))
    in_specs = [slab(c0)] + [slab(0)] * n_d + ([slab(0)] if has_s else [])
    in_specs += [pl.BlockSpec((4, wc), lambda j, b: (0, c0 + j)), pl.BlockSpec((1, wc), lambda j, b: (0, c0 + j))]
    args = [xbc_raw, *dparts] + ([scaled[0]] if has_s else []) + [conv_w, conv_b]
    if has_s:
        in_specs.append(pl.BlockSpec((1, wc), lambda j, b: (0, j)))
        args.append(scaled[1])
    return pl.pallas_call(
        kern, name=name,
        out_shape=[jax.ShapeDtypeStruct((Bn, L, width), BF16), jax.ShapeDtypeStruct((4, width), F32), jax.ShapeDtypeStruct((1, width), F32)],
        grid=(width // wc, Bn), in_specs=in_specs,
        out_specs=[pl.BlockSpec((1, L, wc), lambda j, b: (b, 0, j)), pl.BlockSpec((4, wc), lambda j, b: (0, j)),
                   pl.BlockSpec((1, wc), lambda j, b: (0, j))],
        scratch_shapes=[pltpu.VMEM((8, wc), F32)],
        compiler_params=_cp(("arbitrary", "arbitrary")))(*args)


def _box_mean(x, k, step, pos, n, L, transpose):
    lo, hi = k // 2, k - 1 - k // 2
    cnt = (jnp.minimum(pos + hi + 1, n) - jnp.maximum(pos - lo, 0)).astype(F32)
    if transpose:
        x = x / cnt
        lo, hi = hi, lo
    acc = x
    for o in range(-lo, hi + 1):
        if o == 0:
            continue
        rolled = pltpu.roll(x, (-o * step) % L, 0)
        acc = acc + jnp.where((pos + o >= 0) & (pos + o < n), rolled, 0.0)
    return acc if transpose else acc / cnt


def pool_diff(name, v, col0, gi, transpose):
    L = v.shape[1]
    rows = L // GRID_W
    k = POOL_WINDOWS[gi]

    def body(x):
        tok = lax.broadcasted_iota(jnp.int32, x.shape, 0)
        col = tok & (GRID_W - 1)
        row = tok >> 6
        if not transpose:
            m = _box_mean(x, k, GRID_W, row, rows, L, False)
            m = _box_mean(m, k, 1, col, GRID_W, L, False)
        else:
            m = _box_mean(x, k, 1, col, GRID_W, L, True)
            m = _box_mean(m, k, GRID_W, row, rows, L, True)
        return m - x

    return slab_call(name, body, [(v, col0)], [], [(POOL_GROUP, BF16)], [])[0]


def pool_mix_fwd(name, dgs, z_pool, pool_w, pool_scale):
    def body(d0, d1, d2, d3, z, w, scale):
        q = jnp.concatenate([_dot(d, w[g * POOL_GROUP:(g + 1) * POOL_GROUP]) for g, d in enumerate((d0, d1, d2, d3))], axis=1)
        return q * scale * _silu(z)

    return tok_call(name, body, list(dgs) + [z_pool], [], [pool_w, pool_scale], [(D, BF16)], [], [])[0]


def pool_mix_bwd(name, dgs, z_pool, dyp, pool_w, pool_scale):
    def body(d0, d1, d2, d3, z, dyp, w, scale):
        ds = (d0, d1, d2, d3)
        q = jnp.concatenate([_dot(d, w[g * POOL_GROUP:(g + 1) * POOL_GROUP]) for g, d in enumerate(ds)], axis=1)
        dypm = dyp * _silu(z)
        dz = dyp * (q * scale) * _dsilu(z)
        dq = (dypm * scale).astype(BF16)
        dds, gws = [], []
        for g, d in enumerate(ds):
            dqg = dq[:, g * POOL_GROUP:(g + 1) * POOL_GROUP]
            dds.append(_dot_nt(dqg, w[g * POOL_GROUP:(g + 1) * POOL_GROUP]))
            gws.append(_dot_tn(d, dqg))
        return (*dds, dz, jnp.concatenate(gws, axis=0), _colsum(dypm * q))

    return tok_call(name, body, list(dgs) + [z_pool, dyp], [], [pool_w, pool_scale],
                    [(POOL_GROUP, F32)] * 4 + [(D, BF16)], [], [(D, POOL_GROUP), (1, D)])


def _cumsum_lanes(a, reverse):
    n = a.shape[1]
    k = lax.broadcasted_iota(jnp.int32, (n, n), 0)
    i = lax.broadcasted_iota(jnp.int32, (n, n), 1)
    tri = jnp.where((k >= i) if reverse else (k <= i), 1.0, 0.0).astype(BF16)
    return _dot_exact01(a, tri)


def _rows_to_cols(rows):
    r = rows.shape[0]
    if r < LANES:
        rows = jnp.concatenate([rows, jnp.zeros((LANES - r, rows.shape[1]), F32)], axis=0)
    return rows.T


def _cols_to_rows(cols):
    q = cols[0].shape[0]
    lane = lax.broadcasted_iota(jnp.int32, (q, LANES), 1)
    acc = jnp.zeros((q, LANES), F32)
    for r, c in enumerate(cols):
        acc = acc + jnp.where(lane == r, c, 0.0)
    return acc.T[0:len(cols)]


def _ssd_scalars(dtraw, bias, alog, reverse):
    dt = _softplus(dtraw + bias)
    A = -jnp.exp(alog)
    cs = _cumsum_lanes(dt * A, reverse)
    total = cs[:, 0:1] if reverse else cs[:, CHUNK - 1:CHUNK]
    return dt, A, cs, total


def _decay_matrix(cs_col, cs_row, reverse):
    i = lax.broadcasted_iota(jnp.int32, (CHUNK, CHUNK), 0)
    j = lax.broadcasted_iota(jnp.int32, (CHUNK, CHUNK), 1)
    keep = (i <= j) if reverse else (i >= j)
    return jnp.exp(jnp.where(keep, cs_col - cs_row, -jnp.inf))


def ssd_fwd_v1(name, dtT, bias, alog, xbc, h0, direction, with_y):
    Bn, L = xbc.shape[:2]
    nc = L // CHUNK
    reverse = direction == 1
    rowblk = direction * N_BC

    def chunk_of(s):
        return (nc - 1 - s) if reverse else s

    def kern(dt_ref, bias_ref, alog_ref, x_ref, b_ref, c_ref, h0_ref, *rest):
        if with_y:
            y_ref, hs_ref, hf_ref, h_scr, xt_scr = rest
        else:
            hs_ref, hf_ref, h_scr, xt_scr = rest
        s = pl.program_id(2)

        @pl.when(s == 0)
        def _():
            h_scr[...] = h0_ref[0, 0]

        dt, _, cs, total = _ssd_scalars(dt_ref[0], bias_ref[0], alog_ref[0], reverse)
        e_row = jnp.exp(cs)
        t_row = jnp.exp(total - cs)
        dc = jnp.exp(total)
        cols = _rows_to_cols(jnp.concatenate([dt, e_row, t_row, cs], axis=0))
        x = x_ref[0]
        bm = b_ref[0].astype(BF16)
        cm = c_ref[0].astype(BF16)
        h = h_scr[...]
        hs_ref[0, 0, 0] = h
        if with_y:
            cb = _dot_nt(cm, bm)
            yoff = _dot(cm, h.astype(BF16))
        for r in range(HPG):
            sl = slice(r * HEAD_DIM, (r + 1) * HEAD_DIM)
            xdt = x[:, sl] * cols[:, r:r + 1]
            if with_y:
                lr = _decay_matrix(cols[:, 3 * HPG + r:3 * HPG + r + 1], cs[r:r + 1], reverse)
                ydiag = _dot((cb * lr).astype(BF16), xdt.astype(BF16))
                y_ref[0, :, sl] = ydiag + yoff[:, sl] * cols[:, HPG + r:HPG + r + 1]
            xt_scr[:, sl] = (xdt * cols[:, 2 * HPG + r:2 * HPG + r + 1]).astype(BF16)
        st = _dot_tn(bm, xt_scr[...])
        for r in range(HPG):
            sl = slice(r * HEAD_DIM, (r + 1) * HEAD_DIM)
            h_scr[:, sl] = h[:, sl] * dc[r:r + 1] + st[:, sl]

        @pl.when(s == nc - 1)
        def _():
            hf_ref[0, 0] = h_scr[...]

    in_specs = [
        pl.BlockSpec((1, HPG, CHUNK), lambda b, g, s: (b, rowblk + g, chunk_of(s))),
        pl.BlockSpec((1, HPG, 1), lambda b, g, s: (rowblk + g, 0, 0)),
        pl.BlockSpec((1, HPG, 1), lambda b, g, s: (rowblk + g, 0, 0)),
        pl.BlockSpec((1, CHUNK, GW), lambda b, g, s: (b, chunk_of(s), g)),
        pl.BlockSpec((1, CHUNK, D_STATE), lambda b, g, s: (b, chunk_of(s), D_INNER // D_STATE + g)),
        pl.BlockSpec((1, CHUNK, D_STATE), lambda b, g, s: (b, chunk_of(s), D_INNER // D_STATE + N_BC + g)),
        pl.BlockSpec((1, 1, D_STATE, GW), lambda b, g, s: (b, g, 0, 0)),
    ]
    out_shape, out_specs = [], []
    if with_y:
        out_shape.append(jax.ShapeDtypeStruct((Bn, L, D_INNER), F32))
        out_specs.append(pl.BlockSpec((1, CHUNK, GW), lambda b, g, s: (b, chunk_of(s), g)))
    out_shape += [jax.ShapeDtypeStruct((Bn, N_BC, nc, D_STATE, GW), F32), jax.ShapeDtypeStruct((Bn, N_BC, D_STATE, GW), F32)]
    out_specs += [pl.BlockSpec((1, 1, 1, D_STATE, GW), lambda b, g, s: (b, g, chunk_of(s), 0, 0)),
                  pl.BlockSpec((1, 1, D_STATE, GW), lambda b, g, s: (b, g, 0, 0))]
    return pl.pallas_call(
        kern, name=name, out_shape=out_shape, grid=(Bn, N_BC, nc), in_specs=in_specs, out_specs=out_specs,
        scratch_shapes=[pltpu.VMEM((D_STATE, GW), F32), pltpu.VMEM((CHUNK, GW), BF16)],
        compiler_params=_cp(("arbitrary", "arbitrary", "arbitrary")))(dtT, bias, alog, xbc, xbc, xbc, h0)


def ssd_bwd_v1(name, dtT, bias, alog, xbc, h_start, dy, dh_final, direction):
    Bn, L = xbc.shape[:2]
    nc = L // CHUNK
    reverse = direction == 1
    rowblk = direction * N_BC
    has_y = dy is not None
    last = 0 if reverse else CHUNK - 1

    def chunk_of(s):
        return s if reverse else (nc - 1 - s)

    def kern(*refs):
        if has_y:
            (dt_ref, bias_ref, alog_ref, x_ref, b_ref, c_ref, hs_ref, dhf_ref, dy_ref,
             dx_ref, db_ref, dc_ref, ddt_ref, dbias_ref, dalog_ref, dh0_ref, dh_scr, e_scr, t_scr) = refs
        else:
            (dt_ref, bias_ref, alog_ref, x_ref, b_ref, hs_ref, dhf_ref,
             dx_ref, db_ref, ddt_ref, dbias_ref, dalog_ref, dh0_ref, dh_scr, t_scr) = refs
        s = pl.program_id(2)

        @pl.when(s == 0)
        def _():
            dh_scr[...] = dhf_ref[0, 0]
            dbias_ref[...] = jnp.zeros(dbias_ref.shape, F32)
            dalog_ref[...] = jnp.zeros(dalog_ref.shape, F32)

        dtraw = dt_ref[0]
        dt, A, cs, total = _ssd_scalars(dtraw, bias_ref[0], alog_ref[0], reverse)
        e_row = jnp.exp(cs)
        t_row = jnp.exp(total - cs)
        dcy = jnp.exp(total)
        cols = _rows_to_cols(jnp.concatenate([dt, e_row, t_row, cs], axis=0))
        x = x_ref[0]
        bm = b_ref[0].astype(BF16)
        h = hs_ref[0, 0, 0]
        dh = dh_scr[...]
        dh_bf = dh.astype(BF16)
        bdh = _dot(bm, dh_bf)
        if has_y:
            cm = c_ref[0].astype(BF16)
            dyv = dy_ref[0]
            cb = _dot_nt(cm, bm)
            yoff = _dot(cm, h.astype(BF16))
            dcb = jnp.zeros((CHUNK, CHUNK), F32)
        col_terms, row_terms, ddt_cols, dtot = [], [], [], []
        for r in range(HPG):
            sl = slice(r * HEAD_DIM, (r + 1) * HEAD_DIM)
            dt_c = cols[:, r:r + 1]
            e_c = cols[:, HPG + r:HPG + r + 1]
            t_c = cols[:, 2 * HPG + r:2 * HPG + r + 1]
            xr = x[:, sl]
            xdt = xr * dt_c
            dxdt = t_c * bdh[:, sl]
            d_t = jnp.sum(bdh[:, sl] * xdt, axis=1, keepdims=True)
            col = -(t_c * d_t)
            tot = jnp.sum(t_c * d_t, axis=0, keepdims=True) + dcy[r:r + 1] * jnp.sum(h[:, sl] * dh[:, sl], keepdims=True)
            if has_y:
                dyr = dyv[:, sl]
                lr = _decay_matrix(cols[:, 3 * HPG + r:3 * HPG + r + 1], cs[r:r + 1], reverse)
                w = cb * lr
                gm = _dot_nt(dyr.astype(BF16), xdt.astype(BF16))
                m = gm * w
                dcb = dcb + gm * lr
                dxdt = dxdt + _dot_tn(w.astype(BF16), dyr.astype(BF16))
                col = col + jnp.sum(m, axis=1, keepdims=True) + jnp.sum(yoff[:, sl] * dyr, axis=1, keepdims=True) * e_c
                row_terms.append(-jnp.sum(m, axis=0, keepdims=True))
                e_scr[:, sl] = (e_c * dyr).astype(BF16)
            t_scr[:, sl] = (t_c * xdt).astype(BF16)
            dx_ref[0, :, sl] = dxdt * dt_c
            ddt_cols.append(jnp.sum(dxdt * xr, axis=1, keepdims=True))
            col_terms.append(col)
            dtot.append(tot)
        db = _dot_nt(t_scr[...], dh_bf)
        if has_y:
            dcb_bf = dcb.astype(BF16)
            db = db + _dot_tn(dcb_bf, cm)
            dc_ref[0] = _dot(dcb_bf, bm) + _dot_nt(e_scr[...], h.astype(BF16))
            cte = _dot_tn(cm, e_scr[...])
        db_ref[0] = db
        for r in range(HPG):
            sl = slice(r * HEAD_DIM, (r + 1) * HEAD_DIM)
            new = dh[:, sl] * dcy[r:r + 1]
            if has_y:
                new = new + cte[:, sl]
            dh_scr[:, sl] = new
        dcs = _cols_to_rows(col_terms)
        if has_y:
            dcs = dcs + _stack_rows(row_terms)
        lane = lax.broadcasted_iota(jnp.int32, (HPG, CHUNK), 1)
        dcs = dcs + jnp.where(lane == last, _stack_rows([jnp.broadcast_to(t, (1, CHUNK)) for t in dtot]), 0.0)
        da = _cumsum_lanes(dcs, not reverse)
        ddt = da * A + _cols_to_rows(ddt_cols)
        ddtraw = ddt * _sigmoid(dtraw + bias_ref[0])
        ddt_ref[0] = ddtraw
        dbias_ref[0, 0] += jnp.sum(ddtraw, axis=1, keepdims=True)
        dalog_ref[0, 0] += jnp.sum(da * dt, axis=1, keepdims=True) * A

        @pl.when(s == nc - 1)
        def _():
            dh0_ref[0, 0] = dh_scr[...]

    cidx = lambda b, g, s: (b, chunk_of(s), g)
    in_specs = [
        pl.BlockSpec((1, HPG, CHUNK), lambda b, g, s: (b, rowblk + g, chunk_of(s))),
        pl.BlockSpec((1, HPG, 1), lambda b, g, s: (rowblk + g, 0, 0)),
        pl.BlockSpec((1, HPG, 1), lambda b, g, s: (rowblk + g, 0, 0)),
        pl.BlockSpec((1, CHUNK, GW), cidx),
        pl.BlockSpec((1, CHUNK, D_STATE), lambda b, g, s: (b, chunk_of(s), D_INNER // D_STATE + g)),
    ]
    args = [dtT, bias, alog, xbc, xbc]
    if has_y:
        in_specs.append(pl.BlockSpec((1, CHUNK, D_STATE), lambda b, g, s: (b, chunk_of(s), D_INNER // D_STATE + N_BC + g)))
        args.append(xbc)
    in_specs += [pl.BlockSpec((1, 1, 1, D_STATE, GW), lambda b, g, s: (b, g, chunk_of(s), 0, 0)),
                 pl.BlockSpec((1, 1, D_STATE, GW), lambda b, g, s: (b, g, 0, 0))]
    args += [h_start, dh_final]
    if has_y:
        in_specs.append(pl.BlockSpec((1, CHUNK, GW), cidx))
        args.append(dy)
    out_shape = [jax.ShapeDtypeStruct((Bn, L, D_INNER), F32), jax.ShapeDtypeStruct((Bn, L, N_BC * D_STATE), F32)]
    out_specs = [pl.BlockSpec((1, CHUNK, GW), cidx), pl.BlockSpec((1, CHUNK, D_STATE), cidx)]
    if has_y:
        out_shape.append(jax.ShapeDtypeStruct((Bn, L, N_BC * D_STATE), F32))
        out_specs.append(pl.BlockSpec((1, CHUNK, D_STATE), cidx))
    out_shape += [jax.ShapeDtypeStruct((Bn, N_HEADS, L), F32), jax.ShapeDtypeStruct((Bn, N_BC, HPG, 1), F32),
                  jax.ShapeDtypeStruct((Bn, N_BC, HPG, 1), F32), jax.ShapeDtypeStruct((Bn, N_BC, D_STATE, GW), F32)]
    out_specs += [pl.BlockSpec((1, HPG, CHUNK), lambda b, g, s: (b, g, chunk_of(s))),
                  pl.BlockSpec((1, 1, HPG, 1), lambda b, g, s: (b, g, 0, 0)),
                  pl.BlockSpec((1, 1, HPG, 1), lambda b, g, s: (b, g, 0, 0)),
                  pl.BlockSpec((1, 1, D_STATE, GW), lambda b, g, s: (b, g, 0, 0))]
    scratch = [pltpu.VMEM((D_STATE, GW), F32)] + ([pltpu.VMEM((CHUNK, GW), BF16)] if has_y else []) + [pltpu.VMEM((CHUNK, GW), BF16)]
    res = pl.pallas_call(
        kern, name=name, out_shape=out_shape, grid=(Bn, N_BC, nc), in_specs=in_specs, out_specs=out_specs,
        scratch_shapes=scratch, compiler_params=_cp(("arbitrary", "arbitrary", "arbitrary")))(*args)
    if has_y:
        return res
    dxs, db, ddt, dbias, dalog, dh0 = res
    return dxs, db, None, ddt, dbias, dalog, dh0


def _tri_mask(transposed, reverse):
    sub = lax.broadcasted_iota(jnp.int32, (CHUNK, CHUNK), 0)
    lane = lax.broadcasted_iota(jnp.int32, (CHUNK, CHUNK), 1)
    i, j = (lane, sub) if transposed else (sub, lane)
    return (i <= j) if reverse else (i >= j)


def ssd_fwd(name, dtT, bias, alog, xbc, h0, direction, with_y):
    Bn, L = xbc.shape[:2]
    nc = L // CHUNK
    reverse = direction == 1
    rowblk = direction * N_BC

    def chunk_of(s):
        return (nc - 1 - s) if reverse else s

    def kern(dt_ref, bias_ref, alog_ref, x_ref, b_ref, c_ref, h0_ref, *rest):
        if with_y:
            y_ref, hs_ref, hf_ref, h_scr = rest
        else:
            hs_ref, hf_ref, h_scr = rest
        s = pl.program_id(2)

        @pl.when(s == 0)
        def _():
            h_scr[...] = h0_ref[0, 0]

        dt, _, cs, total = _ssd_scalars(dt_ref[0], bias_ref[0], alog_ref[0], reverse)
        u = cs - jnp.log(dt)
        dtt = jnp.exp(total - u)
        dc = jnp.exp(total)
        x_bf = x_ref[0].astype(BF16)
        bm = b_ref[0]
        h = h_scr[...]
        h_bf = h.astype(BF16)
        hs_ref[0, 0, 0] = h
        bt = bm.T
        if with_y:
            cm = c_ref[0]
            cb = _dot_nt(cm.astype(BF16), bm.astype(BF16))
            cs_cols = _rows_to_cols(cs)
            keep = _tri_mask(False, reverse)
        first = lax.broadcasted_iota(jnp.int32, (1, LANES), 1) < HEAD_DIM
        heads = range(HPG)
        psl = [slice((r // 2) * LANES, (r // 2 + 1) * LANES) for r in heads]
        lhs = []
        if with_y:
            for r in heads:
                cs_col = jnp.broadcast_to(cs_cols[:, r:r + 1], (CHUNK, LANES))
                wf = cb * jnp.exp(jnp.where(keep, cs_col - u[r:r + 1], -jnp.inf))
                lhs.append(jnp.concatenate([wf.astype(BF16), (cm * jnp.exp(cs_col)).astype(BF16)], axis=1))
        bts = [(bt * dtt[r:r + 1]).astype(BF16) for r in heads]
        sts = [_dot(bts[r], x_bf[:, psl[r]]) for r in heads]
        if with_y:
            ys = [_dot(lhs[r], jnp.concatenate([x_bf[:, psl[r]], h_bf[:, psl[r]]], axis=0)) for r in heads]
        for p in range(HPG // 2):
            if with_y:
                y_ref[0, :, psl[2 * p]] = jnp.where(first, ys[2 * p], ys[2 * p + 1])
            dc_p = jnp.where(first, dc[2 * p:2 * p + 1], dc[2 * p + 1:2 * p + 2])
            h_scr[:, psl[2 * p]] = h[:, psl[2 * p]] * dc_p + jnp.where(first, sts[2 * p], sts[2 * p + 1])

        @pl.when(s == nc - 1)
        def _():
            hf_ref[0, 0] = h_scr[...]

    in_specs = [
        pl.BlockSpec((1, HPG, CHUNK), lambda b, g, s: (b, rowblk + g, chunk_of(s))),
        pl.BlockSpec((1, HPG, 1), lambda b, g, s: (rowblk + g, 0, 0)),
        pl.BlockSpec((1, HPG, 1), lambda b, g, s: (rowblk + g, 0, 0)),
        pl.BlockSpec((1, CHUNK, GW), lambda b, g, s: (b, chunk_of(s), g)),
        pl.BlockSpec((1, CHUNK, D_STATE), lambda b, g, s: (b, chunk_of(s), D_INNER // D_STATE + g)),
        pl.BlockSpec((1, CHUNK, D_STATE), lambda b, g, s: (b, chunk_of(s), D_INNER // D_STATE + N_BC + g)),
        pl.BlockSpec((1, 1, D_STATE, GW), lambda b, g, s: (b, g, 0, 0)),
    ]
    out_shape, out_specs = [], []
    if with_y:
        out_shape.append(jax.ShapeDtypeStruct((Bn, L, D_INNER), F32))
        out_specs.append(pl.BlockSpec((1, CHUNK, GW), lambda b, g, s: (b, chunk_of(s), g)))
    out_shape += [jax.ShapeDtypeStruct((Bn, N_BC, nc, D_STATE, GW), F32), jax.ShapeDtypeStruct((Bn, N_BC, D_STATE, GW), F32)]
    out_specs += [pl.BlockSpec((1, 1, 1, D_STATE, GW), lambda b, g, s: (b, g, chunk_of(s), 0, 0)),
                  pl.BlockSpec((1, 1, D_STATE, GW), lambda b, g, s: (b, g, 0, 0))]
    return pl.pallas_call(
        kern, name=name, out_shape=out_shape, grid=(Bn, N_BC, nc), in_specs=in_specs, out_specs=out_specs,
        scratch_shapes=[pltpu.VMEM((D_STATE, GW), F32)],
        compiler_params=_cp(("arbitrary", "arbitrary", "arbitrary")))(dtT, bias, alog, xbc, xbc, xbc, h0)


def ssd_bwd(name, dtT, bias, alog, xbc, h_start, dy, dh_final, direction):
    Bn, L = xbc.shape[:2]
    nc = L // CHUNK
    reverse = direction == 1
    rowblk = direction * N_BC
    has_y = dy is not None
    last = 0 if reverse else CHUNK - 1

    def chunk_of(s):
        return s if reverse else (nc - 1 - s)

    def kern(*refs):
        if has_y:
            (dt_ref, bias_ref, alog_ref, x_ref, b_ref, hs_ref, dhf_ref, c_ref, dy_ref,
             dx_ref, db_ref, ddt_ref, dbias_ref, dalog_ref, dh0_ref, dc_ref, dh_scr) = refs
        else:
            (dt_ref, bias_ref, alog_ref, x_ref, b_ref, hs_ref, dhf_ref,
             dx_ref, db_ref, ddt_ref, dbias_ref, dalog_ref, dh0_ref, dh_scr) = refs
        s = pl.program_id(2)

        @pl.when(s == 0)
        def _():
            dh_scr[...] = dhf_ref[0, 0]
            dbias_ref[...] = jnp.zeros(dbias_ref.shape, F32)
            dalog_ref[...] = jnp.zeros(dalog_ref.shape, F32)

        dtraw = dt_ref[0]
        dt, A, cs, total = _ssd_scalars(dtraw, bias_ref[0], alog_ref[0], reverse)
        u = cs - jnp.log(dt)
        dtt = jnp.exp(total - u)
        dcy = jnp.exp(total)
        u_cols = _rows_to_cols(u)
        x_bf = x_ref[0].astype(BF16)
        bm = b_ref[0]
        bt = bm.T
        h = hs_ref[0, 0, 0]
        dh = dh_scr[...]
        dh_bf = dh.astype(BF16)
        dbt = jnp.zeros((D_STATE, CHUNK), F32)
        if has_y:
            cm = c_ref[0]
            ct = cm.T
            e_row = jnp.exp(cs)
            dy_bf = dy_ref[0].astype(BF16)
            h_bf = h.astype(BF16)
            cbt = _dot_nt(bm.astype(BF16), cm.astype(BF16))
            keep = _tri_mask(True, reverse)
            dcbt = jnp.zeros((CHUNK, CHUNK), F32)
            dct = jnp.zeros((D_STATE, CHUNK), F32)
        tots, out_rows, in_rows, in_cols = [], [], [], []
        first = lax.broadcasted_iota(jnp.int32, (1, LANES), 1) < HEAD_DIM
        heads = range(HPG)
        psl = [slice((r // 2) * LANES, (r // 2 + 1) * LANES) for r in heads]
        mine = [first if r % 2 == 0 else jnp.logical_not(first) for r in heads]
        zeros_bf = jnp.zeros((CHUNK, LANES), BF16)

        def prep(r):
            u_col = jnp.broadcast_to(u_cols[:, r:r + 1], (CHUNK, LANES))
            bs = (bm * jnp.exp(total[r:r + 1] - u_col)).astype(BF16)
            if not has_y:
                return bs, None
            et = jnp.exp(jnp.where(keep, cs[r:r + 1] - u_col, -jnp.inf))
            return jnp.concatenate([(cbt * et).astype(BF16), bs], axis=1), et

        def matmuls(r, lhs):
            p2raw = _dot_nt(dh_bf[:, psl[r]], jnp.where(mine[r], x_bf[:, psl[r]], zeros_bf))
            if not has_y:
                return p2raw, None, None, _dot(lhs, dh_bf[:, psl[r]])
            a1 = _dot_nt(jnp.concatenate([x_bf[:, psl[r]], h_bf[:, psl[r]]], axis=0),
                         jnp.where(mine[r], dy_bf[:, psl[r]], zeros_bf))
            new = _dot((ct * e_row[r:r + 1]).astype(BF16), dy_bf[:, psl[r]])
            dx = _dot(lhs, jnp.concatenate([dy_bf[:, psl[r]], dh_bf[:, psl[r]]], axis=0))
            return p2raw, a1, new, dx

        def post(r, p2raw, a1, et, dbt, dcbt, dct):
            if has_y:
                pt = a1[0:CHUNK] * et
                dcbt = dcbt + pt
                mt = pt * cbt
                ph = a1[CHUNK:] * e_row[r:r + 1]
                dct = dct + ph
                out_rows.append(_colsum(mt + ct * ph))
                in_cols.append(jnp.sum(mt, axis=1, keepdims=True))
            p2 = p2raw * dtt[r:r + 1]
            dbt = dbt + p2
            t_term = _colsum(bt * p2)
            in_rows.append(t_term)
            hdh = h[:, psl[r]] * dh[:, psl[r]]
            tot = jnp.sum(t_term, axis=1, keepdims=True) + dcy[r:r + 1] * jnp.sum(jnp.where(mine[r], hdh, 0.0), keepdims=True)
            tots.append(jnp.broadcast_to(tot, (1, CHUNK)))
            return dbt, dcbt, dct

        if not has_y:
            dcbt = dct = None
        dxs, news, pending = [], [], []
        batch = HPG
        for r0 in range(0, HPG, batch):
            preps = [prep(r) for r in range(r0, r0 + batch)]
            mms = [matmuls(r, preps[r - r0][0]) for r in range(r0, r0 + batch)]
            for args in pending:
                dbt, dcbt, dct = post(*args, dbt, dcbt, dct)
            pending = [(r, mms[r - r0][0], mms[r - r0][1], preps[r - r0][1]) for r in range(r0, r0 + batch)]
            dxs += [m[3] for m in mms]
            news += [m[2] for m in mms]
        for args in pending:
            dbt, dcbt, dct = post(*args, dbt, dcbt, dct)
        for p in range(HPG // 2):
            dx_ref[0, :, psl[2 * p]] = jnp.where(first, dxs[2 * p], dxs[2 * p + 1])
            new = dh[:, psl[2 * p]] * jnp.where(first, dcy[2 * p:2 * p + 1], dcy[2 * p + 1:2 * p + 2])
            if has_y:
                new = new + jnp.where(first, news[2 * p], news[2 * p + 1])
            dh_scr[:, psl[2 * p]] = new
        db = dbt.T
        if has_y:
            dcbt_bf = dcbt.astype(BF16)
            db = db + _dot(dcbt_bf, cm.astype(BF16))
            dc_ref[0] = dct.T + _dot_tn(dcbt_bf, bm.astype(BF16))
        db_ref[0] = db
        s_row = _stack_rows(in_rows)
        lane = lax.broadcasted_iota(jnp.int32, (HPG, CHUNK), 1)
        dcs = jnp.where(lane == last, _stack_rows(tots), 0.0)
        if has_y:
            s_row = s_row + _cols_to_rows(in_cols)
            dcs = dcs + _stack_rows(out_rows)
        dcs = dcs - s_row
        da = _cumsum_lanes(dcs, not reverse)
        ddt = da * A + jnp.where(dt > 0.0, s_row / dt, 0.0)
        ddtraw = ddt * _sigmoid(dtraw + bias_ref[0])
        ddt_ref[0] = ddtraw
        dbias_ref[0, 0] += jnp.sum(ddtraw, axis=1, keepdims=True)
        dalog_ref[0, 0] += jnp.sum(da * dt, axis=1, keepdims=True) * A

        @pl.when(s == nc - 1)
        def _():
            dh0_ref[0, 0] = dh_scr[...]

    cidx = lambda b, g, s: (b, chunk_of(s), g)
    hidx = lambda b, g, s: (b, g, 0, 0)
    in_specs = [
        pl.BlockSpec((1, HPG, CHUNK), lambda b, g, s: (b, rowblk + g, chunk_of(s))),
        pl.BlockSpec((1, HPG, 1), lambda b, g, s: (rowblk + g, 0, 0)),
        pl.BlockSpec((1, HPG, 1), lambda b, g, s: (rowblk + g, 0, 0)),
        pl.BlockSpec((1, CHUNK, GW), cidx),
        pl.BlockSpec((1, CHUNK, D_STATE), lambda b, g, s: (b, chunk_of(s), D_INNER // D_STATE + g)),
        pl.BlockSpec((1, 1, 1, D_STATE, GW), lambda b, g, s: (b, g, chunk_of(s), 0, 0)),
        pl.BlockSpec((1, 1, D_STATE, GW), hidx),
    ]
    args = [dtT, bias, alog, xbc, xbc, h_start, dh_final]
    if has_y:
        in_specs += [pl.BlockSpec((1, CHUNK, D_STATE), lambda b, g, s: (b, chunk_of(s), D_INNER // D_STATE + N_BC + g)),
                     pl.BlockSpec((1, CHUNK, GW), cidx)]
        args += [xbc, dy]
    out_shape = [jax.ShapeDtypeStruct((Bn, L, D_INNER), F32), jax.ShapeDtypeStruct((Bn, L, N_BC * D_STATE), F32),
                 jax.ShapeDtypeStruct((Bn, N_HEADS, L), F32), jax.ShapeDtypeStruct((Bn, N_BC, HPG, 1), F32),
                 jax.ShapeDtypeStruct((Bn, N_BC, HPG, 1), F32), jax.ShapeDtypeStruct((Bn, N_BC, D_STATE, GW), F32)]
    out_specs = [pl.BlockSpec((1, CHUNK, GW), cidx), pl.BlockSpec((1, CHUNK, D_STATE), cidx),
                 pl.BlockSpec((1, HPG, CHUNK), lambda b, g, s: (b, g, chunk_of(s))),
                 pl.BlockSpec((1, 1, HPG, 1), hidx), pl.BlockSpec((1, 1, HPG, 1), hidx), pl.BlockSpec((1, 1, D_STATE, GW), hidx)]
    if has_y:
        out_shape.append(jax.ShapeDtypeStruct((Bn, L, N_BC * D_STATE), F32))
        out_specs.append(pl.BlockSpec((1, CHUNK, D_STATE), cidx))
    res = pl.pallas_call(
        kern, name=name, out_shape=out_shape, grid=(Bn, N_BC, nc), in_specs=in_specs, out_specs=out_specs,
        scratch_shapes=[pltpu.VMEM((D_STATE, GW), F32)],
        compiler_params=_cp(("arbitrary", "arbitrary", "arbitrary")))(*args)
    dxs, db, ddt, dbias, dalog, dh0 = res[:6]
    return dxs, db, (res[6] if has_y else None), ddt, dbias, dalog, dh0


GPS = 4


def ssd_fwd3(name, dtT, bias, alog, xbc, h0, direction, with_y):
    Bn, L = xbc.shape[:2]
    nc = L // CHUNK
    reverse = direction == 1
    blk0 = direction * (N_BC // GPS)
    gs = range(GPS)

    def chunk_of(s):
        return (nc - 1 - s) if reverse else s

    def kern(dt_ref, bias_ref, alog_ref, x_ref, b_ref, c_ref, h0_ref, *rest):
        if with_y:
            y_ref, hs_ref, hf_ref, h_scr = rest
        else:
            hs_ref, hf_ref, h_scr = rest
        s = pl.program_id(2)

        @pl.when(s == 0)
        def _():
            h_scr[...] = h0_ref[0]

        first = lax.broadcasted_iota(jnp.int32, (1, LANES), 1) < HEAD_DIM
        heads = range(HPG)
        psl = [slice((r // 2) * LANES, (r // 2 + 1) * LANES) for r in heads]
        keep = _tri_mask(False, reverse)
        sc, x_bf, bm, h, h_bf, bt, cm, cb, cs_cols = [], [], [], [], [], [], [], [], []
        for g in gs:
            dt, _, cs, total = _ssd_scalars(dt_ref[0, g * HPG:(g + 1) * HPG], bias_ref[g], alog_ref[g], reverse)
            u = cs - jnp.log(dt)
            sc.append((cs, u, jnp.exp(total - u), jnp.exp(total)))
            x_bf.append(x_ref[0, :, g * GW:(g + 1) * GW].astype(BF16))
            bm.append(b_ref[0, :, g * D_STATE:(g + 1) * D_STATE])
            h.append(h_scr[g])
            h_bf.append(h[g].astype(BF16))
            hs_ref[0, g, 0] = h[g]
            bt.append(bm[g].T)
            if with_y:
                cm.append(c_ref[0, :, g * D_STATE:(g + 1) * D_STATE])
                cb.append(_dot_nt(cm[g].astype(BF16), bm[g].astype(BF16)))
                cs_cols.append(_rows_to_cols(cs))
        lhs = [[] for _ in gs]
        if with_y:
            for g in gs:
                cs, u = sc[g][0], sc[g][1]
                for r in heads:
                    cs_col = jnp.broadcast_to(cs_cols[g][:, r:r + 1], (CHUNK, LANES))
                    wf = cb[g] * jnp.exp(jnp.where(keep, cs_col - u[r:r + 1], -jnp.inf))
                    lhs[g].append(jnp.concatenate([wf.astype(BF16), (cm[g] * jnp.exp(cs_col)).astype(BF16)], axis=1))
        bts = [[(bt[g] * sc[g][2][r:r + 1]).astype(BF16) for r in heads] for g in gs]
        sts = [[_dot(bts[g][r], x_bf[g][:, psl[r]]) for r in heads] for g in gs]
        if with_y:
            ys = [[_dot(lhs[g][r], jnp.concatenate([x_bf[g][:, psl[r]], h_bf[g][:, psl[r]]], axis=0)) for r in heads] for g in gs]
        for g in gs:
            dc = sc[g][3]
            for p in range(HPG // 2):
                if with_y:
                    y_ref[0, :, g * GW + p * LANES:g * GW + (p + 1) * LANES] = jnp.where(first, ys[g][2 * p], ys[g][2 * p + 1])
                dc_p = jnp.where(first, dc[2 * p:2 * p + 1], dc[2 * p + 1:2 * p + 2])
                h_scr[g, :, psl[2 * p]] = h[g][:, psl[2 * p]] * dc_p + jnp.where(first, sts[g][2 * p], sts[g][2 * p + 1])

        @pl.when(s == nc - 1)
        def _():
            hf_ref[0] = h_scr[...]

    nb = D_INNER // (GPS * D_STATE)
    in_specs = [
        pl.BlockSpec((1, GPS * HPG, CHUNK), lambda b, g, s: (b, blk0 + g, chunk_of(s))),
        pl.BlockSpec((GPS, HPG, 1), lambda b, g, s: (blk0 + g, 0, 0)),
        pl.BlockSpec((GPS, HPG, 1), lambda b, g, s: (blk0 + g, 0, 0)),
        pl.BlockSpec((1, CHUNK, GPS * GW), lambda b, g, s: (b, chunk_of(s), g)),
        pl.BlockSpec((1, CHUNK, GPS * D_STATE), lambda b, g, s: (b, chunk_of(s), nb + g)),
        pl.BlockSpec((1, CHUNK, GPS * D_STATE), lambda b, g, s: (b, chunk_of(s), nb + N_BC // GPS + g)),
        pl.BlockSpec((1, GPS, D_STATE, GW), lambda b, g, s: (b, g, 0, 0)),
    ]
    out_shape, out_specs = [], []
    if with_y:
        out_shape.append(jax.ShapeDtypeStruct((Bn, L, D_INNER), F32))
        out_specs.append(pl.BlockSpec((1, CHUNK, GPS * GW), lambda b, g, s: (b, chunk_of(s), g)))
    out_shape += [jax.ShapeDtypeStruct((Bn, N_BC, nc, D_STATE, GW), F32), jax.ShapeDtypeStruct((Bn, N_BC, D_STATE, GW), F32)]
    out_specs += [pl.BlockSpec((1, GPS, 1, D_STATE, GW), lambda b, g, s: (b, g, chunk_of(s), 0, 0)),
                  pl.BlockSpec((1, GPS, D_STATE, GW), lambda b, g, s: (b, g, 0, 0))]
    return pl.pallas_call(
        kern, name=name, out_shape=out_shape, grid=(Bn, N_BC // GPS, nc), in_specs=in_specs, out_specs=out_specs,
        scratch_shapes=[pltpu.VMEM((GPS, D_STATE, GW), F32)],
        compiler_params=_cp(("arbitrary", "arbitrary", "arbitrary")))(dtT, bias, alog, xbc, xbc, xbc, h0)


def ssd_bwd3(name, dtT, bias, alog, xbc, h_start, dy, dh_final, direction):
    Bn, L = xbc.shape[:2]
    nc = L // CHUNK
    reverse = direction == 1
    blk0 = direction * (N_BC // GPS)
    has_y = dy is not None
    last = 0 if reverse else CHUNK - 1
    gs = range(GPS)

    def chunk_of(s):
        return s if reverse else (nc - 1 - s)

    def kern(*refs):
        if has_y:
            (dt_ref, bias_ref, alog_ref, x_ref, b_ref, hs_ref, dhf_ref, c_ref, dy_ref,
             dx_ref, db_ref, ddt_ref, dbias_ref, dalog_ref, dh0_ref, dc_ref, dh_scr) = refs
        else:
            (dt_ref, bias_ref, alog_ref, x_ref, b_ref, hs_ref, dhf_ref,
             dx_ref, db_ref, ddt_ref, dbias_ref, dalog_ref, dh0_ref, dh_scr) = refs
        s = pl.program_id(2)

        @pl.when(s == 0)
        def _():
            dh_scr[...] = dhf_ref[0]
            dbias_ref[...] = jnp.zeros(dbias_ref.shape, F32)
            dalog_ref[...] = jnp.zeros(dalog_ref.shape, F32)

        first = lax.broadcasted_iota(jnp.int32, (1, LANES), 1) < HEAD_DIM
        heads = range(HPG)
        psl = [slice((r // 2) * LANES, (r // 2 + 1) * LANES) for r in heads]
        mine = [first if r % 2 == 0 else jnp.logical_not(first) for r in heads]
        zeros_bf = jnp.zeros((CHUNK, LANES), BF16)
        keep = _tri_mask(True, reverse)
        ctx = []
        for g in gs:
            dtraw = dt_ref[0, g * HPG:(g + 1) * HPG]
            dt, A, cs, total = _ssd_scalars(dtraw, bias_ref[g], alog_ref[g], reverse)
            u = cs - jnp.log(dt)
            c = dict(dtraw=dtraw, dt=dt, A=A, cs=cs, total=total, u=u, dtt=jnp.exp(total - u), dcy=jnp.exp(total),
                     u_cols=_rows_to_cols(u), x_bf=x_ref[0, :, g * GW:(g + 1) * GW].astype(BF16),
                     bm=b_ref[0, :, g * D_STATE:(g + 1) * D_STATE], h=hs_ref[0, g, 0], dh=dh_scr[g])
            c["bt"] = c["bm"].T
            c["dh_bf"] = c["dh"].astype(BF16)
            if has_y:
                c["cm"] = c_ref[0, :, g * D_STATE:(g + 1) * D_STATE]
                c["ct"] = c["cm"].T
                c["e_row"] = jnp.exp(cs)
                c["dy_bf"] = dy_ref[0, :, g * GW:(g + 1) * GW].astype(BF16)
                c["h_bf"] = c["h"].astype(BF16)
                c["cbt"] = _dot_nt(c["bm"].astype(BF16), c["cm"].astype(BF16))
            ctx.append(c)
        for c in ctx:
            c["lhs"], c["et"] = [], []
            for r in heads:
                u_col = jnp.broadcast_to(c["u_cols"][:, r:r + 1], (CHUNK, LANES))
                bs = (c["bm"] * jnp.exp(c["total"][r:r + 1] - u_col)).astype(BF16)
                if has_y:
                    et = jnp.exp(jnp.where(keep, c["cs"][r:r + 1] - u_col, -jnp.inf))
                    c["et"].append(et)
                    c["lhs"].append(jnp.concatenate([(c["cbt"] * et).astype(BF16), bs], axis=1))
                else:
                    c["lhs"].append(bs)
        for c in ctx:
            c["p2raw"] = [_dot_nt(c["dh_bf"][:, psl[r]], jnp.where(mine[r], c["x_bf"][:, psl[r]], zeros_bf)) for r in heads]
            if has_y:
                c["a1"] = [_dot_nt(jnp.concatenate([c["x_bf"][:, psl[r]], c["h_bf"][:, psl[r]]], axis=0),
                                   jnp.where(mine[r], c["dy_bf"][:, psl[r]], zeros_bf)) for r in heads]
                c["news"] = [_dot((c["ct"] * c["e_row"][r:r + 1]).astype(BF16), c["dy_bf"][:, psl[r]]) for r in heads]
                c["dxs"] = [_dot(c["lhs"][r], jnp.concatenate([c["dy_bf"][:, psl[r]], c["dh_bf"][:, psl[r]]], axis=0)) for r in heads]
            else:
                c["dxs"] = [_dot(c["lhs"][r], c["dh_bf"][:, psl[r]]) for r in heads]
        for g, c in enumerate(ctx):
            dbt = jnp.zeros((D_STATE, CHUNK), F32)
            dcbt = jnp.zeros((CHUNK, CHUNK), F32)
            dct = jnp.zeros((D_STATE, CHUNK), F32)
            tots, out_rows, in_rows, in_cols = [], [], [], []
            for r in heads:
                if has_y:
                    pt = c["a1"][r][0:CHUNK] * c["et"][r]
                    dcbt = dcbt + pt
                    mt = pt * c["cbt"]
                    ph = c["a1"][r][CHUNK:] * c["e_row"][r:r + 1]
                    dct = dct + ph
                    out_rows.append(_colsum(mt + c["ct"] * ph))
                    in_cols.append(jnp.sum(mt, axis=1, keepdims=True))
                p2 = c["p2raw"][r] * c["dtt"][r:r + 1]
                dbt = dbt + p2
                t_term = _colsum(c["bt"] * p2)
                in_rows.append(t_term)
                hdh = c["h"][:, psl[r]] * c["dh"][:, psl[r]]
                tot = jnp.sum(t_term, axis=1, keepdims=True) + c["dcy"][r:r + 1] * jnp.sum(jnp.where(mine[r], hdh, 0.0), keepdims=True)
                tots.append(jnp.broadcast_to(tot, (1, CHUNK)))
            for p in range(HPG // 2):
                dx_ref[0, :, g * GW + p * LANES:g * GW + (p + 1) * LANES] = jnp.where(first, c["dxs"][2 * p], c["dxs"][2 * p + 1])
                new = c["dh"][:, psl[2 * p]] * jnp.where(first, c["dcy"][2 * p:2 * p + 1], c["dcy"][2 * p + 1:2 * p + 2])
                if has_y:
                    new = new + jnp.where(first, c["news"][2 * p], c["news"][2 * p + 1])
                dh_scr[g, :, psl[2 * p]] = new
            db = dbt.T
            if has_y:
                dcbt_bf = dcbt.astype(BF16)
                db = db + _dot(dcbt_bf, c["cm"].astype(BF16))
                dc_ref[0, :, g * D_STATE:(g + 1) * D_STATE] = dct.T + _dot_tn(dcbt_bf, c["bm"].astype(BF16))
            db_ref[0, :, g * D_STATE:(g + 1) * D_STATE] = db
            s_row = _stack_rows(in_rows)
            lane = lax.broadcasted_iota(jnp.int32, (HPG, CHUNK), 1)
            dcs = jnp.where(lane == last, _stack_rows(tots), 0.0)
            if has_y:
                s_row = s_row + _cols_to_rows(in_cols)
                dcs = dcs + _stack_rows(out_rows)
            dcs = dcs - s_row
            da = _cumsum_lanes(dcs, not reverse)
            ddt = da * c["A"] + jnp.where(c["dt"] > 0.0, s_row / c["dt"], 0.0)
            ddtraw = ddt * _sigmoid(c["dtraw"] + bias_ref[g])
            ddt_ref[0, g * HPG:(g + 1) * HPG] = ddtraw
            dbias_ref[0, g] += jnp.sum(ddtraw, axis=1, keepdims=True)
            dalog_ref[0, g] += jnp.sum(da * c["dt"], axis=1, keepdims=True) * c["A"]

        @pl.when(s == nc - 1)
        def _():
            dh0_ref[0] = dh_scr[...]

    nb = D_INNER // (GPS * D_STATE)
    cidx = lambda b, g, s: (b, chunk_of(s), g)
    hidx = lambda b, g, s: (b, g, 0, 0)
    in_specs = [
        pl.BlockSpec((1, GPS * HPG, CHUNK), lambda b, g, s: (b, blk0 + g, chunk_of(s))),
        pl.BlockSpec((GPS, HPG, 1), lambda b, g, s: (blk0 + g, 0, 0)),
        pl.BlockSpec((GPS, HPG, 1), lambda b, g, s: (blk0 + g, 0, 0)),
        pl.BlockSpec((1, CHUNK, GPS * GW), cidx),
        pl.BlockSpec((1, CHUNK, GPS * D_STATE), lambda b, g, s: (b, chunk_of(s), nb + g)),
        pl.BlockSpec((1, GPS, 1, D_STATE, GW), lambda b, g, s: (b, g, chunk_of(s), 0, 0)),
        pl.BlockSpec((1, GPS, D_STATE, GW), hidx),
    ]
    args = [dtT, bias, alog, xbc, xbc, h_start, dh_final]
    if has_y:
        in_specs += [pl.BlockSpec((1, CHUNK, GPS * D_STATE), lambda b, g, s: (b, chunk_of(s), nb + N_BC // GPS + g)),
                     pl.BlockSpec((1, CHUNK, GPS * GW), cidx)]
        args += [xbc, dy]
    out_shape = [jax.ShapeDtypeStruct((Bn, L, D_INNER), F32), jax.ShapeDtypeStruct((Bn, L, N_BC * D_STATE), F32),
                 jax.ShapeDtypeStruct((Bn, N_HEADS, L), F32), jax.ShapeDtypeStruct((Bn, N_BC, HPG, 1), F32),
                 jax.ShapeDtypeStruct((Bn, N_BC, HPG, 1), F32), jax.ShapeDtypeStruct((Bn, N_BC, D_STATE, GW), F32)]
    out_specs = [pl.BlockSpec((1, CHUNK, GPS * GW), cidx), pl.BlockSpec((1, CHUNK, GPS * D_STATE), cidx),
                 pl.BlockSpec((1, GPS * HPG, CHUNK), lambda b, g, s: (b, g, chunk_of(s))),
                 pl.BlockSpec((1, GPS, HPG, 1), hidx), pl.BlockSpec((1, GPS, HPG, 1), hidx), pl.BlockSpec((1, GPS, D_STATE, GW), hidx)]
    if has_y:
        out_shape.append(jax.ShapeDtypeStruct((Bn, L, N_BC * D_STATE), F32))
        out_specs.append(pl.BlockSpec((1, CHUNK, GPS * D_STATE), cidx))
    res = pl.pallas_call(
        kern, name=name, out_shape=out_shape, grid=(Bn, N_BC // GPS, nc), in_specs=in_specs, out_specs=out_specs,
        scratch_shapes=[pltpu.VMEM((GPS, D_STATE, GW), F32)],
        compiler_params=_cp(("arbitrary", "arbitrary", "arbitrary")))(*args)
    dxs, db, ddt, dbias, dalog, dh0 = res[:6]
    return dxs, db, (res[6] if has_y else None), ddt, dbias, dalog, dh0


def _dot_split2(v, sel):
    hi = v.astype(BF16)
    mid = (v - hi.astype(F32)).astype(BF16)
    return _dot(hi, sel) + _dot(mid, sel)


def ssd_tables():
    lane = jnp.arange(LANES)[:, None]
    col = jnp.arange(2 * GW)[None, :]
    expand = (lane == jnp.where(col < GW, HPG + col // HEAD_DIM, 2 * HPG + (col - GW) // HEAD_DIM)).astype(BF16)
    ch = jnp.arange(GW)[:, None] // HEAD_DIM
    out = jnp.arange(2 * LANES)[None, :]
    seg = ((out == ch) | (out == LANES + HPG + ch)).astype(BF16)
    return expand, seg


def _dc_lanes(dc, first):
    return jnp.concatenate([jnp.where(first, dc[2 * p:2 * p + 1], dc[2 * p + 1:2 * p + 2]) for p in range(HPG // 2)], axis=1)


def ssd_fwd2(name, dtT, bias, alog, xbc, h0, tables, direction, with_y):
    Bn, L = xbc.shape[:2]
    nc = L // CHUNK
    reverse = direction == 1
    rowblk = direction * N_BC
    expand = tables[0]

    def chunk_of(s):
        return (nc - 1 - s) if reverse else s

    def kern(dt_ref, bias_ref, alog_ref, x_ref, b_ref, c_ref, h0_ref, xp_ref, *rest):
        if with_y:
            y_ref, hs_ref, hf_ref, h_scr = rest
        else:
            hs_ref, hf_ref, h_scr = rest
        s = pl.program_id(2)

        @pl.when(s == 0)
        def _():
            h_scr[...] = h0_ref[0, 0]

        dt, _, cs, total = _ssd_scalars(dt_ref[0], bias_ref[0], alog_ref[0], reverse)
        u = cs - jnp.log(dt)
        dtt = jnp.exp(total - u)
        cols = _rows_to_cols(jnp.concatenate([cs, dtt, jnp.exp(cs)], axis=0))
        wide = _dot_split2(cols, xp_ref[...])
        dtt_x, e_x = wide[:, 0:GW], wide[:, GW:]
        first = lax.broadcasted_iota(jnp.int32, (1, LANES), 1) < HEAD_DIM
        x = x_ref[0]
        x_bf = x.astype(BF16)
        bm = b_ref[0]
        h = h_scr[...]
        hs_ref[0, 0, 0] = h
        st = _dot(bm.T.astype(BF16), (x * dtt_x).astype(BF16))
        h_scr[...] = h * _dc_lanes(jnp.exp(total), first) + st
        if with_y:
            cm = c_ref[0].astype(BF16)
            cb = _dot_nt(cm, bm.astype(BF16))
            yoff = _dot(cm, h.astype(BF16)) * e_x
            keep = _tri_mask(False, reverse)
            wfs = []
            for r in range(HPG):
                cs_col = jnp.broadcast_to(cols[:, r:r + 1], (CHUNK, LANES))
                wfs.append((cb * jnp.exp(jnp.where(keep, cs_col - u[r:r + 1], -jnp.inf))).astype(BF16))
            yd = [_dot(wfs[r], x_bf[:, (r // 2) * LANES:(r // 2 + 1) * LANES]) for r in range(HPG)]
            for p in range(HPG // 2):
                psl = slice(p * LANES, (p + 1) * LANES)
                y_ref[0, :, psl] = jnp.where(first, yd[2 * p], yd[2 * p + 1]) + yoff[:, psl]

        @pl.when(s == nc - 1)
        def _():
            hf_ref[0, 0] = h_scr[...]

    in_specs = [
        pl.BlockSpec((1, HPG, CHUNK), lambda b, g, s: (b, rowblk + g, chunk_of(s))),
        pl.BlockSpec((1, HPG, 1), lambda b, g, s: (rowblk + g, 0, 0)),
        pl.BlockSpec((1, HPG, 1), lambda b, g, s: (rowblk + g, 0, 0)),
        pl.BlockSpec((1, CHUNK, GW), lambda b, g, s: (b, chunk_of(s), g)),
        pl.BlockSpec((1, CHUNK, D_STATE), lambda b, g, s: (b, chunk_of(s), D_INNER // D_STATE + g)),
        pl.BlockSpec((1, CHUNK, D_STATE), lambda b, g, s: (b, chunk_of(s), D_INNER // D_STATE + N_BC + g)),
        pl.BlockSpec((1, 1, D_STATE, GW), lambda b, g, s: (b, g, 0, 0)),
        pl.BlockSpec(expand.shape, lambda b, g, s: (0, 0)),
    ]
    out_shape, out_specs = [], []
    if with_y:
        out_shape.append(jax.ShapeDtypeStruct((Bn, L, D_INNER), F32))
        out_specs.append(pl.BlockSpec((1, CHUNK, GW), lambda b, g, s: (b, chunk_of(s), g)))
    out_shape += [jax.ShapeDtypeStruct((Bn, N_BC, nc, D_STATE, GW), F32), jax.ShapeDtypeStruct((Bn, N_BC, D_STATE, GW), F32)]
    out_specs += [pl.BlockSpec((1, 1, 1, D_STATE, GW), lambda b, g, s: (b, g, chunk_of(s), 0, 0)),
                  pl.BlockSpec((1, 1, D_STATE, GW), lambda b, g, s: (b, g, 0, 0))]
    return pl.pallas_call(
        kern, name=name, out_shape=out_shape, grid=(Bn, N_BC, nc), in_specs=in_specs, out_specs=out_specs,
        scratch_shapes=[pltpu.VMEM((D_STATE, GW), F32)],
        compiler_params=_cp(("arbitrary", "arbitrary", "arbitrary")))(dtT, bias, alog, xbc, xbc, xbc, h0, expand)


def ssd_bwd2(name, dtT, bias, alog, xbc, h_start, dy, dh_final, tables, direction):
    Bn, L = xbc.shape[:2]
    nc = L // CHUNK
    reverse = direction == 1
    rowblk = direction * N_BC
    has_y = dy is not None
    last = 0 if reverse else CHUNK - 1
    expand, seg = tables

    def chunk_of(s):
        return s if reverse else (nc - 1 - s)

    def kern(*refs):
        if has_y:
            (dt_ref, bias_ref, alog_ref, x_ref, b_ref, hs_ref, dhf_ref, xp_ref, seg_ref, c_ref, dy_ref,
             dx_ref, db_ref, ddt_ref, dbias_ref, dalog_ref, dh0_ref, dc_ref, dh_scr) = refs
        else:
            (dt_ref, bias_ref, alog_ref, x_ref, b_ref, hs_ref, dhf_ref, xp_ref, seg_ref,
             dx_ref, db_ref, ddt_ref, dbias_ref, dalog_ref, dh0_ref, dh_scr) = refs
        s = pl.program_id(2)

        @pl.when(s == 0)
        def _():
            dh_scr[...] = dhf_ref[0, 0]
            dbias_ref[...] = jnp.zeros(dbias_ref.shape, F32)
            dalog_ref[...] = jnp.zeros(dalog_ref.shape, F32)

        first = lax.broadcasted_iota(jnp.int32, (1, LANES), 1) < HEAD_DIM
        heads = range(HPG)
        psl = [slice((r // 2) * LANES, (r // 2 + 1) * LANES) for r in heads]
        x = x_ref[0]
        bm = b_ref[0].astype(BF16)
        h = hs_ref[0, 0, 0]
        dh = dh_scr[...]
        dh_bf = dh.astype(BF16)
        bdh = _dot(bm, dh_bf)
        if has_y:
            cm = c_ref[0].astype(BF16)
            dyv = dy_ref[0]
            dy_bf = dyv.astype(BF16)
            x_bf = x.astype(BF16)
            h_bf = h.astype(BF16)
            cbt = _dot_nt(bm, cm)
            ch = _dot(cm, h_bf)
            zeros_bf = jnp.zeros((CHUNK, LANES), BF16)
            gts = [_dot_nt(x_bf[:, psl[r]], jnp.where(first if r % 2 == 0 else jnp.logical_not(first), dy_bf[:, psl[r]], zeros_bf))
                   for r in heads]
            ct_bf = c_ref[0].T.astype(BF16)
        dtraw = dt_ref[0]
        dt, A, cs, total = _ssd_scalars(dtraw, bias_ref[0], alog_ref[0], reverse)
        u = cs - jnp.log(dt)
        dtt = jnp.exp(total - u)
        dcy = jnp.exp(total)
        cols = _rows_to_cols(jnp.concatenate([u, dtt, jnp.exp(cs)], axis=0))
        wide = _dot_split2(cols, xp_ref[...])
        dtt_x, e_x = wide[:, 0:GW], wide[:, GW:]
        term2 = bdh * dtt_x
        dbt = _dot_nt(dh_bf, (x * dtt_x).astype(BF16))
        sums = _dot_split2(term2 * x, seg_ref[:, LANES:])
        new_dh = dh * _dc_lanes(dcy, first)
        if has_y:
            dye = dyv * e_x
            dye_bf = dye.astype(BF16)
            dct = _dot_nt(h_bf, dye_bf)
            new_dh = new_dh + _dot(ct_bf, dye_bf)
            sums = sums + _dot_split2(ch * dye, seg_ref[:, 0:LANES])
            keep = _tri_mask(True, reverse)
            ets = []
            for r in heads:
                u_col = jnp.broadcast_to(cols[:, r:r + 1], (CHUNK, LANES))
                ets.append(jnp.exp(jnp.where(keep, cs[r:r + 1] - u_col, -jnp.inf)))
            wts = [(cbt * ets[r]).astype(BF16) for r in heads]
            dxd = [_dot(wts[r], dy_bf[:, psl[r]]) for r in heads]
            dcbt = jnp.zeros((CHUNK, CHUNK), F32)
            out_rows, in_cols = [], []
            for r in heads:
                pt = gts[r] * ets[r]
                dcbt = dcbt + pt
                mt = pt * cbt
                out_rows.append(_colsum(mt))
                in_cols.append(jnp.sum(mt, axis=1, keepdims=True))
            for p in range(HPG // 2):
                dx_ref[0, :, psl[2 * p]] = jnp.where(first, dxd[2 * p], dxd[2 * p + 1]) + term2[:, psl[2 * p]]
            dcbt_bf = dcbt.astype(BF16)
            db_ref[0] = dbt.T + _dot(dcbt_bf, cm)
            dc_ref[0] = dct.T + _dot_tn(dcbt_bf, bm)
        else:
            dx_ref[0] = term2
            db_ref[0] = dbt.T
        dh_scr[...] = new_dh
        sums_t = sums.T
        s_row = sums_t[HPG:2 * HPG]
        hdh = _colsum(h * dh)
        lanes_w = lax.broadcasted_iota(jnp.int32, (1, GW), 1)
        hd = _stack_rows([jnp.sum(jnp.where(lanes_w // HEAD_DIM == r, hdh, 0.0), axis=1, keepdims=True) for r in range(HPG)])
        tot = jnp.sum(s_row, axis=1, keepdims=True) + dcy * hd
        lane = lax.broadcasted_iota(jnp.int32, (HPG, CHUNK), 1)
        dcs = jnp.where(lane == last, tot, 0.0)
        if has_y:
            s_row = s_row + _cols_to_rows(in_cols)
            dcs = dcs + _stack_rows(out_rows) + sums_t[0:HPG]
        dcs = dcs - s_row
        da = _cumsum_lanes(dcs, not reverse)
        ddt = da * A + jnp.where(dt > 0.0, s_row / dt, 0.0)
        ddtraw = ddt * _sigmoid(dtraw + bias_ref[0])
        ddt_ref[0] = ddtraw
        dbias_ref[0, 0] += jnp.sum(ddtraw, axis=1, keepdims=True)
        dalog_ref[0, 0] += jnp.sum(da * dt, axis=1, keepdims=True) * A

        @pl.when(s == nc - 1)
        def _():
            dh0_ref[0, 0] = dh_scr[...]

    cidx = lambda b, g, s: (b, chunk_of(s), g)
    hidx = lambda b, g, s: (b, g, 0, 0)
    in_specs = [
        pl.BlockSpec((1, HPG, CHUNK), lambda b, g, s: (b, rowblk + g, chunk_of(s))),
        pl.BlockSpec((1, HPG, 1), lambda b, g, s: (rowblk + g, 0, 0)),
        pl.BlockSpec((1, HPG, 1), lambda b, g, s: (rowblk + g, 0, 0)),
        pl.BlockSpec((1, CHUNK, GW), cidx),
        pl.BlockSpec((1, CHUNK, D_STATE), lambda b, g, s: (b, chunk_of(s), D_INNER // D_STATE + g)),
        pl.BlockSpec((1, 1, 1, D_STATE, GW), lambda b, g, s: (b, g, chunk_of(s), 0, 0)),
        pl.BlockSpec((1, 1, D_STATE, GW), hidx),
        pl.BlockSpec(expand.shape, lambda b, g, s: (0, 0)),
        pl.BlockSpec(seg.shape, lambda b, g, s: (0, 0)),
    ]
    args = [dtT, bias, alog, xbc, xbc, h_start, dh_final, expand, seg]
    if has_y:
        in_specs += [pl.BlockSpec((1, CHUNK, D_STATE), lambda b, g, s: (b, chunk_of(s), D_INNER // D_STATE + N_BC + g)),
                     pl.BlockSpec((1, CHUNK, GW), cidx)]
        args += [xbc, dy]
    out_shape = [jax.ShapeDtypeStruct((Bn, L, D_INNER), F32), jax.ShapeDtypeStruct((Bn, L, N_BC * D_STATE), F32),
                 jax.ShapeDtypeStruct((Bn, N_HEADS, L), F32), jax.ShapeDtypeStruct((Bn, N_BC, HPG, 1), F32),
                 jax.ShapeDtypeStruct((Bn, N_BC, HPG, 1), F32), jax.ShapeDtypeStruct((Bn, N_BC, D_STATE, GW), F32)]
    out_specs = [pl.BlockSpec((1, CHUNK, GW), cidx), pl.BlockSpec((1, CHUNK, D_STATE), cidx),
                 pl.BlockSpec((1, HPG, CHUNK), lambda b, g, s: (b, g, chunk_of(s))),
                 pl.BlockSpec((1, 1, HPG, 1), hidx), pl.BlockSpec((1, 1, HPG, 1), hidx), pl.BlockSpec((1, 1, D_STATE, GW), hidx)]
    if has_y:
        out_shape.append(jax.ShapeDtypeStruct((Bn, L, N_BC * D_STATE), F32))
        out_specs.append(pl.BlockSpec((1, CHUNK, D_STATE), cidx))
    res = pl.pallas_call(
        kern, name=name, out_shape=out_shape, grid=(Bn, N_BC, nc), in_specs=in_specs, out_specs=out_specs,
        scratch_shapes=[pltpu.VMEM((D_STATE, GW), F32)],
        compiler_params=_cp(("arbitrary", "arbitrary", "arbitrary")))(*args)
    dxs, db, ddt, dbias, dalog, dh0 = res[:6]
    return dxs, db, (res[6] if has_y else None), ddt, dbias, dalog, dh0


def _group_mean(v):
    gw = D_INNER // N_BC
    parts = [jnp.broadcast_to(jnp.mean(v[:, g * gw:(g + 1) * gw], axis=-1, keepdims=True), (v.shape[0], gw)) for g in range(N_BC)]
    return jnp.concatenate(parts, axis=1)


def gated_norm_fwd(name, y_f, y_b, xs_src, z, dskip_lanes, w_norm):
    def body(yf, yb, xs, z, dsk, w):
        u = (yf + yb + dsk * xs) * _silu(z)
        r = lax.rsqrt(_group_mean(u * u) + NORM_EPS)
        return u * r * w

    return tok_call(name, body, [y_f, y_b, xs_src, z], [], [dskip_lanes, w_norm], [(D_INNER, BF16)], [], [])[0]


def _dot_exact01(v, sel):
    hi, mid, lo = _split3(v)
    return _dot(hi, sel) + _dot(mid, sel) + _dot(lo, sel)


def gated_norm_bwd(name, y_f, y_b, xs_src, z, d_out, dskip_lanes, w_norm, head_sel):
    def body(yf, yb, xs, z, do, dsk, w, sel):
        y = yf + yb + dsk * xs
        sz = _silu(z)
        u = y * sz
        r = lax.rsqrt(_group_mean(u * u) + NORM_EPS)
        duh = do * w
        du = r * (duh - u * (r * r) * _group_mean(duh * u))
        dy = du * sz
        dz = du * y * _dsilu(z)
        dsk_heads = _dot_exact01(jnp.broadcast_to(_colsum(dy * xs), (8, D_INNER)), sel)
        return dy, dz, _colsum(do * u * r), dsk_heads

    return tok_call(name, body, [y_f, y_b, xs_src, z, d_out], [], [dskip_lanes, w_norm, head_sel],
                    [(D_INNER, F32), (D_INNER, BF16)], [], [(1, D_INNER), (8, LANES)], tm=128)


def merge_fwd(name, y_pool, y_ssd, gatepre, x, target, gate, b_merge, norm_post, w_pp, w_ps, w_out):
    def body(yp, ys, gp, x, tgt, gate, bm, wpost, w_pp, w_ps, w_out):
        p1 = _dot(yp, w_pp)
        p2 = _dot(ys, w_ps)
        gates = _sigmoid(gp + bm)
        merged = gates[:, :D] * p1 + gates[:, D:] * p2
        out = _dot(merged.astype(BF16), w_out)
        r = _rms_r(out)
        outr = out * r
        nq = outr * wpost
        err = x + gate * nq - tgt
        loss = 0.5 * jnp.sum(jnp.mean(err * err, axis=-1, keepdims=True), keepdims=True).reshape(1, 1)
        g = err * (1.0 / D)
        dnq = g * gate
        dout = _rms_bwd(dnq * wpost, out, r)
        return merged, p1, p2, dout, g, _colsum(g * nq), _colsum(dnq * outr), jnp.broadcast_to(loss, (1, LANES))

    return tok_call(name, body, [y_pool, y_ssd, gatepre, x, target], [gate], [b_merge, norm_post, w_pp, w_ps, w_out],
                    [(D, BF16), (D, F32), (D, F32), (D, BF16), (D, F32)], [D], [(1, D), (1, LANES)])


def merge_bwd(name, dout, gatepre, p1, p2, b_merge, w_pp, w_ps, w_out):
    def body(dout, gp, p1, p2, bm, w_pp, w_ps, w_out):
        dmerged = _dot_nt(dout, w_out)
        gates = _sigmoid(gp + bm)
        g1, g2 = gates[:, :D], gates[:, D:]
        dp1 = (dmerged * g1).astype(BF16)
        dp2 = (dmerged * g2).astype(BF16)
        dgp = jnp.concatenate([dmerged * p1 * g1 * (1.0 - g1), dmerged * p2 * g2 * (1.0 - g2)], axis=1)
        return dp1, dp2, dgp, _dot_nt(dp1, w_pp), _dot_nt(dp2, w_ps), _colsum(dgp)

    return tok_call(name, body, [dout, gatepre, p1, p2], [], [b_merge, w_pp, w_ps, w_out],
                    [(D, BF16), (D, BF16), (2 * D, BF16), (D, F32), (D_INNER, F32)], [], [(1, 2 * D)])


def _adamw_math(w, g, m, v):
    m = ADAM_B1 * m + (1.0 - ADAM_B1) * g
    v = ADAM_B2 * v + (1.0 - ADAM_B2) * (g * g)
    m_hat = m / (1.0 - ADAM_B1 ** ADAM_STEP)
    v_hat = v / (1.0 - ADAM_B2 ** ADAM_STEP)
    delta = -ADAM_LR * (m_hat / (jnp.sqrt(v_hat) + ADAM_EPS) + ADAM_WD * w)
    return delta, m, v


def adamw(name, w, g, m, v, tr=256):
    R, C = w.shape
    tr = min(tr, R)
    assert R % tr == 0

    def body(w_ref, g_ref, m_ref, v_ref, d_ref, nm_ref, nv_ref):
        d, nm, nv = _adamw_math(w_ref[...], g_ref[...], m_ref[...], v_ref[...])
        d_ref[...] = d
        nm_ref[...] = nm
        nv_ref[...] = nv

    spec = pl.BlockSpec((tr, C), lambda i: (i, 0))
    return pl.pallas_call(
        body, name=name, out_shape=[jax.ShapeDtypeStruct((R, C), F32)] * 3, grid=(R // tr,),
        in_specs=[spec] * 4, out_specs=[spec] * 3, compiler_params=_cp(("parallel",)))(w, g, m, v)


def _me():
    return lax.axis_index("x"), lax.axis_index("y"), lax.axis_index("c")


def all_gather_small(name, v):
    R, C = v.shape

    def body(v_ref, out_ref, send_sems, recv_sems, local_sem):
        x, y, c = _me()
        me = 4 * x + 2 * y + c
        mine = pltpu.make_async_copy(v_ref, out_ref.at[me], local_sem)
        mine.start()
        copies = []
        for d in range(1, N_DEV):
            dx, dy, dc = d // 4, (d // 2) % 2, d % 2
            px, py, pc = x ^ dx, y ^ dy, c ^ dc
            copies.append(pltpu.make_async_remote_copy(
                src_ref=v_ref, dst_ref=out_ref.at[me], send_sem=send_sems.at[d - 1], recv_sem=recv_sems.at[d - 1],
                device_id=(px, py, pc), device_id_type=MESH))
        for cp in copies:
            cp.start()
        for d in range(1, N_DEV):
            dx, dy, dc = d // 4, (d // 2) % 2, d % 2
            peer = 4 * (x ^ dx) + 2 * (y ^ dy) + (c ^ dc)
            pltpu.make_async_remote_copy(
                src_ref=v_ref, dst_ref=out_ref.at[peer], send_sem=send_sems.at[d - 1], recv_sem=recv_sems.at[d - 1],
                device_id=(x ^ dx, y ^ dy, c ^ dc), device_id_type=MESH).wait_recv()
        for cp in copies:
            cp.wait_send()
        mine.wait()

    return pl.pallas_call(
        body, name=name, out_shape=jax.ShapeDtypeStruct((N_DEV, R, C), F32),
        in_specs=[pl.BlockSpec(memory_space=pltpu.VMEM)], out_specs=pl.BlockSpec(memory_space=pltpu.VMEM),
        scratch_shapes=[pltpu.SemaphoreType.DMA((N_DEV - 1,)), pltpu.SemaphoreType.DMA((N_DEV - 1,)), pltpu.SemaphoreType.DMA],
        compiler_params=pltpu.CompilerParams(vmem_limit_bytes=VMEM_LIMIT))(v)


def all_gather_chips(name, shard):
    R, C = shard.shape
    half = R // 2
    assert R % 32 == 0

    def body(s_ref, out_ref, send_sems, recv_sems):
        x, y, c = _me()
        chips = [(1 - x, y), (x, 1 - y), (1 - x, 1 - y)]

        def rows(chip, hc):
            return out_ref.at[2 * chip[0] + chip[1], pl.ds(hc * half, half), :]

        first = [pltpu.make_async_remote_copy(
            src_ref=s_ref.at[pl.ds(c * half, half), :], dst_ref=rows((x, y), c), send_sem=send_sems.at[j],
            recv_sem=recv_sems.at[j], device_id=(*chip, c), device_id_type=MESH) for j, chip in enumerate(chips)]
        for cp in first:
            cp.start()
        passed = [pltpu.make_async_remote_copy(
            src_ref=rows(chip, c), dst_ref=rows(chip, c), send_sem=send_sems.at[3 + j], recv_sem=recv_sems.at[3 + j],
            device_id=(x, y, 1 - c), device_id_type=MESH) for j, chip in enumerate(chips)]
        for j, chip in enumerate(chips):
            pltpu.make_async_remote_copy(
                src_ref=rows(chip, c), dst_ref=rows(chip, c), send_sem=send_sems.at[j], recv_sem=recv_sems.at[j],
                device_id=(*chip, c), device_id_type=MESH).wait_recv()
            passed[j].start()
        for j, chip in enumerate(chips):
            pltpu.make_async_remote_copy(
                src_ref=rows(chip, 1 - c), dst_ref=rows(chip, 1 - c), send_sem=send_sems.at[3 + j], recv_sem=recv_sems.at[3 + j],
                device_id=(x, y, 1 - c), device_id_type=MESH).wait_recv()
        for cp in first + passed:
            cp.wait_send()

    out = pl.pallas_call(
        body, name=name, out_shape=jax.ShapeDtypeStruct((N_CHIPS, R, C), shard.dtype),
        in_specs=[pl.BlockSpec(memory_space=pl.ANY)], out_specs=pl.BlockSpec(memory_space=pl.ANY),
        scratch_shapes=[pltpu.SemaphoreType.DMA((6,)), pltpu.SemaphoreType.DMA((6,))],
        compiler_params=pltpu.CompilerParams(vmem_limit_bytes=VMEM_LIMIT))(shard)
    chip = 2 * lax.axis_index("x") + lax.axis_index("y")
    return lax.dynamic_update_index_in_dim(out, shard, chip, 0)


def sibling_swap(name, v):
    def body(v_ref, out_ref, send_sem, recv_sem):
        x, y, c = _me()
        cp = pltpu.make_async_remote_copy(src_ref=v_ref, dst_ref=out_ref, send_sem=send_sem, recv_sem=recv_sem,
                                          device_id=(x, y, 1 - c), device_id_type=MESH)
        cp.start()
        cp.wait()

    return pl.pallas_call(
        body, name=name, out_shape=jax.ShapeDtypeStruct(v.shape, v.dtype),
        in_specs=[pl.BlockSpec(memory_space=pl.ANY)], out_specs=pl.BlockSpec(memory_space=pl.ANY),
        scratch_shapes=[pltpu.SemaphoreType.DMA, pltpu.SemaphoreType.DMA],
        compiler_params=pltpu.CompilerParams(vmem_limit_bytes=VMEM_LIMIT))(v)


def sibling_share(name, v):
    def body(v_ref, out_ref, send_sem, recv_sem, local_sem):
        x, y, c = _me()
        mine = pltpu.make_async_copy(v_ref, out_ref.at[c], local_sem)
        mine.start()
        cp = pltpu.make_async_remote_copy(src_ref=v_ref, dst_ref=out_ref.at[c], send_sem=send_sem, recv_sem=recv_sem,
                                          device_id=(x, y, 1 - c), device_id_type=MESH)
        cp.start()
        pltpu.make_async_remote_copy(src_ref=v_ref, dst_ref=out_ref.at[1 - c], send_sem=send_sem, recv_sem=recv_sem,
                                     device_id=(x, y, 1 - c), device_id_type=MESH).wait_recv()
        cp.wait_send()
        mine.wait()

    return pl.pallas_call(
        body, name=name, out_shape=jax.ShapeDtypeStruct((2, *v.shape), v.dtype),
        in_specs=[pl.BlockSpec(memory_space=pl.ANY)], out_specs=pl.BlockSpec(memory_space=pl.ANY),
        scratch_shapes=[pltpu.SemaphoreType.DMA, pltpu.SemaphoreType.DMA, pltpu.SemaphoreType.DMA],
        compiler_params=pltpu.CompilerParams(vmem_limit_bytes=VMEM_LIMIT))(v)


def chip_exchange(name, parts):
    def body(p_ref, out_ref, send_sems, recv_sems):
        x, y, c = _me()
        k = 2 * x + y
        chips = [(1 - x, y), (x, 1 - y), (1 - x, 1 - y)]
        sends = [pltpu.make_async_remote_copy(
            src_ref=p_ref.at[2 * chip[0] + chip[1]], dst_ref=out_ref.at[k], send_sem=send_sems.at[j], recv_sem=recv_sems.at[j],
            device_id=(*chip, c), device_id_type=MESH) for j, chip in enumerate(chips)]
        for cp in sends:
            cp.start()
        for j, chip in enumerate(chips):
            pltpu.make_async_remote_copy(
                src_ref=p_ref.at[k], dst_ref=out_ref.at[2 * chip[0] + chip[1]], send_sem=send_sems.at[j], recv_sem=recv_sems.at[j],
                device_id=(*chip, c), device_id_type=MESH).wait_recv()
        for cp in sends:
            cp.wait_send()

    out = pl.pallas_call(
        body, name=name, out_shape=jax.ShapeDtypeStruct(parts.shape, parts.dtype),
        in_specs=[pl.BlockSpec(memory_space=pl.ANY)], out_specs=pl.BlockSpec(memory_space=pl.ANY),
        scratch_shapes=[pltpu.SemaphoreType.DMA((3,)), pltpu.SemaphoreType.DMA((3,))],
        compiler_params=pltpu.CompilerParams(vmem_limit_bytes=VMEM_LIMIT))(parts)
    chip = 2 * lax.axis_index("x") + lax.axis_index("y")
    own = lax.dynamic_index_in_dim(parts, chip, 0, keepdims=True)
    return lax.dynamic_update_slice_in_dim(out, own, chip, 0)


def _row_tile(rows, cap, mult=8):
    best = None
    for t in range(mult, min(rows, cap) + 1, mult):
        if rows % t == 0:
            best = t
    assert best is not None, rows
    return best


def add_arrays(name, arrs, out_dtype=F32):
    shape = arrs[0].shape
    C = shape[-1]
    flat = [a.reshape(-1, C) for a in arrs]
    R = flat[0].shape[0]
    narrow = out_dtype == BF16 or any(a.dtype == BF16 for a in arrs)
    tr = _row_tile(R, 2048 if len(arrs) <= 2 else 1024, 16 if narrow else 8)
    n = len(flat)

    def body(*refs):
        acc = refs[0][...].astype(F32)
        for r in refs[1:n]:
            acc = acc + r[...].astype(F32)
        refs[n][...] = acc.astype(out_dtype)

    spec = pl.BlockSpec((tr, C), lambda i: (i, 0))
    out = pl.pallas_call(
        body, name=name, out_shape=jax.ShapeDtypeStruct((R, C), out_dtype), grid=(R // tr,),
        in_specs=[spec] * n, out_specs=spec, compiler_params=_cp(("parallel",)))(*flat)
    return out.reshape(shape)


def reduce_scatter_chips(slabs):
    _, R, C = slabs.shape
    half = R // 2
    c = lax.axis_index("c")
    k = 2 * lax.axis_index("x") + lax.axis_index("y")
    halves = slabs.reshape(N_CHIPS, 2, half, C)
    own = lax.dynamic_index_in_dim(halves, c, axis=1, keepdims=False)
    other = lax.dynamic_index_in_dim(halves, 1 - c, axis=1, keepdims=False)
    from_sibling = sibling_swap("rs_sibling_halves", other.astype(BF16))
    del k
    return add_arrays("rs_add_sibling", [own, from_sibling], out_dtype=BF16)


def reduce_scatter_finish(landed):
    c = lax.axis_index("c")
    mine = add_arrays("rs_add_chips", [landed[j] for j in range(N_CHIPS)])
    sib = sibling_swap("rs_sibling_result", mine)
    return jnp.concatenate([jnp.where(c == 0, mine, sib), jnp.where(c == 0, sib, mine)], axis=0)


def ada_mod_shard(cond_all, w_ada_shard, b_ada_shard):
    def body(c_ref, w_ref, b_ref, o_ref):
        o_ref[...] = _dot(_silu(c_ref[...]).astype(BF16), w_ref[...].astype(BF16)) + b_ref[...]

    return pl.pallas_call(body, name="ada_mod_shard", out_shape=jax.ShapeDtypeStruct((cond_all.shape[0], w_ada_shard.shape[1]), F32),
                          compiler_params=_cp())(cond_all, w_ada_shard, b_ada_shard)


def ada_bwd_shard(cond_all, dmod_all_shard, dmod_all, w_ada_shard, row_is_cctx):
    def body(c_ref, ds_ref, da_ref, w_ref, sel_ref, gw_ref, gb_ref, part_ref):
        sc = _silu(c_ref[...]).astype(BF16)
        gw_ref[...] = _dot_tn(sc, ds_ref[...].astype(BF16))
        gb_ref[...] = _colsum(da_ref[...])
        dc_tot = jnp.broadcast_to(_colsum(ds_ref[...] * sel_ref[...]), (8, ds_ref.shape[1]))
        part_ref[...] = _dot_nt(dc_tot.astype(BF16), w_ref[...].astype(BF16))

    n = cond_all.shape[0]
    return pl.pallas_call(
        body, name="ada_bwd_shard",
        out_shape=[jax.ShapeDtypeStruct(w_ada_shard.shape, F32), jax.ShapeDtypeStruct((1, dmod_all.shape[1]), F32),
                   jax.ShapeDtypeStruct((8, D), F32)],
        compiler_params=_cp())(cond_all, dmod_all_shard, dmod_all, w_ada_shard, row_is_cctx)


def sum_devices(name, gathered):
    def body(g_ref, o_ref):
        acc = g_ref[0]
        for d in range(1, N_DEV):
            acc = acc + g_ref[d]
        o_ref[...] = acc

    return pl.pallas_call(body, name=name, out_shape=jax.ShapeDtypeStruct(gathered.shape[1:], F32), compiler_params=_cp())(gathered)


def cctx_finish(gathered, c_ctx_row):
    def body(g_ref, c_ref, o_ref):
        acc = g_ref[0, 0:1, :]
        for k in range(1, N_CHIPS):
            acc = acc + g_ref[2 * k, 0:1, :]
        o_ref[...] = acc * _dsilu(c_ref[...])

    return pl.pallas_call(body, name="cctx_finish", out_shape=jax.ShapeDtypeStruct((1, D), F32), compiler_params=_cp())(gathered, c_ctx_row)


def _pack(parts, rows):
    flat = []
    for p in parts:
        p = p.reshape(-1)
        pad = (-p.shape[0]) % LANES
        flat.append(jnp.pad(p, (0, pad)) if pad else p)
    v = jnp.concatenate(flat)
    return jnp.pad(v, (0, rows * LANES - v.shape[0])).reshape(rows, LANES)


def _unpack(v, sizes):
    flat = v.reshape(-1)
    out, off = [], 0
    for n in sizes:
        out.append(flat[off:off + n])
        off += n + (-n) % LANES
    return out


W_SHARD_ROWS = 3456
SEG_ROWS = (0, 2320, 2576, 3088, 3344, 3408)


def kernel(x, c, ctx, c_ctx, w_ada, b_ada, norm_pre, norm_post, w_in, b_merge, pool_w, pool_scale, conv_w, conv_b, dt_bias, a_log, d_skip, ssd_norm, w_proj_pool, w_proj_ssd, w_out, loss_target, m_c_ctx, m_w_ada, m_b_ada, m_norm_pre, m_norm_post, m_w_in, m_b_merge, m_pool_w, m_pool_scale, m_conv_w, m_conv_b, m_dt_bias, m_a_log, m_d_skip, m_ssd_norm, m_w_proj_pool, m_w_proj_ssd, m_w_out, v_c_ctx, v_w_ada, v_b_ada, v_norm_pre, v_norm_post, v_w_in, v_b_merge, v_pool_w, v_pool_scale, v_conv_w, v_conv_b, v_dt_bias, v_a_log, v_d_skip, v_ssd_norm, v_w_proj_pool, v_w_proj_ssd, v_w_out):
    Bn, L, _ = x.shape
    Lc = ctx.shape[1]
    T, Tc = Bn * L, Bn * Lc
    assert Bn == 2
    ix, iy, ic = lax.axis_index("x"), lax.axis_index("y"), lax.axis_index("c")
    me = 4 * ix + 2 * iy + ic
    chip = 2 * ix + iy
    ada_cols = w_ada.shape[2]
    cw_cols = conv_w.shape[2]

    cond_own = jnp.pad(c, ((0, 8 - Bn), (0, 0))) + jnp.pad(c_ctx[None, :], ((Bn, 7 - Bn), (0, 0)))
    convw_own = jnp.pad(conv_w[0], ((0, 4), (0, D - cw_cols)))
    g1 = all_gather_small("gather_cond", jnp.concatenate([cond_own, convw_own], axis=0))
    cond_all = g1[:, 0:8].reshape(8 * N_DEV, D)
    conv_w_full = jnp.concatenate([g1[2 * k, 8:12, 0:cw_cols] for k in range(N_CHIPS)], axis=1)
    b_ada_shard = lax.dynamic_slice(b_ada, (0, chip * ada_cols), (1, ada_cols))
    g2 = all_gather_small("gather_mod", ada_mod_shard(cond_all, w_ada[0], b_ada_shard))
    mod_full = jnp.concatenate([g2[2 * k] for k in range(N_CHIPS)], axis=1)
    own = lax.dynamic_slice(mod_full, (8 * me, 0), (8, 3 * D))
    shift, scale, gate = (own[0:Bn, i * D:(i + 1) * D][:, None, :] for i in range(3))
    shift_c, scale_c = (jnp.broadcast_to(own[Bn:Bn + 1, i * D:(i + 1) * D][None], (Bn, 1, D)) for i in range(2))

    w_in_rows = IN_COLS // N_CHIPS
    shard_in = jnp.concatenate([w_in[0].T, jnp.zeros((16, D), F32)], axis=0).astype(BF16)
    shard_rest = jnp.concatenate([w_proj_pool[0], w_proj_ssd[0], w_out[0], pool_w[0].reshape(64, D)], axis=0).astype(BF16)
    w_inT = all_gather_chips("gather_w_in", shard_in)[:, 0:w_in_rows].reshape(IN_COLS, D)
    w_dt = jnp.pad(w_inT[9216:IN_COLS], ((0, LANES - 64), (0, 0)))
    seg_lo = (0, 256, 512, 768, 1024, 2048, 4096, 6144, 8192, 8704)
    seg_hi = (256, 512, 768, 1024, 2048, 4096, 6144, 8192, 8704, 9216)
    w_seg = [w_inT[lo:hi] for lo, hi in zip(seg_lo, seg_hi)] + [w_dt]

    hx = prenorm_fwd("prenorm_x", x, scale, shift, norm_pre)
    hc = prenorm_fwd("prenorm_ctx", ctx, scale_c, shift_c, norm_pre)
    hx2, hc2 = hx.reshape(T, D), hc.reshape(Tc, D)
    v = mm_nt("proj_v", hx2, w_inT[0:1024], F32).reshape(Bn, L, D)
    zp = mm_nt("proj_zpool", hx2, w_inT[1024:2048], F32).reshape(Bn, L, D)
    zs = mm_nt("proj_zssd", hx2, w_inT[2048:4096], F32).reshape(Bn, L, D_INNER)
    gp = mm_nt("proj_gate", hx2, w_inT[4096:6144], F32).reshape(Bn, L, 2 * D)
    xbc_raw, g_rest = mm_nt("proj_xbc", hx2, w_inT[6144:9216], F32, gather=shard_rest)
    xbc_raw = xbc_raw.reshape(Bn, L, CONV_DIM)
    w_pp = g_rest[:, 0:256].reshape(D, D)
    w_ps = g_rest[:, 256:768].reshape(D_INNER, D)
    w_o = g_rest[:, 768:1024].reshape(D, D)
    pool_full = g_rest[:, 1024:1088].reshape(N_CHIPS, 4, 64, POOL_GROUP).transpose(1, 0, 2, 3).reshape(D, POOL_GROUP)
    dt_raw = mm_nt("proj_dt", hx2, w_dt, F32)
    xbc_raw_c = mm_nt("proj_xbc_ctx", hc2, w_inT[6144:9216], F32).reshape(Bn, Lc, CONV_DIM)
    dt_raw_c = mm_nt("proj_dt_ctx", hc2, w_dt, F32)
    dtT = dt_raw[:, :64].reshape(Bn, L, 64).transpose(0, 2, 1)
    dtT_c = dt_raw_c[:, :64].reshape(Bn, Lc, 64).transpose(0, 2, 1)
    bias3 = dt_bias.reshape(2 * N_BC, HPG, 1)
    alog3 = a_log.reshape(2 * N_BC, HPG, 1)

    xbc = conv_fwd("conv_x", xbc_raw, conv_w_full, conv_b)
    xbc_c = conv_fwd("conv_ctx", xbc_raw_c, conv_w_full, conv_b)
    zero_state = jnp.zeros((Bn, N_BC, D_STATE, GW), F32)
    tables = ssd_tables()
    ys, hs_x, hs_c, hf_x, hf_c = [], [], [], [], []
    for d in range(2):
        hsc, hfc = ssd_fwd3(f"ssd_fwd_ctx{d}", dtT_c, bias3, alog3, xbc_c, zero_state, d, False)
        y, hsx, hfx = ssd_fwd3(f"ssd_fwd_x{d}", dtT, bias3, alog3, xbc, hfc, d, True)
        ys.append(y)
        hs_x.append(hsx)
        hs_c.append(hsc)
        hf_x.append(hfx)
        hf_c.append(hfc)

    dgs = [pool_diff(f"pool_diff{g}", v, g * POOL_GROUP, g, False) for g in range(4)]
    y_pool = pool_mix_fwd("pool_mix", dgs, zp, pool_full, pool_scale)
    dskip_lanes = jnp.repeat(d_skip[0], HEAD_DIM)[None, :]
    y_ssd = gated_norm_fwd("gated_norm", ys[0], ys[1], (xbc, D_INNER), zs, dskip_lanes, ssd_norm)
    merged, p1, p2, dout, g_res, dgate, g_norm_post, loss_part = merge_fwd(
        "merge_fwd", y_pool, y_ssd, gp, x, loss_target, gate, b_merge, norm_post, w_pp, w_ps, w_o)

    dp1, dp2, dgp, dyp, dys, g_b_merge = merge_bwd("merge_bwd", dout, gp, p1, p2, b_merge, w_pp, w_ps, w_o)
    gw_o = mm_tn("gw_out", merged.reshape(T, D), dout.reshape(T, D))
    gw_pp = mm_tn("gw_proj_pool", y_pool.reshape(T, D), dp1.reshape(T, D))
    gw_ps = mm_tn("gw_proj_ssd", y_ssd.reshape(T, D_INNER), dp2.reshape(T, D))

    *dds, dzp, g_pool, g_pool_scale = pool_mix_bwd("pool_mix_bwd", dgs, zp, dyp, pool_full, pool_scale)
    dvs = [pool_diff(f"pool_diff_t{g}", dds[g], 0, g, True) for g in range(4)]

    head_sel = (jnp.arange(D_INNER)[:, None] // HEAD_DIM == jnp.arange(LANES)[None, :]).astype(BF16)
    dy, dzs, g_ssd_norm, g_dskip = gated_norm_bwd(
        "gated_norm_bwd", ys[0], ys[1], (xbc, D_INNER), zs, dys, dskip_lanes, ssd_norm, head_sel)

    dxs, dbm, dcm, ddt, dxs_c, dbm_c, ddt_c = [], [], [], [], [], [], []
    g_bias = jnp.zeros((2, N_BC, HPG, 1), F32)
    g_alog = jnp.zeros((2, N_BC, HPG, 1), F32)
    for d in range(2):
        a, b_, c_, t_, gb, ga, dh0 = ssd_bwd3(f"ssd_bwd_x{d}", dtT, bias3, alog3, xbc, hs_x[d], dy, zero_state, d)
        dxs.append(a), dbm.append(b_), dcm.append(c_), ddt.append(t_)
        ac, bc, _, tc, gbc, gac, _ = ssd_bwd3(f"ssd_bwd_ctx{d}", dtT_c, bias3, alog3, xbc_c, hs_c[d], None, dh0, d)
        dxs_c.append(ac), dbm_c.append(bc), ddt_c.append(tc)
        g_bias = g_bias.at[d].set(jnp.sum(gb, axis=0) + jnp.sum(gbc, axis=0))
        g_alog = g_alog.at[d].set(jnp.sum(ga, axis=0) + jnp.sum(gac, axis=0))

    dxr_xs, gcw_xs, gcb_xs = conv_bwd_stream("conv_bwd_xs", xbc_raw, dxs, conv_w_full, conv_b, 0, D_INNER, scaled=(dy, dskip_lanes))
    dxr_b, gcw_b, gcb_b = conv_bwd_stream("conv_bwd_b", xbc_raw, dbm, conv_w_full, conv_b, D_INNER, N_BC * D_STATE)
    dxr_c, gcw_c, gcb_c = conv_bwd_stream("conv_bwd_c", xbc_raw, dcm, conv_w_full, conv_b, D_INNER + N_BC * D_STATE, N_BC * D_STATE)
    dxr_xs_c, gcw_xs_c, gcb_xs_c = conv_bwd_stream("conv_bwd_xs_ctx", xbc_raw_c, dxs_c, conv_w_full, conv_b, 0, D_INNER)
    dxr_b_c, gcw_b_c, gcb_b_c = conv_bwd_stream("conv_bwd_b_ctx", xbc_raw_c, dbm_c, conv_w_full, conv_b, D_INNER, N_BC * D_STATE)
    g_conv_w = jnp.concatenate([gcw_xs + gcw_xs_c, gcw_b + gcw_b_c, gcw_c], axis=1)
    g_conv_b = jnp.concatenate([gcb_xs + gcb_xs_c, gcb_b + gcb_b_c, gcb_c], axis=1)

    def dt_cols(parts, n_tok):
        t = jnp.concatenate(parts, axis=1).transpose(0, 2, 1).reshape(n_tok, 2 * N_HEADS)
        return jnp.pad(t, ((0, 0), (0, LANES - 2 * N_HEADS))).astype(BF16)

    ddt2, ddt2_c = dt_cols(ddt, T), dt_cols(ddt_c, Tc)
    segs = ([dv.reshape(T, POOL_GROUP) for dv in dvs]
            + [dzp.reshape(T, D), dzs.reshape(T, D_INNER), dgp.reshape(T, 2 * D), dxr_xs.reshape(T, D_INNER),
               dxr_b.reshape(T, N_BC * D_STATE), dxr_c.reshape(T, N_BC * D_STATE), ddt2])
    segs_c = {7: dxr_xs_c.reshape(Tc, D_INNER), 8: dxr_b_c.reshape(Tc, N_BC * D_STATE), 10: ddt2_c}
    gw_rows = []
    for i, seg in enumerate(segs):
        init = mm_tn(f"gw_in_ctx{i}", segs_c[i], hc2) if i in segs_c else None
        gw_rows.append(mm_tn(f"gw_in{i}", seg, hx2, init=init))
    gw_rows[-1] = gw_rows[-1][0:2 * N_HEADS]
    gw_inT = jnp.concatenate(gw_rows, axis=0)

    pool_slab = g_pool.reshape(4, N_CHIPS, 64, POOL_GROUP).transpose(1, 0, 2, 3).reshape(N_CHIPS, 64, D)
    slabs = jnp.concatenate([gw_inT.reshape(N_CHIPS, 2320, D), gw_pp.reshape(N_CHIPS, 256, D), gw_ps.reshape(N_CHIPS, 512, D),
                             gw_o.reshape(N_CHIPS, 256, D), pool_slab, jnp.zeros((N_CHIPS, W_SHARD_ROWS - SEG_ROWS[-1], D), F32)], axis=1)
    chip_part = reduce_scatter_chips(slabs)
    d_hx, landed = mm_nn_multi("d_hx", list(zip(segs, w_seg)), F32, tm=1024, tk=256, exchange=chip_part)
    d_hx = d_hx.reshape(Bn, L, D)
    gsh = reduce_scatter_finish(landed)
    d_hc = mm_nn_multi("d_hc", [(segs_c[i], w_seg[i]) for i in (7, 8, 10)], F32).reshape(Bn, Lc, D)

    grad_x, dscale, dshift, g_npre_x = prenorm_bwd("prenorm_bwd_x", x, d_hx, scale, norm_pre, g_res=g_res)
    _, dscale_c, dshift_c, g_npre_c = prenorm_bwd("prenorm_bwd_ctx", ctx, d_hc, scale_c, norm_pre)

    dmod_x = jnp.concatenate([dshift[:, 0], dscale[:, 0], dgate[:, 0]], axis=1)
    dmod_c = jnp.concatenate([jnp.sum(dshift_c[:, 0], axis=0, keepdims=True), jnp.sum(dscale_c[:, 0], axis=0, keepdims=True),
                              jnp.zeros((1, D), F32)], axis=1)
    dmod_own = jnp.pad(dmod_x, ((0, 8 - Bn), (0, 0))) + jnp.pad(dmod_c, ((Bn, 7 - Bn), (0, 0)))
    dmod_all = all_gather_small("gather_dmod", dmod_own).reshape(8 * N_DEV, 3 * D)
    row_is_cctx = (jnp.arange(8 * N_DEV) % 8 == Bn).astype(F32)[:, None]
    g_w_ada, g_b_ada, cpart = ada_bwd_shard(
        cond_all, lax.dynamic_slice(dmod_all, (0, chip * ada_cols), (8 * N_DEV, ada_cols)), dmod_all, w_ada[0], row_is_cctx)
    g_c_ctx = cctx_finish(all_gather_small("gather_cctx", cpart), c_ctx[None, :])

    small_sizes = (D, D, 2 * D, D, CONV_DIM, 2 * N_HEADS, 2 * N_HEADS, N_HEADS, D_INNER, 4 * CONV_DIM, 1)
    pk = _pack([g_npre_x + g_npre_c, g_norm_post, g_b_merge, g_pool_scale, g_conv_b, g_bias, g_alog, g_dskip[0, 0:N_HEADS],
                g_ssd_norm, g_conv_w, loss_part[0, 0:1]], 184)
    small = sum_devices("sum_small", all_gather_small("gather_small", pk))
    (g_norm_pre, g_norm_post_t, g_b_merge_t, g_pool_scale_t, g_conv_b_t, g_dt_bias, g_a_log, g_d_skip, g_ssd_norm_t,
     g_conv_w_t, loss) = _unpack(small, small_sizes)
    g_conv_w_shard = lax.dynamic_slice(g_conv_w_t.reshape(4, CONV_DIM), (0, chip * cw_cols), (4, cw_cols))

    g_w_in = gsh[SEG_ROWS[0]:SEG_ROWS[1]].T
    g_w_pp, g_w_ps, g_w_o = (gsh[SEG_ROWS[i]:SEG_ROWS[i + 1]] for i in (1, 2, 3))
    g_pool_w = gsh[SEG_ROWS[4]:SEG_ROWS[5]].reshape(256, POOL_GROUP)

    grads = {
        "c_ctx": g_c_ctx.reshape(c_ctx.shape), "w_ada": g_w_ada[None], "b_ada": g_b_ada, "norm_pre": g_norm_pre[None],
        "norm_post": g_norm_post_t[None], "w_in": g_w_in[None], "b_merge": g_b_merge_t[None],
        "pool_w": g_pool_w.reshape(pool_w.shape), "pool_scale": g_pool_scale_t[None], "conv_w": g_conv_w_shard[None],
        "conv_b": g_conv_b_t[None], "dt_bias": g_dt_bias.reshape(dt_bias.shape), "a_log": g_a_log.reshape(a_log.shape),
        "d_skip": g_d_skip[None], "ssd_norm": g_ssd_norm_t[None], "w_proj_pool": g_w_pp[None], "w_proj_ssd": g_w_ps[None],
        "w_out": g_w_o[None]}
    weights = dict(c_ctx=c_ctx, w_ada=w_ada, b_ada=b_ada, norm_pre=norm_pre, norm_post=norm_post, w_in=w_in, b_merge=b_merge,
                   pool_w=pool_w, pool_scale=pool_scale, conv_w=conv_w, conv_b=conv_b, dt_bias=dt_bias, a_log=a_log,
                   d_skip=d_skip, ssd_norm=ssd_norm, w_proj_pool=w_proj_pool, w_proj_ssd=w_proj_ssd, w_out=w_out)
    m_in = dict(c_ctx=m_c_ctx, w_ada=m_w_ada, b_ada=m_b_ada, norm_pre=m_norm_pre, norm_post=m_norm_post, w_in=m_w_in,
                b_merge=m_b_merge, pool_w=m_pool_w, pool_scale=m_pool_scale, conv_w=m_conv_w, conv_b=m_conv_b,
                dt_bias=m_dt_bias, a_log=m_a_log, d_skip=m_d_skip, ssd_norm=m_ssd_norm, w_proj_pool=m_w_proj_pool,
                w_proj_ssd=m_w_proj_ssd, w_out=m_w_out)
    v_in = dict(c_ctx=v_c_ctx, w_ada=v_w_ada, b_ada=v_b_ada, norm_pre=v_norm_pre, norm_post=v_norm_post, w_in=v_w_in,
                b_merge=v_b_merge, pool_w=v_pool_w, pool_scale=v_pool_scale, conv_w=v_conv_w, conv_b=v_conv_b,
                dt_bias=v_dt_bias, a_log=v_a_log, d_skip=v_d_skip, ssd_norm=v_ssd_norm, w_proj_pool=v_w_proj_pool,
                w_proj_ssd=v_w_proj_ssd, w_out=v_w_out)
    names = list(weights)
    big = ("w_ada", "w_in", "pool_w", "w_proj_pool", "w_proj_ssd", "w_out")
    small_names = [n for n in names if n not in big]
    delta, new_m, new_v = {}, {}, {}
    for n in big:
        shape2 = (-1, weights[n].shape[-1])
        d_, m_, v_ = adamw(f"adamw_{n}", weights[n].reshape(shape2), grads[n].reshape(shape2), m_in[n].reshape(shape2),
                           v_in[n].reshape(shape2), tr=128)
        delta[n], new_m[n], new_v[n] = (t.reshape(weights[n].shape) for t in (d_, m_, v_))
    sizes = [weights[n].size for n in small_names]
    packed = [_pack([src[n] for n in small_names], 144) for src in (weights, grads, m_in, v_in)]
    outs = adamw("adamw_small", *packed, tr=144)
    for res, store in zip(outs, (delta, new_m, new_v)):
        for n, piece in zip(small_names, _unpack(res, sizes)):
            store[n] = piece.reshape(weights[n].shape)

    return (loss.reshape(()), grad_x, *[grads[n] for n in names], *[delta[n] for n in names],
            *[new_m[n] for n in names], *[new_v[n] for n in names])
```

```python
import jax
import jax.numpy as jnp
from jax import lax
from jax.experimental import pallas as pl
from jax.experimental.pallas import tpu as pltpu

F32 = jnp.float32
BF16 = jnp.bfloat16
MESH = pl.DeviceIdType.MESH

D = 1024
GRID_W = 64
NORM_EPS = 1e-6
POOL_WINDOWS = (2, 4, 8, 16)
POOL_GROUP = 256
D_INNER = 2048
HEAD_DIM = 64
N_HEADS = 32
D_STATE = 128
N_BC = 4
HPG = N_HEADS // N_BC
GW = HPG * HEAD_DIM
CONV_DIM = 3072
CHUNK = 128
OFF_XBC = 6144
IN_COLS = 9280
N_CHIPS = 4
N_DEV = 8

ADAM_LR = 0.001
ADAM_B1 = 0.9
ADAM_B2 = 0.999
ADAM_EPS = 1e-08
ADAM_WD = 0.01
ADAM_STEP = 10

V7X_VMEM_BYTES = 64 * 1024 * 1024
VMEM_LIMIT = V7X_VMEM_BYTES * 3 // 4
LANES = 128


def _cp(sem=None):
    return pltpu.CompilerParams(dimension_semantics=sem, vmem_limit_bytes=VMEM_LIMIT)


def _dot(a, b):
    return jnp.dot(a, b, preferred_element_type=F32)


def _dot_nt(a, b):
    return lax.dot_general(a, b, (((1,), (1,)), ((), ())), preferred_element_type=F32)


def _dot_tn(a, b):
    return lax.dot_general(a, b, (((0,), (0,)), ((), ())), preferred_element_type=F32)


def _split3(x):
    hi = x.astype(BF16)
    r1 = x - hi.astype(F32)
    mid = r1.astype(BF16)
    lo = (r1 - mid.astype(F32)).astype(BF16)
    return hi, mid, lo


def _sigmoid(x):
    return jax.nn.sigmoid(x)


def _silu(x):
    return x * _sigmoid(x)


def _dsilu(x):
    s = _sigmoid(x)
    return s * (1.0 + x * (1.0 - s))


def _softplus(x):
    return jnp.maximum(x, 0.0) + jnp.log(1.0 + jnp.exp(-jnp.abs(x)))


def mm_nt(name, a, b, out_dtype, tm=1024, tn=512, gather=None):
    M, K = a.shape
    N = b.shape[0]
    tm, tn = min(tm, M), min(tn, N)
    assert M % tm == 0 and N % tn == 0, (M, N, tm, tn)
    n_i, n_j = M // tm, N // tn
    has_g = gather is not None
    if has_g:
        half = gather.shape[0] // 2
        assert gather.shape[0] % 32 == 0 and n_i * n_j >= 4

    def body(*refs):
        a_ref, b_ref = refs[0], refs[1]
        if has_g:
            s_ref, o_ref, g_ref, send_sems, recv_sems = refs[2:]
            x, y, c = _me()
            chips = [(1 - x, y), (x, 1 - y), (1 - x, 1 - y)]
            step = pl.program_id(0) * n_j + pl.program_id(1)

            def rows(chip, hc):
                return g_ref.at[2 * chip[0] + chip[1], pl.ds(hc * half, half), :]

            def first(j, chip):
                return pltpu.make_async_remote_copy(
                    src_ref=s_ref.at[pl.ds(c * half, half), :], dst_ref=rows((x, y), c), send_sem=send_sems.at[j],
                    recv_sem=recv_sems.at[j], device_id=(*chip, c), device_id_type=MESH)

            def landed(j, chip, hc):
                return pltpu.make_async_remote_copy(
                    src_ref=rows(chip, hc), dst_ref=rows(chip, hc), send_sem=send_sems.at[j], recv_sem=recv_sems.at[j],
                    device_id=(x, y, 1 - c), device_id_type=MESH)

            @pl.when(step == 0)
            def _():
                for j, chip in enumerate(chips):
                    first(j, chip).start()

            @pl.when(step == (3 * n_i * n_j) // 4)
            def _():
                for j, chip in enumerate(chips):
                    landed(j, chip, c).wait_recv()
                    landed(3 + j, chip, c).start()
        else:
            o_ref = refs[2]

        o_ref[...] = _dot_nt(a_ref[...], b_ref[...]).astype(o_ref.dtype)

        if has_g:
            @pl.when(step == n_i * n_j - 1)
            def _():
                for j, chip in enumerate(chips):
                    landed(3 + j, chip, 1 - c).wait_recv()
                for j, chip in enumerate(chips):
                    first(j, chip).wait_send()
                    landed(3 + j, chip, c).wait_send()

    in_specs = [pl.BlockSpec((tm, K), lambda i, j: (i, 0)), pl.BlockSpec((tn, K), lambda i, j: (j, 0))]
    out_shape = jax.ShapeDtypeStruct((M, N), out_dtype)
    out_specs = pl.BlockSpec((tm, tn), lambda i, j: (i, j))
    if not has_g:
        return pl.pallas_call(body, name=name, out_shape=out_shape, grid=(n_i, n_j), in_specs=in_specs, out_specs=out_specs,
                              compiler_params=_cp(("parallel", "arbitrary")))(a, b)
    out, g = pl.pallas_call(
        body, name=name, out_shape=[out_shape, jax.ShapeDtypeStruct((N_CHIPS, *gather.shape), gather.dtype)], grid=(n_i, n_j),
        in_specs=in_specs + [pl.BlockSpec(memory_space=pl.ANY)], out_specs=[out_specs, pl.BlockSpec(memory_space=pl.ANY)],
        scratch_shapes=[pltpu.SemaphoreType.DMA((6,)), pltpu.SemaphoreType.DMA((6,))],
        compiler_params=_cp(("arbitrary", "arbitrary")))(a, b, gather)
    chip = 2 * lax.axis_index("x") + lax.axis_index("y")
    return out, lax.dynamic_update_index_in_dim(g, gather, chip, 0)


def mm_tn(name, a, b, init=None, tm=1024, tn=1024, tk=512):
    T, M = a.shape
    N = b.shape[1]
    tm, tn, tk = min(tm, M), min(tn, N), min(tk, T)
    assert M % tm == 0 and N % tn == 0 and T % tk == 0, (M, N, T)
    has_init = init is not None

    def body(*refs):
        if has_init:
            a_ref, b_ref, i_ref, o_ref = refs
        else:
            a_ref, b_ref, o_ref = refs
        k = pl.program_id(2)

        @pl.when(k == 0)
        def _():
            o_ref[...] = i_ref[...] if has_init else jnp.zeros(o_ref.shape, F32)

        o_ref[...] += _dot_tn(a_ref[...], b_ref[...])

    in_specs = [pl.BlockSpec((tk, tm), lambda i, j, k: (k, i)), pl.BlockSpec((tk, tn), lambda i, j, k: (k, j))]
    args = [a, b]
    if has_init:
        in_specs.append(pl.BlockSpec((tm, tn), lambda i, j, k: (i, j)))
        args.append(init)
    return pl.pallas_call(
        body, name=name, out_shape=jax.ShapeDtypeStruct((M, N), F32), grid=(M // tm, N // tn, T // tk),
        in_specs=in_specs, out_specs=pl.BlockSpec((tm, tn), lambda i, j, k: (i, j)),
        compiler_params=_cp(("parallel", "parallel", "arbitrary")))(*args)


def mm_nn_multi(name, pairs, out_dtype, tm=512, tk=512, exchange=None):
    M = pairs[0][0].shape[0]
    N = pairs[0][1].shape[1]
    tm = min(tm, M)
    assert M % tm == 0
    plan = []
    step = 0
    for a, b in pairs:
        K = a.shape[1]
        t = min(tk, K)
        assert K % t == 0 and b.shape == (K, N)
        plan.append((t, step, K // t))
        step += K // t
    nsteps = step
    npairs = len(pairs)

    n_i = M // tm
    has_x = exchange is not None

    def body(*refs):
        if has_x:
            p_ref, o_ref, land_ref, acc, send_sems, recv_sems = refs[2 * npairs:]
        else:
            o_ref, acc = refs[2 * npairs:]
        i, k = pl.program_id(0), pl.program_id(1)

        if has_x:
            x, y, c = _me()
            me_chip = 2 * x + y
            chips = [(1 - x, y), (x, 1 - y), (1 - x, 1 - y)]

            def copy(j, src_chip, dst_chip, to):
                return pltpu.make_async_remote_copy(
                    src_ref=p_ref.at[src_chip], dst_ref=land_ref.at[dst_chip], send_sem=send_sems.at[j], recv_sem=recv_sems.at[j],
                    device_id=(*to, c), device_id_type=MESH)

            @pl.when((i == 0) & (k == 0))
            def _():
                for j, chip in enumerate(chips):
                    copy(j, 2 * chip[0] + chip[1], me_chip, chip).start()

        @pl.when(k == 0)
        def _():
            acc[...] = jnp.zeros(acc.shape, F32)

        for p, (_, first, n) in enumerate(plan):
            @pl.when((k >= first) & (k < first + n))
            def _(p=p):
                acc[...] += _dot(refs[2 * p][...], refs[2 * p + 1][...])

        @pl.when(k == nsteps - 1)
        def _():
            o_ref[...] = acc[...].astype(o_ref.dtype)

        if has_x:
            @pl.when((i == n_i - 1) & (k == nsteps - 1))
            def _():
                for j, chip in enumerate(chips):
                    copy(j, me_chip, 2 * chip[0] + chip[1], chip).wait_recv()
                for j, chip in enumerate(chips):
                    copy(j, 2 * chip[0] + chip[1], me_chip, chip).wait_send()

    in_specs, args = [], []
    for (a, b), (t, first, n) in zip(pairs, plan):
        in_specs.append(pl.BlockSpec((tm, t), lambda i, k, first=first, n=n: (i, jnp.clip(k - first, 0, n - 1))))
        in_specs.append(pl.BlockSpec((t, N), lambda i, k, first=first, n=n: (jnp.clip(k - first, 0, n - 1), 0)))
        args += [a, b]
    out_shape = jax.ShapeDtypeStruct((M, N), out_dtype)
    out_specs = pl.BlockSpec((tm, N), lambda i, k: (i, 0))
    scratch = [pltpu.VMEM((tm, N), F32)]
    if has_x:
        in_specs.append(pl.BlockSpec(memory_space=pl.ANY))
        args.append(exchange)
        out_shape = [out_shape, jax.ShapeDtypeStruct(exchange.shape, exchange.dtype)]
        out_specs = [out_specs, pl.BlockSpec(memory_space=pl.ANY)]
        scratch += [pltpu.SemaphoreType.DMA((3,)), pltpu.SemaphoreType.DMA((3,))]
    res = pl.pallas_call(
        body, name=name, out_shape=out_shape, grid=(n_i, nsteps), in_specs=in_specs, out_specs=out_specs,
        scratch_shapes=scratch, compiler_params=_cp(("arbitrary", "arbitrary")))(*args)
    if not has_x:
        return res
    out, landed = res
    chip = 2 * lax.axis_index("x") + lax.axis_index("y")
    own = lax.dynamic_index_in_dim(exchange, chip, 0, keepdims=True)
    return out, lax.dynamic_update_slice_in_dim(landed, own, chip, 0)


def tok_call(name, body, tiled, perb, glob, out_tiled, out_perb, out_glob, tm=256):
    widths = [t[1] if isinstance(t, tuple) else t.shape[2] for t in tiled]
    tiled = [t[0] if isinstance(t, tuple) else t for t in tiled]
    Bn, L = tiled[0].shape[:2]
    tm = min(tm, L)
    assert L % tm == 0
    n_t, n_p, n_g = len(tiled), len(perb), len(glob)
    o_t, o_p, o_g = len(out_tiled), len(out_perb), len(out_glob)
    n_in = n_t + n_p + n_g

    def kern(*refs):
        ins, outs = refs[:n_in], refs[n_in:]
        b, j = pl.program_id(0), pl.program_id(1)
        vals = [r[0] for r in ins[:n_t + n_p]] + [r[...] for r in ins[n_t + n_p:]]
        res = body(*vals)
        if not isinstance(res, (tuple, list)):
            res = (res,)
        assert len(res) == o_t + o_p + o_g, (name, len(res))
        for r, v in zip(outs[:o_t], res[:o_t]):
            r[0] = v.astype(r.dtype)

        def accum(r, v, first, lead):
            @pl.when(first)
            def _():
                r[...] = jnp.zeros(r.shape, F32)
            if lead:
                r[0] += v
            else:
                r[...] += v

        for r, v in zip(outs[o_t:o_t + o_p], res[o_t:o_t + o_p]):
            accum(r, v, j == 0, True)
        for r, v in zip(outs[o_t + o_p:], res[o_t + o_p:]):
            accum(r, v, (j == 0) & (b == 0), False)

    in_specs = ([pl.BlockSpec((1, tm, w), lambda b, j: (b, j, 0)) for w in widths]
                + [pl.BlockSpec((1, 1, a.shape[2]), lambda b, j: (b, 0, 0)) for a in perb]
                + [pl.BlockSpec(a.shape, lambda b, j: (0, 0), pipeline_mode=pl.Buffered(1)) for a in glob])
    out_shape = ([jax.ShapeDtypeStruct((Bn, L, w), dt) for w, dt in out_tiled]
                 + [jax.ShapeDtypeStruct((Bn, 1, w), F32) for w in out_perb]
                 + [jax.ShapeDtypeStruct(s, F32) for s in out_glob])
    out_specs = ([pl.BlockSpec((1, tm, w), lambda b, j: (b, j, 0)) for w, _ in out_tiled]
                 + [pl.BlockSpec((1, 1, w), lambda b, j: (b, 0, 0)) for w in out_perb]
                 + [pl.BlockSpec(s, lambda b, j: (0, 0)) for s in out_glob])
    return pl.pallas_call(
        kern, name=name, out_shape=out_shape, grid=(Bn, L // tm), in_specs=in_specs, out_specs=out_specs,
        compiler_params=_cp(("arbitrary", "arbitrary")))(*tiled, *perb, *glob)


def slab_call(name, body, slabs, colparams, out_slabs, out_colred, wc=LANES):
    Bn, L = slabs[0][0].shape[:2]
    w_out = out_slabs[0][0]
    assert w_out % wc == 0 and all(off % wc == 0 for _, off in slabs + colparams)
    n_col = w_out // wc
    n_s, n_c = len(slabs), len(colparams)
    o_s = len(out_slabs)

    def kern(*refs):
        ins, outs = refs[:n_s + n_c], refs[n_s + n_c:]
        b = pl.program_id(1)
        vals = [r[0] for r in ins[:n_s]] + [r[...] for r in ins[n_s:]]
        res = body(*vals)
        if not isinstance(res, (tuple, list)):
            res = (res,)
        assert len(res) == o_s + len(out_colred), name
        for r, v in zip(outs[:o_s], res[:o_s]):
            r[0] = v.astype(r.dtype)

        def accum(r, v):
            @pl.when(b == 0)
            def _():
                r[...] = jnp.zeros(r.shape, F32)
            r[...] += v

        for r, v in zip(outs[o_s:], res[o_s:]):
            accum(r, v)

    in_specs = ([pl.BlockSpec((1, L, wc), lambda j, b, o=off // wc: (b, 0, o + j)) for _, off in slabs]
                + [pl.BlockSpec((a.shape[0], wc), lambda j, b, o=off // wc: (0, o + j)) for a, off in colparams])
    out_shape = ([jax.ShapeDtypeStruct((Bn, L, w), dt) for w, dt in out_slabs]
                 + [jax.ShapeDtypeStruct((r, w_out), F32) for r in out_colred])
    out_specs = ([pl.BlockSpec((1, L, wc), lambda j, b: (b, 0, j)) for _ in out_slabs]
                 + [pl.BlockSpec((r, wc), lambda j, b: (0, j)) for r in out_colred])
    return pl.pallas_call(
        kern, name=name, out_shape=out_shape, grid=(n_col, Bn), in_specs=in_specs, out_specs=out_specs,
        compiler_params=_cp(("arbitrary", "arbitrary")))(*[a for a, _ in slabs], *[a for a, _ in colparams])


def _rms_r(x):
    return lax.rsqrt(jnp.mean(x * x, axis=-1, keepdims=True) + NORM_EPS)


def _rms_bwd(dxh, x, r):
    return r * (dxh - x * (r * r) * jnp.mean(dxh * x, axis=-1, keepdims=True))


def _colsum(v):
    return jnp.sum(v, axis=0, keepdims=True)


def _stack_rows(rows):
    n, w = len(rows), rows[0].shape[1]
    sub = lax.broadcasted_iota(jnp.int32, (n, w), 0)
    acc = jnp.zeros((n, w), F32)
    for r, row in enumerate(rows):
        acc = acc + jnp.where(sub == r, jnp.broadcast_to(row, (n, w)), 0.0)
    return acc


def prenorm_fwd(name, x, scale, shift, w_pre):
    def body(x, scale, shift, w):
        n = x * _rms_r(x) * w
        return n * (1.0 + scale) + shift

    return tok_call(name, body, [x], [scale, shift], [w_pre], [(D, BF16)], [], [])[0]


def prenorm_bwd(name, x, dhx, scale, w_pre, g_res=None):
    has_res = g_res is not None

    def body(*v):
        if has_res:
            x, dhx, g, scale, w = v
        else:
            x, dhx, scale, w = v
        r = _rms_r(x)
        xr = x * r
        n = xr * w
        dn = dhx * (1.0 + scale)
        dx = _rms_bwd(dn * w, x, r)
        if has_res:
            dx = dx + g
        return dx, _colsum(dhx * n), _colsum(dhx), _colsum(dn * xr)

    tiled = [x, dhx] + ([g_res] if has_res else [])
    return tok_call(name, body, tiled, [scale], [w_pre], [(D, F32)], [D, D], [(1, D)])


def _shift_rows(x, o, tok, L):
    if o == 0:
        return x
    rolled = pltpu.roll(x, (-o) % L, 0)
    return jnp.where((tok + o >= 0) & (tok + o < L), rolled, 0.0)


def conv_fwd(name, xbc_raw, conv_w, conv_b):
    L = xbc_raw.shape[1]

    def body(x, w, b):
        tok = lax.broadcasted_iota(jnp.int32, x.shape, 0)
        pre = b
        for k in range(4):
            pre = pre + _shift_rows(x, k - 2, tok, L) * w[k:k + 1]
        return _silu(pre)

    return slab_call(name, body, [(xbc_raw, 0)], [(conv_w, 0), (conv_b, 0)], [(CONV_DIM, F32)], [])[0]


def conv_bwd(name, xbc_raw, dparts, conv_w, conv_b, col0, width, scaled=None):
    L = xbc_raw.shape[1]
    n_d = len(dparts) + (1 if scaled is not None else 0)

    def body(*v):
        x, ds, w, b = v[0], v[1:1 + n_d], v[1 + n_d], v[2 + n_d]
        tok = lax.broadcasted_iota(jnp.int32, x.shape, 0)
        taps = [_shift_rows(x, k - 2, tok, L) for k in range(4)]
        pre = b
        for k in range(4):
            pre = pre + taps[k] * w[k:k + 1]
        dy = ds[0] * v[3 + n_d] if scaled is not None else ds[0]
        for extra in ds[1:]:
            dy = dy + extra
        dpre = dy * _dsilu(pre)
        dx = jnp.zeros_like(x)
        for k in range(4):
            dx = dx + _shift_rows(dpre, 2 - k, tok, L) * w[k:k + 1]
        dw = _stack_rows([_colsum(dpre * taps[k]) for k in range(4)])
        return dx, dw, _colsum(dpre)

    slabs = [(xbc_raw, col0)] + ([(scaled[0], 0)] if scaled is not None else []) + [(d, 0) for d in dparts]
    colparams = [(conv_w, col0), (conv_b, col0)] + ([(scaled[1], 0)] if scaled is not None else [])
    return slab_call(name, body, slabs, colparams, [(width, BF16)], [4, 1])


CONV_ROWS = 128
CONV_HALO = 8


def _halo_chunks(L, load, work):
    ch, hl = CONV_ROWS, CONV_HALO
    n = L // ch
    assert L % ch == 0
    if n == 1:
        z = jnp.zeros_like(load(0, hl))
        work(0, jnp.concatenate([z, load(0, ch), z], axis=0))
        return
    z = jnp.zeros_like(load(0, hl))
    work(0, jnp.concatenate([z, load(0, ch + hl)], axis=0))

    def step(i, carry):
        start = pl.multiple_of(i * ch, ch)
        work(start, load(pl.multiple_of(start - hl, hl), ch + 2 * hl))
        return carry

    lax.fori_loop(1, n - 1, step, 0)
    work(L - ch, jnp.concatenate([load(L - ch - hl, ch + hl), z], axis=0))


def _rows_at(xh, o):
    return xh if o == 0 else pltpu.roll(xh, (-o) % xh.shape[0], 0)


def conv_fwd_stream(name, xbc_raw, conv_w, conv_b, wc=LANES):
    Bn, L, W = xbc_raw.shape
    mid = slice(CONV_HALO, CONV_HALO + CONV_ROWS)

    def kern(x_ref, w_ref, b_ref, o_ref):
        w, b = w_ref[...], b_ref[...]

        def work(start, xh):
            pre = b
            for k in range(4):
                pre = pre + _rows_at(xh, k - 2) * w[k:k + 1]
            o_ref[0, pl.ds(start, CONV_ROWS), :] = _silu(pre)[mid]

        _halo_chunks(L, lambda s, n: x_ref[0, pl.ds(s, n), :], work)

    return pl.pallas_call(
        kern, name=name, out_shape=jax.ShapeDtypeStruct((Bn, L, W), F32), grid=(W // wc, Bn),
        in_specs=[pl.BlockSpec((1, L, wc), lambda j, b: (b, 0, j)), pl.BlockSpec((4, wc), lambda j, b: (0, j)),
                  pl.BlockSpec((1, wc), lambda j, b: (0, j))],
        out_specs=pl.BlockSpec((1, L, wc), lambda j, b: (b, 0, j)),
        compiler_params=_cp(("arbitrary", "arbitrary")))(xbc_raw, conv_w, conv_b)


def conv_bwd_stream(name, xbc_raw, dparts, conv_w, conv_b, col0, width, scaled=None, wc=LANES):
    Bn, L, _ = xbc_raw.shape
    n_d = len(dparts)
    has_s = scaled is not None
    mid = slice(CONV_HALO, CONV_HALO + CONV_ROWS)
    c0 = col0 // wc

    def kern(*refs):
        x_ref, d_refs = refs[0], refs[1:1 + n_d]
        pos = 1 + n_d
        if has_s:
            s_ref, pos = refs[pos], pos + 1
        w_ref, b_ref = refs[pos], refs[pos + 1]
        pos += 2
        if has_s:
            scale = refs[pos][...]
            pos += 1
        dx_ref, dw_ref, db_ref = refs[pos:pos + 3]
        acc = refs[pos + 3]
        w, b = w_ref[...], b_ref[...]
        acc[...] = jnp.zeros(acc.shape, F32)

        def load(s, n):
            dy = d_refs[0][0, pl.ds(s, n), :]
            for r in d_refs[1:]:
                dy = dy + r[0, pl.ds(s, n), :]
            if has_s:
                dy = dy + s_ref[0, pl.ds(s, n), :] * scale
            return jnp.concatenate([x_ref[0, pl.ds(s, n), :], dy], axis=1)

        def work(start, both):
            xh, dyh = both[:, 0:wc], both[:, wc:]
            taps = [_rows_at(xh, k - 2) for k in range(4)]
            pre = b
            for k in range(4):
                pre = pre + taps[k] * w[k:k + 1]
            dpre = dyh * _dsilu(pre)
            dx = dpre * w[2:3]
            for k in (0, 1, 3):
                dx = dx + _rows_at(dpre, 2 - k) * w[k:k + 1]
            dx_ref[0, pl.ds(start, CONV_ROWS), :] = dx[mid].astype(dx_ref.dtype)
            dm = dpre[mid]
            acc[...] += _stack_rows([_colsum(dm * taps[k][mid]) for k in range(4)] + [_colsum(dm)] + [jnp.zeros((1, wc), F32)] * 3)

        _halo_chunks(L, load, work)
        first = pl.program_id(1) == 0

        @pl.when(first)
        def _():
            dw_ref[...] = acc[0:4]
            db_ref[...] = acc[4:5]

        @pl.when(jnp.logical_not(first))
        def _():
            dw_ref[...] += acc[0:4]
            db_ref[...] += acc[4:5]

    slab = lambda off: pl.BlockSpec((1, L, wc), lambda j, b, off=off: (b, 0, off + j))
    in_specs = [slab(c0)] + [slab(0)] * n_d + ([slab(0)] if has_s else [])
    in_specs += [pl.BlockSpec((4, wc), lambda j, b: (0, c0 + j)), pl.BlockSpec((1, wc), lambda j, b: (0, c0 + j))]
    args = [xbc_raw, *dparts] + ([scaled[0]] if has_s else []) + [conv_w, conv_b]
    if has_s:
        in_specs.append(pl.BlockSpec((1, wc), lambda j, b: (0, j)))
        args.append(scaled[1])
    return pl.pallas_call(
        kern, name=name,
        out_shape=[jax.ShapeDtypeStruct((Bn, L, width), BF16), jax.ShapeDtypeStruct((4, width), F32), jax.ShapeDtypeStruct((1, width), F32)],
        grid=(width // wc, Bn), in_specs=in_specs,
        out_specs=[pl.BlockSpec((1, L, wc), lambda j, b: (b, 0, j)), pl.BlockSpec((4, wc), lambda j, b: (0, j)),
                   pl.BlockSpec((1, wc), lambda j, b: (0, j))],
        scratch_shapes=[pltpu.VMEM((8, wc), F32)],
        compiler_params=_cp(("arbitrary", "arbitrary")))(*args)


def _box_mean(x, k, step, pos, n, L, transpose):
    lo, hi = k // 2, k - 1 - k // 2
    cnt = (jnp.minimum(pos + hi + 1, n) - jnp.maximum(pos - lo, 0)).astype(F32)
    if transpose:
        x = x / cnt
        lo, hi = hi, lo
    acc = x
    for o in range(-lo, hi + 1):
        if o == 0:
            continue
        rolled = pltpu.roll(x, (-o * step) % L, 0)
        acc = acc + jnp.where((pos + o >= 0) & (pos + o < n), rolled, 0.0)
    return acc if transpose else acc / cnt


def pool_diff(name, v, col0, gi, transpose):
    L = v.shape[1]
    rows = L // GRID_W
    k = POOL_WINDOWS[gi]

    def body(x):
        tok = lax.broadcasted_iota(jnp.int32, x.shape, 0)
        col = tok & (GRID_W - 1)
        row = tok >> 6
        if not transpose:
            m = _box_mean(x, k, GRID_W, row, rows, L, False)
            m = _box_mean(m, k, 1, col, GRID_W, L, False)
        else:
            m = _box_mean(x, k, 1, col, GRID_W, L, True)
            m = _box_mean(m, k, GRID_W, row, rows, L, True)
        return m - x

    return slab_call(name, body, [(v, col0)], [], [(POOL_GROUP, BF16)], [])[0]


def pool_mix_fwd(name, dgs, z_pool, pool_w, pool_scale):
    def body(d0, d1, d2, d3, z, w, scale):
        q = jnp.concatenate([_dot(d, w[g * POOL_GROUP:(g + 1) * POOL_GROUP]) for g, d in enumerate((d0, d1, d2, d3))], axis=1)
        return q * scale * _silu(z)

    return tok_call(name, body, list(dgs) + [z_pool], [], [pool_w, pool_scale], [(D, BF16)], [], [])[0]


def pool_mix_bwd(name, dgs, z_pool, dyp, pool_w, pool_scale):
    def body(d0, d1, d2, d3, z, dyp, w, scale):
        ds = (d0, d1, d2, d3)
        q = jnp.concatenate([_dot(d, w[g * POOL_GROUP:(g + 1) * POOL_GROUP]) for g, d in enumerate(ds)], axis=1)
        dypm = dyp * _silu(z)
        dz = dyp * (q * scale) * _dsilu(z)
        dq = (dypm * scale).astype(BF16)
        dds, gws = [], []
        for g, d in enumerate(ds):
            dqg = dq[:, g * POOL_GROUP:(g + 1) * POOL_GROUP]
            dds.append(_dot_nt(dqg, w[g * POOL_GROUP:(g + 1) * POOL_GROUP]))
            gws.append(_dot_tn(d, dqg))
        return (*dds, dz, jnp.concatenate(gws, axis=0), _colsum(dypm * q))

    return tok_call(name, body, list(dgs) + [z_pool, dyp], [], [pool_w, pool_scale],
                    [(POOL_GROUP, F32)] * 4 + [(D, BF16)], [], [(D, POOL_GROUP), (1, D)])


def _cumsum_lanes(a, reverse):
    n = a.shape[1]
    k = lax.broadcasted_iota(jnp.int32, (n, n), 0)
    i = lax.broadcasted_iota(jnp.int32, (n, n), 1)
    tri = jnp.where((k >= i) if reverse else (k <= i), 1.0, 0.0).astype(BF16)
    return _dot_exact01(a, tri)


def _rows_to_cols(rows):
    r = rows.shape[0]
    if r < LANES:
        rows = jnp.concatenate([rows, jnp.zeros((LANES - r, rows.shape[1]), F32)], axis=0)
    return rows.T


def _cols_to_rows(cols):
    q = cols[0].shape[0]
    lane = lax.broadcasted_iota(jnp.int32, (q, LANES), 1)
    acc = jnp.zeros((q, LANES), F32)
    for r, c in enumerate(cols):
        acc = acc + jnp.where(lane == r, c, 0.0)
    return acc.T[0:len(cols)]


def _ssd_scalars(dtraw, bias, alog, reverse):
    dt = _softplus(dtraw + bias)
    A = -jnp.exp(alog)
    cs = _cumsum_lanes(dt * A, reverse)
    total = cs[:, 0:1] if reverse else cs[:, CHUNK - 1:CHUNK]
    return dt, A, cs, total


def _decay_matrix(cs_col, cs_row, reverse):
    i = lax.broadcasted_iota(jnp.int32, (CHUNK, CHUNK), 0)
    j = lax.broadcasted_iota(jnp.int32, (CHUNK, CHUNK), 1)
    keep = (i <= j) if reverse else (i >= j)
    return jnp.exp(jnp.where(keep, cs_col - cs_row, -jnp.inf))


def ssd_fwd_v1(name, dtT, bias, alog, xbc, h0, direction, with_y):
    Bn, L = xbc.shape[:2]
    nc = L // CHUNK
    reverse = direction == 1
    rowblk = direction * N_BC

    def chunk_of(s):
        return (nc - 1 - s) if reverse else s

    def kern(dt_ref, bias_ref, alog_ref, x_ref, b_ref, c_ref, h0_ref, *rest):
        if with_y:
            y_ref, hs_ref, hf_ref, h_scr, xt_scr = rest
        else:
            hs_ref, hf_ref, h_scr, xt_scr = rest
        s = pl.program_id(2)

        @pl.when(s == 0)
        def _():
            h_scr[...] = h0_ref[0, 0]

        dt, _, cs, total = _ssd_scalars(dt_ref[0], bias_ref[0], alog_ref[0], reverse)
        e_row = jnp.exp(cs)
        t_row = jnp.exp(total - cs)
        dc = jnp.exp(total)
        cols = _rows_to_cols(jnp.concatenate([dt, e_row, t_row, cs], axis=0))
        x = x_ref[0]
        bm = b_ref[0].astype(BF16)
        cm = c_ref[0].astype(BF16)
        h = h_scr[...]
        hs_ref[0, 0, 0] = h
        if with_y:
            cb = _dot_nt(cm, bm)
            yoff = _dot(cm, h.astype(BF16))
        for r in range(HPG):
            sl = slice(r * HEAD_DIM, (r + 1) * HEAD_DIM)
            xdt = x[:, sl] * cols[:, r:r + 1]
            if with_y:
                lr = _decay_matrix(cols[:, 3 * HPG + r:3 * HPG + r + 1], cs[r:r + 1], reverse)
                ydiag = _dot((cb * lr).astype(BF16), xdt.astype(BF16))
                y_ref[0, :, sl] = ydiag + yoff[:, sl] * cols[:, HPG + r:HPG + r + 1]
            xt_scr[:, sl] = (xdt * cols[:, 2 * HPG + r:2 * HPG + r + 1]).astype(BF16)
        st = _dot_tn(bm, xt_scr[...])
        for r in range(HPG):
            sl = slice(r * HEAD_DIM, (r + 1) * HEAD_DIM)
            h_scr[:, sl] = h[:, sl] * dc[r:r + 1] + st[:, sl]

        @pl.when(s == nc - 1)
        def _():
            hf_ref[0, 0] = h_scr[...]

    in_specs = [
        pl.BlockSpec((1, HPG, CHUNK), lambda b, g, s: (b, rowblk + g, chunk_of(s))),
        pl.BlockSpec((1, HPG, 1), lambda b, g, s: (rowblk + g, 0, 0)),
        pl.BlockSpec((1, HPG, 1), lambda b, g, s: (rowblk + g, 0, 0)),
        pl.BlockSpec((1, CHUNK, GW), lambda b, g, s: (b, chunk_of(s), g)),
        pl.BlockSpec((1, CHUNK, D_STATE), lambda b, g, s: (b, chunk_of(s), D_INNER // D_STATE + g)),
        pl.BlockSpec((1, CHUNK, D_STATE), lambda b, g, s: (b, chunk_of(s), D_INNER // D_STATE + N_BC + g)),
        pl.BlockSpec((1, 1, D_STATE, GW), lambda b, g, s: (b, g, 0, 0)),
    ]
    out_shape, out_specs = [], []
    if with_y:
        out_shape.append(jax.ShapeDtypeStruct((Bn, L, D_INNER), F32))
        out_specs.append(pl.BlockSpec((1, CHUNK, GW), lambda b, g, s: (b, chunk_of(s), g)))
    out_shape += [jax.ShapeDtypeStruct((Bn, N_BC, nc, D_STATE, GW), F32), jax.ShapeDtypeStruct((Bn, N_BC, D_STATE, GW), F32)]
    out_specs += [pl.BlockSpec((1, 1, 1, D_STATE, GW), lambda b, g, s: (b, g, chunk_of(s), 0, 0)),
                  pl.BlockSpec((1, 1, D_STATE, GW), lambda b, g, s: (b, g, 0, 0))]
    return pl.pallas_call(
        kern, name=name, out_shape=out_shape, grid=(Bn, N_BC, nc), in_specs=in_specs, out_specs=out_specs,
        scratch_shapes=[pltpu.VMEM((D_STATE, GW), F32), pltpu.VMEM((CHUNK, GW), BF16)],
        compiler_params=_cp(("arbitrary", "arbitrary", "arbitrary")))(dtT, bias, alog, xbc, xbc, xbc, h0)


def ssd_bwd_v1(name, dtT, bias, alog, xbc, h_start, dy, dh_final, direction):
    Bn, L = xbc.shape[:2]
    nc = L // CHUNK
    reverse = direction == 1
    rowblk = direction * N_BC
    has_y = dy is not None
    last = 0 if reverse else CHUNK - 1

    def chunk_of(s):
        return s if reverse else (nc - 1 - s)

    def kern(*refs):
        if has_y:
            (dt_ref, bias_ref, alog_ref, x_ref, b_ref, c_ref, hs_ref, dhf_ref, dy_ref,
             dx_ref, db_ref, dc_ref, ddt_ref, dbias_ref, dalog_ref, dh0_ref, dh_scr, e_scr, t_scr) = refs
        else:
            (dt_ref, bias_ref, alog_ref, x_ref, b_ref, hs_ref, dhf_ref,
             dx_ref, db_ref, ddt_ref, dbias_ref, dalog_ref, dh0_ref, dh_scr, t_scr) = refs
        s = pl.program_id(2)

        @pl.when(s == 0)
        def _():
            dh_scr[...] = dhf_ref[0, 0]
            dbias_ref[...] = jnp.zeros(dbias_ref.shape, F32)
            dalog_ref[...] = jnp.zeros(dalog_ref.shape, F32)

        dtraw = dt_ref[0]
        dt, A, cs, total = _ssd_scalars(dtraw, bias_ref[0], alog_ref[0], reverse)
        e_row = jnp.exp(cs)
        t_row = jnp.exp(total - cs)
        dcy = jnp.exp(total)
        cols = _rows_to_cols(jnp.concatenate([dt, e_row, t_row, cs], axis=0))
        x = x_ref[0]
        bm = b_ref[0].astype(BF16)
        h = hs_ref[0, 0, 0]
        dh = dh_scr[...]
        dh_bf = dh.astype(BF16)
        bdh = _dot(bm, dh_bf)
        if has_y:
            cm = c_ref[0].astype(BF16)
            dyv = dy_ref[0]
            cb = _dot_nt(cm, bm)
            yoff = _dot(cm, h.astype(BF16))
            dcb = jnp.zeros((CHUNK, CHUNK), F32)
        col_terms, row_terms, ddt_cols, dtot = [], [], [], []
        for r in range(HPG):
            sl = slice(r * HEAD_DIM, (r + 1) * HEAD_DIM)
            dt_c = cols[:, r:r + 1]
            e_c = cols[:, HPG + r:HPG + r + 1]
            t_c = cols[:, 2 * HPG + r:2 * HPG + r + 1]
            xr = x[:, sl]
            xdt = xr * dt_c
            dxdt = t_c * bdh[:, sl]
            d_t = jnp.sum(bdh[:, sl] * xdt, axis=1, keepdims=True)
            col = -(t_c * d_t)
            tot = jnp.sum(t_c * d_t, axis=0, keepdims=True) + dcy[r:r + 1] * jnp.sum(h[:, sl] * dh[:, sl], keepdims=True)
            if has_y:
                dyr = dyv[:, sl]
                lr = _decay_matrix(cols[:, 3 * HPG + r:3 * HPG + r + 1], cs[r:r + 1], reverse)
                w = cb * lr
                gm = _dot_nt(dyr.astype(BF16), xdt.astype(BF16))
                m = gm * w
                dcb = dcb + gm * lr
                dxdt = dxdt + _dot_tn(w.astype(BF16), dyr.astype(BF16))
                col = col + jnp.sum(m, axis=1, keepdims=True) + jnp.sum(yoff[:, sl] * dyr, axis=1, keepdims=True) * e_c
                row_terms.append(-jnp.sum(m, axis=0, keepdims=True))
                e_scr[:, sl] = (e_c * dyr).astype(BF16)
            t_scr[:, sl] = (t_c * xdt).astype(BF16)
            dx_ref[0, :, sl] = dxdt * dt_c
            ddt_cols.append(jnp.sum(dxdt * xr, axis=1, keepdims=True))
            col_terms.append(col)
            dtot.append(tot)
        db = _dot_nt(t_scr[...], dh_bf)
        if has_y:
            dcb_bf = dcb.astype(BF16)
            db = db + _dot_tn(dcb_bf, cm)
            dc_ref[0] = _dot(dcb_bf, bm) + _dot_nt(e_scr[...], h.astype(BF16))
            cte = _dot_tn(cm, e_scr[...])
        db_ref[0] = db
        for r in range(HPG):
            sl = slice(r * HEAD_DIM, (r + 1) * HEAD_DIM)
            new = dh[:, sl] * dcy[r:r + 1]
            if has_y:
                new = new + cte[:, sl]
            dh_scr[:, sl] = new
        dcs = _cols_to_rows(col_terms)
        if has_y:
            dcs = dcs + _stack_rows(row_terms)
        lane = lax.broadcasted_iota(jnp.int32, (HPG, CHUNK), 1)
        dcs = dcs + jnp.where(lane == last, _stack_rows([jnp.broadcast_to(t, (1, CHUNK)) for t in dtot]), 0.0)
        da = _cumsum_lanes(dcs, not reverse)
        ddt = da * A + _cols_to_rows(ddt_cols)
        ddtraw = ddt * _sigmoid(dtraw + bias_ref[0])
        ddt_ref[0] = ddtraw
        dbias_ref[0, 0] += jnp.sum(ddtraw, axis=1, keepdims=True)
        dalog_ref[0, 0] += jnp.sum(da * dt, axis=1, keepdims=True) * A

        @pl.when(s == nc - 1)
        def _():
            dh0_ref[0, 0] = dh_scr[...]

    cidx = lambda b, g, s: (b, chunk_of(s), g)
    in_specs = [
        pl.BlockSpec((1, HPG, CHUNK), lambda b, g, s: (b, rowblk + g, chunk_of(s))),
        pl.BlockSpec((1, HPG, 1), lambda b, g, s: (rowblk + g, 0, 0)),
        pl.BlockSpec((1, HPG, 1), lambda b, g, s: (rowblk + g, 0, 0)),
        pl.BlockSpec((1, CHUNK, GW), cidx),
        pl.BlockSpec((1, CHUNK, D_STATE), lambda b, g, s: (b, chunk_of(s), D_INNER // D_STATE + g)),
    ]
    args = [dtT, bias, alog, xbc, xbc]
    if has_y:
        in_specs.append(pl.BlockSpec((1, CHUNK, D_STATE), lambda b, g, s: (b, chunk_of(s), D_INNER // D_STATE + N_BC + g)))
        args.append(xbc)
    in_specs += [pl.BlockSpec((1, 1, 1, D_STATE, GW), lambda b, g, s: (b, g, chunk_of(s), 0, 0)),
                 pl.BlockSpec((1, 1, D_STATE, GW), lambda b, g, s: (b, g, 0, 0))]
    args += [h_start, dh_final]
    if has_y:
        in_specs.append(pl.BlockSpec((1, CHUNK, GW), cidx))
        args.append(dy)
    out_shape = [jax.ShapeDtypeStruct((Bn, L, D_INNER), F32), jax.ShapeDtypeStruct((Bn, L, N_BC * D_STATE), F32)]
    out_specs = [pl.BlockSpec((1, CHUNK, GW), cidx), pl.BlockSpec((1, CHUNK, D_STATE), cidx)]
    if has_y:
        out_shape.append(jax.ShapeDtypeStruct((Bn, L, N_BC * D_STATE), F32))
        out_specs.append(pl.BlockSpec((1, CHUNK, D_STATE), cidx))
    out_shape += [jax.ShapeDtypeStruct((Bn, N_HEADS, L), F32), jax.ShapeDtypeStruct((Bn, N_BC, HPG, 1), F32),
                  jax.ShapeDtypeStruct((Bn, N_BC, HPG, 1), F32), jax.ShapeDtypeStruct((Bn, N_BC, D_STATE, GW), F32)]
    out_specs += [pl.BlockSpec((1, HPG, CHUNK), lambda b, g, s: (b, g, chunk_of(s))),
                  pl.BlockSpec((1, 1, HPG, 1), lambda b, g, s: (b, g, 0, 0)),
                  pl.BlockSpec((1, 1, HPG, 1), lambda b, g, s: (b, g, 0, 0)),
                  pl.BlockSpec((1, 1, D_STATE, GW), lambda b, g, s: (b, g, 0, 0))]
    scratch = [pltpu.VMEM((D_STATE, GW), F32)] + ([pltpu.VMEM((CHUNK, GW), BF16)] if has_y else []) + [pltpu.VMEM((CHUNK, GW), BF16)]
    res = pl.pallas_call(
        kern, name=name, out_shape=out_shape, grid=(Bn, N_BC, nc), in_specs=in_specs, out_specs=out_specs,
        scratch_shapes=scratch, compiler_params=_cp(("arbitrary", "arbitrary", "arbitrary")))(*args)
    if has_y:
        return res
    dxs, db, ddt, dbias, dalog, dh0 = res
    return dxs, db, None, ddt, dbias, dalog, dh0


def _tri_mask(transposed, reverse):
    sub = lax.broadcasted_iota(jnp.int32, (CHUNK, CHUNK), 0)
    lane = lax.broadcasted_iota(jnp.int32, (CHUNK, CHUNK), 1)
    i, j = (lane, sub) if transposed else (sub, lane)
    return (i <= j) if reverse else (i >= j)


def ssd_fwd(name, dtT, bias, alog, xbc, h0, direction, with_y):
    Bn, L = xbc.shape[:2]
    nc = L // CHUNK
    reverse = direction == 1
    rowblk = direction * N_BC

    def chunk_of(s):
        return (nc - 1 - s) if reverse else s

    def kern(dt_ref, bias_ref, alog_ref, x_ref, b_ref, c_ref, h0_ref, *rest):
        if with_y:
            y_ref, hs_ref, hf_ref, h_scr = rest
        else:
            hs_ref, hf_ref, h_scr = rest
        s = pl.program_id(2)

        @pl.when(s == 0)
        def _():
            h_scr[...] = h0_ref[0, 0]

        dt, _, cs, total = _ssd_scalars(dt_ref[0], bias_ref[0], alog_ref[0], reverse)
        u = cs - jnp.log(dt)
        dtt = jnp.exp(total - u)
        dc = jnp.exp(total)
        x_bf = x_ref[0].astype(BF16)
        bm = b_ref[0]
        h = h_scr[...]
        h_bf = h.astype(BF16)
        hs_ref[0, 0, 0] = h
        bt = bm.T
        if with_y:
            cm = c_ref[0]
            cb = _dot_nt(cm.astype(BF16), bm.astype(BF16))
            cs_cols = _rows_to_cols(cs)
            keep = _tri_mask(False, reverse)
        first = lax.broadcasted_iota(jnp.int32, (1, LANES), 1) < HEAD_DIM
        heads = range(HPG)
        psl = [slice((r // 2) * LANES, (r // 2 + 1) * LANES) for r in heads]
        lhs = []
        if with_y:
            for r in heads:
                cs_col = jnp.broadcast_to(cs_cols[:, r:r + 1], (CHUNK, LANES))
                wf = cb * jnp.exp(jnp.where(keep, cs_col - u[r:r + 1], -jnp.inf))
                lhs.append(jnp.concatenate([wf.astype(BF16), (cm * jnp.exp(cs_col)).astype(BF16)], axis=1))
        bts = [(bt * dtt[r:r + 1]).astype(BF16) for r in heads]
        sts = [_dot(bts[r], x_bf[:, psl[r]]) for r in heads]
        if with_y:
            ys = [_dot(lhs[r], jnp.concatenate([x_bf[:, psl[r]], h_bf[:, psl[r]]], axis=0)) for r in heads]
        for p in range(HPG // 2):
            if with_y:
                y_ref[0, :, psl[2 * p]] = jnp.where(first, ys[2 * p], ys[2 * p + 1])
            dc_p = jnp.where(first, dc[2 * p:2 * p + 1], dc[2 * p + 1:2 * p + 2])
            h_scr[:, psl[2 * p]] = h[:, psl[2 * p]] * dc_p + jnp.where(first, sts[2 * p], sts[2 * p + 1])

        @pl.when(s == nc - 1)
        def _():
            hf_ref[0, 0] = h_scr[...]

    in_specs = [
        pl.BlockSpec((1, HPG, CHUNK), lambda b, g, s: (b, rowblk + g, chunk_of(s))),
        pl.BlockSpec((1, HPG, 1), lambda b, g, s: (rowblk + g, 0, 0)),
        pl.BlockSpec((1, HPG, 1), lambda b, g, s: (rowblk + g, 0, 0)),
        pl.BlockSpec((1, CHUNK, GW), lambda b, g, s: (b, chunk_of(s), g)),
        pl.BlockSpec((1, CHUNK, D_STATE), lambda b, g, s: (b, chunk_of(s), D_INNER // D_STATE + g)),
        pl.BlockSpec((1, CHUNK, D_STATE), lambda b, g, s: (b, chunk_of(s), D_INNER // D_STATE + N_BC + g)),
        pl.BlockSpec((1, 1, D_STATE, GW), lambda b, g, s: (b, g, 0, 0)),
    ]
    out_shape, out_specs = [], []
    if with_y:
        out_shape.append(jax.ShapeDtypeStruct((Bn, L, D_INNER), F32))
        out_specs.append(pl.BlockSpec((1, CHUNK, GW), lambda b, g, s: (b, chunk_of(s), g)))
    out_shape += [jax.ShapeDtypeStruct((Bn, N_BC, nc, D_STATE, GW), F32), jax.ShapeDtypeStruct((Bn, N_BC, D_STATE, GW), F32)]
    out_specs += [pl.BlockSpec((1, 1, 1, D_STATE, GW), lambda b, g, s: (b, g, chunk_of(s), 0, 0)),
                  pl.BlockSpec((1, 1, D_STATE, GW), lambda b, g, s: (b, g, 0, 0))]
    return pl.pallas_call(
        kern, name=name, out_shape=out_shape, grid=(Bn, N_BC, nc), in_specs=in_specs, out_specs=out_specs,
        scratch_shapes=[pltpu.VMEM((D_STATE, GW), F32)],
        compiler_params=_cp(("arbitrary", "arbitrary", "arbitrary")))(dtT, bias, alog, xbc, xbc, xbc, h0)


def ssd_bwd(name, dtT, bias, alog, xbc, h_start, dy, dh_final, direction):
    Bn, L = xbc.shape[:2]
    nc = L // CHUNK
    reverse = direction == 1
    rowblk = direction * N_BC
    has_y = dy is not None
    last = 0 if reverse else CHUNK - 1

    def chunk_of(s):
        return s if reverse else (nc - 1 - s)

    def kern(*refs):
        if has_y:
            (dt_ref, bias_ref, alog_ref, x_ref, b_ref, hs_ref, dhf_ref, c_ref, dy_ref,
             dx_ref, db_ref, ddt_ref, dbias_ref, dalog_ref, dh0_ref, dc_ref, dh_scr) = refs
        else:
            (dt_ref, bias_ref, alog_ref, x_ref, b_ref, hs_ref, dhf_ref,
             dx_ref, db_ref, ddt_ref, dbias_ref, dalog_ref, dh0_ref, dh_scr) = refs
        s = pl.program_id(2)

        @pl.when(s == 0)
        def _():
            dh_scr[...] = dhf_ref[0, 0]
            dbias_ref[...] = jnp.zeros(dbias_ref.shape, F32)
            dalog_ref[...] = jnp.zeros(dalog_ref.shape, F32)

        dtraw = dt_ref[0]
        dt, A, cs, total = _ssd_scalars(dtraw, bias_ref[0], alog_ref[0], reverse)
        u = cs - jnp.log(dt)
        dtt = jnp.exp(total - u)
        dcy = jnp.exp(total)
        u_cols = _rows_to_cols(u)
        x_bf = x_ref[0].astype(BF16)
        bm = b_ref[0]
        bt = bm.T
        h = hs_ref[0, 0, 0]
        dh = dh_scr[...]
        dh_bf = dh.astype(BF16)
        dbt = jnp.zeros((D_STATE, CHUNK), F32)
        if has_y:
            cm = c_ref[0]
            ct = cm.T
            e_row = jnp.exp(cs)
            dy_bf = dy_ref[0].astype(BF16)
            h_bf = h.astype(BF16)
            cbt = _dot_nt(bm.astype(BF16), cm.astype(BF16))
            keep = _tri_mask(True, reverse)
            dcbt = jnp.zeros((CHUNK, CHUNK), F32)
            dct = jnp.zeros((D_STATE, CHUNK), F32)
        tots, out_rows, in_rows, in_cols = [], [], [], []
        first = lax.broadcasted_iota(jnp.int32, (1, LANES), 1) < HEAD_DIM
        heads = range(HPG)
        psl = [slice((r // 2) * LANES, (r // 2 + 1) * LANES) for r in heads]
        mine = [first if r % 2 == 0 else jnp.logical_not(first) for r in heads]
        zeros_bf = jnp.zeros((CHUNK, LANES), BF16)

        def prep(r):
            u_col = jnp.broadcast_to(u_cols[:, r:r + 1], (CHUNK, LANES))
            bs = (bm * jnp.exp(total[r:r + 1] - u_col)).astype(BF16)
            if not has_y:
                return bs, None
            et = jnp.exp(jnp.where(keep, cs[r:r + 1] - u_col, -jnp.inf))
            return jnp.concatenate([(cbt * et).astype(BF16), bs], axis=1), et

        def matmuls(r, lhs):
            p2raw = _dot_nt(dh_bf[:, psl[r]], jnp.where(mine[r], x_bf[:, psl[r]], zeros_bf))
            if not has_y:
                return p2raw, None, None, _dot(lhs, dh_bf[:, psl[r]])
            a1 = _dot_nt(jnp.concatenate([x_bf[:, psl[r]], h_bf[:, psl[r]]], axis=0),
                         jnp.where(mine[r], dy_bf[:, psl[r]], zeros_bf))
            new = _dot((ct * e_row[r:r + 1]).astype(BF16), dy_bf[:, psl[r]])
            dx = _dot(lhs, jnp.concatenate([dy_bf[:, psl[r]], dh_bf[:, psl[r]]], axis=0))
            return p2raw, a1, new, dx

        def post(r, p2raw, a1, et, dbt, dcbt, dct):
            if has_y:
                pt = a1[0:CHUNK] * et
                dcbt = dcbt + pt
                mt = pt * cbt
                ph = a1[CHUNK:] * e_row[r:r + 1]
                dct = dct + ph
                out_rows.append(_colsum(mt + ct * ph))
                in_cols.append(jnp.sum(mt, axis=1, keepdims=True))
            p2 = p2raw * dtt[r:r + 1]
            dbt = dbt + p2
            t_term = _colsum(bt * p2)
            in_rows.append(t_term)
            hdh = h[:, psl[r]] * dh[:, psl[r]]
            tot = jnp.sum(t_term, axis=1, keepdims=True) + dcy[r:r + 1] * jnp.sum(jnp.where(mine[r], hdh, 0.0), keepdims=True)
            tots.append(jnp.broadcast_to(tot, (1, CHUNK)))
            return dbt, dcbt, dct

        if not has_y:
            dcbt = dct = None
        dxs, news, pending = [], [], []
        batch = HPG
        for r0 in range(0, HPG, batch):
            preps = [prep(r) for r in range(r0, r0 + batch)]
            mms = [matmuls(r, preps[r - r0][0]) for r in range(r0, r0 + batch)]
            for args in pending:
                dbt, dcbt, dct = post(*args, dbt, dcbt, dct)
            pending = [(r, mms[r - r0][0], mms[r - r0][1], preps[r - r0][1]) for r in range(r0, r0 + batch)]
            dxs += [m[3] for m in mms]
            news += [m[2] for m in mms]
        for args in pending:
            dbt, dcbt, dct = post(*args, dbt, dcbt, dct)
        for p in range(HPG // 2):
            dx_ref[0, :, psl[2 * p]] = jnp.where(first, dxs[2 * p], dxs[2 * p + 1])
            new = dh[:, psl[2 * p]] * jnp.where(first, dcy[2 * p:2 * p + 1], dcy[2 * p + 1:2 * p + 2])
            if has_y:
                new = new + jnp.where(first, news[2 * p], news[2 * p + 1])
            dh_scr[:, psl[2 * p]] = new
        db = dbt.T
        if has_y:
            dcbt_bf = dcbt.astype(BF16)
            db = db + _dot(dcbt_bf, cm.astype(BF16))
            dc_ref[0] = dct.T + _dot_tn(dcbt_bf, bm.astype(BF16))
        db_ref[0] = db
        s_row = _stack_rows(in_rows)
        lane = lax.broadcasted_iota(jnp.int32, (HPG, CHUNK), 1)
        dcs = jnp.where(lane == last, _stack_rows(tots), 0.0)
        if has_y:
            s_row = s_row + _cols_to_rows(in_cols)
            dcs = dcs + _stack_rows(out_rows)
        dcs = dcs - s_row
        da = _cumsum_lanes(dcs, not reverse)
        ddt = da * A + jnp.where(dt > 0.0, s_row / dt, 0.0)
        ddtraw = ddt * _sigmoid(dtraw + bias_ref[0])
        ddt_ref[0] = ddtraw
        dbias_ref[0, 0] += jnp.sum(ddtraw, axis=1, keepdims=True)
        dalog_ref[0, 0] += jnp.sum(da * dt, axis=1, keepdims=True) * A

        @pl.when(s == nc - 1)
        def _():
            dh0_ref[0, 0] = dh_scr[...]

    cidx = lambda b, g, s: (b, chunk_of(s), g)
    hidx = lambda b, g, s: (b, g, 0, 0)
    in_specs = [
        pl.BlockSpec((1, HPG, CHUNK), lambda b, g, s: (b, rowblk + g, chunk_of(s))),
        pl.BlockSpec((1, HPG, 1), lambda b, g, s: (rowblk + g, 0, 0)),
        pl.BlockSpec((1, HPG, 1), lambda b, g, s: (rowblk + g, 0, 0)),
        pl.BlockSpec((1, CHUNK, GW), cidx),
        pl.BlockSpec((1, CHUNK, D_STATE), lambda b, g, s: (b, chunk_of(s), D_INNER // D_STATE + g)),
        pl.BlockSpec((1, 1, 1, D_STATE, GW), lambda b, g, s: (b, g, chunk_of(s), 0, 0)),
        pl.BlockSpec((1, 1, D_STATE, GW), hidx),
    ]
    args = [dtT, bias, alog, xbc, xbc, h_start, dh_final]
    if has_y:
        in_specs += [pl.BlockSpec((1, CHUNK, D_STATE), lambda b, g, s: (b, chunk_of(s), D_INNER // D_STATE + N_BC + g)),
                     pl.BlockSpec((1, CHUNK, GW), cidx)]
        args += [xbc, dy]
    out_shape = [jax.ShapeDtypeStruct((Bn, L, D_INNER), F32), jax.ShapeDtypeStruct((Bn, L, N_BC * D_STATE), F32),
                 jax.ShapeDtypeStruct((Bn, N_HEADS, L), F32), jax.ShapeDtypeStruct((Bn, N_BC, HPG, 1), F32),
                 jax.ShapeDtypeStruct((Bn, N_BC, HPG, 1), F32), jax.ShapeDtypeStruct((Bn, N_BC, D_STATE, GW), F32)]
    out_specs = [pl.BlockSpec((1, CHUNK, GW), cidx), pl.BlockSpec((1, CHUNK, D_STATE), cidx),
                 pl.BlockSpec((1, HPG, CHUNK), lambda b, g, s: (b, g, chunk_of(s))),
                 pl.BlockSpec((1, 1, HPG, 1), hidx), pl.BlockSpec((1, 1, HPG, 1), hidx), pl.BlockSpec((1, 1, D_STATE, GW), hidx)]
    if has_y:
        out_shape.append(jax.ShapeDtypeStruct((Bn, L, N_BC * D_STATE), F32))
        out_specs.append(pl.BlockSpec((1, CHUNK, D_STATE), cidx))
    res = pl.pallas_call(
        kern, name=name, out_shape=out_shape, grid=(Bn, N_BC, nc), in_specs=in_specs, out_specs=out_specs,
        scratch_shapes=[pltpu.VMEM((D_STATE, GW), F32)],
        compiler_params=_cp(("arbitrary", "arbitrary", "arbitrary")))(*args)
    dxs, db, ddt, dbias, dalog, dh0 = res[:6]
    return dxs, db, (res[6] if has_y else None), ddt, dbias, dalog, dh0


GPS = 4


def ssd_fwd3(name, dtT, bias, alog, xbc, h0, direction, with_y, y_add=None):
    Bn, L = xbc.shape[:2]
    nc = L // CHUNK
    reverse = direction == 1
    blk0 = direction * (N_BC // GPS)
    gs = range(GPS)
    has_add = y_add is not None

    def chunk_of(s):
        return (nc - 1 - s) if reverse else s

    def kern(dt_ref, bias_ref, alog_ref, x_ref, b_ref, c_ref, h0_ref, *rest):
        if has_add:
            yp_ref, dsk_ref, rest = rest[0], rest[1], rest[2:]
        if with_y:
            y_ref, hs_ref, hf_ref, h_scr = rest
        else:
            hs_ref, hf_ref, h_scr = rest
        s = pl.program_id(2)

        @pl.when(s == 0)
        def _():
            h_scr[...] = h0_ref[0]

        first = lax.broadcasted_iota(jnp.int32, (1, LANES), 1) < HEAD_DIM
        heads = range(HPG)
        psl = [slice((r // 2) * LANES, (r // 2 + 1) * LANES) for r in heads]
        keep = _tri_mask(False, reverse)
        sc, x_bf, bm, h, h_bf, bt, cm, cb, cs_cols = [], [], [], [], [], [], [], [], []
        for g in gs:
            dt, _, cs, total = _ssd_scalars(dt_ref[0, g * HPG:(g + 1) * HPG], bias_ref[g], alog_ref[g], reverse)
            u = cs - jnp.log(dt)
            sc.append((cs, u, jnp.exp(total - u), jnp.exp(total)))
            x_bf.append(x_ref[0, :, g * GW:(g + 1) * GW].astype(BF16))
            bm.append(b_ref[0, :, g * D_STATE:(g + 1) * D_STATE])
            h.append(h_scr[g])
            h_bf.append(h[g].astype(BF16))
            hs_ref[0, g, 0] = h[g]
            bt.append(bm[g].T)
            if with_y:
                cm.append(c_ref[0, :, g * D_STATE:(g + 1) * D_STATE])
                cb.append(_dot_nt(cm[g].astype(BF16), bm[g].astype(BF16)))
                cs_cols.append(_rows_to_cols(cs))
        lhs = [[] for _ in gs]
        if with_y:
            for g in gs:
                cs, u = sc[g][0], sc[g][1]
                for r in heads:
                    cs_col = jnp.broadcast_to(cs_cols[g][:, r:r + 1], (CHUNK, LANES))
                    wf = cb[g] * jnp.exp(jnp.where(keep, cs_col - u[r:r + 1], -jnp.inf))
                    lhs[g].append(jnp.concatenate([wf.astype(BF16), (cm[g] * jnp.exp(cs_col)).astype(BF16)], axis=1))
        bts = [[(bt[g] * sc[g][2][r:r + 1]).astype(BF16) for r in heads] for g in gs]
        sts = [[_dot(bts[g][r], x_bf[g][:, psl[r]]) for r in heads] for g in gs]
        if with_y:
            ys = [[_dot(lhs[g][r], jnp.concatenate([x_bf[g][:, psl[r]], h_bf[g][:, psl[r]]], axis=0)) for r in heads] for g in gs]
        for g in gs:
            dc = sc[g][3]
            for p in range(HPG // 2):
                if with_y:
                    cols = slice(g * GW + p * LANES, g * GW + (p + 1) * LANES)
                    yv = jnp.where(first, ys[g][2 * p], ys[g][2 * p + 1])
                    if has_add:
                        yv = yv + yp_ref[0, :, cols] + dsk_ref[:, cols] * x_ref[0, :, cols]
                    y_ref[0, :, cols] = yv
                dc_p = jnp.where(first, dc[2 * p:2 * p + 1], dc[2 * p + 1:2 * p + 2])
                h_scr[g, :, psl[2 * p]] = h[g][:, psl[2 * p]] * dc_p + jnp.where(first, sts[g][2 * p], sts[g][2 * p + 1])

        @pl.when(s == nc - 1)
        def _():
            hf_ref[0] = h_scr[...]

    nb = D_INNER // (GPS * D_STATE)
    in_specs = [
        pl.BlockSpec((1, GPS * HPG, CHUNK), lambda b, g, s: (b, blk0 + g, chunk_of(s))),
        pl.BlockSpec((GPS, HPG, 1), lambda b, g, s: (blk0 + g, 0, 0)),
        pl.BlockSpec((GPS, HPG, 1), lambda b, g, s: (blk0 + g, 0, 0)),
        pl.BlockSpec((1, CHUNK, GPS * GW), lambda b, g, s: (b, chunk_of(s), g)),
        pl.BlockSpec((1, CHUNK, GPS * D_STATE), lambda b, g, s: (b, chunk_of(s), nb + g)),
        pl.BlockSpec((1, CHUNK, GPS * D_STATE), lambda b, g, s: (b, chunk_of(s), nb + N_BC // GPS + g)),
        pl.BlockSpec((1, GPS, D_STATE, GW), lambda b, g, s: (b, g, 0, 0)),
    ]
    args = [dtT, bias, alog, xbc, xbc, xbc, h0]
    if has_add:
        in_specs += [pl.BlockSpec((1, CHUNK, GPS * GW), lambda b, g, s: (b, chunk_of(s), g)),
                     pl.BlockSpec((1, GPS * GW), lambda b, g, s: (0, g))]
        args += list(y_add)
    out_shape, out_specs = [], []
    if with_y:
        out_shape.append(jax.ShapeDtypeStruct((Bn, L, D_INNER), F32))
        out_specs.append(pl.BlockSpec((1, CHUNK, GPS * GW), lambda b, g, s: (b, chunk_of(s), g)))
    out_shape += [jax.ShapeDtypeStruct((Bn, N_BC, nc, D_STATE, GW), F32), jax.ShapeDtypeStruct((Bn, N_BC, D_STATE, GW), F32)]
    out_specs += [pl.BlockSpec((1, GPS, 1, D_STATE, GW), lambda b, g, s: (b, g, chunk_of(s), 0, 0)),
                  pl.BlockSpec((1, GPS, D_STATE, GW), lambda b, g, s: (b, g, 0, 0))]
    return pl.pallas_call(
        kern, name=name, out_shape=out_shape, grid=(Bn, N_BC // GPS, nc), in_specs=in_specs, out_specs=out_specs,
        scratch_shapes=[pltpu.VMEM((GPS, D_STATE, GW), F32)],
        compiler_params=_cp(("arbitrary", "arbitrary", "arbitrary")))(*args)


def ssd_bwd3(name, dtT, bias, alog, xbc, h_start, dy, dh_final, direction):
    Bn, L = xbc.shape[:2]
    nc = L // CHUNK
    reverse = direction == 1
    blk0 = direction * (N_BC // GPS)
    has_y = dy is not None
    last = 0 if reverse else CHUNK - 1
    gs = range(GPS)

    def chunk_of(s):
        return s if reverse else (nc - 1 - s)

    def kern(*refs):
        if has_y:
            (dt_ref, bias_ref, alog_ref, x_ref, b_ref, hs_ref, dhf_ref, c_ref, dy_ref,
             dx_ref, db_ref, ddt_ref, dbias_ref, dalog_ref, dh0_ref, dc_ref, dh_scr) = refs
        else:
            (dt_ref, bias_ref, alog_ref, x_ref, b_ref, hs_ref, dhf_ref,
             dx_ref, db_ref, ddt_ref, dbias_ref, dalog_ref, dh0_ref, dh_scr) = refs
        s = pl.program_id(2)

        @pl.when(s == 0)
        def _():
            dh_scr[...] = dhf_ref[0]
            dbias_ref[...] = jnp.zeros(dbias_ref.shape, F32)
            dalog_ref[...] = jnp.zeros(dalog_ref.shape, F32)

        first = lax.broadcasted_iota(jnp.int32, (1, LANES), 1) < HEAD_DIM
        heads = range(HPG)
        psl = [slice((r // 2) * LANES, (r // 2 + 1) * LANES) for r in heads]
        mine = [first if r % 2 == 0 else jnp.logical_not(first) for r in heads]
        zeros_bf = jnp.zeros((CHUNK, LANES), BF16)
        keep = _tri_mask(True, reverse)
        ctx = []
        for g in gs:
            dtraw = dt_ref[0, g * HPG:(g + 1) * HPG]
            dt, A, cs, total = _ssd_scalars(dtraw, bias_ref[g], alog_ref[g], reverse)
            u = cs - jnp.log(dt)
            c = dict(dtraw=dtraw, dt=dt, A=A, cs=cs, total=total, u=u, dtt=jnp.exp(total - u), dcy=jnp.exp(total),
                     u_cols=_rows_to_cols(u), x_bf=x_ref[0, :, g * GW:(g + 1) * GW].astype(BF16),
                     bm=b_ref[0, :, g * D_STATE:(g + 1) * D_STATE], h=hs_ref[0, g, 0], dh=dh_scr[g])
            c["bt"] = c["bm"].T
            c["dh_bf"] = c["dh"].astype(BF16)
            if has_y:
                c["cm"] = c_ref[0, :, g * D_STATE:(g + 1) * D_STATE]
                c["ct"] = c["cm"].T
                c["e_row"] = jnp.exp(cs)
                c["dy_bf"] = dy_ref[0, :, g * GW:(g + 1) * GW].astype(BF16)
                c["h_bf"] = c["h"].astype(BF16)
                c["cbt"] = _dot_nt(c["bm"].astype(BF16), c["cm"].astype(BF16))
            ctx.append(c)
        for c in ctx:
            c["lhs"], c["et"] = [], []
            for r in heads:
                u_col = jnp.broadcast_to(c["u_cols"][:, r:r + 1], (CHUNK, LANES))
                bs = (c["bm"] * jnp.exp(c["total"][r:r + 1] - u_col)).astype(BF16)
                if has_y:
                    et = jnp.exp(jnp.where(keep, c["cs"][r:r + 1] - u_col, -jnp.inf))
                    c["et"].append(et)
                    c["lhs"].append(jnp.concatenate([(c["cbt"] * et).astype(BF16), bs], axis=1))
                else:
                    c["lhs"].append(bs)
        for c in ctx:
            c["p2raw"] = [_dot_nt(c["dh_bf"][:, psl[r]], jnp.where(mine[r], c["x_bf"][:, psl[r]], zeros_bf)) for r in heads]
            if has_y:
                c["a1"] = [_dot_nt(jnp.concatenate([c["x_bf"][:, psl[r]], c["h_bf"][:, psl[r]]], axis=0),
                                   jnp.where(mine[r], c["dy_bf"][:, psl[r]], zeros_bf)) for r in heads]
                c["news"] = [_dot((c["ct"] * c["e_row"][r:r + 1]).astype(BF16), c["dy_bf"][:, psl[r]]) for r in heads]
                c["dxs"] = [_dot(c["lhs"][r], jnp.concatenate([c["dy_bf"][:, psl[r]], c["dh_bf"][:, psl[r]]], axis=0)) for r in heads]
            else:
                c["dxs"] = [_dot(c["lhs"][r], c["dh_bf"][:, psl[r]]) for r in heads]
        for g, c in enumerate(ctx):
            dbt = jnp.zeros((D_STATE, CHUNK), F32)
            dcbt = jnp.zeros((CHUNK, CHUNK), F32)
            dct = jnp.zeros((D_STATE, CHUNK), F32)
            tots, out_rows, in_rows, in_cols = [], [], [], []
            for r in heads:
                if has_y:
                    pt = c["a1"][r][0:CHUNK] * c["et"][r]
                    dcbt = dcbt + pt
                    mt = pt * c["cbt"]
                    ph = c["a1"][r][CHUNK:] * c["e_row"][r:r + 1]
                    dct = dct + ph
                    out_rows.append(_colsum(mt + c["ct"] * ph))
                    in_cols.append(jnp.sum(mt, axis=1, keepdims=True))
                p2 = c["p2raw"][r] * c["dtt"][r:r + 1]
                dbt = dbt + p2
                t_term = _colsum(c["bt"] * p2)
                in_rows.append(t_term)
                hdh = c["h"][:, psl[r]] * c["dh"][:, psl[r]]
                tot = jnp.sum(t_term, axis=1, keepdims=True) + c["dcy"][r:r + 1] * jnp.sum(jnp.where(mine[r], hdh, 0.0), keepdims=True)
                tots.append(jnp.broadcast_to(tot, (1, CHUNK)))
            for p in range(HPG // 2):
                dx_ref[0, :, g * GW + p * LANES:g * GW + (p + 1) * LANES] = jnp.where(first, c["dxs"][2 * p], c["dxs"][2 * p + 1])
                new = c["dh"][:, psl[2 * p]] * jnp.where(first, c["dcy"][2 * p:2 * p + 1], c["dcy"][2 * p + 1:2 * p + 2])
                if has_y:
                    new = new + jnp.where(first, c["news"][2 * p], c["news"][2 * p + 1])
                dh_scr[g, :, psl[2 * p]] = new
            db = dbt.T
            if has_y:
                dcbt_bf = dcbt.astype(BF16)
                db = db + _dot(dcbt_bf, c["cm"].astype(BF16))
                dc_ref[0, :, g * D_STATE:(g + 1) * D_STATE] = dct.T + _dot_tn(dcbt_bf, c["bm"].astype(BF16))
            db_ref[0, :, g * D_STATE:(g + 1) * D_STATE] = db
            s_row = _stack_rows(in_rows)
            lane = lax.broadcasted_iota(jnp.int32, (HPG, CHUNK), 1)
            dcs = jnp.where(lane == last, _stack_rows(tots), 0.0)
            if has_y:
                s_row = s_row + _cols_to_rows(in_cols)
                dcs = dcs + _stack_rows(out_rows)
            dcs = dcs - s_row
            da = _cumsum_lanes(dcs, not reverse)
            ddt = da * c["A"] + jnp.where(c["dt"] > 0.0, s_row / c["dt"], 0.0)
            ddtraw = ddt * _sigmoid(c["dtraw"] + bias_ref[g])
            ddt_ref[0, g * HPG:(g + 1) * HPG] = ddtraw
            dbias_ref[0, g] += jnp.sum(ddtraw, axis=1, keepdims=True)
            dalog_ref[0, g] += jnp.sum(da * c["dt"], axis=1, keepdims=True) * c["A"]

        @pl.when(s == nc - 1)
        def _():
            dh0_ref[0] = dh_scr[...]

    nb = D_INNER // (GPS * D_STATE)
    cidx = lambda b, g, s: (b, chunk_of(s), g)
    hidx = lambda b, g, s: (b, g, 0, 0)
    in_specs = [
        pl.BlockSpec((1, GPS * HPG, CHUNK), lambda b, g, s: (b, blk0 + g, chunk_of(s))),
        pl.BlockSpec((GPS, HPG, 1), lambda b, g, s: (blk0 + g, 0, 0)),
        pl.BlockSpec((GPS, HPG, 1), lambda b, g, s: (blk0 + g, 0, 0)),
        pl.BlockSpec((1, CHUNK, GPS * GW), cidx),
        pl.BlockSpec((1, CHUNK, GPS * D_STATE), lambda b, g, s: (b, chunk_of(s), nb + g)),
        pl.BlockSpec((1, GPS, 1, D_STATE, GW), lambda b, g, s: (b, g, chunk_of(s), 0, 0)),
        pl.BlockSpec((1, GPS, D_STATE, GW), hidx),
    ]
    args = [dtT, bias, alog, xbc, xbc, h_start, dh_final]
    if has_y:
        in_specs += [pl.BlockSpec((1, CHUNK, GPS * D_STATE), lambda b, g, s: (b, chunk_of(s), nb + N_BC // GPS + g)),
                     pl.BlockSpec((1, CHUNK, GPS * GW), cidx)]
        args += [xbc, dy]
    out_shape = [jax.ShapeDtypeStruct((Bn, L, D_INNER), F32), jax.ShapeDtypeStruct((Bn, L, N_BC * D_STATE), F32),
                 jax.ShapeDtypeStruct((Bn, N_HEADS, L), F32), jax.ShapeDtypeStruct((Bn, N_BC, HPG, 1), F32),
                 jax.ShapeDtypeStruct((Bn, N_BC, HPG, 1), F32), jax.ShapeDtypeStruct((Bn, N_BC, D_STATE, GW), F32)]
    out_specs = [pl.BlockSpec((1, CHUNK, GPS * GW), cidx), pl.BlockSpec((1, CHUNK, GPS * D_STATE), cidx),
                 pl.BlockSpec((1, GPS * HPG, CHUNK), lambda b, g, s: (b, g, chunk_of(s))),
                 pl.BlockSpec((1, GPS, HPG, 1), hidx), pl.BlockSpec((1, GPS, HPG, 1), hidx), pl.BlockSpec((1, GPS, D_STATE, GW), hidx)]
    if has_y:
        out_shape.append(jax.ShapeDtypeStruct((Bn, L, N_BC * D_STATE), F32))
        out_specs.append(pl.BlockSpec((1, CHUNK, GPS * D_STATE), cidx))
    res = pl.pallas_call(
        kern, name=name, out_shape=out_shape, grid=(Bn, N_BC // GPS, nc), in_specs=in_specs, out_specs=out_specs,
        scratch_shapes=[pltpu.VMEM((GPS, D_STATE, GW), F32)],
        compiler_params=_cp(("arbitrary", "arbitrary", "arbitrary")))(*args)
    dxs, db, ddt, dbias, dalog, dh0 = res[:6]
    return dxs, db, (res[6] if has_y else None), ddt, dbias, dalog, dh0


def _dot_split2(v, sel):
    hi = v.astype(BF16)
    mid = (v - hi.astype(F32)).astype(BF16)
    return _dot(hi, sel) + _dot(mid, sel)


def ssd_tables():
    lane = jnp.arange(LANES)[:, None]
    col = jnp.arange(2 * GW)[None, :]
    expand = (lane == jnp.where(col < GW, HPG + col // HEAD_DIM, 2 * HPG + (col - GW) // HEAD_DIM)).astype(BF16)
    ch = jnp.arange(GW)[:, None] // HEAD_DIM
    out = jnp.arange(2 * LANES)[None, :]
    seg = ((out == ch) | (out == LANES + HPG + ch)).astype(BF16)
    return expand, seg


def _dc_lanes(dc, first):
    return jnp.concatenate([jnp.where(first, dc[2 * p:2 * p + 1], dc[2 * p + 1:2 * p + 2]) for p in range(HPG // 2)], axis=1)


def ssd_fwd2(name, dtT, bias, alog, xbc, h0, tables, direction, with_y):
    Bn, L = xbc.shape[:2]
    nc = L // CHUNK
    reverse = direction == 1
    rowblk = direction * N_BC
    expand = tables[0]

    def chunk_of(s):
        return (nc - 1 - s) if reverse else s

    def kern(dt_ref, bias_ref, alog_ref, x_ref, b_ref, c_ref, h0_ref, xp_ref, *rest):
        if with_y:
            y_ref, hs_ref, hf_ref, h_scr = rest
        else:
            hs_ref, hf_ref, h_scr = rest
        s = pl.program_id(2)

        @pl.when(s == 0)
        def _():
            h_scr[...] = h0_ref[0, 0]

        dt, _, cs, total = _ssd_scalars(dt_ref[0], bias_ref[0], alog_ref[0], reverse)
        u = cs - jnp.log(dt)
        dtt = jnp.exp(total - u)
        cols = _rows_to_cols(jnp.concatenate([cs, dtt, jnp.exp(cs)], axis=0))
        wide = _dot_split2(cols, xp_ref[...])
        dtt_x, e_x = wide[:, 0:GW], wide[:, GW:]
        first = lax.broadcasted_iota(jnp.int32, (1, LANES), 1) < HEAD_DIM
        x = x_ref[0]
        x_bf = x.astype(BF16)
        bm = b_ref[0]
        h = h_scr[...]
        hs_ref[0, 0, 0] = h
        st = _dot(bm.T.astype(BF16), (x * dtt_x).astype(BF16))
        h_scr[...] = h * _dc_lanes(jnp.exp(total), first) + st
        if with_y:
            cm = c_ref[0].astype(BF16)
            cb = _dot_nt(cm, bm.astype(BF16))
            yoff = _dot(cm, h.astype(BF16)) * e_x
            keep = _tri_mask(False, reverse)
            wfs = []
            for r in range(HPG):
                cs_col = jnp.broadcast_to(cols[:, r:r + 1], (CHUNK, LANES))
                wfs.append((cb * jnp.exp(jnp.where(keep, cs_col - u[r:r + 1], -jnp.inf))).astype(BF16))
            yd = [_dot(wfs[r], x_bf[:, (r // 2) * LANES:(r // 2 + 1) * LANES]) for r in range(HPG)]
            for p in range(HPG // 2):
                psl = slice(p * LANES, (p + 1) * LANES)
                y_ref[0, :, psl] = jnp.where(first, yd[2 * p], yd[2 * p + 1]) + yoff[:, psl]

        @pl.when(s == nc - 1)
        def _():
            hf_ref[0, 0] = h_scr[...]

    in_specs = [
        pl.BlockSpec((1, HPG, CHUNK), lambda b, g, s: (b, rowblk + g, chunk_of(s))),
        pl.BlockSpec((1, HPG, 1), lambda b, g, s: (rowblk + g, 0, 0)),
        pl.BlockSpec((1, HPG, 1), lambda b, g, s: (rowblk + g, 0, 0)),
        pl.BlockSpec((1, CHUNK, GW), lambda b, g, s: (b, chunk_of(s), g)),
        pl.BlockSpec((1, CHUNK, D_STATE), lambda b, g, s: (b, chunk_of(s), D_INNER // D_STATE + g)),
        pl.BlockSpec((1, CHUNK, D_STATE), lambda b, g, s: (b, chunk_of(s), D_INNER // D_STATE + N_BC + g)),
        pl.BlockSpec((1, 1, D_STATE, GW), lambda b, g, s: (b, g, 0, 0)),
        pl.BlockSpec(expand.shape, lambda b, g, s: (0, 0)),
    ]
    out_shape, out_specs = [], []
    if with_y:
        out_shape.append(jax.ShapeDtypeStruct((Bn, L, D_INNER), F32))
        out_specs.append(pl.BlockSpec((1, CHUNK, GW), lambda b, g, s: (b, chunk_of(s), g)))
    out_shape += [jax.ShapeDtypeStruct((Bn, N_BC, nc, D_STATE, GW), F32), jax.ShapeDtypeStruct((Bn, N_BC, D_STATE, GW), F32)]
    out_specs += [pl.BlockSpec((1, 1, 1, D_STATE, GW), lambda b, g, s: (b, g, chunk_of(s), 0, 0)),
                  pl.BlockSpec((1, 1, D_STATE, GW), lambda b, g, s: (b, g, 0, 0))]
    return pl.pallas_call(
        kern, name=name, out_shape=out_shape, grid=(Bn, N_BC, nc), in_specs=in_specs, out_specs=out_specs,
        scratch_shapes=[pltpu.VMEM((D_STATE, GW), F32)],
        compiler_params=_cp(("arbitrary", "arbitrary", "arbitrary")))(dtT, bias, alog, xbc, xbc, xbc, h0, expand)


def ssd_bwd2(name, dtT, bias, alog, xbc, h_start, dy, dh_final, tables, direction):
    Bn, L = xbc.shape[:2]
    nc = L // CHUNK
    reverse = direction == 1
    rowblk = direction * N_BC
    has_y = dy is not None
    last = 0 if reverse else CHUNK - 1
    expand, seg = tables

    def chunk_of(s):
        return s if reverse else (nc - 1 - s)

    def kern(*refs):
        if has_y:
            (dt_ref, bias_ref, alog_ref, x_ref, b_ref, hs_ref, dhf_ref, xp_ref, seg_ref, c_ref, dy_ref,
             dx_ref, db_ref, ddt_ref, dbias_ref, dalog_ref, dh0_ref, dc_ref, dh_scr) = refs
        else:
            (dt_ref, bias_ref, alog_ref, x_ref, b_ref, hs_ref, dhf_ref, xp_ref, seg_ref,
             dx_ref, db_ref, ddt_ref, dbias_ref, dalog_ref, dh0_ref, dh_scr) = refs
        s = pl.program_id(2)

        @pl.when(s == 0)
        def _():
            dh_scr[...] = dhf_ref[0, 0]
            dbias_ref[...] = jnp.zeros(dbias_ref.shape, F32)
            dalog_ref[...] = jnp.zeros(dalog_ref.shape, F32)

        first = lax.broadcasted_iota(jnp.int32, (1, LANES), 1) < HEAD_DIM
        heads = range(HPG)
        psl = [slice((r // 2) * LANES, (r // 2 + 1) * LANES) for r in heads]
        x = x_ref[0]
        bm = b_ref[0].astype(BF16)
        h = hs_ref[0, 0, 0]
        dh = dh_scr[...]
        dh_bf = dh.astype(BF16)
        bdh = _dot(bm, dh_bf)
        if has_y:
            cm = c_ref[0].astype(BF16)
            dyv = dy_ref[0]
            dy_bf = dyv.astype(BF16)
            x_bf = x.astype(BF16)
            h_bf = h.astype(BF16)
            cbt = _dot_nt(bm, cm)
            ch = _dot(cm, h_bf)
            zeros_bf = jnp.zeros((CHUNK, LANES), BF16)
            gts = [_dot_nt(x_bf[:, psl[r]], jnp.where(first if r % 2 == 0 else jnp.logical_not(first), dy_bf[:, psl[r]], zeros_bf))
                   for r in heads]
            ct_bf = c_ref[0].T.astype(BF16)
        dtraw = dt_ref[0]
        dt, A, cs, total = _ssd_scalars(dtraw, bias_ref[0], alog_ref[0], reverse)
        u = cs - jnp.log(dt)
        dtt = jnp.exp(total - u)
        dcy = jnp.exp(total)
        cols = _rows_to_cols(jnp.concatenate([u, dtt, jnp.exp(cs)], axis=0))
        wide = _dot_split2(cols, xp_ref[...])
        dtt_x, e_x = wide[:, 0:GW], wide[:, GW:]
        term2 = bdh * dtt_x
        dbt = _dot_nt(dh_bf, (x * dtt_x).astype(BF16))
        sums = _dot_split2(term2 * x, seg_ref[:, LANES:])
        new_dh = dh * _dc_lanes(dcy, first)
        if has_y:
            dye = dyv * e_x
            dye_bf = dye.astype(BF16)
            dct = _dot_nt(h_bf, dye_bf)
            new_dh = new_dh + _dot(ct_bf, dye_bf)
            sums = sums + _dot_split2(ch * dye, seg_ref[:, 0:LANES])
            keep = _tri_mask(True, reverse)
            ets = []
            for r in heads:
                u_col = jnp.broadcast_to(cols[:, r:r + 1], (CHUNK, LANES))
                ets.append(jnp.exp(jnp.where(keep, cs[r:r + 1] - u_col, -jnp.inf)))
            wts = [(cbt * ets[r]).astype(BF16) for r in heads]
            dxd = [_dot(wts[r], dy_bf[:, psl[r]]) for r in heads]
            dcbt = jnp.zeros((CHUNK, CHUNK), F32)
            out_rows, in_cols = [], []
            for r in heads:
                pt = gts[r] * ets[r]
                dcbt = dcbt + pt
                mt = pt * cbt
                out_rows.append(_colsum(mt))
                in_cols.append(jnp.sum(mt, axis=1, keepdims=True))
            for p in range(HPG // 2):
                dx_ref[0, :, psl[2 * p]] = jnp.where(first, dxd[2 * p], dxd[2 * p + 1]) + term2[:, psl[2 * p]]
            dcbt_bf = dcbt.astype(BF16)
            db_ref[0] = dbt.T + _dot(dcbt_bf, cm)
            dc_ref[0] = dct.T + _dot_tn(dcbt_bf, bm)
        else:
            dx_ref[0] = term2
            db_ref[0] = dbt.T
        dh_scr[...] = new_dh
        sums_t = sums.T
        s_row = sums_t[HPG:2 * HPG]
        hdh = _colsum(h * dh)
        lanes_w = lax.broadcasted_iota(jnp.int32, (1, GW), 1)
        hd = _stack_rows([jnp.sum(jnp.where(lanes_w // HEAD_DIM == r, hdh, 0.0), axis=1, keepdims=True) for r in range(HPG)])
        tot = jnp.sum(s_row, axis=1, keepdims=True) + dcy * hd
        lane = lax.broadcasted_iota(jnp.int32, (HPG, CHUNK), 1)
        dcs = jnp.where(lane == last, tot, 0.0)
        if has_y:
            s_row = s_row + _cols_to_rows(in_cols)
            dcs = dcs + _stack_rows(out_rows) + sums_t[0:HPG]
        dcs = dcs - s_row
        da = _cumsum_lanes(dcs, not reverse)
        ddt = da * A + jnp.where(dt > 0.0, s_row / dt, 0.0)
        ddtraw = ddt * _sigmoid(dtraw + bias_ref[0])
        ddt_ref[0] = ddtraw
        dbias_ref[0, 0] += jnp.sum(ddtraw, axis=1, keepdims=True)
        dalog_ref[0, 0] += jnp.sum(da * dt, axis=1, keepdims=True) * A

        @pl.when(s == nc - 1)
        def _():
            dh0_ref[0, 0] = dh_scr[...]

    cidx = lambda b, g, s: (b, chunk_of(s), g)
    hidx = lambda b, g, s: (b, g, 0, 0)
    in_specs = [
        pl.BlockSpec((1, HPG, CHUNK), lambda b, g, s: (b, rowblk + g, chunk_of(s))),
        pl.BlockSpec((1, HPG, 1), lambda b, g, s: (rowblk + g, 0, 0)),
        pl.BlockSpec((1, HPG, 1), lambda b, g, s: (rowblk + g, 0, 0)),
        pl.BlockSpec((1, CHUNK, GW), cidx),
        pl.BlockSpec((1, CHUNK, D_STATE), lambda b, g, s: (b, chunk_of(s), D_INNER // D_STATE + g)),
        pl.BlockSpec((1, 1, 1, D_STATE, GW), lambda b, g, s: (b, g, chunk_of(s), 0, 0)),
        pl.BlockSpec((1, 1, D_STATE, GW), hidx),
        pl.BlockSpec(expand.shape, lambda b, g, s: (0, 0)),
        pl.BlockSpec(seg.shape, lambda b, g, s: (0, 0)),
    ]
    args = [dtT, bias, alog, xbc, xbc, h_start, dh_final, expand, seg]
    if has_y:
        in_specs += [pl.BlockSpec((1, CHUNK, D_STATE), lambda b, g, s: (b, chunk_of(s), D_INNER // D_STATE + N_BC + g)),
                     pl.BlockSpec((1, CHUNK, GW), cidx)]
        args += [xbc, dy]
    out_shape = [jax.ShapeDtypeStruct((Bn, L, D_INNER), F32), jax.ShapeDtypeStruct((Bn, L, N_BC * D_STATE), F32),
                 jax.ShapeDtypeStruct((Bn, N_HEADS, L), F32), jax.ShapeDtypeStruct((Bn, N_BC, HPG, 1), F32),
                 jax.ShapeDtypeStruct((Bn, N_BC, HPG, 1), F32), jax.ShapeDtypeStruct((Bn, N_BC, D_STATE, GW), F32)]
    out_specs = [pl.BlockSpec((1, CHUNK, GW), cidx), pl.BlockSpec((1, CHUNK, D_STATE), cidx),
                 pl.BlockSpec((1, HPG, CHUNK), lambda b, g, s: (b, g, chunk_of(s))),
                 pl.BlockSpec((1, 1, HPG, 1), hidx), pl.BlockSpec((1, 1, HPG, 1), hidx), pl.BlockSpec((1, 1, D_STATE, GW), hidx)]
    if has_y:
        out_shape.append(jax.ShapeDtypeStruct((Bn, L, N_BC * D_STATE), F32))
        out_specs.append(pl.BlockSpec((1, CHUNK, D_STATE), cidx))
    res = pl.pallas_call(
        kern, name=name, out_shape=out_shape, grid=(Bn, N_BC, nc), in_specs=in_specs, out_specs=out_specs,
        scratch_shapes=[pltpu.VMEM((D_STATE, GW), F32)],
        compiler_params=_cp(("arbitrary", "arbitrary", "arbitrary")))(*args)
    dxs, db, ddt, dbias, dalog, dh0 = res[:6]
    return dxs, db, (res[6] if has_y else None), ddt, dbias, dalog, dh0


def _group_mean(v):
    gw = D_INNER // N_BC
    parts = [jnp.broadcast_to(jnp.mean(v[:, g * gw:(g + 1) * gw], axis=-1, keepdims=True), (v.shape[0], gw)) for g in range(N_BC)]
    return jnp.concatenate(parts, axis=1)


def gated_norm_fwd(name, y, z, w_norm):
    def body(y, z, w):
        u = y * _silu(z)
        r = lax.rsqrt(_group_mean(u * u) + NORM_EPS)
        return u * r * w

    return tok_call(name, body, [y, z], [], [w_norm], [(D_INNER, BF16)], [], [])[0]


def _dot_exact01(v, sel):
    hi, mid, lo = _split3(v)
    return _dot(hi, sel) + _dot(mid, sel) + _dot(lo, sel)


def gated_norm_bwd(name, y, xs_src, z, d_out, w_norm, head_sel):
    def body(y, xs, z, do, w, sel):
        sz = _silu(z)
        u = y * sz
        r = lax.rsqrt(_group_mean(u * u) + NORM_EPS)
        duh = do * w
        du = r * (duh - u * (r * r) * _group_mean(duh * u))
        dy = du * sz
        dz = du * y * _dsilu(z)
        dsk_heads = _dot_exact01(jnp.broadcast_to(_colsum(dy * xs), (8, D_INNER)), sel)
        return dy, dz, _colsum(do * u * r), dsk_heads

    return tok_call(name, body, [y, xs_src, z, d_out], [], [w_norm, head_sel],
                    [(D_INNER, F32), (D_INNER, BF16)], [], [(1, D_INNER), (8, LANES)], tm=128)


def merge_fwd(name, y_pool, y_ssd, gatepre, x, target, gate, b_merge, norm_post, w_pp, w_ps, w_out):
    def body(yp, ys, gp, x, tgt, gate, bm, wpost, w_pp, w_ps, w_out):
        p1 = _dot(yp, w_pp)
        p2 = _dot(ys, w_ps)
        gates = _sigmoid(gp + bm)
        merged = gates[:, :D] * p1 + gates[:, D:] * p2
        out = _dot(merged.astype(BF16), w_out)
        r = _rms_r(out)
        outr = out * r
        nq = outr * wpost
        err = x + gate * nq - tgt
        loss = 0.5 * jnp.sum(jnp.mean(err * err, axis=-1, keepdims=True), keepdims=True).reshape(1, 1)
        g = err * (1.0 / D)
        dnq = g * gate
        dout = _rms_bwd(dnq * wpost, out, r)
        return merged, p1, p2, dout, g, _colsum(g * nq), _colsum(dnq * outr), jnp.broadcast_to(loss, (1, LANES))

    return tok_call(name, body, [y_pool, y_ssd, gatepre, x, target], [gate], [b_merge, norm_post, w_pp, w_ps, w_out],
                    [(D, BF16), (D, F32), (D, F32), (D, BF16), (D, F32)], [D], [(1, D), (1, LANES)])


def merge_bwd(name, dout, gatepre, p1, p2, b_merge, w_pp, w_ps, w_out):
    def body(dout, gp, p1, p2, bm, w_pp, w_ps, w_out):
        dmerged = _dot_nt(dout, w_out)
        gates = _sigmoid(gp + bm)
        g1, g2 = gates[:, :D], gates[:, D:]
        dp1 = (dmerged * g1).astype(BF16)
        dp2 = (dmerged * g2).astype(BF16)
        dgp = jnp.concatenate([dmerged * p1 * g1 * (1.0 - g1), dmerged * p2 * g2 * (1.0 - g2)], axis=1)
        return dp1, dp2, dgp, _dot_nt(dp1, w_pp), _dot_nt(dp2, w_ps), _colsum(dgp)

    return tok_call(name, body, [dout, gatepre, p1, p2], [], [b_merge, w_pp, w_ps, w_out],
                    [(D, BF16), (D, BF16), (2 * D, BF16), (D, F32), (D_INNER, F32)], [], [(1, 2 * D)])


def _adamw_math(w, g, m, v):
    m = ADAM_B1 * m + (1.0 - ADAM_B1) * g
    v = ADAM_B2 * v + (1.0 - ADAM_B2) * (g * g)
    m_hat = m / (1.0 - ADAM_B1 ** ADAM_STEP)
    v_hat = v / (1.0 - ADAM_B2 ** ADAM_STEP)
    delta = -ADAM_LR * (m_hat / (jnp.sqrt(v_hat) + ADAM_EPS) + ADAM_WD * w)
    return delta, m, v


def adamw(name, w, g, m, v, tr=256):
    R, C = w.shape
    tr = min(tr, R)
    assert R % tr == 0

    def body(w_ref, g_ref, m_ref, v_ref, d_ref, nm_ref, nv_ref):
        d, nm, nv = _adamw_math(w_ref[...], g_ref[...], m_ref[...], v_ref[...])
        d_ref[...] = d
        nm_ref[...] = nm
        nv_ref[...] = nv

    spec = pl.BlockSpec((tr, C), lambda i: (i, 0))
    return pl.pallas_call(
        body, name=name, out_shape=[jax.ShapeDtypeStruct((R, C), F32)] * 3, grid=(R // tr,),
        in_specs=[spec] * 4, out_specs=[spec] * 3, compiler_params=_cp(("parallel",)))(w, g, m, v)


def _me():
    return lax.axis_index("x"), lax.axis_index("y"), lax.axis_index("c")


def all_gather_small(name, v):
    R, C = v.shape

    def body(v_ref, out_ref, send_sems, recv_sems, local_sem):
        x, y, c = _me()
        me = 4 * x + 2 * y + c
        mine = pltpu.make_async_copy(v_ref, out_ref.at[me], local_sem)
        mine.start()
        copies = []
        for d in range(1, N_DEV):
            dx, dy, dc = d // 4, (d // 2) % 2, d % 2
            px, py, pc = x ^ dx, y ^ dy, c ^ dc
            copies.append(pltpu.make_async_remote_copy(
                src_ref=v_ref, dst_ref=out_ref.at[me], send_sem=send_sems.at[d - 1], recv_sem=recv_sems.at[d - 1],
                device_id=(px, py, pc), device_id_type=MESH))
        for cp in copies:
            cp.start()
        for d in range(1, N_DEV):
            dx, dy, dc = d // 4, (d // 2) % 2, d % 2
            peer = 4 * (x ^ dx) + 2 * (y ^ dy) + (c ^ dc)
            pltpu.make_async_remote_copy(
                src_ref=v_ref, dst_ref=out_ref.at[peer], send_sem=send_sems.at[d - 1], recv_sem=recv_sems.at[d - 1],
                device_id=(x ^ dx, y ^ dy, c ^ dc), device_id_type=MESH).wait_recv()
        for cp in copies:
            cp.wait_send()
        mine.wait()

    return pl.pallas_call(
        body, name=name, out_shape=jax.ShapeDtypeStruct((N_DEV, R, C), F32),
        in_specs=[pl.BlockSpec(memory_space=pltpu.VMEM)], out_specs=pl.BlockSpec(memory_space=pltpu.VMEM),
        scratch_shapes=[pltpu.SemaphoreType.DMA((N_DEV - 1,)), pltpu.SemaphoreType.DMA((N_DEV - 1,)), pltpu.SemaphoreType.DMA],
        compiler_params=pltpu.CompilerParams(vmem_limit_bytes=VMEM_LIMIT))(v)


def all_gather_chips(name, shard):
    R, C = shard.shape
    half = R // 2
    assert R % 32 == 0

    def body(s_ref, out_ref, send_sems, recv_sems):
        x, y, c = _me()
        chips = [(1 - x, y), (x, 1 - y), (1 - x, 1 - y)]

        def rows(chip, hc):
            return out_ref.at[2 * chip[0] + chip[1], pl.ds(hc * half, half), :]

        first = [pltpu.make_async_remote_copy(
            src_ref=s_ref.at[pl.ds(c * half, half), :], dst_ref=rows((x, y), c), send_sem=send_sems.at[j],
            recv_sem=recv_sems.at[j], device_id=(*chip, c), device_id_type=MESH) for j, chip in enumerate(chips)]
        for cp in first:
            cp.start()
        passed = [pltpu.make_async_remote_copy(
            src_ref=rows(chip, c), dst_ref=rows(chip, c), send_sem=send_sems.at[3 + j], recv_sem=recv_sems.at[3 + j],
            device_id=(x, y, 1 - c), device_id_type=MESH) for j, chip in enumerate(chips)]
        for j, chip in enumerate(chips):
            pltpu.make_async_remote_copy(
                src_ref=rows(chip, c), dst_ref=rows(chip, c), send_sem=send_sems.at[j], recv_sem=recv_sems.at[j],
                device_id=(*chip, c), device_id_type=MESH).wait_recv()
            passed[j].start()
        for j, chip in enumerate(chips):
            pltpu.make_async_remote_copy(
                src_ref=rows(chip, 1 - c), dst_ref=rows(chip, 1 - c), send_sem=send_sems.at[3 + j], recv_sem=recv_sems.at[3 + j],
                device_id=(x, y, 1 - c), device_id_type=MESH).wait_recv()
        for cp in first + passed:
            cp.wait_send()

    out = pl.pallas_call(
        body, name=name, out_shape=jax.ShapeDtypeStruct((N_CHIPS, R, C), shard.dtype),
        in_specs=[pl.BlockSpec(memory_space=pl.ANY)], out_specs=pl.BlockSpec(memory_space=pl.ANY),
        scratch_shapes=[pltpu.SemaphoreType.DMA((6,)), pltpu.SemaphoreType.DMA((6,))],
        compiler_params=pltpu.CompilerParams(vmem_limit_bytes=VMEM_LIMIT))(shard)
    chip = 2 * lax.axis_index("x") + lax.axis_index("y")
    return lax.dynamic_update_index_in_dim(out, shard, chip, 0)


def sibling_swap(name, v):
    def body(v_ref, out_ref, send_sem, recv_sem):
        x, y, c = _me()
        cp = pltpu.make_async_remote_copy(src_ref=v_ref, dst_ref=out_ref, send_sem=send_sem, recv_sem=recv_sem,
                                          device_id=(x, y, 1 - c), device_id_type=MESH)
        cp.start()
        cp.wait()

    return pl.pallas_call(
        body, name=name, out_shape=jax.ShapeDtypeStruct(v.shape, v.dtype),
        in_specs=[pl.BlockSpec(memory_space=pl.ANY)], out_specs=pl.BlockSpec(memory_space=pl.ANY),
        scratch_shapes=[pltpu.SemaphoreType.DMA, pltpu.SemaphoreType.DMA],
        compiler_params=pltpu.CompilerParams(vmem_limit_bytes=VMEM_LIMIT))(v)


def sibling_share(name, v):
    def body(v_ref, out_ref, send_sem, recv_sem, local_sem):
        x, y, c = _me()
        mine = pltpu.make_async_copy(v_ref, out_ref.at[c], local_sem)
        mine.start()
        cp = pltpu.make_async_remote_copy(src_ref=v_ref, dst_ref=out_ref.at[c], send_sem=send_sem, recv_sem=recv_sem,
                                          device_id=(x, y, 1 - c), device_id_type=MESH)
        cp.start()
        pltpu.make_async_remote_copy(src_ref=v_ref, dst_ref=out_ref.at[1 - c], send_sem=send_sem, recv_sem=recv_sem,
                                     device_id=(x, y, 1 - c), device_id_type=MESH).wait_recv()
        cp.wait_send()
        mine.wait()

    return pl.pallas_call(
        body, name=name, out_shape=jax.ShapeDtypeStruct((2, *v.shape), v.dtype),
        in_specs=[pl.BlockSpec(memory_space=pl.ANY)], out_specs=pl.BlockSpec(memory_space=pl.ANY),
        scratch_shapes=[pltpu.SemaphoreType.DMA, pltpu.SemaphoreType.DMA, pltpu.SemaphoreType.DMA],
        compiler_params=pltpu.CompilerParams(vmem_limit_bytes=VMEM_LIMIT))(v)


def chip_exchange(name, parts):
    def body(p_ref, out_ref, send_sems, recv_sems):
        x, y, c = _me()
        k = 2 * x + y
        chips = [(1 - x, y), (x, 1 - y), (1 - x, 1 - y)]
        sends = [pltpu.make_async_remote_copy(
            src_ref=p_ref.at[2 * chip[0] + chip[1]], dst_ref=out_ref.at[k], send_sem=send_sems.at[j], recv_sem=recv_sems.at[j],
            device_id=(*chip, c), device_id_type=MESH) for j, chip in enumerate(chips)]
        for cp in sends:
            cp.start()
        for j, chip in enumerate(chips):
            pltpu.make_async_remote_copy(
                src_ref=p_ref.at[k], dst_ref=out_ref.at[2 * chip[0] + chip[1]], send_sem=send_sems.at[j], recv_sem=recv_sems.at[j],
                device_id=(*chip, c), device_id_type=MESH).wait_recv()
        for cp in sends:
            cp.wait_send()

    out = pl.pallas_call(
        body, name=name, out_shape=jax.ShapeDtypeStruct(parts.shape, parts.dtype),
        in_specs=[pl.BlockSpec(memory_space=pl.ANY)], out_specs=pl.BlockSpec(memory_space=pl.ANY),
        scratch_shapes=[pltpu.SemaphoreType.DMA((3,)), pltpu.SemaphoreType.DMA((3,))],
        compiler_params=pltpu.CompilerParams(vmem_limit_bytes=VMEM_LIMIT))(parts)
    chip = 2 * lax.axis_index("x") + lax.axis_index("y")
    own = lax.dynamic_index_in_dim(parts, chip, 0, keepdims=True)
    return lax.dynamic_update_slice_in_dim(out, own, chip, 0)


def _row_tile(rows, cap, mult=8):
    best = None
    for t in range(mult, min(rows, cap) + 1, mult):
        if rows % t == 0:
            best = t
    assert best is not None, rows
    return best


def add_arrays(name, arrs, out_dtype=F32):
    shape = arrs[0].shape
    C = shape[-1]
    flat = [a.reshape(-1, C) for a in arrs]
    R = flat[0].shape[0]
    narrow = out_dtype == BF16 or any(a.dtype == BF16 for a in arrs)
    tr = _row_tile(R, 2048 if len(arrs) <= 2 else 1024, 16 if narrow else 8)
    n = len(flat)

    def body(*refs):
        acc = refs[0][...].astype(F32)
        for r in refs[1:n]:
            acc = acc + r[...].astype(F32)
        refs[n][...] = acc.astype(out_dtype)

    spec = pl.BlockSpec((tr, C), lambda i: (i, 0))
    out = pl.pallas_call(
        body, name=name, out_shape=jax.ShapeDtypeStruct((R, C), out_dtype), grid=(R // tr,),
        in_specs=[spec] * n, out_specs=spec, compiler_params=_cp(("parallel",)))(*flat)
    return out.reshape(shape)


def reduce_scatter_chips(slabs):
    _, R, C = slabs.shape
    half = R // 2
    c = lax.axis_index("c")
    k = 2 * lax.axis_index("x") + lax.axis_index("y")
    halves = slabs.reshape(N_CHIPS, 2, half, C)
    own = lax.dynamic_index_in_dim(halves, c, axis=1, keepdims=False)
    other = lax.dynamic_index_in_dim(halves, 1 - c, axis=1, keepdims=False)
    from_sibling = sibling_swap("rs_sibling_halves", other.astype(BF16))
    del k
    return add_arrays("rs_add_sibling", [own, from_sibling], out_dtype=BF16)


def reduce_scatter_finish(landed):
    c = lax.axis_index("c")
    mine = add_arrays("rs_add_chips", [landed[j] for j in range(N_CHIPS)])
    sib = sibling_swap("rs_sibling_result", mine)
    return jnp.concatenate([jnp.where(c == 0, mine, sib), jnp.where(c == 0, sib, mine)], axis=0)


def ada_mod_shard(cond_all, w_ada_shard, b_ada_shard):
    def body(c_ref, w_ref, b_ref, o_ref):
        o_ref[...] = _dot(_silu(c_ref[...]).astype(BF16), w_ref[...].astype(BF16)) + b_ref[...]

    return pl.pallas_call(body, name="ada_mod_shard", out_shape=jax.ShapeDtypeStruct((cond_all.shape[0], w_ada_shard.shape[1]), F32),
                          compiler_params=_cp())(cond_all, w_ada_shard, b_ada_shard)


def ada_bwd_shard(cond_all, dmod_all_shard, dmod_all, w_ada_shard, row_is_cctx):
    def body(c_ref, ds_ref, da_ref, w_ref, sel_ref, gw_ref, gb_ref, part_ref):
        sc = _silu(c_ref[...]).astype(BF16)
        gw_ref[...] = _dot_tn(sc, ds_ref[...].astype(BF16))
        gb_ref[...] = _colsum(da_ref[...])
        dc_tot = jnp.broadcast_to(_colsum(ds_ref[...] * sel_ref[...]), (8, ds_ref.shape[1]))
        part_ref[...] = _dot_nt(dc_tot.astype(BF16), w_ref[...].astype(BF16))

    n = cond_all.shape[0]
    return pl.pallas_call(
        body, name="ada_bwd_shard",
        out_shape=[jax.ShapeDtypeStruct(w_ada_shard.shape, F32), jax.ShapeDtypeStruct((1, dmod_all.shape[1]), F32),
                   jax.ShapeDtypeStruct((8, D), F32)],
        compiler_params=_cp())(cond_all, dmod_all_shard, dmod_all, w_ada_shard, row_is_cctx)


def sum_devices(name, gathered):
    def body(g_ref, o_ref):
        acc = g_ref[0]
        for d in range(1, N_DEV):
            acc = acc + g_ref[d]
        o_ref[...] = acc

    return pl.pallas_call(body, name=name, out_shape=jax.ShapeDtypeStruct(gathered.shape[1:], F32), compiler_params=_cp())(gathered)


def cctx_finish(gathered, c_ctx_row):
    def body(g_ref, c_ref, o_ref):
        acc = g_ref[0, 0:1, :]
        for k in range(1, N_CHIPS):
            acc = acc + g_ref[2 * k, 0:1, :]
        o_ref[...] = acc * _dsilu(c_ref[...])

    return pl.pallas_call(body, name="cctx_finish", out_shape=jax.ShapeDtypeStruct((1, D), F32), compiler_params=_cp())(gathered, c_ctx_row)


def _pack(parts, rows):
    flat = []
    for p in parts:
        p = p.reshape(-1)
        pad = (-p.shape[0]) % LANES
        flat.append(jnp.pad(p, (0, pad)) if pad else p)
    v = jnp.concatenate(flat)
    return jnp.pad(v, (0, rows * LANES - v.shape[0])).reshape(rows, LANES)


def _unpack(v, sizes):
    flat = v.reshape(-1)
    out, off = [], 0
    for n in sizes:
        out.append(flat[off:off + n])
        off += n + (-n) % LANES
    return out


W_SHARD_ROWS = 3456
SEG_ROWS = (0, 2320, 2576, 3088, 3344, 3408)


def kernel(x, c, ctx, c_ctx, w_ada, b_ada, norm_pre, norm_post, w_in, b_merge, pool_w, pool_scale, conv_w, conv_b, dt_bias, a_log, d_skip, ssd_norm, w_proj_pool, w_proj_ssd, w_out, loss_target, m_c_ctx, m_w_ada, m_b_ada, m_norm_pre, m_norm_post, m_w_in, m_b_merge, m_pool_w, m_pool_scale, m_conv_w, m_conv_b, m_dt_bias, m_a_log, m_d_skip, m_ssd_norm, m_w_proj_pool, m_w_proj_ssd, m_w_out, v_c_ctx, v_w_ada, v_b_ada, v_norm_pre, v_norm_post, v_w_in, v_b_merge, v_pool_w, v_pool_scale, v_conv_w, v_conv_b, v_dt_bias, v_a_log, v_d_skip, v_ssd_norm, v_w_proj_pool, v_w_proj_ssd, v_w_out):
    Bn, L, _ = x.shape
    Lc = ctx.shape[1]
    T, Tc = Bn * L, Bn * Lc
    assert Bn == 2
    ix, iy, ic = lax.axis_index("x"), lax.axis_index("y"), lax.axis_index("c")
    me = 4 * ix + 2 * iy + ic
    chip = 2 * ix + iy
    ada_cols = w_ada.shape[2]
    cw_cols = conv_w.shape[2]

    cond_own = jnp.pad(c, ((0, 8 - Bn), (0, 0))) + jnp.pad(c_ctx[None, :], ((Bn, 7 - Bn), (0, 0)))
    convw_own = jnp.pad(conv_w[0], ((0, 4), (0, D - cw_cols)))
    g1 = all_gather_small("gather_cond", jnp.concatenate([cond_own, convw_own], axis=0))
    cond_all = g1[:, 0:8].reshape(8 * N_DEV, D)
    conv_w_full = jnp.concatenate([g1[2 * k, 8:12, 0:cw_cols] for k in range(N_CHIPS)], axis=1)
    b_ada_shard = lax.dynamic_slice(b_ada, (0, chip * ada_cols), (1, ada_cols))
    g2 = all_gather_small("gather_mod", ada_mod_shard(cond_all, w_ada[0], b_ada_shard))
    mod_full = jnp.concatenate([g2[2 * k] for k in range(N_CHIPS)], axis=1)
    own = lax.dynamic_slice(mod_full, (8 * me, 0), (8, 3 * D))
    shift, scale, gate = (own[0:Bn, i * D:(i + 1) * D][:, None, :] for i in range(3))
    shift_c, scale_c = (jnp.broadcast_to(own[Bn:Bn + 1, i * D:(i + 1) * D][None], (Bn, 1, D)) for i in range(2))

    w_in_rows = IN_COLS // N_CHIPS
    shard_in = jnp.concatenate([w_in[0].T, jnp.zeros((16, D), F32)], axis=0).astype(BF16)
    shard_rest = jnp.concatenate([w_proj_pool[0], w_proj_ssd[0], w_out[0], pool_w[0].reshape(64, D)], axis=0).astype(BF16)
    w_inT = all_gather_chips("gather_w_in", shard_in)[:, 0:w_in_rows].reshape(IN_COLS, D)
    w_dt = jnp.pad(w_inT[9216:IN_COLS], ((0, LANES - 64), (0, 0)))
    seg_lo = (0, 256, 512, 768, 1024, 2048, 4096, 6144, 8192, 8704)
    seg_hi = (256, 512, 768, 1024, 2048, 4096, 6144, 8192, 8704, 9216)
    w_seg = [w_inT[lo:hi] for lo, hi in zip(seg_lo, seg_hi)] + [w_dt]

    hx = prenorm_fwd("prenorm_x", x, scale, shift, norm_pre)
    hc = prenorm_fwd("prenorm_ctx", ctx, scale_c, shift_c, norm_pre)
    hx2, hc2 = hx.reshape(T, D), hc.reshape(Tc, D)
    v = mm_nt("proj_v", hx2, w_inT[0:1024], F32).reshape(Bn, L, D)
    zp = mm_nt("proj_zpool", hx2, w_inT[1024:2048], F32).reshape(Bn, L, D)
    zs = mm_nt("proj_zssd", hx2, w_inT[2048:4096], F32).reshape(Bn, L, D_INNER)
    gp = mm_nt("proj_gate", hx2, w_inT[4096:6144], F32).reshape(Bn, L, 2 * D)
    xbc_raw, g_rest = mm_nt("proj_xbc", hx2, w_inT[6144:9216], F32, gather=shard_rest)
    xbc_raw = xbc_raw.reshape(Bn, L, CONV_DIM)
    w_pp = g_rest[:, 0:256].reshape(D, D)
    w_ps = g_rest[:, 256:768].reshape(D_INNER, D)
    w_o = g_rest[:, 768:1024].reshape(D, D)
    pool_full = g_rest[:, 1024:1088].reshape(N_CHIPS, 4, 64, POOL_GROUP).transpose(1, 0, 2, 3).reshape(D, POOL_GROUP)
    dt_raw = mm_nt("proj_dt", hx2, w_dt, F32)
    xbc_raw_c = mm_nt("proj_xbc_ctx", hc2, w_inT[6144:9216], F32).reshape(Bn, Lc, CONV_DIM)
    dt_raw_c = mm_nt("proj_dt_ctx", hc2, w_dt, F32)
    dtT = dt_raw[:, :64].reshape(Bn, L, 64).transpose(0, 2, 1)
    dtT_c = dt_raw_c[:, :64].reshape(Bn, Lc, 64).transpose(0, 2, 1)
    bias3 = dt_bias.reshape(2 * N_BC, HPG, 1)
    alog3 = a_log.reshape(2 * N_BC, HPG, 1)

    xbc = conv_fwd("conv_x", xbc_raw, conv_w_full, conv_b)
    xbc_c = conv_fwd("conv_ctx", xbc_raw_c, conv_w_full, conv_b)
    zero_state = jnp.zeros((Bn, N_BC, D_STATE, GW), F32)
    tables = ssd_tables()
    ys, hs_x, hs_c, hf_x, hf_c = [], [], [], [], []
    for d in range(2):
        hsc, hfc = ssd_fwd3(f"ssd_fwd_ctx{d}", dtT_c, bias3, alog3, xbc_c, zero_state, d, False)
        y, hsx, hfx = ssd_fwd3(f"ssd_fwd_x{d}", dtT, bias3, alog3, xbc, hfc, d, True,
                               y_add=(ys[0], jnp.repeat(d_skip[0], HEAD_DIM)[None, :]) if d == 1 else None)
        ys.append(y)
        hs_x.append(hsx)
        hs_c.append(hsc)
        hf_x.append(hfx)
        hf_c.append(hfc)

    dgs = [pool_diff(f"pool_diff{g}", v, g * POOL_GROUP, g, False) for g in range(4)]
    y_pool = pool_mix_fwd("pool_mix", dgs, zp, pool_full, pool_scale)
    dskip_lanes = jnp.repeat(d_skip[0], HEAD_DIM)[None, :]
    y_ssd = gated_norm_fwd("gated_norm", ys[1], zs, ssd_norm)
    merged, p1, p2, dout, g_res, dgate, g_norm_post, loss_part = merge_fwd(
        "merge_fwd", y_pool, y_ssd, gp, x, loss_target, gate, b_merge, norm_post, w_pp, w_ps, w_o)

    dp1, dp2, dgp, dyp, dys, g_b_merge = merge_bwd("merge_bwd", dout, gp, p1, p2, b_merge, w_pp, w_ps, w_o)
    gw_o = mm_tn("gw_out", merged.reshape(T, D), dout.reshape(T, D))
    gw_pp = mm_tn("gw_proj_pool", y_pool.reshape(T, D), dp1.reshape(T, D))
    gw_ps = mm_tn("gw_proj_ssd", y_ssd.reshape(T, D_INNER), dp2.reshape(T, D))

    *dds, dzp, g_pool, g_pool_scale = pool_mix_bwd("pool_mix_bwd", dgs, zp, dyp, pool_full, pool_scale)
    dvs = [pool_diff(f"pool_diff_t{g}", dds[g], 0, g, True) for g in range(4)]

    head_sel = (jnp.arange(D_INNER)[:, None] // HEAD_DIM == jnp.arange(LANES)[None, :]).astype(BF16)
    dy, dzs, g_ssd_norm, g_dskip = gated_norm_bwd(
        "gated_norm_bwd", ys[1], (xbc, D_INNER), zs, dys, ssd_norm, head_sel)

    dxs, dbm, dcm, ddt, dxs_c, dbm_c, ddt_c = [], [], [], [], [], [], []
    g_bias = jnp.zeros((2, N_BC, HPG, 1), F32)
    g_alog = jnp.zeros((2, N_BC, HPG, 1), F32)
    for d in range(2):
        a, b_, c_, t_, gb, ga, dh0 = ssd_bwd3(f"ssd_bwd_x{d}", dtT, bias3, alog3, xbc, hs_x[d], dy, zero_state, d)
        dxs.append(a), dbm.append(b_), dcm.append(c_), ddt.append(t_)
        ac, bc, _, tc, gbc, gac, _ = ssd_bwd3(f"ssd_bwd_ctx{d}", dtT_c, bias3, alog3, xbc_c, hs_c[d], None, dh0, d)
        dxs_c.append(ac), dbm_c.append(bc), ddt_c.append(tc)
        g_bias = g_bias.at[d].set(jnp.sum(gb, axis=0) + jnp.sum(gbc, axis=0))
        g_alog = g_alog.at[d].set(jnp.sum(ga, axis=0) + jnp.sum(gac, axis=0))

    dxr_xs, gcw_xs, gcb_xs = conv_bwd_stream("conv_bwd_xs", xbc_raw, dxs, conv_w_full, conv_b, 0, D_INNER, scaled=(dy, dskip_lanes))
    dxr_b, gcw_b, gcb_b = conv_bwd_stream("conv_bwd_b", xbc_raw, dbm, conv_w_full, conv_b, D_INNER, N_BC * D_STATE)
    dxr_c, gcw_c, gcb_c = conv_bwd_stream("conv_bwd_c", xbc_raw, dcm, conv_w_full, conv_b, D_INNER + N_BC * D_STATE, N_BC * D_STATE)
    dxr_xs_c, gcw_xs_c, gcb_xs_c = conv_bwd_stream("conv_bwd_xs_ctx", xbc_raw_c, dxs_c, conv_w_full, conv_b, 0, D_INNER)
    dxr_b_c, gcw_b_c, gcb_b_c = conv_bwd_stream("conv_bwd_b_ctx", xbc_raw_c, dbm_c, conv_w_full, conv_b, D_INNER, N_BC * D_STATE)
    g_conv_w = jnp.concatenate([gcw_xs + gcw_xs_c, gcw_b + gcw_b_c, gcw_c], axis=1)
    g_conv_b = jnp.concatenate([gcb_xs + gcb_xs_c, gcb_b + gcb_b_c, gcb_c], axis=1)

    def dt_cols(parts, n_tok):
        t = jnp.concatenate(parts, axis=1).transpose(0, 2, 1).reshape(n_tok, 2 * N_HEADS)
        return jnp.pad(t, ((0, 0), (0, LANES - 2 * N_HEADS))).astype(BF16)

    ddt2, ddt2_c = dt_cols(ddt, T), dt_cols(ddt_c, Tc)
    segs = ([dv.reshape(T, POOL_GROUP) for dv in dvs]
            + [dzp.reshape(T, D), dzs.reshape(T, D_INNER), dgp.reshape(T, 2 * D), dxr_xs.reshape(T, D_INNER),
               dxr_b.reshape(T, N_BC * D_STATE), dxr_c.reshape(T, N_BC * D_STATE), ddt2])
    segs_c = {7: dxr_xs_c.reshape(Tc, D_INNER), 8: dxr_b_c.reshape(Tc, N_BC * D_STATE), 10: ddt2_c}
    gw_rows = []
    for i, seg in enumerate(segs):
        init = mm_tn(f"gw_in_ctx{i}", segs_c[i], hc2) if i in segs_c else None
        gw_rows.append(mm_tn(f"gw_in{i}", seg, hx2, init=init))
    gw_rows[-1] = gw_rows[-1][0:2 * N_HEADS]
    gw_inT = jnp.concatenate(gw_rows, axis=0)

    pool_slab = g_pool.reshape(4, N_CHIPS, 64, POOL_GROUP).transpose(1, 0, 2, 3).reshape(N_CHIPS, 64, D)
    slabs = jnp.concatenate([gw_inT.reshape(N_CHIPS, 2320, D), gw_pp.reshape(N_CHIPS, 256, D), gw_ps.reshape(N_CHIPS, 512, D),
                             gw_o.reshape(N_CHIPS, 256, D), pool_slab, jnp.zeros((N_CHIPS, W_SHARD_ROWS - SEG_ROWS[-1], D), F32)], axis=1)
    chip_part = reduce_scatter_chips(slabs)
    d_hx, landed = mm_nn_multi("d_hx", list(zip(segs, w_seg)), F32, tm=1024, tk=256, exchange=chip_part)
    d_hx = d_hx.reshape(Bn, L, D)
    gsh = reduce_scatter_finish(landed)
    d_hc = mm_nn_multi("d_hc", [(segs_c[i], w_seg[i]) for i in (7, 8, 10)], F32).reshape(Bn, Lc, D)

    grad_x, dscale, dshift, g_npre_x = prenorm_bwd("prenorm_bwd_x", x, d_hx, scale, norm_pre, g_res=g_res)
    _, dscale_c, dshift_c, g_npre_c = prenorm_bwd("prenorm_bwd_ctx", ctx, d_hc, scale_c, norm_pre)

    dmod_x = jnp.concatenate([dshift[:, 0], dscale[:, 0], dgate[:, 0]], axis=1)
    dmod_c = jnp.concatenate([jnp.sum(dshift_c[:, 0], axis=0, keepdims=True), jnp.sum(dscale_c[:, 0], axis=0, keepdims=True),
                              jnp.zeros((1, D), F32)], axis=1)
    dmod_own = jnp.pad(dmod_x, ((0, 8 - Bn), (0, 0))) + jnp.pad(dmod_c, ((Bn, 7 - Bn), (0, 0)))
    dmod_all = all_gather_small("gather_dmod", dmod_own).reshape(8 * N_DEV, 3 * D)
    row_is_cctx = (jnp.arange(8 * N_DEV) % 8 == Bn).astype(F32)[:, None]
    g_w_ada, g_b_ada, cpart = ada_bwd_shard(
        cond_all, lax.dynamic_slice(dmod_all, (0, chip * ada_cols), (8 * N_DEV, ada_cols)), dmod_all, w_ada[0], row_is_cctx)
    g_c_ctx = cctx_finish(all_gather_small("gather_cctx", cpart), c_ctx[None, :])

    small_sizes = (D, D, 2 * D, D, CONV_DIM, 2 * N_HEADS, 2 * N_HEADS, N_HEADS, D_INNER, 4 * CONV_DIM, 1)
    pk = _pack([g_npre_x + g_npre_c, g_norm_post, g_b_merge, g_pool_scale, g_conv_b, g_bias, g_alog, g_dskip[0, 0:N_HEADS],
                g_ssd_norm, g_conv_w, loss_part[0, 0:1]], 184)
    small = sum_devices("sum_small", all_gather_small("gather_small", pk))
    (g_norm_pre, g_norm_post_t, g_b_merge_t, g_pool_scale_t, g_conv_b_t, g_dt_bias, g_a_log, g_d_skip, g_ssd_norm_t,
     g_conv_w_t, loss) = _unpack(small, small_sizes)
    g_conv_w_shard = lax.dynamic_slice(g_conv_w_t.reshape(4, CONV_DIM), (0, chip * cw_cols), (4, cw_cols))

    g_w_in = gsh[SEG_ROWS[0]:SEG_ROWS[1]].T
    g_w_pp, g_w_ps, g_w_o = (gsh[SEG_ROWS[i]:SEG_ROWS[i + 1]] for i in (1, 2, 3))
    g_pool_w = gsh[SEG_ROWS[4]:SEG_ROWS[5]].reshape(256, POOL_GROUP)

    grads = {
        "c_ctx": g_c_ctx.reshape(c_ctx.shape), "w_ada": g_w_ada[None], "b_ada": g_b_ada, "norm_pre": g_norm_pre[None],
        "norm_post": g_norm_post_t[None], "w_in": g_w_in[None], "b_merge": g_b_merge_t[None],
        "pool_w": g_pool_w.reshape(pool_w.shape), "pool_scale": g_pool_scale_t[None], "conv_w": g_conv_w_shard[None],
        "conv_b": g_conv_b_t[None], "dt_bias": g_dt_bias.reshape(dt_bias.shape), "a_log": g_a_log.reshape(a_log.shape),
        "d_skip": g_d_skip[None], "ssd_norm": g_ssd_norm_t[None], "w_proj_pool": g_w_pp[None], "w_proj_ssd": g_w_ps[None],
        "w_out": g_w_o[None]}
    weights = dict(c_ctx=c_ctx, w_ada=w_ada, b_ada=b_ada, norm_pre=norm_pre, norm_post=norm_post, w_in=w_in, b_merge=b_merge,
                   pool_w=pool_w, pool_scale=pool_scale, conv_w=conv_w, conv_b=conv_b, dt_bias=dt_bias, a_log=a_log,
                   d_skip=d_skip, ssd_norm=ssd_norm, w_proj_pool=w_proj_pool, w_proj_ssd=w_proj_ssd, w_out=w_out)
    m_in = dict(c_ctx=m_c_ctx, w_ada=m_w_ada, b_ada=m_b_ada, norm_pre=m_norm_pre, norm_post=m_norm_post, w_in=m_w_in,
                b_merge=m_b_merge, pool_w=m_pool_w, pool_scale=m_pool_scale, conv_w=m_conv_w, conv_b=m_conv_b,
                dt_bias=m_dt_bias, a_log=m_a_log, d_skip=m_d_skip, ssd_norm=m_ssd_norm, w_proj_pool=m_w_proj_pool,
                w_proj_ssd=m_w_proj_ssd, w_out=m_w_out)
    v_in = dict(c_ctx=v_c_ctx, w_ada=v_w_ada, b_ada=v_b_ada, norm_pre=v_norm_pre, norm_post=v_norm_post, w_in=v_w_in,
                b_merge=v_b_merge, pool_w=v_pool_w, pool_scale=v_pool_scale, conv_w=v_conv_w, conv_b=v_conv_b,
                dt_bias=v_dt_bias, a_log=v_a_log, d_skip=v_d_skip, ssd_norm=v_ssd_norm, w_proj_pool=v_w_proj_pool,
                w_proj_ssd=v_w_proj_ssd, w_out=v_w_out)
    names = list(weights)
    big = ("w_ada", "w_in", "pool_w", "w_proj_pool", "w_proj_ssd", "w_out")
    small_names = [n for n in names if n not in big]
    delta, new_m, new_v = {}, {}, {}
    for n in big:
        shape2 = (-1, weights[n].shape[-1])
        d_, m_, v_ = adamw(f"adamw_{n}", weights[n].reshape(shape2), grads[n].reshape(shape2), m_in[n].reshape(shape2),
                           v_in[n].reshape(shape2), tr=128)
        delta[n], new_m[n], new_v[n] = (t.reshape(weights[n].shape) for t in (d_, m_, v_))
    sizes = [weights[n].size for n in small_names]
    packed = [_pack([src[n] for n in small_names], 144) for src in (weights, grads, m_in, v_in)]
    outs = adamw("adamw_small", *packed, tr=144)
    for res, store in zip(outs, (delta, new_m, new_v)):
        for n, piece in zip(small_names, _unpack(res, sizes)):
            store[n] = piece.reshape(weights[n].shape)

    return (loss.reshape(()), grad_x, *[grads[n] for n in names], *[delta[n] for n in names],
            *[new_m[n] for n in names], *[new_v[n] for n in names])
```

```python
import jax
import jax.numpy as jnp
from jax import lax
from jax.experimental import pallas as pl
from jax.experimental.pallas import tpu as pltpu

F32 = jnp.float32
BF16 = jnp.bfloat16
MESH = pl.DeviceIdType.MESH

D = 1024
GRID_W = 64
NORM_EPS = 1e-6
POOL_WINDOWS = (2, 4, 8, 16)
POOL_GROUP = 256
D_INNER = 2048
HEAD_DIM = 64
N_HEADS = 32
D_STATE = 128
N_BC = 4
HPG = N_HEADS // N_BC
GW = HPG * HEAD_DIM
CONV_DIM = 3072
CHUNK = 128
OFF_XBC = 6144
IN_COLS = 9280
N_CHIPS = 4
N_DEV = 8

ADAM_LR = 0.001
ADAM_B1 = 0.9
ADAM_B2 = 0.999
ADAM_EPS = 1e-08
ADAM_WD = 0.01
ADAM_STEP = 10

V7X_VMEM_BYTES = 64 * 1024 * 1024
VMEM_LIMIT = V7X_VMEM_BYTES * 3 // 4
LANES = 128


def _cp(sem=None):
    return pltpu.CompilerParams(dimension_semantics=sem, vmem_limit_bytes=VMEM_LIMIT)


def _dot(a, b):
    return jnp.dot(a, b, preferred_element_type=F32)


def _dot_nt(a, b):
    return lax.dot_general(a, b, (((1,), (1,)), ((), ())), preferred_element_type=F32)


def _dot_tn(a, b):
    return lax.dot_general(a, b, (((0,), (0,)), ((), ())), preferred_element_type=F32)


def _split3(x):
    hi = x.astype(BF16)
    r1 = x - hi.astype(F32)
    mid = r1.astype(BF16)
    lo = (r1 - mid.astype(F32)).astype(BF16)
    return hi, mid, lo


def _sigmoid(x):
    return jax.nn.sigmoid(x)


def _silu(x):
    return x * _sigmoid(x)


def _dsilu(x):
    s = _sigmoid(x)
    return s * (1.0 + x * (1.0 - s))


def _softplus(x):
    return jnp.maximum(x, 0.0) + jnp.log(1.0 + jnp.exp(-jnp.abs(x)))


def mm_nt(name, a, b, out_dtype, tm=1024, tn=512, gather=None):
    M, K = a.shape
    N = b.shape[0]
    tm, tn = min(tm, M), min(tn, N)
    assert M % tm == 0 and N % tn == 0, (M, N, tm, tn)
    n_i, n_j = M // tm, N // tn
    has_g = gather is not None
    if has_g:
        half = gather.shape[0] // 2
        assert gather.shape[0] % 32 == 0 and n_i * n_j >= 4

    def body(*refs):
        a_ref, b_ref = refs[0], refs[1]
        if has_g:
            s_ref, o_ref, g_ref, send_sems, recv_sems = refs[2:]
            x, y, c = _me()
            chips = [(1 - x, y), (x, 1 - y), (1 - x, 1 - y)]
            step = pl.program_id(0) * n_j + pl.program_id(1)

            def rows(chip, hc):
                return g_ref.at[2 * chip[0] + chip[1], pl.ds(hc * half, half), :]

            def first(j, chip):
                return pltpu.make_async_remote_copy(
                    src_ref=s_ref.at[pl.ds(c * half, half), :], dst_ref=rows((x, y), c), send_sem=send_sems.at[j],
                    recv_sem=recv_sems.at[j], device_id=(*chip, c), device_id_type=MESH)

            def landed(j, chip, hc):
                return pltpu.make_async_remote_copy(
                    src_ref=rows(chip, hc), dst_ref=rows(chip, hc), send_sem=send_sems.at[j], recv_sem=recv_sems.at[j],
                    device_id=(x, y, 1 - c), device_id_type=MESH)

            @pl.when(step == 0)
            def _():
                for j, chip in enumerate(chips):
                    first(j, chip).start()

            @pl.when(step == (3 * n_i * n_j) // 4)
            def _():
                for j, chip in enumerate(chips):
                    landed(j, chip, c).wait_recv()
                    landed(3 + j, chip, c).start()
        else:
            o_ref = refs[2]

        o_ref[...] = _dot_nt(a_ref[...], b_ref[...]).astype(o_ref.dtype)

        if has_g:
            @pl.when(step == n_i * n_j - 1)
            def _():
                for j, chip in enumerate(chips):
                    landed(3 + j, chip, 1 - c).wait_recv()
                for j, chip in enumerate(chips):
                    first(j, chip).wait_send()
                    landed(3 + j, chip, c).wait_send()

    in_specs = [pl.BlockSpec((tm, K), lambda i, j: (i, 0)), pl.BlockSpec((tn, K), lambda i, j: (j, 0))]
    out_shape = jax.ShapeDtypeStruct((M, N), out_dtype)
    out_specs = pl.BlockSpec((tm, tn), lambda i, j: (i, j))
    if not has_g:
        return pl.pallas_call(body, name=name, out_shape=out_shape, grid=(n_i, n_j), in_specs=in_specs, out_specs=out_specs,
                              compiler_params=_cp(("parallel", "arbitrary")))(a, b)
    out, g = pl.pallas_call(
        body, name=name, out_shape=[out_shape, jax.ShapeDtypeStruct((N_CHIPS, *gather.shape), gather.dtype)], grid=(n_i, n_j),
        in_specs=in_specs + [pl.BlockSpec(memory_space=pl.ANY)], out_specs=[out_specs, pl.BlockSpec(memory_space=pl.ANY)],
        scratch_shapes=[pltpu.SemaphoreType.DMA((6,)), pltpu.SemaphoreType.DMA((6,))],
        compiler_params=_cp(("arbitrary", "arbitrary")))(a, b, gather)
    chip = 2 * lax.axis_index("x") + lax.axis_index("y")
    return out, lax.dynamic_update_index_in_dim(g, gather, chip, 0)


def mm_tn(name, a, b, init=None, tm=1024, tn=1024, tk=512):
    T, M = a.shape
    N = b.shape[1]
    tm, tn, tk = min(tm, M), min(tn, N), min(tk, T)
    assert M % tm == 0 and N % tn == 0 and T % tk == 0, (M, N, T)
    has_init = init is not None

    def body(*refs):
        if has_init:
            a_ref, b_ref, i_ref, o_ref = refs
        else:
            a_ref, b_ref, o_ref = refs
        k = pl.program_id(2)

        @pl.when(k == 0)
        def _():
            o_ref[...] = i_ref[...] if has_init else jnp.zeros(o_ref.shape, F32)

        o_ref[...] += _dot_tn(a_ref[...], b_ref[...])

    in_specs = [pl.BlockSpec((tk, tm), lambda i, j, k: (k, i)), pl.BlockSpec((tk, tn), lambda i, j, k: (k, j))]
    args = [a, b]
    if has_init:
        in_specs.append(pl.BlockSpec((tm, tn), lambda i, j, k: (i, j)))
        args.append(init)
    return pl.pallas_call(
        body, name=name, out_shape=jax.ShapeDtypeStruct((M, N), F32), grid=(M // tm, N // tn, T // tk),
        in_specs=in_specs, out_specs=pl.BlockSpec((tm, tn), lambda i, j, k: (i, j)),
        compiler_params=_cp(("parallel", "parallel", "arbitrary")))(*args)


def mm_nn_multi(name, pairs, out_dtype, tm=512, tk=512, exchange=None):
    M = pairs[0][0].shape[0]
    N = pairs[0][1].shape[1]
    tm = min(tm, M)
    assert M % tm == 0
    plan = []
    step = 0
    for a, b in pairs:
        K = a.shape[1]
        t = min(tk, K)
        assert K % t == 0 and b.shape == (K, N)
        plan.append((t, step, K // t))
        step += K // t
    nsteps = step
    npairs = len(pairs)

    n_i = M // tm
    has_x = exchange is not None

    def body(*refs):
        if has_x:
            p_ref, o_ref, land_ref, acc, send_sems, recv_sems = refs[2 * npairs:]
        else:
            o_ref, acc = refs[2 * npairs:]
        i, k = pl.program_id(0), pl.program_id(1)

        if has_x:
            x, y, c = _me()
            me_chip = 2 * x + y
            chips = [(1 - x, y), (x, 1 - y), (1 - x, 1 - y)]

            def copy(j, src_chip, dst_chip, to):
                return pltpu.make_async_remote_copy(
                    src_ref=p_ref.at[src_chip], dst_ref=land_ref.at[dst_chip], send_sem=send_sems.at[j], recv_sem=recv_sems.at[j],
                    device_id=(*to, c), device_id_type=MESH)

            @pl.when((i == 0) & (k == 0))
            def _():
                for j, chip in enumerate(chips):
                    copy(j, 2 * chip[0] + chip[1], me_chip, chip).start()

        @pl.when(k == 0)
        def _():
            acc[...] = jnp.zeros(acc.shape, F32)

        for p, (_, first, n) in enumerate(plan):
            @pl.when((k >= first) & (k < first + n))
            def _(p=p):
                acc[...] += _dot(refs[2 * p][...], refs[2 * p + 1][...])

        @pl.when(k == nsteps - 1)
        def _():
            o_ref[...] = acc[...].astype(o_ref.dtype)

        if has_x:
            @pl.when((i == n_i - 1) & (k == nsteps - 1))
            def _():
                for j, chip in enumerate(chips):
                    copy(j, me_chip, 2 * chip[0] + chip[1], chip).wait_recv()
                for j, chip in enumerate(chips):
                    copy(j, 2 * chip[0] + chip[1], me_chip, chip).wait_send()

    in_specs, args = [], []
    for (a, b), (t, first, n) in zip(pairs, plan):
        in_specs.append(pl.BlockSpec((tm, t), lambda i, k, first=first, n=n: (i, jnp.clip(k - first, 0, n - 1))))
        in_specs.append(pl.BlockSpec((t, N), lambda i, k, first=first, n=n: (jnp.clip(k - first, 0, n - 1), 0)))
        args += [a, b]
    out_shape = jax.ShapeDtypeStruct((M, N), out_dtype)
    out_specs = pl.BlockSpec((tm, N), lambda i, k: (i, 0))
    scratch = [pltpu.VMEM((tm, N), F32)]
    if has_x:
        in_specs.append(pl.BlockSpec(memory_space=pl.ANY))
        args.append(exchange)
        out_shape = [out_shape, jax.ShapeDtypeStruct(exchange.shape, exchange.dtype)]
        out_specs = [out_specs, pl.BlockSpec(memory_space=pl.ANY)]
        scratch += [pltpu.SemaphoreType.DMA((3,)), pltpu.SemaphoreType.DMA((3,))]
    res = pl.pallas_call(
        body, name=name, out_shape=out_shape, grid=(n_i, nsteps), in_specs=in_specs, out_specs=out_specs,
        scratch_shapes=scratch, compiler_params=_cp(("arbitrary", "arbitrary")))(*args)
    if not has_x:
        return res
    out, landed = res
    chip = 2 * lax.axis_index("x") + lax.axis_index("y")
    own = lax.dynamic_index_in_dim(exchange, chip, 0, keepdims=True)
    return out, lax.dynamic_update_slice_in_dim(landed, own, chip, 0)


def tok_call(name, body, tiled, perb, glob, out_tiled, out_perb, out_glob, tm=256):
    widths = [t[1] if isinstance(t, tuple) else t.shape[2] for t in tiled]
    tiled = [t[0] if isinstance(t, tuple) else t for t in tiled]
    Bn, L = tiled[0].shape[:2]
    tm = min(tm, L)
    assert L % tm == 0
    n_t, n_p, n_g = len(tiled), len(perb), len(glob)
    o_t, o_p, o_g = len(out_tiled), len(out_perb), len(out_glob)
    n_in = n_t + n_p + n_g

    def kern(*refs):
        ins, outs = refs[:n_in], refs[n_in:]
        b, j = pl.program_id(0), pl.program_id(1)
        vals = [r[0] for r in ins[:n_t + n_p]] + [r[...] for r in ins[n_t + n_p:]]
        res = body(*vals)
        if not isinstance(res, (tuple, list)):
            res = (res,)
        assert len(res) == o_t + o_p + o_g, (name, len(res))
        for r, v in zip(outs[:o_t], res[:o_t]):
            r[0] = v.astype(r.dtype)

        def accum(r, v, first, lead):
            @pl.when(first)
            def _():
                r[...] = jnp.zeros(r.shape, F32)
            if lead:
                r[0] += v
            else:
                r[...] += v

        for r, v in zip(outs[o_t:o_t + o_p], res[o_t:o_t + o_p]):
            accum(r, v, j == 0, True)
        for r, v in zip(outs[o_t + o_p:], res[o_t + o_p:]):
            accum(r, v, (j == 0) & (b == 0), False)

    in_specs = ([pl.BlockSpec((1, tm, w), lambda b, j: (b, j, 0)) for w in widths]
                + [pl.BlockSpec((1, 1, a.shape[2]), lambda b, j: (b, 0, 0)) for a in perb]
                + [pl.BlockSpec(a.shape, lambda b, j: (0, 0), pipeline_mode=pl.Buffered(1)) for a in glob])
    out_shape = ([jax.ShapeDtypeStruct((Bn, L, w), dt) for w, dt in out_tiled]
                 + [jax.ShapeDtypeStruct((Bn, 1, w), F32) for w in out_perb]
                 + [jax.ShapeDtypeStruct(s, F32) for s in out_glob])
    out_specs = ([pl.BlockSpec((1, tm, w), lambda b, j: (b, j, 0)) for w, _ in out_tiled]
                 + [pl.BlockSpec((1, 1, w), lambda b, j: (b, 0, 0)) for w in out_perb]
                 + [pl.BlockSpec(s, lambda b, j: (0, 0)) for s in out_glob])
    return pl.pallas_call(
        kern, name=name, out_shape=out_shape, grid=(Bn, L // tm), in_specs=in_specs, out_specs=out_specs,
        compiler_params=_cp(("arbitrary", "arbitrary")))(*tiled, *perb, *glob)


def slab_call(name, body, slabs, colparams, out_slabs, out_colred, wc=LANES):
    Bn, L = slabs[0][0].shape[:2]
    w_out = out_slabs[0][0]
    assert w_out % wc == 0 and all(off % wc == 0 for _, off in slabs + colparams)
    n_col = w_out // wc
    n_s, n_c = len(slabs), len(colparams)
    o_s = len(out_slabs)

    def kern(*refs):
        ins, outs = refs[:n_s + n_c], refs[n_s + n_c:]
        b = pl.program_id(1)
        vals = [r[0] for r in ins[:n_s]] + [r[...] for r in ins[n_s:]]
        res = body(*vals)
        if not isinstance(res, (tuple, list)):
            res = (res,)
        assert len(res) == o_s + len(out_colred), name
        for r, v in zip(outs[:o_s], res[:o_s]):
            r[0] = v.astype(r.dtype)

        def accum(r, v):
            @pl.when(b == 0)
            def _():
                r[...] = jnp.zeros(r.shape, F32)
            r[...] += v

        for r, v in zip(outs[o_s:], res[o_s:]):
            accum(r, v)

    in_specs = ([pl.BlockSpec((1, L, wc), lambda j, b, o=off // wc: (b, 0, o + j)) for _, off in slabs]
                + [pl.BlockSpec((a.shape[0], wc), lambda j, b, o=off // wc: (0, o + j)) for a, off in colparams])
    out_shape = ([jax.ShapeDtypeStruct((Bn, L, w), dt) for w, dt in out_slabs]
                 + [jax.ShapeDtypeStruct((r, w_out), F32) for r in out_colred])
    out_specs = ([pl.BlockSpec((1, L, wc), lambda j, b: (b, 0, j)) for _ in out_slabs]
                 + [pl.BlockSpec((r, wc), lambda j, b: (0, j)) for r in out_colred])
    return pl.pallas_call(
        kern, name=name, out_shape=out_shape, grid=(n_col, Bn), in_specs=in_specs, out_specs=out_specs,
        compiler_params=_cp(("arbitrary", "arbitrary")))(*[a for a, _ in slabs], *[a for a, _ in colparams])


def _rms_r(x):
    return lax.rsqrt(jnp.mean(x * x, axis=-1, keepdims=True) + NORM_EPS)


def _rms_bwd(dxh, x, r):
    return r * (dxh - x * (r * r) * jnp.mean(dxh * x, axis=-1, keepdims=True))


def _colsum(v):
    return jnp.sum(v, axis=0, keepdims=True)


def _stack_rows(rows):
    n, w = len(rows), rows[0].shape[1]
    sub = lax.broadcasted_iota(jnp.int32, (n, w), 0)
    acc = jnp.zeros((n, w), F32)
    for r, row in enumerate(rows):
        acc = acc + jnp.where(sub == r, jnp.broadcast_to(row, (n, w)), 0.0)
    return acc


def prenorm_fwd(name, x, scale, shift, w_pre):
    def body(x, scale, shift, w):
        n = x * _rms_r(x) * w
        return n * (1.0 + scale) + shift

    return tok_call(name, body, [x], [scale, shift], [w_pre], [(D, BF16)], [], [])[0]


def prenorm_bwd(name, x, dhx, scale, w_pre, g_res=None):
    has_res = g_res is not None

    def body(*v):
        if has_res:
            x, dhx, g, scale, w = v
        else:
            x, dhx, scale, w = v
        r = _rms_r(x)
        xr = x * r
        n = xr * w
        dn = dhx * (1.0 + scale)
        dx = _rms_bwd(dn * w, x, r)
        if has_res:
            dx = dx + g
        return dx, _colsum(dhx * n), _colsum(dhx), _colsum(dn * xr)

    tiled = [x, dhx] + ([g_res] if has_res else [])
    return tok_call(name, body, tiled, [scale], [w_pre], [(D, F32)], [D, D], [(1, D)])


def _shift_rows(x, o, tok, L):
    if o == 0:
        return x
    rolled = pltpu.roll(x, (-o) % L, 0)
    return jnp.where((tok + o >= 0) & (tok + o < L), rolled, 0.0)


def conv_fwd(name, xbc_raw, conv_w, conv_b):
    L = xbc_raw.shape[1]

    def body(x, w, b):
        tok = lax.broadcasted_iota(jnp.int32, x.shape, 0)
        pre = b
        for k in range(4):
            pre = pre + _shift_rows(x, k - 2, tok, L) * w[k:k + 1]
        return _silu(pre)

    return slab_call(name, body, [(xbc_raw, 0)], [(conv_w, 0), (conv_b, 0)], [(CONV_DIM, F32)], [])[0]


def conv_bwd(name, xbc_raw, dparts, conv_w, conv_b, col0, width, scaled=None):
    L = xbc_raw.shape[1]
    n_d = len(dparts) + (1 if scaled is not None else 0)

    def body(*v):
        x, ds, w, b = v[0], v[1:1 + n_d], v[1 + n_d], v[2 + n_d]
        tok = lax.broadcasted_iota(jnp.int32, x.shape, 0)
        taps = [_shift_rows(x, k - 2, tok, L) for k in range(4)]
        pre = b
        for k in range(4):
            pre = pre + taps[k] * w[k:k + 1]
        dy = ds[0] * v[3 + n_d] if scaled is not None else ds[0]
        for extra in ds[1:]:
            dy = dy + extra
        dpre = dy * _dsilu(pre)
        dx = jnp.zeros_like(x)
        for k in range(4):
            dx = dx + _shift_rows(dpre, 2 - k, tok, L) * w[k:k + 1]
        dw = _stack_rows([_colsum(dpre * taps[k]) for k in range(4)])
        return dx, dw, _colsum(dpre)

    slabs = [(xbc_raw, col0)] + ([(scaled[0], 0)] if scaled is not None else []) + [(d, 0) for d in dparts]
    colparams = [(conv_w, col0), (conv_b, col0)] + ([(scaled[1], 0)] if scaled is not None else [])
    return slab_call(name, body, slabs, colparams, [(width, BF16)], [4, 1])


CONV_ROWS = 128
CONV_HALO = 8


def _halo_chunks(L, load, work):
    ch, hl = CONV_ROWS, CONV_HALO
    n = L // ch
    assert L % ch == 0
    if n == 1:
        z = jnp.zeros_like(load(0, hl))
        work(0, jnp.concatenate([z, load(0, ch), z], axis=0))
        return
    z = jnp.zeros_like(load(0, hl))
    work(0, jnp.concatenate([z, load(0, ch + hl)], axis=0))

    def step(i, carry):
        start = pl.multiple_of(i * ch, ch)
        work(start, load(pl.multiple_of(start - hl, hl), ch + 2 * hl))
        return carry

    lax.fori_loop(1, n - 1, step, 0)
    work(L - ch, jnp.concatenate([load(L - ch - hl, ch + hl), z], axis=0))


def _rows_at(xh, o):
    return xh if o == 0 else pltpu.roll(xh, (-o) % xh.shape[0], 0)


def conv_fwd_stream(name, xbc_raw, conv_w, conv_b, wc=LANES):
    Bn, L, W = xbc_raw.shape
    mid = slice(CONV_HALO, CONV_HALO + CONV_ROWS)

    def kern(x_ref, w_ref, b_ref, o_ref):
        w, b = w_ref[...], b_ref[...]

        def work(start, xh):
            pre = b
            for k in range(4):
                pre = pre + _rows_at(xh, k - 2) * w[k:k + 1]
            o_ref[0, pl.ds(start, CONV_ROWS), :] = _silu(pre)[mid]

        _halo_chunks(L, lambda s, n: x_ref[0, pl.ds(s, n), :], work)

    return pl.pallas_call(
        kern, name=name, out_shape=jax.ShapeDtypeStruct((Bn, L, W), F32), grid=(W // wc, Bn),
        in_specs=[pl.BlockSpec((1, L, wc), lambda j, b: (b, 0, j)), pl.BlockSpec((4, wc), lambda j, b: (0, j)),
                  pl.BlockSpec((1, wc), lambda j, b: (0, j))],
        out_specs=pl.BlockSpec((1, L, wc), lambda j, b: (b, 0, j)),
        compiler_params=_cp(("arbitrary", "arbitrary")))(xbc_raw, conv_w, conv_b)


def conv_bwd_stream(name, xbc_raw, dparts, conv_w, conv_b, col0, width, scaled=None, wc=LANES):
    Bn, L, _ = xbc_raw.shape
    n_d = len(dparts)
    has_s = scaled is not None
    mid = slice(CONV_HALO, CONV_HALO + CONV_ROWS)
    c0 = col0 // wc

    def kern(*refs):
        x_ref, d_refs = refs[0], refs[1:1 + n_d]
        pos = 1 + n_d
        if has_s:
            s_ref, pos = refs[pos], pos + 1
        w_ref, b_ref = refs[pos], refs[pos + 1]
        pos += 2
        if has_s:
            scale = refs[pos][...]
            pos += 1
        dx_ref, dw_ref, db_ref = refs[pos:pos + 3]
        acc = refs[pos + 3]
        w, b = w_ref[...], b_ref[...]
        acc[...] = jnp.zeros(acc.shape, F32)

        def load(s, n):
            dy = d_refs[0][0, pl.ds(s, n), :]
            for r in d_refs[1:]:
                dy = dy + r[0, pl.ds(s, n), :]
            if has_s:
                dy = dy + s_ref[0, pl.ds(s, n), :] * scale
            return jnp.concatenate([x_ref[0, pl.ds(s, n), :], dy], axis=1)

        def work(start, both):
            xh, dyh = both[:, 0:wc], both[:, wc:]
            taps = [_rows_at(xh, k - 2) for k in range(4)]
            pre = b
            for k in range(4):
                pre = pre + taps[k] * w[k:k + 1]
            dpre = dyh * _dsilu(pre)
            dx = dpre * w[2:3]
            for k in (0, 1, 3):
                dx = dx + _rows_at(dpre, 2 - k) * w[k:k + 1]
            dx_ref[0, pl.ds(start, CONV_ROWS), :] = dx[mid].astype(dx_ref.dtype)
            dm = dpre[mid]
            acc[...] += _stack_rows([_colsum(dm * taps[k][mid]) for k in range(4)] + [_colsum(dm)] + [jnp.zeros((1, wc), F32)] * 3)

        _halo_chunks(L, load, work)
        first = pl.program_id(1) == 0

        @pl.when(first)
        def _():
            dw_ref[...] = acc[0:4]
            db_ref[...] = acc[4:5]

        @pl.when(jnp.logical_not(first))
        def _():
            dw_ref[...] += acc[0:4]
            db_ref[...] += acc[4:5]

    slab = lambda off: pl.BlockSpec((1, L, wc), lambda j, b, off=off: (b, 0, off + j))
    in_specs = [slab(c0)] + [slab(0)] * n_d + ([slab(0)] if has_s else [])
    in_specs += [pl.BlockSpec((4, wc), lambda j, b: (0, c0 + j)), pl.BlockSpec((1, wc), lambda j, b: (0, c0 + j))]
    args = [xbc_raw, *dparts] + ([scaled[0]] if has_s else []) + [conv_w, conv_b]
    if has_s:
        in_specs.append(pl.BlockSpec((1, wc), lambda j, b: (0, j)))
        args.append(scaled[1])
    return pl.pallas_call(
        kern, name=name,
        out_shape=[jax.ShapeDtypeStruct((Bn, L, width), BF16), jax.ShapeDtypeStruct((4, width), F32), jax.ShapeDtypeStruct((1, width), F32)],
        grid=(width // wc, Bn), in_specs=in_specs,
        out_specs=[pl.BlockSpec((1, L, wc), lambda j, b: (b, 0, j)), pl.BlockSpec((4, wc), lambda j, b: (0, j)),
                   pl.BlockSpec((1, wc), lambda j, b: (0, j))],
        scratch_shapes=[pltpu.VMEM((8, wc), F32)],
        compiler_params=_cp(("arbitrary", "arbitrary")))(*args)


def _box_mean(x, k, step, pos, n, L, transpose):
    lo, hi = k // 2, k - 1 - k // 2
    cnt = (jnp.minimum(pos + hi + 1, n) - jnp.maximum(pos - lo, 0)).astype(F32)
    if transpose:
        x = x / cnt
        lo, hi = hi, lo
    acc = x
    for o in range(-lo, hi + 1):
        if o == 0:
            continue
        rolled = pltpu.roll(x, (-o * step) % L, 0)
        acc = acc + jnp.where((pos + o >= 0) & (pos + o < n), rolled, 0.0)
    return acc if transpose else acc / cnt


def pool_diff(name, v, col0, gi, transpose):
    L = v.shape[1]
    rows = L // GRID_W
    k = POOL_WINDOWS[gi]

    def body(x):
        tok = lax.broadcasted_iota(jnp.int32, x.shape, 0)
        col = tok & (GRID_W - 1)
        row = tok >> 6
        if not transpose:
            m = _box_mean(x, k, GRID_W, row, rows, L, False)
            m = _box_mean(m, k, 1, col, GRID_W, L, False)
        else:
            m = _box_mean(x, k, 1, col, GRID_W, L, True)
            m = _box_mean(m, k, GRID_W, row, rows, L, True)
        return m - x

    return slab_call(name, body, [(v, col0)], [], [(POOL_GROUP, BF16)], [])[0]


def pool_mix_fwd(name, dgs, z_pool, pool_w, pool_scale):
    def body(d0, d1, d2, d3, z, w, scale):
        q = jnp.concatenate([_dot(d, w[g * POOL_GROUP:(g + 1) * POOL_GROUP]) for g, d in enumerate((d0, d1, d2, d3))], axis=1)
        return q * scale * _silu(z)

    return tok_call(name, body, list(dgs) + [z_pool], [], [pool_w, pool_scale], [(D, BF16)], [], [])[0]


def pool_mix_bwd(name, dgs, z_pool, dyp, pool_w, pool_scale):
    def body(d0, d1, d2, d3, z, dyp, w, scale):
        ds = (d0, d1, d2, d3)
        q = jnp.concatenate([_dot(d, w[g * POOL_GROUP:(g + 1) * POOL_GROUP]) for g, d in enumerate(ds)], axis=1)
        dypm = dyp * _silu(z)
        dz = dyp * (q * scale) * _dsilu(z)
        dq = (dypm * scale).astype(BF16)
        dds, gws = [], []
        for g, d in enumerate(ds):
            dqg = dq[:, g * POOL_GROUP:(g + 1) * POOL_GROUP]
            dds.append(_dot_nt(dqg, w[g * POOL_GROUP:(g + 1) * POOL_GROUP]))
            gws.append(_dot_tn(d, dqg))
        return (*dds, dz, jnp.concatenate(gws, axis=0), _colsum(dypm * q))

    return tok_call(name, body, list(dgs) + [z_pool, dyp], [], [pool_w, pool_scale],
                    [(POOL_GROUP, F32)] * 4 + [(D, BF16)], [], [(D, POOL_GROUP), (1, D)])


def _cumsum_lanes(a, reverse):
    n = a.shape[1]
    k = lax.broadcasted_iota(jnp.int32, (n, n), 0)
    i = lax.broadcasted_iota(jnp.int32, (n, n), 1)
    tri = jnp.where((k >= i) if reverse else (k <= i), 1.0, 0.0).astype(BF16)
    return _dot_exact01(a, tri)


def _rows_to_cols(rows):
    r = rows.shape[0]
    if r < LANES:
        rows = jnp.concatenate([rows, jnp.zeros((LANES - r, rows.shape[1]), F32)], axis=0)
    return rows.T


def _cols_to_rows(cols):
    q = cols[0].shape[0]
    lane = lax.broadcasted_iota(jnp.int32, (q, LANES), 1)
    acc = jnp.zeros((q, LANES), F32)
    for r, c in enumerate(cols):
        acc = acc + jnp.where(lane == r, c, 0.0)
    return acc.T[0:len(cols)]


def _ssd_scalars(dtraw, bias, alog, reverse):
    dt = _softplus(dtraw + bias)
    A = -jnp.exp(alog)
    cs = _cumsum_lanes(dt * A, reverse)
    total = cs[:, 0:1] if reverse else cs[:, CHUNK - 1:CHUNK]
    return dt, A, cs, total


def _decay_matrix(cs_col, cs_row, reverse):
    i = lax.broadcasted_iota(jnp.int32, (CHUNK, CHUNK), 0)
    j = lax.broadcasted_iota(jnp.int32, (CHUNK, CHUNK), 1)
    keep = (i <= j) if reverse else (i >= j)
    return jnp.exp(jnp.where(keep, cs_col - cs_row, -jnp.inf))


def ssd_fwd_v1(name, dtT, bias, alog, xbc, h0, direction, with_y):
    Bn, L = xbc.shape[:2]
    nc = L // CHUNK
    reverse = direction == 1
    rowblk = direction * N_BC

    def chunk_of(s):
        return (nc - 1 - s) if reverse else s

    def kern(dt_ref, bias_ref, alog_ref, x_ref, b_ref, c_ref, h0_ref, *rest):
        if with_y:
            y_ref, hs_ref, hf_ref, h_scr, xt_scr = rest
        else:
            hs_ref, hf_ref, h_scr, xt_scr = rest
        s = pl.program_id(2)

        @pl.when(s == 0)
        def _():
            h_scr[...] = h0_ref[0, 0]

        dt, _, cs, total = _ssd_scalars(dt_ref[0], bias_ref[0], alog_ref[0], reverse)
        e_row = jnp.exp(cs)
        t_row = jnp.exp(total - cs)
        dc = jnp.exp(total)
        cols = _rows_to_cols(jnp.concatenate([dt, e_row, t_row, cs], axis=0))
        x = x_ref[0]
        bm = b_ref[0].astype(BF16)
        cm = c_ref[0].astype(BF16)
        h = h_scr[...]
        hs_ref[0, 0, 0] = h
        if with_y:
            cb = _dot_nt(cm, bm)
            yoff = _dot(cm, h.astype(BF16))
        for r in range(HPG):
            sl = slice(r * HEAD_DIM, (r + 1) * HEAD_DIM)
            xdt = x[:, sl] * cols[:, r:r + 1]
            if with_y:
                lr = _decay_matrix(cols[:, 3 * HPG + r:3 * HPG + r + 1], cs[r:r + 1], reverse)
                ydiag = _dot((cb * lr).astype(BF16), xdt.astype(BF16))
                y_ref[0, :, sl] = ydiag + yoff[:, sl] * cols[:, HPG + r:HPG + r + 1]
            xt_scr[:, sl] = (xdt * cols[:, 2 * HPG + r:2 * HPG + r + 1]).astype(BF16)
        st = _dot_tn(bm, xt_scr[...])
        for r in range(HPG):
            sl = slice(r * HEAD_DIM, (r + 1) * HEAD_DIM)
            h_scr[:, sl] = h[:, sl] * dc[r:r + 1] + st[:, sl]

        @pl.when(s == nc - 1)
        def _():
            hf_ref[0, 0] = h_scr[...]

    in_specs = [
        pl.BlockSpec((1, HPG, CHUNK), lambda b, g, s: (b, rowblk + g, chunk_of(s))),
        pl.BlockSpec((1, HPG, 1), lambda b, g, s: (rowblk + g, 0, 0)),
        pl.BlockSpec((1, HPG, 1), lambda b, g, s: (rowblk + g, 0, 0)),
        pl.BlockSpec((1, CHUNK, GW), lambda b, g, s: (b, chunk_of(s), g)),
        pl.BlockSpec((1, CHUNK, D_STATE), lambda b, g, s: (b, chunk_of(s), D_INNER // D_STATE + g)),
        pl.BlockSpec((1, CHUNK, D_STATE), lambda b, g, s: (b, chunk_of(s), D_INNER // D_STATE + N_BC + g)),
        pl.BlockSpec((1, 1, D_STATE, GW), lambda b, g, s: (b, g, 0, 0)),
    ]
    out_shape, out_specs = [], []
    if with_y:
        out_shape.append(jax.ShapeDtypeStruct((Bn, L, D_INNER), F32))
        out_specs.append(pl.BlockSpec((1, CHUNK, GW), lambda b, g, s: (b, chunk_of(s), g)))
    out_shape += [jax.ShapeDtypeStruct((Bn, N_BC, nc, D_STATE, GW), F32), jax.ShapeDtypeStruct((Bn, N_BC, D_STATE, GW), F32)]
    out_specs += [pl.BlockSpec((1, 1, 1, D_STATE, GW), lambda b, g, s: (b, g, chunk_of(s), 0, 0)),
                  pl.BlockSpec((1, 1, D_STATE, GW), lambda b, g, s: (b, g, 0, 0))]
    return pl.pallas_call(
        kern, name=name, out_shape=out_shape, grid=(Bn, N_BC, nc), in_specs=in_specs, out_specs=out_specs,
        scratch_shapes=[pltpu.VMEM((D_STATE, GW), F32), pltpu.VMEM((CHUNK, GW), BF16)],
        compiler_params=_cp(("arbitrary", "arbitrary", "arbitrary")))(dtT, bias, alog, xbc, xbc, xbc, h0)


def ssd_bwd_v1(name, dtT, bias, alog, xbc, h_start, dy, dh_final, direction):
    Bn, L = xbc.shape[:2]
    nc = L // CHUNK
    reverse = direction == 1
    rowblk = direction * N_BC
    has_y = dy is not None
    last = 0 if reverse else CHUNK - 1

    def chunk_of(s):
        return s if reverse else (nc - 1 - s)

    def kern(*refs):
        if has_y:
            (dt_ref, bias_ref, alog_ref, x_ref, b_ref, c_ref, hs_ref, dhf_ref, dy_ref,
             dx_ref, db_ref, dc_ref, ddt_ref, dbias_ref, dalog_ref, dh0_ref, dh_scr, e_scr, t_scr) = refs
        else:
            (dt_ref, bias_ref, alog_ref, x_ref, b_ref, hs_ref, dhf_ref,
             dx_ref, db_ref, ddt_ref, dbias_ref, dalog_ref, dh0_ref, dh_scr, t_scr) = refs
        s = pl.program_id(2)

        @pl.when(s == 0)
        def _():
            dh_scr[...] = dhf_ref[0, 0]
            dbias_ref[...] = jnp.zeros(dbias_ref.shape, F32)
            dalog_ref[...] = jnp.zeros(dalog_ref.shape, F32)

        dtraw = dt_ref[0]
        dt, A, cs, total = _ssd_scalars(dtraw, bias_ref[0], alog_ref[0], reverse)
        e_row = jnp.exp(cs)
        t_row = jnp.exp(total - cs)
        dcy = jnp.exp(total)
        cols = _rows_to_cols(jnp.concatenate([dt, e_row, t_row, cs], axis=0))
        x = x_ref[0]
        bm = b_ref[0].astype(BF16)
        h = hs_ref[0, 0, 0]
        dh = dh_scr[...]
        dh_bf = dh.astype(BF16)
        bdh = _dot(bm, dh_bf)
        if has_y:
            cm = c_ref[0].astype(BF16)
            dyv = dy_ref[0]
            cb = _dot_nt(cm, bm)
            yoff = _dot(cm, h.astype(BF16))
            dcb = jnp.zeros((CHUNK, CHUNK), F32)
        col_terms, row_terms, ddt_cols, dtot = [], [], [], []
        for r in range(HPG):
            sl = slice(r * HEAD_DIM, (r + 1) * HEAD_DIM)
            dt_c = cols[:, r:r + 1]
            e_c = cols[:, HPG + r:HPG + r + 1]
            t_c = cols[:, 2 * HPG + r:2 * HPG + r + 1]
            xr = x[:, sl]
            xdt = xr * dt_c
            dxdt = t_c * bdh[:, sl]
            d_t = jnp.sum(bdh[:, sl] * xdt, axis=1, keepdims=True)
            col = -(t_c * d_t)
            tot = jnp.sum(t_c * d_t, axis=0, keepdims=True) + dcy[r:r + 1] * jnp.sum(h[:, sl] * dh[:, sl], keepdims=True)
            if has_y:
                dyr = dyv[:, sl]
                lr = _decay_matrix(cols[:, 3 * HPG + r:3 * HPG + r + 1], cs[r:r + 1], reverse)
                w = cb * lr
                gm = _dot_nt(dyr.astype(BF16), xdt.astype(BF16))
                m = gm * w
                dcb = dcb + gm * lr
                dxdt = dxdt + _dot_tn(w.astype(BF16), dyr.astype(BF16))
                col = col + jnp.sum(m, axis=1, keepdims=True) + jnp.sum(yoff[:, sl] * dyr, axis=1, keepdims=True) * e_c
                row_terms.append(-jnp.sum(m, axis=0, keepdims=True))
                e_scr[:, sl] = (e_c * dyr).astype(BF16)
            t_scr[:, sl] = (t_c * xdt).astype(BF16)
            dx_ref[0, :, sl] = dxdt * dt_c
            ddt_cols.append(jnp.sum(dxdt * xr, axis=1, keepdims=True))
            col_terms.append(col)
            dtot.append(tot)
        db = _dot_nt(t_scr[...], dh_bf)
        if has_y:
            dcb_bf = dcb.astype(BF16)
            db = db + _dot_tn(dcb_bf, cm)
            dc_ref[0] = _dot(dcb_bf, bm) + _dot_nt(e_scr[...], h.astype(BF16))
            cte = _dot_tn(cm, e_scr[...])
        db_ref[0] = db
        for r in range(HPG):
            sl = slice(r * HEAD_DIM, (r + 1) * HEAD_DIM)
            new = dh[:, sl] * dcy[r:r + 1]
            if has_y:
                new = new + cte[:, sl]
            dh_scr[:, sl] = new
        dcs = _cols_to_rows(col_terms)
        if has_y:
            dcs = dcs + _stack_rows(row_terms)
        lane = lax.broadcasted_iota(jnp.int32, (HPG, CHUNK), 1)
        dcs = dcs + jnp.where(lane == last, _stack_rows([jnp.broadcast_to(t, (1, CHUNK)) for t in dtot]), 0.0)
        da = _cumsum_lanes(dcs, not reverse)
        ddt = da * A + _cols_to_rows(ddt_cols)
        ddtraw = ddt * _sigmoid(dtraw + bias_ref[0])
        ddt_ref[0] = ddtraw
        dbias_ref[0, 0] += jnp.sum(ddtraw, axis=1, keepdims=True)
        dalog_ref[0, 0] += jnp.sum(da * dt, axis=1, keepdims=True) * A

        @pl.when(s == nc - 1)
        def _():
            dh0_ref[0, 0] = dh_scr[...]

    cidx = lambda b, g, s: (b, chunk_of(s), g)
    in_specs = [
        pl.BlockSpec((1, HPG, CHUNK), lambda b, g, s: (b, rowblk + g, chunk_of(s))),
        pl.BlockSpec((1, HPG, 1), lambda b, g, s: (rowblk + g, 0, 0)),
        pl.BlockSpec((1, HPG, 1), lambda b, g, s: (rowblk + g, 0, 0)),
        pl.BlockSpec((1, CHUNK, GW), cidx),
        pl.BlockSpec((1, CHUNK, D_STATE), lambda b, g, s: (b, chunk_of(s), D_INNER // D_STATE + g)),
    ]
    args = [dtT, bias, alog, xbc, xbc]
    if has_y:
        in_specs.append(pl.BlockSpec((1, CHUNK, D_STATE), lambda b, g, s: (b, chunk_of(s), D_INNER // D_STATE + N_BC + g)))
        args.append(xbc)
    in_specs += [pl.BlockSpec((1, 1, 1, D_STATE, GW), lambda b, g, s: (b, g, chunk_of(s), 0, 0)),
                 pl.BlockSpec((1, 1, D_STATE, GW), lambda b, g, s: (b, g, 0, 0))]
    args += [h_start, dh_final]
    if has_y:
        in_specs.append(pl.BlockSpec((1, CHUNK, GW), cidx))
        args.append(dy)
    out_shape = [jax.ShapeDtypeStruct((Bn, L, D_INNER), F32), jax.ShapeDtypeStruct((Bn, L, N_BC * D_STATE), F32)]
    out_specs = [pl.BlockSpec((1, CHUNK, GW), cidx), pl.BlockSpec((1, CHUNK, D_STATE), cidx)]
    if has_y:
        out_shape.append(jax.ShapeDtypeStruct((Bn, L, N_BC * D_STATE), F32))
        out_specs.append(pl.BlockSpec((1, CHUNK, D_STATE), cidx))
    out_shape += [jax.ShapeDtypeStruct((Bn, N_HEADS, L), F32), jax.ShapeDtypeStruct((Bn, N_BC, HPG, 1), F32),
                  jax.ShapeDtypeStruct((Bn, N_BC, HPG, 1), F32), jax.ShapeDtypeStruct((Bn, N_BC, D_STATE, GW), F32)]
    out_specs += [pl.BlockSpec((1, HPG, CHUNK), lambda b, g, s: (b, g, chunk_of(s))),
                  pl.BlockSpec((1, 1, HPG, 1), lambda b, g, s: (b, g, 0, 0)),
                  pl.BlockSpec((1, 1, HPG, 1), lambda b, g, s: (b, g, 0, 0)),
                  pl.BlockSpec((1, 1, D_STATE, GW), lambda b, g, s: (b, g, 0, 0))]
    scratch = [pltpu.VMEM((D_STATE, GW), F32)] + ([pltpu.VMEM((CHUNK, GW), BF16)] if has_y else []) + [pltpu.VMEM((CHUNK, GW), BF16)]
    res = pl.pallas_call(
        kern, name=name, out_shape=out_shape, grid=(Bn, N_BC, nc), in_specs=in_specs, out_specs=out_specs,
        scratch_shapes=scratch, compiler_params=_cp(("arbitrary", "arbitrary", "arbitrary")))(*args)
    if has_y:
        return res
    dxs, db, ddt, dbias, dalog, dh0 = res
    return dxs, db, None, ddt, dbias, dalog, dh0


def _tri_mask(transposed, reverse):
    sub = lax.broadcasted_iota(jnp.int32, (CHUNK, CHUNK), 0)
    lane = lax.broadcasted_iota(jnp.int32, (CHUNK, CHUNK), 1)
    i, j = (lane, sub) if transposed else (sub, lane)
    return (i <= j) if reverse else (i >= j)


def ssd_fwd(name, dtT, bias, alog, xbc, h0, direction, with_y):
    Bn, L = xbc.shape[:2]
    nc = L // CHUNK
    reverse = direction == 1
    rowblk = direction * N_BC

    def chunk_of(s):
        return (nc - 1 - s) if reverse else s

    def kern(dt_ref, bias_ref, alog_ref, x_ref, b_ref, c_ref, h0_ref, *rest):
        if with_y:
            y_ref, hs_ref, hf_ref, h_scr = rest
        else:
            hs_ref, hf_ref, h_scr = rest
        s = pl.program_id(2)

        @pl.when(s == 0)
        def _():
            h_scr[...] = h0_ref[0, 0]

        dt, _, cs, total = _ssd_scalars(dt_ref[0], bias_ref[0], alog_ref[0], reverse)
        u = cs - jnp.log(dt)
        dtt = jnp.exp(total - u)
        dc = jnp.exp(total)
        x_bf = x_ref[0].astype(BF16)
        bm = b_ref[0]
        h = h_scr[...]
        h_bf = h.astype(BF16)
        hs_ref[0, 0, 0] = h
        bt = bm.T
        if with_y:
            cm = c_ref[0]
            cb = _dot_nt(cm.astype(BF16), bm.astype(BF16))
            cs_cols = _rows_to_cols(cs)
            keep = _tri_mask(False, reverse)
        first = lax.broadcasted_iota(jnp.int32, (1, LANES), 1) < HEAD_DIM
        heads = range(HPG)
        psl = [slice((r // 2) * LANES, (r // 2 + 1) * LANES) for r in heads]
        lhs = []
        if with_y:
            for r in heads:
                cs_col = jnp.broadcast_to(cs_cols[:, r:r + 1], (CHUNK, LANES))
                wf = cb * jnp.exp(jnp.where(keep, cs_col - u[r:r + 1], -jnp.inf))
                lhs.append(jnp.concatenate([wf.astype(BF16), (cm * jnp.exp(cs_col)).astype(BF16)], axis=1))
        bts = [(bt * dtt[r:r + 1]).astype(BF16) for r in heads]
        sts = [_dot(bts[r], x_bf[:, psl[r]]) for r in heads]
        if with_y:
            ys = [_dot(lhs[r], jnp.concatenate([x_bf[:, psl[r]], h_bf[:, psl[r]]], axis=0)) for r in heads]
        for p in range(HPG // 2):
            if with_y:
                y_ref[0, :, psl[2 * p]] = jnp.where(first, ys[2 * p], ys[2 * p + 1])
            dc_p = jnp.where(first, dc[2 * p:2 * p + 1], dc[2 * p + 1:2 * p + 2])
            h_scr[:, psl[2 * p]] = h[:, psl[2 * p]] * dc_p + jnp.where(first, sts[2 * p], sts[2 * p + 1])

        @pl.when(s == nc - 1)
        def _():
            hf_ref[0, 0] = h_scr[...]

    in_specs = [
        pl.BlockSpec((1, HPG, CHUNK), lambda b, g, s: (b, rowblk + g, chunk_of(s))),
        pl.BlockSpec((1, HPG, 1), lambda b, g, s: (rowblk + g, 0, 0)),
        pl.BlockSpec((1, HPG, 1), lambda b, g, s: (rowblk + g, 0, 0)),
        pl.BlockSpec((1, CHUNK, GW), lambda b, g, s: (b, chunk_of(s), g)),
        pl.BlockSpec((1, CHUNK, D_STATE), lambda b, g, s: (b, chunk_of(s), D_INNER // D_STATE + g)),
        pl.BlockSpec((1, CHUNK, D_STATE), lambda b, g, s: (b, chunk_of(s), D_INNER // D_STATE + N_BC + g)),
        pl.BlockSpec((1, 1, D_STATE, GW), lambda b, g, s: (b, g, 0, 0)),
    ]
    out_shape, out_specs = [], []
    if with_y:
        out_shape.append(jax.ShapeDtypeStruct((Bn, L, D_INNER), F32))
        out_specs.append(pl.BlockSpec((1, CHUNK, GW), lambda b, g, s: (b, chunk_of(s), g)))
    out_shape += [jax.ShapeDtypeStruct((Bn, N_BC, nc, D_STATE, GW), F32), jax.ShapeDtypeStruct((Bn, N_BC, D_STATE, GW), F32)]
    out_specs += [pl.BlockSpec((1, 1, 1, D_STATE, GW), lambda b, g, s: (b, g, chunk_of(s), 0, 0)),
                  pl.BlockSpec((1, 1, D_STATE, GW), lambda b, g, s: (b, g, 0, 0))]
    return pl.pallas_call(
        kern, name=name, out_shape=out_shape, grid=(Bn, N_BC, nc), in_specs=in_specs, out_specs=out_specs,
        scratch_shapes=[pltpu.VMEM((D_STATE, GW), F32)],
        compiler_params=_cp(("arbitrary", "arbitrary", "arbitrary")))(dtT, bias, alog, xbc, xbc, xbc, h0)


def ssd_bwd(name, dtT, bias, alog, xbc, h_start, dy, dh_final, direction):
    Bn, L = xbc.shape[:2]
    nc = L // CHUNK
    reverse = direction == 1
    rowblk = direction * N_BC
    has_y = dy is not None
    last = 0 if reverse else CHUNK - 1

    def chunk_of(s):
        return s if reverse else (nc - 1 - s)

    def kern(*refs):
        if has_y:
            (dt_ref, bias_ref, alog_ref, x_ref, b_ref, hs_ref, dhf_ref, c_ref, dy_ref,
             dx_ref, db_ref, ddt_ref, dbias_ref, dalog_ref, dh0_ref, dc_ref, dh_scr) = refs
        else:
            (dt_ref, bias_ref, alog_ref, x_ref, b_ref, hs_ref, dhf_ref,
             dx_ref, db_ref, ddt_ref, dbias_ref, dalog_ref, dh0_ref, dh_scr) = refs
        s = pl.program_id(2)

        @pl.when(s == 0)
        def _():
            dh_scr[...] = dhf_ref[0, 0]
            dbias_ref[...] = jnp.zeros(dbias_ref.shape, F32)
            dalog_ref[...] = jnp.zeros(dalog_ref.shape, F32)

        dtraw = dt_ref[0]
        dt, A, cs, total = _ssd_scalars(dtraw, bias_ref[0], alog_ref[0], reverse)
        u = cs - jnp.log(dt)
        dtt = jnp.exp(total - u)
        dcy = jnp.exp(total)
        u_cols = _rows_to_cols(u)
        x_bf = x_ref[0].astype(BF16)
        bm = b_ref[0]
        bt = bm.T
        h = hs_ref[0, 0, 0]
        dh = dh_scr[...]
        dh_bf = dh.astype(BF16)
        dbt = jnp.zeros((D_STATE, CHUNK), F32)
        if has_y:
            cm = c_ref[0]
            ct = cm.T
            e_row = jnp.exp(cs)
            dy_bf = dy_ref[0].astype(BF16)
            h_bf = h.astype(BF16)
            cbt = _dot_nt(bm.astype(BF16), cm.astype(BF16))
            keep = _tri_mask(True, reverse)
            dcbt = jnp.zeros((CHUNK, CHUNK), F32)
            dct = jnp.zeros((D_STATE, CHUNK), F32)
        tots, out_rows, in_rows, in_cols = [], [], [], []
        first = lax.broadcasted_iota(jnp.int32, (1, LANES), 1) < HEAD_DIM
        heads = range(HPG)
        psl = [slice((r // 2) * LANES, (r // 2 + 1) * LANES) for r in heads]
        mine = [first if r % 2 == 0 else jnp.logical_not(first) for r in heads]
        zeros_bf = jnp.zeros((CHUNK, LANES), BF16)

        def prep(r):
            u_col = jnp.broadcast_to(u_cols[:, r:r + 1], (CHUNK, LANES))
            bs = (bm * jnp.exp(total[r:r + 1] - u_col)).astype(BF16)
            if not has_y:
                return bs, None
            et = jnp.exp(jnp.where(keep, cs[r:r + 1] - u_col, -jnp.inf))
            return jnp.concatenate([(cbt * et).astype(BF16), bs], axis=1), et

        def matmuls(r, lhs):
            p2raw = _dot_nt(dh_bf[:, psl[r]], jnp.where(mine[r], x_bf[:, psl[r]], zeros_bf))
            if not has_y:
                return p2raw, None, None, _dot(lhs, dh_bf[:, psl[r]])
            a1 = _dot_nt(jnp.concatenate([x_bf[:, psl[r]], h_bf[:, psl[r]]], axis=0),
                         jnp.where(mine[r], dy_bf[:, psl[r]], zeros_bf))
            new = _dot((ct * e_row[r:r + 1]).astype(BF16), dy_bf[:, psl[r]])
            dx = _dot(lhs, jnp.concatenate([dy_bf[:, psl[r]], dh_bf[:, psl[r]]], axis=0))
            return p2raw, a1, new, dx

        def post(r, p2raw, a1, et, dbt, dcbt, dct):
            if has_y:
                pt = a1[0:CHUNK] * et
                dcbt = dcbt + pt
                mt = pt * cbt
                ph = a1[CHUNK:] * e_row[r:r + 1]
                dct = dct + ph
                out_rows.append(_colsum(mt + ct * ph))
                in_cols.append(jnp.sum(mt, axis=1, keepdims=True))
            p2 = p2raw * dtt[r:r + 1]
            dbt = dbt + p2
            t_term = _colsum(bt * p2)
            in_rows.append(t_term)
            hdh = h[:, psl[r]] * dh[:, psl[r]]
            tot = jnp.sum(t_term, axis=1, keepdims=True) + dcy[r:r + 1] * jnp.sum(jnp.where(mine[r], hdh, 0.0), keepdims=True)
            tots.append(jnp.broadcast_to(tot, (1, CHUNK)))
            return dbt, dcbt, dct

        if not has_y:
            dcbt = dct = None
        dxs, news, pending = [], [], []
        batch = HPG
        for r0 in range(0, HPG, batch):
            preps = [prep(r) for r in range(r0, r0 + batch)]
            mms = [matmuls(r, preps[r - r0][0]) for r in range(r0, r0 + batch)]
            for args in pending:
                dbt, dcbt, dct = post(*args, dbt, dcbt, dct)
            pending = [(r, mms[r - r0][0], mms[r - r0][1], preps[r - r0][1]) for r in range(r0, r0 + batch)]
            dxs += [m[3] for m in mms]
            news += [m[2] for m in mms]
        for args in pending:
            dbt, dcbt, dct = post(*args, dbt, dcbt, dct)
        for p in range(HPG // 2):
            dx_ref[0, :, psl[2 * p]] = jnp.where(first, dxs[2 * p], dxs[2 * p + 1])
            new = dh[:, psl[2 * p]] * jnp.where(first, dcy[2 * p:2 * p + 1], dcy[2 * p + 1:2 * p + 2])
            if has_y:
                new = new + jnp.where(first, news[2 * p], news[2 * p + 1])
            dh_scr[:, psl[2 * p]] = new
        db = dbt.T
        if has_y:
            dcbt_bf = dcbt.astype(BF16)
            db = db + _dot(dcbt_bf, cm.astype(BF16))
            dc_ref[0] = dct.T + _dot_tn(dcbt_bf, bm.astype(BF16))
        db_ref[0] = db
        s_row = _stack_rows(in_rows)
        lane = lax.broadcasted_iota(jnp.int32, (HPG, CHUNK), 1)
        dcs = jnp.where(lane == last, _stack_rows(tots), 0.0)
        if has_y:
            s_row = s_row + _cols_to_rows(in_cols)
            dcs = dcs + _stack_rows(out_rows)
        dcs = dcs - s_row
        da = _cumsum_lanes(dcs, not reverse)
        ddt = da * A + jnp.where(dt > 0.0, s_row / dt, 0.0)
        ddtraw = ddt * _sigmoid(dtraw + bias_ref[0])
        ddt_ref[0] = ddtraw
        dbias_ref[0, 0] += jnp.sum(ddtraw, axis=1, keepdims=True)
        dalog_ref[0, 0] += jnp.sum(da * dt, axis=1, keepdims=True) * A

        @pl.when(s == nc - 1)
        def _():
            dh0_ref[0, 0] = dh_scr[...]

    cidx = lambda b, g, s: (b, chunk_of(s), g)
    hidx = lambda b, g, s: (b, g, 0, 0)
    in_specs = [
        pl.BlockSpec((1, HPG, CHUNK), lambda b, g, s: (b, rowblk + g, chunk_of(s))),
        pl.BlockSpec((1, HPG, 1), lambda b, g, s: (rowblk + g, 0, 0)),
        pl.BlockSpec((1, HPG, 1), lambda b, g, s: (rowblk + g, 0, 0)),
        pl.BlockSpec((1, CHUNK, GW), cidx),
        pl.BlockSpec((1, CHUNK, D_STATE), lambda b, g, s: (b, chunk_of(s), D_INNER // D_STATE + g)),
        pl.BlockSpec((1, 1, 1, D_STATE, GW), lambda b, g, s: (b, g, chunk_of(s), 0, 0)),
        pl.BlockSpec((1, 1, D_STATE, GW), hidx),
    ]
    args = [dtT, bias, alog, xbc, xbc, h_start, dh_final]
    if has_y:
        in_specs += [pl.BlockSpec((1, CHUNK, D_STATE), lambda b, g, s: (b, chunk_of(s), D_INNER // D_STATE + N_BC + g)),
                     pl.BlockSpec((1, CHUNK, GW), cidx)]
        args += [xbc, dy]
    out_shape = [jax.ShapeDtypeStruct((Bn, L, D_INNER), F32), jax.ShapeDtypeStruct((Bn, L, N_BC * D_STATE), F32),
                 jax.ShapeDtypeStruct((Bn, N_HEADS, L), F32), jax.ShapeDtypeStruct((Bn, N_BC, HPG, 1), F32),
                 jax.ShapeDtypeStruct((Bn, N_BC, HPG, 1), F32), jax.ShapeDtypeStruct((Bn, N_BC, D_STATE, GW), F32)]
    out_specs = [pl.BlockSpec((1, CHUNK, GW), cidx), pl.BlockSpec((1, CHUNK, D_STATE), cidx),
                 pl.BlockSpec((1, HPG, CHUNK), lambda b, g, s: (b, g, chunk_of(s))),
                 pl.BlockSpec((1, 1, HPG, 1), hidx), pl.BlockSpec((1, 1, HPG, 1), hidx), pl.BlockSpec((1, 1, D_STATE, GW), hidx)]
    if has_y:
        out_shape.append(jax.ShapeDtypeStruct((Bn, L, N_BC * D_STATE), F32))
        out_specs.append(pl.BlockSpec((1, CHUNK, D_STATE), cidx))
    res = pl.pallas_call(
        kern, name=name, out_shape=out_shape, grid=(Bn, N_BC, nc), in_specs=in_specs, out_specs=out_specs,
        scratch_shapes=[pltpu.VMEM((D_STATE, GW), F32)],
        compiler_params=_cp(("arbitrary", "arbitrary", "arbitrary")))(*args)
    dxs, db, ddt, dbias, dalog, dh0 = res[:6]
    return dxs, db, (res[6] if has_y else None), ddt, dbias, dalog, dh0


GPS = 4


def ssd_fwd3(name, dtT, bias, alog, xbc, h0, direction, with_y, y_add=None):
    Bn, L = xbc.shape[:2]
    nc = L // CHUNK
    reverse = direction == 1
    blk0 = direction * (N_BC // GPS)
    gs = range(GPS)
    has_add = y_add is not None

    def chunk_of(s):
        return (nc - 1 - s) if reverse else s

    def kern(dt_ref, bias_ref, alog_ref, x_ref, b_ref, c_ref, h0_ref, *rest):
        if has_add:
            yp_ref, dsk_ref, rest = rest[0], rest[1], rest[2:]
        if with_y:
            y_ref, hs_ref, hf_ref, h_scr = rest
        else:
            hs_ref, hf_ref, h_scr = rest
        s = pl.program_id(2)

        @pl.when(s == 0)
        def _():
            h_scr[...] = h0_ref[0]

        first = lax.broadcasted_iota(jnp.int32, (1, LANES), 1) < HEAD_DIM
        heads = range(HPG)
        psl = [slice((r // 2) * LANES, (r // 2 + 1) * LANES) for r in heads]
        keep = _tri_mask(False, reverse)
        sc, x_bf, bm, h, h_bf, bt, cm, cb, cs_cols = [], [], [], [], [], [], [], [], []
        for g in gs:
            dt, _, cs, total = _ssd_scalars(dt_ref[0, g * HPG:(g + 1) * HPG], bias_ref[g], alog_ref[g], reverse)
            u = cs - jnp.log(dt)
            sc.append((cs, u, jnp.exp(total - u), jnp.exp(total)))
            x_bf.append(x_ref[0, :, g * GW:(g + 1) * GW].astype(BF16))
            bm.append(b_ref[0, :, g * D_STATE:(g + 1) * D_STATE])
            h.append(h_scr[g])
            h_bf.append(h[g].astype(BF16))
            hs_ref[0, g, 0] = h[g]
            bt.append(bm[g].T)
            if with_y:
                cm.append(c_ref[0, :, g * D_STATE:(g + 1) * D_STATE])
                cb.append(_dot_nt(cm[g].astype(BF16), bm[g].astype(BF16)))
                cs_cols.append(_rows_to_cols(cs))
        lhs = [[] for _ in gs]
        if with_y:
            for g in gs:
                cs, u = sc[g][0], sc[g][1]
                for r in heads:
                    cs_col = jnp.broadcast_to(cs_cols[g][:, r:r + 1], (CHUNK, LANES))
                    wf = cb[g] * jnp.exp(jnp.where(keep, cs_col - u[r:r + 1], -jnp.inf))
                    lhs[g].append(jnp.concatenate([wf.astype(BF16), (cm[g] * jnp.exp(cs_col)).astype(BF16)], axis=1))
        bts = [[(bt[g] * sc[g][2][r:r + 1]).astype(BF16) for r in heads] for g in gs]
        sts = [[_dot(bts[g][r], x_bf[g][:, psl[r]]) for r in heads] for g in gs]
        if with_y:
            ys = [[_dot(lhs[g][r], jnp.concatenate([x_bf[g][:, psl[r]], h_bf[g][:, psl[r]]], axis=0)) for r in heads] for g in gs]
        for g in gs:
            dc = sc[g][3]
            for p in range(HPG // 2):
                if with_y:
                    cols = slice(g * GW + p * LANES, g * GW + (p + 1) * LANES)
                    yv = jnp.where(first, ys[g][2 * p], ys[g][2 * p + 1])
                    if has_add:
                        yv = yv + yp_ref[0, :, cols] + dsk_ref[:, cols] * x_ref[0, :, cols]
                    y_ref[0, :, cols] = yv
                dc_p = jnp.where(first, dc[2 * p:2 * p + 1], dc[2 * p + 1:2 * p + 2])
                h_scr[g, :, psl[2 * p]] = h[g][:, psl[2 * p]] * dc_p + jnp.where(first, sts[g][2 * p], sts[g][2 * p + 1])

        @pl.when(s == nc - 1)
        def _():
            hf_ref[0] = h_scr[...]

    nb = D_INNER // (GPS * D_STATE)
    in_specs = [
        pl.BlockSpec((1, GPS * HPG, CHUNK), lambda b, g, s: (b, blk0 + g, chunk_of(s))),
        pl.BlockSpec((GPS, HPG, 1), lambda b, g, s: (blk0 + g, 0, 0)),
        pl.BlockSpec((GPS, HPG, 1), lambda b, g, s: (blk0 + g, 0, 0)),
        pl.BlockSpec((1, CHUNK, GPS * GW), lambda b, g, s: (b, chunk_of(s), g)),
        pl.BlockSpec((1, CHUNK, GPS * D_STATE), lambda b, g, s: (b, chunk_of(s), nb + g)),
        pl.BlockSpec((1, CHUNK, GPS * D_STATE), lambda b, g, s: (b, chunk_of(s), nb + N_BC // GPS + g)),
        pl.BlockSpec((1, GPS, D_STATE, GW), lambda b, g, s: (b, g, 0, 0)),
    ]
    args = [dtT, bias, alog, xbc, xbc, xbc, h0]
    if has_add:
        in_specs += [pl.BlockSpec((1, CHUNK, GPS * GW), lambda b, g, s: (b, chunk_of(s), g)),
                     pl.BlockSpec((1, GPS * GW), lambda b, g, s: (0, g))]
        args += list(y_add)
    out_shape, out_specs = [], []
    if with_y:
        out_shape.append(jax.ShapeDtypeStruct((Bn, L, D_INNER), F32))
        out_specs.append(pl.BlockSpec((1, CHUNK, GPS * GW), lambda b, g, s: (b, chunk_of(s), g)))
    out_shape += [jax.ShapeDtypeStruct((Bn, N_BC, nc, D_STATE, GW), F32), jax.ShapeDtypeStruct((Bn, N_BC, D_STATE, GW), F32)]
    out_specs += [pl.BlockSpec((1, GPS, 1, D_STATE, GW), lambda b, g, s: (b, g, chunk_of(s), 0, 0)),
                  pl.BlockSpec((1, GPS, D_STATE, GW), lambda b, g, s: (b, g, 0, 0))]
    return pl.pallas_call(
        kern, name=name, out_shape=out_shape, grid=(Bn, N_BC // GPS, nc), in_specs=in_specs, out_specs=out_specs,
        scratch_shapes=[pltpu.VMEM((GPS, D_STATE, GW), F32)],
        compiler_params=_cp(("arbitrary", "arbitrary", "arbitrary")))(*args)


def ssd_bwd3(name, dtT, bias, alog, xbc, h_start, dy, dh_final, direction, dx_add=None):
    Bn, L = xbc.shape[:2]
    nc = L // CHUNK
    reverse = direction == 1
    blk0 = direction * (N_BC // GPS)
    has_y = dy is not None
    has_add = dx_add is not None
    assert has_y or not has_add
    last = 0 if reverse else CHUNK - 1
    gs = range(GPS)

    def chunk_of(s):
        return s if reverse else (nc - 1 - s)

    def kern(*refs):
        if has_add:
            dxp_ref, dsk_ref = refs[9], refs[10]
            refs = refs[:9] + refs[11:]
        if has_y:
            (dt_ref, bias_ref, alog_ref, x_ref, b_ref, hs_ref, dhf_ref, c_ref, dy_ref,
             dx_ref, db_ref, ddt_ref, dbias_ref, dalog_ref, dh0_ref, dc_ref, dh_scr) = refs
        else:
            (dt_ref, bias_ref, alog_ref, x_ref, b_ref, hs_ref, dhf_ref,
             dx_ref, db_ref, ddt_ref, dbias_ref, dalog_ref, dh0_ref, dh_scr) = refs
        s = pl.program_id(2)

        @pl.when(s == 0)
        def _():
            dh_scr[...] = dhf_ref[0]
            dbias_ref[...] = jnp.zeros(dbias_ref.shape, F32)
            dalog_ref[...] = jnp.zeros(dalog_ref.shape, F32)

        first = lax.broadcasted_iota(jnp.int32, (1, LANES), 1) < HEAD_DIM
        heads = range(HPG)
        psl = [slice((r // 2) * LANES, (r // 2 + 1) * LANES) for r in heads]
        mine = [first if r % 2 == 0 else jnp.logical_not(first) for r in heads]
        zeros_bf = jnp.zeros((CHUNK, LANES), BF16)
        keep = _tri_mask(True, reverse)
        ctx = []
        for g in gs:
            dtraw = dt_ref[0, g * HPG:(g + 1) * HPG]
            dt, A, cs, total = _ssd_scalars(dtraw, bias_ref[g], alog_ref[g], reverse)
            u = cs - jnp.log(dt)
            c = dict(dtraw=dtraw, dt=dt, A=A, cs=cs, total=total, u=u, dtt=jnp.exp(total - u), dcy=jnp.exp(total),
                     u_cols=_rows_to_cols(u), x_bf=x_ref[0, :, g * GW:(g + 1) * GW].astype(BF16),
                     bm=b_ref[0, :, g * D_STATE:(g + 1) * D_STATE], h=hs_ref[0, g, 0], dh=dh_scr[g])
            c["bt"] = c["bm"].T
            c["dh_bf"] = c["dh"].astype(BF16)
            if has_y:
                c["cm"] = c_ref[0, :, g * D_STATE:(g + 1) * D_STATE]
                c["ct"] = c["cm"].T
                c["e_row"] = jnp.exp(cs)
                c["dy_bf"] = dy_ref[0, :, g * GW:(g + 1) * GW].astype(BF16)
                c["h_bf"] = c["h"].astype(BF16)
                c["cbt"] = _dot_nt(c["bm"].astype(BF16), c["cm"].astype(BF16))
            ctx.append(c)
        for c in ctx:
            c["lhs"], c["et"] = [], []
            for r in heads:
                u_col = jnp.broadcast_to(c["u_cols"][:, r:r + 1], (CHUNK, LANES))
                bs = (c["bm"] * jnp.exp(c["total"][r:r + 1] - u_col)).astype(BF16)
                if has_y:
                    et = jnp.exp(jnp.where(keep, c["cs"][r:r + 1] - u_col, -jnp.inf))
                    c["et"].append(et)
                    c["lhs"].append(jnp.concatenate([(c["cbt"] * et).astype(BF16), bs], axis=1))
                else:
                    c["lhs"].append(bs)
        for c in ctx:
            c["p2raw"] = [_dot_nt(c["dh_bf"][:, psl[r]], jnp.where(mine[r], c["x_bf"][:, psl[r]], zeros_bf)) for r in heads]
            if has_y:
                c["a1"] = [_dot_nt(jnp.concatenate([c["x_bf"][:, psl[r]], c["h_bf"][:, psl[r]]], axis=0),
                                   jnp.where(mine[r], c["dy_bf"][:, psl[r]], zeros_bf)) for r in heads]
                c["news"] = [_dot((c["ct"] * c["e_row"][r:r + 1]).astype(BF16), c["dy_bf"][:, psl[r]]) for r in heads]
                c["dxs"] = [_dot(c["lhs"][r], jnp.concatenate([c["dy_bf"][:, psl[r]], c["dh_bf"][:, psl[r]]], axis=0)) for r in heads]
            else:
                c["dxs"] = [_dot(c["lhs"][r], c["dh_bf"][:, psl[r]]) for r in heads]
        for g, c in enumerate(ctx):
            dbt = jnp.zeros((D_STATE, CHUNK), F32)
            dcbt = jnp.zeros((CHUNK, CHUNK), F32)
            dct = jnp.zeros((D_STATE, CHUNK), F32)
            tots, out_rows, in_rows, in_cols = [], [], [], []
            for r in heads:
                if has_y:
                    pt = c["a1"][r][0:CHUNK] * c["et"][r]
                    dcbt = dcbt + pt
                    mt = pt * c["cbt"]
                    ph = c["a1"][r][CHUNK:] * c["e_row"][r:r + 1]
                    dct = dct + ph
                    out_rows.append(_colsum(mt + c["ct"] * ph))
                    in_cols.append(jnp.sum(mt, axis=1, keepdims=True))
                p2 = c["p2raw"][r] * c["dtt"][r:r + 1]
                dbt = dbt + p2
                t_term = _colsum(c["bt"] * p2)
                in_rows.append(t_term)
                hdh = c["h"][:, psl[r]] * c["dh"][:, psl[r]]
                tot = jnp.sum(t_term, axis=1, keepdims=True) + c["dcy"][r:r + 1] * jnp.sum(jnp.where(mine[r], hdh, 0.0), keepdims=True)
                tots.append(jnp.broadcast_to(tot, (1, CHUNK)))
            for p in range(HPG // 2):
                cols = slice(g * GW + p * LANES, g * GW + (p + 1) * LANES)
                dxv = jnp.where(first, c["dxs"][2 * p], c["dxs"][2 * p + 1])
                if has_add:
                    dxv = dxv + dxp_ref[0, :, cols] + dsk_ref[:, cols] * dy_ref[0, :, cols]
                dx_ref[0, :, cols] = dxv
                new = c["dh"][:, psl[2 * p]] * jnp.where(first, c["dcy"][2 * p:2 * p + 1], c["dcy"][2 * p + 1:2 * p + 2])
                if has_y:
                    new = new + jnp.where(first, c["news"][2 * p], c["news"][2 * p + 1])
                dh_scr[g, :, psl[2 * p]] = new
            db = dbt.T
            if has_y:
                dcbt_bf = dcbt.astype(BF16)
                db = db + _dot(dcbt_bf, c["cm"].astype(BF16))
                dc_ref[0, :, g * D_STATE:(g + 1) * D_STATE] = dct.T + _dot_tn(dcbt_bf, c["bm"].astype(BF16))
            db_ref[0, :, g * D_STATE:(g + 1) * D_STATE] = db
            s_row = _stack_rows(in_rows)
            lane = lax.broadcasted_iota(jnp.int32, (HPG, CHUNK), 1)
            dcs = jnp.where(lane == last, _stack_rows(tots), 0.0)
            if has_y:
                s_row = s_row + _cols_to_rows(in_cols)
                dcs = dcs + _stack_rows(out_rows)
            dcs = dcs - s_row
            da = _cumsum_lanes(dcs, not reverse)
            ddt = da * c["A"] + jnp.where(c["dt"] > 0.0, s_row / c["dt"], 0.0)
            ddtraw = ddt * _sigmoid(c["dtraw"] + bias_ref[g])
            ddt_ref[0, g * HPG:(g + 1) * HPG] = ddtraw
            dbias_ref[0, g] += jnp.sum(ddtraw, axis=1, keepdims=True)
            dalog_ref[0, g] += jnp.sum(da * c["dt"], axis=1, keepdims=True) * c["A"]

        @pl.when(s == nc - 1)
        def _():
            dh0_ref[0] = dh_scr[...]

    nb = D_INNER // (GPS * D_STATE)
    cidx = lambda b, g, s: (b, chunk_of(s), g)
    hidx = lambda b, g, s: (b, g, 0, 0)
    in_specs = [
        pl.BlockSpec((1, GPS * HPG, CHUNK), lambda b, g, s: (b, blk0 + g, chunk_of(s))),
        pl.BlockSpec((GPS, HPG, 1), lambda b, g, s: (blk0 + g, 0, 0)),
        pl.BlockSpec((GPS, HPG, 1), lambda b, g, s: (blk0 + g, 0, 0)),
        pl.BlockSpec((1, CHUNK, GPS * GW), cidx),
        pl.BlockSpec((1, CHUNK, GPS * D_STATE), lambda b, g, s: (b, chunk_of(s), nb + g)),
        pl.BlockSpec((1, GPS, 1, D_STATE, GW), lambda b, g, s: (b, g, chunk_of(s), 0, 0)),
        pl.BlockSpec((1, GPS, D_STATE, GW), hidx),
    ]
    args = [dtT, bias, alog, xbc, xbc, h_start, dh_final]
    if has_y:
        in_specs += [pl.BlockSpec((1, CHUNK, GPS * D_STATE), lambda b, g, s: (b, chunk_of(s), nb + N_BC // GPS + g)),
                     pl.BlockSpec((1, CHUNK, GPS * GW), cidx)]
        args += [xbc, dy]
    if has_add:
        in_specs += [pl.BlockSpec((1, CHUNK, GPS * GW), cidx), pl.BlockSpec((1, GPS * GW), lambda b, g, s: (0, g))]
        args += list(dx_add)
    out_shape = [jax.ShapeDtypeStruct((Bn, L, D_INNER), F32), jax.ShapeDtypeStruct((Bn, L, N_BC * D_STATE), F32),
                 jax.ShapeDtypeStruct((Bn, N_HEADS, L), F32), jax.ShapeDtypeStruct((Bn, N_BC, HPG, 1), F32),
                 jax.ShapeDtypeStruct((Bn, N_BC, HPG, 1), F32), jax.ShapeDtypeStruct((Bn, N_BC, D_STATE, GW), F32)]
    out_specs = [pl.BlockSpec((1, CHUNK, GPS * GW), cidx), pl.BlockSpec((1, CHUNK, GPS * D_STATE), cidx),
                 pl.BlockSpec((1, GPS * HPG, CHUNK), lambda b, g, s: (b, g, chunk_of(s))),
                 pl.BlockSpec((1, GPS, HPG, 1), hidx), pl.BlockSpec((1, GPS, HPG, 1), hidx), pl.BlockSpec((1, GPS, D_STATE, GW), hidx)]
    if has_y:
        out_shape.append(jax.ShapeDtypeStruct((Bn, L, N_BC * D_STATE), F32))
        out_specs.append(pl.BlockSpec((1, CHUNK, GPS * D_STATE), cidx))
    res = pl.pallas_call(
        kern, name=name, out_shape=out_shape, grid=(Bn, N_BC // GPS, nc), in_specs=in_specs, out_specs=out_specs,
        scratch_shapes=[pltpu.VMEM((GPS, D_STATE, GW), F32)],
        compiler_params=_cp(("arbitrary", "arbitrary", "arbitrary")))(*args)
    dxs, db, ddt, dbias, dalog, dh0 = res[:6]
    return dxs, db, (res[6] if has_y else None), ddt, dbias, dalog, dh0


def _dot_split2(v, sel):
    hi = v.astype(BF16)
    mid = (v - hi.astype(F32)).astype(BF16)
    return _dot(hi, sel) + _dot(mid, sel)


def ssd_tables():
    lane = jnp.arange(LANES)[:, None]
    col = jnp.arange(2 * GW)[None, :]
    expand = (lane == jnp.where(col < GW, HPG + col // HEAD_DIM, 2 * HPG + (col - GW) // HEAD_DIM)).astype(BF16)
    ch = jnp.arange(GW)[:, None] // HEAD_DIM
    out = jnp.arange(2 * LANES)[None, :]
    seg = ((out == ch) | (out == LANES + HPG + ch)).astype(BF16)
    return expand, seg


def _dc_lanes(dc, first):
    return jnp.concatenate([jnp.where(first, dc[2 * p:2 * p + 1], dc[2 * p + 1:2 * p + 2]) for p in range(HPG // 2)], axis=1)


def ssd_fwd2(name, dtT, bias, alog, xbc, h0, tables, direction, with_y):
    Bn, L = xbc.shape[:2]
    nc = L // CHUNK
    reverse = direction == 1
    rowblk = direction * N_BC
    expand = tables[0]

    def chunk_of(s):
        return (nc - 1 - s) if reverse else s

    def kern(dt_ref, bias_ref, alog_ref, x_ref, b_ref, c_ref, h0_ref, xp_ref, *rest):
        if with_y:
            y_ref, hs_ref, hf_ref, h_scr = rest
        else:
            hs_ref, hf_ref, h_scr = rest
        s = pl.program_id(2)

        @pl.when(s == 0)
        def _():
            h_scr[...] = h0_ref[0, 0]

        dt, _, cs, total = _ssd_scalars(dt_ref[0], bias_ref[0], alog_ref[0], reverse)
        u = cs - jnp.log(dt)
        dtt = jnp.exp(total - u)
        cols = _rows_to_cols(jnp.concatenate([cs, dtt, jnp.exp(cs)], axis=0))
        wide = _dot_split2(cols, xp_ref[...])
        dtt_x, e_x = wide[:, 0:GW], wide[:, GW:]
        first = lax.broadcasted_iota(jnp.int32, (1, LANES), 1) < HEAD_DIM
        x = x_ref[0]
        x_bf = x.astype(BF16)
        bm = b_ref[0]
        h = h_scr[...]
        hs_ref[0, 0, 0] = h
        st = _dot(bm.T.astype(BF16), (x * dtt_x).astype(BF16))
        h_scr[...] = h * _dc_lanes(jnp.exp(total), first) + st
        if with_y:
            cm = c_ref[0].astype(BF16)
            cb = _dot_nt(cm, bm.astype(BF16))
            yoff = _dot(cm, h.astype(BF16)) * e_x
            keep = _tri_mask(False, reverse)
            wfs = []
            for r in range(HPG):
                cs_col = jnp.broadcast_to(cols[:, r:r + 1], (CHUNK, LANES))
                wfs.append((cb * jnp.exp(jnp.where(keep, cs_col - u[r:r + 1], -jnp.inf))).astype(BF16))
            yd = [_dot(wfs[r], x_bf[:, (r // 2) * LANES:(r // 2 + 1) * LANES]) for r in range(HPG)]
            for p in range(HPG // 2):
                psl = slice(p * LANES, (p + 1) * LANES)
                y_ref[0, :, psl] = jnp.where(first, yd[2 * p], yd[2 * p + 1]) + yoff[:, psl]

        @pl.when(s == nc - 1)
        def _():
            hf_ref[0, 0] = h_scr[...]

    in_specs = [
        pl.BlockSpec((1, HPG, CHUNK), lambda b, g, s: (b, rowblk + g, chunk_of(s))),
        pl.BlockSpec((1, HPG, 1), lambda b, g, s: (rowblk + g, 0, 0)),
        pl.BlockSpec((1, HPG, 1), lambda b, g, s: (rowblk + g, 0, 0)),
        pl.BlockSpec((1, CHUNK, GW), lambda b, g, s: (b, chunk_of(s), g)),
        pl.BlockSpec((1, CHUNK, D_STATE), lambda b, g, s: (b, chunk_of(s), D_INNER // D_STATE + g)),
        pl.BlockSpec((1, CHUNK, D_STATE), lambda b, g, s: (b, chunk_of(s), D_INNER // D_STATE + N_BC + g)),
        pl.BlockSpec((1, 1, D_STATE, GW), lambda b, g, s: (b, g, 0, 0)),
        pl.BlockSpec(expand.shape, lambda b, g, s: (0, 0)),
    ]
    out_shape, out_specs = [], []
    if with_y:
        out_shape.append(jax.ShapeDtypeStruct((Bn, L, D_INNER), F32))
        out_specs.append(pl.BlockSpec((1, CHUNK, GW), lambda b, g, s: (b, chunk_of(s), g)))
    out_shape += [jax.ShapeDtypeStruct((Bn, N_BC, nc, D_STATE, GW), F32), jax.ShapeDtypeStruct((Bn, N_BC, D_STATE, GW), F32)]
    out_specs += [pl.BlockSpec((1, 1, 1, D_STATE, GW), lambda b, g, s: (b, g, chunk_of(s), 0, 0)),
                  pl.BlockSpec((1, 1, D_STATE, GW), lambda b, g, s: (b, g, 0, 0))]
    return pl.pallas_call(
        kern, name=name, out_shape=out_shape, grid=(Bn, N_BC, nc), in_specs=in_specs, out_specs=out_specs,
        scratch_shapes=[pltpu.VMEM((D_STATE, GW), F32)],
        compiler_params=_cp(("arbitrary", "arbitrary", "arbitrary")))(dtT, bias, alog, xbc, xbc, xbc, h0, expand)


def ssd_bwd2(name, dtT, bias, alog, xbc, h_start, dy, dh_final, tables, direction):
    Bn, L = xbc.shape[:2]
    nc = L // CHUNK
    reverse = direction == 1
    rowblk = direction * N_BC
    has_y = dy is not None
    last = 0 if reverse else CHUNK - 1
    expand, seg = tables

    def chunk_of(s):
        return s if reverse else (nc - 1 - s)

    def kern(*refs):
        if has_y:
            (dt_ref, bias_ref, alog_ref, x_ref, b_ref, hs_ref, dhf_ref, xp_ref, seg_ref, c_ref, dy_ref,
             dx_ref, db_ref, ddt_ref, dbias_ref, dalog_ref, dh0_ref, dc_ref, dh_scr) = refs
        else:
            (dt_ref, bias_ref, alog_ref, x_ref, b_ref, hs_ref, dhf_ref, xp_ref, seg_ref,
             dx_ref, db_ref, ddt_ref, dbias_ref, dalog_ref, dh0_ref, dh_scr) = refs
        s = pl.program_id(2)

        @pl.when(s == 0)
        def _():
            dh_scr[...] = dhf_ref[0, 0]
            dbias_ref[...] = jnp.zeros(dbias_ref.shape, F32)
            dalog_ref[...] = jnp.zeros(dalog_ref.shape, F32)

        first = lax.broadcasted_iota(jnp.int32, (1, LANES), 1) < HEAD_DIM
        heads = range(HPG)
        psl = [slice((r // 2) * LANES, (r // 2 + 1) * LANES) for r in heads]
        x = x_ref[0]
        bm = b_ref[0].astype(BF16)
        h = hs_ref[0, 0, 0]
        dh = dh_scr[...]
        dh_bf = dh.astype(BF16)
        bdh = _dot(bm, dh_bf)
        if has_y:
            cm = c_ref[0].astype(BF16)
            dyv = dy_ref[0]
            dy_bf = dyv.astype(BF16)
            x_bf = x.astype(BF16)
            h_bf = h.astype(BF16)
            cbt = _dot_nt(bm, cm)
            ch = _dot(cm, h_bf)
            zeros_bf = jnp.zeros((CHUNK, LANES), BF16)
            gts = [_dot_nt(x_bf[:, psl[r]], jnp.where(first if r % 2 == 0 else jnp.logical_not(first), dy_bf[:, psl[r]], zeros_bf))
                   for r in heads]
            ct_bf = c_ref[0].T.astype(BF16)
        dtraw = dt_ref[0]
        dt, A, cs, total = _ssd_scalars(dtraw, bias_ref[0], alog_ref[0], reverse)
        u = cs - jnp.log(dt)
        dtt = jnp.exp(total - u)
        dcy = jnp.exp(total)
        cols = _rows_to_cols(jnp.concatenate([u, dtt, jnp.exp(cs)], axis=0))
        wide = _dot_split2(cols, xp_ref[...])
        dtt_x, e_x = wide[:, 0:GW], wide[:, GW:]
        term2 = bdh * dtt_x
        dbt = _dot_nt(dh_bf, (x * dtt_x).astype(BF16))
        sums = _dot_split2(term2 * x, seg_ref[:, LANES:])
        new_dh = dh * _dc_lanes(dcy, first)
        if has_y:
            dye = dyv * e_x
            dye_bf = dye.astype(BF16)
            dct = _dot_nt(h_bf, dye_bf)
            new_dh = new_dh + _dot(ct_bf, dye_bf)
            sums = sums + _dot_split2(ch * dye, seg_ref[:, 0:LANES])
            keep = _tri_mask(True, reverse)
            ets = []
            for r in heads:
                u_col = jnp.broadcast_to(cols[:, r:r + 1], (CHUNK, LANES))
                ets.append(jnp.exp(jnp.where(keep, cs[r:r + 1] - u_col, -jnp.inf)))
            wts = [(cbt * ets[r]).astype(BF16) for r in heads]
            dxd = [_dot(wts[r], dy_bf[:, psl[r]]) for r in heads]
            dcbt = jnp.zeros((CHUNK, CHUNK), F32)
            out_rows, in_cols = [], []
            for r in heads:
                pt = gts[r] * ets[r]
                dcbt = dcbt + pt
                mt = pt * cbt
                out_rows.append(_colsum(mt))
                in_cols.append(jnp.sum(mt, axis=1, keepdims=True))
            for p in range(HPG // 2):
                dx_ref[0, :, psl[2 * p]] = jnp.where(first, dxd[2 * p], dxd[2 * p + 1]) + term2[:, psl[2 * p]]
            dcbt_bf = dcbt.astype(BF16)
            db_ref[0] = dbt.T + _dot(dcbt_bf, cm)
            dc_ref[0] = dct.T + _dot_tn(dcbt_bf, bm)
        else:
            dx_ref[0] = term2
            db_ref[0] = dbt.T
        dh_scr[...] = new_dh
        sums_t = sums.T
        s_row = sums_t[HPG:2 * HPG]
        hdh = _colsum(h * dh)
        lanes_w = lax.broadcasted_iota(jnp.int32, (1, GW), 1)
        hd = _stack_rows([jnp.sum(jnp.where(lanes_w // HEAD_DIM == r, hdh, 0.0), axis=1, keepdims=True) for r in range(HPG)])
        tot = jnp.sum(s_row, axis=1, keepdims=True) + dcy * hd
        lane = lax.broadcasted_iota(jnp.int32, (HPG, CHUNK), 1)
        dcs = jnp.where(lane == last, tot, 0.0)
        if has_y:
            s_row = s_row + _cols_to_rows(in_cols)
            dcs = dcs + _stack_rows(out_rows) + sums_t[0:HPG]
        dcs = dcs - s_row
        da = _cumsum_lanes(dcs, not reverse)
        ddt = da * A + jnp.where(dt > 0.0, s_row / dt, 0.0)
        ddtraw = ddt * _sigmoid(dtraw + bias_ref[0])
        ddt_ref[0] = ddtraw
        dbias_ref[0, 0] += jnp.sum(ddtraw, axis=1, keepdims=True)
        dalog_ref[0, 0] += jnp.sum(da * dt, axis=1, keepdims=True) * A

        @pl.when(s == nc - 1)
        def _():
            dh0_ref[0, 0] = dh_scr[...]

    cidx = lambda b, g, s: (b, chunk_of(s), g)
    hidx = lambda b, g, s: (b, g, 0, 0)
    in_specs = [
        pl.BlockSpec((1, HPG, CHUNK), lambda b, g, s: (b, rowblk + g, chunk_of(s))),
        pl.BlockSpec((1, HPG, 1), lambda b, g, s: (rowblk + g, 0, 0)),
        pl.BlockSpec((1, HPG, 1), lambda b, g, s: (rowblk + g, 0, 0)),
        pl.BlockSpec((1, CHUNK, GW), cidx),
        pl.BlockSpec((1, CHUNK, D_STATE), lambda b, g, s: (b, chunk_of(s), D_INNER // D_STATE + g)),
        pl.BlockSpec((1, 1, 1, D_STATE, GW), lambda b, g, s: (b, g, chunk_of(s), 0, 0)),
        pl.BlockSpec((1, 1, D_STATE, GW), hidx),
        pl.BlockSpec(expand.shape, lambda b, g, s: (0, 0)),
        pl.BlockSpec(seg.shape, lambda b, g, s: (0, 0)),
    ]
    args = [dtT, bias, alog, xbc, xbc, h_start, dh_final, expand, seg]
    if has_y:
        in_specs += [pl.BlockSpec((1, CHUNK, D_STATE), lambda b, g, s: (b, chunk_of(s), D_INNER // D_STATE + N_BC + g)),
                     pl.BlockSpec((1, CHUNK, GW), cidx)]
        args += [xbc, dy]
    out_shape = [jax.ShapeDtypeStruct((Bn, L, D_INNER), F32), jax.ShapeDtypeStruct((Bn, L, N_BC * D_STATE), F32),
                 jax.ShapeDtypeStruct((Bn, N_HEADS, L), F32), jax.ShapeDtypeStruct((Bn, N_BC, HPG, 1), F32),
                 jax.ShapeDtypeStruct((Bn, N_BC, HPG, 1), F32), jax.ShapeDtypeStruct((Bn, N_BC, D_STATE, GW), F32)]
    out_specs = [pl.BlockSpec((1, CHUNK, GW), cidx), pl.BlockSpec((1, CHUNK, D_STATE), cidx),
                 pl.BlockSpec((1, HPG, CHUNK), lambda b, g, s: (b, g, chunk_of(s))),
                 pl.BlockSpec((1, 1, HPG, 1), hidx), pl.BlockSpec((1, 1, HPG, 1), hidx), pl.BlockSpec((1, 1, D_STATE, GW), hidx)]
    if has_y:
        out_shape.append(jax.ShapeDtypeStruct((Bn, L, N_BC * D_STATE), F32))
        out_specs.append(pl.BlockSpec((1, CHUNK, D_STATE), cidx))
    res = pl.pallas_call(
        kern, name=name, out_shape=out_shape, grid=(Bn, N_BC, nc), in_specs=in_specs, out_specs=out_specs,
        scratch_shapes=[pltpu.VMEM((D_STATE, GW), F32)],
        compiler_params=_cp(("arbitrary", "arbitrary", "arbitrary")))(*args)
    dxs, db, ddt, dbias, dalog, dh0 = res[:6]
    return dxs, db, (res[6] if has_y else None), ddt, dbias, dalog, dh0


def _group_mean(v):
    gw = D_INNER // N_BC
    parts = [jnp.broadcast_to(jnp.mean(v[:, g * gw:(g + 1) * gw], axis=-1, keepdims=True), (v.shape[0], gw)) for g in range(N_BC)]
    return jnp.concatenate(parts, axis=1)


def gated_norm_fwd(name, y, z, w_norm):
    def body(y, z, w):
        u = y * _silu(z)
        r = lax.rsqrt(_group_mean(u * u) + NORM_EPS)
        return u * r * w

    return tok_call(name, body, [y, z], [], [w_norm], [(D_INNER, BF16)], [], [])[0]


def _dot_exact01(v, sel):
    hi, mid, lo = _split3(v)
    return _dot(hi, sel) + _dot(mid, sel) + _dot(lo, sel)


def gated_norm_bwd(name, y, xs_src, z, d_out, w_norm, head_sel):
    def body(y, xs, z, do, w, sel):
        sz = _silu(z)
        u = y * sz
        r = lax.rsqrt(_group_mean(u * u) + NORM_EPS)
        duh = do * w
        du = r * (duh - u * (r * r) * _group_mean(duh * u))
        dy = du * sz
        dz = du * y * _dsilu(z)
        dsk_heads = _dot_exact01(jnp.broadcast_to(_colsum(dy * xs), (8, D_INNER)), sel)
        return dy, dz, _colsum(do * u * r), dsk_heads

    return tok_call(name, body, [y, xs_src, z, d_out], [], [w_norm, head_sel],
                    [(D_INNER, F32), (D_INNER, BF16)], [], [(1, D_INNER), (8, LANES)], tm=128)


def merge_fwd(name, y_pool, y_ssd, gatepre, x, target, gate, b_merge, norm_post, w_pp, w_ps, w_out):
    def body(yp, ys, gp, x, tgt, gate, bm, wpost, w_pp, w_ps, w_out):
        p1 = _dot(yp, w_pp)
        p2 = _dot(ys, w_ps)
        gates = _sigmoid(gp + bm)
        merged = gates[:, :D] * p1 + gates[:, D:] * p2
        out = _dot(merged.astype(BF16), w_out)
        r = _rms_r(out)
        outr = out * r
        nq = outr * wpost
        err = x + gate * nq - tgt
        loss = 0.5 * jnp.sum(jnp.mean(err * err, axis=-1, keepdims=True), keepdims=True).reshape(1, 1)
        g = err * (1.0 / D)
        dnq = g * gate
        dout = _rms_bwd(dnq * wpost, out, r)
        return merged, p1, p2, dout, g, _colsum(g * nq), _colsum(dnq * outr), jnp.broadcast_to(loss, (1, LANES))

    return tok_call(name, body, [y_pool, y_ssd, gatepre, x, target], [gate], [b_merge, norm_post, w_pp, w_ps, w_out],
                    [(D, BF16), (D, F32), (D, F32), (D, BF16), (D, F32)], [D], [(1, D), (1, LANES)])


def merge_bwd(name, dout, gatepre, p1, p2, b_merge, w_pp, w_ps, w_out):
    def body(dout, gp, p1, p2, bm, w_pp, w_ps, w_out):
        dmerged = _dot_nt(dout, w_out)
        gates = _sigmoid(gp + bm)
        g1, g2 = gates[:, :D], gates[:, D:]
        dp1 = (dmerged * g1).astype(BF16)
        dp2 = (dmerged * g2).astype(BF16)
        dgp = jnp.concatenate([dmerged * p1 * g1 * (1.0 - g1), dmerged * p2 * g2 * (1.0 - g2)], axis=1)
        return dp1, dp2, dgp, _dot_nt(dp1, w_pp), _dot_nt(dp2, w_ps), _colsum(dgp)

    return tok_call(name, body, [dout, gatepre, p1, p2], [], [b_merge, w_pp, w_ps, w_out],
                    [(D, BF16), (D, BF16), (2 * D, BF16), (D, F32), (D_INNER, F32)], [], [(1, 2 * D)])


def _adamw_math(w, g, m, v):
    m = ADAM_B1 * m + (1.0 - ADAM_B1) * g
    v = ADAM_B2 * v + (1.0 - ADAM_B2) * (g * g)
    m_hat = m / (1.0 - ADAM_B1 ** ADAM_STEP)
    v_hat = v / (1.0 - ADAM_B2 ** ADAM_STEP)
    delta = -ADAM_LR * (m_hat / (jnp.sqrt(v_hat) + ADAM_EPS) + ADAM_WD * w)
    return delta, m, v


def adamw(name, w, g, m, v, tr=256):
    R, C = w.shape
    tr = min(tr, R)
    assert R % tr == 0

    def body(w_ref, g_ref, m_ref, v_ref, d_ref, nm_ref, nv_ref):
        d, nm, nv = _adamw_math(w_ref[...], g_ref[...], m_ref[...], v_ref[...])
        d_ref[...] = d
        nm_ref[...] = nm
        nv_ref[...] = nv

    spec = pl.BlockSpec((tr, C), lambda i: (i, 0))
    return pl.pallas_call(
        body, name=name, out_shape=[jax.ShapeDtypeStruct((R, C), F32)] * 3, grid=(R // tr,),
        in_specs=[spec] * 4, out_specs=[spec] * 3, compiler_params=_cp(("parallel",)))(w, g, m, v)


def _me():
    return lax.axis_index("x"), lax.axis_index("y"), lax.axis_index("c")


def all_gather_small(name, v):
    R, C = v.shape

    def body(v_ref, out_ref, send_sems, recv_sems, local_sem):
        x, y, c = _me()
        me = 4 * x + 2 * y + c
        mine = pltpu.make_async_copy(v_ref, out_ref.at[me], local_sem)
        mine.start()
        copies = []
        for d in range(1, N_DEV):
            dx, dy, dc = d // 4, (d // 2) % 2, d % 2
            px, py, pc = x ^ dx, y ^ dy, c ^ dc
            copies.append(pltpu.make_async_remote_copy(
                src_ref=v_ref, dst_ref=out_ref.at[me], send_sem=send_sems.at[d - 1], recv_sem=recv_sems.at[d - 1],
                device_id=(px, py, pc), device_id_type=MESH))
        for cp in copies:
            cp.start()
        for d in range(1, N_DEV):
            dx, dy, dc = d // 4, (d // 2) % 2, d % 2
            peer = 4 * (x ^ dx) + 2 * (y ^ dy) + (c ^ dc)
            pltpu.make_async_remote_copy(
                src_ref=v_ref, dst_ref=out_ref.at[peer], send_sem=send_sems.at[d - 1], recv_sem=recv_sems.at[d - 1],
                device_id=(x ^ dx, y ^ dy, c ^ dc), device_id_type=MESH).wait_recv()
        for cp in copies:
            cp.wait_send()
        mine.wait()

    return pl.pallas_call(
        body, name=name, out_shape=jax.ShapeDtypeStruct((N_DEV, R, C), F32),
        in_specs=[pl.BlockSpec(memory_space=pltpu.VMEM)], out_specs=pl.BlockSpec(memory_space=pltpu.VMEM),
        scratch_shapes=[pltpu.SemaphoreType.DMA((N_DEV - 1,)), pltpu.SemaphoreType.DMA((N_DEV - 1,)), pltpu.SemaphoreType.DMA],
        compiler_params=pltpu.CompilerParams(vmem_limit_bytes=VMEM_LIMIT))(v)


def all_gather_chips(name, shard):
    R, C = shard.shape
    half = R // 2
    assert R % 32 == 0

    def body(s_ref, out_ref, send_sems, recv_sems):
        x, y, c = _me()
        chips = [(1 - x, y), (x, 1 - y), (1 - x, 1 - y)]

        def rows(chip, hc):
            return out_ref.at[2 * chip[0] + chip[1], pl.ds(hc * half, half), :]

        first = [pltpu.make_async_remote_copy(
            src_ref=s_ref.at[pl.ds(c * half, half), :], dst_ref=rows((x, y), c), send_sem=send_sems.at[j],
            recv_sem=recv_sems.at[j], device_id=(*chip, c), device_id_type=MESH) for j, chip in enumerate(chips)]
        for cp in first:
            cp.start()
        passed = [pltpu.make_async_remote_copy(
            src_ref=rows(chip, c), dst_ref=rows(chip, c), send_sem=send_sems.at[3 + j], recv_sem=recv_sems.at[3 + j],
            device_id=(x, y, 1 - c), device_id_type=MESH) for j, chip in enumerate(chips)]
        for j, chip in enumerate(chips):
            pltpu.make_async_remote_copy(
                src_ref=rows(chip, c), dst_ref=rows(chip, c), send_sem=send_sems.at[j], recv_sem=recv_sems.at[j],
                device_id=(*chip, c), device_id_type=MESH).wait_recv()
            passed[j].start()
        for j, chip in enumerate(chips):
            pltpu.make_async_remote_copy(
                src_ref=rows(chip, 1 - c), dst_ref=rows(chip, 1 - c), send_sem=send_sems.at[3 + j], recv_sem=recv_sems.at[3 + j],
                device_id=(x, y, 1 - c), device_id_type=MESH).wait_recv()
        for cp in first + passed:
            cp.wait_send()

    out = pl.pallas_call(
        body, name=name, out_shape=jax.ShapeDtypeStruct((N_CHIPS, R, C), shard.dtype),
        in_specs=[pl.BlockSpec(memory_space=pl.ANY)], out_specs=pl.BlockSpec(memory_space=pl.ANY),
        scratch_shapes=[pltpu.SemaphoreType.DMA((6,)), pltpu.SemaphoreType.DMA((6,))],
        compiler_params=pltpu.CompilerParams(vmem_limit_bytes=VMEM_LIMIT))(shard)
    chip = 2 * lax.axis_index("x") + lax.axis_index("y")
    return lax.dynamic_update_index_in_dim(out, shard, chip, 0)


def sibling_swap(name, v):
    def body(v_ref, out_ref, send_sem, recv_sem):
        x, y, c = _me()
        cp = pltpu.make_async_remote_copy(src_ref=v_ref, dst_ref=out_ref, send_sem=send_sem, recv_sem=recv_sem,
                                          device_id=(x, y, 1 - c), device_id_type=MESH)
        cp.start()
        cp.wait()

    return pl.pallas_call(
        body, name=name, out_shape=jax.ShapeDtypeStruct(v.shape, v.dtype),
        in_specs=[pl.BlockSpec(memory_space=pl.ANY)], out_specs=pl.BlockSpec(memory_space=pl.ANY),
        scratch_shapes=[pltpu.SemaphoreType.DMA, pltpu.SemaphoreType.DMA],
        compiler_params=pltpu.CompilerParams(vmem_limit_bytes=VMEM_LIMIT))(v)


def sibling_share(name, v):
    def body(v_ref, out_ref, send_sem, recv_sem, local_sem):
        x, y, c = _me()
        mine = pltpu.make_async_copy(v_ref, out_ref.at[c], local_sem)
        mine.start()
        cp = pltpu.make_async_remote_copy(src_ref=v_ref, dst_ref=out_ref.at[c], send_sem=send_sem, recv_sem=recv_sem,
                                          device_id=(x, y, 1 - c), device_id_type=MESH)
        cp.start()
        pltpu.make_async_remote_copy(src_ref=v_ref, dst_ref=out_ref.at[1 - c], send_sem=send_sem, recv_sem=recv_sem,
                                     device_id=(x, y, 1 - c), device_id_type=MESH).wait_recv()
        cp.wait_send()
        mine.wait()

    return pl.pallas_call(
        body, name=name, out_shape=jax.ShapeDtypeStruct((2, *v.shape), v.dtype),
        in_specs=[pl.BlockSpec(memory_space=pl.ANY)], out_specs=pl.BlockSpec(memory_space=pl.ANY),
        scratch_shapes=[pltpu.SemaphoreType.DMA, pltpu.SemaphoreType.DMA, pltpu.SemaphoreType.DMA],
        compiler_params=pltpu.CompilerParams(vmem_limit_bytes=VMEM_LIMIT))(v)


def chip_exchange(name, parts):
    def body(p_ref, out_ref, send_sems, recv_sems):
        x, y, c = _me()
        k = 2 * x + y
        chips = [(1 - x, y), (x, 1 - y), (1 - x, 1 - y)]
        sends = [pltpu.make_async_remote_copy(
            src_ref=p_ref.at[2 * chip[0] + chip[1]], dst_ref=out_ref.at[k], send_sem=send_sems.at[j], recv_sem=recv_sems.at[j],
            device_id=(*chip, c), device_id_type=MESH) for j, chip in enumerate(chips)]
        for cp in sends:
            cp.start()
        for j, chip in enumerate(chips):
            pltpu.make_async_remote_copy(
                src_ref=p_ref.at[k], dst_ref=out_ref.at[2 * chip[0] + chip[1]], send_sem=send_sems.at[j], recv_sem=recv_sems.at[j],
                device_id=(*chip, c), device_id_type=MESH).wait_recv()
        for cp in sends:
            cp.wait_send()

    out = pl.pallas_call(
        body, name=name, out_shape=jax.ShapeDtypeStruct(parts.shape, parts.dtype),
        in_specs=[pl.BlockSpec(memory_space=pl.ANY)], out_specs=pl.BlockSpec(memory_space=pl.ANY),
        scratch_shapes=[pltpu.SemaphoreType.DMA((3,)), pltpu.SemaphoreType.DMA((3,))],
        compiler_params=pltpu.CompilerParams(vmem_limit_bytes=VMEM_LIMIT))(parts)
    chip = 2 * lax.axis_index("x") + lax.axis_index("y")
    own = lax.dynamic_index_in_dim(parts, chip, 0, keepdims=True)
    return lax.dynamic_update_slice_in_dim(out, own, chip, 0)


def _row_tile(rows, cap, mult=8):
    best = None
    for t in range(mult, min(rows, cap) + 1, mult):
        if rows % t == 0:
            best = t
    assert best is not None, rows
    return best


def add_arrays(name, arrs, out_dtype=F32):
    shape = arrs[0].shape
    C = shape[-1]
    flat = [a.reshape(-1, C) for a in arrs]
    R = flat[0].shape[0]
    narrow = out_dtype == BF16 or any(a.dtype == BF16 for a in arrs)
    tr = _row_tile(R, 2048 if len(arrs) <= 2 else 1024, 16 if narrow else 8)
    n = len(flat)

    def body(*refs):
        acc = refs[0][...].astype(F32)
        for r in refs[1:n]:
            acc = acc + r[...].astype(F32)
        refs[n][...] = acc.astype(out_dtype)

    spec = pl.BlockSpec((tr, C), lambda i: (i, 0))
    out = pl.pallas_call(
        body, name=name, out_shape=jax.ShapeDtypeStruct((R, C), out_dtype), grid=(R // tr,),
        in_specs=[spec] * n, out_specs=spec, compiler_params=_cp(("parallel",)))(*flat)
    return out.reshape(shape)


def reduce_scatter_chips(slabs):
    _, R, C = slabs.shape
    half = R // 2
    c = lax.axis_index("c")
    k = 2 * lax.axis_index("x") + lax.axis_index("y")
    halves = slabs.reshape(N_CHIPS, 2, half, C)
    own = lax.dynamic_index_in_dim(halves, c, axis=1, keepdims=False)
    other = lax.dynamic_index_in_dim(halves, 1 - c, axis=1, keepdims=False)
    from_sibling = sibling_swap("rs_sibling_halves", other.astype(BF16))
    del k
    return add_arrays("rs_add_sibling", [own, from_sibling], out_dtype=BF16)


def reduce_scatter_finish(landed):
    c = lax.axis_index("c")
    mine = add_arrays("rs_add_chips", [landed[j] for j in range(N_CHIPS)])
    sib = sibling_swap("rs_sibling_result", mine)
    return jnp.concatenate([jnp.where(c == 0, mine, sib), jnp.where(c == 0, sib, mine)], axis=0)


def ada_mod_shard(cond_all, w_ada_shard, b_ada_shard):
    def body(c_ref, w_ref, b_ref, o_ref):
        o_ref[...] = _dot(_silu(c_ref[...]).astype(BF16), w_ref[...].astype(BF16)) + b_ref[...]

    return pl.pallas_call(body, name="ada_mod_shard", out_shape=jax.ShapeDtypeStruct((cond_all.shape[0], w_ada_shard.shape[1]), F32),
                          compiler_params=_cp())(cond_all, w_ada_shard, b_ada_shard)


def ada_bwd_shard(cond_all, dmod_all_shard, dmod_all, w_ada_shard, row_is_cctx):
    def body(c_ref, ds_ref, da_ref, w_ref, sel_ref, gw_ref, gb_ref, part_ref):
        sc = _silu(c_ref[...]).astype(BF16)
        gw_ref[...] = _dot_tn(sc, ds_ref[...].astype(BF16))
        gb_ref[...] = _colsum(da_ref[...])
        dc_tot = jnp.broadcast_to(_colsum(ds_ref[...] * sel_ref[...]), (8, ds_ref.shape[1]))
        part_ref[...] = _dot_nt(dc_tot.astype(BF16), w_ref[...].astype(BF16))

    n = cond_all.shape[0]
    return pl.pallas_call(
        body, name="ada_bwd_shard",
        out_shape=[jax.ShapeDtypeStruct(w_ada_shard.shape, F32), jax.ShapeDtypeStruct((1, dmod_all.shape[1]), F32),
                   jax.ShapeDtypeStruct((8, D), F32)],
        compiler_params=_cp())(cond_all, dmod_all_shard, dmod_all, w_ada_shard, row_is_cctx)


def sum_devices(name, gathered):
    def body(g_ref, o_ref):
        acc = g_ref[0]
        for d in range(1, N_DEV):
            acc = acc + g_ref[d]
        o_ref[...] = acc

    return pl.pallas_call(body, name=name, out_shape=jax.ShapeDtypeStruct(gathered.shape[1:], F32), compiler_params=_cp())(gathered)


def cctx_finish(gathered, c_ctx_row):
    def body(g_ref, c_ref, o_ref):
        acc = g_ref[0, 0:1, :]
        for k in range(1, N_CHIPS):
            acc = acc + g_ref[2 * k, 0:1, :]
        o_ref[...] = acc * _dsilu(c_ref[...])

    return pl.pallas_call(body, name="cctx_finish", out_shape=jax.ShapeDtypeStruct((1, D), F32), compiler_params=_cp())(gathered, c_ctx_row)


def _pack(parts, rows):
    flat = []
    for p in parts:
        p = p.reshape(-1)
        pad = (-p.shape[0]) % LANES
        flat.append(jnp.pad(p, (0, pad)) if pad else p)
    v = jnp.concatenate(flat)
    return jnp.pad(v, (0, rows * LANES - v.shape[0])).reshape(rows, LANES)


def _unpack(v, sizes):
    flat = v.reshape(-1)
    out, off = [], 0
    for n in sizes:
        out.append(flat[off:off + n])
        off += n + (-n) % LANES
    return out


W_SHARD_ROWS = 3456
SEG_ROWS = (0, 2320, 2576, 3088, 3344, 3408)


def kernel(x, c, ctx, c_ctx, w_ada, b_ada, norm_pre, norm_post, w_in, b_merge, pool_w, pool_scale, conv_w, conv_b, dt_bias, a_log, d_skip, ssd_norm, w_proj_pool, w_proj_ssd, w_out, loss_target, m_c_ctx, m_w_ada, m_b_ada, m_norm_pre, m_norm_post, m_w_in, m_b_merge, m_pool_w, m_pool_scale, m_conv_w, m_conv_b, m_dt_bias, m_a_log, m_d_skip, m_ssd_norm, m_w_proj_pool, m_w_proj_ssd, m_w_out, v_c_ctx, v_w_ada, v_b_ada, v_norm_pre, v_norm_post, v_w_in, v_b_merge, v_pool_w, v_pool_scale, v_conv_w, v_conv_b, v_dt_bias, v_a_log, v_d_skip, v_ssd_norm, v_w_proj_pool, v_w_proj_ssd, v_w_out):
    Bn, L, _ = x.shape
    Lc = ctx.shape[1]
    T, Tc = Bn * L, Bn * Lc
    assert Bn == 2
    ix, iy, ic = lax.axis_index("x"), lax.axis_index("y"), lax.axis_index("c")
    me = 4 * ix + 2 * iy + ic
    chip = 2 * ix + iy
    ada_cols = w_ada.shape[2]
    cw_cols = conv_w.shape[2]

    cond_own = jnp.pad(c, ((0, 8 - Bn), (0, 0))) + jnp.pad(c_ctx[None, :], ((Bn, 7 - Bn), (0, 0)))
    convw_own = jnp.pad(conv_w[0], ((0, 4), (0, D - cw_cols)))
    g1 = all_gather_small("gather_cond", jnp.concatenate([cond_own, convw_own], axis=0))
    cond_all = g1[:, 0:8].reshape(8 * N_DEV, D)
    conv_w_full = jnp.concatenate([g1[2 * k, 8:12, 0:cw_cols] for k in range(N_CHIPS)], axis=1)
    b_ada_shard = lax.dynamic_slice(b_ada, (0, chip * ada_cols), (1, ada_cols))
    g2 = all_gather_small("gather_mod", ada_mod_shard(cond_all, w_ada[0], b_ada_shard))
    mod_full = jnp.concatenate([g2[2 * k] for k in range(N_CHIPS)], axis=1)
    own = lax.dynamic_slice(mod_full, (8 * me, 0), (8, 3 * D))
    shift, scale, gate = (own[0:Bn, i * D:(i + 1) * D][:, None, :] for i in range(3))
    shift_c, scale_c = (jnp.broadcast_to(own[Bn:Bn + 1, i * D:(i + 1) * D][None], (Bn, 1, D)) for i in range(2))

    w_in_rows = IN_COLS // N_CHIPS
    shard_in = jnp.concatenate([w_in[0].T, jnp.zeros((16, D), F32)], axis=0).astype(BF16)
    shard_rest = jnp.concatenate([w_proj_pool[0], w_proj_ssd[0], w_out[0], pool_w[0].reshape(64, D)], axis=0).astype(BF16)
    w_inT = all_gather_chips("gather_w_in", shard_in)[:, 0:w_in_rows].reshape(IN_COLS, D)
    w_dt = jnp.pad(w_inT[9216:IN_COLS], ((0, LANES - 64), (0, 0)))
    seg_lo = (0, 1024, 2048, 4096, 6144, 8192, 8704)
    seg_hi = (1024, 2048, 4096, 6144, 8192, 8704, 9216)
    w_seg = [w_inT[lo:hi] for lo, hi in zip(seg_lo, seg_hi)] + [w_dt]

    hx = prenorm_fwd("prenorm_x", x, scale, shift, norm_pre)
    hc = prenorm_fwd("prenorm_ctx", ctx, scale_c, shift_c, norm_pre)
    hx2, hc2 = hx.reshape(T, D), hc.reshape(Tc, D)
    v = mm_nt("proj_v", hx2, w_inT[0:1024], F32).reshape(Bn, L, D)
    zp = mm_nt("proj_zpool", hx2, w_inT[1024:2048], F32).reshape(Bn, L, D)
    zs = mm_nt("proj_zssd", hx2, w_inT[2048:4096], F32).reshape(Bn, L, D_INNER)
    gp = mm_nt("proj_gate", hx2, w_inT[4096:6144], F32).reshape(Bn, L, 2 * D)
    xbc_raw, g_rest = mm_nt("proj_xbc", hx2, w_inT[6144:9216], F32, gather=shard_rest)
    xbc_raw = xbc_raw.reshape(Bn, L, CONV_DIM)
    w_pp = g_rest[:, 0:256].reshape(D, D)
    w_ps = g_rest[:, 256:768].reshape(D_INNER, D)
    w_o = g_rest[:, 768:1024].reshape(D, D)
    pool_full = g_rest[:, 1024:1088].reshape(N_CHIPS, 4, 64, POOL_GROUP).transpose(1, 0, 2, 3).reshape(D, POOL_GROUP)
    dt_raw = mm_nt("proj_dt", hx2, w_dt, F32)
    xbc_raw_c = mm_nt("proj_xbc_ctx", hc2, w_inT[6144:9216], F32).reshape(Bn, Lc, CONV_DIM)
    dt_raw_c = mm_nt("proj_dt_ctx", hc2, w_dt, F32)
    dtT = dt_raw[:, :64].reshape(Bn, L, 64).transpose(0, 2, 1)
    dtT_c = dt_raw_c[:, :64].reshape(Bn, Lc, 64).transpose(0, 2, 1)
    bias3 = dt_bias.reshape(2 * N_BC, HPG, 1)
    alog3 = a_log.reshape(2 * N_BC, HPG, 1)

    xbc = conv_fwd("conv_x", xbc_raw, conv_w_full, conv_b)
    xbc_c = conv_fwd("conv_ctx", xbc_raw_c, conv_w_full, conv_b)
    zero_state = jnp.zeros((Bn, N_BC, D_STATE, GW), F32)
    tables = ssd_tables()
    ys, hs_x, hs_c, hf_x, hf_c = [], [], [], [], []
    for d in range(2):
        hsc, hfc = ssd_fwd3(f"ssd_fwd_ctx{d}", dtT_c, bias3, alog3, xbc_c, zero_state, d, False)
        y, hsx, hfx = ssd_fwd3(f"ssd_fwd_x{d}", dtT, bias3, alog3, xbc, hfc, d, True,
                               y_add=(ys[0], jnp.repeat(d_skip[0], HEAD_DIM)[None, :]) if d == 1 else None)
        ys.append(y)
        hs_x.append(hsx)
        hs_c.append(hsc)
        hf_x.append(hfx)
        hf_c.append(hfc)

    dgs = [pool_diff(f"pool_diff{g}", v, g * POOL_GROUP, g, False) for g in range(4)]
    y_pool = pool_mix_fwd("pool_mix", dgs, zp, pool_full, pool_scale)
    dskip_lanes = jnp.repeat(d_skip[0], HEAD_DIM)[None, :]
    y_ssd = gated_norm_fwd("gated_norm", ys[1], zs, ssd_norm)
    merged, p1, p2, dout, g_res, dgate, g_norm_post, loss_part = merge_fwd(
        "merge_fwd", y_pool, y_ssd, gp, x, loss_target, gate, b_merge, norm_post, w_pp, w_ps, w_o)

    dp1, dp2, dgp, dyp, dys, g_b_merge = merge_bwd("merge_bwd", dout, gp, p1, p2, b_merge, w_pp, w_ps, w_o)
    gw_o = mm_tn("gw_out", merged.reshape(T, D), dout.reshape(T, D))
    gw_pp = mm_tn("gw_proj_pool", y_pool.reshape(T, D), dp1.reshape(T, D))
    gw_ps = mm_tn("gw_proj_ssd", y_ssd.reshape(T, D_INNER), dp2.reshape(T, D))

    *dds, dzp, g_pool, g_pool_scale = pool_mix_bwd("pool_mix_bwd", dgs, zp, dyp, pool_full, pool_scale)
    dvs = [pool_diff(f"pool_diff_t{g}", dds[g], 0, g, True) for g in range(4)]

    head_sel = (jnp.arange(D_INNER)[:, None] // HEAD_DIM == jnp.arange(LANES)[None, :]).astype(BF16)
    dy, dzs, g_ssd_norm, g_dskip = gated_norm_bwd(
        "gated_norm_bwd", ys[1], (xbc, D_INNER), zs, dys, ssd_norm, head_sel)

    dxs, dbm, dcm, ddt, dxs_c, dbm_c, ddt_c = [], [], [], [], [], [], []
    g_bias = jnp.zeros((2, N_BC, HPG, 1), F32)
    g_alog = jnp.zeros((2, N_BC, HPG, 1), F32)
    for d in range(2):
        a, b_, c_, t_, gb, ga, dh0 = ssd_bwd3(f"ssd_bwd_x{d}", dtT, bias3, alog3, xbc, hs_x[d], dy, zero_state, d,
                                              dx_add=(dxs[0], dskip_lanes) if d == 1 else None)
        dxs.append(a), dbm.append(b_), dcm.append(c_), ddt.append(t_)
        ac, bc, _, tc, gbc, gac, _ = ssd_bwd3(f"ssd_bwd_ctx{d}", dtT_c, bias3, alog3, xbc_c, hs_c[d], None, dh0, d)
        dxs_c.append(ac), dbm_c.append(bc), ddt_c.append(tc)
        g_bias = g_bias.at[d].set(jnp.sum(gb, axis=0) + jnp.sum(gbc, axis=0))
        g_alog = g_alog.at[d].set(jnp.sum(ga, axis=0) + jnp.sum(gac, axis=0))

    dxr_xs, gcw_xs, gcb_xs = conv_bwd_stream("conv_bwd_xs", xbc_raw, [dxs[1]], conv_w_full, conv_b, 0, D_INNER)
    dxr_b, gcw_b, gcb_b = conv_bwd_stream("conv_bwd_b", xbc_raw, dbm, conv_w_full, conv_b, D_INNER, N_BC * D_STATE)
    dxr_c, gcw_c, gcb_c = conv_bwd_stream("conv_bwd_c", xbc_raw, dcm, conv_w_full, conv_b, D_INNER + N_BC * D_STATE, N_BC * D_STATE)
    dxr_xs_c, gcw_xs_c, gcb_xs_c = conv_bwd_stream("conv_bwd_xs_ctx", xbc_raw_c, dxs_c, conv_w_full, conv_b, 0, D_INNER)
    dxr_b_c, gcw_b_c, gcb_b_c = conv_bwd_stream("conv_bwd_b_ctx", xbc_raw_c, dbm_c, conv_w_full, conv_b, D_INNER, N_BC * D_STATE)
    g_conv_w = jnp.concatenate([gcw_xs + gcw_xs_c, gcw_b + gcw_b_c, gcw_c], axis=1)
    g_conv_b = jnp.concatenate([gcb_xs + gcb_xs_c, gcb_b + gcb_b_c, gcb_c], axis=1)

    def dt_cols(parts, n_tok):
        t = jnp.concatenate(parts, axis=1).transpose(0, 2, 1).reshape(n_tok, 2 * N_HEADS)
        return jnp.pad(t, ((0, 0), (0, LANES - 2 * N_HEADS))).astype(BF16)

    ddt2, ddt2_c = dt_cols(ddt, T), dt_cols(ddt_c, Tc)
    segs = [jnp.concatenate(dvs, axis=-1).reshape(T, D), dzp.reshape(T, D), dzs.reshape(T, D_INNER), dgp.reshape(T, 2 * D),
            dxr_xs.reshape(T, D_INNER), dxr_b.reshape(T, N_BC * D_STATE), dxr_c.reshape(T, N_BC * D_STATE), ddt2]
    segs_c = {4: dxr_xs_c.reshape(Tc, D_INNER), 5: dxr_b_c.reshape(Tc, N_BC * D_STATE), 7: ddt2_c}
    gw_rows = []
    for i, seg in enumerate(segs):
        init = mm_tn(f"gw_in_ctx{i}", segs_c[i], hc2) if i in segs_c else None
        gw_rows.append(mm_tn(f"gw_in{i}", seg, hx2, init=init))
    gw_rows[-1] = gw_rows[-1][0:2 * N_HEADS]
    gw_inT = jnp.concatenate(gw_rows, axis=0)

    pool_slab = g_pool.reshape(4, N_CHIPS, 64, POOL_GROUP).transpose(1, 0, 2, 3).reshape(N_CHIPS, 64, D)
    slabs = jnp.concatenate([gw_inT.reshape(N_CHIPS, 2320, D), gw_pp.reshape(N_CHIPS, 256, D), gw_ps.reshape(N_CHIPS, 512, D),
                             gw_o.reshape(N_CHIPS, 256, D), pool_slab, jnp.zeros((N_CHIPS, W_SHARD_ROWS - SEG_ROWS[-1], D), F32)], axis=1)
    chip_part = reduce_scatter_chips(slabs)
    d_hx, landed = mm_nn_multi("d_hx", list(zip(segs, w_seg)), F32, tm=1024, tk=256, exchange=chip_part)
    d_hx = d_hx.reshape(Bn, L, D)
    gsh = reduce_scatter_finish(landed)
    d_hc = mm_nn_multi("d_hc", [(segs_c[i], w_seg[i]) for i in sorted(segs_c)], F32).reshape(Bn, Lc, D)

    grad_x, dscale, dshift, g_npre_x = prenorm_bwd("prenorm_bwd_x", x, d_hx, scale, norm_pre, g_res=g_res)
    _, dscale_c, dshift_c, g_npre_c = prenorm_bwd("prenorm_bwd_ctx", ctx, d_hc, scale_c, norm_pre)

    dmod_x = jnp.concatenate([dshift[:, 0], dscale[:, 0], dgate[:, 0]], axis=1)
    dmod_c = jnp.concatenate([jnp.sum(dshift_c[:, 0], axis=0, keepdims=True), jnp.sum(dscale_c[:, 0], axis=0, keepdims=True),
                              jnp.zeros((1, D), F32)], axis=1)
    dmod_own = jnp.pad(dmod_x, ((0, 8 - Bn), (0, 0))) + jnp.pad(dmod_c, ((Bn, 7 - Bn), (0, 0)))
    dmod_all = all_gather_small("gather_dmod", dmod_own).reshape(8 * N_DEV, 3 * D)
    row_is_cctx = (jnp.arange(8 * N_DEV) % 8 == Bn).astype(F32)[:, None]
    g_w_ada, g_b_ada, cpart = ada_bwd_shard(
        cond_all, lax.dynamic_slice(dmod_all, (0, chip * ada_cols), (8 * N_DEV, ada_cols)), dmod_all, w_ada[0], row_is_cctx)
    g_c_ctx = cctx_finish(all_gather_small("gather_cctx", cpart), c_ctx[None, :])

    small_sizes = (D, D, 2 * D, D, CONV_DIM, 2 * N_HEADS, 2 * N_HEADS, N_HEADS, D_INNER, 4 * CONV_DIM, 1)
    pk = _pack([g_npre_x + g_npre_c, g_norm_post, g_b_merge, g_pool_scale, g_conv_b, g_bias, g_alog, g_dskip[0, 0:N_HEADS],
                g_ssd_norm, g_conv_w, loss_part[0, 0:1]], 184)
    small = sum_devices("sum_small", all_gather_small("gather_small", pk))
    (g_norm_pre, g_norm_post_t, g_b_merge_t, g_pool_scale_t, g_conv_b_t, g_dt_bias, g_a_log, g_d_skip, g_ssd_norm_t,
     g_conv_w_t, loss) = _unpack(small, small_sizes)
    g_conv_w_shard = lax.dynamic_slice(g_conv_w_t.reshape(4, CONV_DIM), (0, chip * cw_cols), (4, cw_cols))

    g_w_in = gsh[SEG_ROWS[0]:SEG_ROWS[1]].T
    g_w_pp, g_w_ps, g_w_o = (gsh[SEG_ROWS[i]:SEG_ROWS[i + 1]] for i in (1, 2, 3))
    g_pool_w = gsh[SEG_ROWS[4]:SEG_ROWS[5]].reshape(256, POOL_GROUP)

    grads = {
        "c_ctx": g_c_ctx.reshape(c_ctx.shape), "w_ada": g_w_ada[None], "b_ada": g_b_ada, "norm_pre": g_norm_pre[None],
        "norm_post": g_norm_post_t[None], "w_in": g_w_in[None], "b_merge": g_b_merge_t[None],
        "pool_w": g_pool_w.reshape(pool_w.shape), "pool_scale": g_pool_scale_t[None], "conv_w": g_conv_w_shard[None],
        "conv_b": g_conv_b_t[None], "dt_bias": g_dt_bias.reshape(dt_bias.shape), "a_log": g_a_log.reshape(a_log.shape),
        "d_skip": g_d_skip[None], "ssd_norm": g_ssd_norm_t[None], "w_proj_pool": g_w_pp[None], "w_proj_ssd": g_w_ps[None],
        "w_out": g_w_o[None]}
    weights = dict(c_ctx=c_ctx, w_ada=w_ada, b_ada=b_ada, norm_pre=norm_pre, norm_post=norm_post, w_in=w_in, b_merge=b_merge,
                   pool_w=pool_w, pool_scale=pool_scale, conv_w=conv_w, conv_b=conv_b, dt_bias=dt_bias, a_log=a_log,
                   d_skip=d_skip, ssd_norm=ssd_norm, w_proj_pool=w_proj_pool, w_proj_ssd=w_proj_ssd, w_out=w_out)
    m_in = dict(c_ctx=m_c_ctx, w_ada=m_w_ada, b_ada=m_b_ada, norm_pre=m_norm_pre, norm_post=m_norm_post, w_in=m_w_in,
                b_merge=m_b_merge, pool_w=m_pool_w, pool_scale=m_pool_scale, conv_w=m_conv_w, conv_b=m_conv_b,
                dt_bias=m_dt_bias, a_log=m_a_log, d_skip=m_d_skip, ssd_norm=m_ssd_norm, w_proj_pool=m_w_proj_pool,
                w_proj_ssd=m_w_proj_ssd, w_out=m_w_out)
    v_in = dict(c_ctx=v_c_ctx, w_ada=v_w_ada, b_ada=v_b_ada, norm_pre=v_norm_pre, norm_post=v_norm_post, w_in=v_w_in,
                b_merge=v_b_merge, pool_w=v_pool_w, pool_scale=v_pool_scale, conv_w=v_conv_w, conv_b=v_conv_b,
                dt_bias=v_dt_bias, a_log=v_a_log, d_skip=v_d_skip, ssd_norm=v_ssd_norm, w_proj_pool=v_w_proj_pool,
                w_proj_ssd=v_w_proj_ssd, w_out=v_w_out)
    names = list(weights)
    big = ("w_ada", "w_in", "pool_w", "w_proj_pool", "w_proj_ssd", "w_out")
    small_names = [n for n in names if n not in big]
    delta, new_m, new_v = {}, {}, {}
    for n in big:
        shape2 = (-1, weights[n].shape[-1])
        d_, m_, v_ = adamw(f"adamw_{n}", weights[n].reshape(shape2), grads[n].reshape(shape2), m_in[n].reshape(shape2),
                           v_in[n].reshape(shape2), tr=128)
        delta[n], new_m[n], new_v[n] = (t.reshape(weights[n].shape) for t in (d_, m_, v_))
    sizes = [weights[n].size for n in small_names]
    packed = [_pack([src[n] for n in small_names], 144) for src in (weights, grads, m_in, v_in)]
    outs = adamw("adamw_small", *packed, tr=144)
    for res, store in zip(outs, (delta, new_m, new_v)):
        for n, piece in zip(small_names, _unpack(res, sizes)):
            store[n] = piece.reshape(weights[n].shape)

    return (loss.reshape(()), grad_x, *[grads[n] for n in names], *[delta[n] for n in names],
            *[new_m[n] for n in names], *[new_v[n] for n in names])
```

```python
import jax
import jax.numpy as jnp
from jax import lax
from jax.experimental import pallas as pl
from jax.experimental.pallas import tpu as pltpu

F32 = jnp.float32
BF16 = jnp.bfloat16
MESH = pl.DeviceIdType.MESH

D = 1024
GRID_W = 64
NORM_EPS = 1e-6
POOL_WINDOWS = (2, 4, 8, 16)
POOL_GROUP = 256
D_INNER = 2048
HEAD_DIM = 64
N_HEADS = 32
D_STATE = 128
N_BC = 4
HPG = N_HEADS // N_BC
GW = HPG * HEAD_DIM
CONV_DIM = 3072
CHUNK = 128
IN_COLS = 9280
N_CHIPS = 4
N_DEV = 8

ADAM_LR = 0.001
ADAM_B1 = 0.9
ADAM_B2 = 0.999
ADAM_EPS = 1e-08
ADAM_WD = 0.01
ADAM_STEP = 10

V7X_VMEM_BYTES = 64 * 1024 * 1024
VMEM_LIMIT = V7X_VMEM_BYTES * 3 // 4
LANES = 128


def _cp(sem=None):
    return pltpu.CompilerParams(dimension_semantics=sem, vmem_limit_bytes=VMEM_LIMIT)


def _dot(a, b):
    return jnp.dot(a, b, preferred_element_type=F32)


def _dot_nt(a, b):
    return lax.dot_general(a, b, (((1,), (1,)), ((), ())), preferred_element_type=F32)


def _dot_tn(a, b):
    return lax.dot_general(a, b, (((0,), (0,)), ((), ())), preferred_element_type=F32)


def _split3(x):
    hi = x.astype(BF16)
    r1 = x - hi.astype(F32)
    mid = r1.astype(BF16)
    lo = (r1 - mid.astype(F32)).astype(BF16)
    return hi, mid, lo


def _dot_exact01(v, sel):
    hi, mid, lo = _split3(v)
    return _dot(hi, sel) + _dot(mid, sel) + _dot(lo, sel)


def _sigmoid(x):
    return jax.nn.sigmoid(x)


def _silu(x):
    return x * _sigmoid(x)


def _dsilu(x):
    s = _sigmoid(x)
    return s * (1.0 + x * (1.0 - s))


def _softplus(x):
    return jnp.maximum(x, 0.0) + jnp.log(1.0 + jnp.exp(-jnp.abs(x)))


def _me():
    return lax.axis_index("x"), lax.axis_index("y"), lax.axis_index("c")


def mm_nt(name, a, b, out_dtype, tm=1024, tn=512, gather=None):
    M, K = a.shape
    N = b.shape[0]
    tm, tn = min(tm, M), min(tn, N)
    assert M % tm == 0 and N % tn == 0, (M, N, tm, tn)
    n_i, n_j = M // tm, N // tn
    has_g = gather is not None
    if has_g:
        half = gather.shape[0] // 2
        assert gather.shape[0] % 32 == 0 and n_i * n_j >= 4

    def body(*refs):
        a_ref, b_ref = refs[0], refs[1]
        if has_g:
            s_ref, o_ref, g_ref, send_sems, recv_sems = refs[2:]
            x, y, c = _me()
            chips = [(1 - x, y), (x, 1 - y), (1 - x, 1 - y)]
            step = pl.program_id(0) * n_j + pl.program_id(1)

            def rows(chip, hc):
                return g_ref.at[2 * chip[0] + chip[1], pl.ds(hc * half, half), :]

            def first(j, chip):
                return pltpu.make_async_remote_copy(
                    src_ref=s_ref.at[pl.ds(c * half, half), :], dst_ref=rows((x, y), c), send_sem=send_sems.at[j],
                    recv_sem=recv_sems.at[j], device_id=(*chip, c), device_id_type=MESH)

            def landed(j, chip, hc):
                return pltpu.make_async_remote_copy(
                    src_ref=rows(chip, hc), dst_ref=rows(chip, hc), send_sem=send_sems.at[j], recv_sem=recv_sems.at[j],
                    device_id=(x, y, 1 - c), device_id_type=MESH)

            @pl.when(step == 0)
            def _():
                for j, chip in enumerate(chips):
                    first(j, chip).start()

            @pl.when(step == (3 * n_i * n_j) // 4)
            def _():
                for j, chip in enumerate(chips):
                    landed(j, chip, c).wait_recv()
                    landed(3 + j, chip, c).start()
        else:
            o_ref = refs[2]

        o_ref[...] = _dot_nt(a_ref[...], b_ref[...]).astype(o_ref.dtype)

        if has_g:
            @pl.when(step == n_i * n_j - 1)
            def _():
                for j, chip in enumerate(chips):
                    landed(3 + j, chip, 1 - c).wait_recv()
                for j, chip in enumerate(chips):
                    first(j, chip).wait_send()
                    landed(3 + j, chip, c).wait_send()

    in_specs = [pl.BlockSpec((tm, K), lambda i, j: (i, 0)), pl.BlockSpec((tn, K), lambda i, j: (j, 0))]
    out_shape = jax.ShapeDtypeStruct((M, N), out_dtype)
    out_specs = pl.BlockSpec((tm, tn), lambda i, j: (i, j))
    if not has_g:
        return pl.pallas_call(body, name=name, out_shape=out_shape, grid=(n_i, n_j), in_specs=in_specs, out_specs=out_specs,
                              compiler_params=_cp(("parallel", "arbitrary")))(a, b)
    out, g = pl.pallas_call(
        body, name=name, out_shape=[out_shape, jax.ShapeDtypeStruct((N_CHIPS, *gather.shape), gather.dtype)], grid=(n_i, n_j),
        in_specs=in_specs + [pl.BlockSpec(memory_space=pl.ANY)], out_specs=[out_specs, pl.BlockSpec(memory_space=pl.ANY)],
        scratch_shapes=[pltpu.SemaphoreType.DMA((6,)), pltpu.SemaphoreType.DMA((6,))],
        compiler_params=_cp(("arbitrary", "arbitrary")))(a, b, gather)
    chip = 2 * lax.axis_index("x") + lax.axis_index("y")
    return out, lax.dynamic_update_index_in_dim(g, gather, chip, 0)


def mm_tn(name, a, b, init=None, tm=1024, tn=1024, tk=512):
    T, M = a.shape
    N = b.shape[1]
    tm, tn, tk = min(tm, M), min(tn, N), min(tk, T)
    assert M % tm == 0 and N % tn == 0 and T % tk == 0, (M, N, T)
    has_init = init is not None

    def body(*refs):
        if has_init:
            a_ref, b_ref, i_ref, o_ref = refs
        else:
            a_ref, b_ref, o_ref = refs
        k = pl.program_id(2)

        @pl.when(k == 0)
        def _():
            o_ref[...] = i_ref[...] if has_init else jnp.zeros(o_ref.shape, F32)

        o_ref[...] += _dot_tn(a_ref[...], b_ref[...])

    in_specs = [pl.BlockSpec((tk, tm), lambda i, j, k: (k, i)), pl.BlockSpec((tk, tn), lambda i, j, k: (k, j))]
    args = [a, b]
    if has_init:
        in_specs.append(pl.BlockSpec((tm, tn), lambda i, j, k: (i, j)))
        args.append(init)
    return pl.pallas_call(
        body, name=name, out_shape=jax.ShapeDtypeStruct((M, N), F32), grid=(M // tm, N // tn, T // tk),
        in_specs=in_specs, out_specs=pl.BlockSpec((tm, tn), lambda i, j, k: (i, j)),
        compiler_params=_cp(("parallel", "parallel", "arbitrary")))(*args)


def mm_nn_multi(name, pairs, out_dtype, tm=512, tk=512, exchange=None):
    M = pairs[0][0].shape[0]
    N = pairs[0][1].shape[1]
    tm = min(tm, M)
    assert M % tm == 0
    plan = []
    step = 0
    for a, b in pairs:
        K = a.shape[1]
        t = min(tk, K)
        assert K % t == 0 and b.shape == (K, N)
        plan.append((t, step, K // t))
        step += K // t
    nsteps = step
    npairs = len(pairs)

    n_i = M // tm
    has_x = exchange is not None

    def body(*refs):
        if has_x:
            p_ref, o_ref, land_ref, acc, send_sems, recv_sems = refs[2 * npairs:]
        else:
            o_ref, acc = refs[2 * npairs:]
        i, k = pl.program_id(0), pl.program_id(1)

        if has_x:
            x, y, c = _me()
            me_chip = 2 * x + y
            chips = [(1 - x, y), (x, 1 - y), (1 - x, 1 - y)]

            def copy(j, src_chip, dst_chip, to):
                return pltpu.make_async_remote_copy(
                    src_ref=p_ref.at[src_chip], dst_ref=land_ref.at[dst_chip], send_sem=send_sems.at[j], recv_sem=recv_sems.at[j],
                    device_id=(*to, c), device_id_type=MESH)

            @pl.when((i == 0) & (k == 0))
            def _():
                for j, chip in enumerate(chips):
                    copy(j, 2 * chip[0] + chip[1], me_chip, chip).start()

        @pl.when(k == 0)
        def _():
            acc[...] = jnp.zeros(acc.shape, F32)

        for p, (_, first, n) in enumerate(plan):
            @pl.when((k >= first) & (k < first + n))
            def _(p=p):
                acc[...] += _dot(refs[2 * p][...], refs[2 * p + 1][...])

        @pl.when(k == nsteps - 1)
        def _():
            o_ref[...] = acc[...].astype(o_ref.dtype)

        if has_x:
            @pl.when((i == n_i - 1) & (k == nsteps - 1))
            def _():
                for j, chip in enumerate(chips):
                    copy(j, me_chip, 2 * chip[0] + chip[1], chip).wait_recv()
                for j, chip in enumerate(chips):
                    copy(j, 2 * chip[0] + chip[1], me_chip, chip).wait_send()

    in_specs, args = [], []
    for (a, b), (t, first, n) in zip(pairs, plan):
        in_specs.append(pl.BlockSpec((tm, t), lambda i, k, first=first, n=n: (i, jnp.clip(k - first, 0, n - 1))))
        in_specs.append(pl.BlockSpec((t, N), lambda i, k, first=first, n=n: (jnp.clip(k - first, 0, n - 1), 0)))
        args += [a, b]
    out_shape = jax.ShapeDtypeStruct((M, N), out_dtype)
    out_specs = pl.BlockSpec((tm, N), lambda i, k: (i, 0))
    scratch = [pltpu.VMEM((tm, N), F32)]
    if has_x:
        in_specs.append(pl.BlockSpec(memory_space=pl.ANY))
        args.append(exchange)
        out_shape = [out_shape, jax.ShapeDtypeStruct(exchange.shape, exchange.dtype)]
        out_specs = [out_specs, pl.BlockSpec(memory_space=pl.ANY)]
        scratch += [pltpu.SemaphoreType.DMA((3,)), pltpu.SemaphoreType.DMA((3,))]
    res = pl.pallas_call(
        body, name=name, out_shape=out_shape, grid=(n_i, nsteps), in_specs=in_specs, out_specs=out_specs,
        scratch_shapes=scratch, compiler_params=_cp(("arbitrary", "arbitrary")))(*args)
    if not has_x:
        return res
    out, landed = res
    chip = 2 * lax.axis_index("x") + lax.axis_index("y")
    own = lax.dynamic_index_in_dim(exchange, chip, 0, keepdims=True)
    return out, lax.dynamic_update_slice_in_dim(landed, own, chip, 0)


def tok_call(name, body, tiled, perb, glob, out_tiled, out_perb, out_glob, tm=256):
    widths = [t[1] if isinstance(t, tuple) else t.shape[2] for t in tiled]
    tiled = [t[0] if isinstance(t, tuple) else t for t in tiled]
    Bn, L = tiled[0].shape[:2]
    tm = min(tm, L)
    assert L % tm == 0
    n_t, n_p, n_g = len(tiled), len(perb), len(glob)
    o_t, o_p, o_g = len(out_tiled), len(out_perb), len(out_glob)
    n_in = n_t + n_p + n_g

    def kern(*refs):
        ins, outs = refs[:n_in], refs[n_in:]
        b, j = pl.program_id(0), pl.program_id(1)
        vals = [r[0] for r in ins[:n_t + n_p]] + [r[...] for r in ins[n_t + n_p:]]
        res = body(*vals)
        if not isinstance(res, (tuple, list)):
            res = (res,)
        assert len(res) == o_t + o_p + o_g, (name, len(res))
        for r, v in zip(outs[:o_t], res[:o_t]):
            r[0] = v.astype(r.dtype)

        def accum(r, v, first, lead):
            @pl.when(first)
            def _():
                r[...] = jnp.zeros(r.shape, F32)
            if lead:
                r[0] += v
            else:
                r[...] += v

        for r, v in zip(outs[o_t:o_t + o_p], res[o_t:o_t + o_p]):
            accum(r, v, j == 0, True)
        for r, v in zip(outs[o_t + o_p:], res[o_t + o_p:]):
            accum(r, v, (j == 0) & (b == 0), False)

    in_specs = ([pl.BlockSpec((1, tm, w), lambda b, j: (b, j, 0)) for w in widths]
                + [pl.BlockSpec((1, 1, a.shape[2]), lambda b, j: (b, 0, 0)) for a in perb]
                + [pl.BlockSpec(a.shape, lambda b, j: (0, 0), pipeline_mode=pl.Buffered(1)) for a in glob])
    out_shape = ([jax.ShapeDtypeStruct((Bn, L, w), dt) for w, dt in out_tiled]
                 + [jax.ShapeDtypeStruct((Bn, 1, w), F32) for w in out_perb]
                 + [jax.ShapeDtypeStruct(s, F32) for s in out_glob])
    out_specs = ([pl.BlockSpec((1, tm, w), lambda b, j: (b, j, 0)) for w, _ in out_tiled]
                 + [pl.BlockSpec((1, 1, w), lambda b, j: (b, 0, 0)) for w in out_perb]
                 + [pl.BlockSpec(s, lambda b, j: (0, 0)) for s in out_glob])
    return pl.pallas_call(
        kern, name=name, out_shape=out_shape, grid=(Bn, L // tm), in_specs=in_specs, out_specs=out_specs,
        compiler_params=_cp(("arbitrary", "arbitrary")))(*tiled, *perb, *glob)


def slab_call(name, body, slabs, colparams, out_slabs, out_colred, wc=LANES):
    Bn, L = slabs[0][0].shape[:2]
    w_out = out_slabs[0][0]
    assert w_out % wc == 0 and all(off % wc == 0 for _, off in slabs + colparams)
    n_col = w_out // wc
    n_s, n_c = len(slabs), len(colparams)
    o_s = len(out_slabs)

    def kern(*refs):
        ins, outs = refs[:n_s + n_c], refs[n_s + n_c:]
        b = pl.program_id(1)
        vals = [r[0] for r in ins[:n_s]] + [r[...] for r in ins[n_s:]]
        res = body(*vals)
        if not isinstance(res, (tuple, list)):
            res = (res,)
        assert len(res) == o_s + len(out_colred), name
        for r, v in zip(outs[:o_s], res[:o_s]):
            r[0] = v.astype(r.dtype)

        def accum(r, v):
            @pl.when(b == 0)
            def _():
                r[...] = jnp.zeros(r.shape, F32)
            r[...] += v

        for r, v in zip(outs[o_s:], res[o_s:]):
            accum(r, v)

    in_specs = ([pl.BlockSpec((1, L, wc), lambda j, b, o=off // wc: (b, 0, o + j)) for _, off in slabs]
                + [pl.BlockSpec((a.shape[0], wc), lambda j, b, o=off // wc: (0, o + j)) for a, off in colparams])
    out_shape = ([jax.ShapeDtypeStruct((Bn, L, w), dt) for w, dt in out_slabs]
                 + [jax.ShapeDtypeStruct((r, w_out), F32) for r in out_colred])
    out_specs = ([pl.BlockSpec((1, L, wc), lambda j, b: (b, 0, j)) for _ in out_slabs]
                 + [pl.BlockSpec((r, wc), lambda j, b: (0, j)) for r in out_colred])
    return pl.pallas_call(
        kern, name=name, out_shape=out_shape, grid=(n_col, Bn), in_specs=in_specs, out_specs=out_specs,
        compiler_params=_cp(("arbitrary", "arbitrary")))(*[a for a, _ in slabs], *[a for a, _ in colparams])


def _rms_r(x):
    return lax.rsqrt(jnp.mean(x * x, axis=-1, keepdims=True) + NORM_EPS)


def _rms_bwd(dxh, x, r):
    return r * (dxh - x * (r * r) * jnp.mean(dxh * x, axis=-1, keepdims=True))


def _colsum(v):
    return jnp.sum(v, axis=0, keepdims=True)


def _stack_rows(rows):
    n, w = len(rows), rows[0].shape[1]
    sub = lax.broadcasted_iota(jnp.int32, (n, w), 0)
    acc = jnp.zeros((n, w), F32)
    for r, row in enumerate(rows):
        acc = acc + jnp.where(sub == r, jnp.broadcast_to(row, (n, w)), 0.0)
    return acc


def prenorm_fwd(name, x, scale, shift, w_pre):
    def body(x, scale, shift, w):
        n = x * _rms_r(x) * w
        return n * (1.0 + scale) + shift

    return tok_call(name, body, [x], [scale, shift], [w_pre], [(D, BF16)], [], [])[0]


def prenorm_bwd(name, x, dhx, scale, w_pre, g_res=None):
    has_res = g_res is not None

    def body(*v):
        if has_res:
            x, dhx, g, scale, w = v
        else:
            x, dhx, scale, w = v
        r = _rms_r(x)
        xr = x * r
        n = xr * w
        dn = dhx * (1.0 + scale)
        dx = _rms_bwd(dn * w, x, r)
        if has_res:
            dx = dx + g
        return dx, _colsum(dhx * n), _colsum(dhx), _colsum(dn * xr)

    tiled = [x, dhx] + ([g_res] if has_res else [])
    return tok_call(name, body, tiled, [scale], [w_pre], [(D, F32)], [D, D], [(1, D)])


def _shift_rows(x, o, tok, L):
    if o == 0:
        return x
    rolled = pltpu.roll(x, (-o) % L, 0)
    return jnp.where((tok + o >= 0) & (tok + o < L), rolled, 0.0)


def conv_fwd(name, xbc_raw, conv_w, conv_b):
    L = xbc_raw.shape[1]

    def body(x, w, b):
        tok = lax.broadcasted_iota(jnp.int32, x.shape, 0)
        pre = b
        for k in range(4):
            pre = pre + _shift_rows(x, k - 2, tok, L) * w[k:k + 1]
        return _silu(pre)

    return slab_call(name, body, [(xbc_raw, 0)], [(conv_w, 0), (conv_b, 0)], [(CONV_DIM, F32)], [])[0]


CONV_ROWS = 128
CONV_HALO = 8


def _halo_chunks(L, load, work):
    ch, hl = CONV_ROWS, CONV_HALO
    n = L // ch
    assert L % ch == 0
    if n == 1:
        z = jnp.zeros_like(load(0, hl))
        work(0, jnp.concatenate([z, load(0, ch), z], axis=0))
        return
    z = jnp.zeros_like(load(0, hl))
    work(0, jnp.concatenate([z, load(0, ch + hl)], axis=0))

    def step(i, carry):
        start = pl.multiple_of(i * ch, ch)
        work(start, load(pl.multiple_of(start - hl, hl), ch + 2 * hl))
        return carry

    lax.fori_loop(1, n - 1, step, 0)
    work(L - ch, jnp.concatenate([load(L - ch - hl, ch + hl), z], axis=0))


def _rows_at(xh, o):
    return xh if o == 0 else pltpu.roll(xh, (-o) % xh.shape[0], 0)


def conv_bwd(name, xbc_raw, dparts, conv_w, conv_b, col0, width, scaled=None, wc=LANES):
    Bn, L, _ = xbc_raw.shape
    n_d = len(dparts)
    has_s = scaled is not None
    mid = slice(CONV_HALO, CONV_HALO + CONV_ROWS)
    c0 = col0 // wc

    def kern(*refs):
        x_ref, d_refs = refs[0], refs[1:1 + n_d]
        pos = 1 + n_d
        if has_s:
            s_ref, pos = refs[pos], pos + 1
        w_ref, b_ref = refs[pos], refs[pos + 1]
        pos += 2
        if has_s:
            scale = refs[pos][...]
            pos += 1
        dx_ref, dw_ref, db_ref = refs[pos:pos + 3]
        acc = refs[pos + 3]
        w, b = w_ref[...], b_ref[...]
        acc[...] = jnp.zeros(acc.shape, F32)

        def load(s, n):
            dy = d_refs[0][0, pl.ds(s, n), :]
            for r in d_refs[1:]:
                dy = dy + r[0, pl.ds(s, n), :]
            if has_s:
                dy = dy + s_ref[0, pl.ds(s, n), :] * scale
            return jnp.concatenate([x_ref[0, pl.ds(s, n), :], dy], axis=1)

        def work(start, both):
            xh, dyh = both[:, 0:wc], both[:, wc:]
            taps = [_rows_at(xh, k - 2) for k in range(4)]
            pre = b
            for k in range(4):
                pre = pre + taps[k] * w[k:k + 1]
            dpre = dyh * _dsilu(pre)
            dx = dpre * w[2:3]
            for k in (0, 1, 3):
                dx = dx + _rows_at(dpre, 2 - k) * w[k:k + 1]
            dx_ref[0, pl.ds(start, CONV_ROWS), :] = dx[mid].astype(dx_ref.dtype)
            dm = dpre[mid]
            acc[...] += _stack_rows([_colsum(dm * taps[k][mid]) for k in range(4)] + [_colsum(dm)] + [jnp.zeros((1, wc), F32)] * 3)

        _halo_chunks(L, load, work)
        first = pl.program_id(1) == 0

        @pl.when(first)
        def _():
            dw_ref[...] = acc[0:4]
            db_ref[...] = acc[4:5]

        @pl.when(jnp.logical_not(first))
        def _():
            dw_ref[...] += acc[0:4]
            db_ref[...] += acc[4:5]

    slab = lambda off: pl.BlockSpec((1, L, wc), lambda j, b, off=off: (b, 0, off + j))
    in_specs = [slab(c0)] + [slab(0)] * n_d + ([slab(0)] if has_s else [])
    in_specs += [pl.BlockSpec((4, wc), lambda j, b: (0, c0 + j)), pl.BlockSpec((1, wc), lambda j, b: (0, c0 + j))]
    args = [xbc_raw, *dparts] + ([scaled[0]] if has_s else []) + [conv_w, conv_b]
    if has_s:
        in_specs.append(pl.BlockSpec((1, wc), lambda j, b: (0, j)))
        args.append(scaled[1])
    return pl.pallas_call(
        kern, name=name,
        out_shape=[jax.ShapeDtypeStruct((Bn, L, width), BF16), jax.ShapeDtypeStruct((4, width), F32), jax.ShapeDtypeStruct((1, width), F32)],
        grid=(width // wc, Bn), in_specs=in_specs,
        out_specs=[pl.BlockSpec((1, L, wc), lambda j, b: (b, 0, j)), pl.BlockSpec((4, wc), lambda j, b: (0, j)),
                   pl.BlockSpec((1, wc), lambda j, b: (0, j))],
        scratch_shapes=[pltpu.VMEM((8, wc), F32)],
        compiler_params=_cp(("arbitrary", "arbitrary")))(*args)


def _box_mean(x, k, step, pos, n, L, transpose):
    lo, hi = k // 2, k - 1 - k // 2
    cnt = (jnp.minimum(pos + hi + 1, n) - jnp.maximum(pos - lo, 0)).astype(F32)
    if transpose:
        x = x / cnt
        lo, hi = hi, lo
    acc = x
    for o in range(-lo, hi + 1):
        if o == 0:
            continue
        rolled = pltpu.roll(x, (-o * step) % L, 0)
        acc = acc + jnp.where((pos + o >= 0) & (pos + o < n), rolled, 0.0)
    return acc if transpose else acc / cnt


def pool_diff(name, v, col0, gi, transpose):
    L = v.shape[1]
    rows = L // GRID_W
    k = POOL_WINDOWS[gi]

    def body(x):
        tok = lax.broadcasted_iota(jnp.int32, x.shape, 0)
        col = tok & (GRID_W - 1)
        row = tok >> 6
        if not transpose:
            m = _box_mean(x, k, GRID_W, row, rows, L, False)
            m = _box_mean(m, k, 1, col, GRID_W, L, False)
        else:
            m = _box_mean(x, k, 1, col, GRID_W, L, True)
            m = _box_mean(m, k, GRID_W, row, rows, L, True)
        return m - x

    return slab_call(name, body, [(v, col0)], [], [(POOL_GROUP, BF16)], [])[0]


def pool_mix_fwd(name, dgs, z_pool, pool_w, pool_scale):
    def body(d0, d1, d2, d3, z, w, scale):
        q = jnp.concatenate([_dot(d, w[g * POOL_GROUP:(g + 1) * POOL_GROUP]) for g, d in enumerate((d0, d1, d2, d3))], axis=1)
        return q * scale * _silu(z)

    return tok_call(name, body, list(dgs) + [z_pool], [], [pool_w, pool_scale], [(D, BF16)], [], [])[0]


def pool_mix_bwd(name, dgs, z_pool, dyp, pool_w, pool_scale):
    def body(d0, d1, d2, d3, z, dyp, w, scale):
        ds = (d0, d1, d2, d3)
        q = jnp.concatenate([_dot(d, w[g * POOL_GROUP:(g + 1) * POOL_GROUP]) for g, d in enumerate(ds)], axis=1)
        dypm = dyp * _silu(z)
        dz = dyp * (q * scale) * _dsilu(z)
        dq = (dypm * scale).astype(BF16)
        dds, gws = [], []
        for g, d in enumerate(ds):
            dqg = dq[:, g * POOL_GROUP:(g + 1) * POOL_GROUP]
            dds.append(_dot_nt(dqg, w[g * POOL_GROUP:(g + 1) * POOL_GROUP]))
            gws.append(_dot_tn(d, dqg))
        return (*dds, dz, jnp.concatenate(gws, axis=0), _colsum(dypm * q))

    return tok_call(name, body, list(dgs) + [z_pool, dyp], [], [pool_w, pool_scale],
                    [(POOL_GROUP, F32)] * 4 + [(D, BF16)], [], [(D, POOL_GROUP), (1, D)])


def _cumsum_lanes(a, reverse):
    n = a.shape[1]
    k = lax.broadcasted_iota(jnp.int32, (n, n), 0)
    i = lax.broadcasted_iota(jnp.int32, (n, n), 1)
    tri = jnp.where((k >= i) if reverse else (k <= i), 1.0, 0.0).astype(BF16)
    return _dot_exact01(a, tri)


def _rows_to_cols(rows):
    r = rows.shape[0]
    if r < LANES:
        rows = jnp.concatenate([rows, jnp.zeros((LANES - r, rows.shape[1]), F32)], axis=0)
    return rows.T


def _cols_to_rows(cols):
    q = cols[0].shape[0]
    lane = lax.broadcasted_iota(jnp.int32, (q, LANES), 1)
    acc = jnp.zeros((q, LANES), F32)
    for r, c in enumerate(cols):
        acc = acc + jnp.where(lane == r, c, 0.0)
    return acc.T[0:len(cols)]


def _ssd_scalars(dtraw, bias, alog, reverse):
    dt = _softplus(dtraw + bias)
    A = -jnp.exp(alog)
    cs = _cumsum_lanes(dt * A, reverse)
    total = cs[:, 0:1] if reverse else cs[:, CHUNK - 1:CHUNK]
    return dt, A, cs, total


def _tri_mask(transposed, reverse):
    sub = lax.broadcasted_iota(jnp.int32, (CHUNK, CHUNK), 0)
    lane = lax.broadcasted_iota(jnp.int32, (CHUNK, CHUNK), 1)
    i, j = (lane, sub) if transposed else (sub, lane)
    return (i <= j) if reverse else (i >= j)


GPS = 4


def ssd_fwd(name, dtT, bias, alog, xbc, h0, direction, with_y, y_add=None):
    Bn, L = xbc.shape[:2]
    nc = L // CHUNK
    reverse = direction == 1
    blk0 = direction * (N_BC // GPS)
    gs = range(GPS)
    has_add = y_add is not None

    def chunk_of(s):
        return (nc - 1 - s) if reverse else s

    def kern(dt_ref, bias_ref, alog_ref, x_ref, b_ref, c_ref, h0_ref, *rest):
        if has_add:
            yp_ref, dsk_ref, rest = rest[0], rest[1], rest[2:]
        if with_y:
            y_ref, hs_ref, hf_ref, h_scr = rest
        else:
            hs_ref, hf_ref, h_scr = rest
        s = pl.program_id(2)

        @pl.when(s == 0)
        def _():
            h_scr[...] = h0_ref[0]

        first = lax.broadcasted_iota(jnp.int32, (1, LANES), 1) < HEAD_DIM
        heads = range(HPG)
        psl = [slice((r // 2) * LANES, (r // 2 + 1) * LANES) for r in heads]
        keep = _tri_mask(False, reverse)
        sc, x_bf, bm, h, h_bf, bt, cm, cb, cs_cols = [], [], [], [], [], [], [], [], []
        for g in gs:
            dt, _, cs, total = _ssd_scalars(dt_ref[0, g * HPG:(g + 1) * HPG], bias_ref[g], alog_ref[g], reverse)
            u = cs - jnp.log(dt)
            sc.append((cs, u, jnp.exp(total - u), jnp.exp(total)))
            x_bf.append(x_ref[0, :, g * GW:(g + 1) * GW].astype(BF16))
            bm.append(b_ref[0, :, g * D_STATE:(g + 1) * D_STATE])
            h.append(h_scr[g])
            h_bf.append(h[g].astype(BF16))
            hs_ref[0, g, 0] = h[g]
            bt.append(bm[g].T)
            if with_y:
                cm.append(c_ref[0, :, g * D_STATE:(g + 1) * D_STATE])
                cb.append(_dot_nt(cm[g].astype(BF16), bm[g].astype(BF16)))
                cs_cols.append(_rows_to_cols(cs))
        lhs = [[] for _ in gs]
        if with_y:
            for g in gs:
                cs, u = sc[g][0], sc[g][1]
                for r in heads:
                    cs_col = jnp.broadcast_to(cs_cols[g][:, r:r + 1], (CHUNK, LANES))
                    wf = cb[g] * jnp.exp(jnp.where(keep, cs_col - u[r:r + 1], -jnp.inf))
                    lhs[g].append(jnp.concatenate([wf.astype(BF16), (cm[g] * jnp.exp(cs_col)).astype(BF16)], axis=1))
        bts = [[(bt[g] * sc[g][2][r:r + 1]).astype(BF16) for r in heads] for g in gs]
        sts = [[_dot(bts[g][r], x_bf[g][:, psl[r]]) for r in heads] for g in gs]
        if with_y:
            ys = [[_dot(lhs[g][r], jnp.concatenate([x_bf[g][:, psl[r]], h_bf[g][:, psl[r]]], axis=0)) for r in heads] for g in gs]
        for g in gs:
            dc = sc[g][3]
            for p in range(HPG // 2):
                if with_y:
                    cols = slice(g * GW + p * LANES, g * GW + (p + 1) * LANES)
                    yv = jnp.where(first, ys[g][2 * p], ys[g][2 * p + 1])
                    if has_add:
                        yv = yv + yp_ref[0, :, cols] + dsk_ref[:, cols] * x_ref[0, :, cols]
                    y_ref[0, :, cols] = yv
                dc_p = jnp.where(first, dc[2 * p:2 * p + 1], dc[2 * p + 1:2 * p + 2])
                h_scr[g, :, psl[2 * p]] = h[g][:, psl[2 * p]] * dc_p + jnp.where(first, sts[g][2 * p], sts[g][2 * p + 1])

        @pl.when(s == nc - 1)
        def _():
            hf_ref[0] = h_scr[...]

    nb = D_INNER // (GPS * D_STATE)
    in_specs = [
        pl.BlockSpec((1, GPS * HPG, CHUNK), lambda b, g, s: (b, blk0 + g, chunk_of(s))),
        pl.BlockSpec((GPS, HPG, 1), lambda b, g, s: (blk0 + g, 0, 0)),
        pl.BlockSpec((GPS, HPG, 1), lambda b, g, s: (blk0 + g, 0, 0)),
        pl.BlockSpec((1, CHUNK, GPS * GW), lambda b, g, s: (b, chunk_of(s), g)),
        pl.BlockSpec((1, CHUNK, GPS * D_STATE), lambda b, g, s: (b, chunk_of(s), nb + g)),
        pl.BlockSpec((1, CHUNK, GPS * D_STATE), lambda b, g, s: (b, chunk_of(s), nb + N_BC // GPS + g)),
        pl.BlockSpec((1, GPS, D_STATE, GW), lambda b, g, s: (b, g, 0, 0)),
    ]
    args = [dtT, bias, alog, xbc, xbc, xbc, h0]
    if has_add:
        in_specs += [pl.BlockSpec((1, CHUNK, GPS * GW), lambda b, g, s: (b, chunk_of(s), g)),
                     pl.BlockSpec((1, GPS * GW), lambda b, g, s: (0, g))]
        args += list(y_add)
    out_shape, out_specs = [], []
    if with_y:
        out_shape.append(jax.ShapeDtypeStruct((Bn, L, D_INNER), F32))
        out_specs.append(pl.BlockSpec((1, CHUNK, GPS * GW), lambda b, g, s: (b, chunk_of(s), g)))
    out_shape += [jax.ShapeDtypeStruct((Bn, N_BC, nc, D_STATE, GW), F32), jax.ShapeDtypeStruct((Bn, N_BC, D_STATE, GW), F32)]
    out_specs += [pl.BlockSpec((1, GPS, 1, D_STATE, GW), lambda b, g, s: (b, g, chunk_of(s), 0, 0)),
                  pl.BlockSpec((1, GPS, D_STATE, GW), lambda b, g, s: (b, g, 0, 0))]
    return pl.pallas_call(
        kern, name=name, out_shape=out_shape, grid=(Bn, N_BC // GPS, nc), in_specs=in_specs, out_specs=out_specs,
        scratch_shapes=[pltpu.VMEM((GPS, D_STATE, GW), F32)],
        compiler_params=_cp(("arbitrary", "arbitrary", "arbitrary")))(*args)


def ssd_bwd(name, dtT, bias, alog, xbc, h_start, dy, dh_final, direction, dx_add=None):
    Bn, L = xbc.shape[:2]
    nc = L // CHUNK
    reverse = direction == 1
    blk0 = direction * (N_BC // GPS)
    has_y = dy is not None
    has_add = dx_add is not None
    assert has_y or not has_add
    last = 0 if reverse else CHUNK - 1
    gs = range(GPS)

    def chunk_of(s):
        return s if reverse else (nc - 1 - s)

    def kern(*refs):
        if has_add:
            dxp_ref, dsk_ref = refs[9], refs[10]
            refs = refs[:9] + refs[11:]
        if has_y:
            (dt_ref, bias_ref, alog_ref, x_ref, b_ref, hs_ref, dhf_ref, c_ref, dy_ref,
             dx_ref, db_ref, ddt_ref, dbias_ref, dalog_ref, dh0_ref, dc_ref, dh_scr) = refs
        else:
            (dt_ref, bias_ref, alog_ref, x_ref, b_ref, hs_ref, dhf_ref,
             dx_ref, db_ref, ddt_ref, dbias_ref, dalog_ref, dh0_ref, dh_scr) = refs
        s = pl.program_id(2)

        @pl.when(s == 0)
        def _():
            dh_scr[...] = dhf_ref[0]
            dbias_ref[...] = jnp.zeros(dbias_ref.shape, F32)
            dalog_ref[...] = jnp.zeros(dalog_ref.shape, F32)

        first = lax.broadcasted_iota(jnp.int32, (1, LANES), 1) < HEAD_DIM
        heads = range(HPG)
        psl = [slice((r // 2) * LANES, (r // 2 + 1) * LANES) for r in heads]
        mine = [first if r % 2 == 0 else jnp.logical_not(first) for r in heads]
        zeros_bf = jnp.zeros((CHUNK, LANES), BF16)
        keep = _tri_mask(True, reverse)
        ctx = []
        for g in gs:
            dtraw = dt_ref[0, g * HPG:(g + 1) * HPG]
            dt, A, cs, total = _ssd_scalars(dtraw, bias_ref[g], alog_ref[g], reverse)
            u = cs - jnp.log(dt)
            c = dict(dtraw=dtraw, dt=dt, A=A, cs=cs, total=total, u=u, dtt=jnp.exp(total - u), dcy=jnp.exp(total),
                     u_cols=_rows_to_cols(u), x_bf=x_ref[0, :, g * GW:(g + 1) * GW].astype(BF16),
                     bm=b_ref[0, :, g * D_STATE:(g + 1) * D_STATE], h=hs_ref[0, g, 0], dh=dh_scr[g])
            c["bt"] = c["bm"].T
            c["dh_bf"] = c["dh"].astype(BF16)
            if has_y:
                c["cm"] = c_ref[0, :, g * D_STATE:(g + 1) * D_STATE]
                c["ct"] = c["cm"].T
                c["e_row"] = jnp.exp(cs)
                c["dy_bf"] = dy_ref[0, :, g * GW:(g + 1) * GW].astype(BF16)
                c["h_bf"] = c["h"].astype(BF16)
                c["cbt"] = _dot_nt(c["bm"].astype(BF16), c["cm"].astype(BF16))
            ctx.append(c)
        for c in ctx:
            c["lhs"], c["et"] = [], []
            for r in heads:
                u_col = jnp.broadcast_to(c["u_cols"][:, r:r + 1], (CHUNK, LANES))
                bs = (c["bm"] * jnp.exp(c["total"][r:r + 1] - u_col)).astype(BF16)
                if has_y:
                    et = jnp.exp(jnp.where(keep, c["cs"][r:r + 1] - u_col, -jnp.inf))
                    c["et"].append(et)
                    c["lhs"].append(jnp.concatenate([(c["cbt"] * et).astype(BF16), bs], axis=1))
                else:
                    c["lhs"].append(bs)
        for c in ctx:
            c["p2raw"] = [_dot_nt(c["dh_bf"][:, psl[r]], jnp.where(mine[r], c["x_bf"][:, psl[r]], zeros_bf)) for r in heads]
            if has_y:
                c["a1"] = [_dot_nt(jnp.concatenate([c["x_bf"][:, psl[r]], c["h_bf"][:, psl[r]]], axis=0),
                                   jnp.where(mine[r], c["dy_bf"][:, psl[r]], zeros_bf)) for r in heads]
                c["news"] = [_dot((c["ct"] * c["e_row"][r:r + 1]).astype(BF16), c["dy_bf"][:, psl[r]]) for r in heads]
                c["dxs"] = [_dot(c["lhs"][r], jnp.concatenate([c["dy_bf"][:, psl[r]], c["dh_bf"][:, psl[r]]], axis=0)) for r in heads]
            else:
                c["dxs"] = [_dot(c["lhs"][r], c["dh_bf"][:, psl[r]]) for r in heads]
        for g, c in enumerate(ctx):
            dbt = jnp.zeros((D_STATE, CHUNK), F32)
            dcbt = jnp.zeros((CHUNK, CHUNK), F32)
            dct = jnp.zeros((D_STATE, CHUNK), F32)
            tots, out_rows, in_rows, in_cols = [], [], [], []
            for r in heads:
                if has_y:
                    pt = c["a1"][r][0:CHUNK] * c["et"][r]
                    dcbt = dcbt + pt
                    mt = pt * c["cbt"]
                    ph = c["a1"][r][CHUNK:] * c["e_row"][r:r + 1]
                    dct = dct + ph
                    out_rows.append(_colsum(mt + c["ct"] * ph))
                    in_cols.append(jnp.sum(mt, axis=1, keepdims=True))
                p2 = c["p2raw"][r] * c["dtt"][r:r + 1]
                dbt = dbt + p2
                t_term = _colsum(c["bt"] * p2)
                in_rows.append(t_term)
                hdh = c["h"][:, psl[r]] * c["dh"][:, psl[r]]
                tot = jnp.sum(t_term, axis=1, keepdims=True) + c["dcy"][r:r + 1] * jnp.sum(jnp.where(mine[r], hdh, 0.0), keepdims=True)
                tots.append(jnp.broadcast_to(tot, (1, CHUNK)))
            for p in range(HPG // 2):
                cols = slice(g * GW + p * LANES, g * GW + (p + 1) * LANES)
                dxv = jnp.where(first, c["dxs"][2 * p], c["dxs"][2 * p + 1])
                if has_add:
                    dxv = dxv + dxp_ref[0, :, cols] + dsk_ref[:, cols] * dy_ref[0, :, cols]
                dx_ref[0, :, cols] = dxv
                new = c["dh"][:, psl[2 * p]] * jnp.where(first, c["dcy"][2 * p:2 * p + 1], c["dcy"][2 * p + 1:2 * p + 2])
                if has_y:
                    new = new + jnp.where(first, c["news"][2 * p], c["news"][2 * p + 1])
                dh_scr[g, :, psl[2 * p]] = new
            db = dbt.T
            if has_y:
                dcbt_bf = dcbt.astype(BF16)
                db = db + _dot(dcbt_bf, c["cm"].astype(BF16))
                dc_ref[0, :, g * D_STATE:(g + 1) * D_STATE] = dct.T + _dot_tn(dcbt_bf, c["bm"].astype(BF16))
            db_ref[0, :, g * D_STATE:(g + 1) * D_STATE] = db
            s_row = _stack_rows(in_rows)
            lane = lax.broadcasted_iota(jnp.int32, (HPG, CHUNK), 1)
            dcs = jnp.where(lane == last, _stack_rows(tots), 0.0)
            if has_y:
                s_row = s_row + _cols_to_rows(in_cols)
                dcs = dcs + _stack_rows(out_rows)
            dcs = dcs - s_row
            da = _cumsum_lanes(dcs, not reverse)
            ddt = da * c["A"] + jnp.where(c["dt"] > 0.0, s_row / c["dt"], 0.0)
            ddtraw = ddt * _sigmoid(c["dtraw"] + bias_ref[g])
            ddt_ref[0, g * HPG:(g + 1) * HPG] = ddtraw
            dbias_ref[0, g] += jnp.sum(ddtraw, axis=1, keepdims=True)
            dalog_ref[0, g] += jnp.sum(da * c["dt"], axis=1, keepdims=True) * c["A"]

        @pl.when(s == nc - 1)
        def _():
            dh0_ref[0] = dh_scr[...]

    nb = D_INNER // (GPS * D_STATE)
    cidx = lambda b, g, s: (b, chunk_of(s), g)
    hidx = lambda b, g, s: (b, g, 0, 0)
    in_specs = [
        pl.BlockSpec((1, GPS * HPG, CHUNK), lambda b, g, s: (b, blk0 + g, chunk_of(s))),
        pl.BlockSpec((GPS, HPG, 1), lambda b, g, s: (blk0 + g, 0, 0)),
        pl.BlockSpec((GPS, HPG, 1), lambda b, g, s: (blk0 + g, 0, 0)),
        pl.BlockSpec((1, CHUNK, GPS * GW), cidx),
        pl.BlockSpec((1, CHUNK, GPS * D_STATE), lambda b, g, s: (b, chunk_of(s), nb + g)),
        pl.BlockSpec((1, GPS, 1, D_STATE, GW), lambda b, g, s: (b, g, chunk_of(s), 0, 0)),
        pl.BlockSpec((1, GPS, D_STATE, GW), hidx),
    ]
    args = [dtT, bias, alog, xbc, xbc, h_start, dh_final]
    if has_y:
        in_specs += [pl.BlockSpec((1, CHUNK, GPS * D_STATE), lambda b, g, s: (b, chunk_of(s), nb + N_BC // GPS + g)),
                     pl.BlockSpec((1, CHUNK, GPS * GW), cidx)]
        args += [xbc, dy]
    if has_add:
        in_specs += [pl.BlockSpec((1, CHUNK, GPS * GW), cidx), pl.BlockSpec((1, GPS * GW), lambda b, g, s: (0, g))]
        args += list(dx_add)
    out_shape = [jax.ShapeDtypeStruct((Bn, L, D_INNER), F32), jax.ShapeDtypeStruct((Bn, L, N_BC * D_STATE), F32),
                 jax.ShapeDtypeStruct((Bn, N_HEADS, L), F32), jax.ShapeDtypeStruct((Bn, N_BC, HPG, 1), F32),
                 jax.ShapeDtypeStruct((Bn, N_BC, HPG, 1), F32), jax.ShapeDtypeStruct((Bn, N_BC, D_STATE, GW), F32)]
    out_specs = [pl.BlockSpec((1, CHUNK, GPS * GW), cidx), pl.BlockSpec((1, CHUNK, GPS * D_STATE), cidx),
                 pl.BlockSpec((1, GPS * HPG, CHUNK), lambda b, g, s: (b, g, chunk_of(s))),
                 pl.BlockSpec((1, GPS, HPG, 1), hidx), pl.BlockSpec((1, GPS, HPG, 1), hidx), pl.BlockSpec((1, GPS, D_STATE, GW), hidx)]
    if has_y:
        out_shape.append(jax.ShapeDtypeStruct((Bn, L, N_BC * D_STATE), F32))
        out_specs.append(pl.BlockSpec((1, CHUNK, GPS * D_STATE), cidx))
    res = pl.pallas_call(
        kern, name=name, out_shape=out_shape, grid=(Bn, N_BC // GPS, nc), in_specs=in_specs, out_specs=out_specs,
        scratch_shapes=[pltpu.VMEM((GPS, D_STATE, GW), F32)],
        compiler_params=_cp(("arbitrary", "arbitrary", "arbitrary")))(*args)
    dxs, db, ddt, dbias, dalog, dh0 = res[:6]
    return dxs, db, (res[6] if has_y else None), ddt, dbias, dalog, dh0


def _group_mean(v):
    gw = D_INNER // N_BC
    parts = [jnp.broadcast_to(jnp.mean(v[:, g * gw:(g + 1) * gw], axis=-1, keepdims=True), (v.shape[0], gw)) for g in range(N_BC)]
    return jnp.concatenate(parts, axis=1)


def gated_norm_fwd(name, y, z, w_norm):
    def body(y, z, w):
        u = y * _silu(z)
        r = lax.rsqrt(_group_mean(u * u) + NORM_EPS)
        return u * r * w

    return tok_call(name, body, [y, z], [], [w_norm], [(D_INNER, BF16)], [], [])[0]


def gated_norm_bwd(name, y, xs_src, z, d_out, w_norm, head_sel):
    def body(y, xs, z, do, w, sel):
        sz = _silu(z)
        u = y * sz
        r = lax.rsqrt(_group_mean(u * u) + NORM_EPS)
        duh = do * w
        du = r * (duh - u * (r * r) * _group_mean(duh * u))
        dy = du * sz
        dz = du * y * _dsilu(z)
        dsk_heads = _dot_exact01(jnp.broadcast_to(_colsum(dy * xs), (8, D_INNER)), sel)
        return dy, dz, _colsum(do * u * r), dsk_heads

    return tok_call(name, body, [y, xs_src, z, d_out], [], [w_norm, head_sel],
                    [(D_INNER, F32), (D_INNER, BF16)], [], [(1, D_INNER), (8, LANES)], tm=128)


def merge_fwd(name, y_pool, y_ssd, gatepre, x, target, gate, b_merge, norm_post, w_pp, w_ps, w_out):
    def body(yp, ys, gp, x, tgt, gate, bm, wpost, w_pp, w_ps, w_out):
        p1 = _dot(yp, w_pp)
        p2 = _dot(ys, w_ps)
        gates = _sigmoid(gp + bm)
        merged = gates[:, :D] * p1 + gates[:, D:] * p2
        out = _dot(merged.astype(BF16), w_out)
        r = _rms_r(out)
        outr = out * r
        nq = outr * wpost
        err = x + gate * nq - tgt
        loss = 0.5 * jnp.sum(jnp.mean(err * err, axis=-1, keepdims=True), keepdims=True).reshape(1, 1)
        g = err * (1.0 / D)
        dnq = g * gate
        dout = _rms_bwd(dnq * wpost, out, r)
        return merged, p1, p2, dout, g, _colsum(g * nq), _colsum(dnq * outr), jnp.broadcast_to(loss, (1, LANES))

    return tok_call(name, body, [y_pool, y_ssd, gatepre, x, target], [gate], [b_merge, norm_post, w_pp, w_ps, w_out],
                    [(D, BF16), (D, F32), (D, F32), (D, BF16), (D, F32)], [D], [(1, D), (1, LANES)])


def merge_bwd(name, dout, gatepre, p1, p2, b_merge, w_pp, w_ps, w_out):
    def body(dout, gp, p1, p2, bm, w_pp, w_ps, w_out):
        dmerged = _dot_nt(dout, w_out)
        gates = _sigmoid(gp + bm)
        g1, g2 = gates[:, :D], gates[:, D:]
        dp1 = (dmerged * g1).astype(BF16)
        dp2 = (dmerged * g2).astype(BF16)
        dgp = jnp.concatenate([dmerged * p1 * g1 * (1.0 - g1), dmerged * p2 * g2 * (1.0 - g2)], axis=1)
        return dp1, dp2, dgp, _dot_nt(dp1, w_pp), _dot_nt(dp2, w_ps), _colsum(dgp)

    return tok_call(name, body, [dout, gatepre, p1, p2], [], [b_merge, w_pp, w_ps, w_out],
                    [(D, BF16), (D, BF16), (2 * D, BF16), (D, F32), (D_INNER, F32)], [], [(1, 2 * D)])


def _adamw_math(w, g, m, v):
    m = ADAM_B1 * m + (1.0 - ADAM_B1) * g
    v = ADAM_B2 * v + (1.0 - ADAM_B2) * (g * g)
    m_hat = m / (1.0 - ADAM_B1 ** ADAM_STEP)
    v_hat = v / (1.0 - ADAM_B2 ** ADAM_STEP)
    delta = -ADAM_LR * (m_hat / (jnp.sqrt(v_hat) + ADAM_EPS) + ADAM_WD * w)
    return delta, m, v


def adamw(name, w, g, m, v, tr=256):
    R, C = w.shape
    tr = min(tr, R)
    assert R % tr == 0

    def body(w_ref, g_ref, m_ref, v_ref, d_ref, nm_ref, nv_ref):
        d, nm, nv = _adamw_math(w_ref[...], g_ref[...], m_ref[...], v_ref[...])
        d_ref[...] = d
        nm_ref[...] = nm
        nv_ref[...] = nv

    spec = pl.BlockSpec((tr, C), lambda i: (i, 0))
    return pl.pallas_call(
        body, name=name, out_shape=[jax.ShapeDtypeStruct((R, C), F32)] * 3, grid=(R // tr,),
        in_specs=[spec] * 4, out_specs=[spec] * 3, compiler_params=_cp(("parallel",)))(w, g, m, v)


def all_gather_small(name, v):
    R, C = v.shape

    def body(v_ref, out_ref, send_sems, recv_sems, local_sem):
        x, y, c = _me()
        me = 4 * x + 2 * y + c
        mine = pltpu.make_async_copy(v_ref, out_ref.at[me], local_sem)
        mine.start()
        copies = []
        for d in range(1, N_DEV):
            dx, dy, dc = d // 4, (d // 2) % 2, d % 2
            px, py, pc = x ^ dx, y ^ dy, c ^ dc
            copies.append(pltpu.make_async_remote_copy(
                src_ref=v_ref, dst_ref=out_ref.at[me], send_sem=send_sems.at[d - 1], recv_sem=recv_sems.at[d - 1],
                device_id=(px, py, pc), device_id_type=MESH))
        for cp in copies:
            cp.start()
        for d in range(1, N_DEV):
            dx, dy, dc = d // 4, (d // 2) % 2, d % 2
            peer = 4 * (x ^ dx) + 2 * (y ^ dy) + (c ^ dc)
            pltpu.make_async_remote_copy(
                src_ref=v_ref, dst_ref=out_ref.at[peer], send_sem=send_sems.at[d - 1], recv_sem=recv_sems.at[d - 1],
                device_id=(x ^ dx, y ^ dy, c ^ dc), device_id_type=MESH).wait_recv()
        for cp in copies:
            cp.wait_send()
        mine.wait()

    return pl.pallas_call(
        body, name=name, out_shape=jax.ShapeDtypeStruct((N_DEV, R, C), F32),
        in_specs=[pl.BlockSpec(memory_space=pltpu.VMEM)], out_specs=pl.BlockSpec(memory_space=pltpu.VMEM),
        scratch_shapes=[pltpu.SemaphoreType.DMA((N_DEV - 1,)), pltpu.SemaphoreType.DMA((N_DEV - 1,)), pltpu.SemaphoreType.DMA],
        compiler_params=pltpu.CompilerParams(vmem_limit_bytes=VMEM_LIMIT))(v)


def all_gather_chips(name, shard):
    R, C = shard.shape
    half = R // 2
    assert R % 32 == 0

    def body(s_ref, out_ref, send_sems, recv_sems):
        x, y, c = _me()
        chips = [(1 - x, y), (x, 1 - y), (1 - x, 1 - y)]

        def rows(chip, hc):
            return out_ref.at[2 * chip[0] + chip[1], pl.ds(hc * half, half), :]

        first = [pltpu.make_async_remote_copy(
            src_ref=s_ref.at[pl.ds(c * half, half), :], dst_ref=rows((x, y), c), send_sem=send_sems.at[j],
            recv_sem=recv_sems.at[j], device_id=(*chip, c), device_id_type=MESH) for j, chip in enumerate(chips)]
        for cp in first:
            cp.start()
        passed = [pltpu.make_async_remote_copy(
            src_ref=rows(chip, c), dst_ref=rows(chip, c), send_sem=send_sems.at[3 + j], recv_sem=recv_sems.at[3 + j],
            device_id=(x, y, 1 - c), device_id_type=MESH) for j, chip in enumerate(chips)]
        for j, chip in enumerate(chips):
            pltpu.make_async_remote_copy(
                src_ref=rows(chip, c), dst_ref=rows(chip, c), send_sem=send_sems.at[j], recv_sem=recv_sems.at[j],
                device_id=(*chip, c), device_id_type=MESH).wait_recv()
            passed[j].start()
        for j, chip in enumerate(chips):
            pltpu.make_async_remote_copy(
                src_ref=rows(chip, 1 - c), dst_ref=rows(chip, 1 - c), send_sem=send_sems.at[3 + j], recv_sem=recv_sems.at[3 + j],
                device_id=(x, y, 1 - c), device_id_type=MESH).wait_recv()
        for cp in first + passed:
            cp.wait_send()

    out = pl.pallas_call(
        body, name=name, out_shape=jax.ShapeDtypeStruct((N_CHIPS, R, C), shard.dtype),
        in_specs=[pl.BlockSpec(memory_space=pl.ANY)], out_specs=pl.BlockSpec(memory_space=pl.ANY),
        scratch_shapes=[pltpu.SemaphoreType.DMA((6,)), pltpu.SemaphoreType.DMA((6,))],
        compiler_params=pltpu.CompilerParams(vmem_limit_bytes=VMEM_LIMIT))(shard)
    chip = 2 * lax.axis_index("x") + lax.axis_index("y")
    return lax.dynamic_update_index_in_dim(out, shard, chip, 0)


def sibling_swap(name, v):
    def body(v_ref, out_ref, send_sem, recv_sem):
        x, y, c = _me()
        cp = pltpu.make_async_remote_copy(src_ref=v_ref, dst_ref=out_ref, send_sem=send_sem, recv_sem=recv_sem,
                                          device_id=(x, y, 1 - c), device_id_type=MESH)
        cp.start()
        cp.wait()

    return pl.pallas_call(
        body, name=name, out_shape=jax.ShapeDtypeStruct(v.shape, v.dtype),
        in_specs=[pl.BlockSpec(memory_space=pl.ANY)], out_specs=pl.BlockSpec(memory_space=pl.ANY),
        scratch_shapes=[pltpu.SemaphoreType.DMA, pltpu.SemaphoreType.DMA],
        compiler_params=pltpu.CompilerParams(vmem_limit_bytes=VMEM_LIMIT))(v)


def _row_tile(rows, cap, mult=8):
    best = None
    for t in range(mult, min(rows, cap) + 1, mult):
        if rows % t == 0:
            best = t
    assert best is not None, rows
    return best


def add_arrays(name, arrs, out_dtype=F32):
    shape = arrs[0].shape
    C = shape[-1]
    flat = [a.reshape(-1, C) for a in arrs]
    R = flat[0].shape[0]
    narrow = out_dtype == BF16 or any(a.dtype == BF16 for a in arrs)
    tr = _row_tile(R, 2048 if len(arrs) <= 2 else 1024, 16 if narrow else 8)
    n = len(flat)

    def body(*refs):
        acc = refs[0][...].astype(F32)
        for r in refs[1:n]:
            acc = acc + r[...].astype(F32)
        refs[n][...] = acc.astype(out_dtype)

    spec = pl.BlockSpec((tr, C), lambda i: (i, 0))
    out = pl.pallas_call(
        body, name=name, out_shape=jax.ShapeDtypeStruct((R, C), out_dtype), grid=(R // tr,),
        in_specs=[spec] * n, out_specs=spec, compiler_params=_cp(("parallel",)))(*flat)
    return out.reshape(shape)


def reduce_scatter_chips(slabs):
    _, R, C = slabs.shape
    half = R // 2
    c = lax.axis_index("c")
    halves = slabs.reshape(N_CHIPS, 2, half, C)
    own = lax.dynamic_index_in_dim(halves, c, axis=1, keepdims=False)
    other = lax.dynamic_index_in_dim(halves, 1 - c, axis=1, keepdims=False)
    from_sibling = sibling_swap("rs_sibling_halves", other.astype(BF16))
    return add_arrays("rs_add_sibling", [own, from_sibling], out_dtype=BF16)


def reduce_scatter_finish(landed):
    c = lax.axis_index("c")
    mine = add_arrays("rs_add_chips", [landed[j] for j in range(N_CHIPS)])
    sib = sibling_swap("rs_sibling_result", mine)
    return jnp.concatenate([jnp.where(c == 0, mine, sib), jnp.where(c == 0, sib, mine)], axis=0)


def ada_mod_shard(cond_all, w_ada_shard, b_ada_shard):
    def body(c_ref, w_ref, b_ref, o_ref):
        o_ref[...] = _dot(_silu(c_ref[...]).astype(BF16), w_ref[...].astype(BF16)) + b_ref[...]

    return pl.pallas_call(body, name="ada_mod_shard", out_shape=jax.ShapeDtypeStruct((cond_all.shape[0], w_ada_shard.shape[1]), F32),
                          compiler_params=_cp())(cond_all, w_ada_shard, b_ada_shard)


def ada_bwd_shard(cond_all, dmod_all_shard, dmod_all, w_ada_shard, row_is_cctx):
    def body(c_ref, ds_ref, da_ref, w_ref, sel_ref, gw_ref, gb_ref, part_ref):
        sc = _silu(c_ref[...]).astype(BF16)
        gw_ref[...] = _dot_tn(sc, ds_ref[...].astype(BF16))
        gb_ref[...] = _colsum(da_ref[...])
        dc_tot = jnp.broadcast_to(_colsum(ds_ref[...] * sel_ref[...]), (8, ds_ref.shape[1]))
        part_ref[...] = _dot_nt(dc_tot.astype(BF16), w_ref[...].astype(BF16))

    return pl.pallas_call(
        body, name="ada_bwd_shard",
        out_shape=[jax.ShapeDtypeStruct(w_ada_shard.shape, F32), jax.ShapeDtypeStruct((1, dmod_all.shape[1]), F32),
                   jax.ShapeDtypeStruct((8, D), F32)],
        compiler_params=_cp())(cond_all, dmod_all_shard, dmod_all, w_ada_shard, row_is_cctx)


def sum_devices(name, gathered):
    def body(g_ref, o_ref):
        acc = g_ref[0]
        for d in range(1, N_DEV):
            acc = acc + g_ref[d]
        o_ref[...] = acc

    return pl.pallas_call(body, name=name, out_shape=jax.ShapeDtypeStruct(gathered.shape[1:], F32), compiler_params=_cp())(gathered)


def cctx_finish(gathered, c_ctx_row):
    def body(g_ref, c_ref, o_ref):
        acc = g_ref[0, 0:1, :]
        for k in range(1, N_CHIPS):
            acc = acc + g_ref[2 * k, 0:1, :]
        o_ref[...] = acc * _dsilu(c_ref[...])

    return pl.pallas_call(body, name="cctx_finish", out_shape=jax.ShapeDtypeStruct((1, D), F32), compiler_params=_cp())(gathered, c_ctx_row)


def _pack(parts, rows):
    flat = []
    for p in parts:
        p = p.reshape(-1)
        pad = (-p.shape[0]) % LANES
        flat.append(jnp.pad(p, (0, pad)) if pad else p)
    v = jnp.concatenate(flat)
    return jnp.pad(v, (0, rows * LANES - v.shape[0])).reshape(rows, LANES)


def _unpack(v, sizes):
    flat = v.reshape(-1)
    out, off = [], 0
    for n in sizes:
        out.append(flat[off:off + n])
        off += n + (-n) % LANES
    return out


W_SHARD_ROWS = 3456
SEG_ROWS = (0, 2320, 2576, 3088, 3344, 3408)


def kernel(x, c, ctx, c_ctx, w_ada, b_ada, norm_pre, norm_post, w_in, b_merge, pool_w, pool_scale, conv_w, conv_b, dt_bias, a_log, d_skip, ssd_norm, w_proj_pool, w_proj_ssd, w_out, loss_target, m_c_ctx, m_w_ada, m_b_ada, m_norm_pre, m_norm_post, m_w_in, m_b_merge, m_pool_w, m_pool_scale, m_conv_w, m_conv_b, m_dt_bias, m_a_log, m_d_skip, m_ssd_norm, m_w_proj_pool, m_w_proj_ssd, m_w_out, v_c_ctx, v_w_ada, v_b_ada, v_norm_pre, v_norm_post, v_w_in, v_b_merge, v_pool_w, v_pool_scale, v_conv_w, v_conv_b, v_dt_bias, v_a_log, v_d_skip, v_ssd_norm, v_w_proj_pool, v_w_proj_ssd, v_w_out):
    Bn, L, _ = x.shape
    Lc = ctx.shape[1]
    T, Tc = Bn * L, Bn * Lc
    assert Bn == 2
    ix, iy, ic = lax.axis_index("x"), lax.axis_index("y"), lax.axis_index("c")
    me = 4 * ix + 2 * iy + ic
    chip = 2 * ix + iy
    ada_cols = w_ada.shape[2]
    cw_cols = conv_w.shape[2]

    cond_own = jnp.pad(c, ((0, 8 - Bn), (0, 0))) + jnp.pad(c_ctx[None, :], ((Bn, 7 - Bn), (0, 0)))
    convw_own = jnp.pad(conv_w[0], ((0, 4), (0, D - cw_cols)))
    g1 = all_gather_small("gather_cond", jnp.concatenate([cond_own, convw_own], axis=0))
    cond_all = g1[:, 0:8].reshape(8 * N_DEV, D)
    conv_w_full = jnp.concatenate([g1[2 * k, 8:12, 0:cw_cols] for k in range(N_CHIPS)], axis=1)
    b_ada_shard = lax.dynamic_slice(b_ada, (0, chip * ada_cols), (1, ada_cols))
    g2 = all_gather_small("gather_mod", ada_mod_shard(cond_all, w_ada[0], b_ada_shard))
    mod_full = jnp.concatenate([g2[2 * k] for k in range(N_CHIPS)], axis=1)
    own = lax.dynamic_slice(mod_full, (8 * me, 0), (8, 3 * D))
    shift, scale, gate = (own[0:Bn, i * D:(i + 1) * D][:, None, :] for i in range(3))
    shift_c, scale_c = (jnp.broadcast_to(own[Bn:Bn + 1, i * D:(i + 1) * D][None], (Bn, 1, D)) for i in range(2))

    w_in_rows = IN_COLS // N_CHIPS
    shard_in = jnp.concatenate([w_in[0].T, jnp.zeros((16, D), F32)], axis=0).astype(BF16)
    shard_rest = jnp.concatenate([w_proj_pool[0], w_proj_ssd[0], w_out[0], pool_w[0].reshape(64, D)], axis=0).astype(BF16)
    w_inT = all_gather_chips("gather_w_in", shard_in)[:, 0:w_in_rows].reshape(IN_COLS, D)
    w_dt = jnp.pad(w_inT[9216:IN_COLS], ((0, LANES - 64), (0, 0)))
    seg_lo = (0, 1024, 2048, 4096, 6144, 8192, 8704)
    seg_hi = (1024, 2048, 4096, 6144, 8192, 8704, 9216)
    w_seg = [w_inT[lo:hi] for lo, hi in zip(seg_lo, seg_hi)] + [w_dt]

    hx = prenorm_fwd("prenorm_x", x, scale, shift, norm_pre)
    hc = prenorm_fwd("prenorm_ctx", ctx, scale_c, shift_c, norm_pre)
    hx2, hc2 = hx.reshape(T, D), hc.reshape(Tc, D)
    v = mm_nt("proj_v", hx2, w_inT[0:1024], F32).reshape(Bn, L, D)
    zp = mm_nt("proj_zpool", hx2, w_inT[1024:2048], F32).reshape(Bn, L, D)
    zs = mm_nt("proj_zssd", hx2, w_inT[2048:4096], F32).reshape(Bn, L, D_INNER)
    gp = mm_nt("proj_gate", hx2, w_inT[4096:6144], F32).reshape(Bn, L, 2 * D)
    xbc_raw, g_rest = mm_nt("proj_xbc", hx2, w_inT[6144:9216], F32, gather=shard_rest)
    xbc_raw = xbc_raw.reshape(Bn, L, CONV_DIM)
    w_pp = g_rest[:, 0:256].reshape(D, D)
    w_ps = g_rest[:, 256:768].reshape(D_INNER, D)
    w_o = g_rest[:, 768:1024].reshape(D, D)
    pool_full = g_rest[:, 1024:1088].reshape(N_CHIPS, 4, 64, POOL_GROUP).transpose(1, 0, 2, 3).reshape(D, POOL_GROUP)
    dt_raw = mm_nt("proj_dt", hx2, w_dt, F32)
    xbc_raw_c = mm_nt("proj_xbc_ctx", hc2, w_inT[6144:9216], F32).reshape(Bn, Lc, CONV_DIM)
    dt_raw_c = mm_nt("proj_dt_ctx", hc2, w_dt, F32)
    dtT = dt_raw[:, :64].reshape(Bn, L, 64).transpose(0, 2, 1)
    dtT_c = dt_raw_c[:, :64].reshape(Bn, Lc, 64).transpose(0, 2, 1)
    bias3 = dt_bias.reshape(2 * N_BC, HPG, 1)
    alog3 = a_log.reshape(2 * N_BC, HPG, 1)

    xbc = conv_fwd("conv_x", xbc_raw, conv_w_full, conv_b)
    xbc_c = conv_fwd("conv_ctx", xbc_raw_c, conv_w_full, conv_b)
    zero_state = jnp.zeros((Bn, N_BC, D_STATE, GW), F32)
    dskip_lanes = jnp.repeat(d_skip[0], HEAD_DIM)[None, :]
    ys, hs_x, hs_c = [], [], []
    for d in range(2):
        hsc, hfc = ssd_fwd(f"ssd_fwd_ctx{d}", dtT_c, bias3, alog3, xbc_c, zero_state, d, False)
        y, hsx, _ = ssd_fwd(f"ssd_fwd_x{d}", dtT, bias3, alog3, xbc, hfc, d, True,
                            y_add=(ys[0], dskip_lanes) if d == 1 else None)
        ys.append(y)
        hs_x.append(hsx)
        hs_c.append(hsc)

    dgs = [pool_diff(f"pool_diff{g}", v, g * POOL_GROUP, g, False) for g in range(4)]
    y_pool = pool_mix_fwd("pool_mix", dgs, zp, pool_full, pool_scale)
    y_ssd = gated_norm_fwd("gated_norm", ys[1], zs, ssd_norm)
    merged, p1, p2, dout, g_res, dgate, g_norm_post, loss_part = merge_fwd(
        "merge_fwd", y_pool, y_ssd, gp, x, loss_target, gate, b_merge, norm_post, w_pp, w_ps, w_o)

    dp1, dp2, dgp, dyp, dys, g_b_merge = merge_bwd("merge_bwd", dout, gp, p1, p2, b_merge, w_pp, w_ps, w_o)
    gw_o = mm_tn("gw_out", merged.reshape(T, D), dout.reshape(T, D))
    gw_pp = mm_tn("gw_proj_pool", y_pool.reshape(T, D), dp1.reshape(T, D))
    gw_ps = mm_tn("gw_proj_ssd", y_ssd.reshape(T, D_INNER), dp2.reshape(T, D))

    *dds, dzp, g_pool, g_pool_scale = pool_mix_bwd("pool_mix_bwd", dgs, zp, dyp, pool_full, pool_scale)
    dvs = [pool_diff(f"pool_diff_t{g}", dds[g], 0, g, True) for g in range(4)]

    head_sel = (jnp.arange(D_INNER)[:, None] // HEAD_DIM == jnp.arange(LANES)[None, :]).astype(BF16)
    dy, dzs, g_ssd_norm, g_dskip = gated_norm_bwd(
        "gated_norm_bwd", ys[1], (xbc, D_INNER), zs, dys, ssd_norm, head_sel)

    dxs, dbm, dcm, ddt, dxs_c, dbm_c, ddt_c = [], [], [], [], [], [], []
    g_bias = jnp.zeros((2, N_BC, HPG, 1), F32)
    g_alog = jnp.zeros((2, N_BC, HPG, 1), F32)
    for d in range(2):
        a, b_, c_, t_, gb, ga, dh0 = ssd_bwd(f"ssd_bwd_x{d}", dtT, bias3, alog3, xbc, hs_x[d], dy, zero_state, d,
                                             dx_add=(dxs[0], dskip_lanes) if d == 1 else None)
        dxs.append(a), dbm.append(b_), dcm.append(c_), ddt.append(t_)
        ac, bc, _, tc, gbc, gac, _ = ssd_bwd(f"ssd_bwd_ctx{d}", dtT_c, bias3, alog3, xbc_c, hs_c[d], None, dh0, d)
        dxs_c.append(ac), dbm_c.append(bc), ddt_c.append(tc)
        g_bias = g_bias.at[d].set(jnp.sum(gb, axis=0) + jnp.sum(gbc, axis=0))
        g_alog = g_alog.at[d].set(jnp.sum(ga, axis=0) + jnp.sum(gac, axis=0))

    dxr_xs, gcw_xs, gcb_xs = conv_bwd("conv_bwd_xs", xbc_raw, [dxs[1]], conv_w_full, conv_b, 0, D_INNER)
    dxr_b, gcw_b, gcb_b = conv_bwd("conv_bwd_b", xbc_raw, dbm, conv_w_full, conv_b, D_INNER, N_BC * D_STATE)
    dxr_c, gcw_c, gcb_c = conv_bwd("conv_bwd_c", xbc_raw, dcm, conv_w_full, conv_b, D_INNER + N_BC * D_STATE, N_BC * D_STATE)
    dxr_xs_c, gcw_xs_c, gcb_xs_c = conv_bwd("conv_bwd_xs_ctx", xbc_raw_c, dxs_c, conv_w_full, conv_b, 0, D_INNER)
    dxr_b_c, gcw_b_c, gcb_b_c = conv_bwd("conv_bwd_b_ctx", xbc_raw_c, dbm_c, conv_w_full, conv_b, D_INNER, N_BC * D_STATE)
    g_conv_w = jnp.concatenate([gcw_xs + gcw_xs_c, gcw_b + gcw_b_c, gcw_c], axis=1)
    g_conv_b = jnp.concatenate([gcb_xs + gcb_xs_c, gcb_b + gcb_b_c, gcb_c], axis=1)

    def dt_cols(parts, n_tok):
        t = jnp.concatenate(parts, axis=1).transpose(0, 2, 1).reshape(n_tok, 2 * N_HEADS)
        return jnp.pad(t, ((0, 0), (0, LANES - 2 * N_HEADS))).astype(BF16)

    ddt2, ddt2_c = dt_cols(ddt, T), dt_cols(ddt_c, Tc)
    segs = [jnp.concatenate(dvs, axis=-1).reshape(T, D), dzp.reshape(T, D), dzs.reshape(T, D_INNER), dgp.reshape(T, 2 * D),
            dxr_xs.reshape(T, D_INNER), dxr_b.reshape(T, N_BC * D_STATE), dxr_c.reshape(T, N_BC * D_STATE), ddt2]
    segs_c = {4: dxr_xs_c.reshape(Tc, D_INNER), 5: dxr_b_c.reshape(Tc, N_BC * D_STATE), 7: ddt2_c}
    gw_rows = []
    for i, seg in enumerate(segs):
        init = mm_tn(f"gw_in_ctx{i}", segs_c[i], hc2) if i in segs_c else None
        gw_rows.append(mm_tn(f"gw_in{i}", seg, hx2, init=init))
    gw_rows[-1] = gw_rows[-1][0:2 * N_HEADS]
    gw_inT = jnp.concatenate(gw_rows, axis=0)

    pool_slab = g_pool.reshape(4, N_CHIPS, 64, POOL_GROUP).transpose(1, 0, 2, 3).reshape(N_CHIPS, 64, D)
    slabs = jnp.concatenate([gw_inT.reshape(N_CHIPS, 2320, D), gw_pp.reshape(N_CHIPS, 256, D), gw_ps.reshape(N_CHIPS, 512, D),
                             gw_o.reshape(N_CHIPS, 256, D), pool_slab, jnp.zeros((N_CHIPS, W_SHARD_ROWS - SEG_ROWS[-1], D), F32)], axis=1)
    chip_part = reduce_scatter_chips(slabs)
    d_hx, landed = mm_nn_multi("d_hx", list(zip(segs, w_seg)), F32, tm=1024, tk=256, exchange=chip_part)
    d_hx = d_hx.reshape(Bn, L, D)
    gsh = reduce_scatter_finish(landed)
    d_hc = mm_nn_multi("d_hc", [(segs_c[i], w_seg[i]) for i in sorted(segs_c)], F32).reshape(Bn, Lc, D)

    grad_x, dscale, dshift, g_npre_x = prenorm_bwd("prenorm_bwd_x", x, d_hx, scale, norm_pre, g_res=g_res)
    _, dscale_c, dshift_c, g_npre_c = prenorm_bwd("prenorm_bwd_ctx", ctx, d_hc, scale_c, norm_pre)

    dmod_x = jnp.concatenate([dshift[:, 0], dscale[:, 0], dgate[:, 0]], axis=1)
    dmod_c = jnp.concatenate([jnp.sum(dshift_c[:, 0], axis=0, keepdims=True), jnp.sum(dscale_c[:, 0], axis=0, keepdims=True),
                              jnp.zeros((1, D), F32)], axis=1)
    dmod_own = jnp.pad(dmod_x, ((0, 8 - Bn), (0, 0))) + jnp.pad(dmod_c, ((Bn, 7 - Bn), (0, 0)))
    dmod_all = all_gather_small("gather_dmod", dmod_own).reshape(8 * N_DEV, 3 * D)
    row_is_cctx = (jnp.arange(8 * N_DEV) % 8 == Bn).astype(F32)[:, None]
    g_w_ada, g_b_ada, cpart = ada_bwd_shard(
        cond_all, lax.dynamic_slice(dmod_all, (0, chip * ada_cols), (8 * N_DEV, ada_cols)), dmod_all, w_ada[0], row_is_cctx)
    g_c_ctx = cctx_finish(all_gather_small("gather_cctx", cpart), c_ctx[None, :])

    small_sizes = (D, D, 2 * D, D, CONV_DIM, 2 * N_HEADS, 2 * N_HEADS, N_HEADS, D_INNER, 4 * CONV_DIM, 1)
    pk = _pack([g_npre_x + g_npre_c, g_norm_post, g_b_merge, g_pool_scale, g_conv_b, g_bias, g_alog, g_dskip[0, 0:N_HEADS],
                g_ssd_norm, g_conv_w, loss_part[0, 0:1]], 184)
    small = sum_devices("sum_small", all_gather_small("gather_small", pk))
    (g_norm_pre, g_norm_post_t, g_b_merge_t, g_pool_scale_t, g_conv_b_t, g_dt_bias, g_a_log, g_d_skip, g_ssd_norm_t,
     g_conv_w_t, loss) = _unpack(small, small_sizes)
    g_conv_w_shard = lax.dynamic_slice(g_conv_w_t.reshape(4, CONV_DIM), (0, chip * cw_cols), (4, cw_cols))

    g_w_in = gsh[SEG_ROWS[0]:SEG_ROWS[1]].T
    g_w_pp, g_w_ps, g_w_o = (gsh[SEG_ROWS[i]:SEG_ROWS[i + 1]] for i in (1, 2, 3))
    g_pool_w = gsh[SEG_ROWS[4]:SEG_ROWS[5]].reshape(256, POOL_GROUP)

    grads = {
        "c_ctx": g_c_ctx.reshape(c_ctx.shape), "w_ada": g_w_ada[None], "b_ada": g_b_ada, "norm_pre": g_norm_pre[None],
        "norm_post": g_norm_post_t[None], "w_in": g_w_in[None], "b_merge": g_b_merge_t[None],
        "pool_w": g_pool_w.reshape(pool_w.shape), "pool_scale": g_pool_scale_t[None], "conv_w": g_conv_w_shard[None],
        "conv_b": g_conv_b_t[None], "dt_bias": g_dt_bias.reshape(dt_bias.shape), "a_log": g_a_log.reshape(a_log.shape),
        "d_skip": g_d_skip[None], "ssd_norm": g_ssd_norm_t[None], "w_proj_pool": g_w_pp[None], "w_proj_ssd": g_w_ps[None],
        "w_out": g_w_o[None]}
    weights = dict(c_ctx=c_ctx, w_ada=w_ada, b_ada=b_ada, norm_pre=norm_pre, norm_post=norm_post, w_in=w_in, b_merge=b_merge,
                   pool_w=pool_w, pool_scale=pool_scale, conv_w=conv_w, conv_b=conv_b, dt_bias=dt_bias, a_log=a_log,
                   d_skip=d_skip, ssd_norm=ssd_norm, w_proj_pool=w_proj_pool, w_proj_ssd=w_proj_ssd, w_out=w_out)
    m_in = dict(c_ctx=m_c_ctx, w_ada=m_w_ada, b_ada=m_b_ada, norm_pre=m_norm_pre, norm_post=m_norm_post, w_in=m_w_in,
                b_merge=m_b_merge, pool_w=m_pool_w, pool_scale=m_pool_scale, conv_w=m_conv_w, conv_b=m_conv_b,
                dt_bias=m_dt_bias, a_log=m_a_log, d_skip=m_d_skip, ssd_norm=m_ssd_norm, w_proj_pool=m_w_proj_pool,
                w_proj_ssd=m_w_proj_ssd, w_out=m_w_out)
    v_in = dict(c_ctx=v_c_ctx, w_ada=v_w_ada, b_ada=v_b_ada, norm_pre=v_norm_pre, norm_post=v_norm_post, w_in=v_w_in,
                b_merge=v_b_merge, pool_w=v_pool_w, pool_scale=v_pool_scale, conv_w=v_conv_w, conv_b=v_conv_b,
                dt_bias=v_dt_bias, a_log=v_a_log, d_skip=v_d_skip, ssd_norm=v_ssd_norm, w_proj_pool=v_w_proj_pool,
                w_proj_ssd=v_w_proj_ssd, w_out=v_w_out)
    names = list(weights)
    big = ("w_ada", "w_in", "pool_w", "w_proj_pool", "w_proj_ssd", "w_out")
    small_names = [n for n in names if n not in big]
    delta, new_m, new_v = {}, {}, {}
    for n in big:
        shape2 = (-1, weights[n].shape[-1])
        d_, m_, v_ = adamw(f"adamw_{n}", weights[n].reshape(shape2), grads[n].reshape(shape2), m_in[n].reshape(shape2),
                           v_in[n].reshape(shape2), tr=128)
        delta[n], new_m[n], new_v[n] = (t.reshape(weights[n].shape) for t in (d_, m_, v_))
    sizes = [weights[n].size for n in small_names]
    packed = [_pack([src[n] for n in small_names], 144) for src in (weights, grads, m_in, v_in)]
    outs = adamw("adamw_small", *packed, tr=144)
    for res, store in zip(outs, (delta, new_m, new_v)):
        for n, piece in zip(small_names, _unpack(res, sizes)):
            store[n] = piece.reshape(weights[n].shape)

    return (loss.reshape(()), grad_x, *[grads[n] for n in names], *[delta[n] for n in names],
            *[new_m[n] for n in names], *[new_v[n] for n in names])
```

```python
import jax
import jax.numpy as jnp
from jax import lax
from jax.experimental import pallas as pl
from jax.experimental.pallas import tpu as pltpu

F32 = jnp.float32
BF16 = jnp.bfloat16
MESH = pl.DeviceIdType.MESH

D = 1024
GRID_W = 64
NORM_EPS = 1e-6
POOL_WINDOWS = (2, 4, 8, 16)
POOL_GROUP = 256
D_INNER = 2048
HEAD_DIM = 64
N_HEADS = 32
D_STATE = 128
N_BC = 4
HPG = N_HEADS // N_BC
GW = HPG * HEAD_DIM
CONV_DIM = 3072
CHUNK = 128
IN_COLS = 9280
N_CHIPS = 4
N_DEV = 8

ADAM_LR = 0.001
ADAM_B1 = 0.9
ADAM_B2 = 0.999
ADAM_EPS = 1e-08
ADAM_WD = 0.01
ADAM_STEP = 10

V7X_VMEM_BYTES = 64 * 1024 * 1024
VMEM_LIMIT = V7X_VMEM_BYTES * 3 // 4
LANES = 128


def _cp(sem=None):
    return pltpu.CompilerParams(dimension_semantics=sem, vmem_limit_bytes=VMEM_LIMIT)


def _dot(a, b):
    return jnp.dot(a, b, preferred_element_type=F32)


def _dot_nt(a, b):
    return lax.dot_general(a, b, (((1,), (1,)), ((), ())), preferred_element_type=F32)


def _dot_tn(a, b):
    return lax.dot_general(a, b, (((0,), (0,)), ((), ())), preferred_element_type=F32)


def _split3(x):
    hi = x.astype(BF16)
    r1 = x - hi.astype(F32)
    mid = r1.astype(BF16)
    lo = (r1 - mid.astype(F32)).astype(BF16)
    return hi, mid, lo


def _dot_exact01(v, sel):
    hi, mid, lo = _split3(v)
    return _dot(hi, sel) + _dot(mid, sel) + _dot(lo, sel)


def _sigmoid(x):
    return jax.nn.sigmoid(x)


def _silu(x):
    return x * _sigmoid(x)


def _dsilu(x):
    s = _sigmoid(x)
    return s * (1.0 + x * (1.0 - s))


def _softplus(x):
    return jnp.maximum(x, 0.0) + jnp.log(1.0 + jnp.exp(-jnp.abs(x)))


def _me():
    return lax.axis_index("x"), lax.axis_index("y"), lax.axis_index("c")


def mm_nt(name, a, b, out_dtype, tm=1024, tn=512, gather=None):
    M, K = a.shape
    N = b.shape[0]
    tm, tn = min(tm, M), min(tn, N)
    assert M % tm == 0 and N % tn == 0, (M, N, tm, tn)
    n_i, n_j = M // tm, N // tn
    has_g = gather is not None
    if has_g:
        half = gather.shape[0] // 2
        assert gather.shape[0] % 32 == 0 and n_i * n_j >= 4

    def body(*refs):
        a_ref, b_ref = refs[0], refs[1]
        if has_g:
            s_ref, o_ref, g_ref, send_sems, recv_sems = refs[2:]
            x, y, c = _me()
            chips = [(1 - x, y), (x, 1 - y), (1 - x, 1 - y)]
            step = pl.program_id(0) * n_j + pl.program_id(1)

            def rows(chip, hc):
                return g_ref.at[2 * chip[0] + chip[1], pl.ds(hc * half, half), :]

            def first(j, chip):
                return pltpu.make_async_remote_copy(
                    src_ref=s_ref.at[pl.ds(c * half, half), :], dst_ref=rows((x, y), c), send_sem=send_sems.at[j],
                    recv_sem=recv_sems.at[j], device_id=(*chip, c), device_id_type=MESH)

            def landed(j, chip, hc):
                return pltpu.make_async_remote_copy(
                    src_ref=rows(chip, hc), dst_ref=rows(chip, hc), send_sem=send_sems.at[j], recv_sem=recv_sems.at[j],
                    device_id=(x, y, 1 - c), device_id_type=MESH)

            @pl.when(step == 0)
            def _():
                for j, chip in enumerate(chips):
                    first(j, chip).start()

            @pl.when(step == (3 * n_i * n_j) // 4)
            def _():
                for j, chip in enumerate(chips):
                    landed(j, chip, c).wait_recv()
                    landed(3 + j, chip, c).start()
        else:
            o_ref = refs[2]

        o_ref[...] = _dot_nt(a_ref[...], b_ref[...]).astype(o_ref.dtype)

        if has_g:
            @pl.when(step == n_i * n_j - 1)
            def _():
                for j, chip in enumerate(chips):
                    landed(3 + j, chip, 1 - c).wait_recv()
                for j, chip in enumerate(chips):
                    first(j, chip).wait_send()
                    landed(3 + j, chip, c).wait_send()

    in_specs = [pl.BlockSpec((tm, K), lambda i, j: (i, 0)), pl.BlockSpec((tn, K), lambda i, j: (j, 0))]
    out_shape = jax.ShapeDtypeStruct((M, N), out_dtype)
    out_specs = pl.BlockSpec((tm, tn), lambda i, j: (i, j))
    if not has_g:
        return pl.pallas_call(body, name=name, out_shape=out_shape, grid=(n_i, n_j), in_specs=in_specs, out_specs=out_specs,
                              compiler_params=_cp(("parallel", "arbitrary")))(a, b)
    out, g = pl.pallas_call(
        body, name=name, out_shape=[out_shape, jax.ShapeDtypeStruct((N_CHIPS, *gather.shape), gather.dtype)], grid=(n_i, n_j),
        in_specs=in_specs + [pl.BlockSpec(memory_space=pl.ANY)], out_specs=[out_specs, pl.BlockSpec(memory_space=pl.ANY)],
        scratch_shapes=[pltpu.SemaphoreType.DMA((6,)), pltpu.SemaphoreType.DMA((6,))],
        compiler_params=_cp(("arbitrary", "arbitrary")))(a, b, gather)
    chip = 2 * lax.axis_index("x") + lax.axis_index("y")
    return out, lax.dynamic_update_index_in_dim(g, gather, chip, 0)


def mm_tn(name, a, b, init=None, tm=1024, tn=1024, tk=512):
    T, M = a.shape
    N = b.shape[1]
    tm, tn, tk = min(tm, M), min(tn, N), min(tk, T)
    assert M % tm == 0 and N % tn == 0 and T % tk == 0, (M, N, T)
    has_init = init is not None

    def body(*refs):
        if has_init:
            a_ref, b_ref, i_ref, o_ref = refs
        else:
            a_ref, b_ref, o_ref = refs
        k = pl.program_id(2)

        @pl.when(k == 0)
        def _():
            o_ref[...] = i_ref[...] if has_init else jnp.zeros(o_ref.shape, F32)

        o_ref[...] += _dot_tn(a_ref[...], b_ref[...])

    in_specs = [pl.BlockSpec((tk, tm), lambda i, j, k: (k, i)), pl.BlockSpec((tk, tn), lambda i, j, k: (k, j))]
    args = [a, b]
    if has_init:
        in_specs.append(pl.BlockSpec((tm, tn), lambda i, j, k: (i, j)))
        args.append(init)
    return pl.pallas_call(
        body, name=name, out_shape=jax.ShapeDtypeStruct((M, N), F32), grid=(M // tm, N // tn, T // tk),
        in_specs=in_specs, out_specs=pl.BlockSpec((tm, tn), lambda i, j, k: (i, j)),
        compiler_params=_cp(("parallel", "parallel", "arbitrary")))(*args)


def mm_nn_multi(name, pairs, out_dtype, tm=512, tk=512, exchange=None):
    M = pairs[0][0].shape[0]
    N = pairs[0][1].shape[1]
    tm = min(tm, M)
    assert M % tm == 0
    plan = []
    step = 0
    for a, b in pairs:
        K = a.shape[1]
        t = min(tk, K)
        assert K % t == 0 and b.shape == (K, N)
        plan.append((t, step, K // t))
        step += K // t
    nsteps = step
    npairs = len(pairs)

    n_i = M // tm
    has_x = exchange is not None

    def body(*refs):
        if has_x:
            p_ref, o_ref, land_ref, acc, send_sems, recv_sems = refs[2 * npairs:]
        else:
            o_ref, acc = refs[2 * npairs:]
        i, k = pl.program_id(0), pl.program_id(1)

        if has_x:
            x, y, c = _me()
            me_chip = 2 * x + y
            chips = [(1 - x, y), (x, 1 - y), (1 - x, 1 - y)]

            def copy(j, src_chip, dst_chip, to):
                return pltpu.make_async_remote_copy(
                    src_ref=p_ref.at[src_chip], dst_ref=land_ref.at[dst_chip], send_sem=send_sems.at[j], recv_sem=recv_sems.at[j],
                    device_id=(*to, c), device_id_type=MESH)

            @pl.when((i == 0) & (k == 0))
            def _():
                for j, chip in enumerate(chips):
                    copy(j, 2 * chip[0] + chip[1], me_chip, chip).start()

        @pl.when(k == 0)
        def _():
            acc[...] = jnp.zeros(acc.shape, F32)

        for p, (_, first, n) in enumerate(plan):
            @pl.when((k >= first) & (k < first + n))
            def _(p=p):
                acc[...] += _dot(refs[2 * p][...], refs[2 * p + 1][...])

        @pl.when(k == nsteps - 1)
        def _():
            o_ref[...] = acc[...].astype(o_ref.dtype)

        if has_x:
            @pl.when((i == n_i - 1) & (k == nsteps - 1))
            def _():
                for j, chip in enumerate(chips):
                    copy(j, me_chip, 2 * chip[0] + chip[1], chip).wait_recv()
                for j, chip in enumerate(chips):
                    copy(j, 2 * chip[0] + chip[1], me_chip, chip).wait_send()

    in_specs, args = [], []
    for (a, b), (t, first, n) in zip(pairs, plan):
        in_specs.append(pl.BlockSpec((tm, t), lambda i, k, first=first, n=n: (i, jnp.clip(k - first, 0, n - 1))))
        in_specs.append(pl.BlockSpec((t, N), lambda i, k, first=first, n=n: (jnp.clip(k - first, 0, n - 1), 0)))
        args += [a, b]
    out_shape = jax.ShapeDtypeStruct((M, N), out_dtype)
    out_specs = pl.BlockSpec((tm, N), lambda i, k: (i, 0))
    scratch = [pltpu.VMEM((tm, N), F32)]
    if has_x:
        in_specs.append(pl.BlockSpec(memory_space=pl.ANY))
        args.append(exchange)
        out_shape = [out_shape, jax.ShapeDtypeStruct(exchange.shape, exchange.dtype)]
        out_specs = [out_specs, pl.BlockSpec(memory_space=pl.ANY)]
        scratch += [pltpu.SemaphoreType.DMA((3,)), pltpu.SemaphoreType.DMA((3,))]
    res = pl.pallas_call(
        body, name=name, out_shape=out_shape, grid=(n_i, nsteps), in_specs=in_specs, out_specs=out_specs,
        scratch_shapes=scratch, compiler_params=_cp(("arbitrary", "arbitrary")))(*args)
    if not has_x:
        return res
    out, landed = res
    chip = 2 * lax.axis_index("x") + lax.axis_index("y")
    own = lax.dynamic_index_in_dim(exchange, chip, 0, keepdims=True)
    return out, lax.dynamic_update_slice_in_dim(landed, own, chip, 0)


def tok_call(name, body, tiled, perb, glob, out_tiled, out_perb, out_glob, tm=256):
    widths = [t[1] if isinstance(t, tuple) else t.shape[2] for t in tiled]
    tiled = [t[0] if isinstance(t, tuple) else t for t in tiled]
    Bn, L = tiled[0].shape[:2]
    tm = min(tm, L)
    assert L % tm == 0
    n_t, n_p, n_g = len(tiled), len(perb), len(glob)
    o_t, o_p, o_g = len(out_tiled), len(out_perb), len(out_glob)
    n_in = n_t + n_p + n_g

    def kern(*refs):
        ins, outs = refs[:n_in], refs[n_in:]
        b, j = pl.program_id(0), pl.program_id(1)
        vals = [r[0] for r in ins[:n_t + n_p]] + [r[...] for r in ins[n_t + n_p:]]
        res = body(*vals)
        if not isinstance(res, (tuple, list)):
            res = (res,)
        assert len(res) == o_t + o_p + o_g, (name, len(res))
        for r, v in zip(outs[:o_t], res[:o_t]):
            r[0] = v.astype(r.dtype)

        def accum(r, v, first, lead):
            @pl.when(first)
            def _():
                r[...] = jnp.zeros(r.shape, F32)
            if lead:
                r[0] += v
            else:
                r[...] += v

        for r, v in zip(outs[o_t:o_t + o_p], res[o_t:o_t + o_p]):
            accum(r, v, j == 0, True)
        for r, v in zip(outs[o_t + o_p:], res[o_t + o_p:]):
            accum(r, v, (j == 0) & (b == 0), False)

    in_specs = ([pl.BlockSpec((1, tm, w), lambda b, j: (b, j, 0)) for w in widths]
                + [pl.BlockSpec((1, 1, a.shape[2]), lambda b, j: (b, 0, 0)) for a in perb]
                + [pl.BlockSpec(a.shape, lambda b, j: (0, 0), pipeline_mode=pl.Buffered(1)) for a in glob])
    out_shape = ([jax.ShapeDtypeStruct((Bn, L, w), dt) for w, dt in out_tiled]
                 + [jax.ShapeDtypeStruct((Bn, 1, w), F32) for w in out_perb]
                 + [jax.ShapeDtypeStruct(s, F32) for s in out_glob])
    out_specs = ([pl.BlockSpec((1, tm, w), lambda b, j: (b, j, 0)) for w, _ in out_tiled]
                 + [pl.BlockSpec((1, 1, w), lambda b, j: (b, 0, 0)) for w in out_perb]
                 + [pl.BlockSpec(s, lambda b, j: (0, 0)) for s in out_glob])
    return pl.pallas_call(
        kern, name=name, out_shape=out_shape, grid=(Bn, L // tm), in_specs=in_specs, out_specs=out_specs,
        compiler_params=_cp(("arbitrary", "arbitrary")))(*tiled, *perb, *glob)


def slab_call(name, body, slabs, colparams, out_slabs, out_colred, wc=LANES):
    Bn, L = slabs[0][0].shape[:2]
    w_out = out_slabs[0][0]
    assert w_out % wc == 0 and all(off % wc == 0 for _, off in slabs + colparams)
    n_col = w_out // wc
    n_s, n_c = len(slabs), len(colparams)
    o_s = len(out_slabs)

    def kern(*refs):
        ins, outs = refs[:n_s + n_c], refs[n_s + n_c:]
        b = pl.program_id(1)
        vals = [r[0] for r in ins[:n_s]] + [r[...] for r in ins[n_s:]]
        res = body(*vals)
        if not isinstance(res, (tuple, list)):
            res = (res,)
        assert len(res) == o_s + len(out_colred), name
        for r, v in zip(outs[:o_s], res[:o_s]):
            r[0] = v.astype(r.dtype)

        def accum(r, v):
            @pl.when(b == 0)
            def _():
                r[...] = jnp.zeros(r.shape, F32)
            r[...] += v

        for r, v in zip(outs[o_s:], res[o_s:]):
            accum(r, v)

    in_specs = ([pl.BlockSpec((1, L, wc), lambda j, b, o=off // wc: (b, 0, o + j)) for _, off in slabs]
                + [pl.BlockSpec((a.shape[0], wc), lambda j, b, o=off // wc: (0, o + j)) for a, off in colparams])
    out_shape = ([jax.ShapeDtypeStruct((Bn, L, w), dt) for w, dt in out_slabs]
                 + [jax.ShapeDtypeStruct((r, w_out), F32) for r in out_colred])
    out_specs = ([pl.BlockSpec((1, L, wc), lambda j, b: (b, 0, j)) for _ in out_slabs]
                 + [pl.BlockSpec((r, wc), lambda j, b: (0, j)) for r in out_colred])
    return pl.pallas_call(
        kern, name=name, out_shape=out_shape, grid=(n_col, Bn), in_specs=in_specs, out_specs=out_specs,
        compiler_params=_cp(("arbitrary", "arbitrary")))(*[a for a, _ in slabs], *[a for a, _ in colparams])


def _rms_r(x):
    return lax.rsqrt(jnp.mean(x * x, axis=-1, keepdims=True) + NORM_EPS)


def _rms_bwd(dxh, x, r):
    return r * (dxh - x * (r * r) * jnp.mean(dxh * x, axis=-1, keepdims=True))


def _colsum(v):
    return jnp.sum(v, axis=0, keepdims=True)


def _stack_rows(rows):
    n, w = len(rows), rows[0].shape[1]
    sub = lax.broadcasted_iota(jnp.int32, (n, w), 0)
    acc = jnp.zeros((n, w), F32)
    for r, row in enumerate(rows):
        acc = acc + jnp.where(sub == r, jnp.broadcast_to(row, (n, w)), 0.0)
    return acc


def prenorm_fwd(name, x, scale, shift, w_pre):
    def body(x, scale, shift, w):
        n = x * _rms_r(x) * w
        return n * (1.0 + scale) + shift

    return tok_call(name, body, [x], [scale, shift], [w_pre], [(D, BF16)], [], [])[0]


def prenorm_bwd(name, x, dhx, scale, w_pre, g_res=None):
    has_res = g_res is not None

    def body(*v):
        if has_res:
            x, dhx, g, scale, w = v
        else:
            x, dhx, scale, w = v
        r = _rms_r(x)
        xr = x * r
        n = xr * w
        dn = dhx * (1.0 + scale)
        dx = _rms_bwd(dn * w, x, r)
        if has_res:
            dx = dx + g
        return dx, _colsum(dhx * n), _colsum(dhx), _colsum(dn * xr)

    tiled = [x, dhx] + ([g_res] if has_res else [])
    return tok_call(name, body, tiled, [scale], [w_pre], [(D, F32)], [D, D], [(1, D)])


def _shift_rows(x, o, tok, L):
    if o == 0:
        return x
    rolled = pltpu.roll(x, (-o) % L, 0)
    return jnp.where((tok + o >= 0) & (tok + o < L), rolled, 0.0)


def conv_fwd(name, xbc_raw, conv_w, conv_b):
    L = xbc_raw.shape[1]

    def body(x, w, b):
        tok = lax.broadcasted_iota(jnp.int32, x.shape, 0)
        pre = b
        for k in range(4):
            pre = pre + _shift_rows(x, k - 2, tok, L) * w[k:k + 1]
        return _silu(pre)

    return slab_call(name, body, [(xbc_raw, 0)], [(conv_w, 0), (conv_b, 0)], [(CONV_DIM, F32)], [])[0]


CONV_ROWS = 128
CONV_HALO = 8


def _halo_chunks(L, load, work):
    ch, hl = CONV_ROWS, CONV_HALO
    n = L // ch
    assert L % ch == 0
    if n == 1:
        z = jnp.zeros_like(load(0, hl))
        work(0, jnp.concatenate([z, load(0, ch), z], axis=0))
        return
    z = jnp.zeros_like(load(0, hl))
    work(0, jnp.concatenate([z, load(0, ch + hl)], axis=0))

    def step(i, carry):
        start = pl.multiple_of(i * ch, ch)
        work(start, load(pl.multiple_of(start - hl, hl), ch + 2 * hl))
        return carry

    lax.fori_loop(1, n - 1, step, 0)
    work(L - ch, jnp.concatenate([load(L - ch - hl, ch + hl), z], axis=0))


def _rows_at(xh, o):
    return xh if o == 0 else pltpu.roll(xh, (-o) % xh.shape[0], 0)


def conv_bwd(name, xbc_raw, dparts, conv_w, conv_b, col0, width, scaled=None, wc=LANES):
    Bn, L, _ = xbc_raw.shape
    n_d = len(dparts)
    has_s = scaled is not None
    mid = slice(CONV_HALO, CONV_HALO + CONV_ROWS)
    c0 = col0 // wc

    def kern(*refs):
        x_ref, d_refs = refs[0], refs[1:1 + n_d]
        pos = 1 + n_d
        if has_s:
            s_ref, pos = refs[pos], pos + 1
        w_ref, b_ref = refs[pos], refs[pos + 1]
        pos += 2
        if has_s:
            scale = refs[pos][...]
            pos += 1
        dx_ref, dw_ref, db_ref = refs[pos:pos + 3]
        acc = refs[pos + 3]
        w, b = w_ref[...], b_ref[...]
        acc[...] = jnp.zeros(acc.shape, F32)

        def load(s, n):
            dy = d_refs[0][0, pl.ds(s, n), :]
            for r in d_refs[1:]:
                dy = dy + r[0, pl.ds(s, n), :]
            if has_s:
                dy = dy + s_ref[0, pl.ds(s, n), :] * scale
            return jnp.concatenate([x_ref[0, pl.ds(s, n), :], dy], axis=1)

        def work(start, both):
            xh, dyh = both[:, 0:wc], both[:, wc:]
            taps = [_rows_at(xh, k - 2) for k in range(4)]
            pre = b
            for k in range(4):
                pre = pre + taps[k] * w[k:k + 1]
            dpre = dyh * _dsilu(pre)
            dx = dpre * w[2:3]
            for k in (0, 1, 3):
                dx = dx + _rows_at(dpre, 2 - k) * w[k:k + 1]
            dx_ref[0, pl.ds(start, CONV_ROWS), :] = dx[mid].astype(dx_ref.dtype)
            dm = dpre[mid]
            acc[...] += _stack_rows([_colsum(dm * taps[k][mid]) for k in range(4)] + [_colsum(dm)] + [jnp.zeros((1, wc), F32)] * 3)

        _halo_chunks(L, load, work)
        first = pl.program_id(1) == 0

        @pl.when(first)
        def _():
            dw_ref[...] = acc[0:4]
            db_ref[...] = acc[4:5]

        @pl.when(jnp.logical_not(first))
        def _():
            dw_ref[...] += acc[0:4]
            db_ref[...] += acc[4:5]

    slab = lambda off: pl.BlockSpec((1, L, wc), lambda j, b, off=off: (b, 0, off + j))
    in_specs = [slab(c0)] + [slab(0)] * n_d + ([slab(0)] if has_s else [])
    in_specs += [pl.BlockSpec((4, wc), lambda j, b: (0, c0 + j)), pl.BlockSpec((1, wc), lambda j, b: (0, c0 + j))]
    args = [xbc_raw, *dparts] + ([scaled[0]] if has_s else []) + [conv_w, conv_b]
    if has_s:
        in_specs.append(pl.BlockSpec((1, wc), lambda j, b: (0, j)))
        args.append(scaled[1])
    return pl.pallas_call(
        kern, name=name,
        out_shape=[jax.ShapeDtypeStruct((Bn, L, width), BF16), jax.ShapeDtypeStruct((4, width), F32), jax.ShapeDtypeStruct((1, width), F32)],
        grid=(width // wc, Bn), in_specs=in_specs,
        out_specs=[pl.BlockSpec((1, L, wc), lambda j, b: (b, 0, j)), pl.BlockSpec((4, wc), lambda j, b: (0, j)),
                   pl.BlockSpec((1, wc), lambda j, b: (0, j))],
        scratch_shapes=[pltpu.VMEM((8, wc), F32)],
        compiler_params=_cp(("arbitrary", "arbitrary")))(*args)


def _box_mean(x, k, step, pos, n, L, transpose):
    lo, hi = k // 2, k - 1 - k // 2
    cnt = (jnp.minimum(pos + hi + 1, n) - jnp.maximum(pos - lo, 0)).astype(F32)
    if transpose:
        x = x / cnt
        lo, hi = hi, lo
    acc = x
    for o in range(-lo, hi + 1):
        if o == 0:
            continue
        rolled = pltpu.roll(x, (-o * step) % L, 0)
        acc = acc + jnp.where((pos + o >= 0) & (pos + o < n), rolled, 0.0)
    return acc if transpose else acc / cnt


def pool_diff(name, v, col0, gi, transpose):
    L = v.shape[1]
    rows = L // GRID_W
    k = POOL_WINDOWS[gi]

    def body(x):
        tok = lax.broadcasted_iota(jnp.int32, x.shape, 0)
        col = tok & (GRID_W - 1)
        row = tok >> 6
        if not transpose:
            m = _box_mean(x, k, GRID_W, row, rows, L, False)
            m = _box_mean(m, k, 1, col, GRID_W, L, False)
        else:
            m = _box_mean(x, k, 1, col, GRID_W, L, True)
            m = _box_mean(m, k, GRID_W, row, rows, L, True)
        return m - x

    return slab_call(name, body, [(v, col0)], [], [(POOL_GROUP, BF16)], [])[0]


def pool_mix_fwd(name, dgs, z_pool, pool_w, pool_scale):
    def body(d0, d1, d2, d3, z, w, scale):
        q = jnp.concatenate([_dot(d, w[g * POOL_GROUP:(g + 1) * POOL_GROUP]) for g, d in enumerate((d0, d1, d2, d3))], axis=1)
        return q * scale * _silu(z)

    return tok_call(name, body, list(dgs) + [z_pool], [], [pool_w, pool_scale], [(D, BF16)], [], [])[0]


def pool_mix_bwd(name, dgs, z_pool, dyp, pool_w, pool_scale):
    def body(d0, d1, d2, d3, z, dyp, w, scale):
        ds = (d0, d1, d2, d3)
        q = jnp.concatenate([_dot(d, w[g * POOL_GROUP:(g + 1) * POOL_GROUP]) for g, d in enumerate(ds)], axis=1)
        dypm = dyp * _silu(z)
        dz = dyp * (q * scale) * _dsilu(z)
        dq = (dypm * scale).astype(BF16)
        dds, gws = [], []
        for g, d in enumerate(ds):
            dqg = dq[:, g * POOL_GROUP:(g + 1) * POOL_GROUP]
            dds.append(_dot_nt(dqg, w[g * POOL_GROUP:(g + 1) * POOL_GROUP]))
            gws.append(_dot_tn(d, dqg))
        return (*dds, dz, jnp.concatenate(gws, axis=0), _colsum(dypm * q))

    return tok_call(name, body, list(dgs) + [z_pool, dyp], [], [pool_w, pool_scale],
                    [(POOL_GROUP, F32)] * 4 + [(D, BF16)], [], [(D, POOL_GROUP), (1, D)])


def _cumsum_lanes(a, reverse):
    n = a.shape[1]
    k = lax.broadcasted_iota(jnp.int32, (n, n), 0)
    i = lax.broadcasted_iota(jnp.int32, (n, n), 1)
    tri = jnp.where((k >= i) if reverse else (k <= i), 1.0, 0.0).astype(BF16)
    return _dot_exact01(a, tri)


def _rows_to_cols(rows):
    r = rows.shape[0]
    if r < LANES:
        rows = jnp.concatenate([rows, jnp.zeros((LANES - r, rows.shape[1]), F32)], axis=0)
    return rows.T


def _cols_to_rows(cols):
    q = cols[0].shape[0]
    lane = lax.broadcasted_iota(jnp.int32, (q, LANES), 1)
    acc = jnp.zeros((q, LANES), F32)
    for r, c in enumerate(cols):
        acc = acc + jnp.where(lane == r, c, 0.0)
    return acc.T[0:len(cols)]


def _ssd_scalars(dtraw, bias, alog, reverse):
    dt = _softplus(dtraw + bias)
    A = -jnp.exp(alog)
    cs = _cumsum_lanes(dt * A, reverse)
    total = cs[:, 0:1] if reverse else cs[:, CHUNK - 1:CHUNK]
    return dt, A, cs, total


def _tri_mask(transposed, reverse):
    sub = lax.broadcasted_iota(jnp.int32, (CHUNK, CHUNK), 0)
    lane = lax.broadcasted_iota(jnp.int32, (CHUNK, CHUNK), 1)
    i, j = (lane, sub) if transposed else (sub, lane)
    return (i <= j) if reverse else (i >= j)


GPS = 4


def ssd_fwd(name, dtT, bias, alog, xbc, h0, direction, with_y, y_add=None):
    Bn, L = xbc.shape[:2]
    nc = L // CHUNK
    reverse = direction == 1
    blk0 = direction * (N_BC // GPS)
    gs = range(GPS)
    has_add = y_add is not None

    def chunk_of(s):
        return (nc - 1 - s) if reverse else s

    def kern(dt_ref, bias_ref, alog_ref, x_ref, b_ref, c_ref, h0_ref, *rest):
        if has_add:
            yp_ref, dsk_ref, rest = rest[0], rest[1], rest[2:]
        if with_y:
            y_ref, hs_ref, hf_ref, h_scr = rest
        else:
            hs_ref, hf_ref, h_scr = rest
        s = pl.program_id(2)

        @pl.when(s == 0)
        def _():
            h_scr[...] = h0_ref[0]

        first = lax.broadcasted_iota(jnp.int32, (1, LANES), 1) < HEAD_DIM
        heads = range(HPG)
        psl = [slice((r // 2) * LANES, (r // 2 + 1) * LANES) for r in heads]
        keep = _tri_mask(False, reverse)
        sc, x_bf, bm, h, h_bf, bt, cm, cb, cs_cols = [], [], [], [], [], [], [], [], []
        for g in gs:
            dt, _, cs, total = _ssd_scalars(dt_ref[0, g * HPG:(g + 1) * HPG], bias_ref[g], alog_ref[g], reverse)
            u = cs - jnp.log(dt)
            sc.append((cs, u, jnp.exp(total - u), jnp.exp(total)))
            x_bf.append(x_ref[0, :, g * GW:(g + 1) * GW].astype(BF16))
            bm.append(b_ref[0, :, g * D_STATE:(g + 1) * D_STATE])
            h.append(h_scr[g])
            h_bf.append(h[g].astype(BF16))
            hs_ref[0, g, 0] = h[g]
            bt.append(bm[g].T)
            if with_y:
                cm.append(c_ref[0, :, g * D_STATE:(g + 1) * D_STATE])
                cb.append(_dot_nt(cm[g].astype(BF16), bm[g].astype(BF16)))
                cs_cols.append(_rows_to_cols(cs))
        lhs = [[] for _ in gs]
        if with_y:
            for g in gs:
                cs, u = sc[g][0], sc[g][1]
                for r in heads:
                    cs_col = jnp.broadcast_to(cs_cols[g][:, r:r + 1], (CHUNK, LANES))
                    wf = cb[g] * jnp.exp(jnp.where(keep, cs_col - u[r:r + 1], -jnp.inf))
                    lhs[g].append(jnp.concatenate([wf.astype(BF16), (cm[g] * jnp.exp(cs_col)).astype(BF16)], axis=1))
        bts = [[(bt[g] * sc[g][2][r:r + 1]).astype(BF16) for r in heads] for g in gs]
        sts = [[_dot(bts[g][r], x_bf[g][:, psl[r]]) for r in heads] for g in gs]
        if with_y:
            ys = [[_dot(lhs[g][r], jnp.concatenate([x_bf[g][:, psl[r]], h_bf[g][:, psl[r]]], axis=0)) for r in heads] for g in gs]
        for g in gs:
            dc = sc[g][3]
            for p in range(HPG // 2):
                if with_y:
                    cols = slice(g * GW + p * LANES, g * GW + (p + 1) * LANES)
                    yv = jnp.where(first, ys[g][2 * p], ys[g][2 * p + 1])
                    if has_add:
                        yv = yv + yp_ref[0, :, cols] + dsk_ref[:, cols] * x_ref[0, :, cols]
                    y_ref[0, :, cols] = yv
                dc_p = jnp.where(first, dc[2 * p:2 * p + 1], dc[2 * p + 1:2 * p + 2])
                h_scr[g, :, psl[2 * p]] = h[g][:, psl[2 * p]] * dc_p + jnp.where(first, sts[g][2 * p], sts[g][2 * p + 1])

        @pl.when(s == nc - 1)
        def _():
            hf_ref[0] = h_scr[...]

    nb = D_INNER // (GPS * D_STATE)
    in_specs = [
        pl.BlockSpec((1, GPS * HPG, CHUNK), lambda b, g, s: (b, blk0 + g, chunk_of(s))),
        pl.BlockSpec((GPS, HPG, 1), lambda b, g, s: (blk0 + g, 0, 0)),
        pl.BlockSpec((GPS, HPG, 1), lambda b, g, s: (blk0 + g, 0, 0)),
        pl.BlockSpec((1, CHUNK, GPS * GW), lambda b, g, s: (b, chunk_of(s), g)),
        pl.BlockSpec((1, CHUNK, GPS * D_STATE), lambda b, g, s: (b, chunk_of(s), nb + g)),
        pl.BlockSpec((1, CHUNK, GPS * D_STATE), lambda b, g, s: (b, chunk_of(s), nb + N_BC // GPS + g)),
        pl.BlockSpec((1, GPS, D_STATE, GW), lambda b, g, s: (b, g, 0, 0)),
    ]
    args = [dtT, bias, alog, xbc, xbc, xbc, h0]
    if has_add:
        in_specs += [pl.BlockSpec((1, CHUNK, GPS * GW), lambda b, g, s: (b, chunk_of(s), g)),
                     pl.BlockSpec((1, GPS * GW), lambda b, g, s: (0, g))]
        args += list(y_add)
    out_shape, out_specs = [], []
    if with_y:
        out_shape.append(jax.ShapeDtypeStruct((Bn, L, D_INNER), F32))
        out_specs.append(pl.BlockSpec((1, CHUNK, GPS * GW), lambda b, g, s: (b, chunk_of(s), g)))
    out_shape += [jax.ShapeDtypeStruct((Bn, N_BC, nc, D_STATE, GW), F32), jax.ShapeDtypeStruct((Bn, N_BC, D_STATE, GW), F32)]
    out_specs += [pl.BlockSpec((1, GPS, 1, D_STATE, GW), lambda b, g, s: (b, g, chunk_of(s), 0, 0)),
                  pl.BlockSpec((1, GPS, D_STATE, GW), lambda b, g, s: (b, g, 0, 0))]
    return pl.pallas_call(
        kern, name=name, out_shape=out_shape, grid=(Bn, N_BC // GPS, nc), in_specs=in_specs, out_specs=out_specs,
        scratch_shapes=[pltpu.VMEM((GPS, D_STATE, GW), F32)],
        compiler_params=_cp(("arbitrary", "arbitrary", "arbitrary")))(*args)


def ssd_bwd(name, dtT, bias, alog, xbc, h_start, dy, dh_final, direction, dx_add=None):
    Bn, L = xbc.shape[:2]
    nc = L // CHUNK
    reverse = direction == 1
    blk0 = direction * (N_BC // GPS)
    has_y = dy is not None
    has_add = dx_add is not None
    assert has_y or not has_add
    last = 0 if reverse else CHUNK - 1
    gs = range(GPS)

    def chunk_of(s):
        return s if reverse else (nc - 1 - s)

    def kern(*refs):
        if has_add:
            dxp_ref, dsk_ref = refs[9], refs[10]
            refs = refs[:9] + refs[11:]
        if has_y:
            (dt_ref, bias_ref, alog_ref, x_ref, b_ref, hs_ref, dhf_ref, c_ref, dy_ref,
             dx_ref, db_ref, ddt_ref, dbias_ref, dalog_ref, dh0_ref, dc_ref, dh_scr) = refs
        else:
            (dt_ref, bias_ref, alog_ref, x_ref, b_ref, hs_ref, dhf_ref,
             dx_ref, db_ref, ddt_ref, dbias_ref, dalog_ref, dh0_ref, dh_scr) = refs
        s = pl.program_id(2)

        @pl.when(s == 0)
        def _():
            dh_scr[...] = dhf_ref[0]
            dbias_ref[...] = jnp.zeros(dbias_ref.shape, F32)
            dalog_ref[...] = jnp.zeros(dalog_ref.shape, F32)

        first = lax.broadcasted_iota(jnp.int32, (1, LANES), 1) < HEAD_DIM
        heads = range(HPG)
        psl = [slice((r // 2) * LANES, (r // 2 + 1) * LANES) for r in heads]
        mine = [first if r % 2 == 0 else jnp.logical_not(first) for r in heads]
        zeros_bf = jnp.zeros((CHUNK, LANES), BF16)
        keep = _tri_mask(True, reverse)
        ctx = []
        for g in gs:
            dtraw = dt_ref[0, g * HPG:(g + 1) * HPG]
            dt, A, cs, total = _ssd_scalars(dtraw, bias_ref[g], alog_ref[g], reverse)
            u = cs - jnp.log(dt)
            c = dict(dtraw=dtraw, dt=dt, A=A, cs=cs, total=total, u=u, dtt=jnp.exp(total - u), dcy=jnp.exp(total),
                     u_cols=_rows_to_cols(u), x_bf=x_ref[0, :, g * GW:(g + 1) * GW].astype(BF16),
                     bm=b_ref[0, :, g * D_STATE:(g + 1) * D_STATE], h=hs_ref[0, g, 0], dh=dh_scr[g])
            c["bt"] = c["bm"].T
            c["dh_bf"] = c["dh"].astype(BF16)
            if has_y:
                c["cm"] = c_ref[0, :, g * D_STATE:(g + 1) * D_STATE]
                c["ct"] = c["cm"].T
                c["e_row"] = jnp.exp(cs)
                c["dy_bf"] = dy_ref[0, :, g * GW:(g + 1) * GW].astype(BF16)
                c["h_bf"] = c["h"].astype(BF16)
                c["cbt"] = _dot_nt(c["bm"].astype(BF16), c["cm"].astype(BF16))
            ctx.append(c)
        for c in ctx:
            c["lhs"], c["et"] = [], []
            for r in heads:
                u_col = jnp.broadcast_to(c["u_cols"][:, r:r + 1], (CHUNK, LANES))
                bs = (c["bm"] * jnp.exp(c["total"][r:r + 1] - u_col)).astype(BF16)
                if has_y:
                    et = jnp.exp(jnp.where(keep, c["cs"][r:r + 1] - u_col, -jnp.inf))
                    c["et"].append(et)
                    c["lhs"].append(jnp.concatenate([(c["cbt"] * et).astype(BF16), bs], axis=1))
                else:
                    c["lhs"].append(bs)
        for c in ctx:
            c["p2raw"] = [_dot_nt(c["dh_bf"][:, psl[r]], jnp.where(mine[r], c["x_bf"][:, psl[r]], zeros_bf)) for r in heads]
            if has_y:
                c["a1"] = [_dot_nt(jnp.concatenate([c["x_bf"][:, psl[r]], c["h_bf"][:, psl[r]]], axis=0),
                                   jnp.where(mine[r], c["dy_bf"][:, psl[r]], zeros_bf)) for r in heads]
                c["news"] = [_dot((c["ct"] * c["e_row"][r:r + 1]).astype(BF16), c["dy_bf"][:, psl[r]]) for r in heads]
                c["dxs"] = [_dot(c["lhs"][r], jnp.concatenate([c["dy_bf"][:, psl[r]], c["dh_bf"][:, psl[r]]], axis=0)) for r in heads]
            else:
                c["dxs"] = [_dot(c["lhs"][r], c["dh_bf"][:, psl[r]]) for r in heads]
        for g, c in enumerate(ctx):
            dbt = jnp.zeros((D_STATE, CHUNK), F32)
            dcbt = jnp.zeros((CHUNK, CHUNK), F32)
            dct = jnp.zeros((D_STATE, CHUNK), F32)
            tots, out_rows, in_rows, in_cols = [], [], [], []
            for r in heads:
                if has_y:
                    pt = c["a1"][r][0:CHUNK] * c["et"][r]
                    dcbt = dcbt + pt
                    mt = pt * c["cbt"]
                    ph = c["a1"][r][CHUNK:] * c["e_row"][r:r + 1]
                    dct = dct + ph
                    out_rows.append(_colsum(mt + c["ct"] * ph))
                    in_cols.append(jnp.sum(mt, axis=1, keepdims=True))
                p2 = c["p2raw"][r] * c["dtt"][r:r + 1]
                dbt = dbt + p2
                t_term = _colsum(c["bt"] * p2)
                in_rows.append(t_term)
                hdh = c["h"][:, psl[r]] * c["dh"][:, psl[r]]
                tot = jnp.sum(t_term, axis=1, keepdims=True) + c["dcy"][r:r + 1] * jnp.sum(jnp.where(mine[r], hdh, 0.0), keepdims=True)
                tots.append(jnp.broadcast_to(tot, (1, CHUNK)))
            for p in range(HPG // 2):
                cols = slice(g * GW + p * LANES, g * GW + (p + 1) * LANES)
                dxv = jnp.where(first, c["dxs"][2 * p], c["dxs"][2 * p + 1])
                if has_add:
                    dxv = dxv + dxp_ref[0, :, cols] + dsk_ref[:, cols] * dy_ref[0, :, cols]
                dx_ref[0, :, cols] = dxv
                new = c["dh"][:, psl[2 * p]] * jnp.where(first, c["dcy"][2 * p:2 * p + 1], c["dcy"][2 * p + 1:2 * p + 2])
                if has_y:
                    new = new + jnp.where(first, c["news"][2 * p], c["news"][2 * p + 1])
                dh_scr[g, :, psl[2 * p]] = new
            db = dbt.T
            if has_y:
                dcbt_bf = dcbt.astype(BF16)
                db = db + _dot(dcbt_bf, c["cm"].astype(BF16))
                dc_ref[0, :, g * D_STATE:(g + 1) * D_STATE] = dct.T + _dot_tn(dcbt_bf, c["bm"].astype(BF16))
            db_ref[0, :, g * D_STATE:(g + 1) * D_STATE] = db
            s_row = _stack_rows(in_rows)
            lane = lax.broadcasted_iota(jnp.int32, (HPG, CHUNK), 1)
            dcs = jnp.where(lane == last, _stack_rows(tots), 0.0)
            if has_y:
                s_row = s_row + _cols_to_rows(in_cols)
                dcs = dcs + _stack_rows(out_rows)
            dcs = dcs - s_row
            da = _cumsum_lanes(dcs, not reverse)
            ddt = da * c["A"] + jnp.where(c["dt"] > 0.0, s_row / c["dt"], 0.0)
            ddtraw = ddt * _sigmoid(c["dtraw"] + bias_ref[g])
            ddt_ref[0, g * HPG:(g + 1) * HPG] = ddtraw
            dbias_ref[0, g] += jnp.sum(ddtraw, axis=1, keepdims=True)
            dalog_ref[0, g] += jnp.sum(da * c["dt"], axis=1, keepdims=True) * c["A"]

        @pl.when(s == nc - 1)
        def _():
            dh0_ref[0] = dh_scr[...]

    nb = D_INNER // (GPS * D_STATE)
    cidx = lambda b, g, s: (b, chunk_of(s), g)
    hidx = lambda b, g, s: (b, g, 0, 0)
    in_specs = [
        pl.BlockSpec((1, GPS * HPG, CHUNK), lambda b, g, s: (b, blk0 + g, chunk_of(s))),
        pl.BlockSpec((GPS, HPG, 1), lambda b, g, s: (blk0 + g, 0, 0)),
        pl.BlockSpec((GPS, HPG, 1), lambda b, g, s: (blk0 + g, 0, 0)),
        pl.BlockSpec((1, CHUNK, GPS * GW), cidx),
        pl.BlockSpec((1, CHUNK, GPS * D_STATE), lambda b, g, s: (b, chunk_of(s), nb + g)),
        pl.BlockSpec((1, GPS, 1, D_STATE, GW), lambda b, g, s: (b, g, chunk_of(s), 0, 0)),
        pl.BlockSpec((1, GPS, D_STATE, GW), hidx),
    ]
    args = [dtT, bias, alog, xbc, xbc, h_start, dh_final]
    if has_y:
        in_specs += [pl.BlockSpec((1, CHUNK, GPS * D_STATE), lambda b, g, s: (b, chunk_of(s), nb + N_BC // GPS + g)),
                     pl.BlockSpec((1, CHUNK, GPS * GW), cidx)]
        args += [xbc, dy]
    if has_add:
        in_specs += [pl.BlockSpec((1, CHUNK, GPS * GW), cidx), pl.BlockSpec((1, GPS * GW), lambda b, g, s: (0, g))]
        args += list(dx_add)
    out_shape = [jax.ShapeDtypeStruct((Bn, L, D_INNER), F32), jax.ShapeDtypeStruct((Bn, L, N_BC * D_STATE), F32),
                 jax.ShapeDtypeStruct((Bn, N_HEADS, L), F32), jax.ShapeDtypeStruct((Bn, N_BC, HPG, 1), F32),
                 jax.ShapeDtypeStruct((Bn, N_BC, HPG, 1), F32), jax.ShapeDtypeStruct((Bn, N_BC, D_STATE, GW), F32)]
    out_specs = [pl.BlockSpec((1, CHUNK, GPS * GW), cidx), pl.BlockSpec((1, CHUNK, GPS * D_STATE), cidx),
                 pl.BlockSpec((1, GPS * HPG, CHUNK), lambda b, g, s: (b, g, chunk_of(s))),
                 pl.BlockSpec((1, GPS, HPG, 1), hidx), pl.BlockSpec((1, GPS, HPG, 1), hidx), pl.BlockSpec((1, GPS, D_STATE, GW), hidx)]
    if has_y:
        out_shape.append(jax.ShapeDtypeStruct((Bn, L, N_BC * D_STATE), F32))
        out_specs.append(pl.BlockSpec((1, CHUNK, GPS * D_STATE), cidx))
    res = pl.pallas_call(
        kern, name=name, out_shape=out_shape, grid=(Bn, N_BC // GPS, nc), in_specs=in_specs, out_specs=out_specs,
        scratch_shapes=[pltpu.VMEM((GPS, D_STATE, GW), F32)],
        compiler_params=_cp(("arbitrary", "arbitrary", "arbitrary")))(*args)
    dxs, db, ddt, dbias, dalog, dh0 = res[:6]
    return dxs, db, (res[6] if has_y else None), ddt, dbias, dalog, dh0


def _group_mean(v):
    gw = D_INNER // N_BC
    parts = [jnp.broadcast_to(jnp.mean(v[:, g * gw:(g + 1) * gw], axis=-1, keepdims=True), (v.shape[0], gw)) for g in range(N_BC)]
    return jnp.concatenate(parts, axis=1)


def gated_norm_fwd(name, y, z, w_norm):
    def body(y, z, w):
        u = y * _silu(z)
        r = lax.rsqrt(_group_mean(u * u) + NORM_EPS)
        return u * r * w

    return tok_call(name, body, [y, z], [], [w_norm], [(D_INNER, BF16)], [], [])[0]


def gated_norm_bwd(name, y, xs_src, z, d_out, w_norm, head_sel):
    def body(y, xs, z, do, w, sel):
        sz = _silu(z)
        u = y * sz
        r = lax.rsqrt(_group_mean(u * u) + NORM_EPS)
        duh = do * w
        du = r * (duh - u * (r * r) * _group_mean(duh * u))
        dy = du * sz
        dz = du * y * _dsilu(z)
        dsk_heads = _dot_exact01(jnp.broadcast_to(_colsum(dy * xs), (8, D_INNER)), sel)
        return dy, dz, _colsum(do * u * r), dsk_heads

    return tok_call(name, body, [y, xs_src, z, d_out], [], [w_norm, head_sel],
                    [(D_INNER, F32), (D_INNER, BF16)], [], [(1, D_INNER), (8, LANES)], tm=128)


def merge_fwd(name, y_pool, y_ssd, gatepre, x, target, gate, b_merge, norm_post, w_pp, w_ps, w_out):
    def body(yp, ys, gp, x, tgt, gate, bm, wpost, w_pp, w_ps, w_out):
        p1 = _dot(yp, w_pp)
        p2 = _dot(ys, w_ps)
        gates = _sigmoid(gp + bm)
        merged = gates[:, :D] * p1 + gates[:, D:] * p2
        out = _dot(merged.astype(BF16), w_out)
        r = _rms_r(out)
        outr = out * r
        nq = outr * wpost
        err = x + gate * nq - tgt
        loss = 0.5 * jnp.sum(jnp.mean(err * err, axis=-1, keepdims=True), keepdims=True).reshape(1, 1)
        g = err * (1.0 / D)
        dnq = g * gate
        dout = _rms_bwd(dnq * wpost, out, r)
        return merged, p1, p2, dout, g, _colsum(g * nq), _colsum(dnq * outr), jnp.broadcast_to(loss, (1, LANES))

    return tok_call(name, body, [y_pool, y_ssd, gatepre, x, target], [gate], [b_merge, norm_post, w_pp, w_ps, w_out],
                    [(D, BF16), (D, F32), (D, F32), (D, BF16), (D, F32)], [D], [(1, D), (1, LANES)])


def merge_bwd(name, dout, gatepre, p1, p2, b_merge, w_pp, w_ps, w_out):
    def body(dout, gp, p1, p2, bm, w_pp, w_ps, w_out):
        dmerged = _dot_nt(dout, w_out)
        gates = _sigmoid(gp + bm)
        g1, g2 = gates[:, :D], gates[:, D:]
        dp1 = (dmerged * g1).astype(BF16)
        dp2 = (dmerged * g2).astype(BF16)
        dgp = jnp.concatenate([dmerged * p1 * g1 * (1.0 - g1), dmerged * p2 * g2 * (1.0 - g2)], axis=1)
        return dp1, dp2, dgp, _dot_nt(dp1, w_pp), _dot_nt(dp2, w_ps), _colsum(dgp)

    return tok_call(name, body, [dout, gatepre, p1, p2], [], [b_merge, w_pp, w_ps, w_out],
                    [(D, BF16), (D, BF16), (2 * D, BF16), (D, F32), (D_INNER, F32)], [], [(1, 2 * D)])


def _adamw_math(w, g, m, v):
    m = ADAM_B1 * m + (1.0 - ADAM_B1) * g
    v = ADAM_B2 * v + (1.0 - ADAM_B2) * (g * g)
    m_hat = m / (1.0 - ADAM_B1 ** ADAM_STEP)
    v_hat = v / (1.0 - ADAM_B2 ** ADAM_STEP)
    delta = -ADAM_LR * (m_hat / (jnp.sqrt(v_hat) + ADAM_EPS) + ADAM_WD * w)
    return delta, m, v


def adamw(name, w, g, m, v, tr=256):
    R, C = w.shape
    tr = min(tr, R)
    assert R % tr == 0

    def body(w_ref, g_ref, m_ref, v_ref, d_ref, nm_ref, nv_ref):
        d, nm, nv = _adamw_math(w_ref[...], g_ref[...], m_ref[...], v_ref[...])
        d_ref[...] = d
        nm_ref[...] = nm
        nv_ref[...] = nv

    spec = pl.BlockSpec((tr, C), lambda i: (i, 0))
    return pl.pallas_call(
        body, name=name, out_shape=[jax.ShapeDtypeStruct((R, C), F32)] * 3, grid=(R // tr,),
        in_specs=[spec] * 4, out_specs=[spec] * 3, compiler_params=_cp(("parallel",)))(w, g, m, v)


def all_gather_small(name, v):
    R, C = v.shape

    def body(v_ref, out_ref, send_sems, recv_sems, local_sem):
        x, y, c = _me()
        me = 4 * x + 2 * y + c
        mine = pltpu.make_async_copy(v_ref, out_ref.at[me], local_sem)
        mine.start()
        copies = []
        for d in range(1, N_DEV):
            dx, dy, dc = d // 4, (d // 2) % 2, d % 2
            px, py, pc = x ^ dx, y ^ dy, c ^ dc
            copies.append(pltpu.make_async_remote_copy(
                src_ref=v_ref, dst_ref=out_ref.at[me], send_sem=send_sems.at[d - 1], recv_sem=recv_sems.at[d - 1],
                device_id=(px, py, pc), device_id_type=MESH))
        for cp in copies:
            cp.start()
        for d in range(1, N_DEV):
            dx, dy, dc = d // 4, (d // 2) % 2, d % 2
            peer = 4 * (x ^ dx) + 2 * (y ^ dy) + (c ^ dc)
            pltpu.make_async_remote_copy(
                src_ref=v_ref, dst_ref=out_ref.at[peer], send_sem=send_sems.at[d - 1], recv_sem=recv_sems.at[d - 1],
                device_id=(x ^ dx, y ^ dy, c ^ dc), device_id_type=MESH).wait_recv()
        for cp in copies:
            cp.wait_send()
        mine.wait()

    return pl.pallas_call(
        body, name=name, out_shape=jax.ShapeDtypeStruct((N_DEV, R, C), F32),
        in_specs=[pl.BlockSpec(memory_space=pltpu.VMEM)], out_specs=pl.BlockSpec(memory_space=pltpu.VMEM),
        scratch_shapes=[pltpu.SemaphoreType.DMA((N_DEV - 1,)), pltpu.SemaphoreType.DMA((N_DEV - 1,)), pltpu.SemaphoreType.DMA],
        compiler_params=pltpu.CompilerParams(vmem_limit_bytes=VMEM_LIMIT))(v)


def all_gather_chips(name, shard):
    R, C = shard.shape
    half = R // 2
    assert R % 32 == 0

    def body(s_ref, out_ref, send_sems, recv_sems):
        x, y, c = _me()
        chips = [(1 - x, y), (x, 1 - y), (1 - x, 1 - y)]

        def rows(chip, hc):
            return out_ref.at[2 * chip[0] + chip[1], pl.ds(hc * half, half), :]

        first = [pltpu.make_async_remote_copy(
            src_ref=s_ref.at[pl.ds(c * half, half), :], dst_ref=rows((x, y), c), send_sem=send_sems.at[j],
            recv_sem=recv_sems.at[j], device_id=(*chip, c), device_id_type=MESH) for j, chip in enumerate(chips)]
        for cp in first:
            cp.start()
        passed = [pltpu.make_async_remote_copy(
            src_ref=rows(chip, c), dst_ref=rows(chip, c), send_sem=send_sems.at[3 + j], recv_sem=recv_sems.at[3 + j],
            device_id=(x, y, 1 - c), device_id_type=MESH) for j, chip in enumerate(chips)]
        for j, chip in enumerate(chips):
            pltpu.make_async_remote_copy(
                src_ref=rows(chip, c), dst_ref=rows(chip, c), send_sem=send_sems.at[j], recv_sem=recv_sems.at[j],
                device_id=(*chip, c), device_id_type=MESH).wait_recv()
            passed[j].start()
        for j, chip in enumerate(chips):
            pltpu.make_async_remote_copy(
                src_ref=rows(chip, 1 - c), dst_ref=rows(chip, 1 - c), send_sem=send_sems.at[3 + j], recv_sem=recv_sems.at[3 + j],
                device_id=(x, y, 1 - c), device_id_type=MESH).wait_recv()
        for cp in first + passed:
            cp.wait_send()

    out = pl.pallas_call(
        body, name=name, out_shape=jax.ShapeDtypeStruct((N_CHIPS, R, C), shard.dtype),
        in_specs=[pl.BlockSpec(memory_space=pl.ANY)], out_specs=pl.BlockSpec(memory_space=pl.ANY),
        scratch_shapes=[pltpu.SemaphoreType.DMA((6,)), pltpu.SemaphoreType.DMA((6,))],
        compiler_params=pltpu.CompilerParams(vmem_limit_bytes=VMEM_LIMIT))(shard)
    chip = 2 * lax.axis_index("x") + lax.axis_index("y")
    return lax.dynamic_update_index_in_dim(out, shard, chip, 0)


def sibling_swap(name, v):
    def body(v_ref, out_ref, send_sem, recv_sem):
        x, y, c = _me()
        cp = pltpu.make_async_remote_copy(src_ref=v_ref, dst_ref=out_ref, send_sem=send_sem, recv_sem=recv_sem,
                                          device_id=(x, y, 1 - c), device_id_type=MESH)
        cp.start()
        cp.wait()

    return pl.pallas_call(
        body, name=name, out_shape=jax.ShapeDtypeStruct(v.shape, v.dtype),
        in_specs=[pl.BlockSpec(memory_space=pl.ANY)], out_specs=pl.BlockSpec(memory_space=pl.ANY),
        scratch_shapes=[pltpu.SemaphoreType.DMA, pltpu.SemaphoreType.DMA],
        compiler_params=pltpu.CompilerParams(vmem_limit_bytes=VMEM_LIMIT))(v)


def _row_tile(rows, cap, mult=8):
    best = None
    for t in range(mult, min(rows, cap) + 1, mult):
        if rows % t == 0:
            best = t
    assert best is not None, rows
    return best


def add_arrays(name, arrs, out_dtype=F32):
    shape = arrs[0].shape
    C = shape[-1]
    flat = [a.reshape(-1, C) for a in arrs]
    R = flat[0].shape[0]
    narrow = out_dtype == BF16 or any(a.dtype == BF16 for a in arrs)
    tr = _row_tile(R, 2048 if len(arrs) <= 2 else 1024, 16 if narrow else 8)
    n = len(flat)

    def body(*refs):
        acc = refs[0][...].astype(F32)
        for r in refs[1:n]:
            acc = acc + r[...].astype(F32)
        refs[n][...] = acc.astype(out_dtype)

    spec = pl.BlockSpec((tr, C), lambda i: (i, 0))
    out = pl.pallas_call(
        body, name=name, out_shape=jax.ShapeDtypeStruct((R, C), out_dtype), grid=(R // tr,),
        in_specs=[spec] * n, out_specs=spec, compiler_params=_cp(("parallel",)))(*flat)
    return out.reshape(shape)


def reduce_scatter_chips(slabs):
    _, R, C = slabs.shape
    half = R // 2
    c = lax.axis_index("c")
    halves = slabs.reshape(N_CHIPS, 2, half, C)
    own = lax.dynamic_index_in_dim(halves, c, axis=1, keepdims=False)
    other = lax.dynamic_index_in_dim(halves, 1 - c, axis=1, keepdims=False)
    from_sibling = sibling_swap("rs_sibling_halves", other.astype(BF16))
    return add_arrays("rs_add_sibling", [own, from_sibling], out_dtype=BF16)


def reduce_scatter_finish(landed):
    c = lax.axis_index("c")
    mine = add_arrays("rs_add_chips", [landed[j] for j in range(N_CHIPS)])
    sib = sibling_swap("rs_sibling_result", mine)
    return jnp.concatenate([jnp.where(c == 0, mine, sib), jnp.where(c == 0, sib, mine)], axis=0)


def ada_mod_shard(cond_all, w_ada_shard, b_ada_shard):
    def body(c_ref, w_ref, b_ref, o_ref):
        o_ref[...] = _dot(_silu(c_ref[...]).astype(BF16), w_ref[...].astype(BF16)) + b_ref[...]

    return pl.pallas_call(body, name="ada_mod_shard", out_shape=jax.ShapeDtypeStruct((cond_all.shape[0], w_ada_shard.shape[1]), F32),
                          compiler_params=_cp())(cond_all, w_ada_shard, b_ada_shard)


def ada_bwd_shard(cond_all, dmod_all_shard, dmod_all, w_ada_shard, row_is_cctx):
    def body(c_ref, ds_ref, da_ref, w_ref, sel_ref, gw_ref, gb_ref, part_ref):
        sc = _silu(c_ref[...]).astype(BF16)
        gw_ref[...] = _dot_tn(sc, ds_ref[...].astype(BF16))
        gb_ref[...] = _colsum(da_ref[...])
        dc_tot = jnp.broadcast_to(_colsum(ds_ref[...] * sel_ref[...]), (8, ds_ref.shape[1]))
        part_ref[...] = _dot_nt(dc_tot.astype(BF16), w_ref[...].astype(BF16))

    return pl.pallas_call(
        body, name="ada_bwd_shard",
        out_shape=[jax.ShapeDtypeStruct(w_ada_shard.shape, F32), jax.ShapeDtypeStruct((1, dmod_all.shape[1]), F32),
                   jax.ShapeDtypeStruct((8, D), F32)],
        compiler_params=_cp())(cond_all, dmod_all_shard, dmod_all, w_ada_shard, row_is_cctx)


def sum_devices(name, gathered):
    def body(g_ref, o_ref):
        acc = g_ref[0]
        for d in range(1, N_DEV):
            acc = acc + g_ref[d]
        o_ref[...] = acc

    return pl.pallas_call(body, name=name, out_shape=jax.ShapeDtypeStruct(gathered.shape[1:], F32), compiler_params=_cp())(gathered)


def cctx_finish(gathered, c_ctx_row):
    def body(g_ref, c_ref, o_ref):
        acc = g_ref[0, 0:1, :]
        for k in range(1, N_CHIPS):
            acc = acc + g_ref[2 * k, 0:1, :]
        o_ref[...] = acc * _dsilu(c_ref[...])

    return pl.pallas_call(body, name="cctx_finish", out_shape=jax.ShapeDtypeStruct((1, D), F32), compiler_params=_cp())(gathered, c_ctx_row)


def _pack(parts, rows):
    flat = []
    for p in parts:
        p = p.reshape(-1)
        pad = (-p.shape[0]) % LANES
        flat.append(jnp.pad(p, (0, pad)) if pad else p)
    v = jnp.concatenate(flat)
    return jnp.pad(v, (0, rows * LANES - v.shape[0])).reshape(rows, LANES)


def _unpack(v, sizes):
    flat = v.reshape(-1)
    out, off = [], 0
    for n in sizes:
        out.append(flat[off:off + n])
        off += n + (-n) % LANES
    return out


W_SHARD_ROWS = 3456
SEG_ROWS = (0, 2320, 2576, 3088, 3344, 3408)


def kernel(x, c, ctx, c_ctx, w_ada, b_ada, norm_pre, norm_post, w_in, b_merge, pool_w, pool_scale, conv_w, conv_b, dt_bias, a_log, d_skip, ssd_norm, w_proj_pool, w_proj_ssd, w_out, loss_target, m_c_ctx, m_w_ada, m_b_ada, m_norm_pre, m_norm_post, m_w_in, m_b_merge, m_pool_w, m_pool_scale, m_conv_w, m_conv_b, m_dt_bias, m_a_log, m_d_skip, m_ssd_norm, m_w_proj_pool, m_w_proj_ssd, m_w_out, v_c_ctx, v_w_ada, v_b_ada, v_norm_pre, v_norm_post, v_w_in, v_b_merge, v_pool_w, v_pool_scale, v_conv_w, v_conv_b, v_dt_bias, v_a_log, v_d_skip, v_ssd_norm, v_w_proj_pool, v_w_proj_ssd, v_w_out):
    Bn, L, _ = x.shape
    Lc = ctx.shape[1]
    T, Tc = Bn * L, Bn * Lc
    assert Bn == 2
    ix, iy, ic = lax.axis_index("x"), lax.axis_index("y"), lax.axis_index("c")
    me = 4 * ix + 2 * iy + ic
    chip = 2 * ix + iy
    ada_cols = w_ada.shape[2]
    cw_cols = conv_w.shape[2]

    cond_own = jnp.pad(c, ((0, 8 - Bn), (0, 0))) + jnp.pad(c_ctx[None, :], ((Bn, 7 - Bn), (0, 0)))
    convw_own = jnp.pad(conv_w[0], ((0, 4), (0, D - cw_cols)))
    g1 = all_gather_small("gather_cond", jnp.concatenate([cond_own, convw_own], axis=0))
    cond_all = g1[:, 0:8].reshape(8 * N_DEV, D)
    conv_w_full = jnp.concatenate([g1[2 * k, 8:12, 0:cw_cols] for k in range(N_CHIPS)], axis=1)
    b_ada_shard = lax.dynamic_slice(b_ada, (0, chip * ada_cols), (1, ada_cols))
    g2 = all_gather_small("gather_mod", ada_mod_shard(cond_all, w_ada[0], b_ada_shard))
    mod_full = jnp.concatenate([g2[2 * k] for k in range(N_CHIPS)], axis=1)
    own = lax.dynamic_slice(mod_full, (8 * me, 0), (8, 3 * D))
    shift, scale, gate = (own[0:Bn, i * D:(i + 1) * D][:, None, :] for i in range(3))
    shift_c, scale_c = (jnp.broadcast_to(own[Bn:Bn + 1, i * D:(i + 1) * D][None], (Bn, 1, D)) for i in range(2))

    w_in_rows = IN_COLS // N_CHIPS
    shard_in = jnp.concatenate([w_in[0].T, jnp.zeros((16, D), F32)], axis=0).astype(BF16)
    shard_rest = jnp.concatenate([w_proj_pool[0], w_proj_ssd[0], w_out[0], pool_w[0].reshape(64, D)], axis=0).astype(BF16)
    w_inT = all_gather_chips("gather_w_in", shard_in)[:, 0:w_in_rows].reshape(IN_COLS, D)
    w_dt = jnp.pad(w_inT[9216:IN_COLS], ((0, LANES - 64), (0, 0)))
    seg_lo = (0, 1024, 2048, 4096, 6144, 8192, 8704)
    seg_hi = (1024, 2048, 4096, 6144, 8192, 8704, 9216)
    w_seg = [w_inT[lo:hi] for lo, hi in zip(seg_lo, seg_hi)] + [w_dt]

    hx = prenorm_fwd("prenorm_x", x, scale, shift, norm_pre)
    hc = prenorm_fwd("prenorm_ctx", ctx, scale_c, shift_c, norm_pre)
    hx2, hc2 = hx.reshape(T, D), hc.reshape(Tc, D)
    v = mm_nt("proj_v", hx2, w_inT[0:1024], F32).reshape(Bn, L, D)
    zp = mm_nt("proj_zpool", hx2, w_inT[1024:2048], F32).reshape(Bn, L, D)
    zs = mm_nt("proj_zssd", hx2, w_inT[2048:4096], F32).reshape(Bn, L, D_INNER)
    gp = mm_nt("proj_gate", hx2, w_inT[4096:6144], F32).reshape(Bn, L, 2 * D)
    xbc_raw, g_rest = mm_nt("proj_xbc", hx2, w_inT[6144:9216], F32, gather=shard_rest)
    xbc_raw = xbc_raw.reshape(Bn, L, CONV_DIM)
    w_pp = g_rest[:, 0:256].reshape(D, D)
    w_ps = g_rest[:, 256:768].reshape(D_INNER, D)
    w_o = g_rest[:, 768:1024].reshape(D, D)
    pool_full = g_rest[:, 1024:1088].reshape(N_CHIPS, 4, 64, POOL_GROUP).transpose(1, 0, 2, 3).reshape(D, POOL_GROUP)
    dt_raw = mm_nt("proj_dt", hx2, w_dt, F32)
    xbc_raw_c = mm_nt("proj_xbc_ctx", hc2, w_inT[6144:9216], F32).reshape(Bn, Lc, CONV_DIM)
    dt_raw_c = mm_nt("proj_dt_ctx", hc2, w_dt, F32)
    dtT = dt_raw[:, :64].reshape(Bn, L, 64).transpose(0, 2, 1)
    dtT_c = dt_raw_c[:, :64].reshape(Bn, Lc, 64).transpose(0, 2, 1)
    bias3 = dt_bias.reshape(2 * N_BC, HPG, 1)
    alog3 = a_log.reshape(2 * N_BC, HPG, 1)

    xbc = conv_fwd("conv_x", xbc_raw, conv_w_full, conv_b)
    xbc_c = conv_fwd("conv_ctx", xbc_raw_c, conv_w_full, conv_b)
    zero_state = jnp.zeros((Bn, N_BC, D_STATE, GW), F32)
    dskip_lanes = jnp.repeat(d_skip[0], HEAD_DIM)[None, :]
    ys, hs_x, hs_c = [], [], []
    for d in range(2):
        hsc, hfc = ssd_fwd(f"ssd_fwd_ctx{d}", dtT_c, bias3, alog3, xbc_c, zero_state, d, False)
        y, hsx, _ = ssd_fwd(f"ssd_fwd_x{d}", dtT, bias3, alog3, xbc, hfc, d, True,
                            y_add=(ys[0], dskip_lanes) if d == 1 else None)
        ys.append(y)
        hs_x.append(hsx)
        hs_c.append(hsc)

    dgs = [pool_diff(f"pool_diff{g}", v, g * POOL_GROUP, g, False) for g in range(4)]
    y_pool = pool_mix_fwd("pool_mix", dgs, zp, pool_full, pool_scale)
    y_ssd = gated_norm_fwd("gated_norm", ys[1], zs, ssd_norm)
    merged, p1, p2, dout, g_res, dgate, g_norm_post, loss_part = merge_fwd(
        "merge_fwd", y_pool, y_ssd, gp, x, loss_target, gate, b_merge, norm_post, w_pp, w_ps, w_o)

    dp1, dp2, dgp, dyp, dys, g_b_merge = merge_bwd("merge_bwd", dout, gp, p1, p2, b_merge, w_pp, w_ps, w_o)
    gw_o = mm_tn("gw_out", merged.reshape(T, D), dout.reshape(T, D))
    gw_pp = mm_tn("gw_proj_pool", y_pool.reshape(T, D), dp1.reshape(T, D))
    gw_ps = mm_tn("gw_proj_ssd", y_ssd.reshape(T, D_INNER), dp2.reshape(T, D))

    *dds, dzp, g_pool, g_pool_scale = pool_mix_bwd("pool_mix_bwd", dgs, zp, dyp, pool_full, pool_scale)
    dvs = [pool_diff(f"pool_diff_t{g}", dds[g], 0, g, True) for g in range(4)]

    head_sel = (jnp.arange(D_INNER)[:, None] // HEAD_DIM == jnp.arange(LANES)[None, :]).astype(BF16)
    dy, dzs, g_ssd_norm, g_dskip = gated_norm_bwd(
        "gated_norm_bwd", ys[1], (xbc, D_INNER), zs, dys, ssd_norm, head_sel)

    dxs, dbm, dcm, ddt, dxs_c, dbm_c, ddt_c = [], [], [], [], [], [], []
    g_bias = jnp.zeros((2, N_BC, HPG, 1), F32)
    g_alog = jnp.zeros((2, N_BC, HPG, 1), F32)
    for d in range(2):
        a, b_, c_, t_, gb, ga, dh0 = ssd_bwd(f"ssd_bwd_x{d}", dtT, bias3, alog3, xbc, hs_x[d], dy, zero_state, d,
                                             dx_add=(dxs[0], dskip_lanes) if d == 1 else None)
        dxs.append(a), dbm.append(b_), dcm.append(c_), ddt.append(t_)
        ac, bc, _, tc, gbc, gac, _ = ssd_bwd(f"ssd_bwd_ctx{d}", dtT_c, bias3, alog3, xbc_c, hs_c[d], None, dh0, d)
        dxs_c.append(ac), dbm_c.append(bc), ddt_c.append(tc)
        g_bias = g_bias.at[d].set(jnp.sum(gb, axis=0) + jnp.sum(gbc, axis=0))
        g_alog = g_alog.at[d].set(jnp.sum(ga, axis=0) + jnp.sum(gac, axis=0))

    dxr_xs, gcw_xs, gcb_xs = conv_bwd("conv_bwd_xs", xbc_raw, [dxs[1]], conv_w_full, conv_b, 0, D_INNER)
    dxr_b, gcw_b, gcb_b = conv_bwd("conv_bwd_b", xbc_raw, dbm, conv_w_full, conv_b, D_INNER, N_BC * D_STATE)
    dxr_c, gcw_c, gcb_c = conv_bwd("conv_bwd_c", xbc_raw, dcm, conv_w_full, conv_b, D_INNER + N_BC * D_STATE, N_BC * D_STATE)
    dxr_xs_c, gcw_xs_c, gcb_xs_c = conv_bwd("conv_bwd_xs_ctx", xbc_raw_c, dxs_c, conv_w_full, conv_b, 0, D_INNER)
    dxr_b_c, gcw_b_c, gcb_b_c = conv_bwd("conv_bwd_b_ctx", xbc_raw_c, dbm_c, conv_w_full, conv_b, D_INNER, N_BC * D_STATE)
    g_conv_w = jnp.concatenate([gcw_xs + gcw_xs_c, gcw_b + gcw_b_c, gcw_c], axis=1)
    g_conv_b = jnp.concatenate([gcb_xs + gcb_xs_c, gcb_b + gcb_b_c, gcb_c], axis=1)

    def dt_cols(parts, n_tok):
        t = jnp.concatenate(parts, axis=1).transpose(0, 2, 1).reshape(n_tok, 2 * N_HEADS)
        return jnp.pad(t, ((0, 0), (0, LANES - 2 * N_HEADS))).astype(BF16)

    ddt2, ddt2_c = dt_cols(ddt, T), dt_cols(ddt_c, Tc)
    segs = [jnp.concatenate(dvs, axis=-1).reshape(T, D), dzp.reshape(T, D), dzs.reshape(T, D_INNER), dgp.reshape(T, 2 * D),
            dxr_xs.reshape(T, D_INNER), dxr_b.reshape(T, N_BC * D_STATE), dxr_c.reshape(T, N_BC * D_STATE), ddt2]
    segs_c = {4: dxr_xs_c.reshape(Tc, D_INNER), 5: dxr_b_c.reshape(Tc, N_BC * D_STATE), 7: ddt2_c}
    gw_rows = []
    for i, seg in enumerate(segs):
        init = mm_tn(f"gw_in_ctx{i}", segs_c[i], hc2) if i in segs_c else None
        gw_rows.append(mm_tn(f"gw_in{i}", seg, hx2, init=init))
    gw_rows[-1] = gw_rows[-1][0:2 * N_HEADS]
    gw_inT = jnp.concatenate(gw_rows, axis=0)

    pool_slab = g_pool.reshape(4, N_CHIPS, 64, POOL_GROUP).transpose(1, 0, 2, 3).reshape(N_CHIPS, 64, D)
    slabs = jnp.concatenate([gw_inT.reshape(N_CHIPS, 2320, D), gw_pp.reshape(N_CHIPS, 256, D), gw_ps.reshape(N_CHIPS, 512, D),
                             gw_o.reshape(N_CHIPS, 256, D), pool_slab, jnp.zeros((N_CHIPS, W_SHARD_ROWS - SEG_ROWS[-1], D), F32)], axis=1)
    chip_part = reduce_scatter_chips(slabs)
    d_hx, landed = mm_nn_multi("d_hx", list(zip(segs, w_seg)), F32, tm=1024, tk=512, exchange=chip_part)
    d_hx = d_hx.reshape(Bn, L, D)
    gsh = reduce_scatter_finish(landed)
    d_hc = mm_nn_multi("d_hc", [(segs_c[i], w_seg[i]) for i in sorted(segs_c)], F32).reshape(Bn, Lc, D)

    grad_x, dscale, dshift, g_npre_x = prenorm_bwd("prenorm_bwd_x", x, d_hx, scale, norm_pre, g_res=g_res)
    _, dscale_c, dshift_c, g_npre_c = prenorm_bwd("prenorm_bwd_ctx", ctx, d_hc, scale_c, norm_pre)

    dmod_x = jnp.concatenate([dshift[:, 0], dscale[:, 0], dgate[:, 0]], axis=1)
    dmod_c = jnp.concatenate([jnp.sum(dshift_c[:, 0], axis=0, keepdims=True), jnp.sum(dscale_c[:, 0], axis=0, keepdims=True),
                              jnp.zeros((1, D), F32)], axis=1)
    dmod_own = jnp.pad(dmod_x, ((0, 8 - Bn), (0, 0))) + jnp.pad(dmod_c, ((Bn, 7 - Bn), (0, 0)))

    small_sizes = (D, D, 2 * D, D, CONV_DIM, 2 * N_HEADS, 2 * N_HEADS, N_HEADS, D_INNER, 4 * CONV_DIM, 1)
    pk = _pack([g_npre_x + g_npre_c, g_norm_post, g_b_merge, g_pool_scale, g_conv_b, g_bias, g_alog, g_dskip[0, 0:N_HEADS],
                g_ssd_norm, g_conv_w, loss_part[0, 0:1]], 184)
    dmod_rows = 8 * 3 * D // LANES
    gathered = all_gather_small("gather_small", jnp.concatenate([dmod_own.reshape(dmod_rows, LANES), pk], axis=0))
    dmod_all = gathered[:, 0:dmod_rows].reshape(8 * N_DEV, 3 * D)
    small = sum_devices("sum_small", gathered[:, dmod_rows:])
    row_is_cctx = (jnp.arange(8 * N_DEV) % 8 == Bn).astype(F32)[:, None]
    g_w_ada, g_b_ada, cpart = ada_bwd_shard(
        cond_all, lax.dynamic_slice(dmod_all, (0, chip * ada_cols), (8 * N_DEV, ada_cols)), dmod_all, w_ada[0], row_is_cctx)
    g_c_ctx = cctx_finish(all_gather_small("gather_cctx", cpart), c_ctx[None, :])
    (g_norm_pre, g_norm_post_t, g_b_merge_t, g_pool_scale_t, g_conv_b_t, g_dt_bias, g_a_log, g_d_skip, g_ssd_norm_t,
     g_conv_w_t, loss) = _unpack(small, small_sizes)
    g_conv_w_shard = lax.dynamic_slice(g_conv_w_t.reshape(4, CONV_DIM), (0, chip * cw_cols), (4, cw_cols))

    g_w_in = gsh[SEG_ROWS[0]:SEG_ROWS[1]].T
    g_w_pp, g_w_ps, g_w_o = (gsh[SEG_ROWS[i]:SEG_ROWS[i + 1]] for i in (1, 2, 3))
    g_pool_w = gsh[SEG_ROWS[4]:SEG_ROWS[5]].reshape(256, POOL_GROUP)

    grads = {
        "c_ctx": g_c_ctx.reshape(c_ctx.shape), "w_ada": g_w_ada[None], "b_ada": g_b_ada, "norm_pre": g_norm_pre[None],
        "norm_post": g_norm_post_t[None], "w_in": g_w_in[None], "b_merge": g_b_merge_t[None],
        "pool_w": g_pool_w.reshape(pool_w.shape), "pool_scale": g_pool_scale_t[None], "conv_w": g_conv_w_shard[None],
        "conv_b": g_conv_b_t[None], "dt_bias": g_dt_bias.reshape(dt_bias.shape), "a_log": g_a_log.reshape(a_log.shape),
        "d_skip": g_d_skip[None], "ssd_norm": g_ssd_norm_t[None], "w_proj_pool": g_w_pp[None], "w_proj_ssd": g_w_ps[None],
        "w_out": g_w_o[None]}
    weights = dict(c_ctx=c_ctx, w_ada=w_ada, b_ada=b_ada, norm_pre=norm_pre, norm_post=norm_post, w_in=w_in, b_merge=b_merge,
                   pool_w=pool_w, pool_scale=pool_scale, conv_w=conv_w, conv_b=conv_b, dt_bias=dt_bias, a_log=a_log,
                   d_skip=d_skip, ssd_norm=ssd_norm, w_proj_pool=w_proj_pool, w_proj_ssd=w_proj_ssd, w_out=w_out)
    m_in = dict(c_ctx=m_c_ctx, w_ada=m_w_ada, b_ada=m_b_ada, norm_pre=m_norm_pre, norm_post=m_norm_post, w_in=m_w_in,
                b_merge=m_b_merge, pool_w=m_pool_w, pool_scale=m_pool_scale, conv_w=m_conv_w, conv_b=m_conv_b,
                dt_bias=m_dt_bias, a_log=m_a_log, d_skip=m_d_skip, ssd_norm=m_ssd_norm, w_proj_pool=m_w_proj_pool,
                w_proj_ssd=m_w_proj_ssd, w_out=m_w_out)
    v_in = dict(c_ctx=v_c_ctx, w_ada=v_w_ada, b_ada=v_b_ada, norm_pre=v_norm_pre, norm_post=v_norm_post, w_in=v_w_in,
                b_merge=v_b_merge, pool_w=v_pool_w, pool_scale=v_pool_scale, conv_w=v_conv_w, conv_b=v_conv_b,
                dt_bias=v_dt_bias, a_log=v_a_log, d_skip=v_d_skip, ssd_norm=v_ssd_norm, w_proj_pool=v_w_proj_pool,
                w_proj_ssd=v_w_proj_ssd, w_out=v_w_out)
    names = list(weights)
    big = ("w_ada", "w_in", "pool_w", "w_proj_pool", "w_proj_ssd", "w_out")
    small_names = [n for n in names if n not in big]
    delta, new_m, new_v = {}, {}, {}
    for n in big:
        shape2 = (-1, weights[n].shape[-1])
        d_, m_, v_ = adamw(f"adamw_{n}", weights[n].reshape(shape2), grads[n].reshape(shape2), m_in[n].reshape(shape2),
                           v_in[n].reshape(shape2), tr=128)
        delta[n], new_m[n], new_v[n] = (t.reshape(weights[n].shape) for t in (d_, m_, v_))
    sizes = [weights[n].size for n in small_names]
    packed = [_pack([src[n] for n in small_names], 144) for src in (weights, grads, m_in, v_in)]
    outs = adamw("adamw_small", *packed, tr=144)
    for res, store in zip(outs, (delta, new_m, new_v)):
        for n, piece in zip(small_names, _unpack(res, sizes)):
            store[n] = piece.reshape(weights[n].shape)

    return (loss.reshape(()), grad_x, *[grads[n] for n in names], *[delta[n] for n in names],
            *[new_m[n] for n in names], *[new_v[n] for n in names])
```

```python
import jax
import jax.numpy as jnp
from jax import lax
from jax.experimental import pallas as pl
from jax.experimental.pallas import tpu as pltpu

F32 = jnp.float32
BF16 = jnp.bfloat16
MESH = pl.DeviceIdType.MESH

D = 1024
GRID_W = 64
NORM_EPS = 1e-6
POOL_WINDOWS = (2, 4, 8, 16)
POOL_GROUP = 256
D_INNER = 2048
HEAD_DIM = 64
N_HEADS = 32
D_STATE = 128
N_BC = 4
HPG = N_HEADS // N_BC
GW = HPG * HEAD_DIM
CONV_DIM = 3072
CHUNK = 128
IN_COLS = 9280
N_CHIPS = 4
N_DEV = 8

ADAM_LR = 0.001
ADAM_B1 = 0.9
ADAM_B2 = 0.999
ADAM_EPS = 1e-08
ADAM_WD = 0.01
ADAM_STEP = 10

V7X_VMEM_BYTES = 64 * 1024 * 1024
VMEM_LIMIT = V7X_VMEM_BYTES * 3 // 4
LANES = 128


def _cp(sem=None):
    return pltpu.CompilerParams(dimension_semantics=sem, vmem_limit_bytes=VMEM_LIMIT)


def _dot(a, b):
    return jnp.dot(a, b, preferred_element_type=F32)


def _dot_nt(a, b):
    return lax.dot_general(a, b, (((1,), (1,)), ((), ())), preferred_element_type=F32)


def _dot_tn(a, b):
    return lax.dot_general(a, b, (((0,), (0,)), ((), ())), preferred_element_type=F32)


def _split3(x):
    hi = x.astype(BF16)
    r1 = x - hi.astype(F32)
    mid = r1.astype(BF16)
    lo = (r1 - mid.astype(F32)).astype(BF16)
    return hi, mid, lo


def _dot_exact01(v, sel):
    hi, mid, lo = _split3(v)
    return _dot(hi, sel) + _dot(mid, sel) + _dot(lo, sel)


def _sigmoid(x):
    return jax.nn.sigmoid(x)


def _silu(x):
    return x * _sigmoid(x)


def _dsilu(x):
    s = _sigmoid(x)
    return s * (1.0 + x * (1.0 - s))


def _softplus(x):
    return jnp.maximum(x, 0.0) + jnp.log(1.0 + jnp.exp(-jnp.abs(x)))


def _me():
    return lax.axis_index("x"), lax.axis_index("y"), lax.axis_index("c")


def mm_nt(name, a, b, out_dtype, tm=1024, tn=512, gather=None):
    M, K = a.shape
    N = b.shape[0]
    tm, tn = min(tm, M), min(tn, N)
    assert M % tm == 0 and N % tn == 0, (M, N, tm, tn)
    n_i, n_j = M // tm, N // tn
    has_g = gather is not None
    if has_g:
        half = gather.shape[0] // 2
        assert gather.shape[0] % 32 == 0 and n_i * n_j >= 4

    def body(*refs):
        a_ref, b_ref = refs[0], refs[1]
        if has_g:
            s_ref, o_ref, g_ref, send_sems, recv_sems = refs[2:]
            x, y, c = _me()
            chips = [(1 - x, y), (x, 1 - y), (1 - x, 1 - y)]
            step = pl.program_id(0) * n_j + pl.program_id(1)

            def rows(chip, hc):
                return g_ref.at[2 * chip[0] + chip[1], pl.ds(hc * half, half), :]

            def first(j, chip):
                return pltpu.make_async_remote_copy(
                    src_ref=s_ref.at[pl.ds(c * half, half), :], dst_ref=rows((x, y), c), send_sem=send_sems.at[j],
                    recv_sem=recv_sems.at[j], device_id=(*chip, c), device_id_type=MESH)

            def landed(j, chip, hc):
                return pltpu.make_async_remote_copy(
                    src_ref=rows(chip, hc), dst_ref=rows(chip, hc), send_sem=send_sems.at[j], recv_sem=recv_sems.at[j],
                    device_id=(x, y, 1 - c), device_id_type=MESH)

            @pl.when(step == 0)
            def _():
                for j, chip in enumerate(chips):
                    first(j, chip).start()

            @pl.when(step == (3 * n_i * n_j) // 4)
            def _():
                for j, chip in enumerate(chips):
                    landed(j, chip, c).wait_recv()
                    landed(3 + j, chip, c).start()
        else:
            o_ref = refs[2]

        o_ref[...] = _dot_nt(a_ref[...], b_ref[...]).astype(o_ref.dtype)

        if has_g:
            @pl.when(step == n_i * n_j - 1)
            def _():
                for j, chip in enumerate(chips):
                    landed(3 + j, chip, 1 - c).wait_recv()
                for j, chip in enumerate(chips):
                    first(j, chip).wait_send()
                    landed(3 + j, chip, c).wait_send()

    in_specs = [pl.BlockSpec((tm, K), lambda i, j: (i, 0)), pl.BlockSpec((tn, K), lambda i, j: (j, 0))]
    out_shape = jax.ShapeDtypeStruct((M, N), out_dtype)
    out_specs = pl.BlockSpec((tm, tn), lambda i, j: (i, j))
    if not has_g:
        return pl.pallas_call(body, name=name, out_shape=out_shape, grid=(n_i, n_j), in_specs=in_specs, out_specs=out_specs,
                              compiler_params=_cp(("parallel", "arbitrary")))(a, b)
    out, g = pl.pallas_call(
        body, name=name, out_shape=[out_shape, jax.ShapeDtypeStruct((N_CHIPS, *gather.shape), gather.dtype)], grid=(n_i, n_j),
        in_specs=in_specs + [pl.BlockSpec(memory_space=pl.ANY)], out_specs=[out_specs, pl.BlockSpec(memory_space=pl.ANY)],
        scratch_shapes=[pltpu.SemaphoreType.DMA((6,)), pltpu.SemaphoreType.DMA((6,))],
        compiler_params=_cp(("arbitrary", "arbitrary")))(a, b, gather)
    chip = 2 * lax.axis_index("x") + lax.axis_index("y")
    return out, lax.dynamic_update_index_in_dim(g, gather, chip, 0)


def mm_tn(name, a, b, init=None, tm=1024, tn=1024, tk=1024):
    T, M = a.shape
    N = b.shape[1]
    tm, tn, tk = min(tm, M), min(tn, N), min(tk, T)
    assert M % tm == 0 and N % tn == 0 and T % tk == 0, (M, N, T)
    has_init = init is not None

    def body(*refs):
        if has_init:
            a_ref, b_ref, i_ref, o_ref = refs
        else:
            a_ref, b_ref, o_ref = refs
        k = pl.program_id(2)

        @pl.when(k == 0)
        def _():
            o_ref[...] = i_ref[...] if has_init else jnp.zeros(o_ref.shape, F32)

        o_ref[...] += _dot_tn(a_ref[...], b_ref[...])

    in_specs = [pl.BlockSpec((tk, tm), lambda i, j, k: (k, i)), pl.BlockSpec((tk, tn), lambda i, j, k: (k, j))]
    args = [a, b]
    if has_init:
        in_specs.append(pl.BlockSpec((tm, tn), lambda i, j, k: (i, j)))
        args.append(init)
    return pl.pallas_call(
        body, name=name, out_shape=jax.ShapeDtypeStruct((M, N), F32), grid=(M // tm, N // tn, T // tk),
        in_specs=in_specs, out_specs=pl.BlockSpec((tm, tn), lambda i, j, k: (i, j)),
        compiler_params=_cp(("parallel", "parallel", "arbitrary")))(*args)


def mm_nn_multi(name, pairs, out_dtype, tm=512, tk=512, exchange=None):
    M = pairs[0][0].shape[0]
    N = pairs[0][1].shape[1]
    tm = min(tm, M)
    assert M % tm == 0
    plan = []
    step = 0
    for a, b in pairs:
        K = a.shape[1]
        t = min(tk, K)
        assert K % t == 0 and b.shape == (K, N)
        plan.append((t, step, K // t))
        step += K // t
    nsteps = step
    npairs = len(pairs)

    n_i = M // tm
    has_x = exchange is not None

    def body(*refs):
        if has_x:
            p_ref, o_ref, land_ref, acc, send_sems, recv_sems = refs[2 * npairs:]
        else:
            o_ref, acc = refs[2 * npairs:]
        i, k = pl.program_id(0), pl.program_id(1)

        if has_x:
            x, y, c = _me()
            me_chip = 2 * x + y
            chips = [(1 - x, y), (x, 1 - y), (1 - x, 1 - y)]

            def copy(j, src_chip, dst_chip, to):
                return pltpu.make_async_remote_copy(
                    src_ref=p_ref.at[src_chip], dst_ref=land_ref.at[dst_chip], send_sem=send_sems.at[j], recv_sem=recv_sems.at[j],
                    device_id=(*to, c), device_id_type=MESH)

            @pl.when((i == 0) & (k == 0))
            def _():
                for j, chip in enumerate(chips):
                    copy(j, 2 * chip[0] + chip[1], me_chip, chip).start()

        @pl.when(k == 0)
        def _():
            acc[...] = jnp.zeros(acc.shape, F32)

        for p, (_, first, n) in enumerate(plan):
            @pl.when((k >= first) & (k < first + n))
            def _(p=p):
                acc[...] += _dot(refs[2 * p][...], refs[2 * p + 1][...])

        @pl.when(k == nsteps - 1)
        def _():
            o_ref[...] = acc[...].astype(o_ref.dtype)

        if has_x:
            @pl.when((i == n_i - 1) & (k == nsteps - 1))
            def _():
                for j, chip in enumerate(chips):
                    copy(j, me_chip, 2 * chip[0] + chip[1], chip).wait_recv()
                for j, chip in enumerate(chips):
                    copy(j, 2 * chip[0] + chip[1], me_chip, chip).wait_send()

    in_specs, args = [], []
    for (a, b), (t, first, n) in zip(pairs, plan):
        in_specs.append(pl.BlockSpec((tm, t), lambda i, k, first=first, n=n: (i, jnp.clip(k - first, 0, n - 1))))
        in_specs.append(pl.BlockSpec((t, N), lambda i, k, first=first, n=n: (jnp.clip(k - first, 0, n - 1), 0)))
        args += [a, b]
    out_shape = jax.ShapeDtypeStruct((M, N), out_dtype)
    out_specs = pl.BlockSpec((tm, N), lambda i, k: (i, 0))
    scratch = [pltpu.VMEM((tm, N), F32)]
    if has_x:
        in_specs.append(pl.BlockSpec(memory_space=pl.ANY))
        args.append(exchange)
        out_shape = [out_shape, jax.ShapeDtypeStruct(exchange.shape, exchange.dtype)]
        out_specs = [out_specs, pl.BlockSpec(memory_space=pl.ANY)]
        scratch += [pltpu.SemaphoreType.DMA((3,)), pltpu.SemaphoreType.DMA((3,))]
    res = pl.pallas_call(
        body, name=name, out_shape=out_shape, grid=(n_i, nsteps), in_specs=in_specs, out_specs=out_specs,
        scratch_shapes=scratch, compiler_params=_cp(("arbitrary", "arbitrary")))(*args)
    if not has_x:
        return res
    out, landed = res
    chip = 2 * lax.axis_index("x") + lax.axis_index("y")
    own = lax.dynamic_index_in_dim(exchange, chip, 0, keepdims=True)
    return out, lax.dynamic_update_slice_in_dim(landed, own, chip, 0)


def tok_call(name, body, tiled, perb, glob, out_tiled, out_perb, out_glob, tm=256):
    widths = [t[1] if isinstance(t, tuple) else t.shape[2] for t in tiled]
    tiled = [t[0] if isinstance(t, tuple) else t for t in tiled]
    Bn, L = tiled[0].shape[:2]
    tm = min(tm, L)
    assert L % tm == 0
    n_t, n_p, n_g = len(tiled), len(perb), len(glob)
    o_t, o_p, o_g = len(out_tiled), len(out_perb), len(out_glob)
    n_in = n_t + n_p + n_g

    def kern(*refs):
        ins, outs = refs[:n_in], refs[n_in:]
        b, j = pl.program_id(0), pl.program_id(1)
        vals = [r[0] for r in ins[:n_t + n_p]] + [r[...] for r in ins[n_t + n_p:]]
        res = body(*vals)
        if not isinstance(res, (tuple, list)):
            res = (res,)
        assert len(res) == o_t + o_p + o_g, (name, len(res))
        for r, v in zip(outs[:o_t], res[:o_t]):
            r[0] = v.astype(r.dtype)

        def accum(r, v, first, lead):
            @pl.when(first)
            def _():
                r[...] = jnp.zeros(r.shape, F32)
            if lead:
                r[0] += v
            else:
                r[...] += v

        for r, v in zip(outs[o_t:o_t + o_p], res[o_t:o_t + o_p]):
            accum(r, v, j == 0, True)
        for r, v in zip(outs[o_t + o_p:], res[o_t + o_p:]):
            accum(r, v, (j == 0) & (b == 0), False)

    in_specs = ([pl.BlockSpec((1, tm, w), lambda b, j: (b, j, 0)) for w in widths]
                + [pl.BlockSpec((1, 1, a.shape[2]), lambda b, j: (b, 0, 0)) for a in perb]
                + [pl.BlockSpec(a.shape, lambda b, j: (0, 0), pipeline_mode=pl.Buffered(1)) for a in glob])
    out_shape = ([jax.ShapeDtypeStruct((Bn, L, w), dt) for w, dt in out_tiled]
                 + [jax.ShapeDtypeStruct((Bn, 1, w), F32) for w in out_perb]
                 + [jax.ShapeDtypeStruct(s, F32) for s in out_glob])
    out_specs = ([pl.BlockSpec((1, tm, w), lambda b, j: (b, j, 0)) for w, _ in out_tiled]
                 + [pl.BlockSpec((1, 1, w), lambda b, j: (b, 0, 0)) for w in out_perb]
                 + [pl.BlockSpec(s, lambda b, j: (0, 0)) for s in out_glob])
    return pl.pallas_call(
        kern, name=name, out_shape=out_shape, grid=(Bn, L // tm), in_specs=in_specs, out_specs=out_specs,
        compiler_params=_cp(("arbitrary", "arbitrary")))(*tiled, *perb, *glob)


def slab_call(name, body, slabs, colparams, out_slabs, out_colred, wc=LANES):
    Bn, L = slabs[0][0].shape[:2]
    w_out = out_slabs[0][0]
    assert w_out % wc == 0 and all(off % wc == 0 for _, off in slabs + colparams)
    n_col = w_out // wc
    n_s, n_c = len(slabs), len(colparams)
    o_s = len(out_slabs)

    def kern(*refs):
        ins, outs = refs[:n_s + n_c], refs[n_s + n_c:]
        b = pl.program_id(1)
        vals = [r[0] for r in ins[:n_s]] + [r[...] for r in ins[n_s:]]
        res = body(*vals)
        if not isinstance(res, (tuple, list)):
            res = (res,)
        assert len(res) == o_s + len(out_colred), name
        for r, v in zip(outs[:o_s], res[:o_s]):
            r[0] = v.astype(r.dtype)

        def accum(r, v):
            @pl.when(b == 0)
            def _():
                r[...] = jnp.zeros(r.shape, F32)
            r[...] += v

        for r, v in zip(outs[o_s:], res[o_s:]):
            accum(r, v)

    in_specs = ([pl.BlockSpec((1, L, wc), lambda j, b, o=off // wc: (b, 0, o + j)) for _, off in slabs]
                + [pl.BlockSpec((a.shape[0], wc), lambda j, b, o=off // wc: (0, o + j)) for a, off in colparams])
    out_shape = ([jax.ShapeDtypeStruct((Bn, L, w), dt) for w, dt in out_slabs]
                 + [jax.ShapeDtypeStruct((r, w_out), F32) for r in out_colred])
    out_specs = ([pl.BlockSpec((1, L, wc), lambda j, b: (b, 0, j)) for _ in out_slabs]
                 + [pl.BlockSpec((r, wc), lambda j, b: (0, j)) for r in out_colred])
    return pl.pallas_call(
        kern, name=name, out_shape=out_shape, grid=(n_col, Bn), in_specs=in_specs, out_specs=out_specs,
        compiler_params=_cp(("arbitrary", "arbitrary")))(*[a for a, _ in slabs], *[a for a, _ in colparams])


def _rms_r(x):
    return lax.rsqrt(jnp.mean(x * x, axis=-1, keepdims=True) + NORM_EPS)


def _rms_bwd(dxh, x, r):
    return r * (dxh - x * (r * r) * jnp.mean(dxh * x, axis=-1, keepdims=True))


def _colsum(v):
    return jnp.sum(v, axis=0, keepdims=True)


def _stack_rows(rows):
    n, w = len(rows), rows[0].shape[1]
    sub = lax.broadcasted_iota(jnp.int32, (n, w), 0)
    acc = jnp.zeros((n, w), F32)
    for r, row in enumerate(rows):
        acc = acc + jnp.where(sub == r, jnp.broadcast_to(row, (n, w)), 0.0)
    return acc


def prenorm_fwd(name, x, scale, shift, w_pre):
    def body(x, scale, shift, w):
        n = x * _rms_r(x) * w
        return n * (1.0 + scale) + shift

    return tok_call(name, body, [x], [scale, shift], [w_pre], [(D, BF16)], [], [])[0]


def prenorm_bwd(name, x, dhx, scale, w_pre, g_res=None):
    has_res = g_res is not None

    def body(*v):
        if has_res:
            x, dhx, g, scale, w = v
        else:
            x, dhx, scale, w = v
        r = _rms_r(x)
        xr = x * r
        n = xr * w
        dn = dhx * (1.0 + scale)
        dx = _rms_bwd(dn * w, x, r)
        if has_res:
            dx = dx + g
        return dx, _colsum(dhx * n), _colsum(dhx), _colsum(dn * xr)

    tiled = [x, dhx] + ([g_res] if has_res else [])
    return tok_call(name, body, tiled, [scale], [w_pre], [(D, F32)], [D, D], [(1, D)])


def _shift_rows(x, o, tok, L):
    if o == 0:
        return x
    rolled = pltpu.roll(x, (-o) % L, 0)
    return jnp.where((tok + o >= 0) & (tok + o < L), rolled, 0.0)


def conv_fwd(name, xbc_raw, conv_w, conv_b):
    L = xbc_raw.shape[1]

    def body(x, w, b):
        tok = lax.broadcasted_iota(jnp.int32, x.shape, 0)
        pre = b
        for k in range(4):
            pre = pre + _shift_rows(x, k - 2, tok, L) * w[k:k + 1]
        return _silu(pre)

    return slab_call(name, body, [(xbc_raw, 0)], [(conv_w, 0), (conv_b, 0)], [(CONV_DIM, F32)], [])[0]


CONV_ROWS = 128
CONV_HALO = 8


def _halo_chunks(L, load, work):
    ch, hl = CONV_ROWS, CONV_HALO
    n = L // ch
    assert L % ch == 0
    if n == 1:
        z = jnp.zeros_like(load(0, hl))
        work(0, jnp.concatenate([z, load(0, ch), z], axis=0))
        return
    z = jnp.zeros_like(load(0, hl))
    work(0, jnp.concatenate([z, load(0, ch + hl)], axis=0))

    def step(i, carry):
        start = pl.multiple_of(i * ch, ch)
        work(start, load(pl.multiple_of(start - hl, hl), ch + 2 * hl))
        return carry

    lax.fori_loop(1, n - 1, step, 0)
    work(L - ch, jnp.concatenate([load(L - ch - hl, ch + hl), z], axis=0))


def _rows_at(xh, o):
    return xh if o == 0 else pltpu.roll(xh, (-o) % xh.shape[0], 0)


def conv_bwd(name, xbc_raw, dparts, conv_w, conv_b, col0, width, scaled=None, wc=LANES):
    Bn, L, _ = xbc_raw.shape
    n_d = len(dparts)
    has_s = scaled is not None
    mid = slice(CONV_HALO, CONV_HALO + CONV_ROWS)
    c0 = col0 // wc

    def kern(*refs):
        x_ref, d_refs = refs[0], refs[1:1 + n_d]
        pos = 1 + n_d
        if has_s:
            s_ref, pos = refs[pos], pos + 1
        w_ref, b_ref = refs[pos], refs[pos + 1]
        pos += 2
        if has_s:
            scale = refs[pos][...]
            pos += 1
        dx_ref, dw_ref, db_ref = refs[pos:pos + 3]
        acc = refs[pos + 3]
        w, b = w_ref[...], b_ref[...]
        acc[...] = jnp.zeros(acc.shape, F32)

        def load(s, n):
            dy = d_refs[0][0, pl.ds(s, n), :]
            for r in d_refs[1:]:
                dy = dy + r[0, pl.ds(s, n), :]
            if has_s:
                dy = dy + s_ref[0, pl.ds(s, n), :] * scale
            return jnp.concatenate([x_ref[0, pl.ds(s, n), :], dy], axis=1)

        def work(start, both):
            xh, dyh = both[:, 0:wc], both[:, wc:]
            taps = [_rows_at(xh, k - 2) for k in range(4)]
            pre = b
            for k in range(4):
                pre = pre + taps[k] * w[k:k + 1]
            dpre = dyh * _dsilu(pre)
            dx = dpre * w[2:3]
            for k in (0, 1, 3):
                dx = dx + _rows_at(dpre, 2 - k) * w[k:k + 1]
            dx_ref[0, pl.ds(start, CONV_ROWS), :] = dx[mid].astype(dx_ref.dtype)
            dm = dpre[mid]
            acc[...] += _stack_rows([_colsum(dm * taps[k][mid]) for k in range(4)] + [_colsum(dm)] + [jnp.zeros((1, wc), F32)] * 3)

        _halo_chunks(L, load, work)
        first = pl.program_id(1) == 0

        @pl.when(first)
        def _():
            dw_ref[...] = acc[0:4]
            db_ref[...] = acc[4:5]

        @pl.when(jnp.logical_not(first))
        def _():
            dw_ref[...] += acc[0:4]
            db_ref[...] += acc[4:5]

    slab = lambda off: pl.BlockSpec((1, L, wc), lambda j, b, off=off: (b, 0, off + j))
    in_specs = [slab(c0)] + [slab(0)] * n_d + ([slab(0)] if has_s else [])
    in_specs += [pl.BlockSpec((4, wc), lambda j, b: (0, c0 + j)), pl.BlockSpec((1, wc), lambda j, b: (0, c0 + j))]
    args = [xbc_raw, *dparts] + ([scaled[0]] if has_s else []) + [conv_w, conv_b]
    if has_s:
        in_specs.append(pl.BlockSpec((1, wc), lambda j, b: (0, j)))
        args.append(scaled[1])
    return pl.pallas_call(
        kern, name=name,
        out_shape=[jax.ShapeDtypeStruct((Bn, L, width), BF16), jax.ShapeDtypeStruct((4, width), F32), jax.ShapeDtypeStruct((1, width), F32)],
        grid=(width // wc, Bn), in_specs=in_specs,
        out_specs=[pl.BlockSpec((1, L, wc), lambda j, b: (b, 0, j)), pl.BlockSpec((4, wc), lambda j, b: (0, j)),
                   pl.BlockSpec((1, wc), lambda j, b: (0, j))],
        scratch_shapes=[pltpu.VMEM((8, wc), F32)],
        compiler_params=_cp(("arbitrary", "arbitrary")))(*args)


def _box_mean(x, k, step, pos, n, L, transpose):
    lo, hi = k // 2, k - 1 - k // 2
    cnt = (jnp.minimum(pos + hi + 1, n) - jnp.maximum(pos - lo, 0)).astype(F32)
    if transpose:
        x = x / cnt
        lo, hi = hi, lo
    acc = x
    for o in range(-lo, hi + 1):
        if o == 0:
            continue
        rolled = pltpu.roll(x, (-o * step) % L, 0)
        acc = acc + jnp.where((pos + o >= 0) & (pos + o < n), rolled, 0.0)
    return acc if transpose else acc / cnt


def pool_diff(name, v, col0, gi, transpose):
    L = v.shape[1]
    rows = L // GRID_W
    k = POOL_WINDOWS[gi]

    def body(x):
        tok = lax.broadcasted_iota(jnp.int32, x.shape, 0)
        col = tok & (GRID_W - 1)
        row = tok >> 6
        if not transpose:
            m = _box_mean(x, k, GRID_W, row, rows, L, False)
            m = _box_mean(m, k, 1, col, GRID_W, L, False)
        else:
            m = _box_mean(x, k, 1, col, GRID_W, L, True)
            m = _box_mean(m, k, GRID_W, row, rows, L, True)
        return m - x

    return slab_call(name, body, [(v, col0)], [], [(POOL_GROUP, BF16)], [])[0]


def pool_mix_fwd(name, dgs, z_pool, pool_w, pool_scale):
    def body(d0, d1, d2, d3, z, w, scale):
        q = jnp.concatenate([_dot(d, w[g * POOL_GROUP:(g + 1) * POOL_GROUP]) for g, d in enumerate((d0, d1, d2, d3))], axis=1)
        return q * scale * _silu(z)

    return tok_call(name, body, list(dgs) + [z_pool], [], [pool_w, pool_scale], [(D, BF16)], [], [])[0]


def pool_mix_bwd(name, dgs, z_pool, dyp, pool_w, pool_scale):
    def body(d0, d1, d2, d3, z, dyp, w, scale):
        ds = (d0, d1, d2, d3)
        q = jnp.concatenate([_dot(d, w[g * POOL_GROUP:(g + 1) * POOL_GROUP]) for g, d in enumerate(ds)], axis=1)
        dypm = dyp * _silu(z)
        dz = dyp * (q * scale) * _dsilu(z)
        dq = (dypm * scale).astype(BF16)
        dds, gws = [], []
        for g, d in enumerate(ds):
            dqg = dq[:, g * POOL_GROUP:(g + 1) * POOL_GROUP]
            dds.append(_dot_nt(dqg, w[g * POOL_GROUP:(g + 1) * POOL_GROUP]))
            gws.append(_dot_tn(d, dqg))
        return (*dds, dz, jnp.concatenate(gws, axis=0), _colsum(dypm * q))

    return tok_call(name, body, list(dgs) + [z_pool, dyp], [], [pool_w, pool_scale],
                    [(POOL_GROUP, F32)] * 4 + [(D, BF16)], [], [(D, POOL_GROUP), (1, D)])


def _cumsum_lanes(a, reverse):
    n = a.shape[1]
    k = lax.broadcasted_iota(jnp.int32, (n, n), 0)
    i = lax.broadcasted_iota(jnp.int32, (n, n), 1)
    tri = jnp.where((k >= i) if reverse else (k <= i), 1.0, 0.0).astype(BF16)
    return _dot_exact01(a, tri)


def _rows_to_cols(rows):
    r = rows.shape[0]
    if r < LANES:
        rows = jnp.concatenate([rows, jnp.zeros((LANES - r, rows.shape[1]), F32)], axis=0)
    return rows.T


def _cols_to_rows(cols):
    q = cols[0].shape[0]
    lane = lax.broadcasted_iota(jnp.int32, (q, LANES), 1)
    acc = jnp.zeros((q, LANES), F32)
    for r, c in enumerate(cols):
        acc = acc + jnp.where(lane == r, c, 0.0)
    return acc.T[0:len(cols)]


def _ssd_scalars(dtraw, bias, alog, reverse):
    dt = _softplus(dtraw + bias)
    A = -jnp.exp(alog)
    cs = _cumsum_lanes(dt * A, reverse)
    total = cs[:, 0:1] if reverse else cs[:, CHUNK - 1:CHUNK]
    return dt, A, cs, total


def _tri_mask(transposed, reverse):
    sub = lax.broadcasted_iota(jnp.int32, (CHUNK, CHUNK), 0)
    lane = lax.broadcasted_iota(jnp.int32, (CHUNK, CHUNK), 1)
    i, j = (lane, sub) if transposed else (sub, lane)
    return (i <= j) if reverse else (i >= j)


GPS = 4


def ssd_fwd(name, dtT, bias, alog, xbc, h0, direction, with_y, y_add=None):
    Bn, L = xbc.shape[:2]
    nc = L // CHUNK
    reverse = direction == 1
    blk0 = direction * (N_BC // GPS)
    gs = range(GPS)
    has_add = y_add is not None

    def chunk_of(s):
        return (nc - 1 - s) if reverse else s

    def kern(dt_ref, bias_ref, alog_ref, x_ref, b_ref, c_ref, h0_ref, *rest):
        if has_add:
            yp_ref, dsk_ref, rest = rest[0], rest[1], rest[2:]
        if with_y:
            y_ref, hs_ref, hf_ref, h_scr = rest
        else:
            hs_ref, hf_ref, h_scr = rest
        s = pl.program_id(2)

        @pl.when(s == 0)
        def _():
            h_scr[...] = h0_ref[0]

        first = lax.broadcasted_iota(jnp.int32, (1, LANES), 1) < HEAD_DIM
        heads = range(HPG)
        psl = [slice((r // 2) * LANES, (r // 2 + 1) * LANES) for r in heads]
        keep = _tri_mask(False, reverse)
        sc, x_bf, bm, h, h_bf, bt, cm, cb, cs_cols = [], [], [], [], [], [], [], [], []
        for g in gs:
            dt, _, cs, total = _ssd_scalars(dt_ref[0, g * HPG:(g + 1) * HPG], bias_ref[g], alog_ref[g], reverse)
            u = cs - jnp.log(dt)
            sc.append((cs, u, jnp.exp(total - u), jnp.exp(total)))
            x_bf.append(x_ref[0, :, g * GW:(g + 1) * GW].astype(BF16))
            bm.append(b_ref[0, :, g * D_STATE:(g + 1) * D_STATE])
            h.append(h_scr[g])
            h_bf.append(h[g].astype(BF16))
            hs_ref[0, g, 0] = h[g]
            bt.append(bm[g].T)
            if with_y:
                cm.append(c_ref[0, :, g * D_STATE:(g + 1) * D_STATE])
                cb.append(_dot_nt(cm[g].astype(BF16), bm[g].astype(BF16)))
                cs_cols.append(_rows_to_cols(cs))
        lhs = [[] for _ in gs]
        if with_y:
            for g in gs:
                cs, u = sc[g][0], sc[g][1]
                for r in heads:
                    cs_col = jnp.broadcast_to(cs_cols[g][:, r:r + 1], (CHUNK, LANES))
                    wf = cb[g] * jnp.exp(jnp.where(keep, cs_col - u[r:r + 1], -jnp.inf))
                    lhs[g].append(jnp.concatenate([wf.astype(BF16), (cm[g] * jnp.exp(cs_col)).astype(BF16)], axis=1))
        bts = [[(bt[g] * sc[g][2][r:r + 1]).astype(BF16) for r in heads] for g in gs]
        sts = [[_dot(bts[g][r], x_bf[g][:, psl[r]]) for r in heads] for g in gs]
        if with_y:
            ys = [[_dot(lhs[g][r], jnp.concatenate([x_bf[g][:, psl[r]], h_bf[g][:, psl[r]]], axis=0)) for r in heads] for g in gs]
        for g in gs:
            dc = sc[g][3]
            for p in range(HPG // 2):
                if with_y:
                    cols = slice(g * GW + p * LANES, g * GW + (p + 1) * LANES)
                    yv = jnp.where(first, ys[g][2 * p], ys[g][2 * p + 1])
                    if has_add:
                        yv = yv + yp_ref[0, :, cols] + dsk_ref[:, cols] * x_ref[0, :, cols]
                    y_ref[0, :, cols] = yv
                dc_p = jnp.where(first, dc[2 * p:2 * p + 1], dc[2 * p + 1:2 * p + 2])
                h_scr[g, :, psl[2 * p]] = h[g][:, psl[2 * p]] * dc_p + jnp.where(first, sts[g][2 * p], sts[g][2 * p + 1])

        @pl.when(s == nc - 1)
        def _():
            hf_ref[0] = h_scr[...]

    nb = D_INNER // (GPS * D_STATE)
    in_specs = [
        pl.BlockSpec((1, GPS * HPG, CHUNK), lambda b, g, s: (b, blk0 + g, chunk_of(s))),
        pl.BlockSpec((GPS, HPG, 1), lambda b, g, s: (blk0 + g, 0, 0)),
        pl.BlockSpec((GPS, HPG, 1), lambda b, g, s: (blk0 + g, 0, 0)),
        pl.BlockSpec((1, CHUNK, GPS * GW), lambda b, g, s: (b, chunk_of(s), g)),
        pl.BlockSpec((1, CHUNK, GPS * D_STATE), lambda b, g, s: (b, chunk_of(s), nb + g)),
        pl.BlockSpec((1, CHUNK, GPS * D_STATE), lambda b, g, s: (b, chunk_of(s), nb + N_BC // GPS + g)),
        pl.BlockSpec((1, GPS, D_STATE, GW), lambda b, g, s: (b, g, 0, 0)),
    ]
    args = [dtT, bias, alog, xbc, xbc, xbc, h0]
    if has_add:
        in_specs += [pl.BlockSpec((1, CHUNK, GPS * GW), lambda b, g, s: (b, chunk_of(s), g)),
                     pl.BlockSpec((1, GPS * GW), lambda b, g, s: (0, g))]
        args += list(y_add)
    out_shape, out_specs = [], []
    if with_y:
        out_shape.append(jax.ShapeDtypeStruct((Bn, L, D_INNER), F32))
        out_specs.append(pl.BlockSpec((1, CHUNK, GPS * GW), lambda b, g, s: (b, chunk_of(s), g)))
    out_shape += [jax.ShapeDtypeStruct((Bn, N_BC, nc, D_STATE, GW), F32), jax.ShapeDtypeStruct((Bn, N_BC, D_STATE, GW), F32)]
    out_specs += [pl.BlockSpec((1, GPS, 1, D_STATE, GW), lambda b, g, s: (b, g, chunk_of(s), 0, 0)),
                  pl.BlockSpec((1, GPS, D_STATE, GW), lambda b, g, s: (b, g, 0, 0))]
    return pl.pallas_call(
        kern, name=name, out_shape=out_shape, grid=(Bn, N_BC // GPS, nc), in_specs=in_specs, out_specs=out_specs,
        scratch_shapes=[pltpu.VMEM((GPS, D_STATE, GW), F32)],
        compiler_params=_cp(("arbitrary", "arbitrary", "arbitrary")))(*args)


def ssd_bwd(name, dtT, bias, alog, xbc, h_start, dy, dh_final, direction, dx_add=None):
    Bn, L = xbc.shape[:2]
    nc = L // CHUNK
    reverse = direction == 1
    blk0 = direction * (N_BC // GPS)
    has_y = dy is not None
    has_add = dx_add is not None
    assert has_y or not has_add
    last = 0 if reverse else CHUNK - 1
    gs = range(GPS)

    def chunk_of(s):
        return s if reverse else (nc - 1 - s)

    def kern(*refs):
        if has_add:
            dxp_ref, dsk_ref = refs[9], refs[10]
            refs = refs[:9] + refs[11:]
        if has_y:
            (dt_ref, bias_ref, alog_ref, x_ref, b_ref, hs_ref, dhf_ref, c_ref, dy_ref,
             dx_ref, db_ref, ddt_ref, dbias_ref, dalog_ref, dh0_ref, dc_ref, dh_scr) = refs
        else:
            (dt_ref, bias_ref, alog_ref, x_ref, b_ref, hs_ref, dhf_ref,
             dx_ref, db_ref, ddt_ref, dbias_ref, dalog_ref, dh0_ref, dh_scr) = refs
        s = pl.program_id(2)

        @pl.when(s == 0)
        def _():
            dh_scr[...] = dhf_ref[0]
            dbias_ref[...] = jnp.zeros(dbias_ref.shape, F32)
            dalog_ref[...] = jnp.zeros(dalog_ref.shape, F32)

        first = lax.broadcasted_iota(jnp.int32, (1, LANES), 1) < HEAD_DIM
        heads = range(HPG)
        psl = [slice((r // 2) * LANES, (r // 2 + 1) * LANES) for r in heads]
        mine = [first if r % 2 == 0 else jnp.logical_not(first) for r in heads]
        zeros_bf = jnp.zeros((CHUNK, LANES), BF16)
        keep = _tri_mask(True, reverse)
        ctx = []
        for g in gs:
            dtraw = dt_ref[0, g * HPG:(g + 1) * HPG]
            dt, A, cs, total = _ssd_scalars(dtraw, bias_ref[g], alog_ref[g], reverse)
            u = cs - jnp.log(dt)
            c = dict(dtraw=dtraw, dt=dt, A=A, cs=cs, total=total, u=u, dtt=jnp.exp(total - u), dcy=jnp.exp(total),
                     u_cols=_rows_to_cols(u), x_bf=x_ref[0, :, g * GW:(g + 1) * GW].astype(BF16),
                     bm=b_ref[0, :, g * D_STATE:(g + 1) * D_STATE], h=hs_ref[0, g, 0], dh=dh_scr[g])
            c["bt"] = c["bm"].T
            c["dh_bf"] = c["dh"].astype(BF16)
            if has_y:
                c["cm"] = c_ref[0, :, g * D_STATE:(g + 1) * D_STATE]
                c["ct"] = c["cm"].T
                c["e_row"] = jnp.exp(cs)
                c["dy_bf"] = dy_ref[0, :, g * GW:(g + 1) * GW].astype(BF16)
                c["h_bf"] = c["h"].astype(BF16)
                c["cbt"] = _dot_nt(c["bm"].astype(BF16), c["cm"].astype(BF16))
            ctx.append(c)
        for c in ctx:
            c["lhs"], c["et"] = [], []
            for r in heads:
                u_col = jnp.broadcast_to(c["u_cols"][:, r:r + 1], (CHUNK, LANES))
                bs = (c["bm"] * jnp.exp(c["total"][r:r + 1] - u_col)).astype(BF16)
                if has_y:
                    et = jnp.exp(jnp.where(keep, c["cs"][r:r + 1] - u_col, -jnp.inf))
                    c["et"].append(et)
                    c["lhs"].append(jnp.concatenate([(c["cbt"] * et).astype(BF16), bs], axis=1))
                else:
                    c["lhs"].append(bs)
        for c in ctx:
            c["p2raw"] = [_dot_nt(c["dh_bf"][:, psl[r]], jnp.where(mine[r], c["x_bf"][:, psl[r]], zeros_bf)) for r in heads]
            if has_y:
                c["a1"] = [_dot_nt(jnp.concatenate([c["x_bf"][:, psl[r]], c["h_bf"][:, psl[r]]], axis=0),
                                   jnp.where(mine[r], c["dy_bf"][:, psl[r]], zeros_bf)) for r in heads]
                c["news"] = [_dot((c["ct"] * c["e_row"][r:r + 1]).astype(BF16), c["dy_bf"][:, psl[r]]) for r in heads]
                c["dxs"] = [_dot(c["lhs"][r], jnp.concatenate([c["dy_bf"][:, psl[r]], c["dh_bf"][:, psl[r]]], axis=0)) for r in heads]
            else:
                c["dxs"] = [_dot(c["lhs"][r], c["dh_bf"][:, psl[r]]) for r in heads]
        for g, c in enumerate(ctx):
            dbt = jnp.zeros((D_STATE, CHUNK), F32)
            dcbt = jnp.zeros((CHUNK, CHUNK), F32)
            dct = jnp.zeros((D_STATE, CHUNK), F32)
            tots, out_rows, in_rows, in_cols = [], [], [], []
            for r in heads:
                if has_y:
                    pt = c["a1"][r][0:CHUNK] * c["et"][r]
                    dcbt = dcbt + pt
                    mt = pt * c["cbt"]
                    ph = c["a1"][r][CHUNK:] * c["e_row"][r:r + 1]
                    dct = dct + ph
                    out_rows.append(_colsum(mt + c["ct"] * ph))
                    in_cols.append(jnp.sum(mt, axis=1, keepdims=True))
                p2 = c["p2raw"][r] * c["dtt"][r:r + 1]
                dbt = dbt + p2
                t_term = _colsum(c["bt"] * p2)
                in_rows.append(t_term)
                hdh = c["h"][:, psl[r]] * c["dh"][:, psl[r]]
                tot = jnp.sum(t_term, axis=1, keepdims=True) + c["dcy"][r:r + 1] * jnp.sum(jnp.where(mine[r], hdh, 0.0), keepdims=True)
                tots.append(jnp.broadcast_to(tot, (1, CHUNK)))
            for p in range(HPG // 2):
                cols = slice(g * GW + p * LANES, g * GW + (p + 1) * LANES)
                dxv = jnp.where(first, c["dxs"][2 * p], c["dxs"][2 * p + 1])
                if has_add:
                    dxv = dxv + dxp_ref[0, :, cols] + dsk_ref[:, cols] * dy_ref[0, :, cols]
                dx_ref[0, :, cols] = dxv
                new = c["dh"][:, psl[2 * p]] * jnp.where(first, c["dcy"][2 * p:2 * p + 1], c["dcy"][2 * p + 1:2 * p + 2])
                if has_y:
                    new = new + jnp.where(first, c["news"][2 * p], c["news"][2 * p + 1])
                dh_scr[g, :, psl[2 * p]] = new
            db = dbt.T
            if has_y:
                dcbt_bf = dcbt.astype(BF16)
                db = db + _dot(dcbt_bf, c["cm"].astype(BF16))
                dc_ref[0, :, g * D_STATE:(g + 1) * D_STATE] = dct.T + _dot_tn(dcbt_bf, c["bm"].astype(BF16))
            db_ref[0, :, g * D_STATE:(g + 1) * D_STATE] = db
            s_row = _stack_rows(in_rows)
            lane = lax.broadcasted_iota(jnp.int32, (HPG, CHUNK), 1)
            dcs = jnp.where(lane == last, _stack_rows(tots), 0.0)
            if has_y:
                s_row = s_row + _cols_to_rows(in_cols)
                dcs = dcs + _stack_rows(out_rows)
            dcs = dcs - s_row
            da = _cumsum_lanes(dcs, not reverse)
            ddt = da * c["A"] + jnp.where(c["dt"] > 0.0, s_row / c["dt"], 0.0)
            ddtraw = ddt * _sigmoid(c["dtraw"] + bias_ref[g])
            ddt_ref[0, g * HPG:(g + 1) * HPG] = ddtraw
            dbias_ref[0, g] += jnp.sum(ddtraw, axis=1, keepdims=True)
            dalog_ref[0, g] += jnp.sum(da * c["dt"], axis=1, keepdims=True) * c["A"]

        @pl.when(s == nc - 1)
        def _():
            dh0_ref[0] = dh_scr[...]

    nb = D_INNER // (GPS * D_STATE)
    cidx = lambda b, g, s: (b, chunk_of(s), g)
    hidx = lambda b, g, s: (b, g, 0, 0)
    in_specs = [
        pl.BlockSpec((1, GPS * HPG, CHUNK), lambda b, g, s: (b, blk0 + g, chunk_of(s))),
        pl.BlockSpec((GPS, HPG, 1), lambda b, g, s: (blk0 + g, 0, 0)),
        pl.BlockSpec((GPS, HPG, 1), lambda b, g, s: (blk0 + g, 0, 0)),
        pl.BlockSpec((1, CHUNK, GPS * GW), cidx),
        pl.BlockSpec((1, CHUNK, GPS * D_STATE), lambda b, g, s: (b, chunk_of(s), nb + g)),
        pl.BlockSpec((1, GPS, 1, D_STATE, GW), lambda b, g, s: (b, g, chunk_of(s), 0, 0)),
        pl.BlockSpec((1, GPS, D_STATE, GW), hidx),
    ]
    args = [dtT, bias, alog, xbc, xbc, h_start, dh_final]
    if has_y:
        in_specs += [pl.BlockSpec((1, CHUNK, GPS * D_STATE), lambda b, g, s: (b, chunk_of(s), nb + N_BC // GPS + g)),
                     pl.BlockSpec((1, CHUNK, GPS * GW), cidx)]
        args += [xbc, dy]
    if has_add:
        in_specs += [pl.BlockSpec((1, CHUNK, GPS * GW), cidx), pl.BlockSpec((1, GPS * GW), lambda b, g, s: (0, g))]
        args += list(dx_add)
    out_shape = [jax.ShapeDtypeStruct((Bn, L, D_INNER), F32), jax.ShapeDtypeStruct((Bn, L, N_BC * D_STATE), F32),
                 jax.ShapeDtypeStruct((Bn, N_HEADS, L), F32), jax.ShapeDtypeStruct((Bn, N_BC, HPG, 1), F32),
                 jax.ShapeDtypeStruct((Bn, N_BC, HPG, 1), F32), jax.ShapeDtypeStruct((Bn, N_BC, D_STATE, GW), F32)]
    out_specs = [pl.BlockSpec((1, CHUNK, GPS * GW), cidx), pl.BlockSpec((1, CHUNK, GPS * D_STATE), cidx),
                 pl.BlockSpec((1, GPS * HPG, CHUNK), lambda b, g, s: (b, g, chunk_of(s))),
                 pl.BlockSpec((1, GPS, HPG, 1), hidx), pl.BlockSpec((1, GPS, HPG, 1), hidx), pl.BlockSpec((1, GPS, D_STATE, GW), hidx)]
    if has_y:
        out_shape.append(jax.ShapeDtypeStruct((Bn, L, N_BC * D_STATE), F32))
        out_specs.append(pl.BlockSpec((1, CHUNK, GPS * D_STATE), cidx))
    res = pl.pallas_call(
        kern, name=name, out_shape=out_shape, grid=(Bn, N_BC // GPS, nc), in_specs=in_specs, out_specs=out_specs,
        scratch_shapes=[pltpu.VMEM((GPS, D_STATE, GW), F32)],
        compiler_params=_cp(("arbitrary", "arbitrary", "arbitrary")))(*args)
    dxs, db, ddt, dbias, dalog, dh0 = res[:6]
    return dxs, db, (res[6] if has_y else None), ddt, dbias, dalog, dh0


def _group_mean(v):
    gw = D_INNER // N_BC
    parts = [jnp.broadcast_to(jnp.mean(v[:, g * gw:(g + 1) * gw], axis=-1, keepdims=True), (v.shape[0], gw)) for g in range(N_BC)]
    return jnp.concatenate(parts, axis=1)


def gated_norm_fwd(name, y, z, w_norm):
    def body(y, z, w):
        u = y * _silu(z)
        r = lax.rsqrt(_group_mean(u * u) + NORM_EPS)
        return u * r * w

    return tok_call(name, body, [y, z], [], [w_norm], [(D_INNER, BF16)], [], [])[0]


def gated_norm_bwd(name, y, xs_src, z, d_out, w_norm, head_sel):
    def body(y, xs, z, do, w, sel):
        sz = _silu(z)
        u = y * sz
        r = lax.rsqrt(_group_mean(u * u) + NORM_EPS)
        duh = do * w
        du = r * (duh - u * (r * r) * _group_mean(duh * u))
        dy = du * sz
        dz = du * y * _dsilu(z)
        dsk_heads = _dot_exact01(jnp.broadcast_to(_colsum(dy * xs), (8, D_INNER)), sel)
        return dy, dz, _colsum(do * u * r), dsk_heads

    return tok_call(name, body, [y, xs_src, z, d_out], [], [w_norm, head_sel],
                    [(D_INNER, F32), (D_INNER, BF16)], [], [(1, D_INNER), (8, LANES)], tm=128)


def merge_fwd(name, y_pool, y_ssd, gatepre, x, target, gate, b_merge, norm_post, w_pp, w_ps, w_out):
    def body(yp, ys, gp, x, tgt, gate, bm, wpost, w_pp, w_ps, w_out):
        p1 = _dot(yp, w_pp)
        p2 = _dot(ys, w_ps)
        gates = _sigmoid(gp + bm)
        merged = gates[:, :D] * p1 + gates[:, D:] * p2
        out = _dot(merged.astype(BF16), w_out)
        r = _rms_r(out)
        outr = out * r
        nq = outr * wpost
        err = x + gate * nq - tgt
        loss = 0.5 * jnp.sum(jnp.mean(err * err, axis=-1, keepdims=True), keepdims=True).reshape(1, 1)
        g = err * (1.0 / D)
        dnq = g * gate
        dout = _rms_bwd(dnq * wpost, out, r)
        return merged, p1, p2, dout, g, _colsum(g * nq), _colsum(dnq * outr), jnp.broadcast_to(loss, (1, LANES))

    return tok_call(name, body, [y_pool, y_ssd, gatepre, x, target], [gate], [b_merge, norm_post, w_pp, w_ps, w_out],
                    [(D, BF16), (D, F32), (D, F32), (D, BF16), (D, F32)], [D], [(1, D), (1, LANES)])


def merge_bwd(name, dout, gatepre, p1, p2, b_merge, w_pp, w_ps, w_out):
    def body(dout, gp, p1, p2, bm, w_pp, w_ps, w_out):
        dmerged = _dot_nt(dout, w_out)
        gates = _sigmoid(gp + bm)
        g1, g2 = gates[:, :D], gates[:, D:]
        dp1 = (dmerged * g1).astype(BF16)
        dp2 = (dmerged * g2).astype(BF16)
        dgp = jnp.concatenate([dmerged * p1 * g1 * (1.0 - g1), dmerged * p2 * g2 * (1.0 - g2)], axis=1)
        return dp1, dp2, dgp, _dot_nt(dp1, w_pp), _dot_nt(dp2, w_ps), _colsum(dgp)

    return tok_call(name, body, [dout, gatepre, p1, p2], [], [b_merge, w_pp, w_ps, w_out],
                    [(D, BF16), (D, BF16), (2 * D, BF16), (D, F32), (D_INNER, F32)], [], [(1, 2 * D)])


def _adamw_math(w, g, m, v):
    m = ADAM_B1 * m + (1.0 - ADAM_B1) * g
    v = ADAM_B2 * v + (1.0 - ADAM_B2) * (g * g)
    m_hat = m / (1.0 - ADAM_B1 ** ADAM_STEP)
    v_hat = v / (1.0 - ADAM_B2 ** ADAM_STEP)
    delta = -ADAM_LR * (m_hat / (jnp.sqrt(v_hat) + ADAM_EPS) + ADAM_WD * w)
    return delta, m, v


def adamw(name, w, g, m, v, tr=256):
    R, C = w.shape
    tr = min(tr, R)
    assert R % tr == 0

    def body(w_ref, g_ref, m_ref, v_ref, d_ref, nm_ref, nv_ref):
        d, nm, nv = _adamw_math(w_ref[...], g_ref[...], m_ref[...], v_ref[...])
        d_ref[...] = d
        nm_ref[...] = nm
        nv_ref[...] = nv

    spec = pl.BlockSpec((tr, C), lambda i: (i, 0))
    return pl.pallas_call(
        body, name=name, out_shape=[jax.ShapeDtypeStruct((R, C), F32)] * 3, grid=(R // tr,),
        in_specs=[spec] * 4, out_specs=[spec] * 3, compiler_params=_cp(("parallel",)))(w, g, m, v)


def all_gather_small(name, v):
    R, C = v.shape

    def body(v_ref, out_ref, send_sems, recv_sems, local_sem):
        x, y, c = _me()
        me = 4 * x + 2 * y + c
        mine = pltpu.make_async_copy(v_ref, out_ref.at[me], local_sem)
        mine.start()
        copies = []
        for d in range(1, N_DEV):
            dx, dy, dc = d // 4, (d // 2) % 2, d % 2
            px, py, pc = x ^ dx, y ^ dy, c ^ dc
            copies.append(pltpu.make_async_remote_copy(
                src_ref=v_ref, dst_ref=out_ref.at[me], send_sem=send_sems.at[d - 1], recv_sem=recv_sems.at[d - 1],
                device_id=(px, py, pc), device_id_type=MESH))
        for cp in copies:
            cp.start()
        for d in range(1, N_DEV):
            dx, dy, dc = d // 4, (d // 2) % 2, d % 2
            peer = 4 * (x ^ dx) + 2 * (y ^ dy) + (c ^ dc)
            pltpu.make_async_remote_copy(
                src_ref=v_ref, dst_ref=out_ref.at[peer], send_sem=send_sems.at[d - 1], recv_sem=recv_sems.at[d - 1],
                device_id=(x ^ dx, y ^ dy, c ^ dc), device_id_type=MESH).wait_recv()
        for cp in copies:
            cp.wait_send()
        mine.wait()

    return pl.pallas_call(
        body, name=name, out_shape=jax.ShapeDtypeStruct((N_DEV, R, C), F32),
        in_specs=[pl.BlockSpec(memory_space=pltpu.VMEM)], out_specs=pl.BlockSpec(memory_space=pltpu.VMEM),
        scratch_shapes=[pltpu.SemaphoreType.DMA((N_DEV - 1,)), pltpu.SemaphoreType.DMA((N_DEV - 1,)), pltpu.SemaphoreType.DMA],
        compiler_params=pltpu.CompilerParams(vmem_limit_bytes=VMEM_LIMIT))(v)


def all_gather_chips(name, shard):
    R, C = shard.shape
    half = R // 2
    assert R % 32 == 0

    def body(s_ref, out_ref, send_sems, recv_sems):
        x, y, c = _me()
        chips = [(1 - x, y), (x, 1 - y), (1 - x, 1 - y)]

        def rows(chip, hc):
            return out_ref.at[2 * chip[0] + chip[1], pl.ds(hc * half, half), :]

        first = [pltpu.make_async_remote_copy(
            src_ref=s_ref.at[pl.ds(c * half, half), :], dst_ref=rows((x, y), c), send_sem=send_sems.at[j],
            recv_sem=recv_sems.at[j], device_id=(*chip, c), device_id_type=MESH) for j, chip in enumerate(chips)]
        for cp in first:
            cp.start()
        passed = [pltpu.make_async_remote_copy(
            src_ref=rows(chip, c), dst_ref=rows(chip, c), send_sem=send_sems.at[3 + j], recv_sem=recv_sems.at[3 + j],
            device_id=(x, y, 1 - c), device_id_type=MESH) for j, chip in enumerate(chips)]
        for j, chip in enumerate(chips):
            pltpu.make_async_remote_copy(
                src_ref=rows(chip, c), dst_ref=rows(chip, c), send_sem=send_sems.at[j], recv_sem=recv_sems.at[j],
                device_id=(*chip, c), device_id_type=MESH).wait_recv()
            passed[j].start()
        for j, chip in enumerate(chips):
            pltpu.make_async_remote_copy(
                src_ref=rows(chip, 1 - c), dst_ref=rows(chip, 1 - c), send_sem=send_sems.at[3 + j], recv_sem=recv_sems.at[3 + j],
                device_id=(x, y, 1 - c), device_id_type=MESH).wait_recv()
        for cp in first + passed:
            cp.wait_send()

    out = pl.pallas_call(
        body, name=name, out_shape=jax.ShapeDtypeStruct((N_CHIPS, R, C), shard.dtype),
        in_specs=[pl.BlockSpec(memory_space=pl.ANY)], out_specs=pl.BlockSpec(memory_space=pl.ANY),
        scratch_shapes=[pltpu.SemaphoreType.DMA((6,)), pltpu.SemaphoreType.DMA((6,))],
        compiler_params=pltpu.CompilerParams(vmem_limit_bytes=VMEM_LIMIT))(shard)
    chip = 2 * lax.axis_index("x") + lax.axis_index("y")
    return lax.dynamic_update_index_in_dim(out, shard, chip, 0)


def sibling_swap(name, v):
    def body(v_ref, out_ref, send_sem, recv_sem):
        x, y, c = _me()
        cp = pltpu.make_async_remote_copy(src_ref=v_ref, dst_ref=out_ref, send_sem=send_sem, recv_sem=recv_sem,
                                          device_id=(x, y, 1 - c), device_id_type=MESH)
        cp.start()
        cp.wait()

    return pl.pallas_call(
        body, name=name, out_shape=jax.ShapeDtypeStruct(v.shape, v.dtype),
        in_specs=[pl.BlockSpec(memory_space=pl.ANY)], out_specs=pl.BlockSpec(memory_space=pl.ANY),
        scratch_shapes=[pltpu.SemaphoreType.DMA, pltpu.SemaphoreType.DMA],
        compiler_params=pltpu.CompilerParams(vmem_limit_bytes=VMEM_LIMIT))(v)


def _row_tile(rows, cap, mult=8):
    best = None
    for t in range(mult, min(rows, cap) + 1, mult):
        if rows % t == 0:
            best = t
    assert best is not None, rows
    return best


def add_arrays(name, arrs, out_dtype=F32):
    shape = arrs[0].shape
    C = shape[-1]
    flat = [a.reshape(-1, C) for a in arrs]
    R = flat[0].shape[0]
    narrow = out_dtype == BF16 or any(a.dtype == BF16 for a in arrs)
    tr = _row_tile(R, 2048 if len(arrs) <= 2 else 1024, 16 if narrow else 8)
    n = len(flat)

    def body(*refs):
        acc = refs[0][...].astype(F32)
        for r in refs[1:n]:
            acc = acc + r[...].astype(F32)
        refs[n][...] = acc.astype(out_dtype)

    spec = pl.BlockSpec((tr, C), lambda i: (i, 0))
    out = pl.pallas_call(
        body, name=name, out_shape=jax.ShapeDtypeStruct((R, C), out_dtype), grid=(R // tr,),
        in_specs=[spec] * n, out_specs=spec, compiler_params=_cp(("parallel",)))(*flat)
    return out.reshape(shape)


def reduce_scatter_chips(slabs):
    _, R, C = slabs.shape
    half = R // 2
    c = lax.axis_index("c")
    halves = slabs.reshape(N_CHIPS, 2, half, C)
    own = lax.dynamic_index_in_dim(halves, c, axis=1, keepdims=False)
    other = lax.dynamic_index_in_dim(halves, 1 - c, axis=1, keepdims=False)
    from_sibling = sibling_swap("rs_sibling_halves", other.astype(BF16))
    return add_arrays("rs_add_sibling", [own, from_sibling], out_dtype=BF16)


def reduce_scatter_finish(landed):
    c = lax.axis_index("c")
    mine = add_arrays("rs_add_chips", [landed[j] for j in range(N_CHIPS)])
    sib = sibling_swap("rs_sibling_result", mine)
    return jnp.concatenate([jnp.where(c == 0, mine, sib), jnp.where(c == 0, sib, mine)], axis=0)


def ada_mod_shard(cond_all, w_ada_shard, b_ada_shard):
    def body(c_ref, w_ref, b_ref, o_ref):
        o_ref[...] = _dot(_silu(c_ref[...]).astype(BF16), w_ref[...].astype(BF16)) + b_ref[...]

    return pl.pallas_call(body, name="ada_mod_shard", out_shape=jax.ShapeDtypeStruct((cond_all.shape[0], w_ada_shard.shape[1]), F32),
                          compiler_params=_cp())(cond_all, w_ada_shard, b_ada_shard)


def ada_bwd_shard(cond_all, dmod_all_shard, dmod_all, w_ada_shard, row_is_cctx):
    def body(c_ref, ds_ref, da_ref, w_ref, sel_ref, gw_ref, gb_ref, part_ref):
        sc = _silu(c_ref[...]).astype(BF16)
        gw_ref[...] = _dot_tn(sc, ds_ref[...].astype(BF16))
        gb_ref[...] = _colsum(da_ref[...])
        dc_tot = jnp.broadcast_to(_colsum(ds_ref[...] * sel_ref[...]), (8, ds_ref.shape[1]))
        part_ref[...] = _dot_nt(dc_tot.astype(BF16), w_ref[...].astype(BF16))

    return pl.pallas_call(
        body, name="ada_bwd_shard",
        out_shape=[jax.ShapeDtypeStruct(w_ada_shard.shape, F32), jax.ShapeDtypeStruct((1, dmod_all.shape[1]), F32),
                   jax.ShapeDtypeStruct((8, D), F32)],
        compiler_params=_cp())(cond_all, dmod_all_shard, dmod_all, w_ada_shard, row_is_cctx)


def sum_devices(name, gathered):
    def body(g_ref, o_ref):
        acc = g_ref[0]
        for d in range(1, N_DEV):
            acc = acc + g_ref[d]
        o_ref[...] = acc

    return pl.pallas_call(body, name=name, out_shape=jax.ShapeDtypeStruct(gathered.shape[1:], F32), compiler_params=_cp())(gathered)


def cctx_finish(gathered, c_ctx_row):
    def body(g_ref, c_ref, o_ref):
        acc = g_ref[0, 0:1, :]
        for k in range(1, N_CHIPS):
            acc = acc + g_ref[2 * k, 0:1, :]
        o_ref[...] = acc * _dsilu(c_ref[...])

    return pl.pallas_call(body, name="cctx_finish", out_shape=jax.ShapeDtypeStruct((1, D), F32), compiler_params=_cp())(gathered, c_ctx_row)


def _pack(parts, rows):
    flat = []
    for p in parts:
        p = p.reshape(-1)
        pad = (-p.shape[0]) % LANES
        flat.append(jnp.pad(p, (0, pad)) if pad else p)
    v = jnp.concatenate(flat)
    return jnp.pad(v, (0, rows * LANES - v.shape[0])).reshape(rows, LANES)


def _unpack(v, sizes):
    flat = v.reshape(-1)
    out, off = [], 0
    for n in sizes:
        out.append(flat[off:off + n])
        off += n + (-n) % LANES
    return out


W_SHARD_ROWS = 3456
SEG_ROWS = (0, 2320, 2576, 3088, 3344, 3408)


def kernel(x, c, ctx, c_ctx, w_ada, b_ada, norm_pre, norm_post, w_in, b_merge, pool_w, pool_scale, conv_w, conv_b, dt_bias, a_log, d_skip, ssd_norm, w_proj_pool, w_proj_ssd, w_out, loss_target, m_c_ctx, m_w_ada, m_b_ada, m_norm_pre, m_norm_post, m_w_in, m_b_merge, m_pool_w, m_pool_scale, m_conv_w, m_conv_b, m_dt_bias, m_a_log, m_d_skip, m_ssd_norm, m_w_proj_pool, m_w_proj_ssd, m_w_out, v_c_ctx, v_w_ada, v_b_ada, v_norm_pre, v_norm_post, v_w_in, v_b_merge, v_pool_w, v_pool_scale, v_conv_w, v_conv_b, v_dt_bias, v_a_log, v_d_skip, v_ssd_norm, v_w_proj_pool, v_w_proj_ssd, v_w_out):
    Bn, L, _ = x.shape
    Lc = ctx.shape[1]
    T, Tc = Bn * L, Bn * Lc
    assert Bn == 2
    ix, iy, ic = lax.axis_index("x"), lax.axis_index("y"), lax.axis_index("c")
    me = 4 * ix + 2 * iy + ic
    chip = 2 * ix + iy
    ada_cols = w_ada.shape[2]
    cw_cols = conv_w.shape[2]

    cond_own = jnp.pad(c, ((0, 8 - Bn), (0, 0))) + jnp.pad(c_ctx[None, :], ((Bn, 7 - Bn), (0, 0)))
    convw_own = jnp.pad(conv_w[0], ((0, 4), (0, D - cw_cols)))
    g1 = all_gather_small("gather_cond", jnp.concatenate([cond_own, convw_own], axis=0))
    cond_all = g1[:, 0:8].reshape(8 * N_DEV, D)
    conv_w_full = jnp.concatenate([g1[2 * k, 8:12, 0:cw_cols] for k in range(N_CHIPS)], axis=1)
    b_ada_shard = lax.dynamic_slice(b_ada, (0, chip * ada_cols), (1, ada_cols))
    g2 = all_gather_small("gather_mod", ada_mod_shard(cond_all, w_ada[0], b_ada_shard))
    mod_full = jnp.concatenate([g2[2 * k] for k in range(N_CHIPS)], axis=1)
    own = lax.dynamic_slice(mod_full, (8 * me, 0), (8, 3 * D))
    shift, scale, gate = (own[0:Bn, i * D:(i + 1) * D][:, None, :] for i in range(3))
    shift_c, scale_c = (jnp.broadcast_to(own[Bn:Bn + 1, i * D:(i + 1) * D][None], (Bn, 1, D)) for i in range(2))

    w_in_rows = IN_COLS // N_CHIPS
    shard_in = jnp.concatenate([w_in[0].T, jnp.zeros((16, D), F32)], axis=0).astype(BF16)
    shard_rest = jnp.concatenate([w_proj_pool[0], w_proj_ssd[0], w_out[0], pool_w[0].reshape(64, D)], axis=0).astype(BF16)
    w_inT = all_gather_chips("gather_w_in", shard_in)[:, 0:w_in_rows].reshape(IN_COLS, D)
    w_dt = jnp.pad(w_inT[9216:IN_COLS], ((0, LANES - 64), (0, 0)))
    seg_lo = (0, 1024, 2048, 4096, 6144, 8192, 8704)
    seg_hi = (1024, 2048, 4096, 6144, 8192, 8704, 9216)
    w_seg = [w_inT[lo:hi] for lo, hi in zip(seg_lo, seg_hi)] + [w_dt]

    hx = prenorm_fwd("prenorm_x", x, scale, shift, norm_pre)
    hc = prenorm_fwd("prenorm_ctx", ctx, scale_c, shift_c, norm_pre)
    hx2, hc2 = hx.reshape(T, D), hc.reshape(Tc, D)
    v = mm_nt("proj_v", hx2, w_inT[0:1024], F32).reshape(Bn, L, D)
    zp = mm_nt("proj_zpool", hx2, w_inT[1024:2048], F32).reshape(Bn, L, D)
    zs = mm_nt("proj_zssd", hx2, w_inT[2048:4096], F32).reshape(Bn, L, D_INNER)
    gp = mm_nt("proj_gate", hx2, w_inT[4096:6144], F32).reshape(Bn, L, 2 * D)
    xbc_raw, g_rest = mm_nt("proj_xbc", hx2, w_inT[6144:9216], F32, gather=shard_rest)
    xbc_raw = xbc_raw.reshape(Bn, L, CONV_DIM)
    w_pp = g_rest[:, 0:256].reshape(D, D)
    w_ps = g_rest[:, 256:768].reshape(D_INNER, D)
    w_o = g_rest[:, 768:1024].reshape(D, D)
    pool_full = g_rest[:, 1024:1088].reshape(N_CHIPS, 4, 64, POOL_GROUP).transpose(1, 0, 2, 3).reshape(D, POOL_GROUP)
    dt_raw = mm_nt("proj_dt", hx2, w_dt, F32)
    xbc_raw_c = mm_nt("proj_xbc_ctx", hc2, w_inT[6144:9216], F32).reshape(Bn, Lc, CONV_DIM)
    dt_raw_c = mm_nt("proj_dt_ctx", hc2, w_dt, F32)
    dtT = dt_raw[:, :64].reshape(Bn, L, 64).transpose(0, 2, 1)
    dtT_c = dt_raw_c[:, :64].reshape(Bn, Lc, 64).transpose(0, 2, 1)
    bias3 = dt_bias.reshape(2 * N_BC, HPG, 1)
    alog3 = a_log.reshape(2 * N_BC, HPG, 1)

    xbc = conv_fwd("conv_x", xbc_raw, conv_w_full, conv_b)
    xbc_c = conv_fwd("conv_ctx", xbc_raw_c, conv_w_full, conv_b)
    zero_state = jnp.zeros((Bn, N_BC, D_STATE, GW), F32)
    dskip_lanes = jnp.repeat(d_skip[0], HEAD_DIM)[None, :]
    ys, hs_x, hs_c = [], [], []
    for d in range(2):
        hsc, hfc = ssd_fwd(f"ssd_fwd_ctx{d}", dtT_c, bias3, alog3, xbc_c, zero_state, d, False)
        y, hsx, _ = ssd_fwd(f"ssd_fwd_x{d}", dtT, bias3, alog3, xbc, hfc, d, True,
                            y_add=(ys[0], dskip_lanes) if d == 1 else None)
        ys.append(y)
        hs_x.append(hsx)
        hs_c.append(hsc)

    dgs = [pool_diff(f"pool_diff{g}", v, g * POOL_GROUP, g, False) for g in range(4)]
    y_pool = pool_mix_fwd("pool_mix", dgs, zp, pool_full, pool_scale)
    y_ssd = gated_norm_fwd("gated_norm", ys[1], zs, ssd_norm)
    merged, p1, p2, dout, g_res, dgate, g_norm_post, loss_part = merge_fwd(
        "merge_fwd", y_pool, y_ssd, gp, x, loss_target, gate, b_merge, norm_post, w_pp, w_ps, w_o)

    dp1, dp2, dgp, dyp, dys, g_b_merge = merge_bwd("merge_bwd", dout, gp, p1, p2, b_merge, w_pp, w_ps, w_o)
    gw_o = mm_tn("gw_out", merged.reshape(T, D), dout.reshape(T, D))
    gw_pp = mm_tn("gw_proj_pool", y_pool.reshape(T, D), dp1.reshape(T, D))
    gw_ps = mm_tn("gw_proj_ssd", y_ssd.reshape(T, D_INNER), dp2.reshape(T, D))

    *dds, dzp, g_pool, g_pool_scale = pool_mix_bwd("pool_mix_bwd", dgs, zp, dyp, pool_full, pool_scale)
    dvs = [pool_diff(f"pool_diff_t{g}", dds[g], 0, g, True) for g in range(4)]

    head_sel = (jnp.arange(D_INNER)[:, None] // HEAD_DIM == jnp.arange(LANES)[None, :]).astype(BF16)
    dy, dzs, g_ssd_norm, g_dskip = gated_norm_bwd(
        "gated_norm_bwd", ys[1], (xbc, D_INNER), zs, dys, ssd_norm, head_sel)

    dxs, dbm, dcm, ddt, dxs_c, dbm_c, ddt_c = [], [], [], [], [], [], []
    g_bias = jnp.zeros((2, N_BC, HPG, 1), F32)
    g_alog = jnp.zeros((2, N_BC, HPG, 1), F32)
    for d in range(2):
        a, b_, c_, t_, gb, ga, dh0 = ssd_bwd(f"ssd_bwd_x{d}", dtT, bias3, alog3, xbc, hs_x[d], dy, zero_state, d,
                                             dx_add=(dxs[0], dskip_lanes) if d == 1 else None)
        dxs.append(a), dbm.append(b_), dcm.append(c_), ddt.append(t_)
        ac, bc, _, tc, gbc, gac, _ = ssd_bwd(f"ssd_bwd_ctx{d}", dtT_c, bias3, alog3, xbc_c, hs_c[d], None, dh0, d)
        dxs_c.append(ac), dbm_c.append(bc), ddt_c.append(tc)
        g_bias = g_bias.at[d].set(jnp.sum(gb, axis=0) + jnp.sum(gbc, axis=0))
        g_alog = g_alog.at[d].set(jnp.sum(ga, axis=0) + jnp.sum(gac, axis=0))

    dxr_xs, gcw_xs, gcb_xs = conv_bwd("conv_bwd_xs", xbc_raw, [dxs[1]], conv_w_full, conv_b, 0, D_INNER)
    dxr_b, gcw_b, gcb_b = conv_bwd("conv_bwd_b", xbc_raw, dbm, conv_w_full, conv_b, D_INNER, N_BC * D_STATE)
    dxr_c, gcw_c, gcb_c = conv_bwd("conv_bwd_c", xbc_raw, dcm, conv_w_full, conv_b, D_INNER + N_BC * D_STATE, N_BC * D_STATE)
    dxr_xs_c, gcw_xs_c, gcb_xs_c = conv_bwd("conv_bwd_xs_ctx", xbc_raw_c, dxs_c, conv_w_full, conv_b, 0, D_INNER)
    dxr_b_c, gcw_b_c, gcb_b_c = conv_bwd("conv_bwd_b_ctx", xbc_raw_c, dbm_c, conv_w_full, conv_b, D_INNER, N_BC * D_STATE)
    g_conv_w = jnp.concatenate([gcw_xs + gcw_xs_c, gcw_b + gcw_b_c, gcw_c], axis=1)
    g_conv_b = jnp.concatenate([gcb_xs + gcb_xs_c, gcb_b + gcb_b_c, gcb_c], axis=1)

    def dt_cols(parts, n_tok):
        t = jnp.concatenate(parts, axis=1).transpose(0, 2, 1).reshape(n_tok, 2 * N_HEADS)
        return jnp.pad(t, ((0, 0), (0, LANES - 2 * N_HEADS))).astype(BF16)

    ddt2, ddt2_c = dt_cols(ddt, T), dt_cols(ddt_c, Tc)
    segs = [jnp.concatenate(dvs, axis=-1).reshape(T, D), dzp.reshape(T, D), dzs.reshape(T, D_INNER), dgp.reshape(T, 2 * D),
            dxr_xs.reshape(T, D_INNER), dxr_b.reshape(T, N_BC * D_STATE), dxr_c.reshape(T, N_BC * D_STATE), ddt2]
    segs_c = {4: dxr_xs_c.reshape(Tc, D_INNER), 5: dxr_b_c.reshape(Tc, N_BC * D_STATE), 7: ddt2_c}
    gw_rows = []
    for i, seg in enumerate(segs):
        init = mm_tn(f"gw_in_ctx{i}", segs_c[i], hc2) if i in segs_c else None
        gw_rows.append(mm_tn(f"gw_in{i}", seg, hx2, init=init))
    gw_rows[-1] = gw_rows[-1][0:2 * N_HEADS]
    gw_inT = jnp.concatenate(gw_rows, axis=0)

    pool_slab = g_pool.reshape(4, N_CHIPS, 64, POOL_GROUP).transpose(1, 0, 2, 3).reshape(N_CHIPS, 64, D)
    slabs = jnp.concatenate([gw_inT.reshape(N_CHIPS, 2320, D), gw_pp.reshape(N_CHIPS, 256, D), gw_ps.reshape(N_CHIPS, 512, D),
                             gw_o.reshape(N_CHIPS, 256, D), pool_slab, jnp.zeros((N_CHIPS, W_SHARD_ROWS - SEG_ROWS[-1], D), F32)], axis=1)
    chip_part = reduce_scatter_chips(slabs)
    d_hx, landed = mm_nn_multi("d_hx", list(zip(segs, w_seg)), F32, tm=1024, tk=512, exchange=chip_part)
    d_hx = d_hx.reshape(Bn, L, D)
    gsh = reduce_scatter_finish(landed)
    d_hc = mm_nn_multi("d_hc", [(segs_c[i], w_seg[i]) for i in sorted(segs_c)], F32).reshape(Bn, Lc, D)

    grad_x, dscale, dshift, g_npre_x = prenorm_bwd("prenorm_bwd_x", x, d_hx, scale, norm_pre, g_res=g_res)
    _, dscale_c, dshift_c, g_npre_c = prenorm_bwd("prenorm_bwd_ctx", ctx, d_hc, scale_c, norm_pre)

    dmod_x = jnp.concatenate([dshift[:, 0], dscale[:, 0], dgate[:, 0]], axis=1)
    dmod_c = jnp.concatenate([jnp.sum(dshift_c[:, 0], axis=0, keepdims=True), jnp.sum(dscale_c[:, 0], axis=0, keepdims=True),
                              jnp.zeros((1, D), F32)], axis=1)
    dmod_own = jnp.pad(dmod_x, ((0, 8 - Bn), (0, 0))) + jnp.pad(dmod_c, ((Bn, 7 - Bn), (0, 0)))

    small_sizes = (D, D, 2 * D, D, CONV_DIM, 2 * N_HEADS, 2 * N_HEADS, N_HEADS, D_INNER, 4 * CONV_DIM, 1)
    pk = _pack([g_npre_x + g_npre_c, g_norm_post, g_b_merge, g_pool_scale, g_conv_b, g_bias, g_alog, g_dskip[0, 0:N_HEADS],
                g_ssd_norm, g_conv_w, loss_part[0, 0:1]], 184)
    dmod_rows = 8 * 3 * D // LANES
    gathered = all_gather_small("gather_small", jnp.concatenate([dmod_own.reshape(dmod_rows, LANES), pk], axis=0))
    dmod_all = gathered[:, 0:dmod_rows].reshape(8 * N_DEV, 3 * D)
    small = sum_devices("sum_small", gathered[:, dmod_rows:])
    row_is_cctx = (jnp.arange(8 * N_DEV) % 8 == Bn).astype(F32)[:, None]
    g_w_ada, g_b_ada, cpart = ada_bwd_shard(
        cond_all, lax.dynamic_slice(dmod_all, (0, chip * ada_cols), (8 * N_DEV, ada_cols)), dmod_all, w_ada[0], row_is_cctx)
    g_c_ctx = cctx_finish(all_gather_small("gather_cctx", cpart), c_ctx[None, :])
    (g_norm_pre, g_norm_post_t, g_b_merge_t, g_pool_scale_t, g_conv_b_t, g_dt_bias, g_a_log, g_d_skip, g_ssd_norm_t,
     g_conv_w_t, loss) = _unpack(small, small_sizes)
    g_conv_w_shard = lax.dynamic_slice(g_conv_w_t.reshape(4, CONV_DIM), (0, chip * cw_cols), (4, cw_cols))

    g_w_in = gsh[SEG_ROWS[0]:SEG_ROWS[1]].T
    g_w_pp, g_w_ps, g_w_o = (gsh[SEG_ROWS[i]:SEG_ROWS[i + 1]] for i in (1, 2, 3))
    g_pool_w = gsh[SEG_ROWS[4]:SEG_ROWS[5]].reshape(256, POOL_GROUP)

    grads = {
        "c_ctx": g_c_ctx.reshape(c_ctx.shape), "w_ada": g_w_ada[None], "b_ada": g_b_ada, "norm_pre": g_norm_pre[None],
        "norm_post": g_norm_post_t[None], "w_in": g_w_in[None], "b_merge": g_b_merge_t[None],
        "pool_w": g_pool_w.reshape(pool_w.shape), "pool_scale": g_pool_scale_t[None], "conv_w": g_conv_w_shard[None],
        "conv_b": g_conv_b_t[None], "dt_bias": g_dt_bias.reshape(dt_bias.shape), "a_log": g_a_log.reshape(a_log.shape),
        "d_skip": g_d_skip[None], "ssd_norm": g_ssd_norm_t[None], "w_proj_pool": g_w_pp[None], "w_proj_ssd": g_w_ps[None],
        "w_out": g_w_o[None]}
    weights = dict(c_ctx=c_ctx, w_ada=w_ada, b_ada=b_ada, norm_pre=norm_pre, norm_post=norm_post, w_in=w_in, b_merge=b_merge,
                   pool_w=pool_w, pool_scale=pool_scale, conv_w=conv_w, conv_b=conv_b, dt_bias=dt_bias, a_log=a_log,
                   d_skip=d_skip, ssd_norm=ssd_norm, w_proj_pool=w_proj_pool, w_proj_ssd=w_proj_ssd, w_out=w_out)
    m_in = dict(c_ctx=m_c_ctx, w_ada=m_w_ada, b_ada=m_b_ada, norm_pre=m_norm_pre, norm_post=m_norm_post, w_in=m_w_in,
                b_merge=m_b_merge, pool_w=m_pool_w, pool_scale=m_pool_scale, conv_w=m_conv_w, conv_b=m_conv_b,
                dt_bias=m_dt_bias, a_log=m_a_log, d_skip=m_d_skip, ssd_norm=m_ssd_norm, w_proj_pool=m_w_proj_pool,
                w_proj_ssd=m_w_proj_ssd, w_out=m_w_out)
    v_in = dict(c_ctx=v_c_ctx, w_ada=v_w_ada, b_ada=v_b_ada, norm_pre=v_norm_pre, norm_post=v_norm_post, w_in=v_w_in,
                b_merge=v_b_merge, pool_w=v_pool_w, pool_scale=v_pool_scale, conv_w=v_conv_w, conv_b=v_conv_b,
                dt_bias=v_dt_bias, a_log=v_a_log, d_skip=v_d_skip, ssd_norm=v_ssd_norm, w_proj_pool=v_w_proj_pool,
                w_proj_ssd=v_w_proj_ssd, w_out=v_w_out)
    names = list(weights)
    big = ("w_ada", "w_in", "pool_w", "w_proj_pool", "w_proj_ssd", "w_out")
    small_names = [n for n in names if n not in big]
    delta, new_m, new_v = {}, {}, {}
    for n in big:
        shape2 = (-1, weights[n].shape[-1])
        d_, m_, v_ = adamw(f"adamw_{n}", weights[n].reshape(shape2), grads[n].reshape(shape2), m_in[n].reshape(shape2),
                           v_in[n].reshape(shape2), tr=128)
        delta[n], new_m[n], new_v[n] = (t.reshape(weights[n].shape) for t in (d_, m_, v_))
    sizes = [weights[n].size for n in small_names]
    packed = [_pack([src[n] for n in small_names], 144) for src in (weights, grads, m_in, v_in)]
    outs = adamw("adamw_small", *packed, tr=144)
    for res, store in zip(outs, (delta, new_m, new_v)):
        for n, piece in zip(small_names, _unpack(res, sizes)):
            store[n] = piece.reshape(weights[n].shape)

    return (loss.reshape(()), grad_x, *[grads[n] for n in names], *[delta[n] for n in names],
            *[new_m[n] for n in names], *[new_v[n] for n in names])
```

```python
import jax
import jax.numpy as jnp
from jax import lax
from jax.experimental import pallas as pl
from jax.experimental.pallas import tpu as pltpu

F32 = jnp.float32
BF16 = jnp.bfloat16
MESH = pl.DeviceIdType.MESH

D = 1024
GRID_W = 64
NORM_EPS = 1e-6
POOL_WINDOWS = (2, 4, 8, 16)
POOL_GROUP = 256
D_INNER = 2048
HEAD_DIM = 64
N_HEADS = 32
D_STATE = 128
N_BC = 4
HPG = N_HEADS // N_BC
GW = HPG * HEAD_DIM
CONV_DIM = 3072
CHUNK = 128
IN_COLS = 9280
N_CHIPS = 4
N_DEV = 8

ADAM_LR = 0.001
ADAM_B1 = 0.9
ADAM_B2 = 0.999
ADAM_EPS = 1e-08
ADAM_WD = 0.01
ADAM_STEP = 10

V7X_VMEM_BYTES = 64 * 1024 * 1024
VMEM_LIMIT = V7X_VMEM_BYTES * 3 // 4
LANES = 128


def _cp(sem=None):
    return pltpu.CompilerParams(dimension_semantics=sem, vmem_limit_bytes=VMEM_LIMIT)


def _dot(a, b):
    return jnp.dot(a, b, preferred_element_type=F32)


def _dot_nt(a, b):
    return lax.dot_general(a, b, (((1,), (1,)), ((), ())), preferred_element_type=F32)


def _dot_tn(a, b):
    return lax.dot_general(a, b, (((0,), (0,)), ((), ())), preferred_element_type=F32)


def _split3(x):
    hi = x.astype(BF16)
    r1 = x - hi.astype(F32)
    mid = r1.astype(BF16)
    lo = (r1 - mid.astype(F32)).astype(BF16)
    return hi, mid, lo


def _dot_exact01(v, sel):
    hi, mid, lo = _split3(v)
    return _dot(hi, sel) + _dot(mid, sel) + _dot(lo, sel)


def _sigmoid(x):
    return jax.nn.sigmoid(x)


def _silu(x):
    return x * _sigmoid(x)


def _dsilu(x):
    s = _sigmoid(x)
    return s * (1.0 + x * (1.0 - s))


def _softplus(x):
    return jnp.maximum(x, 0.0) + jnp.log(1.0 + jnp.exp(-jnp.abs(x)))


def _me():
    return lax.axis_index("x"), lax.axis_index("y"), lax.axis_index("c")


def mm_nt(name, a, b, out_dtype, tm=1024, tn=512, gather=None):
    M, K = a.shape
    N = b.shape[0]
    tm, tn = min(tm, M), min(tn, N)
    assert M % tm == 0 and N % tn == 0, (M, N, tm, tn)
    n_i, n_j = M // tm, N // tn
    has_g = gather is not None
    if has_g:
        half = gather.shape[0] // 2
        assert gather.shape[0] % 32 == 0 and n_i * n_j >= 4

    def body(*refs):
        a_ref, b_ref = refs[0], refs[1]
        if has_g:
            s_ref, o_ref, g_ref, send_sems, recv_sems = refs[2:]
            x, y, c = _me()
            chips = [(1 - x, y), (x, 1 - y), (1 - x, 1 - y)]
            step = pl.program_id(0) * n_j + pl.program_id(1)

            def rows(chip, hc):
                return g_ref.at[2 * chip[0] + chip[1], pl.ds(hc * half, half), :]

            def first(j, chip):
                return pltpu.make_async_remote_copy(
                    src_ref=s_ref.at[pl.ds(c * half, half), :], dst_ref=rows((x, y), c), send_sem=send_sems.at[j],
                    recv_sem=recv_sems.at[j], device_id=(*chip, c), device_id_type=MESH)

            def landed(j, chip, hc):
                return pltpu.make_async_remote_copy(
                    src_ref=rows(chip, hc), dst_ref=rows(chip, hc), send_sem=send_sems.at[j], recv_sem=recv_sems.at[j],
                    device_id=(x, y, 1 - c), device_id_type=MESH)

            @pl.when(step == 0)
            def _():
                for j, chip in enumerate(chips):
                    first(j, chip).start()

            @pl.when(step == (3 * n_i * n_j) // 4)
            def _():
                for j, chip in enumerate(chips):
                    landed(j, chip, c).wait_recv()
                    landed(3 + j, chip, c).start()
        else:
            o_ref = refs[2]

        o_ref[...] = _dot_nt(a_ref[...], b_ref[...]).astype(o_ref.dtype)

        if has_g:
            @pl.when(step == n_i * n_j - 1)
            def _():
                for j, chip in enumerate(chips):
                    landed(3 + j, chip, 1 - c).wait_recv()
                for j, chip in enumerate(chips):
                    first(j, chip).wait_send()
                    landed(3 + j, chip, c).wait_send()

    in_specs = [pl.BlockSpec((tm, K), lambda i, j: (i, 0)), pl.BlockSpec((tn, K), lambda i, j: (j, 0))]
    out_shape = jax.ShapeDtypeStruct((M, N), out_dtype)
    out_specs = pl.BlockSpec((tm, tn), lambda i, j: (i, j))
    if not has_g:
        return pl.pallas_call(body, name=name, out_shape=out_shape, grid=(n_i, n_j), in_specs=in_specs, out_specs=out_specs,
                              compiler_params=_cp(("parallel", "arbitrary")))(a, b)
    out, g = pl.pallas_call(
        body, name=name, out_shape=[out_shape, jax.ShapeDtypeStruct((N_CHIPS, *gather.shape), gather.dtype)], grid=(n_i, n_j),
        in_specs=in_specs + [pl.BlockSpec(memory_space=pl.ANY)], out_specs=[out_specs, pl.BlockSpec(memory_space=pl.ANY)],
        scratch_shapes=[pltpu.SemaphoreType.DMA((6,)), pltpu.SemaphoreType.DMA((6,))],
        compiler_params=_cp(("arbitrary", "arbitrary")))(a, b, gather)
    chip = 2 * lax.axis_index("x") + lax.axis_index("y")
    return out, lax.dynamic_update_index_in_dim(g, gather, chip, 0)


def mm_tn(name, a, b, init=None, tm=1024, tn=1024, tk=2048):
    T, M = a.shape
    N = b.shape[1]
    tm, tn, tk = min(tm, M), min(tn, N), min(tk, T)
    assert M % tm == 0 and N % tn == 0 and T % tk == 0, (M, N, T)
    has_init = init is not None

    def body(*refs):
        if has_init:
            a_ref, b_ref, i_ref, o_ref = refs
        else:
            a_ref, b_ref, o_ref = refs
        k = pl.program_id(2)

        @pl.when(k == 0)
        def _():
            o_ref[...] = i_ref[...] if has_init else jnp.zeros(o_ref.shape, F32)

        o_ref[...] += _dot_tn(a_ref[...], b_ref[...])

    in_specs = [pl.BlockSpec((tk, tm), lambda i, j, k: (k, i)), pl.BlockSpec((tk, tn), lambda i, j, k: (k, j))]
    args = [a, b]
    if has_init:
        in_specs.append(pl.BlockSpec((tm, tn), lambda i, j, k: (i, j)))
        args.append(init)
    return pl.pallas_call(
        body, name=name, out_shape=jax.ShapeDtypeStruct((M, N), F32), grid=(M // tm, N // tn, T // tk),
        in_specs=in_specs, out_specs=pl.BlockSpec((tm, tn), lambda i, j, k: (i, j)),
        compiler_params=_cp(("parallel", "parallel", "arbitrary")))(*args)


def mm_nn_multi(name, pairs, out_dtype, tm=512, tk=512, exchange=None):
    M = pairs[0][0].shape[0]
    N = pairs[0][1].shape[1]
    tm = min(tm, M)
    assert M % tm == 0
    plan = []
    step = 0
    for a, b in pairs:
        K = a.shape[1]
        t = min(tk, K)
        assert K % t == 0 and b.shape == (K, N)
        plan.append((t, step, K // t))
        step += K // t
    nsteps = step
    npairs = len(pairs)

    n_i = M // tm
    has_x = exchange is not None

    def body(*refs):
        if has_x:
            p_ref, o_ref, land_ref, acc, send_sems, recv_sems = refs[2 * npairs:]
        else:
            o_ref, acc = refs[2 * npairs:]
        i, k = pl.program_id(0), pl.program_id(1)

        if has_x:
            x, y, c = _me()
            me_chip = 2 * x + y
            chips = [(1 - x, y), (x, 1 - y), (1 - x, 1 - y)]

            def copy(j, src_chip, dst_chip, to):
                return pltpu.make_async_remote_copy(
                    src_ref=p_ref.at[src_chip], dst_ref=land_ref.at[dst_chip], send_sem=send_sems.at[j], recv_sem=recv_sems.at[j],
                    device_id=(*to, c), device_id_type=MESH)

            @pl.when((i == 0) & (k == 0))
            def _():
                for j, chip in enumerate(chips):
                    copy(j, 2 * chip[0] + chip[1], me_chip, chip).start()

        @pl.when(k == 0)
        def _():
            acc[...] = jnp.zeros(acc.shape, F32)

        for p, (_, first, n) in enumerate(plan):
            @pl.when((k >= first) & (k < first + n))
            def _(p=p):
                acc[...] += _dot(refs[2 * p][...], refs[2 * p + 1][...])

        @pl.when(k == nsteps - 1)
        def _():
            o_ref[...] = acc[...].astype(o_ref.dtype)

        if has_x:
            @pl.when((i == n_i - 1) & (k == nsteps - 1))
            def _():
                for j, chip in enumerate(chips):
                    copy(j, me_chip, 2 * chip[0] + chip[1], chip).wait_recv()
                for j, chip in enumerate(chips):
                    copy(j, 2 * chip[0] + chip[1], me_chip, chip).wait_send()

    in_specs, args = [], []
    for (a, b), (t, first, n) in zip(pairs, plan):
        in_specs.append(pl.BlockSpec((tm, t), lambda i, k, first=first, n=n: (i, jnp.clip(k - first, 0, n - 1))))
        in_specs.append(pl.BlockSpec((t, N), lambda i, k, first=first, n=n: (jnp.clip(k - first, 0, n - 1), 0)))
        args += [a, b]
    out_shape = jax.ShapeDtypeStruct((M, N), out_dtype)
    out_specs = pl.BlockSpec((tm, N), lambda i, k: (i, 0))
    scratch = [pltpu.VMEM((tm, N), F32)]
    if has_x:
        in_specs.append(pl.BlockSpec(memory_space=pl.ANY))
        args.append(exchange)
        out_shape = [out_shape, jax.ShapeDtypeStruct(exchange.shape, exchange.dtype)]
        out_specs = [out_specs, pl.BlockSpec(memory_space=pl.ANY)]
        scratch += [pltpu.SemaphoreType.DMA((3,)), pltpu.SemaphoreType.DMA((3,))]
    res = pl.pallas_call(
        body, name=name, out_shape=out_shape, grid=(n_i, nsteps), in_specs=in_specs, out_specs=out_specs,
        scratch_shapes=scratch, compiler_params=_cp(("arbitrary", "arbitrary")))(*args)
    if not has_x:
        return res
    out, landed = res
    chip = 2 * lax.axis_index("x") + lax.axis_index("y")
    own = lax.dynamic_index_in_dim(exchange, chip, 0, keepdims=True)
    return out, lax.dynamic_update_slice_in_dim(landed, own, chip, 0)


def tok_call(name, body, tiled, perb, glob, out_tiled, out_perb, out_glob, tm=256):
    widths = [t[1] if isinstance(t, tuple) else t.shape[2] for t in tiled]
    tiled = [t[0] if isinstance(t, tuple) else t for t in tiled]
    Bn, L = tiled[0].shape[:2]
    tm = min(tm, L)
    assert L % tm == 0
    n_t, n_p, n_g = len(tiled), len(perb), len(glob)
    o_t, o_p, o_g = len(out_tiled), len(out_perb), len(out_glob)
    n_in = n_t + n_p + n_g

    def kern(*refs):
        ins, outs = refs[:n_in], refs[n_in:]
        b, j = pl.program_id(0), pl.program_id(1)
        vals = [r[0] for r in ins[:n_t + n_p]] + [r[...] for r in ins[n_t + n_p:]]
        res = body(*vals)
        if not isinstance(res, (tuple, list)):
            res = (res,)
        assert len(res) == o_t + o_p + o_g, (name, len(res))
        for r, v in zip(outs[:o_t], res[:o_t]):
            r[0] = v.astype(r.dtype)

        def accum(r, v, first, lead):
            @pl.when(first)
            def _():
                r[...] = jnp.zeros(r.shape, F32)
            if lead:
                r[0] += v
            else:
                r[...] += v

        for r, v in zip(outs[o_t:o_t + o_p], res[o_t:o_t + o_p]):
            accum(r, v, j == 0, True)
        for r, v in zip(outs[o_t + o_p:], res[o_t + o_p:]):
            accum(r, v, (j == 0) & (b == 0), False)

    in_specs = ([pl.BlockSpec((1, tm, w), lambda b, j: (b, j, 0)) for w in widths]
                + [pl.BlockSpec((1, 1, a.shape[2]), lambda b, j: (b, 0, 0)) for a in perb]
                + [pl.BlockSpec(a.shape, lambda b, j: (0, 0), pipeline_mode=pl.Buffered(1)) for a in glob])
    out_shape = ([jax.ShapeDtypeStruct((Bn, L, w), dt) for w, dt in out_tiled]
                 + [jax.ShapeDtypeStruct((Bn, 1, w), F32) for w in out_perb]
                 + [jax.ShapeDtypeStruct(s, F32) for s in out_glob])
    out_specs = ([pl.BlockSpec((1, tm, w), lambda b, j: (b, j, 0)) for w, _ in out_tiled]
                 + [pl.BlockSpec((1, 1, w), lambda b, j: (b, 0, 0)) for w in out_perb]
                 + [pl.BlockSpec(s, lambda b, j: (0, 0)) for s in out_glob])
    return pl.pallas_call(
        kern, name=name, out_shape=out_shape, grid=(Bn, L // tm), in_specs=in_specs, out_specs=out_specs,
        compiler_params=_cp(("arbitrary", "arbitrary")))(*tiled, *perb, *glob)


def slab_call(name, body, slabs, colparams, out_slabs, out_colred, wc=LANES):
    Bn, L = slabs[0][0].shape[:2]
    w_out = out_slabs[0][0]
    assert w_out % wc == 0 and all(off % wc == 0 for _, off in slabs + colparams)
    n_col = w_out // wc
    n_s, n_c = len(slabs), len(colparams)
    o_s = len(out_slabs)

    def kern(*refs):
        ins, outs = refs[:n_s + n_c], refs[n_s + n_c:]
        b = pl.program_id(1)
        vals = [r[0] for r in ins[:n_s]] + [r[...] for r in ins[n_s:]]
        res = body(*vals)
        if not isinstance(res, (tuple, list)):
            res = (res,)
        assert len(res) == o_s + len(out_colred), name
        for r, v in zip(outs[:o_s], res[:o_s]):
            r[0] = v.astype(r.dtype)

        def accum(r, v):
            @pl.when(b == 0)
            def _():
                r[...] = jnp.zeros(r.shape, F32)
            r[...] += v

        for r, v in zip(outs[o_s:], res[o_s:]):
            accum(r, v)

    in_specs = ([pl.BlockSpec((1, L, wc), lambda j, b, o=off // wc: (b, 0, o + j)) for _, off in slabs]
                + [pl.BlockSpec((a.shape[0], wc), lambda j, b, o=off // wc: (0, o + j)) for a, off in colparams])
    out_shape = ([jax.ShapeDtypeStruct((Bn, L, w), dt) for w, dt in out_slabs]
                 + [jax.ShapeDtypeStruct((r, w_out), F32) for r in out_colred])
    out_specs = ([pl.BlockSpec((1, L, wc), lambda j, b: (b, 0, j)) for _ in out_slabs]
                 + [pl.BlockSpec((r, wc), lambda j, b: (0, j)) for r in out_colred])
    return pl.pallas_call(
        kern, name=name, out_shape=out_shape, grid=(n_col, Bn), in_specs=in_specs, out_specs=out_specs,
        compiler_params=_cp(("arbitrary", "arbitrary")))(*[a for a, _ in slabs], *[a for a, _ in colparams])


def _rms_r(x):
    return lax.rsqrt(jnp.mean(x * x, axis=-1, keepdims=True) + NORM_EPS)


def _rms_bwd(dxh, x, r):
    return r * (dxh - x * (r * r) * jnp.mean(dxh * x, axis=-1, keepdims=True))


def _colsum(v):
    return jnp.sum(v, axis=0, keepdims=True)


def _stack_rows(rows):
    n, w = len(rows), rows[0].shape[1]
    sub = lax.broadcasted_iota(jnp.int32, (n, w), 0)
    acc = jnp.zeros((n, w), F32)
    for r, row in enumerate(rows):
        acc = acc + jnp.where(sub == r, jnp.broadcast_to(row, (n, w)), 0.0)
    return acc


def prenorm_fwd(name, x, scale, shift, w_pre):
    def body(x, scale, shift, w):
        n = x * _rms_r(x) * w
        return n * (1.0 + scale) + shift

    return tok_call(name, body, [x], [scale, shift], [w_pre], [(D, BF16)], [], [])[0]


def prenorm_bwd(name, x, dhx, scale, w_pre, g_res=None):
    has_res = g_res is not None

    def body(*v):
        if has_res:
            x, dhx, g, scale, w = v
        else:
            x, dhx, scale, w = v
        r = _rms_r(x)
        xr = x * r
        n = xr * w
        dn = dhx * (1.0 + scale)
        dx = _rms_bwd(dn * w, x, r)
        if has_res:
            dx = dx + g
        return dx, _colsum(dhx * n), _colsum(dhx), _colsum(dn * xr)

    tiled = [x, dhx] + ([g_res] if has_res else [])
    return tok_call(name, body, tiled, [scale], [w_pre], [(D, F32)], [D, D], [(1, D)])


def _shift_rows(x, o, tok, L):
    if o == 0:
        return x
    rolled = pltpu.roll(x, (-o) % L, 0)
    return jnp.where((tok + o >= 0) & (tok + o < L), rolled, 0.0)


def conv_fwd(name, xbc_raw, conv_w, conv_b):
    L = xbc_raw.shape[1]

    def body(x, w, b):
        tok = lax.broadcasted_iota(jnp.int32, x.shape, 0)
        pre = b
        for k in range(4):
            pre = pre + _shift_rows(x, k - 2, tok, L) * w[k:k + 1]
        return _silu(pre)

    return slab_call(name, body, [(xbc_raw, 0)], [(conv_w, 0), (conv_b, 0)], [(CONV_DIM, F32)], [])[0]


CONV_ROWS = 128
CONV_HALO = 8


def _halo_chunks(L, load, work):
    ch, hl = CONV_ROWS, CONV_HALO
    n = L // ch
    assert L % ch == 0
    if n == 1:
        z = jnp.zeros_like(load(0, hl))
        work(0, jnp.concatenate([z, load(0, ch), z], axis=0))
        return
    z = jnp.zeros_like(load(0, hl))
    work(0, jnp.concatenate([z, load(0, ch + hl)], axis=0))

    def step(i, carry):
        start = pl.multiple_of(i * ch, ch)
        work(start, load(pl.multiple_of(start - hl, hl), ch + 2 * hl))
        return carry

    lax.fori_loop(1, n - 1, step, 0)
    work(L - ch, jnp.concatenate([load(L - ch - hl, ch + hl), z], axis=0))


def _rows_at(xh, o):
    return xh if o == 0 else pltpu.roll(xh, (-o) % xh.shape[0], 0)


def conv_bwd(name, xbc_raw, dparts, conv_w, conv_b, col0, width, scaled=None, wc=LANES):
    Bn, L, _ = xbc_raw.shape
    n_d = len(dparts)
    has_s = scaled is not None
    mid = slice(CONV_HALO, CONV_HALO + CONV_ROWS)
    c0 = col0 // wc

    def kern(*refs):
        x_ref, d_refs = refs[0], refs[1:1 + n_d]
        pos = 1 + n_d
        if has_s:
            s_ref, pos = refs[pos], pos + 1
        w_ref, b_ref = refs[pos], refs[pos + 1]
        pos += 2
        if has_s:
            scale = refs[pos][...]
            pos += 1
        dx_ref, dw_ref, db_ref = refs[pos:pos + 3]
        acc = refs[pos + 3]
        w, b = w_ref[...], b_ref[...]
        acc[...] = jnp.zeros(acc.shape, F32)

        def load(s, n):
            dy = d_refs[0][0, pl.ds(s, n), :]
            for r in d_refs[1:]:
                dy = dy + r[0, pl.ds(s, n), :]
            if has_s:
                dy = dy + s_ref[0, pl.ds(s, n), :] * scale
            return jnp.concatenate([x_ref[0, pl.ds(s, n), :], dy], axis=1)

        def work(start, both):
            xh, dyh = both[:, 0:wc], both[:, wc:]
            taps = [_rows_at(xh, k - 2) for k in range(4)]
            pre = b
            for k in range(4):
                pre = pre + taps[k] * w[k:k + 1]
            dpre = dyh * _dsilu(pre)
            dx = dpre * w[2:3]
            for k in (0, 1, 3):
                dx = dx + _rows_at(dpre, 2 - k) * w[k:k + 1]
            dx_ref[0, pl.ds(start, CONV_ROWS), :] = dx[mid].astype(dx_ref.dtype)
            dm = dpre[mid]
            acc[...] += _stack_rows([_colsum(dm * taps[k][mid]) for k in range(4)] + [_colsum(dm)] + [jnp.zeros((1, wc), F32)] * 3)

        _halo_chunks(L, load, work)
        first = pl.program_id(1) == 0

        @pl.when(first)
        def _():
            dw_ref[...] = acc[0:4]
            db_ref[...] = acc[4:5]

        @pl.when(jnp.logical_not(first))
        def _():
            dw_ref[...] += acc[0:4]
            db_ref[...] += acc[4:5]

    slab = lambda off: pl.BlockSpec((1, L, wc), lambda j, b, off=off: (b, 0, off + j))
    in_specs = [slab(c0)] + [slab(0)] * n_d + ([slab(0)] if has_s else [])
    in_specs += [pl.BlockSpec((4, wc), lambda j, b: (0, c0 + j)), pl.BlockSpec((1, wc), lambda j, b: (0, c0 + j))]
    args = [xbc_raw, *dparts] + ([scaled[0]] if has_s else []) + [conv_w, conv_b]
    if has_s:
        in_specs.append(pl.BlockSpec((1, wc), lambda j, b: (0, j)))
        args.append(scaled[1])
    return pl.pallas_call(
        kern, name=name,
        out_shape=[jax.ShapeDtypeStruct((Bn, L, width), BF16), jax.ShapeDtypeStruct((4, width), F32), jax.ShapeDtypeStruct((1, width), F32)],
        grid=(width // wc, Bn), in_specs=in_specs,
        out_specs=[pl.BlockSpec((1, L, wc), lambda j, b: (b, 0, j)), pl.BlockSpec((4, wc), lambda j, b: (0, j)),
                   pl.BlockSpec((1, wc), lambda j, b: (0, j))],
        scratch_shapes=[pltpu.VMEM((8, wc), F32)],
        compiler_params=_cp(("arbitrary", "arbitrary")))(*args)


def _box_mean(x, k, step, pos, n, L, transpose):
    lo, hi = k // 2, k - 1 - k // 2
    cnt = (jnp.minimum(pos + hi + 1, n) - jnp.maximum(pos - lo, 0)).astype(F32)
    if transpose:
        x = x / cnt
        lo, hi = hi, lo
    acc = x
    for o in range(-lo, hi + 1):
        if o == 0:
            continue
        rolled = pltpu.roll(x, (-o * step) % L, 0)
        acc = acc + jnp.where((pos + o >= 0) & (pos + o < n), rolled, 0.0)
    return acc if transpose else acc / cnt


def pool_diff(name, v, col0, gi, transpose):
    L = v.shape[1]
    rows = L // GRID_W
    k = POOL_WINDOWS[gi]

    def body(x):
        tok = lax.broadcasted_iota(jnp.int32, x.shape, 0)
        col = tok & (GRID_W - 1)
        row = tok >> 6
        if not transpose:
            m = _box_mean(x, k, GRID_W, row, rows, L, False)
            m = _box_mean(m, k, 1, col, GRID_W, L, False)
        else:
            m = _box_mean(x, k, 1, col, GRID_W, L, True)
            m = _box_mean(m, k, GRID_W, row, rows, L, True)
        return m - x

    return slab_call(name, body, [(v, col0)], [], [(POOL_GROUP, BF16)], [])[0]


def pool_mix_fwd(name, dgs, z_pool, pool_w, pool_scale):
    def body(d0, d1, d2, d3, z, w, scale):
        q = jnp.concatenate([_dot(d, w[g * POOL_GROUP:(g + 1) * POOL_GROUP]) for g, d in enumerate((d0, d1, d2, d3))], axis=1)
        return q * scale * _silu(z)

    return tok_call(name, body, list(dgs) + [z_pool], [], [pool_w, pool_scale], [(D, BF16)], [], [])[0]


def pool_mix_bwd(name, dgs, z_pool, dyp, pool_w, pool_scale):
    def body(d0, d1, d2, d3, z, dyp, w, scale):
        ds = (d0, d1, d2, d3)
        q = jnp.concatenate([_dot(d, w[g * POOL_GROUP:(g + 1) * POOL_GROUP]) for g, d in enumerate(ds)], axis=1)
        dypm = dyp * _silu(z)
        dz = dyp * (q * scale) * _dsilu(z)
        dq = (dypm * scale).astype(BF16)
        dds, gws = [], []
        for g, d in enumerate(ds):
            dqg = dq[:, g * POOL_GROUP:(g + 1) * POOL_GROUP]
            dds.append(_dot_nt(dqg, w[g * POOL_GROUP:(g + 1) * POOL_GROUP]))
            gws.append(_dot_tn(d, dqg))
        return (*dds, dz, jnp.concatenate(gws, axis=0), _colsum(dypm * q))

    return tok_call(name, body, list(dgs) + [z_pool, dyp], [], [pool_w, pool_scale],
                    [(POOL_GROUP, F32)] * 4 + [(D, BF16)], [], [(D, POOL_GROUP), (1, D)])


def _cumsum_lanes(a, reverse):
    n = a.shape[1]
    k = lax.broadcasted_iota(jnp.int32, (n, n), 0)
    i = lax.broadcasted_iota(jnp.int32, (n, n), 1)
    tri = jnp.where((k >= i) if reverse else (k <= i), 1.0, 0.0).astype(BF16)
    return _dot_exact01(a, tri)


def _rows_to_cols(rows):
    r = rows.shape[0]
    if r < LANES:
        rows = jnp.concatenate([rows, jnp.zeros((LANES - r, rows.shape[1]), F32)], axis=0)
    return rows.T


def _cols_to_rows(cols):
    q = cols[0].shape[0]
    lane = lax.broadcasted_iota(jnp.int32, (q, LANES), 1)
    acc = jnp.zeros((q, LANES), F32)
    for r, c in enumerate(cols):
        acc = acc + jnp.where(lane == r, c, 0.0)
    return acc.T[0:len(cols)]


def _ssd_scalars(dtraw, bias, alog, reverse):
    dt = _softplus(dtraw + bias)
    A = -jnp.exp(alog)
    cs = _cumsum_lanes(dt * A, reverse)
    total = cs[:, 0:1] if reverse else cs[:, CHUNK - 1:CHUNK]
    return dt, A, cs, total


def _tri_mask(transposed, reverse):
    sub = lax.broadcasted_iota(jnp.int32, (CHUNK, CHUNK), 0)
    lane = lax.broadcasted_iota(jnp.int32, (CHUNK, CHUNK), 1)
    i, j = (lane, sub) if transposed else (sub, lane)
    return (i <= j) if reverse else (i >= j)


GPS = 4


def ssd_fwd(name, dtT, bias, alog, xbc, h0, direction, with_y, y_add=None):
    Bn, L = xbc.shape[:2]
    nc = L // CHUNK
    reverse = direction == 1
    blk0 = direction * (N_BC // GPS)
    gs = range(GPS)
    has_add = y_add is not None

    def chunk_of(s):
        return (nc - 1 - s) if reverse else s

    def kern(dt_ref, bias_ref, alog_ref, x_ref, b_ref, c_ref, h0_ref, *rest):
        if has_add:
            yp_ref, dsk_ref, rest = rest[0], rest[1], rest[2:]
        if with_y:
            y_ref, hs_ref, hf_ref, h_scr = rest
        else:
            hs_ref, hf_ref, h_scr = rest
        s = pl.program_id(2)

        @pl.when(s == 0)
        def _():
            h_scr[...] = h0_ref[0]

        first = lax.broadcasted_iota(jnp.int32, (1, LANES), 1) < HEAD_DIM
        heads = range(HPG)
        psl = [slice((r // 2) * LANES, (r // 2 + 1) * LANES) for r in heads]
        keep = _tri_mask(False, reverse)
        sc, x_bf, bm, h, h_bf, bt, cm, cb, cs_cols = [], [], [], [], [], [], [], [], []
        for g in gs:
            dt, _, cs, total = _ssd_scalars(dt_ref[0, g * HPG:(g + 1) * HPG], bias_ref[g], alog_ref[g], reverse)
            u = cs - jnp.log(dt)
            sc.append((cs, u, jnp.exp(total - u), jnp.exp(total)))
            x_bf.append(x_ref[0, :, g * GW:(g + 1) * GW].astype(BF16))
            bm.append(b_ref[0, :, g * D_STATE:(g + 1) * D_STATE])
            h.append(h_scr[g])
            h_bf.append(h[g].astype(BF16))
            hs_ref[0, g, 0] = h[g]
            bt.append(bm[g].T)
            if with_y:
                cm.append(c_ref[0, :, g * D_STATE:(g + 1) * D_STATE])
                cb.append(_dot_nt(cm[g].astype(BF16), bm[g].astype(BF16)))
                cs_cols.append(_rows_to_cols(cs))
        lhs = [[] for _ in gs]
        if with_y:
            for g in gs:
                cs, u = sc[g][0], sc[g][1]
                for r in heads:
                    cs_col = jnp.broadcast_to(cs_cols[g][:, r:r + 1], (CHUNK, LANES))
                    wf = cb[g] * jnp.exp(jnp.where(keep, cs_col - u[r:r + 1], -jnp.inf))
                    lhs[g].append(jnp.concatenate([wf.astype(BF16), (cm[g] * jnp.exp(cs_col)).astype(BF16)], axis=1))
        bts = [[(bt[g] * sc[g][2][r:r + 1]).astype(BF16) for r in heads] for g in gs]
        sts = [[_dot(bts[g][r], x_bf[g][:, psl[r]]) for r in heads] for g in gs]
        if with_y:
            ys = [[_dot(lhs[g][r], jnp.concatenate([x_bf[g][:, psl[r]], h_bf[g][:, psl[r]]], axis=0)) for r in heads] for g in gs]
        for g in gs:
            dc = sc[g][3]
            for p in range(HPG // 2):
                if with_y:
                    cols = slice(g * GW + p * LANES, g * GW + (p + 1) * LANES)
                    yv = jnp.where(first, ys[g][2 * p], ys[g][2 * p + 1])
                    if has_add:
                        yv = yv + yp_ref[0, :, cols] + dsk_ref[:, cols] * x_ref[0, :, cols]
                    y_ref[0, :, cols] = yv
                dc_p = jnp.where(first, dc[2 * p:2 * p + 1], dc[2 * p + 1:2 * p + 2])
                h_scr[g, :, psl[2 * p]] = h[g][:, psl[2 * p]] * dc_p + jnp.where(first, sts[g][2 * p], sts[g][2 * p + 1])

        @pl.when(s == nc - 1)
        def _():
            hf_ref[0] = h_scr[...]

    nb = D_INNER // (GPS * D_STATE)
    in_specs = [
        pl.BlockSpec((1, GPS * HPG, CHUNK), lambda b, g, s: (b, blk0 + g, chunk_of(s))),
        pl.BlockSpec((GPS, HPG, 1), lambda b, g, s: (blk0 + g, 0, 0)),
        pl.BlockSpec((GPS, HPG, 1), lambda b, g, s: (blk0 + g, 0, 0)),
        pl.BlockSpec((1, CHUNK, GPS * GW), lambda b, g, s: (b, chunk_of(s), g)),
        pl.BlockSpec((1, CHUNK, GPS * D_STATE), lambda b, g, s: (b, chunk_of(s), nb + g)),
        pl.BlockSpec((1, CHUNK, GPS * D_STATE), lambda b, g, s: (b, chunk_of(s), nb + N_BC // GPS + g)),
        pl.BlockSpec((1, GPS, D_STATE, GW), lambda b, g, s: (b, g, 0, 0)),
    ]
    args = [dtT, bias, alog, xbc, xbc, xbc, h0]
    if has_add:
        in_specs += [pl.BlockSpec((1, CHUNK, GPS * GW), lambda b, g, s: (b, chunk_of(s), g)),
                     pl.BlockSpec((1, GPS * GW), lambda b, g, s: (0, g))]
        args += list(y_add)
    out_shape, out_specs = [], []
    if with_y:
        out_shape.append(jax.ShapeDtypeStruct((Bn, L, D_INNER), F32))
        out_specs.append(pl.BlockSpec((1, CHUNK, GPS * GW), lambda b, g, s: (b, chunk_of(s), g)))
    out_shape += [jax.ShapeDtypeStruct((Bn, N_BC, nc, D_STATE, GW), F32), jax.ShapeDtypeStruct((Bn, N_BC, D_STATE, GW), F32)]
    out_specs += [pl.BlockSpec((1, GPS, 1, D_STATE, GW), lambda b, g, s: (b, g, chunk_of(s), 0, 0)),
                  pl.BlockSpec((1, GPS, D_STATE, GW), lambda b, g, s: (b, g, 0, 0))]
    return pl.pallas_call(
        kern, name=name, out_shape=out_shape, grid=(Bn, N_BC // GPS, nc), in_specs=in_specs, out_specs=out_specs,
        scratch_shapes=[pltpu.VMEM((GPS, D_STATE, GW), F32)],
        compiler_params=_cp(("arbitrary", "arbitrary", "arbitrary")))(*args)


def ssd_bwd(name, dtT, bias, alog, xbc, h_start, dy, dh_final, direction, dx_add=None):
    Bn, L = xbc.shape[:2]
    nc = L // CHUNK
    reverse = direction == 1
    blk0 = direction * (N_BC // GPS)
    has_y = dy is not None
    has_add = dx_add is not None
    assert has_y or not has_add
    last = 0 if reverse else CHUNK - 1
    gs = range(GPS)

    def chunk_of(s):
        return s if reverse else (nc - 1 - s)

    def kern(*refs):
        if has_add:
            dxp_ref, dsk_ref = refs[9], refs[10]
            refs = refs[:9] + refs[11:]
        if has_y:
            (dt_ref, bias_ref, alog_ref, x_ref, b_ref, hs_ref, dhf_ref, c_ref, dy_ref,
             dx_ref, db_ref, ddt_ref, dbias_ref, dalog_ref, dh0_ref, dc_ref, dh_scr) = refs
        else:
            (dt_ref, bias_ref, alog_ref, x_ref, b_ref, hs_ref, dhf_ref,
             dx_ref, db_ref, ddt_ref, dbias_ref, dalog_ref, dh0_ref, dh_scr) = refs
        s = pl.program_id(2)

        @pl.when(s == 0)
        def _():
            dh_scr[...] = dhf_ref[0]
            dbias_ref[...] = jnp.zeros(dbias_ref.shape, F32)
            dalog_ref[...] = jnp.zeros(dalog_ref.shape, F32)

        first = lax.broadcasted_iota(jnp.int32, (1, LANES), 1) < HEAD_DIM
        heads = range(HPG)
        psl = [slice((r // 2) * LANES, (r // 2 + 1) * LANES) for r in heads]
        mine = [first if r % 2 == 0 else jnp.logical_not(first) for r in heads]
        zeros_bf = jnp.zeros((CHUNK, LANES), BF16)
        keep = _tri_mask(True, reverse)
        ctx = []
        for g in gs:
            dtraw = dt_ref[0, g * HPG:(g + 1) * HPG]
            dt, A, cs, total = _ssd_scalars(dtraw, bias_ref[g], alog_ref[g], reverse)
            u = cs - jnp.log(dt)
            c = dict(dtraw=dtraw, dt=dt, A=A, cs=cs, total=total, u=u, dtt=jnp.exp(total - u), dcy=jnp.exp(total),
                     u_cols=_rows_to_cols(u), x_bf=x_ref[0, :, g * GW:(g + 1) * GW].astype(BF16),
                     bm=b_ref[0, :, g * D_STATE:(g + 1) * D_STATE], h=hs_ref[0, g, 0], dh=dh_scr[g])
            c["bt"] = c["bm"].T
            c["dh_bf"] = c["dh"].astype(BF16)
            if has_y:
                c["cm"] = c_ref[0, :, g * D_STATE:(g + 1) * D_STATE]
                c["ct"] = c["cm"].T
                c["e_row"] = jnp.exp(cs)
                c["dy_bf"] = dy_ref[0, :, g * GW:(g + 1) * GW].astype(BF16)
                c["h_bf"] = c["h"].astype(BF16)
                c["cbt"] = _dot_nt(c["bm"].astype(BF16), c["cm"].astype(BF16))
            ctx.append(c)
        for c in ctx:
            c["lhs"], c["et"] = [], []
            for r in heads:
                u_col = jnp.broadcast_to(c["u_cols"][:, r:r + 1], (CHUNK, LANES))
                bs = (c["bm"] * jnp.exp(c["total"][r:r + 1] - u_col)).astype(BF16)
                if has_y:
                    et = jnp.exp(jnp.where(keep, c["cs"][r:r + 1] - u_col, -jnp.inf))
                    c["et"].append(et)
                    c["lhs"].append(jnp.concatenate([(c["cbt"] * et).astype(BF16), bs], axis=1))
                else:
                    c["lhs"].append(bs)
        for c in ctx:
            c["p2raw"] = [_dot_nt(c["dh_bf"][:, psl[r]], jnp.where(mine[r], c["x_bf"][:, psl[r]], zeros_bf)) for r in heads]
            if has_y:
                c["a1"] = [_dot_nt(jnp.concatenate([c["x_bf"][:, psl[r]], c["h_bf"][:, psl[r]]], axis=0),
                                   jnp.where(mine[r], c["dy_bf"][:, psl[r]], zeros_bf)) for r in heads]
                c["news"] = [_dot((c["ct"] * c["e_row"][r:r + 1]).astype(BF16), c["dy_bf"][:, psl[r]]) for r in heads]
                c["dxs"] = [_dot(c["lhs"][r], jnp.concatenate([c["dy_bf"][:, psl[r]], c["dh_bf"][:, psl[r]]], axis=0)) for r in heads]
            else:
                c["dxs"] = [_dot(c["lhs"][r], c["dh_bf"][:, psl[r]]) for r in heads]
        for g, c in enumerate(ctx):
            dbt = jnp.zeros((D_STATE, CHUNK), F32)
            dcbt = jnp.zeros((CHUNK, CHUNK), F32)
            dct = jnp.zeros((D_STATE, CHUNK), F32)
            tots, out_rows, in_rows, in_cols = [], [], [], []
            for r in heads:
                if has_y:
                    pt = c["a1"][r][0:CHUNK] * c["et"][r]
                    dcbt = dcbt + pt
                    mt = pt * c["cbt"]
                    ph = c["a1"][r][CHUNK:] * c["e_row"][r:r + 1]
                    dct = dct + ph
                    out_rows.append(_colsum(mt + c["ct"] * ph))
                    in_cols.append(jnp.sum(mt, axis=1, keepdims=True))
                p2 = c["p2raw"][r] * c["dtt"][r:r + 1]
                dbt = dbt + p2
                t_term = _colsum(c["bt"] * p2)
                in_rows.append(t_term)
                hdh = c["h"][:, psl[r]] * c["dh"][:, psl[r]]
                tot = jnp.sum(t_term, axis=1, keepdims=True) + c["dcy"][r:r + 1] * jnp.sum(jnp.where(mine[r], hdh, 0.0), keepdims=True)
                tots.append(jnp.broadcast_to(tot, (1, CHUNK)))
            for p in range(HPG // 2):
                cols = slice(g * GW + p * LANES, g * GW + (p + 1) * LANES)
                dxv = jnp.where(first, c["dxs"][2 * p], c["dxs"][2 * p + 1])
                if has_add:
                    dxv = dxv + dxp_ref[0, :, cols] + dsk_ref[:, cols] * dy_ref[0, :, cols]
                dx_ref[0, :, cols] = dxv
                new = c["dh"][:, psl[2 * p]] * jnp.where(first, c["dcy"][2 * p:2 * p + 1], c["dcy"][2 * p + 1:2 * p + 2])
                if has_y:
                    new = new + jnp.where(first, c["news"][2 * p], c["news"][2 * p + 1])
                dh_scr[g, :, psl[2 * p]] = new
            db = dbt.T
            if has_y:
                dcbt_bf = dcbt.astype(BF16)
                db = db + _dot(dcbt_bf, c["cm"].astype(BF16))
                dc_ref[0, :, g * D_STATE:(g + 1) * D_STATE] = dct.T + _dot_tn(dcbt_bf, c["bm"].astype(BF16))
            db_ref[0, :, g * D_STATE:(g + 1) * D_STATE] = db
            s_row = _stack_rows(in_rows)
            lane = lax.broadcasted_iota(jnp.int32, (HPG, CHUNK), 1)
            dcs = jnp.where(lane == last, _stack_rows(tots), 0.0)
            if has_y:
                s_row = s_row + _cols_to_rows(in_cols)
                dcs = dcs + _stack_rows(out_rows)
            dcs = dcs - s_row
            da = _cumsum_lanes(dcs, not reverse)
            ddt = da * c["A"] + jnp.where(c["dt"] > 0.0, s_row / c["dt"], 0.0)
            ddtraw = ddt * _sigmoid(c["dtraw"] + bias_ref[g])
            ddt_ref[0, g * HPG:(g + 1) * HPG] = ddtraw
            dbias_ref[0, g] += jnp.sum(ddtraw, axis=1, keepdims=True)
            dalog_ref[0, g] += jnp.sum(da * c["dt"], axis=1, keepdims=True) * c["A"]

        @pl.when(s == nc - 1)
        def _():
            dh0_ref[0] = dh_scr[...]

    nb = D_INNER // (GPS * D_STATE)
    cidx = lambda b, g, s: (b, chunk_of(s), g)
    hidx = lambda b, g, s: (b, g, 0, 0)
    in_specs = [
        pl.BlockSpec((1, GPS * HPG, CHUNK), lambda b, g, s: (b, blk0 + g, chunk_of(s))),
        pl.BlockSpec((GPS, HPG, 1), lambda b, g, s: (blk0 + g, 0, 0)),
        pl.BlockSpec((GPS, HPG, 1), lambda b, g, s: (blk0 + g, 0, 0)),
        pl.BlockSpec((1, CHUNK, GPS * GW), cidx),
        pl.BlockSpec((1, CHUNK, GPS * D_STATE), lambda b, g, s: (b, chunk_of(s), nb + g)),
        pl.BlockSpec((1, GPS, 1, D_STATE, GW), lambda b, g, s: (b, g, chunk_of(s), 0, 0)),
        pl.BlockSpec((1, GPS, D_STATE, GW), hidx),
    ]
    args = [dtT, bias, alog, xbc, xbc, h_start, dh_final]
    if has_y:
        in_specs += [pl.BlockSpec((1, CHUNK, GPS * D_STATE), lambda b, g, s: (b, chunk_of(s), nb + N_BC // GPS + g)),
                     pl.BlockSpec((1, CHUNK, GPS * GW), cidx)]
        args += [xbc, dy]
    if has_add:
        in_specs += [pl.BlockSpec((1, CHUNK, GPS * GW), cidx), pl.BlockSpec((1, GPS * GW), lambda b, g, s: (0, g))]
        args += list(dx_add)
    out_shape = [jax.ShapeDtypeStruct((Bn, L, D_INNER), F32), jax.ShapeDtypeStruct((Bn, L, N_BC * D_STATE), F32),
                 jax.ShapeDtypeStruct((Bn, N_HEADS, L), F32), jax.ShapeDtypeStruct((Bn, N_BC, HPG, 1), F32),
                 jax.ShapeDtypeStruct((Bn, N_BC, HPG, 1), F32), jax.ShapeDtypeStruct((Bn, N_BC, D_STATE, GW), F32)]
    out_specs = [pl.BlockSpec((1, CHUNK, GPS * GW), cidx), pl.BlockSpec((1, CHUNK, GPS * D_STATE), cidx),
                 pl.BlockSpec((1, GPS * HPG, CHUNK), lambda b, g, s: (b, g, chunk_of(s))),
                 pl.BlockSpec((1, GPS, HPG, 1), hidx), pl.BlockSpec((1, GPS, HPG, 1), hidx), pl.BlockSpec((1, GPS, D_STATE, GW), hidx)]
    if has_y:
        out_shape.append(jax.ShapeDtypeStruct((Bn, L, N_BC * D_STATE), F32))
        out_specs.append(pl.BlockSpec((1, CHUNK, GPS * D_STATE), cidx))
    res = pl.pallas_call(
        kern, name=name, out_shape=out_shape, grid=(Bn, N_BC // GPS, nc), in_specs=in_specs, out_specs=out_specs,
        scratch_shapes=[pltpu.VMEM((GPS, D_STATE, GW), F32)],
        compiler_params=_cp(("arbitrary", "arbitrary", "arbitrary")))(*args)
    dxs, db, ddt, dbias, dalog, dh0 = res[:6]
    return dxs, db, (res[6] if has_y else None), ddt, dbias, dalog, dh0


def _group_mean(v):
    gw = D_INNER // N_BC
    parts = [jnp.broadcast_to(jnp.mean(v[:, g * gw:(g + 1) * gw], axis=-1, keepdims=True), (v.shape[0], gw)) for g in range(N_BC)]
    return jnp.concatenate(parts, axis=1)


def gated_norm_fwd(name, y, z, w_norm):
    def body(y, z, w):
        u = y * _silu(z)
        r = lax.rsqrt(_group_mean(u * u) + NORM_EPS)
        return u * r * w

    return tok_call(name, body, [y, z], [], [w_norm], [(D_INNER, BF16)], [], [])[0]


def gated_norm_bwd(name, y, xs_src, z, d_out, w_norm, head_sel):
    def body(y, xs, z, do, w, sel):
        sz = _silu(z)
        u = y * sz
        r = lax.rsqrt(_group_mean(u * u) + NORM_EPS)
        duh = do * w
        du = r * (duh - u * (r * r) * _group_mean(duh * u))
        dy = du * sz
        dz = du * y * _dsilu(z)
        dsk_heads = _dot_exact01(jnp.broadcast_to(_colsum(dy * xs), (8, D_INNER)), sel)
        return dy, dz, _colsum(do * u * r), dsk_heads

    return tok_call(name, body, [y, xs_src, z, d_out], [], [w_norm, head_sel],
                    [(D_INNER, F32), (D_INNER, BF16)], [], [(1, D_INNER), (8, LANES)], tm=128)


def merge_fwd(name, y_pool, y_ssd, gatepre, x, target, gate, b_merge, norm_post, w_pp, w_ps, w_out):
    def body(yp, ys, gp, x, tgt, gate, bm, wpost, w_pp, w_ps, w_out):
        p1 = _dot(yp, w_pp)
        p2 = _dot(ys, w_ps)
        gates = _sigmoid(gp + bm)
        merged = gates[:, :D] * p1 + gates[:, D:] * p2
        out = _dot(merged.astype(BF16), w_out)
        r = _rms_r(out)
        outr = out * r
        nq = outr * wpost
        err = x + gate * nq - tgt
        loss = 0.5 * jnp.sum(jnp.mean(err * err, axis=-1, keepdims=True), keepdims=True).reshape(1, 1)
        g = err * (1.0 / D)
        dnq = g * gate
        dout = _rms_bwd(dnq * wpost, out, r)
        return merged, p1, p2, dout, g, _colsum(g * nq), _colsum(dnq * outr), jnp.broadcast_to(loss, (1, LANES))

    return tok_call(name, body, [y_pool, y_ssd, gatepre, x, target], [gate], [b_merge, norm_post, w_pp, w_ps, w_out],
                    [(D, BF16), (D, F32), (D, F32), (D, BF16), (D, F32)], [D], [(1, D), (1, LANES)])


def merge_bwd(name, dout, gatepre, p1, p2, b_merge, w_pp, w_ps, w_out):
    def body(dout, gp, p1, p2, bm, w_pp, w_ps, w_out):
        dmerged = _dot_nt(dout, w_out)
        gates = _sigmoid(gp + bm)
        g1, g2 = gates[:, :D], gates[:, D:]
        dp1 = (dmerged * g1).astype(BF16)
        dp2 = (dmerged * g2).astype(BF16)
        dgp = jnp.concatenate([dmerged * p1 * g1 * (1.0 - g1), dmerged * p2 * g2 * (1.0 - g2)], axis=1)
        return dp1, dp2, dgp, _dot_nt(dp1, w_pp), _dot_nt(dp2, w_ps), _colsum(dgp)

    return tok_call(name, body, [dout, gatepre, p1, p2], [], [b_merge, w_pp, w_ps, w_out],
                    [(D, BF16), (D, BF16), (2 * D, BF16), (D, F32), (D_INNER, F32)], [], [(1, 2 * D)])


def _adamw_math(w, g, m, v):
    m = ADAM_B1 * m + (1.0 - ADAM_B1) * g
    v = ADAM_B2 * v + (1.0 - ADAM_B2) * (g * g)
    m_hat = m / (1.0 - ADAM_B1 ** ADAM_STEP)
    v_hat = v / (1.0 - ADAM_B2 ** ADAM_STEP)
    delta = -ADAM_LR * (m_hat / (jnp.sqrt(v_hat) + ADAM_EPS) + ADAM_WD * w)
    return delta, m, v


def adamw(name, w, g, m, v, tr=256):
    R, C = w.shape
    tr = min(tr, R)
    assert R % tr == 0

    def body(w_ref, g_ref, m_ref, v_ref, d_ref, nm_ref, nv_ref):
        d, nm, nv = _adamw_math(w_ref[...], g_ref[...], m_ref[...], v_ref[...])
        d_ref[...] = d
        nm_ref[...] = nm
        nv_ref[...] = nv

    spec = pl.BlockSpec((tr, C), lambda i: (i, 0))
    return pl.pallas_call(
        body, name=name, out_shape=[jax.ShapeDtypeStruct((R, C), F32)] * 3, grid=(R // tr,),
        in_specs=[spec] * 4, out_specs=[spec] * 3, compiler_params=_cp(("parallel",)))(w, g, m, v)


def all_gather_small(name, v):
    R, C = v.shape

    def body(v_ref, out_ref, send_sems, recv_sems, local_sem):
        x, y, c = _me()
        me = 4 * x + 2 * y + c
        mine = pltpu.make_async_copy(v_ref, out_ref.at[me], local_sem)
        mine.start()
        copies = []
        for d in range(1, N_DEV):
            dx, dy, dc = d // 4, (d // 2) % 2, d % 2
            px, py, pc = x ^ dx, y ^ dy, c ^ dc
            copies.append(pltpu.make_async_remote_copy(
                src_ref=v_ref, dst_ref=out_ref.at[me], send_sem=send_sems.at[d - 1], recv_sem=recv_sems.at[d - 1],
                device_id=(px, py, pc), device_id_type=MESH))
        for cp in copies:
            cp.start()
        for d in range(1, N_DEV):
            dx, dy, dc = d // 4, (d // 2) % 2, d % 2
            peer = 4 * (x ^ dx) + 2 * (y ^ dy) + (c ^ dc)
            pltpu.make_async_remote_copy(
                src_ref=v_ref, dst_ref=out_ref.at[peer], send_sem=send_sems.at[d - 1], recv_sem=recv_sems.at[d - 1],
                device_id=(x ^ dx, y ^ dy, c ^ dc), device_id_type=MESH).wait_recv()
        for cp in copies:
            cp.wait_send()
        mine.wait()

    return pl.pallas_call(
        body, name=name, out_shape=jax.ShapeDtypeStruct((N_DEV, R, C), F32),
        in_specs=[pl.BlockSpec(memory_space=pltpu.VMEM)], out_specs=pl.BlockSpec(memory_space=pltpu.VMEM),
        scratch_shapes=[pltpu.SemaphoreType.DMA((N_DEV - 1,)), pltpu.SemaphoreType.DMA((N_DEV - 1,)), pltpu.SemaphoreType.DMA],
        compiler_params=pltpu.CompilerParams(vmem_limit_bytes=VMEM_LIMIT))(v)


def all_gather_chips(name, shard):
    R, C = shard.shape
    half = R // 2
    assert R % 32 == 0

    def body(s_ref, out_ref, send_sems, recv_sems):
        x, y, c = _me()
        chips = [(1 - x, y), (x, 1 - y), (1 - x, 1 - y)]

        def rows(chip, hc):
            return out_ref.at[2 * chip[0] + chip[1], pl.ds(hc * half, half), :]

        first = [pltpu.make_async_remote_copy(
            src_ref=s_ref.at[pl.ds(c * half, half), :], dst_ref=rows((x, y), c), send_sem=send_sems.at[j],
            recv_sem=recv_sems.at[j], device_id=(*chip, c), device_id_type=MESH) for j, chip in enumerate(chips)]
        for cp in first:
            cp.start()
        passed = [pltpu.make_async_remote_copy(
            src_ref=rows(chip, c), dst_ref=rows(chip, c), send_sem=send_sems.at[3 + j], recv_sem=recv_sems.at[3 + j],
            device_id=(x, y, 1 - c), device_id_type=MESH) for j, chip in enumerate(chips)]
        for j, chip in enumerate(chips):
            pltpu.make_async_remote_copy(
                src_ref=rows(chip, c), dst_ref=rows(chip, c), send_sem=send_sems.at[j], recv_sem=recv_sems.at[j],
                device_id=(*chip, c), device_id_type=MESH).wait_recv()
            passed[j].start()
        for j, chip in enumerate(chips):
            pltpu.make_async_remote_copy(
                src_ref=rows(chip, 1 - c), dst_ref=rows(chip, 1 - c), send_sem=send_sems.at[3 + j], recv_sem=recv_sems.at[3 + j],
                device_id=(x, y, 1 - c), device_id_type=MESH).wait_recv()
        for cp in first + passed:
            cp.wait_send()

    out = pl.pallas_call(
        body, name=name, out_shape=jax.ShapeDtypeStruct((N_CHIPS, R, C), shard.dtype),
        in_specs=[pl.BlockSpec(memory_space=pl.ANY)], out_specs=pl.BlockSpec(memory_space=pl.ANY),
        scratch_shapes=[pltpu.SemaphoreType.DMA((6,)), pltpu.SemaphoreType.DMA((6,))],
        compiler_params=pltpu.CompilerParams(vmem_limit_bytes=VMEM_LIMIT))(shard)
    chip = 2 * lax.axis_index("x") + lax.axis_index("y")
    return lax.dynamic_update_index_in_dim(out, shard, chip, 0)


def sibling_swap(name, v):
    def body(v_ref, out_ref, send_sem, recv_sem):
        x, y, c = _me()
        cp = pltpu.make_async_remote_copy(src_ref=v_ref, dst_ref=out_ref, send_sem=send_sem, recv_sem=recv_sem,
                                          device_id=(x, y, 1 - c), device_id_type=MESH)
        cp.start()
        cp.wait()

    return pl.pallas_call(
        body, name=name, out_shape=jax.ShapeDtypeStruct(v.shape, v.dtype),
        in_specs=[pl.BlockSpec(memory_space=pl.ANY)], out_specs=pl.BlockSpec(memory_space=pl.ANY),
        scratch_shapes=[pltpu.SemaphoreType.DMA, pltpu.SemaphoreType.DMA],
        compiler_params=pltpu.CompilerParams(vmem_limit_bytes=VMEM_LIMIT))(v)


def _row_tile(rows, cap, mult=8):
    best = None
    for t in range(mult, min(rows, cap) + 1, mult):
        if rows % t == 0:
            best = t
    assert best is not None, rows
    return best


def add_arrays(name, arrs, out_dtype=F32):
    shape = arrs[0].shape
    C = shape[-1]
    flat = [a.reshape(-1, C) for a in arrs]
    R = flat[0].shape[0]
    narrow = out_dtype == BF16 or any(a.dtype == BF16 for a in arrs)
    tr = _row_tile(R, 2048 if len(arrs) <= 2 else 1024, 16 if narrow else 8)
    n = len(flat)

    def body(*refs):
        acc = refs[0][...].astype(F32)
        for r in refs[1:n]:
            acc = acc + r[...].astype(F32)
        refs[n][...] = acc.astype(out_dtype)

    spec = pl.BlockSpec((tr, C), lambda i: (i, 0))
    out = pl.pallas_call(
        body, name=name, out_shape=jax.ShapeDtypeStruct((R, C), out_dtype), grid=(R // tr,),
        in_specs=[spec] * n, out_specs=spec, compiler_params=_cp(("parallel",)))(*flat)
    return out.reshape(shape)


def reduce_scatter_chips(slabs):
    _, R, C = slabs.shape
    half = R // 2
    c = lax.axis_index("c")
    halves = slabs.reshape(N_CHIPS, 2, half, C)
    own = lax.dynamic_index_in_dim(halves, c, axis=1, keepdims=False)
    other = lax.dynamic_index_in_dim(halves, 1 - c, axis=1, keepdims=False)
    from_sibling = sibling_swap("rs_sibling_halves", other.astype(BF16))
    return add_arrays("rs_add_sibling", [own, from_sibling], out_dtype=BF16)


def reduce_scatter_finish(landed):
    c = lax.axis_index("c")
    mine = add_arrays("rs_add_chips", [landed[j] for j in range(N_CHIPS)])
    sib = sibling_swap("rs_sibling_result", mine)
    return jnp.concatenate([jnp.where(c == 0, mine, sib), jnp.where(c == 0, sib, mine)], axis=0)


def ada_mod_shard(cond_all, w_ada_shard, b_ada_shard):
    def body(c_ref, w_ref, b_ref, o_ref):
        o_ref[...] = _dot(_silu(c_ref[...]).astype(BF16), w_ref[...].astype(BF16)) + b_ref[...]

    return pl.pallas_call(body, name="ada_mod_shard", out_shape=jax.ShapeDtypeStruct((cond_all.shape[0], w_ada_shard.shape[1]), F32),
                          compiler_params=_cp())(cond_all, w_ada_shard, b_ada_shard)


def ada_bwd_shard(cond_all, dmod_all_shard, dmod_all, w_ada_shard, row_is_cctx):
    def body(c_ref, ds_ref, da_ref, w_ref, sel_ref, gw_ref, gb_ref, part_ref):
        sc = _silu(c_ref[...]).astype(BF16)
        gw_ref[...] = _dot_tn(sc, ds_ref[...].astype(BF16))
        gb_ref[...] = _colsum(da_ref[...])
        dc_tot = jnp.broadcast_to(_colsum(ds_ref[...] * sel_ref[...]), (8, ds_ref.shape[1]))
        part_ref[...] = _dot_nt(dc_tot.astype(BF16), w_ref[...].astype(BF16))

    return pl.pallas_call(
        body, name="ada_bwd_shard",
        out_shape=[jax.ShapeDtypeStruct(w_ada_shard.shape, F32), jax.ShapeDtypeStruct((1, dmod_all.shape[1]), F32),
                   jax.ShapeDtypeStruct((8, D), F32)],
        compiler_params=_cp())(cond_all, dmod_all_shard, dmod_all, w_ada_shard, row_is_cctx)


def sum_devices(name, gathered):
    def body(g_ref, o_ref):
        acc = g_ref[0]
        for d in range(1, N_DEV):
            acc = acc + g_ref[d]
        o_ref[...] = acc

    return pl.pallas_call(body, name=name, out_shape=jax.ShapeDtypeStruct(gathered.shape[1:], F32), compiler_params=_cp())(gathered)


def cctx_finish(gathered, c_ctx_row):
    def body(g_ref, c_ref, o_ref):
        acc = g_ref[0, 0:1, :]
        for k in range(1, N_CHIPS):
            acc = acc + g_ref[2 * k, 0:1, :]
        o_ref[...] = acc * _dsilu(c_ref[...])

    return pl.pallas_call(body, name="cctx_finish", out_shape=jax.ShapeDtypeStruct((1, D), F32), compiler_params=_cp())(gathered, c_ctx_row)


def _pack(parts, rows):
    flat = []
    for p in parts:
        p = p.reshape(-1)
        pad = (-p.shape[0]) % LANES
        flat.append(jnp.pad(p, (0, pad)) if pad else p)
    v = jnp.concatenate(flat)
    return jnp.pad(v, (0, rows * LANES - v.shape[0])).reshape(rows, LANES)


def _unpack(v, sizes):
    flat = v.reshape(-1)
    out, off = [], 0
    for n in sizes:
        out.append(flat[off:off + n])
        off += n + (-n) % LANES
    return out


W_SHARD_ROWS = 3456
SEG_ROWS = (0, 2320, 2576, 3088, 3344, 3408)


def kernel(x, c, ctx, c_ctx, w_ada, b_ada, norm_pre, norm_post, w_in, b_merge, pool_w, pool_scale, conv_w, conv_b, dt_bias, a_log, d_skip, ssd_norm, w_proj_pool, w_proj_ssd, w_out, loss_target, m_c_ctx, m_w_ada, m_b_ada, m_norm_pre, m_norm_post, m_w_in, m_b_merge, m_pool_w, m_pool_scale, m_conv_w, m_conv_b, m_dt_bias, m_a_log, m_d_skip, m_ssd_norm, m_w_proj_pool, m_w_proj_ssd, m_w_out, v_c_ctx, v_w_ada, v_b_ada, v_norm_pre, v_norm_post, v_w_in, v_b_merge, v_pool_w, v_pool_scale, v_conv_w, v_conv_b, v_dt_bias, v_a_log, v_d_skip, v_ssd_norm, v_w_proj_pool, v_w_proj_ssd, v_w_out):
    Bn, L, _ = x.shape
    Lc = ctx.shape[1]
    T, Tc = Bn * L, Bn * Lc
    assert Bn == 2
    ix, iy, ic = lax.axis_index("x"), lax.axis_index("y"), lax.axis_index("c")
    me = 4 * ix + 2 * iy + ic
    chip = 2 * ix + iy
    ada_cols = w_ada.shape[2]
    cw_cols = conv_w.shape[2]

    cond_own = jnp.pad(c, ((0, 8 - Bn), (0, 0))) + jnp.pad(c_ctx[None, :], ((Bn, 7 - Bn), (0, 0)))
    convw_own = jnp.pad(conv_w[0], ((0, 4), (0, D - cw_cols)))
    g1 = all_gather_small("gather_cond", jnp.concatenate([cond_own, convw_own], axis=0))
    cond_all = g1[:, 0:8].reshape(8 * N_DEV, D)
    conv_w_full = jnp.concatenate([g1[2 * k, 8:12, 0:cw_cols] for k in range(N_CHIPS)], axis=1)
    b_ada_shard = lax.dynamic_slice(b_ada, (0, chip * ada_cols), (1, ada_cols))
    g2 = all_gather_small("gather_mod", ada_mod_shard(cond_all, w_ada[0], b_ada_shard))
    mod_full = jnp.concatenate([g2[2 * k] for k in range(N_CHIPS)], axis=1)
    own = lax.dynamic_slice(mod_full, (8 * me, 0), (8, 3 * D))
    shift, scale, gate = (own[0:Bn, i * D:(i + 1) * D][:, None, :] for i in range(3))
    shift_c, scale_c = (jnp.broadcast_to(own[Bn:Bn + 1, i * D:(i + 1) * D][None], (Bn, 1, D)) for i in range(2))

    w_in_rows = IN_COLS // N_CHIPS
    shard_in = jnp.concatenate([w_in[0].T, jnp.zeros((16, D), F32)], axis=0).astype(BF16)
    shard_rest = jnp.concatenate([w_proj_pool[0], w_proj_ssd[0], w_out[0], pool_w[0].reshape(64, D)], axis=0).astype(BF16)
    w_inT = all_gather_chips("gather_w_in", shard_in)[:, 0:w_in_rows].reshape(IN_COLS, D)
    w_dt = jnp.pad(w_inT[9216:IN_COLS], ((0, LANES - 64), (0, 0)))
    seg_lo = (0, 1024, 2048, 4096, 6144, 8192, 8704)
    seg_hi = (1024, 2048, 4096, 6144, 8192, 8704, 9216)
    w_seg = [w_inT[lo:hi] for lo, hi in zip(seg_lo, seg_hi)] + [w_dt]

    hx = prenorm_fwd("prenorm_x", x, scale, shift, norm_pre)
    hc = prenorm_fwd("prenorm_ctx", ctx, scale_c, shift_c, norm_pre)
    hx2, hc2 = hx.reshape(T, D), hc.reshape(Tc, D)
    v = mm_nt("proj_v", hx2, w_inT[0:1024], F32).reshape(Bn, L, D)
    zp = mm_nt("proj_zpool", hx2, w_inT[1024:2048], F32).reshape(Bn, L, D)
    zs = mm_nt("proj_zssd", hx2, w_inT[2048:4096], F32).reshape(Bn, L, D_INNER)
    gp = mm_nt("proj_gate", hx2, w_inT[4096:6144], F32).reshape(Bn, L, 2 * D)
    xbc_raw, g_rest = mm_nt("proj_xbc", hx2, w_inT[6144:9216], F32, gather=shard_rest)
    xbc_raw = xbc_raw.reshape(Bn, L, CONV_DIM)
    w_pp = g_rest[:, 0:256].reshape(D, D)
    w_ps = g_rest[:, 256:768].reshape(D_INNER, D)
    w_o = g_rest[:, 768:1024].reshape(D, D)
    pool_full = g_rest[:, 1024:1088].reshape(N_CHIPS, 4, 64, POOL_GROUP).transpose(1, 0, 2, 3).reshape(D, POOL_GROUP)
    dt_raw = mm_nt("proj_dt", hx2, w_dt, F32)
    xbc_raw_c = mm_nt("proj_xbc_ctx", hc2, w_inT[6144:9216], F32).reshape(Bn, Lc, CONV_DIM)
    dt_raw_c = mm_nt("proj_dt_ctx", hc2, w_dt, F32)
    dtT = dt_raw[:, :64].reshape(Bn, L, 64).transpose(0, 2, 1)
    dtT_c = dt_raw_c[:, :64].reshape(Bn, Lc, 64).transpose(0, 2, 1)
    bias3 = dt_bias.reshape(2 * N_BC, HPG, 1)
    alog3 = a_log.reshape(2 * N_BC, HPG, 1)

    xbc = conv_fwd("conv_x", xbc_raw, conv_w_full, conv_b)
    xbc_c = conv_fwd("conv_ctx", xbc_raw_c, conv_w_full, conv_b)
    zero_state = jnp.zeros((Bn, N_BC, D_STATE, GW), F32)
    dskip_lanes = jnp.repeat(d_skip[0], HEAD_DIM)[None, :]
    ys, hs_x, hs_c = [], [], []
    for d in range(2):
        hsc, hfc = ssd_fwd(f"ssd_fwd_ctx{d}", dtT_c, bias3, alog3, xbc_c, zero_state, d, False)
        y, hsx, _ = ssd_fwd(f"ssd_fwd_x{d}", dtT, bias3, alog3, xbc, hfc, d, True,
                            y_add=(ys[0], dskip_lanes) if d == 1 else None)
        ys.append(y)
        hs_x.append(hsx)
        hs_c.append(hsc)

    dgs = [pool_diff(f"pool_diff{g}", v, g * POOL_GROUP, g, False) for g in range(4)]
    y_pool = pool_mix_fwd("pool_mix", dgs, zp, pool_full, pool_scale)
    y_ssd = gated_norm_fwd("gated_norm", ys[1], zs, ssd_norm)
    merged, p1, p2, dout, g_res, dgate, g_norm_post, loss_part = merge_fwd(
        "merge_fwd", y_pool, y_ssd, gp, x, loss_target, gate, b_merge, norm_post, w_pp, w_ps, w_o)

    dp1, dp2, dgp, dyp, dys, g_b_merge = merge_bwd("merge_bwd", dout, gp, p1, p2, b_merge, w_pp, w_ps, w_o)
    gw_o = mm_tn("gw_out", merged.reshape(T, D), dout.reshape(T, D))
    gw_pp = mm_tn("gw_proj_pool", y_pool.reshape(T, D), dp1.reshape(T, D))
    gw_ps = mm_tn("gw_proj_ssd", y_ssd.reshape(T, D_INNER), dp2.reshape(T, D))

    *dds, dzp, g_pool, g_pool_scale = pool_mix_bwd("pool_mix_bwd", dgs, zp, dyp, pool_full, pool_scale)
    dvs = [pool_diff(f"pool_diff_t{g}", dds[g], 0, g, True) for g in range(4)]

    head_sel = (jnp.arange(D_INNER)[:, None] // HEAD_DIM == jnp.arange(LANES)[None, :]).astype(BF16)
    dy, dzs, g_ssd_norm, g_dskip = gated_norm_bwd(
        "gated_norm_bwd", ys[1], (xbc, D_INNER), zs, dys, ssd_norm, head_sel)

    dxs, dbm, dcm, ddt, dxs_c, dbm_c, ddt_c = [], [], [], [], [], [], []
    g_bias = jnp.zeros((2, N_BC, HPG, 1), F32)
    g_alog = jnp.zeros((2, N_BC, HPG, 1), F32)
    for d in range(2):
        a, b_, c_, t_, gb, ga, dh0 = ssd_bwd(f"ssd_bwd_x{d}", dtT, bias3, alog3, xbc, hs_x[d], dy, zero_state, d,
                                             dx_add=(dxs[0], dskip_lanes) if d == 1 else None)
        dxs.append(a), dbm.append(b_), dcm.append(c_), ddt.append(t_)
        ac, bc, _, tc, gbc, gac, _ = ssd_bwd(f"ssd_bwd_ctx{d}", dtT_c, bias3, alog3, xbc_c, hs_c[d], None, dh0, d)
        dxs_c.append(ac), dbm_c.append(bc), ddt_c.append(tc)
        g_bias = g_bias.at[d].set(jnp.sum(gb, axis=0) + jnp.sum(gbc, axis=0))
        g_alog = g_alog.at[d].set(jnp.sum(ga, axis=0) + jnp.sum(gac, axis=0))

    dxr_xs, gcw_xs, gcb_xs = conv_bwd("conv_bwd_xs", xbc_raw, [dxs[1]], conv_w_full, conv_b, 0, D_INNER)
    dxr_b, gcw_b, gcb_b = conv_bwd("conv_bwd_b", xbc_raw, dbm, conv_w_full, conv_b, D_INNER, N_BC * D_STATE)
    dxr_c, gcw_c, gcb_c = conv_bwd("conv_bwd_c", xbc_raw, dcm, conv_w_full, conv_b, D_INNER + N_BC * D_STATE, N_BC * D_STATE)
    dxr_xs_c, gcw_xs_c, gcb_xs_c = conv_bwd("conv_bwd_xs_ctx", xbc_raw_c, dxs_c, conv_w_full, conv_b, 0, D_INNER)
    dxr_b_c, gcw_b_c, gcb_b_c = conv_bwd("conv_bwd_b_ctx", xbc_raw_c, dbm_c, conv_w_full, conv_b, D_INNER, N_BC * D_STATE)
    g_conv_w = jnp.concatenate([gcw_xs + gcw_xs_c, gcw_b + gcw_b_c, gcw_c], axis=1)
    g_conv_b = jnp.concatenate([gcb_xs + gcb_xs_c, gcb_b + gcb_b_c, gcb_c], axis=1)

    def dt_cols(parts, n_tok):
        t = jnp.concatenate(parts, axis=1).transpose(0, 2, 1).reshape(n_tok, 2 * N_HEADS)
        return jnp.pad(t, ((0, 0), (0, LANES - 2 * N_HEADS))).astype(BF16)

    ddt2, ddt2_c = dt_cols(ddt, T), dt_cols(ddt_c, Tc)
    segs = [jnp.concatenate(dvs, axis=-1).reshape(T, D), dzp.reshape(T, D), dzs.reshape(T, D_INNER), dgp.reshape(T, 2 * D),
            dxr_xs.reshape(T, D_INNER), dxr_b.reshape(T, N_BC * D_STATE), dxr_c.reshape(T, N_BC * D_STATE), ddt2]
    segs_c = {4: dxr_xs_c.reshape(Tc, D_INNER), 5: dxr_b_c.reshape(Tc, N_BC * D_STATE), 7: ddt2_c}
    gw_rows = []
    for i, seg in enumerate(segs):
        init = mm_tn(f"gw_in_ctx{i}", segs_c[i], hc2) if i in segs_c else None
        gw_rows.append(mm_tn(f"gw_in{i}", seg, hx2, init=init))
    gw_rows[-1] = gw_rows[-1][0:2 * N_HEADS]
    gw_inT = jnp.concatenate(gw_rows, axis=0)

    pool_slab = g_pool.reshape(4, N_CHIPS, 64, POOL_GROUP).transpose(1, 0, 2, 3).reshape(N_CHIPS, 64, D)
    slabs = jnp.concatenate([gw_inT.reshape(N_CHIPS, 2320, D), gw_pp.reshape(N_CHIPS, 256, D), gw_ps.reshape(N_CHIPS, 512, D),
                             gw_o.reshape(N_CHIPS, 256, D), pool_slab, jnp.zeros((N_CHIPS, W_SHARD_ROWS - SEG_ROWS[-1], D), F32)], axis=1)
    chip_part = reduce_scatter_chips(slabs)
    d_hx, landed = mm_nn_multi("d_hx", list(zip(segs, w_seg)), F32, tm=1024, tk=512, exchange=chip_part)
    d_hx = d_hx.reshape(Bn, L, D)
    gsh = reduce_scatter_finish(landed)
    d_hc = mm_nn_multi("d_hc", [(segs_c[i], w_seg[i]) for i in sorted(segs_c)], F32).reshape(Bn, Lc, D)

    grad_x, dscale, dshift, g_npre_x = prenorm_bwd("prenorm_bwd_x", x, d_hx, scale, norm_pre, g_res=g_res)
    _, dscale_c, dshift_c, g_npre_c = prenorm_bwd("prenorm_bwd_ctx", ctx, d_hc, scale_c, norm_pre)

    dmod_x = jnp.concatenate([dshift[:, 0], dscale[:, 0], dgate[:, 0]], axis=1)
    dmod_c = jnp.concatenate([jnp.sum(dshift_c[:, 0], axis=0, keepdims=True), jnp.sum(dscale_c[:, 0], axis=0, keepdims=True),
                              jnp.zeros((1, D), F32)], axis=1)
    dmod_own = jnp.pad(dmod_x, ((0, 8 - Bn), (0, 0))) + jnp.pad(dmod_c, ((Bn, 7 - Bn), (0, 0)))

    small_sizes = (D, D, 2 * D, D, CONV_DIM, 2 * N_HEADS, 2 * N_HEADS, N_HEADS, D_INNER, 4 * CONV_DIM, 1)
    pk = _pack([g_npre_x + g_npre_c, g_norm_post, g_b_merge, g_pool_scale, g_conv_b, g_bias, g_alog, g_dskip[0, 0:N_HEADS],
                g_ssd_norm, g_conv_w, loss_part[0, 0:1]], 184)
    dmod_rows = 8 * 3 * D // LANES
    gathered = all_gather_small("gather_small", jnp.concatenate([dmod_own.reshape(dmod_rows, LANES), pk], axis=0))
    dmod_all = gathered[:, 0:dmod_rows].reshape(8 * N_DEV, 3 * D)
    small = sum_devices("sum_small", gathered[:, dmod_rows:])
    row_is_cctx = (jnp.arange(8 * N_DEV) % 8 == Bn).astype(F32)[:, None]
    g_w_ada, g_b_ada, cpart = ada_bwd_shard(
        cond_all, lax.dynamic_slice(dmod_all, (0, chip * ada_cols), (8 * N_DEV, ada_cols)), dmod_all, w_ada[0], row_is_cctx)
    g_c_ctx = cctx_finish(all_gather_small("gather_cctx", cpart), c_ctx[None, :])
    (g_norm_pre, g_norm_post_t, g_b_merge_t, g_pool_scale_t, g_conv_b_t, g_dt_bias, g_a_log, g_d_skip, g_ssd_norm_t,
     g_conv_w_t, loss) = _unpack(small, small_sizes)
    g_conv_w_shard = lax.dynamic_slice(g_conv_w_t.reshape(4, CONV_DIM), (0, chip * cw_cols), (4, cw_cols))

    g_w_in = gsh[SEG_ROWS[0]:SEG_ROWS[1]].T
    g_w_pp, g_w_ps, g_w_o = (gsh[SEG_ROWS[i]:SEG_ROWS[i + 1]] for i in (1, 2, 3))
    g_pool_w = gsh[SEG_ROWS[4]:SEG_ROWS[5]].reshape(256, POOL_GROUP)

    grads = {
        "c_ctx": g_c_ctx.reshape(c_ctx.shape), "w_ada": g_w_ada[None], "b_ada": g_b_ada, "norm_pre": g_norm_pre[None],
        "norm_post": g_norm_post_t[None], "w_in": g_w_in[None], "b_merge": g_b_merge_t[None],
        "pool_w": g_pool_w.reshape(pool_w.shape), "pool_scale": g_pool_scale_t[None], "conv_w": g_conv_w_shard[None],
        "conv_b": g_conv_b_t[None], "dt_bias": g_dt_bias.reshape(dt_bias.shape), "a_log": g_a_log.reshape(a_log.shape),
        "d_skip": g_d_skip[None], "ssd_norm": g_ssd_norm_t[None], "w_proj_pool": g_w_pp[None], "w_proj_ssd": g_w_ps[None],
        "w_out": g_w_o[None]}
    weights = dict(c_ctx=c_ctx, w_ada=w_ada, b_ada=b_ada, norm_pre=norm_pre, norm_post=norm_post, w_in=w_in, b_merge=b_merge,
                   pool_w=pool_w, pool_scale=pool_scale, conv_w=conv_w, conv_b=conv_b, dt_bias=dt_bias, a_log=a_log,
                   d_skip=d_skip, ssd_norm=ssd_norm, w_proj_pool=w_proj_pool, w_proj_ssd=w_proj_ssd, w_out=w_out)
    m_in = dict(c_ctx=m_c_ctx, w_ada=m_w_ada, b_ada=m_b_ada, norm_pre=m_norm_pre, norm_post=m_norm_post, w_in=m_w_in,
                b_merge=m_b_merge, pool_w=m_pool_w, pool_scale=m_pool_scale, conv_w=m_conv_w, conv_b=m_conv_b,
                dt_bias=m_dt_bias, a_log=m_a_log, d_skip=m_d_skip, ssd_norm=m_ssd_norm, w_proj_pool=m_w_proj_pool,
                w_proj_ssd=m_w_proj_ssd, w_out=m_w_out)
    v_in = dict(c_ctx=v_c_ctx, w_ada=v_w_ada, b_ada=v_b_ada, norm_pre=v_norm_pre, norm_post=v_norm_post, w_in=v_w_in,
                b_merge=v_b_merge, pool_w=v_pool_w, pool_scale=v_pool_scale, conv_w=v_conv_w, conv_b=v_conv_b,
                dt_bias=v_dt_bias, a_log=v_a_log, d_skip=v_d_skip, ssd_norm=v_ssd_norm, w_proj_pool=v_w_proj_pool,
                w_proj_ssd=v_w_proj_ssd, w_out=v_w_out)
    names = list(weights)
    big = ("w_ada", "w_in", "pool_w", "w_proj_pool", "w_proj_ssd", "w_out")
    small_names = [n for n in names if n not in big]
    delta, new_m, new_v = {}, {}, {}
    for n in big:
        shape2 = (-1, weights[n].shape[-1])
        d_, m_, v_ = adamw(f"adamw_{n}", weights[n].reshape(shape2), grads[n].reshape(shape2), m_in[n].reshape(shape2),
                           v_in[n].reshape(shape2), tr=128)
        delta[n], new_m[n], new_v[n] = (t.reshape(weights[n].shape) for t in (d_, m_, v_))
    sizes = [weights[n].size for n in small_names]
    packed = [_pack([src[n] for n in small_names], 144) for src in (weights, grads, m_in, v_in)]
    outs = adamw("adamw_small", *packed, tr=144)
    for res, store in zip(outs, (delta, new_m, new_v)):
        for n, piece in zip(small_names, _unpack(res, sizes)):
            store[n] = piece.reshape(weights[n].shape)

    return (loss.reshape(()), grad_x, *[grads[n] for n in names], *[delta[n] for n in names],
            *[new_m[n] for n in names], *[new_v[n] for n in names])
```
